```python
import math
import jax, jax.numpy as jnp
from jax import lax
import numpy as np

D_MODEL = 2048
BATCH = 2
SEQ = 4096
DEPTH = 1

CHUNK = 64
QBLOCK = 128
HEAD_DIM = 128
N_HEADS_SB = 8
N_HEADS_DSA = 8
IDX_HEADS = 16
IDX_DIM = 64
TOPK_MAX = 256
N_BUCKETS = 32
MAX_DISTANCE = 128
N_MEM = 256
MEM_HEADS = 4
D_FF = 3 * D_MODEL
CONV_WIDTH = 3
EPS = 1e-6

SB_WIDTH = N_HEADS_SB * HEAD_DIM
DSA_WIDTH = N_HEADS_DSA * HEAD_DIM
MEM_WIDTH = MEM_HEADS * HEAD_DIM
IN_WIDTHS = (SB_WIDTH, SB_WIDTH, SB_WIDTH,
             DSA_WIDTH, DSA_WIDTH, DSA_WIDTH,
             IDX_HEADS * IDX_DIM, IDX_DIM, IDX_HEADS,
             2 * D_MODEL)
IN_TOTAL = sum(IN_WIDTHS)

kernel_name = "hybrid_stickbreak_dsa_convffn_layer"


def rmsnorm(x, g):
    xf = x.astype(jnp.float32)
    inv = lax.rsqrt(jnp.mean(xf * xf, axis=-1, keepdims=True) + EPS)
    return (xf * inv * g.astype(jnp.float32)).astype(x.dtype)


def to_blocks(a, nb):
    b, s = a.shape[:2]
    return a.reshape((b, nb, s // nb) + a.shape[2:]).swapaxes(0, 1)


def from_blocks(a):
    nb, b, q = a.shape[:3]
    return a.swapaxes(0, 1).reshape((b, nb * q) + a.shape[3:])


def relative_bucket(rel):
    nb = N_BUCKETS // 2
    max_exact = nb // 2
    ret = jnp.where(rel > 0, nb, 0)
    n = jnp.abs(rel)
    nf = jnp.maximum(n, 1).astype(jnp.float32)
    large = max_exact + (jnp.log(nf / max_exact) / math.log(MAX_DISTANCE / max_exact)
                         * (nb - max_exact)).astype(jnp.int32)
    large = jnp.minimum(large, nb - 1)
    return ret + jnp.where(n < max_exact, n, large)


def stick_breaking_attention(q, k, v):
    b, s, h, dh = q.shape
    nb = s // QBLOCK
    scale = dh ** -0.5
    kf = k.astype(jnp.float32)
    kpos = jnp.arange(s)

    def block(args):
        q_blk, bi = args
        qpos = bi * QBLOCK + jnp.arange(QBLOCK)
        z = jnp.einsum('bqhd,bshd->bhqs', q_blk.astype(jnp.float32), kf) * scale
        before = kpos[None, :] < qpos[:, None]
        log_keep = jnp.where(before, jax.nn.log_sigmoid(-z), 0.0)
        log_between = lax.cumsum(log_keep, axis=3, reverse=True) - log_keep
        weights = jnp.where(before, jnp.exp(jax.nn.log_sigmoid(z) + log_between), 0.0)
        return jnp.einsum('bhqs,bshd->bqhd', weights.astype(v.dtype), v)

    out = lax.map(block, (to_blocks(q, nb), jnp.arange(nb)))
    return from_blocks(out)


def dsa_sparse_attention(q, k, v, q_idx, k_idx, w_idx, rel_bias):
    b, s, h, dh = q.shape
    top = min(TOPK_MAX, s // 4)
    nb = s // QBLOCK
    k_chunk = jnp.arange(s) // CHUNK
    k_idx_f = k_idx.astype(jnp.float32)
    gather = jax.vmap(lambda arr, ix: arr[ix])

    def block(args):
        qb, qib, wb, bi = args
        qpos = bi * QBLOCK + jnp.arange(QBLOCK)
        q_chunk = qpos // CHUNK
        visible = k_chunk[None, :] <= q_chunk[:, None]
        dots = jnp.einsum('bqhi,bsi->bqhs', qib.astype(jnp.float32), k_idx_f) * IDX_DIM ** -0.5
        score = jnp.einsum('bqh,bqhs->bqs', wb.astype(jnp.float32) * IDX_HEADS ** -0.5,
                           jax.nn.relu(dots))
        score = jnp.where(visible[None], score, -jnp.inf)
        _, sel = lax.top_k(score, top)
        sel_ok = (sel // CHUNK) <= q_chunk[None, :, None]
        k_sel = gather(k, sel)
        v_sel = gather(v, sel)
        logits = jnp.einsum('bqhd,bqkhd->bhqk', qb.astype(jnp.float32),
                            k_sel.astype(jnp.float32)) * dh ** -0.5
        bias = rel_bias[relative_bucket(sel - qpos[None, :, None])]
        logits = logits + jnp.transpose(bias, (0, 3, 1, 2)).astype(jnp.float32)
        logits = jnp.where(sel_ok[:, None], logits, -jnp.inf)
        p = jax.nn.softmax(logits, axis=-1)
        return jnp.einsum('bhqk,bqkhd->bqhd', p.astype(v.dtype), v_sel)

    out = lax.map(block, (to_blocks(q, nb), to_blocks(q_idx, nb), to_blocks(w_idx, nb),
                          jnp.arange(nb)))
    return from_blocks(out)


def memory_cross_attention(x, mem, g_cross, g_mem, w_cq, w_ckv, w_co):
    b, s, _ = x.shape
    m = mem.shape[1]
    hq = rmsnorm(x, g_cross)
    hm = rmsnorm(mem, g_mem)
    q = (hq @ w_cq).reshape(b, s, MEM_HEADS, HEAD_DIM)
    km, vm = jnp.split(hm @ w_ckv, 2, axis=-1)
    km = km.reshape(b, m, MEM_HEADS, HEAD_DIM)
    vm = vm.reshape(b, m, MEM_HEADS, HEAD_DIM)
    logits = jnp.einsum('bshd,bmhd->bhsm', q.astype(jnp.float32),
                        km.astype(jnp.float32)) * HEAD_DIM ** -0.5
    p = jax.nn.softmax(logits, axis=-1)
    o = jnp.einsum('bhsm,bmhd->bshd', p.astype(vm.dtype), vm).reshape(b, s, MEM_WIDTH)
    return o @ w_co


def conv_ffn(x, g_ffn, w_up, conv_w, conv_b, w_down):
    s = x.shape[1]
    h = rmsnorm(x, g_ffn)
    u = h @ w_up
    up = jnp.pad(u, ((0, 0), (CONV_WIDTH - 1, 0), (0, 0)))
    c = conv_b
    for i in range(CONV_WIDTH):
        c = c + conv_w[i] * up[:, i:i + s]
    a, val = jnp.split(c, 2, axis=-1)
    return (jax.nn.gelu(a) * val) @ w_down


def hybrid_layer(x, mem, g_mix, w_in, b_gate, w_proj_sb, w_proj_dsa, w_out, rel_bias,
                 g_cross, g_mem, w_cq, w_ckv, w_co, g_ffn, w_up, conv_w, conv_b, w_down):
    b, s, _ = x.shape
    h = rmsnorm(x, g_mix)
    u = h @ w_in
    offsets = []
    acc = 0
    for wdt in IN_WIDTHS[:-1]:
        acc += wdt
        offsets.append(acc)
    (q_sb, k_sb, v_sb, q_ds, k_ds, v_ds, q_ix, k_ix, w_ix, gates) = jnp.split(u, offsets, axis=-1)
    hd = lambda a, n, d: a.reshape(b, s, n, d)
    o_sb = stick_breaking_attention(hd(q_sb, N_HEADS_SB, HEAD_DIM), hd(k_sb, N_HEADS_SB, HEAD_DIM),
                                    hd(v_sb, N_HEADS_SB, HEAD_DIM))
    o_ds = dsa_sparse_attention(hd(q_ds, N_HEADS_DSA, HEAD_DIM), hd(k_ds, N_HEADS_DSA, HEAD_DIM),
                                hd(v_ds, N_HEADS_DSA, HEAD_DIM), hd(q_ix, IDX_HEADS, IDX_DIM),
                                k_ix, w_ix, rel_bias)
    o_sb = o_sb.reshape(b, s, SB_WIDTH) @ w_proj_sb
    o_ds = o_ds.reshape(b, s, DSA_WIDTH) @ w_proj_dsa
    g = jax.nn.sigmoid((gates + b_gate).astype(jnp.float32)).astype(x.dtype)
    g_sb, g_ds = jnp.split(g, 2, axis=-1)
    x = x + (g_sb * o_sb + g_ds * o_ds) @ w_out
    x = x + memory_cross_attention(x, mem, g_cross, g_mem, w_cq, w_ckv, w_co)
    x = x + conv_ffn(x, g_ffn, w_up, conv_w, conv_b, w_down)
    return x


def setup_inputs(seed: int = 0) -> dict:
    key = jax.random.key(seed)
    ks = jax.random.split(key, 24)
    f32 = jnp.float32
    nrm = lambda k, shape, fan_in: jax.random.normal(k, shape, f32) * fan_in ** -0.5
    gain = lambda k: 1.0 + 0.01 * jax.random.normal(k, (DEPTH, D_MODEL), f32)
    return {
        "x": jax.random.normal(ks[0], (BATCH, SEQ, D_MODEL), f32),
        "mem": jax.random.normal(ks[1], (BATCH, N_MEM, D_MODEL), f32),
        "g_mix": gain(ks[2]),
        "w_in": nrm(ks[3], (DEPTH, D_MODEL, IN_TOTAL), D_MODEL),
        "b_gate": 0.01 * jax.random.normal(ks[4], (DEPTH, 2 * D_MODEL), f32),
        "w_proj_sb": nrm(ks[5], (DEPTH, SB_WIDTH, D_MODEL), SB_WIDTH),
        "w_proj_dsa": nrm(ks[6], (DEPTH, DSA_WIDTH, D_MODEL), DSA_WIDTH),
        "w_out": nrm(ks[7], (DEPTH, D_MODEL, D_MODEL), D_MODEL),
        "rel_bias": 0.2 * jax.random.normal(ks[8], (N_BUCKETS, N_HEADS_DSA), f32),
        "g_cross": gain(ks[9]),
        "g_mem": gain(ks[10]),
        "w_cq": nrm(ks[11], (DEPTH, D_MODEL, MEM_WIDTH), D_MODEL),
        "w_ckv": nrm(ks[12], (DEPTH, D_MODEL, 2 * MEM_WIDTH), D_MODEL),
        "w_co": nrm(ks[13], (DEPTH, MEM_WIDTH, D_MODEL), MEM_WIDTH),
        "g_ffn": gain(ks[14]),
        "w_up": nrm(ks[15], (DEPTH, D_MODEL, 2 * D_FF), D_MODEL),
        "conv_w": nrm(ks[16], (DEPTH, CONV_WIDTH, 2 * D_FF), CONV_WIDTH),
        "conv_b": 0.01 * jax.random.normal(ks[17], (DEPTH, 2 * D_FF), f32),
        "w_down": nrm(ks[18], (DEPTH, D_FF, D_MODEL), D_FF),
        "g_final": 1.0 + 0.01 * jax.random.normal(ks[19], (D_MODEL,), f32),
    }


def reference(x, mem, g_mix, w_in, b_gate, w_proj_sb, w_proj_dsa, w_out, rel_bias,
              g_cross, g_mem, w_cq, w_ckv, w_co, g_ffn, w_up, conv_w, conv_b, w_down, g_final):
    for l in range(DEPTH):
        x = hybrid_layer(x, mem, g_mix[l], w_in[l], b_gate[l], w_proj_sb[l], w_proj_dsa[l],
                         w_out[l], rel_bias, g_cross[l], g_mem[l], w_cq[l], w_ckv[l], w_co[l],
                         g_ffn[l], w_up[l], conv_w[l], conv_b[l], w_down[l])
    return rmsnorm(x, g_final)
```

```python
import functools

import jax
import jax.numpy as jnp
from jax import lax
from jax.experimental import pallas as pl
from jax.experimental.pallas import tpu as pltpu

F32, BF16, I32 = jnp.float32, jnp.bfloat16, jnp.int32

EPS = 1e-6
HEAD_DIM = 128
N_HEADS = 8
IDX_HEADS = 16
IDX_DIM = 64
CHUNK = 64
TOPK_MAX = 256
N_BUCKETS = 32
MAX_DISTANCE = 128
MEM_HEADS = 4
CONV_WIDTH = 3

LANES = 128
SUBLANES = 8
VMEM_LIMIT_BYTES = 56 * 1024 * 1024
INT_MIN = -(2 ** 31)
NEG_BIG = -1e30
EXP_UNDERFLOW = -104.0

_NT = (((1,), (1,)), ((), ()))


def _params(*sem):
    return pltpu.CompilerParams(dimension_semantics=sem, vmem_limit_bytes=VMEM_LIMIT_BYTES)


def _rms(x, g):
    inv = lax.rsqrt(jnp.mean(x * x, axis=-1, keepdims=True) + EPS)
    return x * inv * g


def _mm_body(*refs, has_gain, has_resid):
    it = iter(refs)
    a_ref = next(it)
    g_ref = next(it) if has_gain else None
    w_ref = next(it)
    r_ref = next(it) if has_resid else None
    o_ref = next(it)
    h_ref = next(it) if has_gain else None
    if has_gain:
        @pl.when(pl.program_id(1) == 0)
        def _():
            h_ref[...] = _rms(a_ref[...].astype(F32), g_ref[...]).astype(BF16)
        a = h_ref[...]
    else:
        a = a_ref[...]
    acc = jnp.dot(a, w_ref[...], preferred_element_type=F32)
    if has_resid:
        acc = acc + r_ref[...]
    o_ref[...] = acc.astype(o_ref.dtype)


def _matmul(a, w, *, name, gain=None, resid=None, out_dtype, tm, tn):
    m, k = a.shape
    n = w.shape[1]
    tm, tn = min(tm, m), min(tn, n)
    assert m % tm == 0 and n % tn == 0, (m, n, tm, tn)
    in_specs = [pl.BlockSpec((tm, k), lambda i, j: (i, 0))]
    args = [a]
    scratch = []
    if gain is not None:
        in_specs.append(pl.BlockSpec((1, k), lambda i, j: (0, 0)))
        args.append(gain.reshape(1, k).astype(F32))
        scratch.append(pltpu.VMEM((tm, k), BF16))
    in_specs.append(pl.BlockSpec((k, tn), lambda i, j: (0, j)))
    args.append(w)
    if resid is not None:
        in_specs.append(pl.BlockSpec((tm, tn), lambda i, j: (i, j)))
        args.append(resid)
    return pl.pallas_call(
        functools.partial(_mm_body, has_gain=gain is not None, has_resid=resid is not None),
        grid=(m // tm, n // tn),
        in_specs=in_specs,
        out_specs=pl.BlockSpec((tm, tn), lambda i, j: (i, j)),
        out_shape=jax.ShapeDtypeStruct((m, n), out_dtype),
        scratch_shapes=scratch,
        compiler_params=_params("parallel", "arbitrary"),
        name=name,
    )(*args)


def _sb_body(q_ref, k_ref, v_ref, u_ref, o_ref, *, tq):
    i = pl.program_id(2)
    q = (q_ref[...].astype(F32) * HEAD_DIM ** -0.5).astype(BF16)
    tri = u_ref[...]
    row = lax.broadcasted_iota(I32, (tq, tq), 0)
    col = lax.broadcasted_iota(I32, (tq, tq), 1)
    before = col < row

    def tile(j, carry, acc, diagonal):
        start = pl.multiple_of(j * tq, tq)
        k = k_ref[pl.ds(start, tq), :]
        v = v_ref[pl.ds(start, tq), :]
        z = lax.dot_general(q, k, _NT, preferred_element_type=F32)
        log_keep = jnp.minimum(-z, 0.0) - jnp.log1p(jnp.exp(-jnp.abs(z)))
        if diagonal:
            log_keep = jnp.where(before, log_keep, 0.0)
        c = jnp.dot(log_keep.astype(BF16), tri, preferred_element_type=F32) + carry
        w = jnp.exp(z + c)
        if diagonal:
            w = jnp.where(before, w, 0.0)
        acc = acc + jnp.dot(w.astype(BF16), v, preferred_element_type=F32)
        return c[:, 0:1], acc

    carry, acc = tile(i, jnp.zeros((tq, 1), F32), jnp.zeros((tq, HEAD_DIM), F32), True)

    def cond(state):
        j, live, _, _ = state
        return jnp.logical_and(j >= 0, live > EXP_UNDERFLOW)

    def body(state):
        j, _, carry, acc = state
        carry, acc = tile(j, carry, acc, False)
        return j - 1, jnp.max(carry), carry, acc

    _, _, _, acc = lax.while_loop(cond, body, (i - 1, jnp.max(carry), carry, acc))
    o_ref[...] = acc.astype(o_ref.dtype)


def _sb_attention(qkv, batch, seq, *, q_col, k_col, v_col, tq=256):
    nq = seq // tq
    tri = (jnp.arange(tq)[:, None] >= jnp.arange(tq)[None, :]).astype(BF16)
    return pl.pallas_call(
        functools.partial(_sb_body, tq=tq),
        grid=(batch, N_HEADS, nq),
        in_specs=[
            pl.BlockSpec((tq, HEAD_DIM), lambda b, h, i: (b * nq + i, q_col + h)),
            pl.BlockSpec((seq, HEAD_DIM), lambda b, h, i: (b, k_col + h)),
            pl.BlockSpec((seq, HEAD_DIM), lambda b, h, i: (b, v_col + h)),
            pl.BlockSpec((tq, tq), lambda b, h, i: (0, 0)),
        ],
        out_specs=pl.BlockSpec((tq, HEAD_DIM), lambda b, h, i: (b * nq + i, h)),
        out_shape=jax.ShapeDtypeStruct((batch * seq, N_HEADS * HEAD_DIM), BF16),
        compiler_params=_params("parallel", "parallel", "arbitrary"),
        name="sb_attention",
    )(qkv, qkv, qkv, tri)


def _bucket_thresholds():
    nb = N_BUCKETS // 2
    max_exact = nb // 2
    span = nb - max_exact
    out = []
    for k in range(1, span):
        n = max_exact
        while n ** span * max_exact ** k < MAX_DISTANCE ** k * max_exact ** span:
            n += 1
        out.append(n)
    return max_exact, out


def _bias_body(rb_ref, o_ref, *, tq):
    nb = N_BUCKETS // 2
    max_exact, steps = _bucket_thresholds()
    shape = (tq, 2 * tq)
    rel = lax.broadcasted_iota(I32, shape, 1) - lax.broadcasted_iota(I32, shape, 0) - tq
    n = jnp.abs(rel)
    large = jnp.full(shape, max_exact, I32)
    for t in steps:
        large = large + (n >= t).astype(I32)
    bucket = jnp.where(rel > 0, nb, 0) + jnp.where(n < max_exact, n, large)
    for h in range(N_HEADS):
        val = jnp.zeros(shape, F32)
        for b in range(N_BUCKETS):
            val = jnp.where(bucket == b, rb_ref[b, h], val)
        o_ref[h] = val - rb_ref[nb - 1, h]


def _near_bias(rel_bias, tq):
    return pl.pallas_call(
        functools.partial(_bias_body, tq=tq),
        in_specs=[pl.BlockSpec(memory_space=pltpu.SMEM)],
        out_specs=pl.BlockSpec(memory_space=pltpu.VMEM),
        out_shape=jax.ShapeDtypeStruct((N_HEADS, tq, 2 * tq), F32),
        compiler_params=pltpu.CompilerParams(vmem_limit_bytes=VMEM_LIMIT_BYTES),
        name="dsa_near_bias",
    )(rel_bias.astype(F32))


def _key_to_float(key):
    bits = key ^ ((key >> 31) & 0x7FFFFFFF)
    return lax.bitcast_convert_type(bits, F32)


def _dsa_body(qd_ref, qi_ref, wq_ref, kd_ref, vd_ref, ki_ref, bias_ref, o_ref,
              sc_ref, qs_ref, m_ref, l_ref, acc_ref, *, tq, top):
    i = pl.program_id(1)
    shape = (tq, tq)
    row = lax.broadcasted_iota(I32, shape, 0)
    col = lax.broadcasted_iota(I32, shape, 1)

    wq = wq_ref[...] * (IDX_DIM ** -0.5 * IDX_HEADS ** -0.5)

    def score_tile(j, diagonal):
        start = pl.multiple_of(j * tq, tq)
        ki = ki_ref[pl.ds(start, tq), :].astype(BF16)
        s = jnp.zeros(shape, F32)
        for h in range(IDX_HEADS):
            d = lax.dot_general(qi_ref[:, h * LANES:(h + 1) * LANES], ki, _NT,
                                preferred_element_type=F32)
            s = s + wq[:, h:h + 1] * jnp.maximum(d, 0.0)
        if diagonal:
            s = jnp.where(col // CHUNK <= row // CHUNK, s, -jnp.inf)
        sc_ref[j] = s

    def score_step(j, carry):
        score_tile(j, False)
        return carry

    lax.fori_loop(0, i, score_step, 0)
    score_tile(i, True)

    def count_ge(cand):
        def step(j, acc):
            hit = (sc_ref[j] >= cand).astype(I32)
            for g in range(tq // LANES):
                acc = acc + hit[:, g * LANES:(g + 1) * LANES]
            return acc
        acc = lax.fori_loop(0, i + 1, step, jnp.zeros((tq, LANES), I32))
        return jnp.sum(acc, axis=1, keepdims=True)

    def bit_step(b, prefix):
        cand = prefix + lax.shift_left(jnp.int32(1), 31 - b)
        return jnp.where(count_ge(_key_to_float(cand)) >= top, cand, prefix)

    prefix = lax.fori_loop(0, 32, bit_step, jnp.full((tq, 1), INT_MIN, I32))
    thr = jnp.where(prefix == INT_MIN, jnp.finfo(F32).min, _key_to_float(prefix))

    qs_ref[...] = (qd_ref[...].astype(F32) * HEAD_DIM ** -0.5).astype(BF16)
    m_ref[...] = jnp.full(m_ref.shape, NEG_BIG, F32)
    l_ref[...] = jnp.zeros(l_ref.shape, F32)
    acc_ref[...] = jnp.zeros(acc_ref.shape, F32)

    def att_tile(j, near):
        start = pl.multiple_of(j * tq, tq)
        sel = sc_ref[j] >= thr
        for h in range(N_HEADS):
            hs = slice(h * HEAD_DIM, (h + 1) * HEAD_DIM)
            lg = lax.dot_general(qs_ref[:, hs], kd_ref[pl.ds(start, tq), hs], _NT,
                                 preferred_element_type=F32)
            if near is not None:
                lg = lg + bias_ref[h, :, near * tq:(near + 1) * tq]
            lg = jnp.where(sel, lg, NEG_BIG)
            m_old = m_ref[h]
            m_new = jnp.maximum(m_old, jnp.max(lg, axis=1, keepdims=True))
            p = jnp.exp(lg - m_new)
            alpha = jnp.exp(m_old - m_new)
            l_ref[h] = alpha * l_ref[h] + jnp.sum(p, axis=1, keepdims=True)
            acc_ref[h] = alpha * acc_ref[h] + jnp.dot(
                p.astype(BF16), vd_ref[pl.ds(start, tq), hs], preferred_element_type=F32)
            m_ref[h] = m_new

    def far_step(j, carry):
        att_tile(j, None)
        return carry

    lax.fori_loop(0, i - 1, far_step, 0)

    @pl.when(i >= 1)
    def _():
        att_tile(i - 1, 0)

    att_tile(i, 1)
    for h in range(N_HEADS):
        o_ref[:, h * HEAD_DIM:(h + 1) * HEAD_DIM] = (acc_ref[h] / l_ref[h]).astype(o_ref.dtype)


def _dsa_attention(main, small, bias, batch, seq, *, qd_col, kd_col, vd_col, qi_col, tq=256):
    nq = seq // tq
    width = N_HEADS * HEAD_DIM
    top = min(TOPK_MAX, seq // 4)
    resident = pl.Buffered(1)
    return pl.pallas_call(
        functools.partial(_dsa_body, tq=tq, top=top),
        grid=(batch, nq),
        in_specs=[
            pl.BlockSpec((tq, width), lambda b, i: (b * nq + i, qd_col)),
            pl.BlockSpec((tq, IDX_HEADS * LANES), lambda b, i: (b * nq + i, qi_col)),
            pl.BlockSpec((tq, LANES), lambda b, i: (b * nq + i, 1)),
            pl.BlockSpec((seq, width), lambda b, i: (b, kd_col), pipeline_mode=resident),
            pl.BlockSpec((seq, width), lambda b, i: (b, vd_col), pipeline_mode=resident),
            pl.BlockSpec((seq, LANES), lambda b, i: (b, 0), pipeline_mode=resident),
            pl.BlockSpec((N_HEADS, tq, 2 * tq), lambda b, i: (0, 0, 0), pipeline_mode=resident),
        ],
        out_specs=pl.BlockSpec((tq, width), lambda b, i: (b * nq + i, 0)),
        out_shape=jax.ShapeDtypeStruct((batch * seq, width), BF16),
        scratch_shapes=[
            pltpu.VMEM((nq, tq, tq), F32),
            pltpu.VMEM((tq, width), BF16),
            pltpu.VMEM((N_HEADS, tq, 1), F32),
            pltpu.VMEM((N_HEADS, tq, 1), F32),
            pltpu.VMEM((N_HEADS, tq, HEAD_DIM), F32),
        ],
        compiler_params=_params("parallel", "arbitrary"),
        name="dsa_attention",
    )(main, main, small, main, main, small, bias)


def _merge_body(osb_ref, ods_ref, wsb_ref, wds_ref, gsb_ref, gds_ref, bsb_ref, bds_ref, o_ref):
    p_sb = jnp.dot(osb_ref[...], wsb_ref[...], preferred_element_type=F32)
    p_ds = jnp.dot(ods_ref[...], wds_ref[...], preferred_element_type=F32)
    g_sb = jax.nn.sigmoid(gsb_ref[...] + bsb_ref[...])
    g_ds = jax.nn.sigmoid(gds_ref[...] + bds_ref[...])
    o_ref[...] = (g_sb * p_sb + g_ds * p_ds).astype(o_ref.dtype)


def _merge(o_sb, o_ds, w_sb, w_ds, gates, b_gate, *, tm=512, tn=1024):
    m, k = o_sb.shape
    d = w_sb.shape[1]
    tm, tn = min(tm, m), min(tn, d)
    nd = d // tn
    b_gate = b_gate.reshape(1, 2 * d).astype(F32)
    return pl.pallas_call(
        _merge_body,
        grid=(m // tm, nd),
        in_specs=[
            pl.BlockSpec((tm, k), lambda i, j: (i, 0)),
            pl.BlockSpec((tm, k), lambda i, j: (i, 0)),
            pl.BlockSpec((k, tn), lambda i, j: (0, j)),
            pl.BlockSpec((k, tn), lambda i, j: (0, j)),
            pl.BlockSpec((tm, tn), lambda i, j: (i, j)),
            pl.BlockSpec((tm, tn), lambda i, j: (i, nd + j)),
            pl.BlockSpec((1, tn), lambda i, j: (0, j)),
            pl.BlockSpec((1, tn), lambda i, j: (0, nd + j)),
        ],
        out_specs=pl.BlockSpec((tm, tn), lambda i, j: (i, j)),
        out_shape=jax.ShapeDtypeStruct((m, d), BF16),
        compiler_params=_params("parallel", "arbitrary"),
        name="gated_merge",
    )(o_sb, o_ds, w_sb, w_ds, gates, gates, b_gate, b_gate)


def _cross_body(x_ref, g_ref, wq_ref, km_ref, vm_ref, wo_ref, o_ref):
    x = x_ref[...]
    h = _rms(x, g_ref[...]).astype(BF16)
    q = jnp.dot(h, wq_ref[...], preferred_element_type=F32) * HEAD_DIM ** -0.5
    q = q.astype(BF16)
    outs = []
    for hh in range(MEM_HEADS):
        hs = slice(hh * HEAD_DIM, (hh + 1) * HEAD_DIM)
        lg = lax.dot_general(q[:, hs], km_ref[:, hs], _NT, preferred_element_type=F32)
        p = jnp.exp(lg - jnp.max(lg, axis=1, keepdims=True))
        o = jnp.dot(p.astype(BF16), vm_ref[:, hs], preferred_element_type=F32)
        outs.append((o / jnp.sum(p, axis=1, keepdims=True)).astype(BF16))
    o = jnp.concatenate(outs, axis=1)
    o_ref[...] = x + jnp.dot(o, wo_ref[...], preferred_element_type=F32)


def _cross_attention(x, kv, g_cross, w_cq, w_co, batch, seq, *, tm=512):
    m, d = x.shape
    n_mem = kv.shape[0] // batch
    width = MEM_HEADS * HEAD_DIM
    tm = min(tm, seq)
    nt = seq // tm
    return pl.pallas_call(
        _cross_body,
        grid=(batch, nt),
        in_specs=[
            pl.BlockSpec((tm, d), lambda b, i: (b * nt + i, 0)),
            pl.BlockSpec((1, d), lambda b, i: (0, 0)),
            pl.BlockSpec((d, width), lambda b, i: (0, 0)),
            pl.BlockSpec((n_mem, width), lambda b, i: (b, 0)),
            pl.BlockSpec((n_mem, width), lambda b, i: (b, 1)),
            pl.BlockSpec((width, d), lambda b, i: (0, 0)),
        ],
        out_specs=pl.BlockSpec((tm, d), lambda b, i: (b * nt + i, 0)),
        out_shape=jax.ShapeDtypeStruct((m, d), F32),
        compiler_params=_params("parallel", "parallel"),
        name="cross_attention",
    )(x, g_cross.reshape(1, d).astype(F32), w_cq, kv, kv, w_co)


def _shift_rows(u, prev, shift, row):
    out = pltpu.roll(u, shift, axis=0)
    for r in range(shift):
        out = jnp.where(row == r, prev[SUBLANES - shift + r:SUBLANES - shift + r + 1, :], out)
    return out


def _conv_body(ua_ref, uv_ref, wa_ref, wv_ref, ba_ref, bv_ref, o_ref, pa_ref, pv_ref, *, tiles_per_seq):
    i = pl.program_id(1)

    @pl.when(i % tiles_per_seq == 0)
    def _():
        pa_ref[...] = jnp.zeros(pa_ref.shape, F32)
        pv_ref[...] = jnp.zeros(pv_ref.shape, F32)

    row = lax.broadcasted_iota(I32, ua_ref.shape, 0)

    def conv(u_ref, w_ref, b_ref, p_ref):
        u = u_ref[...].astype(F32)
        prev = p_ref[...]
        c = b_ref[...] + w_ref[CONV_WIDTH - 1:CONV_WIDTH, :] * u
        for tap in range(CONV_WIDTH - 1):
            shift = CONV_WIDTH - 1 - tap
            c = c + w_ref[tap:tap + 1, :] * _shift_rows(u, prev, shift, row)
        p_ref[...] = u[u.shape[0] - SUBLANES:, :]
        return c

    a = conv(ua_ref, wa_ref, ba_ref, pa_ref)
    val = conv(uv_ref, wv_ref, bv_ref, pv_ref)
    o_ref[...] = (jax.nn.gelu(a) * val).astype(o_ref.dtype)


def _conv_gate(u, conv_w, conv_b, seq, *, tm=512, tn=1024):
    m, two_ff = u.shape
    d_ff = two_ff // 2
    tm, tn = min(tm, seq), min(tn, d_ff)
    nf = d_ff // tn
    conv_w = conv_w.astype(F32)
    conv_b = conv_b.reshape(1, two_ff).astype(F32)
    return pl.pallas_call(
        functools.partial(_conv_body, tiles_per_seq=seq // tm),
        grid=(nf, m // tm),
        in_specs=[
            pl.BlockSpec((tm, tn), lambda j, i: (i, j)),
            pl.BlockSpec((tm, tn), lambda j, i: (i, nf + j)),
            pl.BlockSpec((CONV_WIDTH, tn), lambda j, i: (0, j)),
            pl.BlockSpec((CONV_WIDTH, tn), lambda j, i: (0, nf + j)),
            pl.BlockSpec((1, tn), lambda j, i: (0, j)),
            pl.BlockSpec((1, tn), lambda j, i: (0, nf + j)),
        ],
        out_specs=pl.BlockSpec((tm, tn), lambda j, i: (i, j)),
        out_shape=jax.ShapeDtypeStruct((m, d_ff), BF16),
        scratch_shapes=[pltpu.VMEM((SUBLANES, tn), F32), pltpu.VMEM((SUBLANES, tn), F32)],
        compiler_params=_params("parallel", "arbitrary"),
        name="conv_gate",
    )(u, u, conv_w, conv_w, conv_b, conv_b)


def _norm_body(x_ref, g_ref, o_ref):
    o_ref[...] = _rms(x_ref[...], g_ref[...]).astype(o_ref.dtype)


def _rmsnorm(x, g, *, tm=512):
    m, d = x.shape
    tm = min(tm, m)
    return pl.pallas_call(
        _norm_body,
        grid=(m // tm,),
        in_specs=[pl.BlockSpec((tm, d), lambda i: (i, 0)), pl.BlockSpec((1, d), lambda i: (0, 0))],
        out_specs=pl.BlockSpec((tm, d), lambda i: (i, 0)),
        out_shape=jax.ShapeDtypeStruct((m, d), F32),
        compiler_params=_params("parallel"),
        name="final_norm",
    )(x, g.reshape(1, d).astype(F32))


def _layer(x, mem, g_mix, w_in, b_gate, w_proj_sb, w_proj_dsa, w_out, rel_bias,
           g_cross, g_mem, w_cq, w_ckv, w_co, g_ffn, w_up, conv_w, conv_b, w_down, batch, seq):
    d = x.shape[1]
    width = N_HEADS * HEAD_DIM
    idx_w = IDX_HEADS * IDX_DIM
    o_qi = 6 * width
    o_ki = o_qi + idx_w
    o_wi = o_ki + IDX_DIM
    o_g = o_wi + IDX_HEADS

    w_qi = jnp.pad(w_in[:, o_qi:o_ki].reshape(d, IDX_HEADS, IDX_DIM),
                   ((0, 0), (0, 0), (0, LANES - IDX_DIM))).reshape(d, IDX_HEADS * LANES)
    w_main = jnp.concatenate([w_in[:, :o_qi], w_qi], axis=1).astype(BF16)
    w_small = jnp.concatenate([
        jnp.pad(w_in[:, o_ki:o_wi], ((0, 0), (0, LANES - IDX_DIM))),
        jnp.pad(w_in[:, o_wi:o_g], ((0, 0), (0, LANES - IDX_HEADS)))], axis=1).astype(BF16)
    w_gates = w_in[:, o_g:].astype(BF16)

    main = _matmul(x, w_main, name="in_proj_main", gain=g_mix, out_dtype=BF16, tm=512, tn=1024)
    small = _matmul(x, w_small, name="in_proj_index", gain=g_mix, out_dtype=F32, tm=512, tn=2 * LANES)
    gates = _matmul(x, w_gates, name="in_proj_gates", gain=g_mix, out_dtype=F32, tm=512, tn=1024)

    blocks = width // HEAD_DIM
    o_sb = _sb_attention(main, batch, seq, q_col=0, k_col=blocks, v_col=2 * blocks)
    tq = 256
    bias = _near_bias(rel_bias, tq)
    o_ds = _dsa_attention(main, small, bias, batch, seq, qd_col=3, kd_col=4, vd_col=5, qi_col=3, tq=tq)

    merged = _merge(o_sb, o_ds, w_proj_sb.astype(BF16), w_proj_dsa.astype(BF16), gates, b_gate)
    x = _matmul(merged, w_out.astype(BF16), name="out_proj", resid=x, out_dtype=F32, tm=512, tn=1024)

    kv = _matmul(mem, w_ckv.astype(BF16), name="mem_kv_proj", gain=g_mem, out_dtype=BF16, tm=512, tn=1024)
    x = _cross_attention(x, kv, g_cross, w_cq.astype(BF16), w_co.astype(BF16), batch, seq)

    u = _matmul(x, w_up.astype(BF16), name="ffn_up", gain=g_ffn, out_dtype=BF16, tm=512, tn=1024)
    act = _conv_gate(u, conv_w, conv_b, seq)
    return _matmul(act, w_down.astype(BF16), name="ffn_down", resid=x, out_dtype=F32, tm=512, tn=512)


def kernel(x, mem, g_mix, w_in, b_gate, w_proj_sb, w_proj_dsa, w_out, rel_bias, g_cross, g_mem,
           w_cq, w_ckv, w_co, g_ffn, w_up, conv_w, conv_b, w_down, g_final):
    batch, seq, d = x.shape
    h = x.reshape(batch * seq, d)
    mem2 = mem.reshape(batch * mem.shape[1], d)
    for l in range(g_mix.shape[0]):
        h = _layer(h, mem2, g_mix[l], w_in[l], b_gate[l], w_proj_sb[l], w_proj_dsa[l], w_out[l],
                   rel_bias, g_cross[l], g_mem[l], w_cq[l], w_ckv[l], w_co[l], g_ffn[l], w_up[l],
                   conv_w[l], conv_b[l], w_down[l], batch, seq)
    return _rmsnorm(h, g_final).reshape(batch, seq, d)
```

```python
import functools

import jax
import jax.numpy as jnp
from jax import lax
from jax.experimental import pallas as pl
from jax.experimental.pallas import tpu as pltpu

F32, BF16, I32 = jnp.float32, jnp.bfloat16, jnp.int32

EPS = 1e-6
HEAD_DIM = 128
N_HEADS = 8
IDX_HEADS = 16
IDX_DIM = 64
CHUNK = 64
TOPK_MAX = 256
N_BUCKETS = 32
MAX_DISTANCE = 128
MEM_HEADS = 4
CONV_WIDTH = 3

LANES = 128
SUBLANES = 8
VMEM_LIMIT_BYTES = 56 * 1024 * 1024
NEG_BIG = -1e30
EXP_UNDERFLOW = -104.0
LOG2E = 1.4426950408889634
_FLOAT_KEY_LOWEST = -(2 ** 31) + 2 ** 23
_MAX_SEARCH_STEPS = 160

_NT = (((1,), (1,)), ((), ()))


def _params(*sem):
    return pltpu.CompilerParams(dimension_semantics=sem, vmem_limit_bytes=VMEM_LIMIT_BYTES)


def _rms(x, g):
    inv = lax.rsqrt(jnp.mean(x * x, axis=-1, keepdims=True) + EPS)
    return x * inv * g


def _mm_body(*refs, has_gain, has_resid):
    it = iter(refs)
    a_ref = next(it)
    g_ref = next(it) if has_gain else None
    w_ref = next(it)
    r_ref = next(it) if has_resid else None
    o_ref = next(it)
    h_ref = next(it) if has_gain else None
    if has_gain:
        @pl.when(pl.program_id(1) == 0)
        def _():
            h_ref[...] = _rms(a_ref[...].astype(F32), g_ref[...]).astype(BF16)
        a = h_ref[...]
    else:
        a = a_ref[...]
    acc = jnp.dot(a, w_ref[...], preferred_element_type=F32)
    if has_resid:
        acc = acc + r_ref[...]
    o_ref[...] = acc.astype(o_ref.dtype)


def _matmul(a, w, *, name, gain=None, resid=None, out_dtype, tm, tn):
    m, k = a.shape
    n = w.shape[1]
    tm, tn = min(tm, m), min(tn, n)
    assert m % tm == 0 and n % tn == 0, (m, n, tm, tn)
    in_specs = [pl.BlockSpec((tm, k), lambda i, j: (i, 0))]
    args = [a]
    scratch = []
    if gain is not None:
        in_specs.append(pl.BlockSpec((1, k), lambda i, j: (0, 0)))
        args.append(gain.reshape(1, k).astype(F32))
        scratch.append(pltpu.VMEM((tm, k), BF16))
    in_specs.append(pl.BlockSpec((k, tn), lambda i, j: (0, j)))
    args.append(w)
    if resid is not None:
        in_specs.append(pl.BlockSpec((tm, tn), lambda i, j: (i, j)))
        args.append(resid)
    return pl.pallas_call(
        functools.partial(_mm_body, has_gain=gain is not None, has_resid=resid is not None),
        grid=(m // tm, n // tn),
        in_specs=in_specs,
        out_specs=pl.BlockSpec((tm, tn), lambda i, j: (i, j)),
        out_shape=jax.ShapeDtypeStruct((m, n), out_dtype),
        scratch_shapes=scratch,
        compiler_params=_params("parallel", "arbitrary"),
        name=name,
    )(*args)


def _sb_body(q_ref, k_ref, v_ref, u_ref, o_ref, *, tq):
    i = pl.program_id(2)
    q = (q_ref[...].astype(F32) * HEAD_DIM ** -0.5).astype(BF16)
    tri = u_ref[...]
    row = lax.broadcasted_iota(I32, (tq, tq), 0)
    col = lax.broadcasted_iota(I32, (tq, tq), 1)
    before = col < row

    def tile(j, carry, acc, diagonal):
        start = pl.multiple_of(j * tq, tq)
        k = k_ref[pl.ds(start, tq), :]
        v = v_ref[pl.ds(start, tq), :]
        z = lax.dot_general(q, k, _NT, preferred_element_type=F32)
        log_keep = jnp.minimum(-z, 0.0) - jnp.log1p(jnp.exp(-jnp.abs(z)))
        if diagonal:
            log_keep = jnp.where(before, log_keep, 0.0)
        c = jnp.dot(log_keep.astype(BF16), tri, preferred_element_type=F32) + carry
        w = jnp.exp(z + c)
        if diagonal:
            w = jnp.where(before, w, 0.0)
        acc = acc + jnp.dot(w.astype(BF16), v, preferred_element_type=F32)
        return c[:, 0:1], acc

    carry, acc = tile(i, jnp.zeros((tq, 1), F32), jnp.zeros((tq, HEAD_DIM), F32), True)

    def cond(state):
        j, live, _, _ = state
        return jnp.logical_and(j >= 0, live > EXP_UNDERFLOW)

    def body(state):
        j, _, carry, acc = state
        carry, acc = tile(j, carry, acc, False)
        return j - 1, jnp.max(carry), carry, acc

    _, _, _, acc = lax.while_loop(cond, body, (i - 1, jnp.max(carry), carry, acc))
    o_ref[...] = acc.astype(o_ref.dtype)


def _sb_attention(qkv, batch, seq, *, q_col, k_col, v_col, tq=256):
    nq = seq // tq
    tri = (jnp.arange(tq)[:, None] >= jnp.arange(tq)[None, :]).astype(BF16)
    return pl.pallas_call(
        functools.partial(_sb_body, tq=tq),
        grid=(batch, N_HEADS, nq),
        in_specs=[
            pl.BlockSpec((tq, HEAD_DIM), lambda b, h, i: (b * nq + i, q_col + h)),
            pl.BlockSpec((seq, HEAD_DIM), lambda b, h, i: (b, k_col + h)),
            pl.BlockSpec((seq, HEAD_DIM), lambda b, h, i: (b, v_col + h)),
            pl.BlockSpec((tq, tq), lambda b, h, i: (0, 0)),
        ],
        out_specs=pl.BlockSpec((tq, HEAD_DIM), lambda b, h, i: (b * nq + i, h)),
        out_shape=jax.ShapeDtypeStruct((batch * seq, N_HEADS * HEAD_DIM), BF16),
        compiler_params=_params("parallel", "parallel", "arbitrary"),
        name="sb_attention",
    )(qkv, qkv, qkv, tri)


def _bucket_thresholds():
    nb = N_BUCKETS // 2
    max_exact = nb // 2
    span = nb - max_exact
    out = []
    for k in range(1, span):
        n = max_exact
        while n ** span * max_exact ** k < MAX_DISTANCE ** k * max_exact ** span:
            n += 1
        out.append(n)
    return max_exact, out


def _bias_body(rb_ref, o_ref, *, tq):
    nb = N_BUCKETS // 2
    max_exact, steps = _bucket_thresholds()
    shape = (2 * tq, tq)
    rel = lax.broadcasted_iota(I32, shape, 0) - lax.broadcasted_iota(I32, shape, 1) - tq
    n = jnp.abs(rel)
    large = jnp.full(shape, max_exact, I32)
    for t in steps:
        large = large + (n >= t).astype(I32)
    bucket = jnp.where(rel > 0, nb, 0) + jnp.where(n < max_exact, n, large)
    for h in range(N_HEADS):
        val = jnp.zeros(shape, F32)
        for b in range(N_BUCKETS):
            val = jnp.where(bucket == b, rb_ref[b, h], val)
        o_ref[h] = (val - rb_ref[nb - 1, h]) * LOG2E


def _near_bias(rel_bias, tq):
    return pl.pallas_call(
        functools.partial(_bias_body, tq=tq),
        in_specs=[pl.BlockSpec(memory_space=pltpu.SMEM)],
        out_specs=pl.BlockSpec(memory_space=pltpu.VMEM),
        out_shape=jax.ShapeDtypeStruct((N_HEADS, 2 * tq, tq), F32),
        compiler_params=pltpu.CompilerParams(vmem_limit_bytes=VMEM_LIMIT_BYTES),
        name="dsa_near_bias",
    )(rel_bias.astype(F32))


def _float_key(x):
    bits = lax.bitcast_convert_type(x, I32)
    return bits ^ ((bits >> 31) & 0x7FFFFFFF)


def _key_float(key):
    return lax.bitcast_convert_type(key ^ ((key >> 31) & 0x7FFFFFFF), F32)


def _dsa_body(qd_ref, qi_ref, wq_ref, kd_ref, vt_ref, ki_ref, bias_ref, tri_ref, o_ref,
              sc_ref, qs_ref, m_ref, l_ref, acc_ref, lg_ref, p_ref, *, tq, top):
    i = pl.program_id(1)
    shape = (tq, tq)
    key_row = lax.broadcasted_iota(I32, shape, 0)
    qry_col = lax.broadcasted_iota(I32, shape, 1)
    visible = key_row // CHUNK <= qry_col // CHUNK

    w_t = (wq_ref[...] * (IDX_DIM ** -0.5 * IDX_HEADS ** -0.5)).T

    def score_tile(j, lo, hi, diagonal):
        start = pl.multiple_of(j * tq, tq)
        ki = ki_ref[pl.ds(start, tq), :].astype(BF16)
        s = jnp.zeros(shape, F32)
        for h in range(IDX_HEADS):
            d = lax.dot_general(ki, qi_ref[:, h * LANES:(h + 1) * LANES], _NT,
                                preferred_element_type=F32)
            s = s + w_t[h:h + 1, :] * jnp.maximum(d, 0.0)
        if diagonal:
            lo = jnp.minimum(lo, jnp.min(jnp.where(visible, s, jnp.inf), axis=0, keepdims=True))
            s = jnp.where(visible, s, -jnp.inf)
        else:
            lo = jnp.minimum(lo, jnp.min(s, axis=0, keepdims=True))
        hi = jnp.maximum(hi, jnp.max(s, axis=0, keepdims=True))
        sc_ref[j] = s
        return lo, hi

    lo, hi = lax.fori_loop(
        0, i, lambda j, c: score_tile(j, c[0], c[1], False),
        (jnp.full((1, tq), jnp.inf, F32), jnp.full((1, tq), -jnp.inf, F32)))
    lo, hi = score_tile(i, lo, hi, True)

    def count_ge(t):
        def step(j, acc):
            return acc + jnp.sum((sc_ref[j] >= t).astype(I32), axis=0, keepdims=True)
        return lax.fori_loop(0, i + 1, step, jnp.zeros((1, tq), I32))

    qry = lax.broadcasted_iota(I32, (1, tq), 1)
    n_visible = i * tq + (qry // CHUNK + 1) * CHUNK
    k_lo0 = _float_key(lo)
    k_hi0 = _float_key(hi) + 1
    wanted = n_visible > top
    adjacent0 = k_lo0 + 1 == k_hi0
    select_all = jnp.full((1, tq), _FLOAT_KEY_LOWEST, I32)

    def search_cond(state):
        it, active = state[0], state[1]
        return jnp.logical_and(it < _MAX_SEARCH_STEPS, jnp.max(active) > 0)

    def search_step(state):
        it, active, k_lo, c_lo, k_hi, c_hi, k_thr, tied = state
        active, tied = active > 0, tied > 0
        f_lo, f_hi = _key_float(k_lo), _key_float(k_hi)
        frac = ((c_lo - top).astype(F32) - 0.5) / (c_lo - c_hi).astype(F32)
        guess = _float_key(f_lo + (f_hi - f_lo) * frac)
        middle = (k_lo & k_hi) + ((k_lo ^ k_hi) >> 1)
        k_t = jnp.where(it % 3 == 2, middle, guess)
        k_t = jnp.minimum(jnp.maximum(k_t, k_lo + 1), k_hi - 1)
        c = count_ge(_key_float(k_t))
        found = jnp.logical_and(active, c == top)
        above = jnp.logical_and(active, c > top)
        below = jnp.logical_and(active, c < top)
        k_lo, c_lo = jnp.where(above, k_t, k_lo), jnp.where(above, c, c_lo)
        k_hi, c_hi = jnp.where(below, k_t, k_hi), jnp.where(below, c, c_hi)
        closed = jnp.logical_and(jnp.logical_or(above, below), k_lo + 1 == k_hi)
        k_thr = jnp.where(found, k_t, jnp.where(closed, k_lo, k_thr))
        tied = jnp.logical_or(tied, closed)
        active = jnp.logical_and(active, jnp.logical_not(jnp.logical_or(found, closed)))
        return it + 1, active.astype(I32), k_lo, c_lo, k_hi, c_hi, k_thr, tied.astype(I32)

    tied0 = jnp.logical_and(wanted, adjacent0)
    state = lax.while_loop(search_cond, search_step, (
        jnp.int32(0), jnp.logical_and(wanted, jnp.logical_not(adjacent0)).astype(I32),
        k_lo0, n_visible, k_hi0, jnp.zeros((1, tq), I32),
        jnp.where(tied0, k_lo0, select_all), tied0.astype(I32)))
    _, _, _, _, _, c_hi, k_thr, tied = state
    tied = tied > 0
    thr = _key_float(k_thr)

    def plain_mask(j, carry):
        sc_ref[j] = jnp.where(sc_ref[j] >= thr, 0.0, NEG_BIG)
        return carry

    def tie_mask(j, seen):
        s = sc_ref[j]
        equal = s == thr
        rank = jnp.dot(tri_ref[...], equal.astype(BF16), preferred_element_type=F32) + seen
        quota = jnp.where(tied, (top - c_hi).astype(F32), jnp.inf)
        keep_equal = jnp.where(rank < quota, 0.0, NEG_BIG)
        sc_ref[j] = jnp.where(s > thr, 0.0, jnp.where(equal, keep_equal, NEG_BIG))
        return seen + jnp.sum(equal.astype(F32), axis=0, keepdims=True)

    def with_ties():
        lax.fori_loop(0, i + 1, tie_mask, jnp.zeros((1, tq), F32))
        return jnp.int32(0)

    def without_ties():
        return lax.fori_loop(0, i + 1, plain_mask, jnp.int32(0))

    lax.cond(jnp.max(tied.astype(I32)) > 0, with_ties, without_ties)

    qs_ref[...] = (qd_ref[...].astype(F32) * (HEAD_DIM ** -0.5 * LOG2E)).astype(BF16)
    m_ref[...] = jnp.full(m_ref.shape, NEG_BIG, F32)
    l_ref[...] = jnp.zeros(l_ref.shape, F32)
    acc_ref[...] = jnp.zeros(acc_ref.shape, F32)
    ones = jnp.ones((2 * SUBLANES, tq), BF16)

    heads = [slice(h * HEAD_DIM, (h + 1) * HEAD_DIM) for h in range(N_HEADS)]

    def att_tile(j, near):
        start = pl.multiple_of(j * tq, tq)
        mask = sc_ref[j]
        tile_max = []
        for h, hs in enumerate(heads):
            lg = lax.dot_general(kd_ref[pl.ds(start, tq), hs], qs_ref[:, hs], _NT,
                                 preferred_element_type=F32)
            if near is not None:
                lg = lg + bias_ref[h, near * tq:(near + 1) * tq, :]
            lg = lg + mask
            lg_ref[h] = lg
            tile_max.append(jnp.max(lg, axis=0, keepdims=True))
        m_old = m_ref[...]
        m_new = jnp.maximum(m_old, jnp.concatenate(tile_max, axis=0))
        alpha = jnp.exp2(m_old - m_new)
        m_ref[...] = m_new
        for h in range(N_HEADS):
            p_ref[h] = jnp.exp2(lg_ref[h] - m_new[h:h + 1, :]).astype(BF16)
        denom = []
        for h, hs in enumerate(heads):
            v_ext = jnp.concatenate([vt_ref[j, hs, :], ones], axis=0)
            pv = jnp.dot(v_ext, p_ref[h], preferred_element_type=F32)
            acc_ref[h] = alpha[h:h + 1, :] * acc_ref[h] + pv[:HEAD_DIM]
            denom.append(pv[HEAD_DIM:HEAD_DIM + 1])
        l_ref[...] = alpha * l_ref[...] + jnp.concatenate(denom, axis=0)

    def far_step(j, carry):
        att_tile(j, None)
        return carry

    lax.fori_loop(0, i - 1, far_step, 0)

    @pl.when(i >= 1)
    def _():
        att_tile(i - 1, 0)

    att_tile(i, 1)
    for h in range(N_HEADS):
        o = acc_ref[h] / l_ref[h:h + 1, :]
        o_ref[:, h * HEAD_DIM:(h + 1) * HEAD_DIM] = o.T.astype(o_ref.dtype)


def _dsa_attention(main, small, bias, batch, seq, *, qd_col, kd_col, vd_col, qi_col, tq=256):
    nq = seq // tq
    width = N_HEADS * HEAD_DIM
    top = min(TOPK_MAX, seq // 4)
    v_t = main[:, vd_col * width:(vd_col + 1) * width].reshape(batch, nq, tq, width).swapaxes(2, 3)
    tri = (jnp.arange(tq)[None, :] < jnp.arange(tq)[:, None]).astype(BF16)
    resident = pl.Buffered(1)
    return pl.pallas_call(
        functools.partial(_dsa_body, tq=tq, top=top),
        grid=(batch, nq),
        in_specs=[
            pl.BlockSpec((tq, width), lambda b, i: (b * nq + i, qd_col)),
            pl.BlockSpec((tq, IDX_HEADS * LANES), lambda b, i: (b * nq + i, qi_col)),
            pl.BlockSpec((tq, LANES), lambda b, i: (b * nq + i, 1)),
            pl.BlockSpec((seq, width), lambda b, i: (b, kd_col), pipeline_mode=resident),
            pl.BlockSpec((None, nq, width, tq), lambda b, i: (b, 0, 0, 0), pipeline_mode=resident),
            pl.BlockSpec((seq, LANES), lambda b, i: (b, 0), pipeline_mode=resident),
            pl.BlockSpec((N_HEADS, 2 * tq, tq), lambda b, i: (0, 0, 0), pipeline_mode=resident),
            pl.BlockSpec((tq, tq), lambda b, i: (0, 0), pipeline_mode=resident),
        ],
        out_specs=pl.BlockSpec((tq, width), lambda b, i: (b * nq + i, 0)),
        out_shape=jax.ShapeDtypeStruct((batch * seq, width), BF16),
        scratch_shapes=[
            pltpu.VMEM((nq, tq, tq), F32),
            pltpu.VMEM((tq, width), BF16),
            pltpu.VMEM((N_HEADS, tq), F32),
            pltpu.VMEM((N_HEADS, tq), F32),
            pltpu.VMEM((N_HEADS, HEAD_DIM, tq), F32),
            pltpu.VMEM((N_HEADS, tq, tq), F32),
            pltpu.VMEM((N_HEADS, tq, tq), BF16),
        ],
        compiler_params=_params("parallel", "arbitrary"),
        name="dsa_attention",
    )(main, main, small, main, v_t, small, bias, tri)


def _merge_body(osb_ref, ods_ref, wsb_ref, wds_ref, gsb_ref, gds_ref, bsb_ref, bds_ref, o_ref):
    p_sb = jnp.dot(osb_ref[...], wsb_ref[...], preferred_element_type=F32)
    p_ds = jnp.dot(ods_ref[...], wds_ref[...], preferred_element_type=F32)
    g_sb = jax.nn.sigmoid(gsb_ref[...] + bsb_ref[...])
    g_ds = jax.nn.sigmoid(gds_ref[...] + bds_ref[...])
    o_ref[...] = (g_sb * p_sb + g_ds * p_ds).astype(o_ref.dtype)


def _merge(o_sb, o_ds, w_sb, w_ds, gates, b_gate, *, tm=512, tn=1024):
    m, k = o_sb.shape
    d = w_sb.shape[1]
    tm, tn = min(tm, m), min(tn, d)
    nd = d // tn
    b_gate = b_gate.reshape(1, 2 * d).astype(F32)
    return pl.pallas_call(
        _merge_body,
        grid=(m // tm, nd),
        in_specs=[
            pl.BlockSpec((tm, k), lambda i, j: (i, 0)),
            pl.BlockSpec((tm, k), lambda i, j: (i, 0)),
            pl.BlockSpec((k, tn), lambda i, j: (0, j)),
            pl.BlockSpec((k, tn), lambda i, j: (0, j)),
            pl.BlockSpec((tm, tn), lambda i, j: (i, j)),
            pl.BlockSpec((tm, tn), lambda i, j: (i, nd + j)),
            pl.BlockSpec((1, tn), lambda i, j: (0, j)),
            pl.BlockSpec((1, tn), lambda i, j: (0, nd + j)),
        ],
        out_specs=pl.BlockSpec((tm, tn), lambda i, j: (i, j)),
        out_shape=jax.ShapeDtypeStruct((m, d), BF16),
        compiler_params=_params("parallel", "arbitrary"),
        name="gated_merge",
    )(o_sb, o_ds, w_sb, w_ds, gates, gates, b_gate, b_gate)


def _cross_body(x_ref, g_ref, wq_ref, km_ref, vm_ref, wo_ref, o_ref):
    x = x_ref[...]
    h = _rms(x, g_ref[...]).astype(BF16)
    q = jnp.dot(h, wq_ref[...], preferred_element_type=F32) * HEAD_DIM ** -0.5
    q = q.astype(BF16)
    outs = []
    for hh in range(MEM_HEADS):
        hs = slice(hh * HEAD_DIM, (hh + 1) * HEAD_DIM)
        lg = lax.dot_general(q[:, hs], km_ref[:, hs], _NT, preferred_element_type=F32)
        p = jnp.exp(lg - jnp.max(lg, axis=1, keepdims=True))
        o = jnp.dot(p.astype(BF16), vm_ref[:, hs], preferred_element_type=F32)
        outs.append((o / jnp.sum(p, axis=1, keepdims=True)).astype(BF16))
    o = jnp.concatenate(outs, axis=1)
    o_ref[...] = x + jnp.dot(o, wo_ref[...], preferred_element_type=F32)


def _cross_attention(x, kv, g_cross, w_cq, w_co, batch, seq, *, tm=512):
    m, d = x.shape
    n_mem = kv.shape[0] // batch
    width = MEM_HEADS * HEAD_DIM
    tm = min(tm, seq)
    nt = seq // tm
    return pl.pallas_call(
        _cross_body,
        grid=(batch, nt),
        in_specs=[
            pl.BlockSpec((tm, d), lambda b, i: (b * nt + i, 0)),
            pl.BlockSpec((1, d), lambda b, i: (0, 0)),
            pl.BlockSpec((d, width), lambda b, i: (0, 0)),
            pl.BlockSpec((n_mem, width), lambda b, i: (b, 0)),
            pl.BlockSpec((n_mem, width), lambda b, i: (b, 1)),
            pl.BlockSpec((width, d), lambda b, i: (0, 0)),
        ],
        out_specs=pl.BlockSpec((tm, d), lambda b, i: (b * nt + i, 0)),
        out_shape=jax.ShapeDtypeStruct((m, d), F32),
        compiler_params=_params("parallel", "parallel"),
        name="cross_attention",
    )(x, g_cross.reshape(1, d).astype(F32), w_cq, kv, kv, w_co)


def _shift_rows(u, prev, shift, row):
    out = pltpu.roll(u, shift, axis=0)
    for r in range(shift):
        out = jnp.where(row == r, prev[SUBLANES - shift + r:SUBLANES - shift + r + 1, :], out)
    return out


def _conv_body(ua_ref, uv_ref, wa_ref, wv_ref, ba_ref, bv_ref, o_ref, pa_ref, pv_ref, *, tiles_per_seq):
    i = pl.program_id(1)

    @pl.when(i % tiles_per_seq == 0)
    def _():
        pa_ref[...] = jnp.zeros(pa_ref.shape, F32)
        pv_ref[...] = jnp.zeros(pv_ref.shape, F32)

    row = lax.broadcasted_iota(I32, ua_ref.shape, 0)

    def conv(u_ref, w_ref, b_ref, p_ref):
        u = u_ref[...].astype(F32)
        prev = p_ref[...]
        c = b_ref[...] + w_ref[CONV_WIDTH - 1:CONV_WIDTH, :] * u
        for tap in range(CONV_WIDTH - 1):
            shift = CONV_WIDTH - 1 - tap
            c = c + w_ref[tap:tap + 1, :] * _shift_rows(u, prev, shift, row)
        p_ref[...] = u[u.shape[0] - SUBLANES:, :]
        return c

    a = conv(ua_ref, wa_ref, ba_ref, pa_ref)
    val = conv(uv_ref, wv_ref, bv_ref, pv_ref)
    o_ref[...] = (jax.nn.gelu(a) * val).astype(o_ref.dtype)


def _conv_gate(u, conv_w, conv_b, seq, *, tm=512, tn=1024):
    m, two_ff = u.shape
    d_ff = two_ff // 2
    tm, tn = min(tm, seq), min(tn, d_ff)
    nf = d_ff // tn
    conv_w = conv_w.astype(F32)
    conv_b = conv_b.reshape(1, two_ff).astype(F32)
    return pl.pallas_call(
        functools.partial(_conv_body, tiles_per_seq=seq // tm),
        grid=(nf, m // tm),
        in_specs=[
            pl.BlockSpec((tm, tn), lambda j, i: (i, j)),
            pl.BlockSpec((tm, tn), lambda j, i: (i, nf + j)),
            pl.BlockSpec((CONV_WIDTH, tn), lambda j, i: (0, j)),
            pl.BlockSpec((CONV_WIDTH, tn), lambda j, i: (0, nf + j)),
            pl.BlockSpec((1, tn), lambda j, i: (0, j)),
            pl.BlockSpec((1, tn), lambda j, i: (0, nf + j)),
        ],
        out_specs=pl.BlockSpec((tm, tn), lambda j, i: (i, j)),
        out_shape=jax.ShapeDtypeStruct((m, d_ff), BF16),
        scratch_shapes=[pltpu.VMEM((SUBLANES, tn), F32), pltpu.VMEM((SUBLANES, tn), F32)],
        compiler_params=_params("parallel", "arbitrary"),
        name="conv_gate",
    )(u, u, conv_w, conv_w, conv_b, conv_b)


def _norm_body(x_ref, g_ref, o_ref):
    o_ref[...] = _rms(x_ref[...], g_ref[...]).astype(o_ref.dtype)


def _rmsnorm(x, g, *, tm=512):
    m, d = x.shape
    tm = min(tm, m)
    return pl.pallas_call(
        _norm_body,
        grid=(m // tm,),
        in_specs=[pl.BlockSpec((tm, d), lambda i: (i, 0)), pl.BlockSpec((1, d), lambda i: (0, 0))],
        out_specs=pl.BlockSpec((tm, d), lambda i: (i, 0)),
        out_shape=jax.ShapeDtypeStruct((m, d), F32),
        compiler_params=_params("parallel"),
        name="final_norm",
    )(x, g.reshape(1, d).astype(F32))


def _layer(x, mem, g_mix, w_in, b_gate, w_proj_sb, w_proj_dsa, w_out, rel_bias,
           g_cross, g_mem, w_cq, w_ckv, w_co, g_ffn, w_up, conv_w, conv_b, w_down, batch, seq):
    d = x.shape[1]
    width = N_HEADS * HEAD_DIM
    idx_w = IDX_HEADS * IDX_DIM
    o_qi = 6 * width
    o_ki = o_qi + idx_w
    o_wi = o_ki + IDX_DIM
    o_g = o_wi + IDX_HEADS

    w_qi = jnp.pad(w_in[:, o_qi:o_ki].reshape(d, IDX_HEADS, IDX_DIM),
                   ((0, 0), (0, 0), (0, LANES - IDX_DIM))).reshape(d, IDX_HEADS * LANES)
    w_main = jnp.concatenate([w_in[:, :o_qi], w_qi], axis=1).astype(BF16)
    w_small = jnp.concatenate([
        jnp.pad(w_in[:, o_ki:o_wi], ((0, 0), (0, LANES - IDX_DIM))),
        jnp.pad(w_in[:, o_wi:o_g], ((0, 0), (0, LANES - IDX_HEADS)))], axis=1).astype(BF16)
    w_gates = w_in[:, o_g:].astype(BF16)

    main = _matmul(x, w_main, name="in_proj_main", gain=g_mix, out_dtype=BF16, tm=512, tn=1024)
    small = _matmul(x, w_small, name="in_proj_index", gain=g_mix, out_dtype=F32, tm=512, tn=2 * LANES)
    gates = _matmul(x, w_gates, name="in_proj_gates", gain=g_mix, out_dtype=F32, tm=512, tn=1024)

    blocks = width // HEAD_DIM
    o_sb = _sb_attention(main, batch, seq, q_col=0, k_col=blocks, v_col=2 * blocks)
    tq = 256
    bias = _near_bias(rel_bias, tq)
    o_ds = _dsa_attention(main, small, bias, batch, seq, qd_col=3, kd_col=4, vd_col=5, qi_col=3, tq=tq)

    merged = _merge(o_sb, o_ds, w_proj_sb.astype(BF16), w_proj_dsa.astype(BF16), gates, b_gate)
    x = _matmul(merged, w_out.astype(BF16), name="out_proj", resid=x, out_dtype=F32, tm=512, tn=1024)

    kv = _matmul(mem, w_ckv.astype(BF16), name="mem_kv_proj", gain=g_mem, out_dtype=BF16, tm=512, tn=1024)
    x = _cross_attention(x, kv, g_cross, w_cq.astype(BF16), w_co.astype(BF16), batch, seq)

    u = _matmul(x, w_up.astype(BF16), name="ffn_up", gain=g_ffn, out_dtype=BF16, tm=512, tn=1024)
    act = _conv_gate(u, conv_w, conv_b, seq)
    return _matmul(act, w_down.astype(BF16), name="ffn_down", resid=x, out_dtype=F32, tm=512, tn=512)


def kernel(x, mem, g_mix, w_in, b_gate, w_proj_sb, w_proj_dsa, w_out, rel_bias, g_cross, g_mem,
           w_cq, w_ckv, w_co, g_ffn, w_up, conv_w, conv_b, w_down, g_final):
    batch, seq, d = x.shape
    h = x.reshape(batch * seq, d)
    mem2 = mem.reshape(batch * mem.shape[1], d)
    for l in range(g_mix.shape[0]):
        h = _layer(h, mem2, g_mix[l], w_in[l], b_gate[l], w_proj_sb[l], w_proj_dsa[l], w_out[l],
                   rel_bias, g_cross[l], g_mem[l], w_cq[l], w_ckv[l], w_co[l], g_ffn[l], w_up[l],
                   conv_w[l], conv_b[l], w_down[l], batch, seq)
    return _rmsnorm(h, g_final).reshape(batch, seq, d)
```

```python
import functools

import jax
import jax.numpy as jnp
from jax import lax
from jax.experimental import pallas as pl
from jax.experimental.pallas import tpu as pltpu

F32, BF16, I32 = jnp.float32, jnp.bfloat16, jnp.int32

EPS = 1e-6
HEAD_DIM = 128
N_HEADS = 8
IDX_HEADS = 16
IDX_DIM = 64
CHUNK = 64
TOPK_MAX = 256
N_BUCKETS = 32
MAX_DISTANCE = 128
MEM_HEADS = 4
CONV_WIDTH = 3

LANES = 128
SUBLANES = 8
VMEM_LIMIT_BYTES = 56 * 1024 * 1024
NEG_BIG = -1e30
EXP_UNDERFLOW = -104.0
LOG2E = 1.4426950408889634
_FLOAT_KEY_LOWEST = -(2 ** 31) + 2 ** 23
_VALUE_STEPS = 3
_STEPS_PER_CHECK = 4
_MAX_SEARCH_STEPS = 48

_NT = (((1,), (1,)), ((), ()))


def _params(*sem):
    return pltpu.CompilerParams(dimension_semantics=sem, vmem_limit_bytes=VMEM_LIMIT_BYTES)


def _rms(x, g):
    inv = lax.rsqrt(jnp.mean(x * x, axis=-1, keepdims=True) + EPS)
    return x * inv * g


def _mm_body(*refs, has_gain, has_resid):
    it = iter(refs)
    a_ref = next(it)
    g_ref = next(it) if has_gain else None
    w_ref = next(it)
    r_ref = next(it) if has_resid else None
    o_ref = next(it)
    h_ref = next(it) if has_gain else None
    if has_gain:
        @pl.when(pl.program_id(1) == 0)
        def _():
            h_ref[...] = _rms(a_ref[...].astype(F32), g_ref[...]).astype(BF16)
        a = h_ref[...]
    else:
        a = a_ref[...]
    acc = jnp.dot(a, w_ref[...], preferred_element_type=F32)
    if has_resid:
        acc = acc + r_ref[...]
    o_ref[...] = acc.astype(o_ref.dtype)


def _matmul(a, w, *, name, gain=None, resid=None, out_dtype, tm, tn):
    m, k = a.shape
    n = w.shape[1]
    tm, tn = min(tm, m), min(tn, n)
    assert m % tm == 0 and n % tn == 0, (m, n, tm, tn)
    in_specs = [pl.BlockSpec((tm, k), lambda i, j: (i, 0))]
    args = [a]
    scratch = []
    if gain is not None:
        in_specs.append(pl.BlockSpec((1, k), lambda i, j: (0, 0)))
        args.append(gain.reshape(1, k).astype(F32))
        scratch.append(pltpu.VMEM((tm, k), BF16))
    in_specs.append(pl.BlockSpec((k, tn), lambda i, j: (0, j)))
    args.append(w)
    if resid is not None:
        in_specs.append(pl.BlockSpec((tm, tn), lambda i, j: (i, j)))
        args.append(resid)
    return pl.pallas_call(
        functools.partial(_mm_body, has_gain=gain is not None, has_resid=resid is not None),
        grid=(m // tm, n // tn),
        in_specs=in_specs,
        out_specs=pl.BlockSpec((tm, tn), lambda i, j: (i, j)),
        out_shape=jax.ShapeDtypeStruct((m, n), out_dtype),
        scratch_shapes=scratch,
        compiler_params=_params("parallel", "arbitrary"),
        name=name,
    )(*args)


def _sb_body(q_ref, k_ref, v_ref, u_ref, o_ref, *, tq):
    i = pl.program_id(2)
    q = (q_ref[...].astype(F32) * HEAD_DIM ** -0.5).astype(BF16)
    tri = u_ref[...]
    row = lax.broadcasted_iota(I32, (tq, tq), 0)
    col = lax.broadcasted_iota(I32, (tq, tq), 1)
    before = col < row

    def tile(j, carry, acc, diagonal):
        start = pl.multiple_of(j * tq, tq)
        k = k_ref[pl.ds(start, tq), :]
        v = v_ref[pl.ds(start, tq), :]
        z = lax.dot_general(q, k, _NT, preferred_element_type=F32)
        log_keep = jnp.minimum(-z, 0.0) - jnp.log1p(jnp.exp(-jnp.abs(z)))
        if diagonal:
            log_keep = jnp.where(before, log_keep, 0.0)
        c = jnp.dot(log_keep.astype(BF16), tri, preferred_element_type=F32) + carry
        w = jnp.exp(z + c)
        if diagonal:
            w = jnp.where(before, w, 0.0)
        acc = acc + jnp.dot(w.astype(BF16), v, preferred_element_type=F32)
        return c[:, 0:1], acc

    carry, acc = tile(i, jnp.zeros((tq, 1), F32), jnp.zeros((tq, HEAD_DIM), F32), True)

    def cond(state):
        j, live, _, _ = state
        return jnp.logical_and(j >= 0, live > EXP_UNDERFLOW)

    def body(state):
        j, _, carry, acc = state
        carry, acc = tile(j, carry, acc, False)
        return j - 1, jnp.max(carry), carry, acc

    _, _, _, acc = lax.while_loop(cond, body, (i - 1, jnp.max(carry), carry, acc))
    o_ref[...] = acc.astype(o_ref.dtype)


def _sb_attention(qkv, batch, seq, *, q_col, k_col, v_col, tq=256):
    nq = seq // tq
    tri = (jnp.arange(tq)[:, None] >= jnp.arange(tq)[None, :]).astype(BF16)
    return pl.pallas_call(
        functools.partial(_sb_body, tq=tq),
        grid=(batch, N_HEADS, nq),
        in_specs=[
            pl.BlockSpec((tq, HEAD_DIM), lambda b, h, i: (b * nq + i, q_col + h)),
            pl.BlockSpec((seq, HEAD_DIM), lambda b, h, i: (b, k_col + h)),
            pl.BlockSpec((seq, HEAD_DIM), lambda b, h, i: (b, v_col + h)),
            pl.BlockSpec((tq, tq), lambda b, h, i: (0, 0)),
        ],
        out_specs=pl.BlockSpec((tq, HEAD_DIM), lambda b, h, i: (b * nq + i, h)),
        out_shape=jax.ShapeDtypeStruct((batch * seq, N_HEADS * HEAD_DIM), BF16),
        compiler_params=_params("parallel", "parallel", "arbitrary"),
        name="sb_attention",
    )(qkv, qkv, qkv, tri)


def _bucket_thresholds():
    nb = N_BUCKETS // 2
    max_exact = nb // 2
    span = nb - max_exact
    out = []
    for k in range(1, span):
        n = max_exact
        while n ** span * max_exact ** k < MAX_DISTANCE ** k * max_exact ** span:
            n += 1
        out.append(n)
    return max_exact, out


def _bias_body(rb_ref, o_ref, *, tq):
    nb = N_BUCKETS // 2
    max_exact, steps = _bucket_thresholds()
    shape = (2 * tq, tq)
    rel = lax.broadcasted_iota(I32, shape, 0) - lax.broadcasted_iota(I32, shape, 1) - tq
    n = jnp.abs(rel)
    large = jnp.full(shape, max_exact, I32)
    for t in steps:
        large = large + (n >= t).astype(I32)
    bucket = jnp.where(rel > 0, nb, 0) + jnp.where(n < max_exact, n, large)
    for h in range(N_HEADS):
        val = jnp.zeros(shape, F32)
        for b in range(N_BUCKETS):
            val = jnp.where(bucket == b, rb_ref[b, h], val)
        o_ref[h] = (val - rb_ref[nb - 1, h]) * LOG2E


def _near_bias(rel_bias, tq):
    return pl.pallas_call(
        functools.partial(_bias_body, tq=tq),
        in_specs=[pl.BlockSpec(memory_space=pltpu.SMEM)],
        out_specs=pl.BlockSpec(memory_space=pltpu.VMEM),
        out_shape=jax.ShapeDtypeStruct((N_HEADS, 2 * tq, tq), F32),
        compiler_params=pltpu.CompilerParams(vmem_limit_bytes=VMEM_LIMIT_BYTES),
        name="dsa_near_bias",
    )(rel_bias.astype(F32))


def _float_key(x):
    bits = lax.bitcast_convert_type(x, I32)
    return bits ^ ((bits >> 31) & 0x7FFFFFFF)


def _key_float(key):
    return lax.bitcast_convert_type(key ^ ((key >> 31) & 0x7FFFFFFF), F32)


def _dsa_body(qd_ref, qi_ref, wq_ref, kd_ref, vt_ref, ki_ref, bias_ref, tri_ref, o_ref,
              sc_ref, qs_ref, m_ref, l_ref, acc_ref, lg_ref, p_ref, *, tq, top):
    i = pl.program_id(1)
    shape = (tq, tq)
    key_row = lax.broadcasted_iota(I32, shape, 0)
    qry_col = lax.broadcasted_iota(I32, shape, 1)
    visible = key_row // CHUNK <= qry_col // CHUNK

    w_t = (wq_ref[...] * (IDX_DIM ** -0.5 * IDX_HEADS ** -0.5)).T

    def score_tile(j, lo, hi, diagonal):
        start = pl.multiple_of(j * tq, tq)
        ki = ki_ref[pl.ds(start, tq), :].astype(BF16)
        s = jnp.zeros(shape, F32)
        for h in range(IDX_HEADS):
            d = lax.dot_general(ki, qi_ref[:, h * LANES:(h + 1) * LANES], _NT,
                                preferred_element_type=F32)
            s = s + w_t[h:h + 1, :] * jnp.maximum(d, 0.0)
        if diagonal:
            lo = jnp.minimum(lo, jnp.min(jnp.where(visible, s, jnp.inf), axis=0, keepdims=True))
            s = jnp.where(visible, s, -jnp.inf)
        else:
            lo = jnp.minimum(lo, jnp.min(s, axis=0, keepdims=True))
        hi = jnp.maximum(hi, jnp.max(s, axis=0, keepdims=True))
        sc_ref[j] = s
        return lo, hi

    lo, hi = lax.fori_loop(
        0, i, lambda j, c: score_tile(j, c[0], c[1], False),
        (jnp.full((1, tq), jnp.inf, F32), jnp.full((1, tq), -jnp.inf, F32)))
    lo, hi = score_tile(i, lo, hi, True)

    def count_ge(t):
        def step(j, acc):
            hit = (sc_ref[j] >= t).astype(I32)
            for g in range(tq // SUBLANES):
                acc = acc + hit[g * SUBLANES:(g + 1) * SUBLANES, :]
            return acc
        acc = lax.fori_loop(0, i + 1, step, jnp.zeros((SUBLANES, tq), I32))
        return jnp.sum(acc, axis=0, keepdims=True)

    qry = lax.broadcasted_iota(I32, (1, tq), 1)
    n_visible = i * tq + (qry // CHUNK + 1) * CHUNK
    k_lo0 = _float_key(lo)
    k_hi0 = _float_key(hi) + 1
    wanted = n_visible > top
    adjacent0 = k_lo0 + 1 == k_hi0
    select_all = jnp.full((1, tq), _FLOAT_KEY_LOWEST, I32)

    def search_cond(state):
        it, active = state[0], state[1]
        return jnp.logical_and(it < _MAX_SEARCH_STEPS, jnp.max(active) > 0)

    def search_step(it, active, k_lo, k_hi, c_hi, k_thr, tied):
        by_value = _float_key(0.5 * _key_float(k_lo) + 0.5 * _key_float(k_hi))
        by_bits = (k_lo & k_hi) + ((k_lo ^ k_hi) >> 1)
        k_t = jnp.where(it < _VALUE_STEPS, by_value, by_bits)
        k_t = jnp.minimum(jnp.maximum(k_t, k_lo + 1), k_hi - 1)
        c = count_ge(_key_float(k_t))
        found = jnp.logical_and(active, c == top)
        above = jnp.logical_and(active, c > top)
        below = jnp.logical_and(active, c < top)
        k_lo = jnp.where(above, k_t, k_lo)
        k_hi, c_hi = jnp.where(below, k_t, k_hi), jnp.where(below, c, c_hi)
        closed = jnp.logical_and(jnp.logical_or(above, below), k_lo + 1 == k_hi)
        k_thr = jnp.where(found, k_t, jnp.where(closed, k_lo, k_thr))
        tied = jnp.logical_or(tied, closed)
        active = jnp.logical_and(active, jnp.logical_not(jnp.logical_or(found, closed)))
        return active, k_lo, k_hi, c_hi, k_thr, tied

    def search_body(state):
        it, active, k_lo, k_hi, c_hi, k_thr, tied = state
        active, tied = active > 0, tied > 0
        for _ in range(_STEPS_PER_CHECK):
            active, k_lo, k_hi, c_hi, k_thr, tied = search_step(
                it, active, k_lo, k_hi, c_hi, k_thr, tied)
            it = it + 1
        return it, active.astype(I32), k_lo, k_hi, c_hi, k_thr, tied.astype(I32)

    tied0 = jnp.logical_and(wanted, adjacent0)
    state = lax.while_loop(search_cond, search_body, (
        jnp.int32(0), jnp.logical_and(wanted, jnp.logical_not(adjacent0)).astype(I32),
        k_lo0, k_hi0, jnp.zeros((1, tq), I32),
        jnp.where(tied0, k_lo0, select_all), tied0.astype(I32)))
    _, _, _, _, c_hi, k_thr, tied = state
    tied = tied > 0
    thr = _key_float(k_thr)

    def plain_mask(j, carry):
        sc_ref[j] = jnp.where(sc_ref[j] >= thr, 0.0, NEG_BIG)
        return carry

    def tie_mask(j, seen):
        s = sc_ref[j]
        equal = s == thr
        rank = jnp.dot(tri_ref[...], equal.astype(BF16), preferred_element_type=F32) + seen
        quota = jnp.where(tied, (top - c_hi).astype(F32), jnp.inf)
        keep_equal = jnp.where(rank < quota, 0.0, NEG_BIG)
        sc_ref[j] = jnp.where(s > thr, 0.0, jnp.where(equal, keep_equal, NEG_BIG))
        return seen + jnp.sum(equal.astype(F32), axis=0, keepdims=True)

    def with_ties():
        lax.fori_loop(0, i + 1, tie_mask, jnp.zeros((1, tq), F32))
        return jnp.int32(0)

    def without_ties():
        return lax.fori_loop(0, i + 1, plain_mask, jnp.int32(0))

    lax.cond(jnp.max(tied.astype(I32)) > 0, with_ties, without_ties)

    qs_ref[...] = (qd_ref[...].astype(F32) * (HEAD_DIM ** -0.5 * LOG2E)).astype(BF16)
    m_ref[...] = jnp.full(m_ref.shape, NEG_BIG, F32)
    l_ref[...] = jnp.zeros(l_ref.shape, F32)
    acc_ref[...] = jnp.zeros(acc_ref.shape, F32)
    ones = jnp.ones((2 * SUBLANES, tq), BF16)

    heads = [slice(h * HEAD_DIM, (h + 1) * HEAD_DIM) for h in range(N_HEADS)]

    def att_tile(j, near):
        start = pl.multiple_of(j * tq, tq)
        mask = sc_ref[j]
        tile_max = []
        for h, hs in enumerate(heads):
            lg = lax.dot_general(kd_ref[pl.ds(start, tq), hs], qs_ref[:, hs], _NT,
                                 preferred_element_type=F32)
            if near is not None:
                lg = lg + bias_ref[h, near * tq:(near + 1) * tq, :]
            lg = lg + mask
            lg_ref[h] = lg
            tile_max.append(jnp.max(lg, axis=0, keepdims=True))
        m_old = m_ref[...]
        m_new = jnp.maximum(m_old, jnp.concatenate(tile_max, axis=0))
        alpha = jnp.exp2(m_old - m_new)
        m_ref[...] = m_new
        for h in range(N_HEADS):
            p_ref[h] = jnp.exp2(lg_ref[h] - m_new[h:h + 1, :]).astype(BF16)
        denom = []
        for h, hs in enumerate(heads):
            v_ext = jnp.concatenate([vt_ref[j, hs, :], ones], axis=0)
            pv = jnp.dot(v_ext, p_ref[h], preferred_element_type=F32)
            acc_ref[h] = alpha[h:h + 1, :] * acc_ref[h] + pv[:HEAD_DIM]
            denom.append(pv[HEAD_DIM:HEAD_DIM + 1])
        l_ref[...] = alpha * l_ref[...] + jnp.concatenate(denom, axis=0)

    def far_step(j, carry):
        att_tile(j, None)
        return carry

    lax.fori_loop(0, i - 1, far_step, 0)

    @pl.when(i >= 1)
    def _():
        att_tile(i - 1, 0)

    att_tile(i, 1)
    for h in range(N_HEADS):
        o = acc_ref[h] / l_ref[h:h + 1, :]
        o_ref[:, h * HEAD_DIM:(h + 1) * HEAD_DIM] = o.T.astype(o_ref.dtype)


def _dsa_attention(main, small, bias, batch, seq, *, qd_col, kd_col, vd_col, qi_col, tq=256):
    nq = seq // tq
    width = N_HEADS * HEAD_DIM
    top = min(TOPK_MAX, seq // 4)
    v_t = main[:, vd_col * width:(vd_col + 1) * width].reshape(batch, nq, tq, width).swapaxes(2, 3)
    tri = (jnp.arange(tq)[None, :] < jnp.arange(tq)[:, None]).astype(BF16)
    resident = pl.Buffered(1)
    return pl.pallas_call(
        functools.partial(_dsa_body, tq=tq, top=top),
        grid=(batch, nq),
        in_specs=[
            pl.BlockSpec((tq, width), lambda b, i: (b * nq + i, qd_col)),
            pl.BlockSpec((tq, IDX_HEADS * LANES), lambda b, i: (b * nq + i, qi_col)),
            pl.BlockSpec((tq, LANES), lambda b, i: (b * nq + i, 1)),
            pl.BlockSpec((seq, width), lambda b, i: (b, kd_col), pipeline_mode=resident),
            pl.BlockSpec((None, nq, width, tq), lambda b, i: (b, 0, 0, 0), pipeline_mode=resident),
            pl.BlockSpec((seq, LANES), lambda b, i: (b, 0), pipeline_mode=resident),
            pl.BlockSpec((N_HEADS, 2 * tq, tq), lambda b, i: (0, 0, 0), pipeline_mode=resident),
            pl.BlockSpec((tq, tq), lambda b, i: (0, 0), pipeline_mode=resident),
        ],
        out_specs=pl.BlockSpec((tq, width), lambda b, i: (b * nq + i, 0)),
        out_shape=jax.ShapeDtypeStruct((batch * seq, width), BF16),
        scratch_shapes=[
            pltpu.VMEM((nq, tq, tq), F32),
            pltpu.VMEM((tq, width), BF16),
            pltpu.VMEM((N_HEADS, tq), F32),
            pltpu.VMEM((N_HEADS, tq), F32),
            pltpu.VMEM((N_HEADS, HEAD_DIM, tq), F32),
            pltpu.VMEM((N_HEADS, tq, tq), F32),
            pltpu.VMEM((N_HEADS, tq, tq), BF16),
        ],
        compiler_params=_params("parallel", "arbitrary"),
        name="dsa_attention",
    )(main, main, small, main, v_t, small, bias, tri)


def _merge_body(osb_ref, ods_ref, wsb_ref, wds_ref, gsb_ref, gds_ref, bsb_ref, bds_ref, o_ref):
    p_sb = jnp.dot(osb_ref[...], wsb_ref[...], preferred_element_type=F32)
    p_ds = jnp.dot(ods_ref[...], wds_ref[...], preferred_element_type=F32)
    g_sb = jax.nn.sigmoid(gsb_ref[...] + bsb_ref[...])
    g_ds = jax.nn.sigmoid(gds_ref[...] + bds_ref[...])
    o_ref[...] = (g_sb * p_sb + g_ds * p_ds).astype(o_ref.dtype)


def _merge(o_sb, o_ds, w_sb, w_ds, gates, b_gate, *, tm=512, tn=1024):
    m, k = o_sb.shape
    d = w_sb.shape[1]
    tm, tn = min(tm, m), min(tn, d)
    nd = d // tn
    b_gate = b_gate.reshape(1, 2 * d).astype(F32)
    return pl.pallas_call(
        _merge_body,
        grid=(m // tm, nd),
        in_specs=[
            pl.BlockSpec((tm, k), lambda i, j: (i, 0)),
            pl.BlockSpec((tm, k), lambda i, j: (i, 0)),
            pl.BlockSpec((k, tn), lambda i, j: (0, j)),
            pl.BlockSpec((k, tn), lambda i, j: (0, j)),
            pl.BlockSpec((tm, tn), lambda i, j: (i, j)),
            pl.BlockSpec((tm, tn), lambda i, j: (i, nd + j)),
            pl.BlockSpec((1, tn), lambda i, j: (0, j)),
            pl.BlockSpec((1, tn), lambda i, j: (0, nd + j)),
        ],
        out_specs=pl.BlockSpec((tm, tn), lambda i, j: (i, j)),
        out_shape=jax.ShapeDtypeStruct((m, d), BF16),
        compiler_params=_params("parallel", "arbitrary"),
        name="gated_merge",
    )(o_sb, o_ds, w_sb, w_ds, gates, gates, b_gate, b_gate)


def _cross_body(x_ref, g_ref, wq_ref, km_ref, vm_ref, wo_ref, o_ref):
    x = x_ref[...]
    h = _rms(x, g_ref[...]).astype(BF16)
    q = jnp.dot(h, wq_ref[...], preferred_element_type=F32) * HEAD_DIM ** -0.5
    q = q.astype(BF16)
    outs = []
    for hh in range(MEM_HEADS):
        hs = slice(hh * HEAD_DIM, (hh + 1) * HEAD_DIM)
        lg = lax.dot_general(q[:, hs], km_ref[:, hs], _NT, preferred_element_type=F32)
        p = jnp.exp(lg - jnp.max(lg, axis=1, keepdims=True))
        o = jnp.dot(p.astype(BF16), vm_ref[:, hs], preferred_element_type=F32)
        outs.append((o / jnp.sum(p, axis=1, keepdims=True)).astype(BF16))
    o = jnp.concatenate(outs, axis=1)
    o_ref[...] = x + jnp.dot(o, wo_ref[...], preferred_element_type=F32)


def _cross_attention(x, kv, g_cross, w_cq, w_co, batch, seq, *, tm=512):
    m, d = x.shape
    n_mem = kv.shape[0] // batch
    width = MEM_HEADS * HEAD_DIM
    tm = min(tm, seq)
    nt = seq // tm
    return pl.pallas_call(
        _cross_body,
        grid=(batch, nt),
        in_specs=[
            pl.BlockSpec((tm, d), lambda b, i: (b * nt + i, 0)),
            pl.BlockSpec((1, d), lambda b, i: (0, 0)),
            pl.BlockSpec((d, width), lambda b, i: (0, 0)),
            pl.BlockSpec((n_mem, width), lambda b, i: (b, 0)),
            pl.BlockSpec((n_mem, width), lambda b, i: (b, 1)),
            pl.BlockSpec((width, d), lambda b, i: (0, 0)),
        ],
        out_specs=pl.BlockSpec((tm, d), lambda b, i: (b * nt + i, 0)),
        out_shape=jax.ShapeDtypeStruct((m, d), F32),
        compiler_params=_params("parallel", "parallel"),
        name="cross_attention",
    )(x, g_cross.reshape(1, d).astype(F32), w_cq, kv, kv, w_co)


def _delayed(u, tail, shift):
    rolled = pltpu.roll(u, shift, axis=0)
    row = lax.broadcasted_iota(I32, tail.shape, 0)
    head = jnp.where(row < shift, pltpu.roll(tail, shift, axis=0), rolled[:SUBLANES])
    return jnp.concatenate([head, rolled[SUBLANES:]], axis=0)


def _ffn_up_body(x_ref, g_ref, wa_ref, wv_ref, cwa_ref, cwv_ref, cba_ref, cbv_ref, o_ref,
                 h_ref, halo_ref, *, tiles_per_seq):
    i, j = pl.program_id(0), pl.program_id(1)

    @pl.when(j == 0)
    def _():
        h_ref[...] = _rms(x_ref[...], g_ref[...]).astype(BF16)

    h = h_ref[...]
    tm = h.shape[0]
    sequence_start = i % tiles_per_seq == 0

    def conv(w_ref, cw_ref, cb_ref, slot):
        u = jnp.dot(h, w_ref[...], preferred_element_type=F32)
        tail = jnp.where(sequence_start, 0.0, halo_ref[slot, j])
        halo_ref[slot, j] = u[tm - SUBLANES:, :]
        c = cb_ref[...] + cw_ref[CONV_WIDTH - 1:CONV_WIDTH, :] * u
        for tap in range(CONV_WIDTH - 1):
            c = c + cw_ref[tap:tap + 1, :] * _delayed(u, tail, CONV_WIDTH - 1 - tap)
        return c

    a = conv(wa_ref, cwa_ref, cba_ref, 0)
    val = conv(wv_ref, cwv_ref, cbv_ref, 1)
    o_ref[...] = (jax.nn.gelu(a) * val).astype(o_ref.dtype)


def _ffn_up_gate(x, g_ffn, w_up, conv_w, conv_b, seq, *, tm=512, tn=512):
    m, d = x.shape
    two_ff = w_up.shape[1]
    d_ff = two_ff // 2
    tm, tn = min(tm, seq), min(tn, d_ff)
    assert seq % tm == 0 and d_ff % tn == 0 and tm >= SUBLANES >= CONV_WIDTH - 1
    nf = d_ff // tn
    conv_w = conv_w.astype(F32)
    conv_b = conv_b.reshape(1, two_ff).astype(F32)
    return pl.pallas_call(
        functools.partial(_ffn_up_body, tiles_per_seq=seq // tm),
        grid=(m // tm, nf),
        in_specs=[
            pl.BlockSpec((tm, d), lambda i, j: (i, 0)),
            pl.BlockSpec((1, d), lambda i, j: (0, 0)),
            pl.BlockSpec((d, tn), lambda i, j: (0, j)),
            pl.BlockSpec((d, tn), lambda i, j: (0, nf + j)),
            pl.BlockSpec((CONV_WIDTH, tn), lambda i, j: (0, j)),
            pl.BlockSpec((CONV_WIDTH, tn), lambda i, j: (0, nf + j)),
            pl.BlockSpec((1, tn), lambda i, j: (0, j)),
            pl.BlockSpec((1, tn), lambda i, j: (0, nf + j)),
        ],
        out_specs=pl.BlockSpec((tm, tn), lambda i, j: (i, j)),
        out_shape=jax.ShapeDtypeStruct((m, d_ff), BF16),
        scratch_shapes=[pltpu.VMEM((tm, d), BF16), pltpu.VMEM((2, nf, SUBLANES, tn), F32)],
        compiler_params=_params("arbitrary", "arbitrary"),
        name="ffn_up_conv_gate",
    )(x, g_ffn.reshape(1, d).astype(F32), w_up, w_up, conv_w, conv_w, conv_b, conv_b)


def _norm_body(x_ref, g_ref, o_ref):
    o_ref[...] = _rms(x_ref[...], g_ref[...]).astype(o_ref.dtype)


def _rmsnorm(x, g, *, tm=512):
    m, d = x.shape
    tm = min(tm, m)
    return pl.pallas_call(
        _norm_body,
        grid=(m // tm,),
        in_specs=[pl.BlockSpec((tm, d), lambda i: (i, 0)), pl.BlockSpec((1, d), lambda i: (0, 0))],
        out_specs=pl.BlockSpec((tm, d), lambda i: (i, 0)),
        out_shape=jax.ShapeDtypeStruct((m, d), F32),
        compiler_params=_params("parallel"),
        name="final_norm",
    )(x, g.reshape(1, d).astype(F32))


def _layer(x, mem, g_mix, w_in, b_gate, w_proj_sb, w_proj_dsa, w_out, rel_bias,
           g_cross, g_mem, w_cq, w_ckv, w_co, g_ffn, w_up, conv_w, conv_b, w_down, batch, seq):
    d = x.shape[1]
    width = N_HEADS * HEAD_DIM
    idx_w = IDX_HEADS * IDX_DIM
    o_qi = 6 * width
    o_ki = o_qi + idx_w
    o_wi = o_ki + IDX_DIM
    o_g = o_wi + IDX_HEADS

    w_qi = jnp.pad(w_in[:, o_qi:o_ki].reshape(d, IDX_HEADS, IDX_DIM),
                   ((0, 0), (0, 0), (0, LANES - IDX_DIM))).reshape(d, IDX_HEADS * LANES)
    w_main = jnp.concatenate([w_in[:, :o_qi], w_qi], axis=1).astype(BF16)
    w_small = jnp.concatenate([
        jnp.pad(w_in[:, o_ki:o_wi], ((0, 0), (0, LANES - IDX_DIM))),
        jnp.pad(w_in[:, o_wi:o_g], ((0, 0), (0, LANES - IDX_HEADS)))], axis=1).astype(BF16)
    w_gates = w_in[:, o_g:].astype(BF16)

    main = _matmul(x, w_main, name="in_proj_main", gain=g_mix, out_dtype=BF16, tm=512, tn=1024)
    small = _matmul(x, w_small, name="in_proj_index", gain=g_mix, out_dtype=F32, tm=512, tn=2 * LANES)
    gates = _matmul(x, w_gates, name="in_proj_gates", gain=g_mix, out_dtype=F32, tm=512, tn=1024)

    blocks = width // HEAD_DIM
    o_sb = _sb_attention(main, batch, seq, q_col=0, k_col=blocks, v_col=2 * blocks)
    tq = 256
    bias = _near_bias(rel_bias, tq)
    o_ds = _dsa_attention(main, small, bias, batch, seq, qd_col=3, kd_col=4, vd_col=5, qi_col=3, tq=tq)

    merged = _merge(o_sb, o_ds, w_proj_sb.astype(BF16), w_proj_dsa.astype(BF16), gates, b_gate)
    x = _matmul(merged, w_out.astype(BF16), name="out_proj", resid=x, out_dtype=F32, tm=512, tn=1024)

    kv = _matmul(mem, w_ckv.astype(BF16), name="mem_kv_proj", gain=g_mem, out_dtype=BF16, tm=512, tn=1024)
    x = _cross_attention(x, kv, g_cross, w_cq.astype(BF16), w_co.astype(BF16), batch, seq)

    act = _ffn_up_gate(x, g_ffn, w_up.astype(BF16), conv_w, conv_b, seq)
    return _matmul(act, w_down.astype(BF16), name="ffn_down", resid=x, out_dtype=F32, tm=512, tn=512)


def kernel(x, mem, g_mix, w_in, b_gate, w_proj_sb, w_proj_dsa, w_out, rel_bias, g_cross, g_mem,
           w_cq, w_ckv, w_co, g_ffn, w_up, conv_w, conv_b, w_down, g_final):
    batch, seq, d = x.shape
    h = x.reshape(batch * seq, d)
    mem2 = mem.reshape(batch * mem.shape[1], d)
    for l in range(g_mix.shape[0]):
        h = _layer(h, mem2, g_mix[l], w_in[l], b_gate[l], w_proj_sb[l], w_proj_dsa[l], w_out[l],
                   rel_bias, g_cross[l], g_mem[l], w_cq[l], w_ckv[l], w_co[l], g_ffn[l], w_up[l],
                   conv_w[l], conv_b[l], w_down[l], batch, seq)
    return _rmsnorm(h, g_final).reshape(batch, seq, d)
```

```python
import functools

import jax
import jax.numpy as jnp
from jax import lax
from jax.experimental import pallas as pl
from jax.experimental.pallas import tpu as pltpu

F32, BF16, I32 = jnp.float32, jnp.bfloat16, jnp.int32

EPS = 1e-6
HEAD_DIM = 128
N_HEADS = 8
IDX_HEADS = 16
IDX_DIM = 64
CHUNK = 64
TOPK_MAX = 256
N_BUCKETS = 32
MAX_DISTANCE = 128
MEM_HEADS = 4
CONV_WIDTH = 3

LANES = 128
SUBLANES = 8
VMEM_LIMIT_BYTES = 56 * 1024 * 1024
NEG_BIG = -1e30
EXP2_UNDERFLOW = -151.0
LOG2E = 1.4426950408889634
_FLOAT_KEY_LOWEST = -(2 ** 31) + 2 ** 23
_VALUE_STEPS = 3
_STEPS_PER_CHECK = 4
_MAX_SEARCH_STEPS = 48

_NT = (((1,), (1,)), ((), ()))


def _params(*sem):
    return pltpu.CompilerParams(dimension_semantics=sem, vmem_limit_bytes=VMEM_LIMIT_BYTES)


def _rms(x, g):
    inv = lax.rsqrt(jnp.mean(x * x, axis=-1, keepdims=True) + EPS)
    return x * inv * g


def _mm_body(*refs, has_gain, has_resid):
    it = iter(refs)
    a_ref = next(it)
    g_ref = next(it) if has_gain else None
    w_ref = next(it)
    r_ref = next(it) if has_resid else None
    o_ref = next(it)
    h_ref = next(it) if has_gain else None
    if has_gain:
        @pl.when(pl.program_id(1) == 0)
        def _():
            h_ref[...] = _rms(a_ref[...].astype(F32), g_ref[...]).astype(BF16)
        a = h_ref[...]
    else:
        a = a_ref[...]
    acc = jnp.dot(a, w_ref[...], preferred_element_type=F32)
    if has_resid:
        acc = acc + r_ref[...]
    o_ref[...] = acc.astype(o_ref.dtype)


def _matmul(a, w, *, name, gain=None, resid=None, out_dtype, tm, tn):
    m, k = a.shape
    n = w.shape[1]
    tm, tn = min(tm, m), min(tn, n)
    assert m % tm == 0 and n % tn == 0, (m, n, tm, tn)
    in_specs = [pl.BlockSpec((tm, k), lambda i, j: (i, 0))]
    args = [a]
    scratch = []
    if gain is not None:
        in_specs.append(pl.BlockSpec((1, k), lambda i, j: (0, 0)))
        args.append(gain.reshape(1, k).astype(F32))
        scratch.append(pltpu.VMEM((tm, k), BF16))
    in_specs.append(pl.BlockSpec((k, tn), lambda i, j: (0, j)))
    args.append(w)
    if resid is not None:
        in_specs.append(pl.BlockSpec((tm, tn), lambda i, j: (i, j)))
        args.append(resid)
    return pl.pallas_call(
        functools.partial(_mm_body, has_gain=gain is not None, has_resid=resid is not None),
        grid=(m // tm, n // tn),
        in_specs=in_specs,
        out_specs=pl.BlockSpec((tm, tn), lambda i, j: (i, j)),
        out_shape=jax.ShapeDtypeStruct((m, n), out_dtype),
        scratch_shapes=scratch,
        compiler_params=_params("parallel", "arbitrary"),
        name=name,
    )(*args)


def _values_by_key_block(arr, col, batch, seq, tq):
    width = N_HEADS * HEAD_DIM
    return arr[:, col * width:(col + 1) * width].reshape(batch, seq // tq, tq, width).swapaxes(2, 3)


def _sb_body(q_ref, k_ref, vt_ref, tri_ref, o_ref, qs_ref, z_ref, lb_ref, wb_ref, acc_ref, *, tq):
    i = pl.program_id(1)
    shape = (tq, tq)
    before = lax.broadcasted_iota(I32, shape, 0) < lax.broadcasted_iota(I32, shape, 1)
    heads = [slice(h * HEAD_DIM, (h + 1) * HEAD_DIM) for h in range(N_HEADS)]
    qs_ref[...] = (q_ref[...].astype(F32) * (HEAD_DIM ** -0.5 * LOG2E)).astype(BF16)
    acc_ref[...] = jnp.zeros(acc_ref.shape, F32)

    def tile(j, carry, diagonal):
        start = pl.multiple_of(j * tq, tq)
        for h, hs in enumerate(heads):
            z = lax.dot_general(k_ref[pl.ds(start, tq), hs], qs_ref[:, hs], _NT,
                                preferred_element_type=F32)
            log_keep = jnp.minimum(-z, 0.0) - jnp.log2(1.0 + jnp.exp2(-jnp.abs(z)))
            if diagonal:
                log_keep = jnp.where(before, log_keep, 0.0)
            z_ref[h] = z
            lb_ref[h] = log_keep.astype(BF16)
        new_carry = []
        for h in range(N_HEADS):
            c = jnp.dot(tri_ref[...], lb_ref[h], preferred_element_type=F32) + carry[h:h + 1, :]
            w = jnp.exp2(z_ref[h] + c)
            if diagonal:
                w = jnp.where(before, w, 0.0)
            wb_ref[h] = w.astype(BF16)
            new_carry.append(c[0:1, :])
        for h, hs in enumerate(heads):
            acc_ref[h] += jnp.dot(vt_ref[j, hs, :], wb_ref[h], preferred_element_type=F32)
        return jnp.concatenate(new_carry, axis=0)

    carry = tile(i, jnp.zeros((N_HEADS, tq), F32), True)

    def cond(state):
        j, live, _ = state
        return jnp.logical_and(j >= 0, live > EXP2_UNDERFLOW)

    def body(state):
        j, _, carry = state
        carry = tile(j, carry, False)
        return j - 1, jnp.max(carry), carry

    lax.while_loop(cond, body, (i - 1, jnp.max(carry), carry))
    for h, hs in enumerate(heads):
        o_ref[:, hs] = acc_ref[h].T.astype(o_ref.dtype)


def _sb_attention(qkv, batch, seq, *, q_col, k_col, v_col, tq=256):
    nq = seq // tq
    width = N_HEADS * HEAD_DIM
    v_t = _values_by_key_block(qkv, v_col, batch, seq, tq)
    tri = (jnp.arange(tq)[None, :] >= jnp.arange(tq)[:, None]).astype(BF16)
    resident = pl.Buffered(1)
    return pl.pallas_call(
        functools.partial(_sb_body, tq=tq),
        grid=(batch, nq),
        in_specs=[
            pl.BlockSpec((tq, width), lambda b, i: (b * nq + i, q_col)),
            pl.BlockSpec((seq, width), lambda b, i: (b, k_col), pipeline_mode=resident),
            pl.BlockSpec((None, nq, width, tq), lambda b, i: (b, 0, 0, 0), pipeline_mode=resident),
            pl.BlockSpec((tq, tq), lambda b, i: (0, 0), pipeline_mode=resident),
        ],
        out_specs=pl.BlockSpec((tq, width), lambda b, i: (b * nq + i, 0)),
        out_shape=jax.ShapeDtypeStruct((batch * seq, width), BF16),
        scratch_shapes=[
            pltpu.VMEM((tq, width), BF16),
            pltpu.VMEM((N_HEADS, tq, tq), F32),
            pltpu.VMEM((N_HEADS, tq, tq), BF16),
            pltpu.VMEM((N_HEADS, tq, tq), BF16),
            pltpu.VMEM((N_HEADS, HEAD_DIM, tq), F32),
        ],
        compiler_params=_params("parallel", "arbitrary"),
        name="sb_attention",
    )(qkv, qkv, v_t, tri)


def _bucket_thresholds():
    nb = N_BUCKETS // 2
    max_exact = nb // 2
    span = nb - max_exact
    out = []
    for k in range(1, span):
        n = max_exact
        while n ** span * max_exact ** k < MAX_DISTANCE ** k * max_exact ** span:
            n += 1
        out.append(n)
    return max_exact, out


def _bias_body(rb_ref, o_ref, *, tq):
    nb = N_BUCKETS // 2
    max_exact, steps = _bucket_thresholds()
    shape = (2 * tq, tq)
    rel = lax.broadcasted_iota(I32, shape, 0) - lax.broadcasted_iota(I32, shape, 1) - tq
    n = jnp.abs(rel)
    large = jnp.full(shape, max_exact, I32)
    for t in steps:
        large = large + (n >= t).astype(I32)
    bucket = jnp.where(rel > 0, nb, 0) + jnp.where(n < max_exact, n, large)
    for h in range(N_HEADS):
        val = jnp.zeros(shape, F32)
        for b in range(N_BUCKETS):
            val = jnp.where(bucket == b, rb_ref[b, h], val)
        o_ref[h] = (val - rb_ref[nb - 1, h]) * LOG2E


def _near_bias(rel_bias, tq):
    return pl.pallas_call(
        functools.partial(_bias_body, tq=tq),
        in_specs=[pl.BlockSpec(memory_space=pltpu.SMEM)],
        out_specs=pl.BlockSpec(memory_space=pltpu.VMEM),
        out_shape=jax.ShapeDtypeStruct((N_HEADS, 2 * tq, tq), F32),
        compiler_params=pltpu.CompilerParams(vmem_limit_bytes=VMEM_LIMIT_BYTES),
        name="dsa_near_bias",
    )(rel_bias.astype(F32))


def _float_key(x):
    bits = lax.bitcast_convert_type(x, I32)
    return bits ^ ((bits >> 31) & 0x7FFFFFFF)


def _key_float(key):
    return lax.bitcast_convert_type(key ^ ((key >> 31) & 0x7FFFFFFF), F32)


def _dsa_body(qd_ref, qi_ref, wq_ref, kd_ref, vt_ref, ki_ref, bias_ref, tri_ref, o_ref,
              sc_ref, qs_ref, m_ref, l_ref, acc_ref, lg_ref, p_ref, *, tq, top):
    i = pl.program_id(1)
    shape = (tq, tq)
    key_row = lax.broadcasted_iota(I32, shape, 0)
    qry_col = lax.broadcasted_iota(I32, shape, 1)
    visible = key_row // CHUNK <= qry_col // CHUNK

    w_t = (wq_ref[...] * (IDX_DIM ** -0.5 * IDX_HEADS ** -0.5)).T

    def score_tile(j, lo, hi, diagonal):
        start = pl.multiple_of(j * tq, tq)
        ki = ki_ref[pl.ds(start, tq), :].astype(BF16)
        s = jnp.zeros(shape, F32)
        for h in range(IDX_HEADS):
            d = lax.dot_general(ki, qi_ref[:, h * LANES:(h + 1) * LANES], _NT,
                                preferred_element_type=F32)
            s = s + w_t[h:h + 1, :] * jnp.maximum(d, 0.0)
        if diagonal:
            lo = jnp.minimum(lo, jnp.min(jnp.where(visible, s, jnp.inf), axis=0, keepdims=True))
            s = jnp.where(visible, s, -jnp.inf)
        else:
            lo = jnp.minimum(lo, jnp.min(s, axis=0, keepdims=True))
        hi = jnp.maximum(hi, jnp.max(s, axis=0, keepdims=True))
        sc_ref[j] = s
        return lo, hi

    lo, hi = lax.fori_loop(
        0, i, lambda j, c: score_tile(j, c[0], c[1], False),
        (jnp.full((1, tq), jnp.inf, F32), jnp.full((1, tq), -jnp.inf, F32)))
    lo, hi = score_tile(i, lo, hi, True)

    def count_ge(t):
        def step(j, acc):
            hit = (sc_ref[j] >= t).astype(I32)
            for g in range(tq // SUBLANES):
                acc = acc + hit[g * SUBLANES:(g + 1) * SUBLANES, :]
            return acc
        acc = lax.fori_loop(0, i + 1, step, jnp.zeros((SUBLANES, tq), I32))
        return jnp.sum(acc, axis=0, keepdims=True)

    qry = lax.broadcasted_iota(I32, (1, tq), 1)
    n_visible = i * tq + (qry // CHUNK + 1) * CHUNK
    k_lo0 = _float_key(lo)
    k_hi0 = _float_key(hi) + 1
    wanted = n_visible > top
    adjacent0 = k_lo0 + 1 == k_hi0
    select_all = jnp.full((1, tq), _FLOAT_KEY_LOWEST, I32)

    def search_cond(state):
        it, active = state[0], state[1]
        return jnp.logical_and(it < _MAX_SEARCH_STEPS, jnp.max(active) > 0)

    def search_step(it, active, k_lo, k_hi, c_hi, k_thr, tied):
        by_value = _float_key(0.5 * _key_float(k_lo) + 0.5 * _key_float(k_hi))
        by_bits = (k_lo & k_hi) + ((k_lo ^ k_hi) >> 1)
        k_t = jnp.where(it < _VALUE_STEPS, by_value, by_bits)
        k_t = jnp.minimum(jnp.maximum(k_t, k_lo + 1), k_hi - 1)
        c = count_ge(_key_float(k_t))
        found = jnp.logical_and(active, c == top)
        above = jnp.logical_and(active, c > top)
        below = jnp.logical_and(active, c < top)
        k_lo = jnp.where(above, k_t, k_lo)
        k_hi, c_hi = jnp.where(below, k_t, k_hi), jnp.where(below, c, c_hi)
        closed = jnp.logical_and(jnp.logical_or(above, below), k_lo + 1 == k_hi)
        k_thr = jnp.where(found, k_t, jnp.where(closed, k_lo, k_thr))
        tied = jnp.logical_or(tied, closed)
        active = jnp.logical_and(active, jnp.logical_not(jnp.logical_or(found, closed)))
        return active, k_lo, k_hi, c_hi, k_thr, tied

    def search_body(state):
        it, active, k_lo, k_hi, c_hi, k_thr, tied = state
        active, tied = active > 0, tied > 0
        for _ in range(_STEPS_PER_CHECK):
            active, k_lo, k_hi, c_hi, k_thr, tied = search_step(
                it, active, k_lo, k_hi, c_hi, k_thr, tied)
            it = it + 1
        return it, active.astype(I32), k_lo, k_hi, c_hi, k_thr, tied.astype(I32)

    tied0 = jnp.logical_and(wanted, adjacent0)
    state = lax.while_loop(search_cond, search_body, (
        jnp.int32(0), jnp.logical_and(wanted, jnp.logical_not(adjacent0)).astype(I32),
        k_lo0, k_hi0, jnp.zeros((1, tq), I32),
        jnp.where(tied0, k_lo0, select_all), tied0.astype(I32)))
    _, _, _, _, c_hi, k_thr, tied = state
    tied = tied > 0
    thr = _key_float(k_thr)

    def plain_mask(j, carry):
        sc_ref[j] = jnp.where(sc_ref[j] >= thr, 0.0, NEG_BIG)
        return carry

    def tie_mask(j, seen):
        s = sc_ref[j]
        equal = s == thr
        rank = jnp.dot(tri_ref[...], equal.astype(BF16), preferred_element_type=F32) + seen
        quota = jnp.where(tied, (top - c_hi).astype(F32), jnp.inf)
        keep_equal = jnp.where(rank < quota, 0.0, NEG_BIG)
        sc_ref[j] = jnp.where(s > thr, 0.0, jnp.where(equal, keep_equal, NEG_BIG))
        return seen + jnp.sum(equal.astype(F32), axis=0, keepdims=True)

    def with_ties():
        lax.fori_loop(0, i + 1, tie_mask, jnp.zeros((1, tq), F32))
        return jnp.int32(0)

    def without_ties():
        return lax.fori_loop(0, i + 1, plain_mask, jnp.int32(0))

    lax.cond(jnp.max(tied.astype(I32)) > 0, with_ties, without_ties)

    qs_ref[...] = (qd_ref[...].astype(F32) * (HEAD_DIM ** -0.5 * LOG2E)).astype(BF16)
    m_ref[...] = jnp.full(m_ref.shape, NEG_BIG, F32)
    l_ref[...] = jnp.zeros(l_ref.shape, F32)
    acc_ref[...] = jnp.zeros(acc_ref.shape, F32)
    ones = jnp.ones((2 * SUBLANES, tq), BF16)

    heads = [slice(h * HEAD_DIM, (h + 1) * HEAD_DIM) for h in range(N_HEADS)]

    def att_tile(j, near):
        start = pl.multiple_of(j * tq, tq)
        mask = sc_ref[j]
        tile_max = []
        for h, hs in enumerate(heads):
            lg = lax.dot_general(kd_ref[pl.ds(start, tq), hs], qs_ref[:, hs], _NT,
                                 preferred_element_type=F32)
            if near is not None:
                lg = lg + bias_ref[h, near * tq:(near + 1) * tq, :]
            lg = lg + mask
            lg_ref[h] = lg
            tile_max.append(jnp.max(lg, axis=0, keepdims=True))
        m_old = m_ref[...]
        m_new = jnp.maximum(m_old, jnp.concatenate(tile_max, axis=0))
        alpha = jnp.exp2(m_old - m_new)
        m_ref[...] = m_new
        for h in range(N_HEADS):
            p_ref[h] = jnp.exp2(lg_ref[h] - m_new[h:h + 1, :]).astype(BF16)
        denom = []
        for h, hs in enumerate(heads):
            v_ext = jnp.concatenate([vt_ref[j, hs, :], ones], axis=0)
            pv = jnp.dot(v_ext, p_ref[h], preferred_element_type=F32)
            acc_ref[h] = alpha[h:h + 1, :] * acc_ref[h] + pv[:HEAD_DIM]
            denom.append(pv[HEAD_DIM:HEAD_DIM + 1])
        l_ref[...] = alpha * l_ref[...] + jnp.concatenate(denom, axis=0)

    def far_step(j, carry):
        att_tile(j, None)
        return carry

    lax.fori_loop(0, i - 1, far_step, 0)

    @pl.when(i >= 1)
    def _():
        att_tile(i - 1, 0)

    att_tile(i, 1)
    for h in range(N_HEADS):
        o = acc_ref[h] / l_ref[h:h + 1, :]
        o_ref[:, h * HEAD_DIM:(h + 1) * HEAD_DIM] = o.T.astype(o_ref.dtype)


def _dsa_attention(main, small, bias, batch, seq, *, qd_col, kd_col, vd_col, qi_col, tq=256):
    nq = seq // tq
    width = N_HEADS * HEAD_DIM
    top = min(TOPK_MAX, seq // 4)
    v_t = _values_by_key_block(main, vd_col, batch, seq, tq)
    tri = (jnp.arange(tq)[None, :] < jnp.arange(tq)[:, None]).astype(BF16)
    resident = pl.Buffered(1)
    return pl.pallas_call(
        functools.partial(_dsa_body, tq=tq, top=top),
        grid=(batch, nq),
        in_specs=[
            pl.BlockSpec((tq, width), lambda b, i: (b * nq + i, qd_col)),
            pl.BlockSpec((tq, IDX_HEADS * LANES), lambda b, i: (b * nq + i, qi_col)),
            pl.BlockSpec((tq, LANES), lambda b, i: (b * nq + i, 1)),
            pl.BlockSpec((seq, width), lambda b, i: (b, kd_col), pipeline_mode=resident),
            pl.BlockSpec((None, nq, width, tq), lambda b, i: (b, 0, 0, 0), pipeline_mode=resident),
            pl.BlockSpec((seq, LANES), lambda b, i: (b, 0), pipeline_mode=resident),
            pl.BlockSpec((N_HEADS, 2 * tq, tq), lambda b, i: (0, 0, 0), pipeline_mode=resident),
            pl.BlockSpec((tq, tq), lambda b, i: (0, 0), pipeline_mode=resident),
        ],
        out_specs=pl.BlockSpec((tq, width), lambda b, i: (b * nq + i, 0)),
        out_shape=jax.ShapeDtypeStruct((batch * seq, width), BF16),
        scratch_shapes=[
            pltpu.VMEM((nq, tq, tq), F32),
            pltpu.VMEM((tq, width), BF16),
            pltpu.VMEM((N_HEADS, tq), F32),
            pltpu.VMEM((N_HEADS, tq), F32),
            pltpu.VMEM((N_HEADS, HEAD_DIM, tq), F32),
            pltpu.VMEM((N_HEADS, tq, tq), F32),
            pltpu.VMEM((N_HEADS, tq, tq), BF16),
        ],
        compiler_params=_params("parallel", "arbitrary"),
        name="dsa_attention",
    )(main, main, small, main, v_t, small, bias, tri)


def _merge_body(osb_ref, ods_ref, wsb_ref, wds_ref, gsb_ref, gds_ref, bsb_ref, bds_ref, o_ref):
    p_sb = jnp.dot(osb_ref[...], wsb_ref[...], preferred_element_type=F32)
    p_ds = jnp.dot(ods_ref[...], wds_ref[...], preferred_element_type=F32)
    g_sb = jax.nn.sigmoid(gsb_ref[...] + bsb_ref[...])
    g_ds = jax.nn.sigmoid(gds_ref[...] + bds_ref[...])
    o_ref[...] = (g_sb * p_sb + g_ds * p_ds).astype(o_ref.dtype)


def _merge(o_sb, o_ds, w_sb, w_ds, gates, b_gate, *, tm=512, tn=1024):
    m, k = o_sb.shape
    d = w_sb.shape[1]
    tm, tn = min(tm, m), min(tn, d)
    nd = d // tn
    b_gate = b_gate.reshape(1, 2 * d).astype(F32)
    return pl.pallas_call(
        _merge_body,
        grid=(m // tm, nd),
        in_specs=[
            pl.BlockSpec((tm, k), lambda i, j: (i, 0)),
            pl.BlockSpec((tm, k), lambda i, j: (i, 0)),
            pl.BlockSpec((k, tn), lambda i, j: (0, j)),
            pl.BlockSpec((k, tn), lambda i, j: (0, j)),
            pl.BlockSpec((tm, tn), lambda i, j: (i, j)),
            pl.BlockSpec((tm, tn), lambda i, j: (i, nd + j)),
            pl.BlockSpec((1, tn), lambda i, j: (0, j)),
            pl.BlockSpec((1, tn), lambda i, j: (0, nd + j)),
        ],
        out_specs=pl.BlockSpec((tm, tn), lambda i, j: (i, j)),
        out_shape=jax.ShapeDtypeStruct((m, d), BF16),
        compiler_params=_params("parallel", "arbitrary"),
        name="gated_merge",
    )(o_sb, o_ds, w_sb, w_ds, gates, gates, b_gate, b_gate)


def _cross_body(x_ref, g_ref, wq_ref, km_ref, vm_ref, wo_ref, o_ref):
    x = x_ref[...]
    h = _rms(x, g_ref[...]).astype(BF16)
    q = jnp.dot(h, wq_ref[...], preferred_element_type=F32) * HEAD_DIM ** -0.5
    q = q.astype(BF16)
    outs = []
    for hh in range(MEM_HEADS):
        hs = slice(hh * HEAD_DIM, (hh + 1) * HEAD_DIM)
        lg = lax.dot_general(q[:, hs], km_ref[:, hs], _NT, preferred_element_type=F32)
        p = jnp.exp(lg - jnp.max(lg, axis=1, keepdims=True))
        o = jnp.dot(p.astype(BF16), vm_ref[:, hs], preferred_element_type=F32)
        outs.append((o / jnp.sum(p, axis=1, keepdims=True)).astype(BF16))
    o = jnp.concatenate(outs, axis=1)
    o_ref[...] = x + jnp.dot(o, wo_ref[...], preferred_element_type=F32)


def _cross_attention(x, kv, g_cross, w_cq, w_co, batch, seq, *, tm=512):
    m, d = x.shape
    n_mem = kv.shape[0] // batch
    width = MEM_HEADS * HEAD_DIM
    tm = min(tm, seq)
    nt = seq // tm
    return pl.pallas_call(
        _cross_body,
        grid=(batch, nt),
        in_specs=[
            pl.BlockSpec((tm, d), lambda b, i: (b * nt + i, 0)),
            pl.BlockSpec((1, d), lambda b, i: (0, 0)),
            pl.BlockSpec((d, width), lambda b, i: (0, 0)),
            pl.BlockSpec((n_mem, width), lambda b, i: (b, 0)),
            pl.BlockSpec((n_mem, width), lambda b, i: (b, 1)),
            pl.BlockSpec((width, d), lambda b, i: (0, 0)),
        ],
        out_specs=pl.BlockSpec((tm, d), lambda b, i: (b * nt + i, 0)),
        out_shape=jax.ShapeDtypeStruct((m, d), F32),
        compiler_params=_params("parallel", "parallel"),
        name="cross_attention",
    )(x, g_cross.reshape(1, d).astype(F32), w_cq, kv, kv, w_co)


def _delayed(u, tail, shift):
    rolled = pltpu.roll(u, shift, axis=0)
    row = lax.broadcasted_iota(I32, tail.shape, 0)
    head = jnp.where(row < shift, pltpu.roll(tail, shift, axis=0), rolled[:SUBLANES])
    return jnp.concatenate([head, rolled[SUBLANES:]], axis=0)


def _ffn_up_body(x_ref, g_ref, wa_ref, wv_ref, cwa_ref, cwv_ref, cba_ref, cbv_ref, o_ref,
                 h_ref, halo_ref, *, tiles_per_seq):
    i, j = pl.program_id(0), pl.program_id(1)

    @pl.when(j == 0)
    def _():
        h_ref[...] = _rms(x_ref[...], g_ref[...]).astype(BF16)

    h = h_ref[...]
    tm = h.shape[0]
    sequence_start = i % tiles_per_seq == 0

    def conv(w_ref, cw_ref, cb_ref, slot):
        u = jnp.dot(h, w_ref[...], preferred_element_type=F32)
        tail = jnp.where(sequence_start, 0.0, halo_ref[slot, j])
        halo_ref[slot, j] = u[tm - SUBLANES:, :]
        c = cb_ref[...] + cw_ref[CONV_WIDTH - 1:CONV_WIDTH, :] * u
        for tap in range(CONV_WIDTH - 1):
            c = c + cw_ref[tap:tap + 1, :] * _delayed(u, tail, CONV_WIDTH - 1 - tap)
        return c

    a = conv(wa_ref, cwa_ref, cba_ref, 0)
    val = conv(wv_ref, cwv_ref, cbv_ref, 1)
    o_ref[...] = (jax.nn.gelu(a) * val).astype(o_ref.dtype)


def _ffn_up_gate(x, g_ffn, w_up, conv_w, conv_b, seq, *, tm=512, tn=512):
    m, d = x.shape
    two_ff = w_up.shape[1]
    d_ff = two_ff // 2
    tm, tn = min(tm, seq), min(tn, d_ff)
    assert seq % tm == 0 and d_ff % tn == 0 and tm >= SUBLANES >= CONV_WIDTH - 1
    nf = d_ff // tn
    conv_w = conv_w.astype(F32)
    conv_b = conv_b.reshape(1, two_ff).astype(F32)
    return pl.pallas_call(
        functools.partial(_ffn_up_body, tiles_per_seq=seq // tm),
        grid=(m // tm, nf),
        in_specs=[
            pl.BlockSpec((tm, d), lambda i, j: (i, 0)),
            pl.BlockSpec((1, d), lambda i, j: (0, 0)),
            pl.BlockSpec((d, tn), lambda i, j: (0, j)),
            pl.BlockSpec((d, tn), lambda i, j: (0, nf + j)),
            pl.BlockSpec((CONV_WIDTH, tn), lambda i, j: (0, j)),
            pl.BlockSpec((CONV_WIDTH, tn), lambda i, j: (0, nf + j)),
            pl.BlockSpec((1, tn), lambda i, j: (0, j)),
            pl.BlockSpec((1, tn), lambda i, j: (0, nf + j)),
        ],
        out_specs=pl.BlockSpec((tm, tn), lambda i, j: (i, j)),
        out_shape=jax.ShapeDtypeStruct((m, d_ff), BF16),
        scratch_shapes=[pltpu.VMEM((tm, d), BF16), pltpu.VMEM((2, nf, SUBLANES, tn), F32)],
        compiler_params=_params("arbitrary", "arbitrary"),
        name="ffn_up_conv_gate",
    )(x, g_ffn.reshape(1, d).astype(F32), w_up, w_up, conv_w, conv_w, conv_b, conv_b)


def _norm_body(x_ref, g_ref, o_ref):
    o_ref[...] = _rms(x_ref[...], g_ref[...]).astype(o_ref.dtype)


def _rmsnorm(x, g, *, tm=512):
    m, d = x.shape
    tm = min(tm, m)
    return pl.pallas_call(
        _norm_body,
        grid=(m // tm,),
        in_specs=[pl.BlockSpec((tm, d), lambda i: (i, 0)), pl.BlockSpec((1, d), lambda i: (0, 0))],
        out_specs=pl.BlockSpec((tm, d), lambda i: (i, 0)),
        out_shape=jax.ShapeDtypeStruct((m, d), F32),
        compiler_params=_params("parallel"),
        name="final_norm",
    )(x, g.reshape(1, d).astype(F32))


def _layer(x, mem, g_mix, w_in, b_gate, w_proj_sb, w_proj_dsa, w_out, rel_bias,
           g_cross, g_mem, w_cq, w_ckv, w_co, g_ffn, w_up, conv_w, conv_b, w_down, batch, seq):
    d = x.shape[1]
    width = N_HEADS * HEAD_DIM
    idx_w = IDX_HEADS * IDX_DIM
    o_qi = 6 * width
    o_ki = o_qi + idx_w
    o_wi = o_ki + IDX_DIM
    o_g = o_wi + IDX_HEADS

    w_qi = jnp.pad(w_in[:, o_qi:o_ki].reshape(d, IDX_HEADS, IDX_DIM),
                   ((0, 0), (0, 0), (0, LANES - IDX_DIM))).reshape(d, IDX_HEADS * LANES)
    w_main = jnp.concatenate([w_in[:, :o_qi], w_qi], axis=1).astype(BF16)
    w_small = jnp.concatenate([
        jnp.pad(w_in[:, o_ki:o_wi], ((0, 0), (0, LANES - IDX_DIM))),
        jnp.pad(w_in[:, o_wi:o_g], ((0, 0), (0, LANES - IDX_HEADS)))], axis=1).astype(BF16)
    w_gates = w_in[:, o_g:].astype(BF16)

    main = _matmul(x, w_main, name="in_proj_main", gain=g_mix, out_dtype=BF16, tm=512, tn=1024)
    small = _matmul(x, w_small, name="in_proj_index", gain=g_mix, out_dtype=F32, tm=512, tn=2 * LANES)
    gates = _matmul(x, w_gates, name="in_proj_gates", gain=g_mix, out_dtype=F32, tm=512, tn=1024)

    o_sb = _sb_attention(main, batch, seq, q_col=0, k_col=1, v_col=2)
    tq = 256
    bias = _near_bias(rel_bias, tq)
    o_ds = _dsa_attention(main, small, bias, batch, seq, qd_col=3, kd_col=4, vd_col=5, qi_col=3, tq=tq)

    merged = _merge(o_sb, o_ds, w_proj_sb.astype(BF16), w_proj_dsa.astype(BF16), gates, b_gate)
    x = _matmul(merged, w_out.astype(BF16), name="out_proj", resid=x, out_dtype=F32, tm=512, tn=1024)

    kv = _matmul(mem, w_ckv.astype(BF16), name="mem_kv_proj", gain=g_mem, out_dtype=BF16, tm=512, tn=1024)
    x = _cross_attention(x, kv, g_cross, w_cq.astype(BF16), w_co.astype(BF16), batch, seq)

    act = _ffn_up_gate(x, g_ffn, w_up.astype(BF16), conv_w, conv_b, seq)
    return _matmul(act, w_down.astype(BF16), name="ffn_down", resid=x, out_dtype=F32, tm=512, tn=512)


def kernel(x, mem, g_mix, w_in, b_gate, w_proj_sb, w_proj_dsa, w_out, rel_bias, g_cross, g_mem,
           w_cq, w_ckv, w_co, g_ffn, w_up, conv_w, conv_b, w_down, g_final):
    batch, seq, d = x.shape
    h = x.reshape(batch * seq, d)
    mem2 = mem.reshape(batch * mem.shape[1], d)
    for l in range(g_mix.shape[0]):
        h = _layer(h, mem2, g_mix[l], w_in[l], b_gate[l], w_proj_sb[l], w_proj_dsa[l], w_out[l],
                   rel_bias, g_cross[l], g_mem[l], w_cq[l], w_ckv[l], w_co[l], g_ffn[l], w_up[l],
                   conv_w[l], conv_b[l], w_down[l], batch, seq)
    return _rmsnorm(h, g_final).reshape(batch, seq, d)
```

```python
import functools

import jax
import jax.numpy as jnp
from jax import lax
from jax.experimental import pallas as pl
from jax.experimental.pallas import tpu as pltpu

F32, BF16, I32 = jnp.float32, jnp.bfloat16, jnp.int32

EPS = 1e-6
HEAD_DIM = 128
N_HEADS = 8
IDX_HEADS = 16
IDX_DIM = 64
CHUNK = 64
TOPK_MAX = 256
N_BUCKETS = 32
MAX_DISTANCE = 128
MEM_HEADS = 4
CONV_WIDTH = 3

LANES = 128
SUBLANES = 8
VMEM_LIMIT_BYTES = 56 * 1024 * 1024
NEG_BIG = -1e30
EXP2_UNDERFLOW = -151.0
LOG2E = 1.4426950408889634
_FLOAT_KEY_LOWEST = -(2 ** 31) + 2 ** 23
_VALUE_STEPS = 3
_STEPS_PER_CHECK = 4
_MAX_SEARCH_STEPS = 48

_NT = (((1,), (1,)), ((), ()))


def _params(*sem):
    return pltpu.CompilerParams(dimension_semantics=sem, vmem_limit_bytes=VMEM_LIMIT_BYTES)


def _rms(x, g):
    inv = lax.rsqrt(jnp.mean(x * x, axis=-1, keepdims=True) + EPS)
    return x * inv * g


def _mm_body(*refs, has_gain, has_resid):
    it = iter(refs)
    a_ref = next(it)
    g_ref = next(it) if has_gain else None
    w_ref = next(it)
    r_ref = next(it) if has_resid else None
    o_ref = next(it)
    h_ref = next(it) if has_gain else None
    if has_gain:
        @pl.when(pl.program_id(1) == 0)
        def _():
            h_ref[...] = _rms(a_ref[...].astype(F32), g_ref[...]).astype(BF16)
        a = h_ref[...]
    else:
        a = a_ref[...]
    acc = jnp.dot(a, w_ref[...], preferred_element_type=F32)
    if has_resid:
        acc = acc + r_ref[...]
    o_ref[...] = acc.astype(o_ref.dtype)


def _matmul(a, w, *, name, gain=None, resid=None, out_dtype, tm, tn):
    m, k = a.shape
    n = w.shape[1]
    tm, tn = min(tm, m), min(tn, n)
    assert m % tm == 0 and n % tn == 0, (m, n, tm, tn)
    in_specs = [pl.BlockSpec((tm, k), lambda i, j: (i, 0))]
    args = [a]
    scratch = []
    if gain is not None:
        in_specs.append(pl.BlockSpec((1, k), lambda i, j: (0, 0)))
        args.append(gain.reshape(1, k).astype(F32))
        scratch.append(pltpu.VMEM((tm, k), BF16))
    in_specs.append(pl.BlockSpec((k, tn), lambda i, j: (0, j)))
    args.append(w)
    if resid is not None:
        in_specs.append(pl.BlockSpec((tm, tn), lambda i, j: (i, j)))
        args.append(resid)
    return pl.pallas_call(
        functools.partial(_mm_body, has_gain=gain is not None, has_resid=resid is not None),
        grid=(m // tm, n // tn),
        in_specs=in_specs,
        out_specs=pl.BlockSpec((tm, tn), lambda i, j: (i, j)),
        out_shape=jax.ShapeDtypeStruct((m, n), out_dtype),
        scratch_shapes=scratch,
        compiler_params=_params("parallel", "arbitrary"),
        name=name,
    )(*args)


def _values_by_key_block(arr, col, batch, seq, tq):
    width = N_HEADS * HEAD_DIM
    return arr[:, col * width:(col + 1) * width].reshape(batch, seq // tq, tq, width).swapaxes(2, 3)


def _sb_body(q_ref, k_ref, vt_ref, tri_ref, o_ref, qs_ref, z_ref, lb_ref, wb_ref, acc_ref, *, tq):
    i = pl.program_id(1)
    shape = (tq, tq)
    before = lax.broadcasted_iota(I32, shape, 0) < lax.broadcasted_iota(I32, shape, 1)
    heads = [slice(h * HEAD_DIM, (h + 1) * HEAD_DIM) for h in range(N_HEADS)]
    qs_ref[...] = (q_ref[...].astype(F32) * (HEAD_DIM ** -0.5 * LOG2E)).astype(BF16)
    acc_ref[...] = jnp.zeros(acc_ref.shape, F32)

    def tile(j, carry, diagonal):
        start = pl.multiple_of(j * tq, tq)
        for h, hs in enumerate(heads):
            z = lax.dot_general(k_ref[pl.ds(start, tq), hs], qs_ref[:, hs], _NT,
                                preferred_element_type=F32)
            log_keep = jnp.minimum(-z, 0.0) - jnp.log2(1.0 + jnp.exp2(-jnp.abs(z)))
            if diagonal:
                log_keep = jnp.where(before, log_keep, 0.0)
            z_ref[h] = z
            lb_ref[h] = log_keep.astype(BF16)
        new_carry = []
        for h in range(N_HEADS):
            c = jnp.dot(tri_ref[...], lb_ref[h], preferred_element_type=F32) + carry[h:h + 1, :]
            w = jnp.exp2(z_ref[h] + c)
            if diagonal:
                w = jnp.where(before, w, 0.0)
            wb_ref[h] = w.astype(BF16)
            new_carry.append(c[0:1, :])
        for h, hs in enumerate(heads):
            acc_ref[h] += jnp.dot(vt_ref[j, hs, :], wb_ref[h], preferred_element_type=F32)
        return jnp.concatenate(new_carry, axis=0)

    carry = tile(i, jnp.zeros((N_HEADS, tq), F32), True)

    def cond(state):
        j, live, _ = state
        return jnp.logical_and(j >= 0, live > EXP2_UNDERFLOW)

    def body(state):
        j, _, carry = state
        carry = tile(j, carry, False)
        return j - 1, jnp.max(carry), carry

    lax.while_loop(cond, body, (i - 1, jnp.max(carry), carry))
    for h, hs in enumerate(heads):
        o_ref[:, hs] = acc_ref[h].T.astype(o_ref.dtype)


def _sb_attention(qkv, batch, seq, *, q_col, k_col, v_col, tq=256):
    nq = seq // tq
    width = N_HEADS * HEAD_DIM
    v_t = _values_by_key_block(qkv, v_col, batch, seq, tq)
    tri = (jnp.arange(tq)[None, :] >= jnp.arange(tq)[:, None]).astype(BF16)
    resident = pl.Buffered(1)
    return pl.pallas_call(
        functools.partial(_sb_body, tq=tq),
        grid=(batch, nq),
        in_specs=[
            pl.BlockSpec((tq, width), lambda b, i: (b * nq + i, q_col)),
            pl.BlockSpec((seq, width), lambda b, i: (b, k_col), pipeline_mode=resident),
            pl.BlockSpec((None, nq, width, tq), lambda b, i: (b, 0, 0, 0), pipeline_mode=resident),
            pl.BlockSpec((tq, tq), lambda b, i: (0, 0), pipeline_mode=resident),
        ],
        out_specs=pl.BlockSpec((tq, width), lambda b, i: (b * nq + i, 0)),
        out_shape=jax.ShapeDtypeStruct((batch * seq, width), BF16),
        scratch_shapes=[
            pltpu.VMEM((tq, width), BF16),
            pltpu.VMEM((N_HEADS, tq, tq), F32),
            pltpu.VMEM((N_HEADS, tq, tq), BF16),
            pltpu.VMEM((N_HEADS, tq, tq), BF16),
            pltpu.VMEM((N_HEADS, HEAD_DIM, tq), F32),
        ],
        compiler_params=_params("parallel", "arbitrary"),
        name="sb_attention",
    )(qkv, qkv, v_t, tri)


def _bucket_thresholds():
    nb = N_BUCKETS // 2
    max_exact = nb // 2
    span = nb - max_exact
    out = []
    for k in range(1, span):
        n = max_exact
        while n ** span * max_exact ** k < MAX_DISTANCE ** k * max_exact ** span:
            n += 1
        out.append(n)
    return max_exact, out


def _bias_body(rb_ref, o_ref, *, tq):
    nb = N_BUCKETS // 2
    max_exact, steps = _bucket_thresholds()
    shape = (2 * tq, tq)
    rel = lax.broadcasted_iota(I32, shape, 0) - lax.broadcasted_iota(I32, shape, 1) - tq
    n = jnp.abs(rel)
    large = jnp.full(shape, max_exact, I32)
    for t in steps:
        large = large + (n >= t).astype(I32)
    bucket = jnp.where(rel > 0, nb, 0) + jnp.where(n < max_exact, n, large)
    for h in range(N_HEADS):
        val = jnp.zeros(shape, F32)
        for b in range(N_BUCKETS):
            val = jnp.where(bucket == b, rb_ref[b, h], val)
        o_ref[h] = (val - rb_ref[nb - 1, h]) * LOG2E


def _near_bias(rel_bias, tq):
    return pl.pallas_call(
        functools.partial(_bias_body, tq=tq),
        in_specs=[pl.BlockSpec(memory_space=pltpu.SMEM)],
        out_specs=pl.BlockSpec(memory_space=pltpu.VMEM),
        out_shape=jax.ShapeDtypeStruct((N_HEADS, 2 * tq, tq), F32),
        compiler_params=pltpu.CompilerParams(vmem_limit_bytes=VMEM_LIMIT_BYTES),
        name="dsa_near_bias",
    )(rel_bias.astype(F32))


def _float_key(x):
    bits = lax.bitcast_convert_type(x, I32)
    return bits ^ ((bits >> 31) & 0x7FFFFFFF)


def _key_float(key):
    return lax.bitcast_convert_type(key ^ ((key >> 31) & 0x7FFFFFFF), F32)


def _dsa_body(qd_ref, qi_ref, wq_ref, kd_ref, vt_ref, ki_ref, bias_ref, tri_ref, o_ref,
              sc_ref, qs_ref, m_ref, l_ref, acc_ref, lg_ref, p_ref, *, tq, top):
    i = pl.program_id(1)
    shape = (tq, tq)
    key_row = lax.broadcasted_iota(I32, shape, 0)
    qry_col = lax.broadcasted_iota(I32, shape, 1)
    visible = key_row // CHUNK <= qry_col // CHUNK

    w_t = (wq_ref[...] * (IDX_DIM ** -0.5 * IDX_HEADS ** -0.5)).T

    def score_tile(j, lo, hi, diagonal):
        start = pl.multiple_of(j * tq, tq)
        ki = ki_ref[pl.ds(start, tq), :].astype(BF16)
        s = jnp.zeros(shape, F32)
        for h in range(IDX_HEADS):
            d = lax.dot_general(ki, qi_ref[:, h * LANES:(h + 1) * LANES], _NT,
                                preferred_element_type=F32)
            s = s + w_t[h:h + 1, :] * jnp.maximum(d, 0.0)
        if diagonal:
            lo = jnp.minimum(lo, jnp.min(jnp.where(visible, s, jnp.inf), axis=0, keepdims=True))
            s = jnp.where(visible, s, -jnp.inf)
        else:
            lo = jnp.minimum(lo, jnp.min(s, axis=0, keepdims=True))
        hi = jnp.maximum(hi, jnp.max(s, axis=0, keepdims=True))
        sc_ref[j] = s
        return lo, hi

    lo, hi = lax.fori_loop(
        0, i, lambda j, c: score_tile(j, c[0], c[1], False),
        (jnp.full((1, tq), jnp.inf, F32), jnp.full((1, tq), -jnp.inf, F32)))
    lo, hi = score_tile(i, lo, hi, True)

    ones = jnp.ones((2 * SUBLANES, tq), BF16)

    def count_ge(t):
        def step(j, acc):
            hit = (sc_ref[j] >= t).astype(I32)
            for g in range(tq // SUBLANES):
                acc = acc + hit[g * SUBLANES:(g + 1) * SUBLANES, :]
            return acc
        acc = lax.fori_loop(0, i + 1, step, jnp.zeros((SUBLANES, tq), I32))
        return jnp.sum(acc, axis=0, keepdims=True)

    qry = lax.broadcasted_iota(I32, (1, tq), 1)
    n_visible = i * tq + (qry // CHUNK + 1) * CHUNK
    k_lo0 = _float_key(lo)
    k_hi0 = _float_key(hi) + 1
    wanted = n_visible > top
    adjacent0 = k_lo0 + 1 == k_hi0
    select_all = jnp.full((1, tq), _FLOAT_KEY_LOWEST, I32)

    def search_cond(state):
        it, active = state[0], state[1]
        return jnp.logical_and(it < _MAX_SEARCH_STEPS, jnp.max(active) > 0)

    def search_step(it, active, k_lo, k_hi, c_hi, k_thr, tied):
        by_value = _float_key(0.5 * _key_float(k_lo) + 0.5 * _key_float(k_hi))
        by_bits = (k_lo & k_hi) + ((k_lo ^ k_hi) >> 1)
        k_t = jnp.where(it < _VALUE_STEPS, by_value, by_bits)
        k_t = jnp.minimum(jnp.maximum(k_t, k_lo + 1), k_hi - 1)
        c = count_ge(_key_float(k_t))
        found = jnp.logical_and(active, c == top)
        above = jnp.logical_and(active, c > top)
        below = jnp.logical_and(active, c < top)
        k_lo = jnp.where(above, k_t, k_lo)
        k_hi, c_hi = jnp.where(below, k_t, k_hi), jnp.where(below, c, c_hi)
        closed = jnp.logical_and(jnp.logical_or(above, below), k_lo + 1 == k_hi)
        k_thr = jnp.where(found, k_t, jnp.where(closed, k_lo, k_thr))
        tied = jnp.logical_or(tied, closed)
        active = jnp.logical_and(active, jnp.logical_not(jnp.logical_or(found, closed)))
        return active, k_lo, k_hi, c_hi, k_thr, tied

    def search_body(state):
        it, active, k_lo, k_hi, c_hi, k_thr, tied = state
        active, tied = active > 0, tied > 0
        for _ in range(_STEPS_PER_CHECK):
            active, k_lo, k_hi, c_hi, k_thr, tied = search_step(
                it, active, k_lo, k_hi, c_hi, k_thr, tied)
            it = it + 1
        return it, active.astype(I32), k_lo, k_hi, c_hi, k_thr, tied.astype(I32)

    tied0 = jnp.logical_and(wanted, adjacent0)
    state = lax.while_loop(search_cond, search_body, (
        jnp.int32(0), jnp.logical_and(wanted, jnp.logical_not(adjacent0)).astype(I32),
        k_lo0, k_hi0, jnp.zeros((1, tq), I32),
        jnp.where(tied0, k_lo0, select_all), tied0.astype(I32)))
    _, _, _, _, c_hi, k_thr, tied = state
    tied = tied > 0
    thr = _key_float(k_thr)

    def plain_mask(j, carry):
        sc_ref[j] = jnp.where(sc_ref[j] >= thr, 0.0, NEG_BIG)
        return carry

    def tie_mask(j, seen):
        s = sc_ref[j]
        equal = s == thr
        rank = jnp.dot(tri_ref[...], equal.astype(BF16), preferred_element_type=F32) + seen
        quota = jnp.where(tied, (top - c_hi).astype(F32), jnp.inf)
        keep_equal = jnp.where(rank < quota, 0.0, NEG_BIG)
        sc_ref[j] = jnp.where(s > thr, 0.0, jnp.where(equal, keep_equal, NEG_BIG))
        return seen + jnp.sum(equal.astype(F32), axis=0, keepdims=True)

    def with_ties():
        lax.fori_loop(0, i + 1, tie_mask, jnp.zeros((1, tq), F32))
        return jnp.int32(0)

    def without_ties():
        return lax.fori_loop(0, i + 1, plain_mask, jnp.int32(0))

    lax.cond(jnp.max(tied.astype(I32)) > 0, with_ties, without_ties)

    qs_ref[...] = (qd_ref[...].astype(F32) * (HEAD_DIM ** -0.5 * LOG2E)).astype(BF16)
    m_ref[...] = jnp.full(m_ref.shape, NEG_BIG, F32)
    l_ref[...] = jnp.zeros(l_ref.shape, F32)
    acc_ref[...] = jnp.zeros(acc_ref.shape, F32)

    heads = [slice(h * HEAD_DIM, (h + 1) * HEAD_DIM) for h in range(N_HEADS)]

    def att_tile(j, near):
        start = pl.multiple_of(j * tq, tq)
        mask = sc_ref[j]
        tile_max = []
        for h, hs in enumerate(heads):
            lg = lax.dot_general(kd_ref[pl.ds(start, tq), hs], qs_ref[:, hs], _NT,
                                 preferred_element_type=F32)
            if near is not None:
                lg = lg + bias_ref[h, near * tq:(near + 1) * tq, :]
            lg = lg + mask
            lg_ref[h] = lg
            tile_max.append(jnp.max(lg, axis=0, keepdims=True))
        m_old = m_ref[...]
        m_new = jnp.maximum(m_old, jnp.concatenate(tile_max, axis=0))
        alpha = jnp.exp2(m_old - m_new)
        m_ref[...] = m_new
        for h in range(N_HEADS):
            p_ref[h] = jnp.exp2(lg_ref[h] - m_new[h:h + 1, :]).astype(BF16)
        denom = []
        for h, hs in enumerate(heads):
            v_ext = jnp.concatenate([vt_ref[j, hs, :], ones], axis=0)
            pv = jnp.dot(v_ext, p_ref[h], preferred_element_type=F32)
            acc_ref[h] = alpha[h:h + 1, :] * acc_ref[h] + pv[:HEAD_DIM]
            denom.append(pv[HEAD_DIM:HEAD_DIM + 1])
        l_ref[...] = alpha * l_ref[...] + jnp.concatenate(denom, axis=0)

    def far_step(j, carry):
        att_tile(j, None)
        return carry

    lax.fori_loop(0, i - 1, far_step, 0)

    @pl.when(i >= 1)
    def _():
        att_tile(i - 1, 0)

    att_tile(i, 1)
    for h in range(N_HEADS):
        o = acc_ref[h] / l_ref[h:h + 1, :]
        o_ref[:, h * HEAD_DIM:(h + 1) * HEAD_DIM] = o.T.astype(o_ref.dtype)


def _dsa_attention(main, small, bias, batch, seq, *, qd_col, kd_col, vd_col, qi_col, tq=256):
    nq = seq // tq
    width = N_HEADS * HEAD_DIM
    top = min(TOPK_MAX, seq // 4)
    v_t = _values_by_key_block(main, vd_col, batch, seq, tq)
    tri = (jnp.arange(tq)[None, :] < jnp.arange(tq)[:, None]).astype(BF16)
    resident = pl.Buffered(1)
    return pl.pallas_call(
        functools.partial(_dsa_body, tq=tq, top=top),
        grid=(batch, nq),
        in_specs=[
            pl.BlockSpec((tq, width), lambda b, i: (b * nq + i, qd_col)),
            pl.BlockSpec((tq, IDX_HEADS * LANES), lambda b, i: (b * nq + i, qi_col)),
            pl.BlockSpec((tq, LANES), lambda b, i: (b * nq + i, 1)),
            pl.BlockSpec((seq, width), lambda b, i: (b, kd_col), pipeline_mode=resident),
            pl.BlockSpec((None, nq, width, tq), lambda b, i: (b, 0, 0, 0), pipeline_mode=resident),
            pl.BlockSpec((seq, LANES), lambda b, i: (b, 0), pipeline_mode=resident),
            pl.BlockSpec((N_HEADS, 2 * tq, tq), lambda b, i: (0, 0, 0), pipeline_mode=resident),
            pl.BlockSpec((tq, tq), lambda b, i: (0, 0), pipeline_mode=resident),
        ],
        out_specs=pl.BlockSpec((tq, width), lambda b, i: (b * nq + i, 0)),
        out_shape=jax.ShapeDtypeStruct((batch * seq, width), BF16),
        scratch_shapes=[
            pltpu.VMEM((nq, tq, tq), F32),
            pltpu.VMEM((tq, width), BF16),
            pltpu.VMEM((N_HEADS, tq), F32),
            pltpu.VMEM((N_HEADS, tq), F32),
            pltpu.VMEM((N_HEADS, HEAD_DIM, tq), F32),
            pltpu.VMEM((N_HEADS, tq, tq), F32),
            pltpu.VMEM((N_HEADS, tq, tq), BF16),
        ],
        compiler_params=_params("parallel", "arbitrary"),
        name="dsa_attention",
    )(main, main, small, main, v_t, small, bias, tri)


def _merge_body(osb_ref, ods_ref, wsb_ref, wds_ref, gsb_ref, gds_ref, bsb_ref, bds_ref, wo_ref,
                x_ref, o_ref):
    p_sb = jnp.dot(osb_ref[...], wsb_ref[...], preferred_element_type=F32)
    p_ds = jnp.dot(ods_ref[...], wds_ref[...], preferred_element_type=F32)
    g_sb = jax.nn.sigmoid(gsb_ref[...].astype(F32) + bsb_ref[...])
    g_ds = jax.nn.sigmoid(gds_ref[...].astype(F32) + bds_ref[...])
    merged = (g_sb * p_sb + g_ds * p_ds).astype(BF16)
    o_ref[...] = x_ref[...] + jnp.dot(merged, wo_ref[...], preferred_element_type=F32)


def _merge_out(o_sb, o_ds, w_sb, w_ds, w_out, proj, gate_offset, b_gate, x, *, tm=256):
    m, k = o_sb.shape
    d = w_sb.shape[1]
    tm = min(tm, m)
    assert gate_offset % d == 0 and m % tm == 0
    g = gate_offset // d
    b_gate = b_gate.reshape(1, 2 * d).astype(F32)
    resident = pl.Buffered(1)
    return pl.pallas_call(
        _merge_body,
        grid=(m // tm,),
        in_specs=[
            pl.BlockSpec((tm, k), lambda i: (i, 0)),
            pl.BlockSpec((tm, k), lambda i: (i, 0)),
            pl.BlockSpec((k, d), lambda i: (0, 0), pipeline_mode=resident),
            pl.BlockSpec((k, d), lambda i: (0, 0), pipeline_mode=resident),
            pl.BlockSpec((tm, d), lambda i: (i, g)),
            pl.BlockSpec((tm, d), lambda i: (i, g + 1)),
            pl.BlockSpec((1, d), lambda i: (0, 0), pipeline_mode=resident),
            pl.BlockSpec((1, d), lambda i: (0, 1), pipeline_mode=resident),
            pl.BlockSpec((d, d), lambda i: (0, 0), pipeline_mode=resident),
            pl.BlockSpec((tm, d), lambda i: (i, 0)),
        ],
        out_specs=pl.BlockSpec((tm, d), lambda i: (i, 0)),
        out_shape=jax.ShapeDtypeStruct((m, d), F32),
        compiler_params=_params("parallel"),
        name="merge_out_proj",
    )(o_sb, o_ds, w_sb, w_ds, proj, proj, b_gate, b_gate, w_out, x)


def _cross_body(x_ref, g_ref, wq_ref, km_ref, vm_ref, wo_ref, o_ref):
    x = x_ref[...]
    h = _rms(x, g_ref[...]).astype(BF16)
    q = jnp.dot(h, wq_ref[...], preferred_element_type=F32) * HEAD_DIM ** -0.5
    q = q.astype(BF16)
    outs = []
    for hh in range(MEM_HEADS):
        hs = slice(hh * HEAD_DIM, (hh + 1) * HEAD_DIM)
        lg = lax.dot_general(q[:, hs], km_ref[:, hs], _NT, preferred_element_type=F32)
        p = jnp.exp(lg - jnp.max(lg, axis=1, keepdims=True))
        o = jnp.dot(p.astype(BF16), vm_ref[:, hs], preferred_element_type=F32)
        outs.append((o / jnp.sum(p, axis=1, keepdims=True)).astype(BF16))
    o = jnp.concatenate(outs, axis=1)
    o_ref[...] = x + jnp.dot(o, wo_ref[...], preferred_element_type=F32)


def _cross_attention(x, kv, g_cross, w_cq, w_co, batch, seq, *, tm=512):
    m, d = x.shape
    n_mem = kv.shape[0] // batch
    width = MEM_HEADS * HEAD_DIM
    tm = min(tm, seq)
    nt = seq // tm
    return pl.pallas_call(
        _cross_body,
        grid=(batch, nt),
        in_specs=[
            pl.BlockSpec((tm, d), lambda b, i: (b * nt + i, 0)),
            pl.BlockSpec((1, d), lambda b, i: (0, 0)),
            pl.BlockSpec((d, width), lambda b, i: (0, 0)),
            pl.BlockSpec((n_mem, width), lambda b, i: (b, 0)),
            pl.BlockSpec((n_mem, width), lambda b, i: (b, 1)),
            pl.BlockSpec((width, d), lambda b, i: (0, 0)),
        ],
        out_specs=pl.BlockSpec((tm, d), lambda b, i: (b * nt + i, 0)),
        out_shape=jax.ShapeDtypeStruct((m, d), F32),
        compiler_params=_params("parallel", "parallel"),
        name="cross_attention",
    )(x, g_cross.reshape(1, d).astype(F32), w_cq, kv, kv, w_co)


def _delayed(u, tail, shift):
    rolled = pltpu.roll(u, shift, axis=0)
    row = lax.broadcasted_iota(I32, tail.shape, 0)
    head = jnp.where(row < shift, pltpu.roll(tail, shift, axis=0), rolled[:SUBLANES])
    return jnp.concatenate([head, rolled[SUBLANES:]], axis=0)


def _ffn_up_body(x_ref, g_ref, wa_ref, wv_ref, cwa_ref, cwv_ref, cba_ref, cbv_ref, o_ref,
                 h_ref, halo_ref, *, tiles_per_seq):
    i, j = pl.program_id(0), pl.program_id(1)

    @pl.when(j == 0)
    def _():
        h_ref[...] = _rms(x_ref[...], g_ref[...]).astype(BF16)

    h = h_ref[...]
    tm = h.shape[0]
    sequence_start = i % tiles_per_seq == 0

    def conv(w_ref, cw_ref, cb_ref, slot):
        u = jnp.dot(h, w_ref[...], preferred_element_type=F32)
        tail = jnp.where(sequence_start, 0.0, halo_ref[slot, j])
        halo_ref[slot, j] = u[tm - SUBLANES:, :]
        c = cb_ref[...] + cw_ref[CONV_WIDTH - 1:CONV_WIDTH, :] * u
        for tap in range(CONV_WIDTH - 1):
            c = c + cw_ref[tap:tap + 1, :] * _delayed(u, tail, CONV_WIDTH - 1 - tap)
        return c

    a = conv(wa_ref, cwa_ref, cba_ref, 0)
    val = conv(wv_ref, cwv_ref, cbv_ref, 1)
    o_ref[...] = (jax.nn.gelu(a) * val).astype(o_ref.dtype)


def _ffn_up_gate(x, g_ffn, w_up, conv_w, conv_b, seq, *, tm=1024, tn=512):
    m, d = x.shape
    two_ff = w_up.shape[1]
    d_ff = two_ff // 2
    tm, tn = min(tm, seq), min(tn, d_ff)
    assert seq % tm == 0 and d_ff % tn == 0 and tm >= SUBLANES >= CONV_WIDTH - 1
    nf = d_ff // tn
    conv_w = conv_w.astype(F32)
    conv_b = conv_b.reshape(1, two_ff).astype(F32)
    return pl.pallas_call(
        functools.partial(_ffn_up_body, tiles_per_seq=seq // tm),
        grid=(m // tm, nf),
        in_specs=[
            pl.BlockSpec((tm, d), lambda i, j: (i, 0)),
            pl.BlockSpec((1, d), lambda i, j: (0, 0)),
            pl.BlockSpec((d, tn), lambda i, j: (0, j)),
            pl.BlockSpec((d, tn), lambda i, j: (0, nf + j)),
            pl.BlockSpec((CONV_WIDTH, tn), lambda i, j: (0, j)),
            pl.BlockSpec((CONV_WIDTH, tn), lambda i, j: (0, nf + j)),
            pl.BlockSpec((1, tn), lambda i, j: (0, j)),
            pl.BlockSpec((1, tn), lambda i, j: (0, nf + j)),
        ],
        out_specs=pl.BlockSpec((tm, tn), lambda i, j: (i, j)),
        out_shape=jax.ShapeDtypeStruct((m, d_ff), BF16),
        scratch_shapes=[pltpu.VMEM((tm, d), BF16), pltpu.VMEM((2, nf, SUBLANES, tn), F32)],
        compiler_params=_params("arbitrary", "arbitrary"),
        name="ffn_up_conv_gate",
    )(x, g_ffn.reshape(1, d).astype(F32), w_up, w_up, conv_w, conv_w, conv_b, conv_b)


def _ffn_down_body(a_ref, w_ref, x_ref, g_ref, o_ref, *, final_norm):
    k = pl.program_id(1)

    @pl.when(k == 0)
    def _():
        o_ref[...] = x_ref[...]

    o_ref[...] += jnp.dot(a_ref[...], w_ref[...], preferred_element_type=F32)

    if final_norm:
        @pl.when(k == pl.num_programs(1) - 1)
        def _():
            o_ref[...] = _rms(o_ref[...], g_ref[...])


def _ffn_down(a, w, x, g_final, *, tm=512, tk=1024):
    m, kdim = a.shape
    d = w.shape[1]
    tm, tk = min(tm, m), min(tk, kdim)
    assert m % tm == 0 and kdim % tk == 0
    final_norm = g_final is not None
    g = (g_final if final_norm else jnp.ones((d,), F32)).reshape(1, d).astype(F32)
    return pl.pallas_call(
        functools.partial(_ffn_down_body, final_norm=final_norm),
        grid=(m // tm, kdim // tk),
        in_specs=[
            pl.BlockSpec((tm, tk), lambda i, k: (i, k)),
            pl.BlockSpec((tk, d), lambda i, k: (k, 0)),
            pl.BlockSpec((tm, d), lambda i, k: (i, 0)),
            pl.BlockSpec((1, d), lambda i, k: (0, 0)),
        ],
        out_specs=pl.BlockSpec((tm, d), lambda i, k: (i, 0)),
        out_shape=jax.ShapeDtypeStruct((m, d), F32),
        compiler_params=_params("parallel", "arbitrary"),
        name="ffn_down",
    )(a, w, x, g)


def _layer(x, mem, g_mix, w_in, b_gate, w_proj_sb, w_proj_dsa, w_out, rel_bias,
           g_cross, g_mem, w_cq, w_ckv, w_co, g_ffn, w_up, conv_w, conv_b, w_down, g_final, batch, seq):
    d = x.shape[1]
    width = N_HEADS * HEAD_DIM
    idx_w = IDX_HEADS * IDX_DIM
    o_qi = 6 * width
    o_ki = o_qi + idx_w
    o_wi = o_ki + IDX_DIM
    o_g = o_wi + IDX_HEADS

    w_qi = jnp.pad(w_in[:, o_qi:o_ki].reshape(d, IDX_HEADS, IDX_DIM),
                   ((0, 0), (0, 0), (0, LANES - IDX_DIM))).reshape(d, IDX_HEADS * LANES)
    w_main = jnp.concatenate([w_in[:, :o_qi], w_qi, w_in[:, o_g:]], axis=1).astype(BF16)
    w_small = jnp.concatenate([
        jnp.pad(w_in[:, o_ki:o_wi], ((0, 0), (0, LANES - IDX_DIM))),
        jnp.pad(w_in[:, o_wi:o_g], ((0, 0), (0, LANES - IDX_HEADS)))], axis=1).astype(BF16)

    main = _matmul(x, w_main, name="in_proj_main", gain=g_mix, out_dtype=BF16, tm=1024, tn=1024)
    small = _matmul(x, w_small, name="in_proj_index", gain=g_mix, out_dtype=F32, tm=1024, tn=2 * LANES)

    o_sb = _sb_attention(main, batch, seq, q_col=0, k_col=1, v_col=2)
    tq = 256
    bias = _near_bias(rel_bias, tq)
    o_ds = _dsa_attention(main, small, bias, batch, seq, qd_col=3, kd_col=4, vd_col=5, qi_col=3, tq=tq)

    x = _merge_out(o_sb, o_ds, w_proj_sb.astype(BF16), w_proj_dsa.astype(BF16), w_out.astype(BF16),
                   main, o_qi + IDX_HEADS * LANES, b_gate, x)

    kv = _matmul(mem, w_ckv.astype(BF16), name="mem_kv_proj", gain=g_mem, out_dtype=BF16, tm=512, tn=1024)
    x = _cross_attention(x, kv, g_cross, w_cq.astype(BF16), w_co.astype(BF16), batch, seq)

    act = _ffn_up_gate(x, g_ffn, w_up.astype(BF16), conv_w, conv_b, seq)
    return _ffn_down(act, w_down.astype(BF16), x, g_final)


def kernel(x, mem, g_mix, w_in, b_gate, w_proj_sb, w_proj_dsa, w_out, rel_bias, g_cross, g_mem,
           w_cq, w_ckv, w_co, g_ffn, w_up, conv_w, conv_b, w_down, g_final):
    batch, seq, d = x.shape
    h = x.reshape(batch * seq, d)
    mem2 = mem.reshape(batch * mem.shape[1], d)
    depth = g_mix.shape[0]
    for l in range(depth):
        h = _layer(h, mem2, g_mix[l], w_in[l], b_gate[l], w_proj_sb[l], w_proj_dsa[l], w_out[l],
                   rel_bias, g_cross[l], g_mem[l], w_cq[l], w_ckv[l], w_co[l], g_ffn[l], w_up[l],
                   conv_w[l], conv_b[l], w_down[l], g_final if l == depth - 1 else None, batch, seq)
    return h.reshape(batch, seq, d)
```

```python
import functools

import jax
import jax.numpy as jnp
from jax import lax
from jax.experimental import pallas as pl
from jax.experimental.pallas import tpu as pltpu

F32, BF16, I32 = jnp.float32, jnp.bfloat16, jnp.int32

EPS = 1e-6
HEAD_DIM = 128
N_HEADS = 8
IDX_HEADS = 16
IDX_DIM = 64
CHUNK = 64
TOPK_MAX = 256
N_BUCKETS = 32
MAX_DISTANCE = 128
MEM_HEADS = 4
CONV_WIDTH = 3

LANES = 128
SUBLANES = 8
VMEM_LIMIT_BYTES = 56 * 1024 * 1024
NEG_BIG = -1e30
EXP2_UNDERFLOW = -151.0
LOG2E = 1.4426950408889634
_FLOAT_KEY_LOWEST = -(2 ** 31) + 2 ** 23
_VALUE_STEPS = 3
_STEPS_PER_CHECK = 4
_MAX_SEARCH_STEPS = 48

_NT = (((1,), (1,)), ((), ()))


def _params(*sem):
    return pltpu.CompilerParams(dimension_semantics=sem, vmem_limit_bytes=VMEM_LIMIT_BYTES)


def _rms(x, g):
    inv = lax.rsqrt(jnp.mean(x * x, axis=-1, keepdims=True) + EPS)
    return x * inv * g


def _mm_body(*refs, has_gain):
    if has_gain:
        a_ref, g_ref, w_ref, o_ref = refs
        a = _rms(a_ref[...].astype(F32), g_ref[...]).astype(BF16)
    else:
        a_ref, w_ref, o_ref = refs
        a = a_ref[...]
    o_ref[...] = jnp.dot(a, w_ref[...], preferred_element_type=F32).astype(o_ref.dtype)


def _matmul(a, w, *, name, gain=None, out_dtype, tm):
    m, k = a.shape
    n = w.shape[1]
    tm = min(tm, m)
    assert m % tm == 0, (m, tm)
    in_specs = [pl.BlockSpec((tm, k), lambda i: (i, 0))]
    args = [a]
    if gain is not None:
        in_specs.append(pl.BlockSpec((1, k), lambda i: (0, 0)))
        args.append(gain.reshape(1, k).astype(F32))
    in_specs.append(pl.BlockSpec((k, n), lambda i: (0, 0)))
    args.append(w)
    return pl.pallas_call(
        functools.partial(_mm_body, has_gain=gain is not None),
        grid=(m // tm,),
        in_specs=in_specs,
        out_specs=pl.BlockSpec((tm, n), lambda i: (i, 0)),
        out_shape=jax.ShapeDtypeStruct((m, n), out_dtype),
        compiler_params=_params("parallel"),
        name=name,
    )(*args)


def _mm_ws_body(a_ref, w_ref, o_ref, wb_ref):
    @pl.when(pl.program_id(1) == 0)
    def _():
        wb_ref[...] = w_ref[...].astype(BF16)

    o_ref[...] = jnp.dot(a_ref[...], wb_ref[...], preferred_element_type=F32).astype(o_ref.dtype)


def _matmul_ws(a, w, *, name, col_block, n_blocks, out_dtype, tm, tn):
    m, k = a.shape
    tm = min(tm, m)
    assert m % tm == 0 and (col_block + n_blocks) * tn <= w.shape[1]
    return pl.pallas_call(
        _mm_ws_body,
        grid=(n_blocks, m // tm),
        in_specs=[
            pl.BlockSpec((tm, k), lambda j, i: (i, 0)),
            pl.BlockSpec((k, tn), lambda j, i: (0, col_block + j)),
        ],
        out_specs=pl.BlockSpec((tm, tn), lambda j, i: (i, j)),
        out_shape=jax.ShapeDtypeStruct((m, n_blocks * tn), out_dtype),
        scratch_shapes=[pltpu.VMEM((k, tn), BF16)],
        compiler_params=_params("parallel", "arbitrary"),
        name=name,
    )(a, w)


def _norm_body(x_ref, g_ref, o_ref):
    o_ref[...] = _rms(x_ref[...], g_ref[...]).astype(o_ref.dtype)


def _rmsnorm_bf16(x, g, *, tm=512):
    m, d = x.shape
    tm = min(tm, m)
    return pl.pallas_call(
        _norm_body,
        grid=(m // tm,),
        in_specs=[pl.BlockSpec((tm, d), lambda i: (i, 0)), pl.BlockSpec((1, d), lambda i: (0, 0))],
        out_specs=pl.BlockSpec((tm, d), lambda i: (i, 0)),
        out_shape=jax.ShapeDtypeStruct((m, d), BF16),
        compiler_params=_params("parallel"),
        name="mixer_norm",
    )(x, g.reshape(1, d).astype(F32))


def _values_by_key_block(arr, col, batch, seq, tq):
    width = N_HEADS * HEAD_DIM
    return arr[:, col * width:(col + 1) * width].reshape(batch, seq // tq, tq, width).swapaxes(2, 3)


def _sb_body(q_ref, k_ref, vt_ref, tri_ref, o_ref, qs_ref, z_ref, lb_ref, wb_ref, acc_ref, *, tq):
    i = pl.program_id(1)
    shape = (tq, tq)
    before = lax.broadcasted_iota(I32, shape, 0) < lax.broadcasted_iota(I32, shape, 1)
    heads = [slice(h * HEAD_DIM, (h + 1) * HEAD_DIM) for h in range(N_HEADS)]
    qs_ref[...] = (q_ref[...].astype(F32) * (HEAD_DIM ** -0.5 * LOG2E)).astype(BF16)
    acc_ref[...] = jnp.zeros(acc_ref.shape, F32)

    def tile(j, carry, diagonal):
        start = pl.multiple_of(j * tq, tq)
        for h, hs in enumerate(heads):
            z = lax.dot_general(k_ref[pl.ds(start, tq), hs], qs_ref[:, hs], _NT,
                                preferred_element_type=F32)
            log_keep = jnp.minimum(-z, 0.0) - jnp.log2(1.0 + jnp.exp2(-jnp.abs(z)))
            if diagonal:
                log_keep = jnp.where(before, log_keep, 0.0)
            z_ref[h] = z
            lb_ref[h] = log_keep.astype(BF16)
        new_carry = []
        for h in range(N_HEADS):
            c = jnp.dot(tri_ref[...], lb_ref[h], preferred_element_type=F32) + carry[h:h + 1, :]
            w = jnp.exp2(z_ref[h] + c)
            if diagonal:
                w = jnp.where(before, w, 0.0)
            wb_ref[h] = w.astype(BF16)
            new_carry.append(c[0:1, :])
        for h, hs in enumerate(heads):
            acc_ref[h] += jnp.dot(vt_ref[j, hs, :], wb_ref[h], preferred_element_type=F32)
        return jnp.concatenate(new_carry, axis=0)

    carry = tile(i, jnp.zeros((N_HEADS, tq), F32), True)

    def cond(state):
        j, live, _ = state
        return jnp.logical_and(j >= 0, live > EXP2_UNDERFLOW)

    def body(state):
        j, _, carry = state
        carry = tile(j, carry, False)
        return j - 1, jnp.max(carry), carry

    lax.while_loop(cond, body, (i - 1, jnp.max(carry), carry))
    for h, hs in enumerate(heads):
        o_ref[:, hs] = acc_ref[h].T.astype(o_ref.dtype)


def _sb_attention(qkv, batch, seq, *, q_col, k_col, v_col, tq=256):
    nq = seq // tq
    width = N_HEADS * HEAD_DIM
    v_t = _values_by_key_block(qkv, v_col, batch, seq, tq)
    tri = (jnp.arange(tq)[None, :] >= jnp.arange(tq)[:, None]).astype(BF16)
    resident = pl.Buffered(1)
    return pl.pallas_call(
        functools.partial(_sb_body, tq=tq),
        grid=(batch, nq),
        in_specs=[
            pl.BlockSpec((tq, width), lambda b, i: (b * nq + i, q_col)),
            pl.BlockSpec((seq, width), lambda b, i: (b, k_col), pipeline_mode=resident),
            pl.BlockSpec((None, nq, width, tq), lambda b, i: (b, 0, 0, 0), pipeline_mode=resident),
            pl.BlockSpec((tq, tq), lambda b, i: (0, 0), pipeline_mode=resident),
        ],
        out_specs=pl.BlockSpec((tq, width), lambda b, i: (b * nq + i, 0)),
        out_shape=jax.ShapeDtypeStruct((batch * seq, width), BF16),
        scratch_shapes=[
            pltpu.VMEM((tq, width), BF16),
            pltpu.VMEM((N_HEADS, tq, tq), F32),
            pltpu.VMEM((N_HEADS, tq, tq), BF16),
            pltpu.VMEM((N_HEADS, tq, tq), BF16),
            pltpu.VMEM((N_HEADS, HEAD_DIM, tq), F32),
        ],
        compiler_params=_params("parallel", "arbitrary"),
        name="sb_attention",
    )(qkv, qkv, v_t, tri)


def _bucket_thresholds():
    nb = N_BUCKETS // 2
    max_exact = nb // 2
    span = nb - max_exact
    out = []
    for k in range(1, span):
        n = max_exact
        while n ** span * max_exact ** k < MAX_DISTANCE ** k * max_exact ** span:
            n += 1
        out.append(n)
    return max_exact, out


def _bias_body(rb_ref, o_ref, *, tq):
    nb = N_BUCKETS // 2
    max_exact, steps = _bucket_thresholds()
    shape = (2 * tq, tq)
    rel = lax.broadcasted_iota(I32, shape, 0) - lax.broadcasted_iota(I32, shape, 1) - tq
    n = jnp.abs(rel)
    large = jnp.full(shape, max_exact, I32)
    for t in steps:
        large = large + (n >= t).astype(I32)
    bucket = jnp.where(rel > 0, nb, 0) + jnp.where(n < max_exact, n, large)
    for h in range(N_HEADS):
        val = jnp.zeros(shape, F32)
        for b in range(N_BUCKETS):
            val = jnp.where(bucket == b, rb_ref[b, h], val)
        o_ref[h] = (val - rb_ref[nb - 1, h]) * LOG2E


def _near_bias(rel_bias, tq):
    return pl.pallas_call(
        functools.partial(_bias_body, tq=tq),
        in_specs=[pl.BlockSpec(memory_space=pltpu.SMEM)],
        out_specs=pl.BlockSpec(memory_space=pltpu.VMEM),
        out_shape=jax.ShapeDtypeStruct((N_HEADS, 2 * tq, tq), F32),
        compiler_params=pltpu.CompilerParams(vmem_limit_bytes=VMEM_LIMIT_BYTES),
        name="dsa_near_bias",
    )(rel_bias.astype(F32))


def _float_key(x):
    bits = lax.bitcast_convert_type(x, I32)
    return bits ^ ((bits >> 31) & 0x7FFFFFFF)


def _key_float(key):
    return lax.bitcast_convert_type(key ^ ((key >> 31) & 0x7FFFFFFF), F32)


def _dsa_body(qd_ref, qi_ref, wq_ref, kd_ref, vt_ref, ki_ref, bias_ref, tri_ref, o_ref,
              sc_ref, qs_ref, m_ref, l_ref, acc_ref, lg_ref, p_ref, *, tq, top):
    i = pl.program_id(1)
    shape = (tq, tq)
    key_row = lax.broadcasted_iota(I32, shape, 0)
    qry_col = lax.broadcasted_iota(I32, shape, 1)
    visible = key_row // CHUNK <= qry_col // CHUNK

    w_t = (wq_ref[...] * (IDX_DIM ** -0.5 * IDX_HEADS ** -0.5)).T

    heads_per_vreg = LANES // IDX_DIM

    def score_tile(j, lo, hi, diagonal):
        start = pl.multiple_of(j * tq, tq)
        ki = [ki_ref[pl.ds(start, tq), c * LANES:(c + 1) * LANES].astype(BF16)
              for c in range(heads_per_vreg)]
        s = jnp.zeros(shape, F32)
        for h in range(IDX_HEADS):
            g, c = divmod(h, heads_per_vreg)
            d = lax.dot_general(ki[c], qi_ref[:, g * LANES:(g + 1) * LANES], _NT,
                                preferred_element_type=F32)
            s = s + w_t[h:h + 1, :] * jnp.maximum(d, 0.0)
        if diagonal:
            lo = jnp.minimum(lo, jnp.min(jnp.where(visible, s, jnp.inf), axis=0, keepdims=True))
            s = jnp.where(visible, s, -jnp.inf)
        else:
            lo = jnp.minimum(lo, jnp.min(s, axis=0, keepdims=True))
        hi = jnp.maximum(hi, jnp.max(s, axis=0, keepdims=True))
        sc_ref[j] = s
        return lo, hi

    lo, hi = lax.fori_loop(
        0, i, lambda j, c: score_tile(j, c[0], c[1], False),
        (jnp.full((1, tq), jnp.inf, F32), jnp.full((1, tq), -jnp.inf, F32)))
    lo, hi = score_tile(i, lo, hi, True)

    ones = jnp.ones((2 * SUBLANES, tq), BF16)

    def count_ge(t):
        def step(j, acc):
            hit = (sc_ref[j] >= t).astype(I32)
            for g in range(tq // SUBLANES):
                acc = acc + hit[g * SUBLANES:(g + 1) * SUBLANES, :]
            return acc
        acc = lax.fori_loop(0, i + 1, step, jnp.zeros((SUBLANES, tq), I32))
        return jnp.sum(acc, axis=0, keepdims=True)

    qry = lax.broadcasted_iota(I32, (1, tq), 1)
    n_visible = i * tq + (qry // CHUNK + 1) * CHUNK
    k_lo0 = _float_key(lo)
    k_hi0 = _float_key(hi) + 1
    wanted = n_visible > top
    adjacent0 = k_lo0 + 1 == k_hi0
    select_all = jnp.full((1, tq), _FLOAT_KEY_LOWEST, I32)

    def search_cond(state):
        it, active = state[0], state[1]
        return jnp.logical_and(it < _MAX_SEARCH_STEPS, jnp.max(active) > 0)

    def search_step(it, active, k_lo, k_hi, c_hi, k_thr, tied):
        by_value = _float_key(0.5 * _key_float(k_lo) + 0.5 * _key_float(k_hi))
        by_bits = (k_lo & k_hi) + ((k_lo ^ k_hi) >> 1)
        k_t = jnp.where(it < _VALUE_STEPS, by_value, by_bits)
        k_t = jnp.minimum(jnp.maximum(k_t, k_lo + 1), k_hi - 1)
        c = count_ge(_key_float(k_t))
        found = jnp.logical_and(active, c == top)
        above = jnp.logical_and(active, c > top)
        below = jnp.logical_and(active, c < top)
        k_lo = jnp.where(above, k_t, k_lo)
        k_hi, c_hi = jnp.where(below, k_t, k_hi), jnp.where(below, c, c_hi)
        closed = jnp.logical_and(jnp.logical_or(above, below), k_lo + 1 == k_hi)
        k_thr = jnp.where(found, k_t, jnp.where(closed, k_lo, k_thr))
        tied = jnp.logical_or(tied, closed)
        active = jnp.logical_and(active, jnp.logical_not(jnp.logical_or(found, closed)))
        return active, k_lo, k_hi, c_hi, k_thr, tied

    def search_body(state):
        it, active, k_lo, k_hi, c_hi, k_thr, tied = state
        active, tied = active > 0, tied > 0
        for _ in range(_STEPS_PER_CHECK):
            active, k_lo, k_hi, c_hi, k_thr, tied = search_step(
                it, active, k_lo, k_hi, c_hi, k_thr, tied)
            it = it + 1
        return it, active.astype(I32), k_lo, k_hi, c_hi, k_thr, tied.astype(I32)

    tied0 = jnp.logical_and(wanted, adjacent0)
    state = lax.while_loop(search_cond, search_body, (
        jnp.int32(0), jnp.logical_and(wanted, jnp.logical_not(adjacent0)).astype(I32),
        k_lo0, k_hi0, jnp.zeros((1, tq), I32),
        jnp.where(tied0, k_lo0, select_all), tied0.astype(I32)))
    _, _, _, _, c_hi, k_thr, tied = state
    tied = tied > 0
    thr = _key_float(k_thr)

    def plain_mask(j, carry):
        sc_ref[j] = jnp.where(sc_ref[j] >= thr, 0.0, NEG_BIG)
        return carry

    def tie_mask(j, seen):
        s = sc_ref[j]
        equal = s == thr
        rank = jnp.dot(tri_ref[...], equal.astype(BF16), preferred_element_type=F32) + seen
        quota = jnp.where(tied, (top - c_hi).astype(F32), jnp.inf)
        keep_equal = jnp.where(rank < quota, 0.0, NEG_BIG)
        sc_ref[j] = jnp.where(s > thr, 0.0, jnp.where(equal, keep_equal, NEG_BIG))
        return seen + jnp.sum(equal.astype(F32), axis=0, keepdims=True)

    def with_ties():
        lax.fori_loop(0, i + 1, tie_mask, jnp.zeros((1, tq), F32))
        return jnp.int32(0)

    def without_ties():
        return lax.fori_loop(0, i + 1, plain_mask, jnp.int32(0))

    lax.cond(jnp.max(tied.astype(I32)) > 0, with_ties, without_ties)

    qs_ref[...] = (qd_ref[...].astype(F32) * (HEAD_DIM ** -0.5 * LOG2E)).astype(BF16)
    m_ref[...] = jnp.full(m_ref.shape, NEG_BIG, F32)
    l_ref[...] = jnp.zeros(l_ref.shape, F32)
    acc_ref[...] = jnp.zeros(acc_ref.shape, F32)

    heads = [slice(h * HEAD_DIM, (h + 1) * HEAD_DIM) for h in range(N_HEADS)]

    def att_tile(j, near):
        start = pl.multiple_of(j * tq, tq)
        mask = sc_ref[j]
        tile_max = []
        for h, hs in enumerate(heads):
            lg = lax.dot_general(kd_ref[pl.ds(start, tq), hs], qs_ref[:, hs], _NT,
                                 preferred_element_type=F32)
            if near is not None:
                lg = lg + bias_ref[h, near * tq:(near + 1) * tq, :]
            lg = lg + mask
            lg_ref[h] = lg
            tile_max.append(jnp.max(lg, axis=0, keepdims=True))
        m_old = m_ref[...]
        m_new = jnp.maximum(m_old, jnp.concatenate(tile_max, axis=0))
        alpha = jnp.exp2(m_old - m_new)
        m_ref[...] = m_new
        for h in range(N_HEADS):
            p_ref[h] = jnp.exp2(lg_ref[h] - m_new[h:h + 1, :]).astype(BF16)
        denom = []
        for h, hs in enumerate(heads):
            v_ext = jnp.concatenate([vt_ref[j, hs, :], ones], axis=0)
            pv = jnp.dot(v_ext, p_ref[h], preferred_element_type=F32)
            acc_ref[h] = alpha[h:h + 1, :] * acc_ref[h] + pv[:HEAD_DIM]
            denom.append(pv[HEAD_DIM:HEAD_DIM + 1])
        l_ref[...] = alpha * l_ref[...] + jnp.concatenate(denom, axis=0)

    def far_step(j, carry):
        att_tile(j, None)
        return carry

    lax.fori_loop(0, i - 1, far_step, 0)

    @pl.when(i >= 1)
    def _():
        att_tile(i - 1, 0)

    att_tile(i, 1)
    for h in range(N_HEADS):
        o = acc_ref[h] / l_ref[h:h + 1, :]
        o_ref[:, h * HEAD_DIM:(h + 1) * HEAD_DIM] = o.T.astype(o_ref.dtype)


def _dsa_attention(main, small, bias, batch, seq, *, qd_col, kd_col, vd_col, qi_col, tq=256):
    nq = seq // tq
    width = N_HEADS * HEAD_DIM
    assert IDX_HEADS * IDX_DIM == width
    key_copies = LANES // IDX_DIM
    top = min(TOPK_MAX, seq // 4)
    v_t = _values_by_key_block(main, vd_col, batch, seq, tq)
    tri = (jnp.arange(tq)[None, :] < jnp.arange(tq)[:, None]).astype(BF16)
    resident = pl.Buffered(1)
    return pl.pallas_call(
        functools.partial(_dsa_body, tq=tq, top=top),
        grid=(batch, nq),
        in_specs=[
            pl.BlockSpec((tq, width), lambda b, i: (b * nq + i, qd_col)),
            pl.BlockSpec((tq, IDX_HEADS * IDX_DIM), lambda b, i: (b * nq + i, qi_col)),
            pl.BlockSpec((tq, LANES), lambda b, i: (b * nq + i, key_copies)),
            pl.BlockSpec((seq, width), lambda b, i: (b, kd_col), pipeline_mode=resident),
            pl.BlockSpec((None, nq, width, tq), lambda b, i: (b, 0, 0, 0), pipeline_mode=resident),
            pl.BlockSpec((seq, key_copies * LANES), lambda b, i: (b, 0), pipeline_mode=resident),
            pl.BlockSpec((N_HEADS, 2 * tq, tq), lambda b, i: (0, 0, 0), pipeline_mode=resident),
            pl.BlockSpec((tq, tq), lambda b, i: (0, 0), pipeline_mode=resident),
        ],
        out_specs=pl.BlockSpec((tq, width), lambda b, i: (b * nq + i, 0)),
        out_shape=jax.ShapeDtypeStruct((batch * seq, width), BF16),
        scratch_shapes=[
            pltpu.VMEM((nq, tq, tq), F32),
            pltpu.VMEM((tq, width), BF16),
            pltpu.VMEM((N_HEADS, tq), F32),
            pltpu.VMEM((N_HEADS, tq), F32),
            pltpu.VMEM((N_HEADS, HEAD_DIM, tq), F32),
            pltpu.VMEM((N_HEADS, tq, tq), F32),
            pltpu.VMEM((N_HEADS, tq, tq), BF16),
        ],
        compiler_params=_params("parallel", "arbitrary"),
        name="dsa_attention",
    )(main, main, small, main, v_t, small, bias, tri)


def _merge_body(osb_ref, ods_ref, wsb_ref, wds_ref, gsb_ref, gds_ref, bsb_ref, bds_ref, wo_ref,
                x_ref, o_ref):
    p_sb = jnp.dot(osb_ref[...], wsb_ref[...], preferred_element_type=F32)
    p_ds = jnp.dot(ods_ref[...], wds_ref[...], preferred_element_type=F32)
    g_sb = jax.nn.sigmoid(gsb_ref[...].astype(F32) + bsb_ref[...])
    g_ds = jax.nn.sigmoid(gds_ref[...].astype(F32) + bds_ref[...])
    merged = (g_sb * p_sb + g_ds * p_ds).astype(BF16)
    o_ref[...] = x_ref[...] + jnp.dot(merged, wo_ref[...], preferred_element_type=F32)


def _merge_out(o_sb, o_ds, w_sb, w_ds, w_out, proj, gate_offset, b_gate, x, *, tm=256):
    m, k = o_sb.shape
    d = w_sb.shape[1]
    tm = min(tm, m)
    assert gate_offset % d == 0 and m % tm == 0
    g = gate_offset // d
    b_gate = b_gate.reshape(1, 2 * d).astype(F32)
    resident = pl.Buffered(1)
    return pl.pallas_call(
        _merge_body,
        grid=(m // tm,),
        in_specs=[
            pl.BlockSpec((tm, k), lambda i: (i, 0)),
            pl.BlockSpec((tm, k), lambda i: (i, 0)),
            pl.BlockSpec((k, d), lambda i: (0, 0), pipeline_mode=resident),
            pl.BlockSpec((k, d), lambda i: (0, 0), pipeline_mode=resident),
            pl.BlockSpec((tm, d), lambda i: (i, g)),
            pl.BlockSpec((tm, d), lambda i: (i, g + 1)),
            pl.BlockSpec((1, d), lambda i: (0, 0), pipeline_mode=resident),
            pl.BlockSpec((1, d), lambda i: (0, 1), pipeline_mode=resident),
            pl.BlockSpec((d, d), lambda i: (0, 0), pipeline_mode=resident),
            pl.BlockSpec((tm, d), lambda i: (i, 0)),
        ],
        out_specs=pl.BlockSpec((tm, d), lambda i: (i, 0)),
        out_shape=jax.ShapeDtypeStruct((m, d), F32),
        compiler_params=_params("parallel"),
        name="merge_out_proj",
    )(o_sb, o_ds, w_sb, w_ds, proj, proj, b_gate, b_gate, w_out, x)


def _cross_body(x_ref, g_ref, wq_ref, km_ref, vm_ref, wo_ref, gn_ref, o_ref, hn_ref):
    x = x_ref[...]
    h = _rms(x, g_ref[...]).astype(BF16)
    q = jnp.dot(h, wq_ref[...], preferred_element_type=F32) * HEAD_DIM ** -0.5
    q = q.astype(BF16)
    outs = []
    for hh in range(MEM_HEADS):
        hs = slice(hh * HEAD_DIM, (hh + 1) * HEAD_DIM)
        lg = lax.dot_general(q[:, hs], km_ref[:, hs], _NT, preferred_element_type=F32)
        p = jnp.exp(lg - jnp.max(lg, axis=1, keepdims=True))
        o = jnp.dot(p.astype(BF16), vm_ref[:, hs], preferred_element_type=F32)
        outs.append((o / jnp.sum(p, axis=1, keepdims=True)).astype(BF16))
    o = jnp.concatenate(outs, axis=1)
    y = x + jnp.dot(o, wo_ref[...], preferred_element_type=F32)
    o_ref[...] = y
    hn_ref[...] = _rms(y, gn_ref[...]).astype(hn_ref.dtype)


def _cross_attention(x, kv, g_cross, w_cq, w_co, g_next, batch, seq, *, tm=512):
    m, d = x.shape
    n_mem = kv.shape[0] // batch
    width = MEM_HEADS * HEAD_DIM
    tm = min(tm, seq)
    nt = seq // tm
    row_tile = pl.BlockSpec((tm, d), lambda b, i: (b * nt + i, 0))
    gain = pl.BlockSpec((1, d), lambda b, i: (0, 0))
    return pl.pallas_call(
        _cross_body,
        grid=(batch, nt),
        in_specs=[
            row_tile,
            gain,
            pl.BlockSpec((d, width), lambda b, i: (0, 0)),
            pl.BlockSpec((n_mem, width), lambda b, i: (b, 0)),
            pl.BlockSpec((n_mem, width), lambda b, i: (b, 1)),
            pl.BlockSpec((width, d), lambda b, i: (0, 0)),
            gain,
        ],
        out_specs=[row_tile, row_tile],
        out_shape=[jax.ShapeDtypeStruct((m, d), F32), jax.ShapeDtypeStruct((m, d), BF16)],
        compiler_params=_params("parallel", "parallel"),
        name="cross_attention",
    )(x, g_cross.reshape(1, d).astype(F32), w_cq, kv, kv, w_co, g_next.reshape(1, d).astype(F32))


def _delayed(u, tail, shift):
    rolled = pltpu.roll(u, shift, axis=0)
    row = lax.broadcasted_iota(I32, tail.shape, 0)
    head = jnp.where(row < shift, pltpu.roll(tail, shift, axis=0), rolled[:SUBLANES])
    return jnp.concatenate([head, rolled[SUBLANES:]], axis=0)


def _ffn_up_body(h_ref, wa_ref, wv_ref, cwa_ref, cwv_ref, cba_ref, cbv_ref, o_ref,
                 wab_ref, wvb_ref, halo_ref, *, tiles_per_seq):
    i = pl.program_id(1)

    @pl.when(i == 0)
    def _():
        wab_ref[...] = wa_ref[...].astype(BF16)
        wvb_ref[...] = wv_ref[...].astype(BF16)

    h = h_ref[...]
    tm = h.shape[0]
    sequence_start = i % tiles_per_seq == 0

    def conv(wb_ref, cw_ref, cb_ref, slot):
        u = jnp.dot(h, wb_ref[...], preferred_element_type=F32)
        tail = jnp.where(sequence_start, 0.0, halo_ref[slot])
        halo_ref[slot] = u[tm - SUBLANES:, :]
        c = cb_ref[...] + cw_ref[CONV_WIDTH - 1:CONV_WIDTH, :] * u
        for tap in range(CONV_WIDTH - 1):
            c = c + cw_ref[tap:tap + 1, :] * _delayed(u, tail, CONV_WIDTH - 1 - tap)
        return c

    a = conv(wab_ref, cwa_ref, cba_ref, 0)
    val = conv(wvb_ref, cwv_ref, cbv_ref, 1)
    o_ref[...] = (jax.nn.gelu(a) * val).astype(o_ref.dtype)


def _ffn_up_gate(h, w_up, conv_w, conv_b, seq, *, tm=1024, tn=512):
    m, d = h.shape
    two_ff = w_up.shape[1]
    d_ff = two_ff // 2
    tm, tn = min(tm, seq), min(tn, d_ff)
    assert seq % tm == 0 and d_ff % tn == 0 and tm >= SUBLANES >= CONV_WIDTH - 1
    nf = d_ff // tn
    conv_w = conv_w.astype(F32)
    conv_b = conv_b.reshape(1, two_ff).astype(F32)
    return pl.pallas_call(
        functools.partial(_ffn_up_body, tiles_per_seq=seq // tm),
        grid=(nf, m // tm),
        in_specs=[
            pl.BlockSpec((tm, d), lambda j, i: (i, 0)),
            pl.BlockSpec((d, tn), lambda j, i: (0, j)),
            pl.BlockSpec((d, tn), lambda j, i: (0, nf + j)),
            pl.BlockSpec((CONV_WIDTH, tn), lambda j, i: (0, j)),
            pl.BlockSpec((CONV_WIDTH, tn), lambda j, i: (0, nf + j)),
            pl.BlockSpec((1, tn), lambda j, i: (0, j)),
            pl.BlockSpec((1, tn), lambda j, i: (0, nf + j)),
        ],
        out_specs=pl.BlockSpec((tm, tn), lambda j, i: (i, j)),
        out_shape=jax.ShapeDtypeStruct((m, d_ff), BF16),
        scratch_shapes=[pltpu.VMEM((d, tn), BF16), pltpu.VMEM((d, tn), BF16),
                        pltpu.VMEM((2, SUBLANES, tn), F32)],
        compiler_params=_params("parallel", "arbitrary"),
        name="ffn_up_conv_gate",
    )(h, w_up, w_up, conv_w, conv_w, conv_b, conv_b)


def _ffn_down_body(a_ref, w_ref, x_ref, g_ref, o_ref, *, final_norm):
    k = pl.program_id(1)

    @pl.when(k == 0)
    def _():
        o_ref[...] = x_ref[...]

    o_ref[...] += jnp.dot(a_ref[...], w_ref[...], preferred_element_type=F32)

    if final_norm:
        @pl.when(k == pl.num_programs(1) - 1)
        def _():
            o_ref[...] = _rms(o_ref[...], g_ref[...])


def _ffn_down(a, w, x, g_final, *, tm=512, tk=1024):
    m, kdim = a.shape
    d = w.shape[1]
    tm, tk = min(tm, m), min(tk, kdim)
    assert m % tm == 0 and kdim % tk == 0
    final_norm = g_final is not None
    g = (g_final if final_norm else jnp.ones((d,), F32)).reshape(1, d).astype(F32)
    return pl.pallas_call(
        functools.partial(_ffn_down_body, final_norm=final_norm),
        grid=(m // tm, kdim // tk),
        in_specs=[
            pl.BlockSpec((tm, tk), lambda i, k: (i, k)),
            pl.BlockSpec((tk, d), lambda i, k: (k, 0)),
            pl.BlockSpec((tm, d), lambda i, k: (i, 0)),
            pl.BlockSpec((1, d), lambda i, k: (0, 0)),
        ],
        out_specs=pl.BlockSpec((tm, d), lambda i, k: (i, 0)),
        out_shape=jax.ShapeDtypeStruct((m, d), F32),
        compiler_params=_params("parallel", "arbitrary"),
        name="ffn_down",
    )(a, w, x, g)


def _layer(x, mem, g_mix, w_in, b_gate, w_proj_sb, w_proj_dsa, w_out, rel_bias,
           g_cross, g_mem, w_cq, w_ckv, w_co, g_ffn, w_up, conv_w, conv_b, w_down, g_final, batch, seq):
    d = x.shape[1]
    width = N_HEADS * HEAD_DIM
    idx_w = IDX_HEADS * IDX_DIM
    o_qi = 6 * width
    o_ki = o_qi + idx_w
    o_wi = o_ki + IDX_DIM
    o_g = o_wi + IDX_HEADS

    h = _rmsnorm_bf16(x, g_mix)
    main = _matmul_ws(h, w_in, name="in_proj_main", col_block=0, n_blocks=o_ki // width,
                      out_dtype=BF16, tm=1024, tn=width)
    gates = _matmul_ws(h, w_in[:, o_g:], name="in_proj_gates", col_block=0, n_blocks=2 * d // width,
                       out_dtype=BF16, tm=1024, tn=width)
    zeros = jnp.zeros((d, LANES - IDX_DIM), F32)
    w_small = jnp.concatenate([
        w_in[:, o_ki:o_wi], zeros, zeros, w_in[:, o_ki:o_wi],
        jnp.pad(w_in[:, o_wi:o_g], ((0, 0), (0, LANES - IDX_HEADS)))], axis=1).astype(BF16)
    small = _matmul(h, w_small, name="in_proj_index", out_dtype=F32, tm=1024)

    o_sb = _sb_attention(main, batch, seq, q_col=0, k_col=1, v_col=2)
    tq = 256
    bias = _near_bias(rel_bias, tq)
    o_ds = _dsa_attention(main, small, bias, batch, seq, qd_col=3, kd_col=4, vd_col=5, qi_col=6, tq=tq)

    x = _merge_out(o_sb, o_ds, w_proj_sb.astype(BF16), w_proj_dsa.astype(BF16), w_out.astype(BF16),
                   gates, 0, b_gate, x)

    kv = _matmul(mem, w_ckv.astype(BF16), name="mem_kv_proj", gain=g_mem, out_dtype=BF16, tm=512)
    x, h_ffn = _cross_attention(x, kv, g_cross, w_cq.astype(BF16), w_co.astype(BF16), g_ffn, batch, seq)

    act = _ffn_up_gate(h_ffn, w_up, conv_w, conv_b, seq)
    return _ffn_down(act, w_down.astype(BF16), x, g_final)


def kernel(x, mem, g_mix, w_in, b_gate, w_proj_sb, w_proj_dsa, w_out, rel_bias, g_cross, g_mem,
           w_cq, w_ckv, w_co, g_ffn, w_up, conv_w, conv_b, w_down, g_final):
    batch, seq, d = x.shape
    h = x.reshape(batch * seq, d)
    mem2 = mem.reshape(batch * mem.shape[1], d)
    depth = g_mix.shape[0]
    for l in range(depth):
        h = _layer(h, mem2, g_mix[l], w_in[l], b_gate[l], w_proj_sb[l], w_proj_dsa[l], w_out[l],
                   rel_bias, g_cross[l], g_mem[l], w_cq[l], w_ckv[l], w_co[l], g_ffn[l], w_up[l],
                   conv_w[l], conv_b[l], w_down[l], g_final if l == depth - 1 else None, batch, seq)
    return h.reshape(batch, seq, d)
```

```python
import functools

import jax
import jax.numpy as jnp
from jax import lax
from jax.experimental import pallas as pl
from jax.experimental.pallas import tpu as pltpu

F32, BF16, I32 = jnp.float32, jnp.bfloat16, jnp.int32

EPS = 1e-6
HEAD_DIM = 128
N_HEADS = 8
IDX_HEADS = 16
IDX_DIM = 64
CHUNK = 64
TOPK_MAX = 256
N_BUCKETS = 32
MAX_DISTANCE = 128
MEM_HEADS = 4
CONV_WIDTH = 3

LANES = 128
SUBLANES = 8
VMEM_LIMIT_BYTES = 56 * 1024 * 1024
NEG_BIG = -1e30
EXP2_UNDERFLOW = -151.0
LOG2E = 1.4426950408889634
_FLOAT_KEY_LOWEST = -(2 ** 31) + 2 ** 23
_VALUE_STEPS = 3
_STEPS_PER_CHECK = 4
_MAX_SEARCH_STEPS = 48

_NT = (((1,), (1,)), ((), ()))


def _params(*sem):
    return pltpu.CompilerParams(dimension_semantics=sem, vmem_limit_bytes=VMEM_LIMIT_BYTES)


def _rms(x, g):
    inv = lax.rsqrt(jnp.mean(x * x, axis=-1, keepdims=True) + EPS)
    return x * inv * g


def _mm_body(*refs, has_gain):
    if has_gain:
        a_ref, g_ref, w_ref, o_ref = refs
        a = _rms(a_ref[...].astype(F32), g_ref[...]).astype(BF16)
    else:
        a_ref, w_ref, o_ref = refs
        a = a_ref[...]
    o_ref[...] = jnp.dot(a, w_ref[...], preferred_element_type=F32).astype(o_ref.dtype)


def _matmul(a, w, *, name, gain=None, out_dtype, tm):
    m, k = a.shape
    n = w.shape[1]
    tm = min(tm, m)
    assert m % tm == 0, (m, tm)
    in_specs = [pl.BlockSpec((tm, k), lambda i: (i, 0))]
    args = [a]
    if gain is not None:
        in_specs.append(pl.BlockSpec((1, k), lambda i: (0, 0)))
        args.append(gain.reshape(1, k).astype(F32))
    in_specs.append(pl.BlockSpec((k, n), lambda i: (0, 0)))
    args.append(w)
    return pl.pallas_call(
        functools.partial(_mm_body, has_gain=gain is not None),
        grid=(m // tm,),
        in_specs=in_specs,
        out_specs=pl.BlockSpec((tm, n), lambda i: (i, 0)),
        out_shape=jax.ShapeDtypeStruct((m, n), out_dtype),
        compiler_params=_params("parallel"),
        name=name,
    )(*args)


def _mm_ws_body(a_ref, wt_ref, o_ref, wb_ref):
    @pl.when(pl.program_id(1) == 0)
    def _():
        wb_ref[...] = wt_ref[...].astype(BF16)

    o_ref[...] = lax.dot_general(a_ref[...], wb_ref[...], _NT,
                                 preferred_element_type=F32).astype(o_ref.dtype)


def _matmul_ws(a, w_t, *, name, n_blocks, out_dtype, tm, tn):
    m, k = a.shape
    tm = min(tm, m)
    assert m % tm == 0 and n_blocks * tn <= w_t.shape[0] and w_t.shape[1] == k
    return pl.pallas_call(
        _mm_ws_body,
        grid=(n_blocks, m // tm),
        in_specs=[
            pl.BlockSpec((tm, k), lambda j, i: (i, 0)),
            pl.BlockSpec((tn, k), lambda j, i: (j, 0)),
        ],
        out_specs=pl.BlockSpec((tm, tn), lambda j, i: (i, j)),
        out_shape=jax.ShapeDtypeStruct((m, n_blocks * tn), out_dtype),
        scratch_shapes=[pltpu.VMEM((tn, k), BF16)],
        compiler_params=_params("parallel", "arbitrary"),
        name=name,
    )(a, w_t)


def _norm_body(x_ref, g_ref, o_ref):
    o_ref[...] = _rms(x_ref[...], g_ref[...]).astype(o_ref.dtype)


def _rmsnorm_bf16(x, g, *, tm=512):
    m, d = x.shape
    tm = min(tm, m)
    return pl.pallas_call(
        _norm_body,
        grid=(m // tm,),
        in_specs=[pl.BlockSpec((tm, d), lambda i: (i, 0)), pl.BlockSpec((1, d), lambda i: (0, 0))],
        out_specs=pl.BlockSpec((tm, d), lambda i: (i, 0)),
        out_shape=jax.ShapeDtypeStruct((m, d), BF16),
        compiler_params=_params("parallel"),
        name="mixer_norm",
    )(x, g.reshape(1, d).astype(F32))


def _values_by_key_block(arr, col, batch, seq, tq):
    width = N_HEADS * HEAD_DIM
    return arr[:, col * width:(col + 1) * width].reshape(batch, seq // tq, tq, width).swapaxes(2, 3)


def _sb_body(q_ref, k_ref, vt_ref, tri_ref, o_ref, qs_ref, z_ref, lb_ref, wb_ref, acc_ref, *, tq):
    i = pl.program_id(1)
    shape = (tq, tq)
    before = lax.broadcasted_iota(I32, shape, 0) < lax.broadcasted_iota(I32, shape, 1)
    heads = [slice(h * HEAD_DIM, (h + 1) * HEAD_DIM) for h in range(N_HEADS)]
    qs_ref[...] = (q_ref[...].astype(F32) * (HEAD_DIM ** -0.5 * LOG2E)).astype(BF16)
    acc_ref[...] = jnp.zeros(acc_ref.shape, F32)

    def tile(j, carry, diagonal):
        start = pl.multiple_of(j * tq, tq)
        for h, hs in enumerate(heads):
            z = lax.dot_general(k_ref[pl.ds(start, tq), hs], qs_ref[:, hs], _NT,
                                preferred_element_type=F32)
            log_keep = jnp.minimum(-z, 0.0) - jnp.log2(1.0 + jnp.exp2(-jnp.abs(z)))
            if diagonal:
                log_keep = jnp.where(before, log_keep, 0.0)
            z_ref[h] = z
            lb_ref[h] = log_keep.astype(BF16)
        new_carry = []
        for h in range(N_HEADS):
            c = jnp.dot(tri_ref[...], lb_ref[h], preferred_element_type=F32) + carry[h:h + 1, :]
            w = jnp.exp2(z_ref[h] + c)
            if diagonal:
                w = jnp.where(before, w, 0.0)
            wb_ref[h] = w.astype(BF16)
            new_carry.append(c[0:1, :])
        for h, hs in enumerate(heads):
            acc_ref[h] += jnp.dot(vt_ref[j, hs, :], wb_ref[h], preferred_element_type=F32)
        return jnp.concatenate(new_carry, axis=0)

    carry = tile(i, jnp.zeros((N_HEADS, tq), F32), True)

    def cond(state):
        j, live, _ = state
        return jnp.logical_and(j >= 0, live > EXP2_UNDERFLOW)

    def body(state):
        j, _, carry = state
        carry = tile(j, carry, False)
        return j - 1, jnp.max(carry), carry

    lax.while_loop(cond, body, (i - 1, jnp.max(carry), carry))
    for h, hs in enumerate(heads):
        o_ref[:, hs] = acc_ref[h].T.astype(o_ref.dtype)


def _sb_attention(qkv, batch, seq, *, q_col, k_col, v_col, tq=256):
    nq = seq // tq
    width = N_HEADS * HEAD_DIM
    v_t = _values_by_key_block(qkv, v_col, batch, seq, tq)
    tri = (jnp.arange(tq)[None, :] >= jnp.arange(tq)[:, None]).astype(BF16)
    resident = pl.Buffered(1)
    return pl.pallas_call(
        functools.partial(_sb_body, tq=tq),
        grid=(batch, nq),
        in_specs=[
            pl.BlockSpec((tq, width), lambda b, i: (b * nq + i, q_col)),
            pl.BlockSpec((seq, width), lambda b, i: (b, k_col), pipeline_mode=resident),
            pl.BlockSpec((None, nq, width, tq), lambda b, i: (b, 0, 0, 0), pipeline_mode=resident),
            pl.BlockSpec((tq, tq), lambda b, i: (0, 0), pipeline_mode=resident),
        ],
        out_specs=pl.BlockSpec((tq, width), lambda b, i: (b * nq + i, 0)),
        out_shape=jax.ShapeDtypeStruct((batch * seq, width), BF16),
        scratch_shapes=[
            pltpu.VMEM((tq, width), BF16),
            pltpu.VMEM((N_HEADS, tq, tq), F32),
            pltpu.VMEM((N_HEADS, tq, tq), BF16),
            pltpu.VMEM((N_HEADS, tq, tq), BF16),
            pltpu.VMEM((N_HEADS, HEAD_DIM, tq), F32),
        ],
        compiler_params=_params("parallel", "arbitrary"),
        name="sb_attention",
    )(qkv, qkv, v_t, tri)


def _bucket_thresholds():
    nb = N_BUCKETS // 2
    max_exact = nb // 2
    span = nb - max_exact
    out = []
    for k in range(1, span):
        n = max_exact
        while n ** span * max_exact ** k < MAX_DISTANCE ** k * max_exact ** span:
            n += 1
        out.append(n)
    return max_exact, out


def _bias_body(rb_ref, o_ref, *, tq):
    nb = N_BUCKETS // 2
    max_exact, steps = _bucket_thresholds()
    shape = (2 * tq, tq)
    rel = lax.broadcasted_iota(I32, shape, 0) - lax.broadcasted_iota(I32, shape, 1) - tq
    n = jnp.abs(rel)
    large = jnp.full(shape, max_exact, I32)
    for t in steps:
        large = large + (n >= t).astype(I32)
    bucket = jnp.where(rel > 0, nb, 0) + jnp.where(n < max_exact, n, large)
    for h in range(N_HEADS):
        val = jnp.zeros(shape, F32)
        for b in range(N_BUCKETS):
            val = jnp.where(bucket == b, rb_ref[b, h], val)
        o_ref[h] = (val - rb_ref[nb - 1, h]) * LOG2E


def _near_bias(rel_bias, tq):
    return pl.pallas_call(
        functools.partial(_bias_body, tq=tq),
        in_specs=[pl.BlockSpec(memory_space=pltpu.SMEM)],
        out_specs=pl.BlockSpec(memory_space=pltpu.VMEM),
        out_shape=jax.ShapeDtypeStruct((N_HEADS, 2 * tq, tq), F32),
        compiler_params=pltpu.CompilerParams(vmem_limit_bytes=VMEM_LIMIT_BYTES),
        name="dsa_near_bias",
    )(rel_bias.astype(F32))


def _float_key(x):
    bits = lax.bitcast_convert_type(x, I32)
    return bits ^ ((bits >> 31) & 0x7FFFFFFF)


def _key_float(key):
    return lax.bitcast_convert_type(key ^ ((key >> 31) & 0x7FFFFFFF), F32)


def _dsa_body(qd_ref, qi_ref, wq_ref, kd_ref, vt_ref, ki_ref, bias_ref, tri_ref, o_ref,
              sc_ref, qs_ref, m_ref, l_ref, acc_ref, lg_ref, p_ref, *, tq, top):
    i = pl.program_id(1)
    shape = (tq, tq)
    key_row = lax.broadcasted_iota(I32, shape, 0)
    qry_col = lax.broadcasted_iota(I32, shape, 1)
    visible = key_row // CHUNK <= qry_col // CHUNK

    w_t = (wq_ref[...] * (IDX_DIM ** -0.5 * IDX_HEADS ** -0.5)).T

    heads_per_vreg = LANES // IDX_DIM

    def score_tile(j, lo, hi, diagonal):
        start = pl.multiple_of(j * tq, tq)
        ki = [ki_ref[pl.ds(start, tq), c * LANES:(c + 1) * LANES].astype(BF16)
              for c in range(heads_per_vreg)]
        s = jnp.zeros(shape, F32)
        for h in range(IDX_HEADS):
            g, c = divmod(h, heads_per_vreg)
            d = lax.dot_general(ki[c], qi_ref[:, g * LANES:(g + 1) * LANES], _NT,
                                preferred_element_type=F32)
            s = s + w_t[h:h + 1, :] * jnp.maximum(d, 0.0)
        if diagonal:
            lo = jnp.minimum(lo, jnp.min(jnp.where(visible, s, jnp.inf), axis=0, keepdims=True))
            s = jnp.where(visible, s, -jnp.inf)
        else:
            lo = jnp.minimum(lo, jnp.min(s, axis=0, keepdims=True))
        hi = jnp.maximum(hi, jnp.max(s, axis=0, keepdims=True))
        sc_ref[j] = s
        return lo, hi

    lo, hi = lax.fori_loop(
        0, i, lambda j, c: score_tile(j, c[0], c[1], False),
        (jnp.full((1, tq), jnp.inf, F32), jnp.full((1, tq), -jnp.inf, F32)))
    lo, hi = score_tile(i, lo, hi, True)

    ones = jnp.ones((2 * SUBLANES, tq), BF16)

    def count_ge(t):
        def step(j, acc):
            hit = (sc_ref[j] >= t).astype(I32)
            for g in range(tq // SUBLANES):
                acc = acc + hit[g * SUBLANES:(g + 1) * SUBLANES, :]
            return acc
        acc = lax.fori_loop(0, i + 1, step, jnp.zeros((SUBLANES, tq), I32))
        return jnp.sum(acc, axis=0, keepdims=True)

    qry = lax.broadcasted_iota(I32, (1, tq), 1)
    n_visible = i * tq + (qry // CHUNK + 1) * CHUNK
    k_lo0 = _float_key(lo)
    k_hi0 = _float_key(hi) + 1
    wanted = n_visible > top
    adjacent0 = k_lo0 + 1 == k_hi0
    select_all = jnp.full((1, tq), _FLOAT_KEY_LOWEST, I32)

    def search_cond(state):
        it, active = state[0], state[1]
        return jnp.logical_and(it < _MAX_SEARCH_STEPS, jnp.max(active) > 0)

    def search_step(it, active, k_lo, k_hi, c_hi, k_thr, tied):
        by_value = _float_key(0.5 * _key_float(k_lo) + 0.5 * _key_float(k_hi))
        by_bits = (k_lo & k_hi) + ((k_lo ^ k_hi) >> 1)
        k_t = jnp.where(it < _VALUE_STEPS, by_value, by_bits)
        k_t = jnp.minimum(jnp.maximum(k_t, k_lo + 1), k_hi - 1)
        c = count_ge(_key_float(k_t))
        found = jnp.logical_and(active, c == top)
        above = jnp.logical_and(active, c > top)
        below = jnp.logical_and(active, c < top)
        k_lo = jnp.where(above, k_t, k_lo)
        k_hi, c_hi = jnp.where(below, k_t, k_hi), jnp.where(below, c, c_hi)
        closed = jnp.logical_and(jnp.logical_or(above, below), k_lo + 1 == k_hi)
        k_thr = jnp.where(found, k_t, jnp.where(closed, k_lo, k_thr))
        tied = jnp.logical_or(tied, closed)
        active = jnp.logical_and(active, jnp.logical_not(jnp.logical_or(found, closed)))
        return active, k_lo, k_hi, c_hi, k_thr, tied

    def search_body(state):
        it, active, k_lo, k_hi, c_hi, k_thr, tied = state
        active, tied = active > 0, tied > 0
        for _ in range(_STEPS_PER_CHECK):
            active, k_lo, k_hi, c_hi, k_thr, tied = search_step(
                it, active, k_lo, k_hi, c_hi, k_thr, tied)
            it = it + 1
        return it, active.astype(I32), k_lo, k_hi, c_hi, k_thr, tied.astype(I32)

    tied0 = jnp.logical_and(wanted, adjacent0)
    state = lax.while_loop(search_cond, search_body, (
        jnp.int32(0), jnp.logical_and(wanted, jnp.logical_not(adjacent0)).astype(I32),
        k_lo0, k_hi0, jnp.zeros((1, tq), I32),
        jnp.where(tied0, k_lo0, select_all), tied0.astype(I32)))
    _, _, _, _, c_hi, k_thr, tied = state
    tied = tied > 0
    thr = _key_float(k_thr)

    def plain_mask(j, carry):
        sc_ref[j] = jnp.where(sc_ref[j] >= thr, 0.0, NEG_BIG)
        return carry

    def tie_mask(j, seen):
        s = sc_ref[j]
        equal = s == thr
        rank = jnp.dot(tri_ref[...], equal.astype(BF16), preferred_element_type=F32) + seen
        quota = jnp.where(tied, (top - c_hi).astype(F32), jnp.inf)
        keep_equal = jnp.where(rank < quota, 0.0, NEG_BIG)
        sc_ref[j] = jnp.where(s > thr, 0.0, jnp.where(equal, keep_equal, NEG_BIG))
        return seen + jnp.sum(equal.astype(F32), axis=0, keepdims=True)

    def with_ties():
        lax.fori_loop(0, i + 1, tie_mask, jnp.zeros((1, tq), F32))
        return jnp.int32(0)

    def without_ties():
        return lax.fori_loop(0, i + 1, plain_mask, jnp.int32(0))

    lax.cond(jnp.max(tied.astype(I32)) > 0, with_ties, without_ties)

    qs_ref[...] = (qd_ref[...].astype(F32) * (HEAD_DIM ** -0.5 * LOG2E)).astype(BF16)
    m_ref[...] = jnp.full(m_ref.shape, NEG_BIG, F32)
    l_ref[...] = jnp.zeros(l_ref.shape, F32)
    acc_ref[...] = jnp.zeros(acc_ref.shape, F32)

    heads = [slice(h * HEAD_DIM, (h + 1) * HEAD_DIM) for h in range(N_HEADS)]

    def att_tile(j, near):
        start = pl.multiple_of(j * tq, tq)
        mask = sc_ref[j]
        tile_max = []
        for h, hs in enumerate(heads):
            lg = lax.dot_general(kd_ref[pl.ds(start, tq), hs], qs_ref[:, hs], _NT,
                                 preferred_element_type=F32)
            if near is not None:
                lg = lg + bias_ref[h, near * tq:(near + 1) * tq, :]
            lg = lg + mask
            lg_ref[h] = lg
            tile_max.append(jnp.max(lg, axis=0, keepdims=True))
        m_old = m_ref[...]
        m_new = jnp.maximum(m_old, jnp.concatenate(tile_max, axis=0))
        alpha = jnp.exp2(m_old - m_new)
        m_ref[...] = m_new
        for h in range(N_HEADS):
            p_ref[h] = jnp.exp2(lg_ref[h] - m_new[h:h + 1, :]).astype(BF16)
        denom = []
        for h, hs in enumerate(heads):
            v_ext = jnp.concatenate([vt_ref[j, hs, :], ones], axis=0)
            pv = jnp.dot(v_ext, p_ref[h], preferred_element_type=F32)
            acc_ref[h] = alpha[h:h + 1, :] * acc_ref[h] + pv[:HEAD_DIM]
            denom.append(pv[HEAD_DIM:HEAD_DIM + 1])
        l_ref[...] = alpha * l_ref[...] + jnp.concatenate(denom, axis=0)

    def far_step(j, carry):
        att_tile(j, None)
        return carry

    lax.fori_loop(0, i - 1, far_step, 0)

    @pl.when(i >= 1)
    def _():
        att_tile(i - 1, 0)

    att_tile(i, 1)
    for h in range(N_HEADS):
        o = acc_ref[h] / l_ref[h:h + 1, :]
        o_ref[:, h * HEAD_DIM:(h + 1) * HEAD_DIM] = o.T.astype(o_ref.dtype)


def _dsa_attention(main, small, bias, batch, seq, *, qd_col, kd_col, vd_col, qi_col, tq=256):
    nq = seq // tq
    width = N_HEADS * HEAD_DIM
    assert IDX_HEADS * IDX_DIM == width
    key_copies = LANES // IDX_DIM
    top = min(TOPK_MAX, seq // 4)
    v_t = _values_by_key_block(main, vd_col, batch, seq, tq)
    tri = (jnp.arange(tq)[None, :] < jnp.arange(tq)[:, None]).astype(BF16)
    resident = pl.Buffered(1)
    return pl.pallas_call(
        functools.partial(_dsa_body, tq=tq, top=top),
        grid=(batch, nq),
        in_specs=[
            pl.BlockSpec((tq, width), lambda b, i: (b * nq + i, qd_col)),
            pl.BlockSpec((tq, IDX_HEADS * IDX_DIM), lambda b, i: (b * nq + i, qi_col)),
            pl.BlockSpec((tq, LANES), lambda b, i: (b * nq + i, key_copies)),
            pl.BlockSpec((seq, width), lambda b, i: (b, kd_col), pipeline_mode=resident),
            pl.BlockSpec((None, nq, width, tq), lambda b, i: (b, 0, 0, 0), pipeline_mode=resident),
            pl.BlockSpec((seq, key_copies * LANES), lambda b, i: (b, 0), pipeline_mode=resident),
            pl.BlockSpec((N_HEADS, 2 * tq, tq), lambda b, i: (0, 0, 0), pipeline_mode=resident),
            pl.BlockSpec((tq, tq), lambda b, i: (0, 0), pipeline_mode=resident),
        ],
        out_specs=pl.BlockSpec((tq, width), lambda b, i: (b * nq + i, 0)),
        out_shape=jax.ShapeDtypeStruct((batch * seq, width), BF16),
        scratch_shapes=[
            pltpu.VMEM((nq, tq, tq), F32),
            pltpu.VMEM((tq, width), BF16),
            pltpu.VMEM((N_HEADS, tq), F32),
            pltpu.VMEM((N_HEADS, tq), F32),
            pltpu.VMEM((N_HEADS, HEAD_DIM, tq), F32),
            pltpu.VMEM((N_HEADS, tq, tq), F32),
            pltpu.VMEM((N_HEADS, tq, tq), BF16),
        ],
        compiler_params=_params("parallel", "arbitrary"),
        name="dsa_attention",
    )(main, main, small, main, v_t, small, bias, tri)


def _merge_body(osb_ref, ods_ref, wsb_ref, wds_ref, gsb_ref, gds_ref, bsb_ref, bds_ref, wo_ref,
                x_ref, o_ref):
    p_sb = jnp.dot(osb_ref[...], wsb_ref[...], preferred_element_type=F32)
    p_ds = jnp.dot(ods_ref[...], wds_ref[...], preferred_element_type=F32)
    g_sb = jax.nn.sigmoid(gsb_ref[...].astype(F32) + bsb_ref[...])
    g_ds = jax.nn.sigmoid(gds_ref[...].astype(F32) + bds_ref[...])
    merged = (g_sb * p_sb + g_ds * p_ds).astype(BF16)
    o_ref[...] = x_ref[...] + jnp.dot(merged, wo_ref[...], preferred_element_type=F32)


def _merge_out(o_sb, o_ds, w_sb, w_ds, w_out, proj, gate_offset, b_gate, x, *, tm=256):
    m, k = o_sb.shape
    d = w_sb.shape[1]
    tm = min(tm, m)
    assert gate_offset % d == 0 and m % tm == 0
    g = gate_offset // d
    b_gate = b_gate.reshape(1, 2 * d).astype(F32)
    resident = pl.Buffered(1)
    return pl.pallas_call(
        _merge_body,
        grid=(m // tm,),
        in_specs=[
            pl.BlockSpec((tm, k), lambda i: (i, 0)),
            pl.BlockSpec((tm, k), lambda i: (i, 0)),
            pl.BlockSpec((k, d), lambda i: (0, 0), pipeline_mode=resident),
            pl.BlockSpec((k, d), lambda i: (0, 0), pipeline_mode=resident),
            pl.BlockSpec((tm, d), lambda i: (i, g)),
            pl.BlockSpec((tm, d), lambda i: (i, g + 1)),
            pl.BlockSpec((1, d), lambda i: (0, 0), pipeline_mode=resident),
            pl.BlockSpec((1, d), lambda i: (0, 1), pipeline_mode=resident),
            pl.BlockSpec((d, d), lambda i: (0, 0), pipeline_mode=resident),
            pl.BlockSpec((tm, d), lambda i: (i, 0)),
        ],
        out_specs=pl.BlockSpec((tm, d), lambda i: (i, 0)),
        out_shape=jax.ShapeDtypeStruct((m, d), F32),
        compiler_params=_params("parallel"),
        name="merge_out_proj",
    )(o_sb, o_ds, w_sb, w_ds, proj, proj, b_gate, b_gate, w_out, x)


def _cross_body(x_ref, g_ref, wq_ref, km_ref, vm_ref, wo_ref, gn_ref, o_ref, hn_ref):
    x = x_ref[...]
    h = _rms(x, g_ref[...]).astype(BF16)
    q = jnp.dot(h, wq_ref[...], preferred_element_type=F32) * HEAD_DIM ** -0.5
    q = q.astype(BF16)
    outs = []
    for hh in range(MEM_HEADS):
        hs = slice(hh * HEAD_DIM, (hh + 1) * HEAD_DIM)
        lg = lax.dot_general(q[:, hs], km_ref[:, hs], _NT, preferred_element_type=F32)
        p = jnp.exp(lg - jnp.max(lg, axis=1, keepdims=True))
        o = jnp.dot(p.astype(BF16), vm_ref[:, hs], preferred_element_type=F32)
        outs.append((o / jnp.sum(p, axis=1, keepdims=True)).astype(BF16))
    o = jnp.concatenate(outs, axis=1)
    y = x + jnp.dot(o, wo_ref[...], preferred_element_type=F32)
    o_ref[...] = y
    hn_ref[...] = _rms(y, gn_ref[...]).astype(hn_ref.dtype)


def _cross_attention(x, kv, g_cross, w_cq, w_co, g_next, batch, seq, *, tm=512):
    m, d = x.shape
    n_mem = kv.shape[0] // batch
    width = MEM_HEADS * HEAD_DIM
    tm = min(tm, seq)
    nt = seq // tm
    row_tile = pl.BlockSpec((tm, d), lambda b, i: (b * nt + i, 0))
    gain = pl.BlockSpec((1, d), lambda b, i: (0, 0))
    return pl.pallas_call(
        _cross_body,
        grid=(batch, nt),
        in_specs=[
            row_tile,
            gain,
            pl.BlockSpec((d, width), lambda b, i: (0, 0)),
            pl.BlockSpec((n_mem, width), lambda b, i: (b, 0)),
            pl.BlockSpec((n_mem, width), lambda b, i: (b, 1)),
            pl.BlockSpec((width, d), lambda b, i: (0, 0)),
            gain,
        ],
        out_specs=[row_tile, row_tile],
        out_shape=[jax.ShapeDtypeStruct((m, d), F32), jax.ShapeDtypeStruct((m, d), BF16)],
        compiler_params=_params("parallel", "parallel"),
        name="cross_attention",
    )(x, g_cross.reshape(1, d).astype(F32), w_cq, kv, kv, w_co, g_next.reshape(1, d).astype(F32))


def _delayed(u, tail, shift):
    rolled = pltpu.roll(u, shift, axis=0)
    row = lax.broadcasted_iota(I32, tail.shape, 0)
    head = jnp.where(row < shift, pltpu.roll(tail, shift, axis=0), rolled[:SUBLANES])
    return jnp.concatenate([head, rolled[SUBLANES:]], axis=0)


def _ffn_up_body(h_ref, wa_ref, wv_ref, cwa_ref, cwv_ref, cba_ref, cbv_ref, o_ref,
                 wab_ref, wvb_ref, halo_ref, *, tiles_per_seq):
    i = pl.program_id(1)

    @pl.when(i == 0)
    def _():
        wab_ref[...] = wa_ref[...].astype(BF16)
        wvb_ref[...] = wv_ref[...].astype(BF16)

    h = h_ref[...]
    tm = h.shape[0]
    sequence_start = i % tiles_per_seq == 0

    def conv(wb_ref, cw_ref, cb_ref, slot):
        u = jnp.dot(h, wb_ref[...], preferred_element_type=F32)
        tail = jnp.where(sequence_start, 0.0, halo_ref[slot])
        halo_ref[slot] = u[tm - SUBLANES:, :]
        c = cb_ref[...] + cw_ref[CONV_WIDTH - 1:CONV_WIDTH, :] * u
        for tap in range(CONV_WIDTH - 1):
            c = c + cw_ref[tap:tap + 1, :] * _delayed(u, tail, CONV_WIDTH - 1 - tap)
        return c

    a = conv(wab_ref, cwa_ref, cba_ref, 0)
    val = conv(wvb_ref, cwv_ref, cbv_ref, 1)
    o_ref[...] = (jax.nn.gelu(a) * val).astype(o_ref.dtype)


def _ffn_up_gate(h, w_up, conv_w, conv_b, seq, *, tm=1024, tn=512):
    m, d = h.shape
    two_ff = w_up.shape[1]
    d_ff = two_ff // 2
    tm, tn = min(tm, seq), min(tn, d_ff)
    assert seq % tm == 0 and d_ff % tn == 0 and tm >= SUBLANES >= CONV_WIDTH - 1
    nf = d_ff // tn
    conv_w = conv_w.astype(F32)
    conv_b = conv_b.reshape(1, two_ff).astype(F32)
    return pl.pallas_call(
        functools.partial(_ffn_up_body, tiles_per_seq=seq // tm),
        grid=(nf, m // tm),
        in_specs=[
            pl.BlockSpec((tm, d), lambda j, i: (i, 0)),
            pl.BlockSpec((d, tn), lambda j, i: (0, j)),
            pl.BlockSpec((d, tn), lambda j, i: (0, nf + j)),
            pl.BlockSpec((CONV_WIDTH, tn), lambda j, i: (0, j)),
            pl.BlockSpec((CONV_WIDTH, tn), lambda j, i: (0, nf + j)),
            pl.BlockSpec((1, tn), lambda j, i: (0, j)),
            pl.BlockSpec((1, tn), lambda j, i: (0, nf + j)),
        ],
        out_specs=pl.BlockSpec((tm, tn), lambda j, i: (i, j)),
        out_shape=jax.ShapeDtypeStruct((m, d_ff), BF16),
        scratch_shapes=[pltpu.VMEM((d, tn), BF16), pltpu.VMEM((d, tn), BF16),
                        pltpu.VMEM((2, SUBLANES, tn), F32)],
        compiler_params=_params("parallel", "arbitrary"),
        name="ffn_up_conv_gate",
    )(h, w_up, w_up, conv_w, conv_w, conv_b, conv_b)


def _ffn_down_body(a_ref, w_ref, x_ref, g_ref, o_ref, *, final_norm):
    k = pl.program_id(1)

    @pl.when(k == 0)
    def _():
        o_ref[...] = x_ref[...]

    o_ref[...] += jnp.dot(a_ref[...], w_ref[...], preferred_element_type=F32)

    if final_norm:
        @pl.when(k == pl.num_programs(1) - 1)
        def _():
            o_ref[...] = _rms(o_ref[...], g_ref[...])


def _ffn_down(a, w, x, g_final, *, tm=512, tk=1024):
    m, kdim = a.shape
    d = w.shape[1]
    tm, tk = min(tm, m), min(tk, kdim)
    assert m % tm == 0 and kdim % tk == 0
    final_norm = g_final is not None
    g = (g_final if final_norm else jnp.ones((d,), F32)).reshape(1, d).astype(F32)
    return pl.pallas_call(
        functools.partial(_ffn_down_body, final_norm=final_norm),
        grid=(m // tm, kdim // tk),
        in_specs=[
            pl.BlockSpec((tm, tk), lambda i, k: (i, k)),
            pl.BlockSpec((tk, d), lambda i, k: (k, 0)),
            pl.BlockSpec((tm, d), lambda i, k: (i, 0)),
            pl.BlockSpec((1, d), lambda i, k: (0, 0)),
        ],
        out_specs=pl.BlockSpec((tm, d), lambda i, k: (i, 0)),
        out_shape=jax.ShapeDtypeStruct((m, d), F32),
        compiler_params=_params("parallel", "arbitrary"),
        name="ffn_down",
    )(a, w, x, g)


def _layer(x, mem, g_mix, w_in, b_gate, w_proj_sb, w_proj_dsa, w_out, rel_bias,
           g_cross, g_mem, w_cq, w_ckv, w_co, g_ffn, w_up, conv_w, conv_b, w_down, g_final, batch, seq):
    d = x.shape[1]
    width = N_HEADS * HEAD_DIM
    idx_w = IDX_HEADS * IDX_DIM
    o_qi = 6 * width
    o_ki = o_qi + idx_w
    o_wi = o_ki + IDX_DIM
    o_g = o_wi + IDX_HEADS

    h = _rmsnorm_bf16(x, g_mix)
    w_in_t = w_in.T
    main = _matmul_ws(h, w_in_t, name="in_proj_main", n_blocks=o_ki // width,
                      out_dtype=BF16, tm=1024, tn=width)
    gates = _matmul_ws(h, w_in_t[o_g:], name="in_proj_gates", n_blocks=2 * d // width,
                       out_dtype=BF16, tm=1024, tn=width)
    zeros = jnp.zeros((d, LANES - IDX_DIM), F32)
    w_small = jnp.concatenate([
        w_in[:, o_ki:o_wi], zeros, zeros, w_in[:, o_ki:o_wi],
        jnp.pad(w_in[:, o_wi:o_g], ((0, 0), (0, LANES - IDX_HEADS)))], axis=1).astype(BF16)
    small = _matmul(h, w_small, name="in_proj_index", out_dtype=F32, tm=1024)

    o_sb = _sb_attention(main, batch, seq, q_col=0, k_col=1, v_col=2)
    tq = 256
    bias = _near_bias(rel_bias, tq)
    o_ds = _dsa_attention(main, small, bias, batch, seq, qd_col=3, kd_col=4, vd_col=5, qi_col=6, tq=tq)

    x = _merge_out(o_sb, o_ds, w_proj_sb.astype(BF16), w_proj_dsa.astype(BF16), w_out.astype(BF16),
                   gates, 0, b_gate, x)

    kv = _matmul(mem, w_ckv.astype(BF16), name="mem_kv_proj", gain=g_mem, out_dtype=BF16, tm=512)
    x, h_ffn = _cross_attention(x, kv, g_cross, w_cq.astype(BF16), w_co.astype(BF16), g_ffn, batch, seq)

    act = _ffn_up_gate(h_ffn, w_up, conv_w, conv_b, seq)
    return _ffn_down(act, w_down.astype(BF16), x, g_final)


def kernel(x, mem, g_mix, w_in, b_gate, w_proj_sb, w_proj_dsa, w_out, rel_bias, g_cross, g_mem,
           w_cq, w_ckv, w_co, g_ffn, w_up, conv_w, conv_b, w_down, g_final):
    batch, seq, d = x.shape
    h = x.reshape(batch * seq, d)
    mem2 = mem.reshape(batch * mem.shape[1], d)
    depth = g_mix.shape[0]
    for l in range(depth):
        h = _layer(h, mem2, g_mix[l], w_in[l], b_gate[l], w_proj_sb[l], w_proj_dsa[l], w_out[l],
                   rel_bias, g_cross[l], g_mem[l], w_cq[l], w_ckv[l], w_co[l], g_ffn[l], w_up[l],
                   conv_w[l], conv_b[l], w_down[l], g_final if l == depth - 1 else None, batch, seq)
    return h.reshape(batch, seq, d)
```

```python
import functools

import jax
import jax.numpy as jnp
from jax import lax
from jax.experimental import pallas as pl
from jax.experimental.pallas import tpu as pltpu

F32, BF16, I32 = jnp.float32, jnp.bfloat16, jnp.int32

EPS = 1e-6
HEAD_DIM = 128
N_HEADS = 8
IDX_HEADS = 16
IDX_DIM = 64
CHUNK = 64
TOPK_MAX = 256
N_BUCKETS = 32
MAX_DISTANCE = 128
MEM_HEADS = 4
CONV_WIDTH = 3

LANES = 128
SUBLANES = 8
VMEM_LIMIT_BYTES = 56 * 1024 * 1024
NEG_BIG = -1e30
EXP2_UNDERFLOW = -151.0
LOG2E = 1.4426950408889634
KEY_BITS = 32
SIGN_BIT = -(2 ** 31)

_NT = (((1,), (1,)), ((), ()))


def _params(*sem):
    return pltpu.CompilerParams(dimension_semantics=sem, vmem_limit_bytes=VMEM_LIMIT_BYTES)


def _rms(x, g):
    inv = lax.rsqrt(jnp.mean(x * x, axis=-1, keepdims=True) + EPS)
    return x * inv * g


def _mm_body(*refs, has_gain):
    if has_gain:
        a_ref, g_ref, w_ref, o_ref = refs
        a = _rms(a_ref[...].astype(F32), g_ref[...]).astype(BF16)
    else:
        a_ref, w_ref, o_ref = refs
        a = a_ref[...]
    o_ref[...] = jnp.dot(a, w_ref[...], preferred_element_type=F32).astype(o_ref.dtype)


def _matmul(a, w, *, name, gain=None, out_dtype, tm):
    m, k = a.shape
    n = w.shape[1]
    tm = min(tm, m)
    assert m % tm == 0, (m, tm)
    in_specs = [pl.BlockSpec((tm, k), lambda i: (i, 0))]
    args = [a]
    if gain is not None:
        in_specs.append(pl.BlockSpec((1, k), lambda i: (0, 0)))
        args.append(gain.reshape(1, k).astype(F32))
    in_specs.append(pl.BlockSpec((k, n), lambda i: (0, 0)))
    args.append(w)
    return pl.pallas_call(
        functools.partial(_mm_body, has_gain=gain is not None),
        grid=(m // tm,),
        in_specs=in_specs,
        out_specs=pl.BlockSpec((tm, n), lambda i: (i, 0)),
        out_shape=jax.ShapeDtypeStruct((m, n), out_dtype),
        compiler_params=_params("parallel"),
        name=name,
    )(*args)


def _mm_ws_body(a_ref, wt_ref, o_ref, wb_ref):
    @pl.when(pl.program_id(1) == 0)
    def _():
        wb_ref[...] = wt_ref[...].astype(BF16)

    o_ref[...] = lax.dot_general(a_ref[...], wb_ref[...], _NT,
                                 preferred_element_type=F32).astype(o_ref.dtype)


def _matmul_ws(a, w_t, *, name, n_blocks, out_dtype, tm, tn):
    m, k = a.shape
    tm = min(tm, m)
    assert m % tm == 0 and n_blocks * tn <= w_t.shape[0] and w_t.shape[1] == k
    return pl.pallas_call(
        _mm_ws_body,
        grid=(n_blocks, m // tm),
        in_specs=[
            pl.BlockSpec((tm, k), lambda j, i: (i, 0)),
            pl.BlockSpec((tn, k), lambda j, i: (j, 0)),
        ],
        out_specs=pl.BlockSpec((tm, tn), lambda j, i: (i, j)),
        out_shape=jax.ShapeDtypeStruct((m, n_blocks * tn), out_dtype),
        scratch_shapes=[pltpu.VMEM((tn, k), BF16)],
        compiler_params=_params("parallel", "arbitrary"),
        name=name,
    )(a, w_t)


def _norm_body(x_ref, g_ref, o_ref):
    o_ref[...] = _rms(x_ref[...], g_ref[...]).astype(o_ref.dtype)


def _rmsnorm_bf16(x, g, *, tm=512):
    m, d = x.shape
    tm = min(tm, m)
    return pl.pallas_call(
        _norm_body,
        grid=(m // tm,),
        in_specs=[pl.BlockSpec((tm, d), lambda i: (i, 0)), pl.BlockSpec((1, d), lambda i: (0, 0))],
        out_specs=pl.BlockSpec((tm, d), lambda i: (i, 0)),
        out_shape=jax.ShapeDtypeStruct((m, d), BF16),
        compiler_params=_params("parallel"),
        name="mixer_norm",
    )(x, g.reshape(1, d).astype(F32))


def _values_by_key_block(arr, col, batch, seq, tq):
    width = N_HEADS * HEAD_DIM
    return arr[:, col * width:(col + 1) * width].reshape(batch, seq // tq, tq, width).swapaxes(2, 3)


def _sb_body(q_ref, k_ref, vt_ref, tri_ref, o_ref, qs_ref, z_ref, lb_ref, wb_ref, acc_ref, *, tq):
    i = pl.program_id(1)
    shape = (tq, tq)
    before = lax.broadcasted_iota(I32, shape, 0) < lax.broadcasted_iota(I32, shape, 1)
    heads = [slice(h * HEAD_DIM, (h + 1) * HEAD_DIM) for h in range(N_HEADS)]
    qs_ref[...] = (q_ref[...].astype(F32) * (HEAD_DIM ** -0.5 * LOG2E)).astype(BF16)
    acc_ref[...] = jnp.zeros(acc_ref.shape, F32)

    def tile(j, carry, diagonal):
        start = pl.multiple_of(j * tq, tq)
        for h, hs in enumerate(heads):
            z = lax.dot_general(k_ref[pl.ds(start, tq), hs], qs_ref[:, hs], _NT,
                                preferred_element_type=F32)
            log_keep = jnp.minimum(-z, 0.0) - jnp.log2(1.0 + jnp.exp2(-jnp.abs(z)))
            if diagonal:
                log_keep = jnp.where(before, log_keep, 0.0)
            z_ref[h] = z
            lb_ref[h] = log_keep.astype(BF16)
        new_carry = []
        for h in range(N_HEADS):
            c = jnp.dot(tri_ref[...], lb_ref[h], preferred_element_type=F32) + carry[h:h + 1, :]
            w = jnp.exp2(z_ref[h] + c)
            if diagonal:
                w = jnp.where(before, w, 0.0)
            wb_ref[h] = w.astype(BF16)
            new_carry.append(c[0:1, :])
        for h, hs in enumerate(heads):
            acc_ref[h] += jnp.dot(vt_ref[j, hs, :], wb_ref[h], preferred_element_type=F32)
        return jnp.concatenate(new_carry, axis=0)

    carry = tile(i, jnp.zeros((N_HEADS, tq), F32), True)

    def cond(state):
        j, live, _ = state
        return jnp.logical_and(j >= 0, live > EXP2_UNDERFLOW)

    def body(state):
        j, _, carry = state
        carry = tile(j, carry, False)
        return j - 1, jnp.max(carry), carry

    lax.while_loop(cond, body, (i - 1, jnp.max(carry), carry))
    for h, hs in enumerate(heads):
        o_ref[:, hs] = acc_ref[h].T.astype(o_ref.dtype)


def _sb_attention(qkv, batch, seq, *, q_col, k_col, v_col, tq=256):
    nq = seq // tq
    width = N_HEADS * HEAD_DIM
    v_t = _values_by_key_block(qkv, v_col, batch, seq, tq)
    tri = (jnp.arange(tq)[None, :] >= jnp.arange(tq)[:, None]).astype(BF16)
    resident = pl.Buffered(1)
    return pl.pallas_call(
        functools.partial(_sb_body, tq=tq),
        grid=(batch, nq),
        in_specs=[
            pl.BlockSpec((tq, width), lambda b, i: (b * nq + i, q_col)),
            pl.BlockSpec((seq, width), lambda b, i: (b, k_col), pipeline_mode=resident),
            pl.BlockSpec((None, nq, width, tq), lambda b, i: (b, 0, 0, 0), pipeline_mode=resident),
            pl.BlockSpec((tq, tq), lambda b, i: (0, 0), pipeline_mode=resident),
        ],
        out_specs=pl.BlockSpec((tq, width), lambda b, i: (b * nq + i, 0)),
        out_shape=jax.ShapeDtypeStruct((batch * seq, width), BF16),
        scratch_shapes=[
            pltpu.VMEM((tq, width), BF16),
            pltpu.VMEM((N_HEADS, tq, tq), F32),
            pltpu.VMEM((N_HEADS, tq, tq), BF16),
            pltpu.VMEM((N_HEADS, tq, tq), BF16),
            pltpu.VMEM((N_HEADS, HEAD_DIM, tq), F32),
        ],
        compiler_params=_params("parallel", "arbitrary"),
        name="sb_attention",
    )(qkv, qkv, v_t, tri)


def _bucket_thresholds():
    nb = N_BUCKETS // 2
    max_exact = nb // 2
    span = nb - max_exact
    out = []
    for k in range(1, span):
        n = max_exact
        while n ** span * max_exact ** k < MAX_DISTANCE ** k * max_exact ** span:
            n += 1
        out.append(n)
    return max_exact, out


def _bias_body(rb_ref, o_ref, *, tq):
    nb = N_BUCKETS // 2
    max_exact, steps = _bucket_thresholds()
    shape = (2 * tq, tq)
    rel = lax.broadcasted_iota(I32, shape, 0) - lax.broadcasted_iota(I32, shape, 1) - tq
    n = jnp.abs(rel)
    large = jnp.full(shape, max_exact, I32)
    for t in steps:
        large = large + (n >= t).astype(I32)
    bucket = jnp.where(rel > 0, nb, 0) + jnp.where(n < max_exact, n, large)
    for h in range(N_HEADS):
        val = jnp.zeros(shape, F32)
        for b in range(N_BUCKETS):
            val = jnp.where(bucket == b, rb_ref[b, h], val)
        o_ref[h] = (val - rb_ref[nb - 1, h]) * LOG2E


def _near_bias(rel_bias, tq):
    return pl.pallas_call(
        functools.partial(_bias_body, tq=tq),
        in_specs=[pl.BlockSpec(memory_space=pltpu.SMEM)],
        out_specs=pl.BlockSpec(memory_space=pltpu.VMEM),
        out_shape=jax.ShapeDtypeStruct((N_HEADS, 2 * tq, tq), F32),
        compiler_params=pltpu.CompilerParams(vmem_limit_bytes=VMEM_LIMIT_BYTES),
        name="dsa_near_bias",
    )(rel_bias.astype(F32))


def _float_key(x):
    bits = lax.bitcast_convert_type(x, I32)
    return bits ^ ((bits >> 31) & 0x7FFFFFFF)


def _key_float(key):
    return lax.bitcast_convert_type(key ^ ((key >> 31) & 0x7FFFFFFF), F32)


def _bit_transpose32(words):
    a = list(words)
    j, m = 16, 0x0000FFFF
    while j:
        mask = jnp.int32(m - (1 << 32) if m >= 1 << 31 else m)
        k = 0
        while k < 32:
            t = (lax.shift_right_logical(a[k], jnp.int32(j)) ^ a[k + j]) & mask
            a[k] = a[k] ^ lax.shift_left(t, jnp.int32(j))
            a[k + j] = a[k + j] ^ t
            k = (k + j + 1) & ~j
        j >>= 1
        m = (m ^ (m << j)) & 0xFFFFFFFF
    return a


def _dsa_body(qd_ref, qi_ref, wq_ref, kd_ref, vt_ref, ki_ref, bias_ref, tri_ref, o_ref,
              sc_ref, plane_ref, qs_ref, m_ref, l_ref, acc_ref, lg_ref, p_ref, *, tq, top):
    i = pl.program_id(1)
    shape = (tq, tq)
    key_row = lax.broadcasted_iota(I32, shape, 0)
    qry_col = lax.broadcasted_iota(I32, shape, 1)
    visible = key_row // CHUNK <= qry_col // CHUNK

    w_t = (wq_ref[...] * (IDX_DIM ** -0.5 * IDX_HEADS ** -0.5)).T

    heads_per_vreg = LANES // IDX_DIM

    def score_tile(j, diagonal):
        start = pl.multiple_of(j * tq, tq)
        ki = [ki_ref[pl.ds(start, tq), c * LANES:(c + 1) * LANES].astype(BF16)
              for c in range(heads_per_vreg)]
        s = jnp.zeros(shape, F32)
        for h in range(IDX_HEADS):
            g, c = divmod(h, heads_per_vreg)
            d = lax.dot_general(ki[c], qi_ref[:, g * LANES:(g + 1) * LANES], _NT,
                                preferred_element_type=F32)
            s = s + w_t[h:h + 1, :] * jnp.maximum(d, 0.0)
        if diagonal:
            s = jnp.where(visible, s, -jnp.inf)
        sc_ref[j] = s
        ukey = _float_key(s) ^ SIGN_BIT
        planes = _bit_transpose32([ukey[g * SUBLANES:(g + 1) * SUBLANES, :] for g in range(KEY_BITS)])
        for b in range(KEY_BITS):
            plane_ref[b, pl.ds(pl.multiple_of(j * SUBLANES, SUBLANES), SUBLANES), :] = planes[b]

    def score_step(j, carry):
        score_tile(j, False)
        return carry

    lax.fori_loop(0, i, score_step, 0)
    score_tile(i, True)

    n_rows = plane_ref.shape[1]
    block_of_row = lax.broadcasted_iota(I32, (n_rows, tq), 0) // SUBLANES
    qry_of_col = lax.broadcasted_iota(I32, (n_rows, tq), 1)
    n_bits = (qry_of_col // CHUNK + 1) * (CHUNK // SUBLANES)
    diag_bits = jnp.where(n_bits >= KEY_BITS, -1, lax.shift_left(jnp.int32(1), n_bits) - 1)
    cand0 = jnp.where(block_of_row < i, -1, jnp.where(block_of_row == i, diag_bits, 0))

    def popcount_rows(words):
        return jnp.sum(lax.population_count(words), axis=0, keepdims=True)

    def bit_step(t, state):
        cand, n_above, thr_bits = state
        b = KEY_BITS - 1 - t
        ones = cand & plane_ref[b]
        n_ones = popcount_rows(ones)
        take = n_above + n_ones >= top
        cand = jnp.where(take, ones, cand ^ ones)
        n_above = jnp.where(take, n_above, n_above + n_ones)
        thr_bits = thr_bits | jnp.where(take, lax.shift_left(jnp.int32(1), b), 0)
        return cand, n_above, thr_bits

    zero = jnp.zeros((1, tq), I32)
    cand, n_above, thr_bits = lax.fori_loop(0, KEY_BITS, bit_step, (cand0, zero, zero))
    qry = lax.broadcasted_iota(I32, (1, tq), 1)
    n_visible = i * tq + (qry // CHUNK + 1) * CHUNK
    wanted = n_visible > top
    thr = jnp.where(wanted, _key_float(thr_bits ^ SIGN_BIT), jnp.finfo(F32).min)
    tied = jnp.logical_and(wanted, n_above + popcount_rows(cand) > top)
    c_hi = n_above

    ones = jnp.ones((2 * SUBLANES, tq), BF16)

    def plain_mask(j, carry):
        sc_ref[j] = jnp.where(sc_ref[j] >= thr, 0.0, NEG_BIG)
        return carry

    def tie_mask(j, seen):
        s = sc_ref[j]
        equal = s == thr
        rank = jnp.dot(tri_ref[...], equal.astype(BF16), preferred_element_type=F32) + seen
        quota = jnp.where(tied, (top - c_hi).astype(F32), jnp.inf)
        keep_equal = jnp.where(rank < quota, 0.0, NEG_BIG)
        sc_ref[j] = jnp.where(s > thr, 0.0, jnp.where(equal, keep_equal, NEG_BIG))
        return seen + jnp.sum(equal.astype(F32), axis=0, keepdims=True)

    def with_ties():
        lax.fori_loop(0, i + 1, tie_mask, jnp.zeros((1, tq), F32))
        return jnp.int32(0)

    def without_ties():
        return lax.fori_loop(0, i + 1, plain_mask, jnp.int32(0))

    lax.cond(jnp.max(tied.astype(I32)) > 0, with_ties, without_ties)

    qs_ref[...] = (qd_ref[...].astype(F32) * (HEAD_DIM ** -0.5 * LOG2E)).astype(BF16)
    m_ref[...] = jnp.full(m_ref.shape, NEG_BIG, F32)
    l_ref[...] = jnp.zeros(l_ref.shape, F32)
    acc_ref[...] = jnp.zeros(acc_ref.shape, F32)

    heads = [slice(h * HEAD_DIM, (h + 1) * HEAD_DIM) for h in range(N_HEADS)]

    def att_tile(j, near):
        start = pl.multiple_of(j * tq, tq)
        mask = sc_ref[j]
        tile_max = []
        for h, hs in enumerate(heads):
            lg = lax.dot_general(kd_ref[pl.ds(start, tq), hs], qs_ref[:, hs], _NT,
                                 preferred_element_type=F32)
            if near is not None:
                lg = lg + bias_ref[h, near * tq:(near + 1) * tq, :]
            lg = lg + mask
            lg_ref[h] = lg
            tile_max.append(jnp.max(lg, axis=0, keepdims=True))
        m_old = m_ref[...]
        m_new = jnp.maximum(m_old, jnp.concatenate(tile_max, axis=0))
        alpha = jnp.exp2(m_old - m_new)
        m_ref[...] = m_new
        for h in range(N_HEADS):
            p_ref[h] = jnp.exp2(lg_ref[h] - m_new[h:h + 1, :]).astype(BF16)
        denom = []
        for h, hs in enumerate(heads):
            v_ext = jnp.concatenate([vt_ref[j, hs, :], ones], axis=0)
            pv = jnp.dot(v_ext, p_ref[h], preferred_element_type=F32)
            acc_ref[h] = alpha[h:h + 1, :] * acc_ref[h] + pv[:HEAD_DIM]
            denom.append(pv[HEAD_DIM:HEAD_DIM + 1])
        l_ref[...] = alpha * l_ref[...] + jnp.concatenate(denom, axis=0)

    def far_step(j, carry):
        att_tile(j, None)
        return carry

    lax.fori_loop(0, i - 1, far_step, 0)

    @pl.when(i >= 1)
    def _():
        att_tile(i - 1, 0)

    att_tile(i, 1)
    for h in range(N_HEADS):
        o = acc_ref[h] / l_ref[h:h + 1, :]
        o_ref[:, h * HEAD_DIM:(h + 1) * HEAD_DIM] = o.T.astype(o_ref.dtype)


def _dsa_attention(main, small, bias, batch, seq, *, qd_col, kd_col, vd_col, qi_col, tq=256):
    nq = seq // tq
    width = N_HEADS * HEAD_DIM
    assert IDX_HEADS * IDX_DIM == width and tq == KEY_BITS * SUBLANES and tq % CHUNK == 0
    key_copies = LANES // IDX_DIM
    top = min(TOPK_MAX, seq // 4)
    v_t = _values_by_key_block(main, vd_col, batch, seq, tq)
    tri = (jnp.arange(tq)[None, :] < jnp.arange(tq)[:, None]).astype(BF16)
    resident = pl.Buffered(1)
    return pl.pallas_call(
        functools.partial(_dsa_body, tq=tq, top=top),
        grid=(batch, nq),
        in_specs=[
            pl.BlockSpec((tq, width), lambda b, i: (b * nq + i, qd_col)),
            pl.BlockSpec((tq, IDX_HEADS * IDX_DIM), lambda b, i: (b * nq + i, qi_col)),
            pl.BlockSpec((tq, LANES), lambda b, i: (b * nq + i, key_copies)),
            pl.BlockSpec((seq, width), lambda b, i: (b, kd_col), pipeline_mode=resident),
            pl.BlockSpec((None, nq, width, tq), lambda b, i: (b, 0, 0, 0), pipeline_mode=resident),
            pl.BlockSpec((seq, key_copies * LANES), lambda b, i: (b, 0), pipeline_mode=resident),
            pl.BlockSpec((N_HEADS, 2 * tq, tq), lambda b, i: (0, 0, 0), pipeline_mode=resident),
            pl.BlockSpec((tq, tq), lambda b, i: (0, 0), pipeline_mode=resident),
        ],
        out_specs=pl.BlockSpec((tq, width), lambda b, i: (b * nq + i, 0)),
        out_shape=jax.ShapeDtypeStruct((batch * seq, width), BF16),
        scratch_shapes=[
            pltpu.VMEM((nq, tq, tq), F32),
            pltpu.VMEM((KEY_BITS, nq * SUBLANES, tq), I32),
            pltpu.VMEM((tq, width), BF16),
            pltpu.VMEM((N_HEADS, tq), F32),
            pltpu.VMEM((N_HEADS, tq), F32),
            pltpu.VMEM((N_HEADS, HEAD_DIM, tq), F32),
            pltpu.VMEM((N_HEADS, tq, tq), F32),
            pltpu.VMEM((N_HEADS, tq, tq), BF16),
        ],
        compiler_params=_params("parallel", "arbitrary"),
        name="dsa_attention",
    )(main, main, small, main, v_t, small, bias, tri)


def _merge_body(osb_ref, ods_ref, wsb_ref, wds_ref, gsb_ref, gds_ref, bsb_ref, bds_ref, wo_ref,
                x_ref, o_ref):
    p_sb = jnp.dot(osb_ref[...], wsb_ref[...], preferred_element_type=F32)
    p_ds = jnp.dot(ods_ref[...], wds_ref[...], preferred_element_type=F32)
    g_sb = jax.nn.sigmoid(gsb_ref[...].astype(F32) + bsb_ref[...])
    g_ds = jax.nn.sigmoid(gds_ref[...].astype(F32) + bds_ref[...])
    merged = (g_sb * p_sb + g_ds * p_ds).astype(BF16)
    o_ref[...] = x_ref[...] + jnp.dot(merged, wo_ref[...], preferred_element_type=F32)


def _merge_out(o_sb, o_ds, w_sb, w_ds, w_out, proj, gate_offset, b_gate, x, *, tm=256):
    m, k = o_sb.shape
    d = w_sb.shape[1]
    tm = min(tm, m)
    assert gate_offset % d == 0 and m % tm == 0
    g = gate_offset // d
    b_gate = b_gate.reshape(1, 2 * d).astype(F32)
    resident = pl.Buffered(1)
    return pl.pallas_call(
        _merge_body,
        grid=(m // tm,),
        in_specs=[
            pl.BlockSpec((tm, k), lambda i: (i, 0)),
            pl.BlockSpec((tm, k), lambda i: (i, 0)),
            pl.BlockSpec((k, d), lambda i: (0, 0), pipeline_mode=resident),
            pl.BlockSpec((k, d), lambda i: (0, 0), pipeline_mode=resident),
            pl.BlockSpec((tm, d), lambda i: (i, g)),
            pl.BlockSpec((tm, d), lambda i: (i, g + 1)),
            pl.BlockSpec((1, d), lambda i: (0, 0), pipeline_mode=resident),
            pl.BlockSpec((1, d), lambda i: (0, 1), pipeline_mode=resident),
            pl.BlockSpec((d, d), lambda i: (0, 0), pipeline_mode=resident),
            pl.BlockSpec((tm, d), lambda i: (i, 0)),
        ],
        out_specs=pl.BlockSpec((tm, d), lambda i: (i, 0)),
        out_shape=jax.ShapeDtypeStruct((m, d), F32),
        compiler_params=_params("parallel"),
        name="merge_out_proj",
    )(o_sb, o_ds, w_sb, w_ds, proj, proj, b_gate, b_gate, w_out, x)


def _cross_body(x_ref, g_ref, wq_ref, km_ref, vm_ref, wo_ref, gn_ref, o_ref, hn_ref):
    x = x_ref[...]
    h = _rms(x, g_ref[...]).astype(BF16)
    q = jnp.dot(h, wq_ref[...], preferred_element_type=F32) * HEAD_DIM ** -0.5
    q = q.astype(BF16)
    outs = []
    for hh in range(MEM_HEADS):
        hs = slice(hh * HEAD_DIM, (hh + 1) * HEAD_DIM)
        lg = lax.dot_general(q[:, hs], km_ref[:, hs], _NT, preferred_element_type=F32)
        p = jnp.exp(lg - jnp.max(lg, axis=1, keepdims=True))
        o = jnp.dot(p.astype(BF16), vm_ref[:, hs], preferred_element_type=F32)
        outs.append((o / jnp.sum(p, axis=1, keepdims=True)).astype(BF16))
    o = jnp.concatenate(outs, axis=1)
    y = x + jnp.dot(o, wo_ref[...], preferred_element_type=F32)
    o_ref[...] = y
    hn_ref[...] = _rms(y, gn_ref[...]).astype(hn_ref.dtype)


def _cross_attention(x, kv, g_cross, w_cq, w_co, g_next, batch, seq, *, tm=512):
    m, d = x.shape
    n_mem = kv.shape[0] // batch
    width = MEM_HEADS * HEAD_DIM
    tm = min(tm, seq)
    nt = seq // tm
    row_tile = pl.BlockSpec((tm, d), lambda b, i: (b * nt + i, 0))
    gain = pl.BlockSpec((1, d), lambda b, i: (0, 0))
    return pl.pallas_call(
        _cross_body,
        grid=(batch, nt),
        in_specs=[
            row_tile,
            gain,
            pl.BlockSpec((d, width), lambda b, i: (0, 0)),
            pl.BlockSpec((n_mem, width), lambda b, i: (b, 0)),
            pl.BlockSpec((n_mem, width), lambda b, i: (b, 1)),
            pl.BlockSpec((width, d), lambda b, i: (0, 0)),
            gain,
        ],
        out_specs=[row_tile, row_tile],
        out_shape=[jax.ShapeDtypeStruct((m, d), F32), jax.ShapeDtypeStruct((m, d), BF16)],
        compiler_params=_params("parallel", "parallel"),
        name="cross_attention",
    )(x, g_cross.reshape(1, d).astype(F32), w_cq, kv, kv, w_co, g_next.reshape(1, d).astype(F32))


def _delayed(u, tail, shift):
    rolled = pltpu.roll(u, shift, axis=0)
    row = lax.broadcasted_iota(I32, tail.shape, 0)
    head = jnp.where(row < shift, pltpu.roll(tail, shift, axis=0), rolled[:SUBLANES])
    return jnp.concatenate([head, rolled[SUBLANES:]], axis=0)


def _ffn_up_body(h_ref, wa_ref, wv_ref, cwa_ref, cwv_ref, cba_ref, cbv_ref, o_ref,
                 wab_ref, wvb_ref, halo_ref, *, tiles_per_seq):
    i = pl.program_id(1)

    @pl.when(i == 0)
    def _():
        wab_ref[...] = wa_ref[...].astype(BF16)
        wvb_ref[...] = wv_ref[...].astype(BF16)

    h = h_ref[...]
    tm = h.shape[0]
    sequence_start = i % tiles_per_seq == 0

    def conv(wb_ref, cw_ref, cb_ref, slot):
        u = jnp.dot(h, wb_ref[...], preferred_element_type=F32)
        tail = jnp.where(sequence_start, 0.0, halo_ref[slot])
        halo_ref[slot] = u[tm - SUBLANES:, :]
        c = cb_ref[...] + cw_ref[CONV_WIDTH - 1:CONV_WIDTH, :] * u
        for tap in range(CONV_WIDTH - 1):
            c = c + cw_ref[tap:tap + 1, :] * _delayed(u, tail, CONV_WIDTH - 1 - tap)
        return c

    a = conv(wab_ref, cwa_ref, cba_ref, 0)
    val = conv(wvb_ref, cwv_ref, cbv_ref, 1)
    o_ref[...] = (jax.nn.gelu(a) * val).astype(o_ref.dtype)


def _ffn_up_gate(h, w_up, conv_w, conv_b, seq, *, tm=1024, tn=512):
    m, d = h.shape
    two_ff = w_up.shape[1]
    d_ff = two_ff // 2
    tm, tn = min(tm, seq), min(tn, d_ff)
    assert seq % tm == 0 and d_ff % tn == 0 and tm >= SUBLANES >= CONV_WIDTH - 1
    nf = d_ff // tn
    conv_w = conv_w.astype(F32)
    conv_b = conv_b.reshape(1, two_ff).astype(F32)
    return pl.pallas_call(
        functools.partial(_ffn_up_body, tiles_per_seq=seq // tm),
        grid=(nf, m // tm),
        in_specs=[
            pl.BlockSpec((tm, d), lambda j, i: (i, 0)),
            pl.BlockSpec((d, tn), lambda j, i: (0, j)),
            pl.BlockSpec((d, tn), lambda j, i: (0, nf + j)),
            pl.BlockSpec((CONV_WIDTH, tn), lambda j, i: (0, j)),
            pl.BlockSpec((CONV_WIDTH, tn), lambda j, i: (0, nf + j)),
            pl.BlockSpec((1, tn), lambda j, i: (0, j)),
            pl.BlockSpec((1, tn), lambda j, i: (0, nf + j)),
        ],
        out_specs=pl.BlockSpec((tm, tn), lambda j, i: (i, j)),
        out_shape=jax.ShapeDtypeStruct((m, d_ff), BF16),
        scratch_shapes=[pltpu.VMEM((d, tn), BF16), pltpu.VMEM((d, tn), BF16),
                        pltpu.VMEM((2, SUBLANES, tn), F32)],
        compiler_params=_params("parallel", "arbitrary"),
        name="ffn_up_conv_gate",
    )(h, w_up, w_up, conv_w, conv_w, conv_b, conv_b)


def _ffn_down_body(a_ref, w_ref, x_ref, g_ref, o_ref, *, final_norm):
    k = pl.program_id(1)

    @pl.when(k == 0)
    def _():
        o_ref[...] = x_ref[...]

    o_ref[...] += jnp.dot(a_ref[...], w_ref[...], preferred_element_type=F32)

    if final_norm:
        @pl.when(k == pl.num_programs(1) - 1)
        def _():
            o_ref[...] = _rms(o_ref[...], g_ref[...])


def _ffn_down(a, w, x, g_final, *, tm=512, tk=1024):
    m, kdim = a.shape
    d = w.shape[1]
    tm, tk = min(tm, m), min(tk, kdim)
    assert m % tm == 0 and kdim % tk == 0
    final_norm = g_final is not None
    g = (g_final if final_norm else jnp.ones((d,), F32)).reshape(1, d).astype(F32)
    return pl.pallas_call(
        functools.partial(_ffn_down_body, final_norm=final_norm),
        grid=(m // tm, kdim // tk),
        in_specs=[
            pl.BlockSpec((tm, tk), lambda i, k: (i, k)),
            pl.BlockSpec((tk, d), lambda i, k: (k, 0)),
            pl.BlockSpec((tm, d), lambda i, k: (i, 0)),
            pl.BlockSpec((1, d), lambda i, k: (0, 0)),
        ],
        out_specs=pl.BlockSpec((tm, d), lambda i, k: (i, 0)),
        out_shape=jax.ShapeDtypeStruct((m, d), F32),
        compiler_params=_params("parallel", "arbitrary"),
        name="ffn_down",
    )(a, w, x, g)


def _layer(x, mem, g_mix, w_in, b_gate, w_proj_sb, w_proj_dsa, w_out, rel_bias,
           g_cross, g_mem, w_cq, w_ckv, w_co, g_ffn, w_up, conv_w, conv_b, w_down, g_final, batch, seq):
    d = x.shape[1]
    width = N_HEADS * HEAD_DIM
    idx_w = IDX_HEADS * IDX_DIM
    o_qi = 6 * width
    o_ki = o_qi + idx_w
    o_wi = o_ki + IDX_DIM
    o_g = o_wi + IDX_HEADS

    h = _rmsnorm_bf16(x, g_mix)
    w_in_t = w_in.T
    main = _matmul_ws(h, w_in_t, name="in_proj_main", n_blocks=o_ki // width,
                      out_dtype=BF16, tm=1024, tn=width)
    gates = _matmul_ws(h, w_in_t[o_g:], name="in_proj_gates", n_blocks=2 * d // width,
                       out_dtype=BF16, tm=1024, tn=width)
    zeros = jnp.zeros((d, LANES - IDX_DIM), F32)
    w_small = jnp.concatenate([
        w_in[:, o_ki:o_wi], zeros, zeros, w_in[:, o_ki:o_wi],
        jnp.pad(w_in[:, o_wi:o_g], ((0, 0), (0, LANES - IDX_HEADS)))], axis=1).astype(BF16)
    small = _matmul(h, w_small, name="in_proj_index", out_dtype=F32, tm=1024)

    o_sb = _sb_attention(main, batch, seq, q_col=0, k_col=1, v_col=2)
    tq = 256
    bias = _near_bias(rel_bias, tq)
    o_ds = _dsa_attention(main, small, bias, batch, seq, qd_col=3, kd_col=4, vd_col=5, qi_col=6, tq=tq)

    x = _merge_out(o_sb, o_ds, w_proj_sb.astype(BF16), w_proj_dsa.astype(BF16), w_out.astype(BF16),
                   gates, 0, b_gate, x)

    kv = _matmul(mem, w_ckv.astype(BF16), name="mem_kv_proj", gain=g_mem, out_dtype=BF16, tm=512)
    x, h_ffn = _cross_attention(x, kv, g_cross, w_cq.astype(BF16), w_co.astype(BF16), g_ffn, batch, seq)

    act = _ffn_up_gate(h_ffn, w_up, conv_w, conv_b, seq)
    return _ffn_down(act, w_down.astype(BF16), x, g_final)


def kernel(x, mem, g_mix, w_in, b_gate, w_proj_sb, w_proj_dsa, w_out, rel_bias, g_cross, g_mem,
           w_cq, w_ckv, w_co, g_ffn, w_up, conv_w, conv_b, w_down, g_final):
    batch, seq, d = x.shape
    h = x.reshape(batch * seq, d)
    mem2 = mem.reshape(batch * mem.shape[1], d)
    depth = g_mix.shape[0]
    for l in range(depth):
        h = _layer(h, mem2, g_mix[l], w_in[l], b_gate[l], w_proj_sb[l], w_proj_dsa[l], w_out[l],
                   rel_bias, g_cross[l], g_mem[l], w_cq[l], w_ckv[l], w_co[l], g_ffn[l], w_up[l],
                   conv_w[l], conv_b[l], w_down[l], g_final if l == depth - 1 else None, batch, seq)
    return h.reshape(batch, seq, d)
```

```python
import functools

import jax
import jax.numpy as jnp
from jax import lax
from jax.experimental import pallas as pl
from jax.experimental.pallas import tpu as pltpu

F32, BF16, I32 = jnp.float32, jnp.bfloat16, jnp.int32

EPS = 1e-6
HEAD_DIM = 128
N_HEADS = 8
IDX_HEADS = 16
IDX_DIM = 64
CHUNK = 64
TOPK_MAX = 256
N_BUCKETS = 32
MAX_DISTANCE = 128
MEM_HEADS = 4
CONV_WIDTH = 3

LANES = 128
SUBLANES = 8
VMEM_LIMIT_BYTES = 56 * 1024 * 1024
NEG_BIG = -1e30
EXP2_UNDERFLOW = -151.0
LOG2E = 1.4426950408889634
KEY_BITS = 32
SIGN_BIT = -(2 ** 31)

_NT = (((1,), (1,)), ((), ()))


def _params(*sem):
    return pltpu.CompilerParams(dimension_semantics=sem, vmem_limit_bytes=VMEM_LIMIT_BYTES)


def _rms(x, g):
    inv = lax.rsqrt(jnp.mean(x * x, axis=-1, keepdims=True) + EPS)
    return x * inv * g


def _mm_body(*refs, has_gain):
    if has_gain:
        a_ref, g_ref, w_ref, o_ref = refs
        a = _rms(a_ref[...].astype(F32), g_ref[...]).astype(BF16)
    else:
        a_ref, w_ref, o_ref = refs
        a = a_ref[...]
    o_ref[...] = jnp.dot(a, w_ref[...], preferred_element_type=F32).astype(o_ref.dtype)


def _matmul(a, w, *, name, gain=None, out_dtype, tm):
    m, k = a.shape
    n = w.shape[1]
    tm = min(tm, m)
    assert m % tm == 0, (m, tm)
    in_specs = [pl.BlockSpec((tm, k), lambda i: (i, 0))]
    args = [a]
    if gain is not None:
        in_specs.append(pl.BlockSpec((1, k), lambda i: (0, 0)))
        args.append(gain.reshape(1, k).astype(F32))
    in_specs.append(pl.BlockSpec((k, n), lambda i: (0, 0)))
    args.append(w)
    return pl.pallas_call(
        functools.partial(_mm_body, has_gain=gain is not None),
        grid=(m // tm,),
        in_specs=in_specs,
        out_specs=pl.BlockSpec((tm, n), lambda i: (i, 0)),
        out_shape=jax.ShapeDtypeStruct((m, n), out_dtype),
        compiler_params=_params("parallel"),
        name=name,
    )(*args)


def _mm_ws_body(a_ref, wt_ref, o_ref, wb_ref):
    @pl.when(pl.program_id(1) == 0)
    def _():
        wb_ref[...] = wt_ref[...].astype(BF16)

    o_ref[...] = lax.dot_general(a_ref[...], wb_ref[...], _NT,
                                 preferred_element_type=F32).astype(o_ref.dtype)


def _matmul_ws(a, w_t, *, name, n_blocks, out_dtype, tm, tn):
    m, k = a.shape
    tm = min(tm, m)
    assert m % tm == 0 and n_blocks * tn <= w_t.shape[0] and w_t.shape[1] == k
    return pl.pallas_call(
        _mm_ws_body,
        grid=(n_blocks, m // tm),
        in_specs=[
            pl.BlockSpec((tm, k), lambda j, i: (i, 0)),
            pl.BlockSpec((tn, k), lambda j, i: (j, 0)),
        ],
        out_specs=pl.BlockSpec((tm, tn), lambda j, i: (i, j)),
        out_shape=jax.ShapeDtypeStruct((m, n_blocks * tn), out_dtype),
        scratch_shapes=[pltpu.VMEM((tn, k), BF16)],
        compiler_params=_params("parallel", "arbitrary"),
        name=name,
    )(a, w_t)


def _norm_body(x_ref, g_ref, o_ref):
    o_ref[...] = _rms(x_ref[...], g_ref[...]).astype(o_ref.dtype)


def _rmsnorm_bf16(x, g, *, tm=512):
    m, d = x.shape
    tm = min(tm, m)
    return pl.pallas_call(
        _norm_body,
        grid=(m // tm,),
        in_specs=[pl.BlockSpec((tm, d), lambda i: (i, 0)), pl.BlockSpec((1, d), lambda i: (0, 0))],
        out_specs=pl.BlockSpec((tm, d), lambda i: (i, 0)),
        out_shape=jax.ShapeDtypeStruct((m, d), BF16),
        compiler_params=_params("parallel"),
        name="mixer_norm",
    )(x, g.reshape(1, d).astype(F32))


def _values_by_key_block(arr, col, batch, seq, tq):
    width = N_HEADS * HEAD_DIM
    return arr[:, col * width:(col + 1) * width].reshape(batch, seq // tq, tq, width).swapaxes(2, 3)


def _sb_body(q_ref, k_ref, vt_ref, tri_ref, o_ref, qs_ref, z_ref, lb_ref, wb_ref, acc_ref, *, tq):
    i = pl.program_id(1)
    shape = (tq, tq)
    before = lax.broadcasted_iota(I32, shape, 0) < lax.broadcasted_iota(I32, shape, 1)
    heads = [slice(h * HEAD_DIM, (h + 1) * HEAD_DIM) for h in range(N_HEADS)]
    qs_ref[...] = (q_ref[...].astype(F32) * (HEAD_DIM ** -0.5 * LOG2E)).astype(BF16)
    acc_ref[...] = jnp.zeros(acc_ref.shape, F32)

    def tile(j, carry, diagonal):
        start = pl.multiple_of(j * tq, tq)
        for h, hs in enumerate(heads):
            z = lax.dot_general(k_ref[pl.ds(start, tq), hs], qs_ref[:, hs], _NT,
                                preferred_element_type=F32)
            log_keep = jnp.minimum(-z, 0.0) - jnp.log2(1.0 + jnp.exp2(-jnp.abs(z)))
            if diagonal:
                log_keep = jnp.where(before, log_keep, 0.0)
            z_ref[h] = z
            lb_ref[h] = log_keep.astype(BF16)
        new_carry = []
        for h in range(N_HEADS):
            c = jnp.dot(tri_ref[...], lb_ref[h], preferred_element_type=F32) + carry[h:h + 1, :]
            w = jnp.exp2(z_ref[h] + c)
            if diagonal:
                w = jnp.where(before, w, 0.0)
            wb_ref[h] = w.astype(BF16)
            new_carry.append(c[0:1, :])
        for h, hs in enumerate(heads):
            acc_ref[h] += jnp.dot(vt_ref[j, hs, :], wb_ref[h], preferred_element_type=F32)
        return jnp.concatenate(new_carry, axis=0)

    carry = tile(i, jnp.zeros((N_HEADS, tq), F32), True)

    def cond(state):
        j, live, _ = state
        return jnp.logical_and(j >= 0, live > EXP2_UNDERFLOW)

    def body(state):
        j, _, carry = state
        carry = tile(j, carry, False)
        return j - 1, jnp.max(carry), carry

    lax.while_loop(cond, body, (i - 1, jnp.max(carry), carry))
    for h, hs in enumerate(heads):
        o_ref[:, hs] = acc_ref[h].T.astype(o_ref.dtype)


def _sb_attention(qkv, batch, seq, *, q_col, k_col, v_col, tq=256):
    nq = seq // tq
    width = N_HEADS * HEAD_DIM
    v_t = _values_by_key_block(qkv, v_col, batch, seq, tq)
    tri = (jnp.arange(tq)[None, :] >= jnp.arange(tq)[:, None]).astype(BF16)
    resident = pl.Buffered(1)
    return pl.pallas_call(
        functools.partial(_sb_body, tq=tq),
        grid=(batch, nq),
        in_specs=[
            pl.BlockSpec((tq, width), lambda b, i: (b * nq + i, q_col)),
            pl.BlockSpec((seq, width), lambda b, i: (b, k_col), pipeline_mode=resident),
            pl.BlockSpec((None, nq, width, tq), lambda b, i: (b, 0, 0, 0), pipeline_mode=resident),
            pl.BlockSpec((tq, tq), lambda b, i: (0, 0), pipeline_mode=resident),
        ],
        out_specs=pl.BlockSpec((tq, width), lambda b, i: (b * nq + i, 0)),
        out_shape=jax.ShapeDtypeStruct((batch * seq, width), BF16),
        scratch_shapes=[
            pltpu.VMEM((tq, width), BF16),
            pltpu.VMEM((N_HEADS, tq, tq), F32),
            pltpu.VMEM((N_HEADS, tq, tq), BF16),
            pltpu.VMEM((N_HEADS, tq, tq), BF16),
            pltpu.VMEM((N_HEADS, HEAD_DIM, tq), F32),
        ],
        compiler_params=_params("parallel", "arbitrary"),
        name="sb_attention",
    )(qkv, qkv, v_t, tri)


def _bucket_thresholds():
    nb = N_BUCKETS // 2
    max_exact = nb // 2
    span = nb - max_exact
    out = []
    for k in range(1, span):
        n = max_exact
        while n ** span * max_exact ** k < MAX_DISTANCE ** k * max_exact ** span:
            n += 1
        out.append(n)
    return max_exact, out


def _bias_body(rb_ref, o_ref, *, tq):
    nb = N_BUCKETS // 2
    max_exact, steps = _bucket_thresholds()
    shape = (2 * tq, tq)
    rel = lax.broadcasted_iota(I32, shape, 0) - lax.broadcasted_iota(I32, shape, 1) - tq
    n = jnp.abs(rel)
    large = jnp.full(shape, max_exact, I32)
    for t in steps:
        large = large + (n >= t).astype(I32)
    bucket = jnp.where(rel > 0, nb, 0) + jnp.where(n < max_exact, n, large)
    for h in range(N_HEADS):
        val = jnp.zeros(shape, F32)
        for b in range(N_BUCKETS):
            val = jnp.where(bucket == b, rb_ref[b, h], val)
        o_ref[h] = (val - rb_ref[nb - 1, h]) * LOG2E


def _near_bias(rel_bias, tq):
    return pl.pallas_call(
        functools.partial(_bias_body, tq=tq),
        in_specs=[pl.BlockSpec(memory_space=pltpu.SMEM)],
        out_specs=pl.BlockSpec(memory_space=pltpu.VMEM),
        out_shape=jax.ShapeDtypeStruct((N_HEADS, 2 * tq, tq), F32),
        compiler_params=pltpu.CompilerParams(vmem_limit_bytes=VMEM_LIMIT_BYTES),
        name="dsa_near_bias",
    )(rel_bias.astype(F32))


def _float_key(x):
    bits = lax.bitcast_convert_type(x, I32)
    return bits ^ ((bits >> 31) & 0x7FFFFFFF)


def _key_float(key):
    return lax.bitcast_convert_type(key ^ ((key >> 31) & 0x7FFFFFFF), F32)


def _bit_transpose32(words):
    a = list(words)
    j, m = 16, 0x0000FFFF
    while j:
        mask = jnp.int32(m - (1 << 32) if m >= 1 << 31 else m)
        k = 0
        while k < 32:
            t = (lax.shift_right_logical(a[k], jnp.int32(j)) ^ a[k + j]) & mask
            a[k] = a[k] ^ lax.shift_left(t, jnp.int32(j))
            a[k + j] = a[k + j] ^ t
            k = (k + j + 1) & ~j
        j >>= 1
        m = (m ^ (m << j)) & 0xFFFFFFFF
    return a


def _dsa_body(qd_ref, qi_ref, wq_ref, kd_ref, vt_ref, ki_ref, bias_ref, tri_ref, o_ref,
              sc_ref, plane_ref, qs_ref, m_ref, l_ref, acc_ref, lg_ref, bmax_ref, p_ref, *, tq, top):
    i = pl.program_id(1)
    shape = (tq, tq)
    key_row = lax.broadcasted_iota(I32, shape, 0)
    qry_col = lax.broadcasted_iota(I32, shape, 1)
    visible = key_row // CHUNK <= qry_col // CHUNK

    w_t = (wq_ref[...] * (IDX_DIM ** -0.5 * IDX_HEADS ** -0.5)).T

    heads_per_vreg = LANES // IDX_DIM

    def score_tile(j):
        start = pl.multiple_of(j * tq, tq)
        ki = [ki_ref[pl.ds(start, tq), c * LANES:(c + 1) * LANES].astype(BF16)
              for c in range(heads_per_vreg)]
        s = jnp.zeros(shape, F32)
        for h in range(IDX_HEADS):
            g, c = divmod(h, heads_per_vreg)
            d = lax.dot_general(ki[c], qi_ref[:, g * LANES:(g + 1) * LANES], _NT,
                                preferred_element_type=F32)
            s = s + w_t[h:h + 1, :] * jnp.maximum(d, 0.0)
        sc_ref[j] = s

    def key_planes(j):
        ukey = _float_key(sc_ref[j]) ^ SIGN_BIT
        planes = _bit_transpose32([ukey[g * SUBLANES:(g + 1) * SUBLANES, :] for g in range(KEY_BITS)])
        for b in range(KEY_BITS):
            plane_ref[b, pl.ds(pl.multiple_of(j * SUBLANES, SUBLANES), SUBLANES), :] = planes[b]

    @pl.when(i == 0)
    def _():
        plane_ref[...] = jnp.zeros(plane_ref.shape, I32)

    def score_step(j, carry):
        key_planes(j)
        score_tile(j + 1)
        return carry

    score_tile(0)
    lax.fori_loop(0, i, score_step, 0)
    key_planes(i)
    sc_ref[i] = jnp.where(visible, sc_ref[i], -jnp.inf)

    n_rows = plane_ref.shape[1]
    block_of_row = lax.broadcasted_iota(I32, (n_rows, tq), 0) // SUBLANES
    qry_of_col = lax.broadcasted_iota(I32, (n_rows, tq), 1)
    n_bits = (qry_of_col // CHUNK + 1) * (CHUNK // SUBLANES)
    diag_bits = jnp.where(n_bits >= KEY_BITS, -1, lax.shift_left(jnp.int32(1), n_bits) - 1)
    cand0 = jnp.where(block_of_row < i, -1, jnp.where(block_of_row == i, diag_bits, 0))

    def popcount_rows(words):
        return jnp.sum(lax.population_count(words), axis=0, keepdims=True)

    def bit_step(t, state):
        cand, n_above, thr_bits = state
        b = KEY_BITS - 1 - t
        ones = cand & plane_ref[b]
        n_ones = popcount_rows(ones)
        take = n_above + n_ones >= top
        cand = jnp.where(take, ones, cand ^ ones)
        n_above = jnp.where(take, n_above, n_above + n_ones)
        thr_bits = thr_bits | jnp.where(take, lax.shift_left(jnp.int32(1), b), 0)
        return cand, n_above, thr_bits

    zero = jnp.zeros((1, tq), I32)
    cand, n_above, thr_bits = lax.fori_loop(0, KEY_BITS, bit_step, (cand0, zero, zero))
    qry = lax.broadcasted_iota(I32, (1, tq), 1)
    n_visible = i * tq + (qry // CHUNK + 1) * CHUNK
    wanted = n_visible > top
    thr = jnp.where(wanted, _key_float(thr_bits ^ SIGN_BIT), jnp.finfo(F32).min)
    tied = jnp.logical_and(wanted, n_above + popcount_rows(cand) > top)
    c_hi = n_above

    ones = jnp.ones((2 * SUBLANES, tq), BF16)

    def plain_mask(j, carry):
        sc_ref[j] = jnp.where(sc_ref[j] >= thr, 0.0, NEG_BIG)
        return carry

    def tie_mask(j, seen):
        s = sc_ref[j]
        equal = s == thr
        rank = jnp.dot(tri_ref[...], equal.astype(BF16), preferred_element_type=F32) + seen
        quota = jnp.where(tied, (top - c_hi).astype(F32), jnp.inf)
        keep_equal = jnp.where(rank < quota, 0.0, NEG_BIG)
        sc_ref[j] = jnp.where(s > thr, 0.0, jnp.where(equal, keep_equal, NEG_BIG))
        return seen + jnp.sum(equal.astype(F32), axis=0, keepdims=True)

    def with_ties():
        lax.fori_loop(0, i + 1, tie_mask, jnp.zeros((1, tq), F32))
        return jnp.int32(0)

    def without_ties():
        return lax.fori_loop(0, i + 1, plain_mask, jnp.int32(0))

    lax.cond(jnp.max(tied.astype(I32)) > 0, with_ties, without_ties)

    qs_ref[...] = (qd_ref[...].astype(F32) * (HEAD_DIM ** -0.5 * LOG2E)).astype(BF16)
    m_ref[...] = jnp.full(m_ref.shape, NEG_BIG, F32)
    l_ref[...] = jnp.zeros(l_ref.shape, F32)
    acc_ref[...] = jnp.zeros(acc_ref.shape, F32)

    heads = [slice(h * HEAD_DIM, (h + 1) * HEAD_DIM) for h in range(N_HEADS)]

    far, prev, diag = None, 0, 1

    def logits(u, near, slot):
        j = i - u
        start = pl.multiple_of(j * tq, tq)
        mask = sc_ref[j]
        block_max = []
        for h, hs in enumerate(heads):
            lg = lax.dot_general(kd_ref[pl.ds(start, tq), hs], qs_ref[:, hs], _NT,
                                 preferred_element_type=F32)
            if near is not None:
                lg = lg + bias_ref[h, near * tq:(near + 1) * tq, :]
            lg = lg + mask
            lg_ref[slot, h] = lg
            block_max.append(jnp.max(lg, axis=0, keepdims=True))
        bmax_ref[slot] = jnp.concatenate(block_max, axis=0)

    def values(u, slot):
        j = i - u
        m_old = m_ref[...]
        m_new = jnp.maximum(m_old, bmax_ref[slot])
        alpha = jnp.exp2(m_old - m_new)
        m_ref[...] = m_new
        for h in range(N_HEADS):
            p_ref[h] = jnp.exp2(lg_ref[slot, h] - m_new[h:h + 1, :]).astype(BF16)
        denom = []
        for h, hs in enumerate(heads):
            v_ext = jnp.concatenate([vt_ref[j, hs, :], ones], axis=0)
            pv = jnp.dot(v_ext, p_ref[h], preferred_element_type=F32)
            acc_ref[h] = alpha[h:h + 1, :] * acc_ref[h] + pv[:HEAD_DIM]
            denom.append(pv[HEAD_DIM:HEAD_DIM + 1])
        l_ref[...] = alpha * l_ref[...] + jnp.concatenate(denom, axis=0)

    def even_step(u, near_a, near_b):
        logits(u - 1, near_a, 1)
        values(u, 0)
        logits(u - 2, near_b, 0)
        values(u - 1, 1)

    @pl.when(i == 0)
    def _():
        logits(0, diag, 0)
        values(0, 0)

    @pl.when(i == 1)
    def _():
        logits(1, prev, 1)

    @pl.when(jnp.logical_and(i >= 2, i % 2 == 1))
    def _():
        logits(i, far, 1)
        logits(i - 1, far, 0)
        values(i, 1)

    @pl.when(jnp.logical_and(i >= 2, i % 2 == 0))
    def _():
        logits(i, far, 0)

    def far_pair(k, carry):
        even_step(2 * (i // 2 - k), far, far)
        return carry

    lax.fori_loop(0, i // 2 - 1, far_pair, 0)

    @pl.when(i >= 2)
    def _():
        even_step(2, prev, diag)

    @pl.when(i == 1)
    def _():
        logits(0, diag, 0)
        values(1, 1)

    @pl.when(i >= 1)
    def _():
        values(0, 0)

    for h in range(N_HEADS):
        o = acc_ref[h] / l_ref[h:h + 1, :]
        o_ref[:, h * HEAD_DIM:(h + 1) * HEAD_DIM] = o.T.astype(o_ref.dtype)


def _dsa_attention(main, small, bias, batch, seq, *, qd_col, kd_col, vd_col, qi_col, tq=256):
    nq = seq // tq
    width = N_HEADS * HEAD_DIM
    assert IDX_HEADS * IDX_DIM == width and tq == KEY_BITS * SUBLANES and tq % CHUNK == 0
    key_copies = LANES // IDX_DIM
    top = min(TOPK_MAX, seq // 4)
    v_t = _values_by_key_block(main, vd_col, batch, seq, tq)
    tri = (jnp.arange(tq)[None, :] < jnp.arange(tq)[:, None]).astype(BF16)
    resident = pl.Buffered(1)
    return pl.pallas_call(
        functools.partial(_dsa_body, tq=tq, top=top),
        grid=(batch, nq),
        in_specs=[
            pl.BlockSpec((tq, width), lambda b, i: (b * nq + i, qd_col)),
            pl.BlockSpec((tq, IDX_HEADS * IDX_DIM), lambda b, i: (b * nq + i, qi_col)),
            pl.BlockSpec((tq, LANES), lambda b, i: (b * nq + i, key_copies)),
            pl.BlockSpec((seq, width), lambda b, i: (b, kd_col), pipeline_mode=resident),
            pl.BlockSpec((None, nq, width, tq), lambda b, i: (b, 0, 0, 0), pipeline_mode=resident),
            pl.BlockSpec((seq, key_copies * LANES), lambda b, i: (b, 0), pipeline_mode=resident),
            pl.BlockSpec((N_HEADS, 2 * tq, tq), lambda b, i: (0, 0, 0), pipeline_mode=resident),
            pl.BlockSpec((tq, tq), lambda b, i: (0, 0), pipeline_mode=resident),
        ],
        out_specs=pl.BlockSpec((tq, width), lambda b, i: (b * nq + i, 0)),
        out_shape=jax.ShapeDtypeStruct((batch * seq, width), BF16),
        scratch_shapes=[
            pltpu.VMEM((nq, tq, tq), F32),
            pltpu.VMEM((KEY_BITS, nq * SUBLANES, tq), I32),
            pltpu.VMEM((tq, width), BF16),
            pltpu.VMEM((N_HEADS, tq), F32),
            pltpu.VMEM((N_HEADS, tq), F32),
            pltpu.VMEM((N_HEADS, HEAD_DIM, tq), F32),
            pltpu.VMEM((2, N_HEADS, tq, tq), F32),
            pltpu.VMEM((2, N_HEADS, tq), F32),
            pltpu.VMEM((N_HEADS, tq, tq), BF16),
        ],
        compiler_params=_params("parallel", "arbitrary"),
        name="dsa_attention",
    )(main, main, small, main, v_t, small, bias, tri)


def _merge_body(osb_ref, ods_ref, wsb_ref, wds_ref, gsb_ref, gds_ref, bsb_ref, bds_ref, wo_ref,
                x_ref, o_ref):
    p_sb = jnp.dot(osb_ref[...], wsb_ref[...], preferred_element_type=F32)
    p_ds = jnp.dot(ods_ref[...], wds_ref[...], preferred_element_type=F32)
    g_sb = jax.nn.sigmoid(gsb_ref[...].astype(F32) + bsb_ref[...])
    g_ds = jax.nn.sigmoid(gds_ref[...].astype(F32) + bds_ref[...])
    merged = (g_sb * p_sb + g_ds * p_ds).astype(BF16)
    o_ref[...] = x_ref[...] + jnp.dot(merged, wo_ref[...], preferred_element_type=F32)


def _merge_out(o_sb, o_ds, w_sb, w_ds, w_out, proj, gate_offset, b_gate, x, *, tm=256):
    m, k = o_sb.shape
    d = w_sb.shape[1]
    tm = min(tm, m)
    assert gate_offset % d == 0 and m % tm == 0
    g = gate_offset // d
    b_gate = b_gate.reshape(1, 2 * d).astype(F32)
    resident = pl.Buffered(1)
    return pl.pallas_call(
        _merge_body,
        grid=(m // tm,),
        in_specs=[
            pl.BlockSpec((tm, k), lambda i: (i, 0)),
            pl.BlockSpec((tm, k), lambda i: (i, 0)),
            pl.BlockSpec((k, d), lambda i: (0, 0), pipeline_mode=resident),
            pl.BlockSpec((k, d), lambda i: (0, 0), pipeline_mode=resident),
            pl.BlockSpec((tm, d), lambda i: (i, g)),
            pl.BlockSpec((tm, d), lambda i: (i, g + 1)),
            pl.BlockSpec((1, d), lambda i: (0, 0), pipeline_mode=resident),
            pl.BlockSpec((1, d), lambda i: (0, 1), pipeline_mode=resident),
            pl.BlockSpec((d, d), lambda i: (0, 0), pipeline_mode=resident),
            pl.BlockSpec((tm, d), lambda i: (i, 0)),
        ],
        out_specs=pl.BlockSpec((tm, d), lambda i: (i, 0)),
        out_shape=jax.ShapeDtypeStruct((m, d), F32),
        compiler_params=_params("parallel"),
        name="merge_out_proj",
    )(o_sb, o_ds, w_sb, w_ds, proj, proj, b_gate, b_gate, w_out, x)


def _cross_body(x_ref, g_ref, wq_ref, km_ref, vm_ref, wo_ref, gn_ref, o_ref, hn_ref):
    x = x_ref[...]
    h = _rms(x, g_ref[...]).astype(BF16)
    q = jnp.dot(h, wq_ref[...], preferred_element_type=F32) * HEAD_DIM ** -0.5
    q = q.astype(BF16)
    outs = []
    for hh in range(MEM_HEADS):
        hs = slice(hh * HEAD_DIM, (hh + 1) * HEAD_DIM)
        lg = lax.dot_general(q[:, hs], km_ref[:, hs], _NT, preferred_element_type=F32)
        p = jnp.exp(lg - jnp.max(lg, axis=1, keepdims=True))
        o = jnp.dot(p.astype(BF16), vm_ref[:, hs], preferred_element_type=F32)
        outs.append((o / jnp.sum(p, axis=1, keepdims=True)).astype(BF16))
    o = jnp.concatenate(outs, axis=1)
    y = x + jnp.dot(o, wo_ref[...], preferred_element_type=F32)
    o_ref[...] = y
    hn_ref[...] = _rms(y, gn_ref[...]).astype(hn_ref.dtype)


def _cross_attention(x, kv, g_cross, w_cq, w_co, g_next, batch, seq, *, tm=512):
    m, d = x.shape
    n_mem = kv.shape[0] // batch
    width = MEM_HEADS * HEAD_DIM
    tm = min(tm, seq)
    nt = seq // tm
    row_tile = pl.BlockSpec((tm, d), lambda b, i: (b * nt + i, 0))
    gain = pl.BlockSpec((1, d), lambda b, i: (0, 0))
    return pl.pallas_call(
        _cross_body,
        grid=(batch, nt),
        in_specs=[
            row_tile,
            gain,
            pl.BlockSpec((d, width), lambda b, i: (0, 0)),
            pl.BlockSpec((n_mem, width), lambda b, i: (b, 0)),
            pl.BlockSpec((n_mem, width), lambda b, i: (b, 1)),
            pl.BlockSpec((width, d), lambda b, i: (0, 0)),
            gain,
        ],
        out_specs=[row_tile, row_tile],
        out_shape=[jax.ShapeDtypeStruct((m, d), F32), jax.ShapeDtypeStruct((m, d), BF16)],
        compiler_params=_params("parallel", "parallel"),
        name="cross_attention",
    )(x, g_cross.reshape(1, d).astype(F32), w_cq, kv, kv, w_co, g_next.reshape(1, d).astype(F32))


def _delayed(u, tail, shift):
    rolled = pltpu.roll(u, shift, axis=0)
    row = lax.broadcasted_iota(I32, tail.shape, 0)
    head = jnp.where(row < shift, pltpu.roll(tail, shift, axis=0), rolled[:SUBLANES])
    return jnp.concatenate([head, rolled[SUBLANES:]], axis=0)


def _ffn_up_body(h_ref, wa_ref, wv_ref, cwa_ref, cwv_ref, cba_ref, cbv_ref, o_ref,
                 wab_ref, wvb_ref, halo_ref, *, tiles_per_seq):
    i = pl.program_id(1)

    @pl.when(i == 0)
    def _():
        wab_ref[...] = wa_ref[...].astype(BF16)
        wvb_ref[...] = wv_ref[...].astype(BF16)

    h = h_ref[...]
    tm = h.shape[0]
    sequence_start = i % tiles_per_seq == 0

    def conv(wb_ref, cw_ref, cb_ref, slot):
        u = jnp.dot(h, wb_ref[...], preferred_element_type=F32)
        tail = jnp.where(sequence_start, 0.0, halo_ref[slot])
        halo_ref[slot] = u[tm - SUBLANES:, :]
        c = cb_ref[...] + cw_ref[CONV_WIDTH - 1:CONV_WIDTH, :] * u
        for tap in range(CONV_WIDTH - 1):
            c = c + cw_ref[tap:tap + 1, :] * _delayed(u, tail, CONV_WIDTH - 1 - tap)
        return c

    a = conv(wab_ref, cwa_ref, cba_ref, 0)
    val = conv(wvb_ref, cwv_ref, cbv_ref, 1)
    o_ref[...] = (jax.nn.gelu(a) * val).astype(o_ref.dtype)


def _ffn_up_gate(h, w_up, conv_w, conv_b, seq, *, tm=1024, tn=512):
    m, d = h.shape
    two_ff = w_up.shape[1]
    d_ff = two_ff // 2
    tm, tn = min(tm, seq), min(tn, d_ff)
    assert seq % tm == 0 and d_ff % tn == 0 and tm >= SUBLANES >= CONV_WIDTH - 1
    nf = d_ff // tn
    conv_w = conv_w.astype(F32)
    conv_b = conv_b.reshape(1, two_ff).astype(F32)
    return pl.pallas_call(
        functools.partial(_ffn_up_body, tiles_per_seq=seq // tm),
        grid=(nf, m // tm),
        in_specs=[
            pl.BlockSpec((tm, d), lambda j, i: (i, 0)),
            pl.BlockSpec((d, tn), lambda j, i: (0, j)),
            pl.BlockSpec((d, tn), lambda j, i: (0, nf + j)),
            pl.BlockSpec((CONV_WIDTH, tn), lambda j, i: (0, j)),
            pl.BlockSpec((CONV_WIDTH, tn), lambda j, i: (0, nf + j)),
            pl.BlockSpec((1, tn), lambda j, i: (0, j)),
            pl.BlockSpec((1, tn), lambda j, i: (0, nf + j)),
        ],
        out_specs=pl.BlockSpec((tm, tn), lambda j, i: (i, j)),
        out_shape=jax.ShapeDtypeStruct((m, d_ff), BF16),
        scratch_shapes=[pltpu.VMEM((d, tn), BF16), pltpu.VMEM((d, tn), BF16),
                        pltpu.VMEM((2, SUBLANES, tn), F32)],
        compiler_params=_params("parallel", "arbitrary"),
        name="ffn_up_conv_gate",
    )(h, w_up, w_up, conv_w, conv_w, conv_b, conv_b)


def _ffn_down_body(a_ref, w_ref, x_ref, g_ref, o_ref, *, final_norm):
    k = pl.program_id(1)

    @pl.when(k == 0)
    def _():
        o_ref[...] = x_ref[...]

    o_ref[...] += jnp.dot(a_ref[...], w_ref[...], preferred_element_type=F32)

    if final_norm:
        @pl.when(k == pl.num_programs(1) - 1)
        def _():
            o_ref[...] = _rms(o_ref[...], g_ref[...])


def _ffn_down(a, w, x, g_final, *, tm=512, tk=1024):
    m, kdim = a.shape
    d = w.shape[1]
    tm, tk = min(tm, m), min(tk, kdim)
    assert m % tm == 0 and kdim % tk == 0
    final_norm = g_final is not None
    g = (g_final if final_norm else jnp.ones((d,), F32)).reshape(1, d).astype(F32)
    return pl.pallas_call(
        functools.partial(_ffn_down_body, final_norm=final_norm),
        grid=(m // tm, kdim // tk),
        in_specs=[
            pl.BlockSpec((tm, tk), lambda i, k: (i, k)),
            pl.BlockSpec((tk, d), lambda i, k: (k, 0)),
            pl.BlockSpec((tm, d), lambda i, k: (i, 0)),
            pl.BlockSpec((1, d), lambda i, k: (0, 0)),
        ],
        out_specs=pl.BlockSpec((tm, d), lambda i, k: (i, 0)),
        out_shape=jax.ShapeDtypeStruct((m, d), F32),
        compiler_params=_params("parallel", "arbitrary"),
        name="ffn_down",
    )(a, w, x, g)


def _layer(x, mem, g_mix, w_in, b_gate, w_proj_sb, w_proj_dsa, w_out, rel_bias,
           g_cross, g_mem, w_cq, w_ckv, w_co, g_ffn, w_up, conv_w, conv_b, w_down, g_final, batch, seq):
    d = x.shape[1]
    width = N_HEADS * HEAD_DIM
    idx_w = IDX_HEADS * IDX_DIM
    o_qi = 6 * width
    o_ki = o_qi + idx_w
    o_wi = o_ki + IDX_DIM
    o_g = o_wi + IDX_HEADS

    h = _rmsnorm_bf16(x, g_mix)
    w_in_t = w_in.T
    main = _matmul_ws(h, w_in_t, name="in_proj_main", n_blocks=o_ki // width,
                      out_dtype=BF16, tm=1024, tn=width)
    gates = _matmul_ws(h, w_in_t[o_g:], name="in_proj_gates", n_blocks=2 * d // width,
                       out_dtype=BF16, tm=1024, tn=width)
    zeros = jnp.zeros((d, LANES - IDX_DIM), F32)
    w_small = jnp.concatenate([
        w_in[:, o_ki:o_wi], zeros, zeros, w_in[:, o_ki:o_wi],
        jnp.pad(w_in[:, o_wi:o_g], ((0, 0), (0, LANES - IDX_HEADS)))], axis=1).astype(BF16)
    small = _matmul(h, w_small, name="in_proj_index", out_dtype=F32, tm=1024)

    o_sb = _sb_attention(main, batch, seq, q_col=0, k_col=1, v_col=2)
    tq = 256
    bias = _near_bias(rel_bias, tq)
    o_ds = _dsa_attention(main, small, bias, batch, seq, qd_col=3, kd_col=4, vd_col=5, qi_col=6, tq=tq)

    x = _merge_out(o_sb, o_ds, w_proj_sb.astype(BF16), w_proj_dsa.astype(BF16), w_out.astype(BF16),
                   gates, 0, b_gate, x)

    kv = _matmul(mem, w_ckv.astype(BF16), name="mem_kv_proj", gain=g_mem, out_dtype=BF16, tm=512)
    x, h_ffn = _cross_attention(x, kv, g_cross, w_cq.astype(BF16), w_co.astype(BF16), g_ffn, batch, seq)

    act = _ffn_up_gate(h_ffn, w_up, conv_w, conv_b, seq)
    return _ffn_down(act, w_down.astype(BF16), x, g_final)


def kernel(x, mem, g_mix, w_in, b_gate, w_proj_sb, w_proj_dsa, w_out, rel_bias, g_cross, g_mem,
           w_cq, w_ckv, w_co, g_ffn, w_up, conv_w, conv_b, w_down, g_final):
    batch, seq, d = x.shape
    h = x.reshape(batch * seq, d)
    mem2 = mem.reshape(batch * mem.shape[1], d)
    depth = g_mix.shape[0]
    for l in range(depth):
        h = _layer(h, mem2, g_mix[l], w_in[l], b_gate[l], w_proj_sb[l], w_proj_dsa[l], w_out[l],
                   rel_bias, g_cross[l], g_mem[l], w_cq[l], w_ckv[l], w_co[l], g_ffn[l], w_up[l],
                   conv_w[l], conv_b[l], w_down[l], g_final if l == depth - 1 else None, batch, seq)
    return h.reshape(batch, seq, d)
```

```python
import functools

import jax
import jax.numpy as jnp
from jax import lax
from jax.experimental import pallas as pl
from jax.experimental.pallas import tpu as pltpu

F32, BF16, I32 = jnp.float32, jnp.bfloat16, jnp.int32

EPS = 1e-6
HEAD_DIM = 128
N_HEADS = 8
IDX_HEADS = 16
IDX_DIM = 64
CHUNK = 64
TOPK_MAX = 256
N_BUCKETS = 32
MAX_DISTANCE = 128
MEM_HEADS = 4
CONV_WIDTH = 3

LANES = 128
SUBLANES = 8
VMEM_LIMIT_BYTES = 56 * 1024 * 1024
NEG_BIG = -1e30
EXP2_UNDERFLOW = -151.0
LOG2E = 1.4426950408889634
KEY_BITS = 32
SIGN_BIT = -(2 ** 31)

_NT = (((1,), (1,)), ((), ()))


def _params(*sem):
    return pltpu.CompilerParams(dimension_semantics=sem, vmem_limit_bytes=VMEM_LIMIT_BYTES)


def _rms(x, g):
    inv = lax.rsqrt(jnp.mean(x * x, axis=-1, keepdims=True) + EPS)
    return x * inv * g


def _mm_ws_body(a_ref, wt_ref, o_ref, wb_ref):
    @pl.when(pl.program_id(1) == 0)
    def _():
        wb_ref[...] = wt_ref[...].astype(BF16)

    o_ref[...] = lax.dot_general(a_ref[...], wb_ref[...], _NT,
                                 preferred_element_type=F32).astype(o_ref.dtype)


def _matmul_ws(a, w_t, *, name, first_row, n_blocks, out_dtype, tm, tn):
    m, k = a.shape
    tm = min(tm, m)
    assert m % tm == 0 and first_row + n_blocks * tn <= w_t.shape[0] and w_t.shape[1] == k
    assert first_row % SUBLANES == 0
    return pl.pallas_call(
        _mm_ws_body,
        grid=(n_blocks, m // tm),
        in_specs=[
            pl.BlockSpec((tm, k), lambda j, i: (i, 0)),
            pl.BlockSpec((pl.Element(tn), pl.Element(k)),
                         lambda j, i: (pl.multiple_of(first_row + j * tn, SUBLANES), 0)),
        ],
        out_specs=pl.BlockSpec((tm, tn), lambda j, i: (i, j)),
        out_shape=jax.ShapeDtypeStruct((m, n_blocks * tn), out_dtype),
        scratch_shapes=[pltpu.VMEM((tn, k), BF16)],
        compiler_params=_params("parallel", "arbitrary"),
        name=name,
    )(a, w_t)


def _norm_proj_body(x_ref, g_ref, w_ref, h_ref, p_ref):
    h = _rms(x_ref[...], g_ref[...]).astype(BF16)
    h_ref[...] = h
    p_ref[...] = jnp.dot(h, w_ref[...], preferred_element_type=F32).astype(p_ref.dtype)


def _rmsnorm_proj(x, g, w, *, name, proj_dtype, tm=512):
    m, d = x.shape
    n = w.shape[1]
    tm = min(tm, m)
    assert m % tm == 0
    return pl.pallas_call(
        _norm_proj_body,
        grid=(m // tm,),
        in_specs=[pl.BlockSpec((tm, d), lambda i: (i, 0)), pl.BlockSpec((1, d), lambda i: (0, 0)),
                  pl.BlockSpec((d, n), lambda i: (0, 0))],
        out_specs=[pl.BlockSpec((tm, d), lambda i: (i, 0)), pl.BlockSpec((tm, n), lambda i: (i, 0))],
        out_shape=[jax.ShapeDtypeStruct((m, d), BF16), jax.ShapeDtypeStruct((m, n), proj_dtype)],
        compiler_params=_params("parallel"),
        name=name,
    )(x, g.reshape(1, d).astype(F32), w)


def _values_by_key_block(arr, col, batch, seq, tq):
    width = N_HEADS * HEAD_DIM
    return arr[:, col * width:(col + 1) * width].reshape(batch, seq // tq, tq, width).swapaxes(2, 3)


def _sb_body(q_ref, k_ref, vt_ref, tri_ref, o_ref, qs_ref, z_ref, lb_ref, wb_ref, acc_ref, *, tq):
    i = pl.program_id(1)
    shape = (tq, tq)
    before = lax.broadcasted_iota(I32, shape, 0) < lax.broadcasted_iota(I32, shape, 1)
    heads = [slice(h * HEAD_DIM, (h + 1) * HEAD_DIM) for h in range(N_HEADS)]
    qs_ref[...] = (q_ref[...].astype(F32) * (HEAD_DIM ** -0.5 * LOG2E)).astype(BF16)
    acc_ref[...] = jnp.zeros(acc_ref.shape, F32)

    def tile(j, carry, diagonal):
        start = pl.multiple_of(j * tq, tq)
        for h, hs in enumerate(heads):
            z = lax.dot_general(k_ref[pl.ds(start, tq), hs], qs_ref[:, hs], _NT,
                                preferred_element_type=F32)
            log_keep = jnp.minimum(-z, 0.0) - jnp.log2(1.0 + jnp.exp2(-jnp.abs(z)))
            if diagonal:
                log_keep = jnp.where(before, log_keep, 0.0)
            z_ref[h] = z
            lb_ref[h] = log_keep.astype(BF16)
        new_carry = []
        for h in range(N_HEADS):
            c = jnp.dot(tri_ref[...], lb_ref[h], preferred_element_type=F32) + carry[h:h + 1, :]
            w = jnp.exp2(z_ref[h] + c)
            if diagonal:
                w = jnp.where(before, w, 0.0)
            wb_ref[h] = w.astype(BF16)
            new_carry.append(c[0:1, :])
        for h, hs in enumerate(heads):
            acc_ref[h] += jnp.dot(vt_ref[j, hs, :], wb_ref[h], preferred_element_type=F32)
        return jnp.concatenate(new_carry, axis=0)

    carry = tile(i, jnp.zeros((N_HEADS, tq), F32), True)

    def cond(state):
        j, live, _ = state
        return jnp.logical_and(j >= 0, live > EXP2_UNDERFLOW)

    def body(state):
        j, _, carry = state
        carry = tile(j, carry, False)
        return j - 1, jnp.max(carry), carry

    lax.while_loop(cond, body, (i - 1, jnp.max(carry), carry))
    for h, hs in enumerate(heads):
        o_ref[:, hs] = acc_ref[h].T.astype(o_ref.dtype)


def _sb_attention(qkv, batch, seq, *, q_col, k_col, v_col, tq=256):
    nq = seq // tq
    width = N_HEADS * HEAD_DIM
    v_t = _values_by_key_block(qkv, v_col, batch, seq, tq)
    tri = (jnp.arange(tq)[None, :] >= jnp.arange(tq)[:, None]).astype(BF16)
    resident = pl.Buffered(1)
    return pl.pallas_call(
        functools.partial(_sb_body, tq=tq),
        grid=(batch, nq),
        in_specs=[
            pl.BlockSpec((tq, width), lambda b, i: (b * nq + i, q_col)),
            pl.BlockSpec((seq, width), lambda b, i: (b, k_col), pipeline_mode=resident),
            pl.BlockSpec((None, nq, width, tq), lambda b, i: (b, 0, 0, 0), pipeline_mode=resident),
            pl.BlockSpec((tq, tq), lambda b, i: (0, 0), pipeline_mode=resident),
        ],
        out_specs=pl.BlockSpec((tq, width), lambda b, i: (b * nq + i, 0)),
        out_shape=jax.ShapeDtypeStruct((batch * seq, width), BF16),
        scratch_shapes=[
            pltpu.VMEM((tq, width), BF16),
            pltpu.VMEM((N_HEADS, tq, tq), F32),
            pltpu.VMEM((N_HEADS, tq, tq), BF16),
            pltpu.VMEM((N_HEADS, tq, tq), BF16),
            pltpu.VMEM((N_HEADS, HEAD_DIM, tq), F32),
        ],
        compiler_params=_params("parallel", "arbitrary"),
        name="sb_attention",
    )(qkv, qkv, v_t, tri)


def _bucket_thresholds():
    nb = N_BUCKETS // 2
    max_exact = nb // 2
    span = nb - max_exact
    out = []
    for k in range(1, span):
        n = max_exact
        while n ** span * max_exact ** k < MAX_DISTANCE ** k * max_exact ** span:
            n += 1
        out.append(n)
    return max_exact, out


def _bias_body(rb_ref, o_ref, *, tq):
    nb = N_BUCKETS // 2
    max_exact, steps = _bucket_thresholds()
    shape = (2 * tq, tq)
    rel = lax.broadcasted_iota(I32, shape, 0) - lax.broadcasted_iota(I32, shape, 1) - tq
    n = jnp.abs(rel)
    large = jnp.full(shape, max_exact, I32)
    for t in steps:
        large = large + (n >= t).astype(I32)
    bucket = jnp.where(rel > 0, nb, 0) + jnp.where(n < max_exact, n, large)
    for h in range(N_HEADS):
        val = jnp.zeros(shape, F32)
        for b in range(N_BUCKETS):
            val = jnp.where(bucket == b, rb_ref[b, h], val)
        o_ref[h] = (val - rb_ref[nb - 1, h]) * LOG2E


def _near_bias(rel_bias, tq):
    return pl.pallas_call(
        functools.partial(_bias_body, tq=tq),
        in_specs=[pl.BlockSpec(memory_space=pltpu.SMEM)],
        out_specs=pl.BlockSpec(memory_space=pltpu.VMEM),
        out_shape=jax.ShapeDtypeStruct((N_HEADS, 2 * tq, tq), F32),
        compiler_params=pltpu.CompilerParams(vmem_limit_bytes=VMEM_LIMIT_BYTES),
        name="dsa_near_bias",
    )(rel_bias.astype(F32))


def _order_key(x):
    bits = lax.bitcast_convert_type(x, I32)
    return bits ^ ((bits >> 31) | SIGN_BIT)


def _order_key_to_float(key):
    return lax.bitcast_convert_type(key ^ ((~key >> 31) | SIGN_BIT), F32)


def _bit_transpose32(words):
    a = list(words)
    j, m = 16, 0x0000FFFF
    while j:
        mask = jnp.int32(m - (1 << 32) if m >= 1 << 31 else m)
        k = 0
        while k < 32:
            t = (lax.shift_right_logical(a[k], jnp.int32(j)) ^ a[k + j]) & mask
            a[k] = a[k] ^ lax.shift_left(t, jnp.int32(j))
            a[k + j] = a[k + j] ^ t
            k = (k + j + 1) & ~j
        j >>= 1
        m = (m ^ (m << j)) & 0xFFFFFFFF
    return a


def _dsa_body(qd_ref, qi_ref, wq_ref, kd_ref, vt_ref, ki_ref, bias_ref, tri_ref, o_ref,
              sc_ref, plane_ref, qs_ref, m_ref, l_ref, acc_ref, lg_ref, bmax_ref, p_ref, *, tq, top):
    i = pl.program_id(1)
    shape = (tq, tq)
    key_row = lax.broadcasted_iota(I32, shape, 0)
    qry_col = lax.broadcasted_iota(I32, shape, 1)
    visible = key_row // CHUNK <= qry_col // CHUNK

    w_t = (wq_ref[...] * (IDX_DIM ** -0.5 * IDX_HEADS ** -0.5)).T

    heads_per_vreg = LANES // IDX_DIM

    def score_tile(j):
        start = pl.multiple_of(j * tq, tq)
        ki = [ki_ref[pl.ds(start, tq), c * LANES:(c + 1) * LANES].astype(BF16)
              for c in range(heads_per_vreg)]
        s = jnp.zeros(shape, F32)
        for h in range(IDX_HEADS):
            g, c = divmod(h, heads_per_vreg)
            d = lax.dot_general(ki[c], qi_ref[:, g * LANES:(g + 1) * LANES], _NT,
                                preferred_element_type=F32)
            s = s + w_t[h:h + 1, :] * jnp.maximum(d, 0.0)
        sc_ref[j] = s

    def key_planes(j):
        ukey = _order_key(sc_ref[j])
        planes = _bit_transpose32([ukey[g * SUBLANES:(g + 1) * SUBLANES, :] for g in range(KEY_BITS)])
        for b in range(KEY_BITS):
            plane_ref[b, pl.ds(pl.multiple_of(j * SUBLANES, SUBLANES), SUBLANES), :] = planes[b]

    @pl.when(i == 0)
    def _():
        plane_ref[...] = jnp.zeros(plane_ref.shape, I32)

    def score_step(j, carry):
        key_planes(j)
        score_tile(j + 1)
        return carry

    score_tile(0)
    lax.fori_loop(0, i, score_step, 0)
    key_planes(i)
    sc_ref[i] = jnp.where(visible, sc_ref[i], -jnp.inf)

    n_rows = plane_ref.shape[1]
    block_of_row = lax.broadcasted_iota(I32, (n_rows, tq), 0) // SUBLANES
    qry_of_col = lax.broadcasted_iota(I32, (n_rows, tq), 1)
    n_bits = (qry_of_col // CHUNK + 1) * (CHUNK // SUBLANES)
    diag_bits = jnp.where(n_bits >= KEY_BITS, -1, lax.shift_left(jnp.int32(1), n_bits) - 1)
    cand0 = jnp.where(block_of_row < i, -1, jnp.where(block_of_row == i, diag_bits, 0))

    def popcount_rows(words):
        return jnp.sum(lax.population_count(words), axis=0, keepdims=True)

    def bit_step(t, state):
        cand, n_above, thr_bits = state
        b = KEY_BITS - 1 - t
        ones = cand & plane_ref[b]
        n_ones = popcount_rows(ones)
        take = n_above + n_ones >= top
        cand = jnp.where(take, ones, cand ^ ones)
        n_above = jnp.where(take, n_above, n_above + n_ones)
        thr_bits = thr_bits | jnp.where(take, lax.shift_left(jnp.int32(1), b), 0)
        return cand, n_above, thr_bits

    zero = jnp.zeros((1, tq), I32)
    cand, n_above, thr_bits = lax.fori_loop(0, KEY_BITS, bit_step, (cand0, zero, zero))
    qry = lax.broadcasted_iota(I32, (1, tq), 1)
    n_visible = i * tq + (qry // CHUNK + 1) * CHUNK
    wanted = n_visible > top
    thr = jnp.where(wanted, _order_key_to_float(thr_bits), jnp.finfo(F32).min)
    tied = jnp.logical_and(wanted, n_above + popcount_rows(cand) > top)
    c_hi = n_above

    ones = jnp.ones((2 * SUBLANES, tq), BF16)

    def plain_mask(j, carry):
        sc_ref[j] = jnp.where(sc_ref[j] >= thr, 0.0, NEG_BIG)
        return carry

    def tie_mask(j, seen):
        s = sc_ref[j]
        equal = s == thr
        rank = jnp.dot(tri_ref[...], equal.astype(BF16), preferred_element_type=F32) + seen
        quota = jnp.where(tied, (top - c_hi).astype(F32), jnp.inf)
        keep_equal = jnp.where(rank < quota, 0.0, NEG_BIG)
        sc_ref[j] = jnp.where(s > thr, 0.0, jnp.where(equal, keep_equal, NEG_BIG))
        return seen + jnp.sum(equal.astype(F32), axis=0, keepdims=True)

    def with_ties():
        lax.fori_loop(0, i + 1, tie_mask, jnp.zeros((1, tq), F32))
        return jnp.int32(0)

    def without_ties():
        return lax.fori_loop(0, i + 1, plain_mask, jnp.int32(0))

    lax.cond(jnp.max(tied.astype(I32)) > 0, with_ties, without_ties)

    qs_ref[...] = (qd_ref[...].astype(F32) * (HEAD_DIM ** -0.5 * LOG2E)).astype(BF16)
    m_ref[...] = jnp.full(m_ref.shape, NEG_BIG, F32)
    l_ref[...] = jnp.zeros(l_ref.shape, F32)
    acc_ref[...] = jnp.zeros(acc_ref.shape, F32)

    heads = [slice(h * HEAD_DIM, (h + 1) * HEAD_DIM) for h in range(N_HEADS)]

    far, prev, diag = None, 0, 1

    def logits(u, near, slot):
        j = i - u
        start = pl.multiple_of(j * tq, tq)
        mask = sc_ref[j]
        block_max = []
        for h, hs in enumerate(heads):
            lg = lax.dot_general(kd_ref[pl.ds(start, tq), hs], qs_ref[:, hs], _NT,
                                 preferred_element_type=F32)
            if near is not None:
                lg = lg + bias_ref[h, near * tq:(near + 1) * tq, :]
            lg = lg + mask
            lg_ref[slot, h] = lg
            block_max.append(jnp.max(lg, axis=0, keepdims=True))
        bmax_ref[slot] = jnp.concatenate(block_max, axis=0)

    def values(u, slot):
        j = i - u
        m_old = m_ref[...]
        m_new = jnp.maximum(m_old, bmax_ref[slot])
        alpha = jnp.exp2(m_old - m_new)
        m_ref[...] = m_new
        for h in range(N_HEADS):
            p_ref[h] = jnp.exp2(lg_ref[slot, h] - m_new[h:h + 1, :]).astype(BF16)
        denom = []
        for h, hs in enumerate(heads):
            v_ext = jnp.concatenate([vt_ref[j, hs, :], ones], axis=0)
            pv = jnp.dot(v_ext, p_ref[h], preferred_element_type=F32)
            acc_ref[h] = alpha[h:h + 1, :] * acc_ref[h] + pv[:HEAD_DIM]
            denom.append(pv[HEAD_DIM:HEAD_DIM + 1])
        l_ref[...] = alpha * l_ref[...] + jnp.concatenate(denom, axis=0)

    def even_step(u, near_a, near_b):
        logits(u - 1, near_a, 1)
        values(u, 0)
        logits(u - 2, near_b, 0)
        values(u - 1, 1)

    @pl.when(i == 0)
    def _():
        logits(0, diag, 0)
        values(0, 0)

    @pl.when(i == 1)
    def _():
        logits(1, prev, 1)

    @pl.when(jnp.logical_and(i >= 2, i % 2 == 1))
    def _():
        logits(i, far, 1)
        logits(i - 1, far, 0)
        values(i, 1)

    @pl.when(jnp.logical_and(i >= 2, i % 2 == 0))
    def _():
        logits(i, far, 0)

    def far_pair(k, carry):
        even_step(2 * (i // 2 - k), far, far)
        return carry

    lax.fori_loop(0, i // 2 - 1, far_pair, 0)

    @pl.when(i >= 2)
    def _():
        even_step(2, prev, diag)

    @pl.when(i == 1)
    def _():
        logits(0, diag, 0)
        values(1, 1)

    @pl.when(i >= 1)
    def _():
        values(0, 0)

    for h in range(N_HEADS):
        o = acc_ref[h] / l_ref[h:h + 1, :]
        o_ref[:, h * HEAD_DIM:(h + 1) * HEAD_DIM] = o.T.astype(o_ref.dtype)


def _dsa_attention(main, small, bias, batch, seq, *, qd_col, kd_col, vd_col, qi_col, tq=256):
    nq = seq // tq
    width = N_HEADS * HEAD_DIM
    assert IDX_HEADS * IDX_DIM == width and tq == KEY_BITS * SUBLANES and tq % CHUNK == 0
    key_copies = LANES // IDX_DIM
    top = min(TOPK_MAX, seq // 4)
    v_t = _values_by_key_block(main, vd_col, batch, seq, tq)
    tri = (jnp.arange(tq)[None, :] < jnp.arange(tq)[:, None]).astype(BF16)
    resident = pl.Buffered(1)
    return pl.pallas_call(
        functools.partial(_dsa_body, tq=tq, top=top),
        grid=(batch, nq),
        in_specs=[
            pl.BlockSpec((tq, width), lambda b, i: (b * nq + i, qd_col)),
            pl.BlockSpec((tq, IDX_HEADS * IDX_DIM), lambda b, i: (b * nq + i, qi_col)),
            pl.BlockSpec((tq, LANES), lambda b, i: (b * nq + i, key_copies)),
            pl.BlockSpec((seq, width), lambda b, i: (b, kd_col), pipeline_mode=resident),
            pl.BlockSpec((None, nq, width, tq), lambda b, i: (b, 0, 0, 0), pipeline_mode=resident),
            pl.BlockSpec((seq, key_copies * LANES), lambda b, i: (b, 0), pipeline_mode=resident),
            pl.BlockSpec((N_HEADS, 2 * tq, tq), lambda b, i: (0, 0, 0), pipeline_mode=resident),
            pl.BlockSpec((tq, tq), lambda b, i: (0, 0), pipeline_mode=resident),
        ],
        out_specs=pl.BlockSpec((tq, width), lambda b, i: (b * nq + i, 0)),
        out_shape=jax.ShapeDtypeStruct((batch * seq, width), BF16),
        scratch_shapes=[
            pltpu.VMEM((nq, tq, tq), F32),
            pltpu.VMEM((KEY_BITS, nq * SUBLANES, tq), I32),
            pltpu.VMEM((tq, width), BF16),
            pltpu.VMEM((N_HEADS, tq), F32),
            pltpu.VMEM((N_HEADS, tq), F32),
            pltpu.VMEM((N_HEADS, HEAD_DIM, tq), F32),
            pltpu.VMEM((2, N_HEADS, tq, tq), F32),
            pltpu.VMEM((2, N_HEADS, tq), F32),
            pltpu.VMEM((N_HEADS, tq, tq), BF16),
        ],
        compiler_params=_params("parallel", "arbitrary"),
        name="dsa_attention",
    )(main, main, small, main, v_t, small, bias, tri)


def _merge_body(osb_ref, ods_ref, wsb_ref, wds_ref, gsb_ref, gds_ref, bsb_ref, bds_ref, wo_ref,
                x_ref, o_ref):
    p_sb = jnp.dot(osb_ref[...], wsb_ref[...], preferred_element_type=F32)
    p_ds = jnp.dot(ods_ref[...], wds_ref[...], preferred_element_type=F32)
    g_sb = jax.nn.sigmoid(gsb_ref[...].astype(F32) + bsb_ref[...])
    g_ds = jax.nn.sigmoid(gds_ref[...].astype(F32) + bds_ref[...])
    merged = (g_sb * p_sb + g_ds * p_ds).astype(BF16)
    o_ref[...] = x_ref[...] + jnp.dot(merged, wo_ref[...], preferred_element_type=F32)


def _merge_out(o_sb, o_ds, w_sb, w_ds, w_out, proj, gate_offset, b_gate, x, *, tm=256):
    m, k = o_sb.shape
    d = w_sb.shape[1]
    tm = min(tm, m)
    assert gate_offset % d == 0 and m % tm == 0
    g = gate_offset // d
    b_gate = b_gate.reshape(1, 2 * d).astype(F32)
    resident = pl.Buffered(1)
    return pl.pallas_call(
        _merge_body,
        grid=(m // tm,),
        in_specs=[
            pl.BlockSpec((tm, k), lambda i: (i, 0)),
            pl.BlockSpec((tm, k), lambda i: (i, 0)),
            pl.BlockSpec((k, d), lambda i: (0, 0), pipeline_mode=resident),
            pl.BlockSpec((k, d), lambda i: (0, 0), pipeline_mode=resident),
            pl.BlockSpec((tm, d), lambda i: (i, g)),
            pl.BlockSpec((tm, d), lambda i: (i, g + 1)),
            pl.BlockSpec((1, d), lambda i: (0, 0), pipeline_mode=resident),
            pl.BlockSpec((1, d), lambda i: (0, 1), pipeline_mode=resident),
            pl.BlockSpec((d, d), lambda i: (0, 0), pipeline_mode=resident),
            pl.BlockSpec((tm, d), lambda i: (i, 0)),
        ],
        out_specs=pl.BlockSpec((tm, d), lambda i: (i, 0)),
        out_shape=jax.ShapeDtypeStruct((m, d), F32),
        compiler_params=_params("parallel"),
        name="merge_out_proj",
    )(o_sb, o_ds, w_sb, w_ds, proj, proj, b_gate, b_gate, w_out, x)


def _cross_body(x_ref, g_ref, wq_ref, km_ref, vm_ref, wo_ref, gn_ref, o_ref, hn_ref):
    x = x_ref[...]
    h = _rms(x, g_ref[...]).astype(BF16)
    q = jnp.dot(h, wq_ref[...], preferred_element_type=F32) * HEAD_DIM ** -0.5
    q = q.astype(BF16)
    outs = []
    for hh in range(MEM_HEADS):
        hs = slice(hh * HEAD_DIM, (hh + 1) * HEAD_DIM)
        lg = lax.dot_general(q[:, hs], km_ref[:, hs], _NT, preferred_element_type=F32)
        p = jnp.exp(lg - jnp.max(lg, axis=1, keepdims=True))
        o = jnp.dot(p.astype(BF16), vm_ref[:, hs], preferred_element_type=F32)
        outs.append((o / jnp.sum(p, axis=1, keepdims=True)).astype(BF16))
    o = jnp.concatenate(outs, axis=1)
    y = x + jnp.dot(o, wo_ref[...], preferred_element_type=F32)
    o_ref[...] = y
    hn_ref[...] = _rms(y, gn_ref[...]).astype(hn_ref.dtype)


def _cross_attention(x, kv, g_cross, w_cq, w_co, g_next, batch, seq, *, tm=512):
    m, d = x.shape
    n_mem = kv.shape[0] // batch
    width = MEM_HEADS * HEAD_DIM
    tm = min(tm, seq)
    nt = seq // tm
    row_tile = pl.BlockSpec((tm, d), lambda b, i: (b * nt + i, 0))
    gain = pl.BlockSpec((1, d), lambda b, i: (0, 0))
    return pl.pallas_call(
        _cross_body,
        grid=(batch, nt),
        in_specs=[
            row_tile,
            gain,
            pl.BlockSpec((d, width), lambda b, i: (0, 0)),
            pl.BlockSpec((n_mem, width), lambda b, i: (b, 0)),
            pl.BlockSpec((n_mem, width), lambda b, i: (b, 1)),
            pl.BlockSpec((width, d), lambda b, i: (0, 0)),
            gain,
        ],
        out_specs=[row_tile, row_tile],
        out_shape=[jax.ShapeDtypeStruct((m, d), F32), jax.ShapeDtypeStruct((m, d), BF16)],
        compiler_params=_params("parallel", "parallel"),
        name="cross_attention",
    )(x, g_cross.reshape(1, d).astype(F32), w_cq, kv, kv, w_co, g_next.reshape(1, d).astype(F32))


def _delayed(u, tail, shift):
    rolled = pltpu.roll(u, shift, axis=0)
    row = lax.broadcasted_iota(I32, tail.shape, 0)
    head = jnp.where(row < shift, pltpu.roll(tail, shift, axis=0), rolled[:SUBLANES])
    return jnp.concatenate([head, rolled[SUBLANES:]], axis=0)


def _ffn_up_body(h_ref, wa_ref, wv_ref, cwa_ref, cwv_ref, cba_ref, cbv_ref, o_ref,
                 wab_ref, wvb_ref, halo_ref, *, tiles_per_seq):
    i = pl.program_id(1)

    @pl.when(i == 0)
    def _():
        wab_ref[...] = wa_ref[...].astype(BF16)
        wvb_ref[...] = wv_ref[...].astype(BF16)

    h = h_ref[...]
    tm = h.shape[0]
    sequence_start = i % tiles_per_seq == 0

    def conv(wb_ref, cw_ref, cb_ref, slot):
        u = jnp.dot(h, wb_ref[...], preferred_element_type=F32)
        tail = jnp.where(sequence_start, 0.0, halo_ref[slot])
        halo_ref[slot] = u[tm - SUBLANES:, :]
        c = cb_ref[...] + cw_ref[CONV_WIDTH - 1:CONV_WIDTH, :] * u
        for tap in range(CONV_WIDTH - 1):
            c = c + cw_ref[tap:tap + 1, :] * _delayed(u, tail, CONV_WIDTH - 1 - tap)
        return c

    a = conv(wab_ref, cwa_ref, cba_ref, 0)
    val = conv(wvb_ref, cwv_ref, cbv_ref, 1)
    o_ref[...] = (jax.nn.gelu(a) * val).astype(o_ref.dtype)


def _ffn_up_gate(h, w_up, conv_w, conv_b, seq, *, tm=1024, tn=512):
    m, d = h.shape
    two_ff = w_up.shape[1]
    d_ff = two_ff // 2
    tm, tn = min(tm, seq), min(tn, d_ff)
    assert seq % tm == 0 and d_ff % tn == 0 and tm >= SUBLANES >= CONV_WIDTH - 1
    nf = d_ff // tn
    conv_w = conv_w.astype(F32)
    conv_b = conv_b.reshape(1, two_ff).astype(F32)
    return pl.pallas_call(
        functools.partial(_ffn_up_body, tiles_per_seq=seq // tm),
        grid=(nf, m // tm),
        in_specs=[
            pl.BlockSpec((tm, d), lambda j, i: (i, 0)),
            pl.BlockSpec((d, tn), lambda j, i: (0, j)),
            pl.BlockSpec((d, tn), lambda j, i: (0, nf + j)),
            pl.BlockSpec((CONV_WIDTH, tn), lambda j, i: (0, j)),
            pl.BlockSpec((CONV_WIDTH, tn), lambda j, i: (0, nf + j)),
            pl.BlockSpec((1, tn), lambda j, i: (0, j)),
            pl.BlockSpec((1, tn), lambda j, i: (0, nf + j)),
        ],
        out_specs=pl.BlockSpec((tm, tn), lambda j, i: (i, j)),
        out_shape=jax.ShapeDtypeStruct((m, d_ff), BF16),
        scratch_shapes=[pltpu.VMEM((d, tn), BF16), pltpu.VMEM((d, tn), BF16),
                        pltpu.VMEM((2, SUBLANES, tn), F32)],
        compiler_params=_params("parallel", "arbitrary"),
        name="ffn_up_conv_gate",
    )(h, w_up, w_up, conv_w, conv_w, conv_b, conv_b)


def _ffn_down_body(a_ref, w_ref, x_ref, g_ref, o_ref, *, final_norm):
    k = pl.program_id(1)

    @pl.when(k == 0)
    def _():
        o_ref[...] = x_ref[...]

    o_ref[...] += jnp.dot(a_ref[...], w_ref[...], preferred_element_type=F32)

    if final_norm:
        @pl.when(k == pl.num_programs(1) - 1)
        def _():
            o_ref[...] = _rms(o_ref[...], g_ref[...])


def _ffn_down(a, w, x, g_final, *, tm=1024, tk=512):
    m, kdim = a.shape
    d = w.shape[1]
    tm, tk = min(tm, m), min(tk, kdim)
    assert m % tm == 0 and kdim % tk == 0
    final_norm = g_final is not None
    g = (g_final if final_norm else jnp.ones((d,), F32)).reshape(1, d).astype(F32)
    return pl.pallas_call(
        functools.partial(_ffn_down_body, final_norm=final_norm),
        grid=(m // tm, kdim // tk),
        in_specs=[
            pl.BlockSpec((tm, tk), lambda i, k: (i, k)),
            pl.BlockSpec((tk, d), lambda i, k: (k, 0)),
            pl.BlockSpec((tm, d), lambda i, k: (i, 0)),
            pl.BlockSpec((1, d), lambda i, k: (0, 0)),
        ],
        out_specs=pl.BlockSpec((tm, d), lambda i, k: (i, 0)),
        out_shape=jax.ShapeDtypeStruct((m, d), F32),
        compiler_params=_params("parallel", "arbitrary"),
        name="ffn_down",
    )(a, w, x, g)


def _layer(x, mem, g_mix, w_in, b_gate, w_proj_sb, w_proj_dsa, w_out, rel_bias,
           g_cross, g_mem, w_cq, w_ckv, w_co, g_ffn, w_up, conv_w, conv_b, w_down, g_final, batch, seq):
    d = x.shape[1]
    width = N_HEADS * HEAD_DIM
    idx_w = IDX_HEADS * IDX_DIM
    o_qi = 6 * width
    o_ki = o_qi + idx_w
    o_wi = o_ki + IDX_DIM
    o_g = o_wi + IDX_HEADS

    zeros = jnp.zeros((d, LANES - IDX_DIM), F32)
    w_small = jnp.concatenate([
        w_in[:, o_ki:o_wi], zeros, zeros, w_in[:, o_ki:o_wi],
        jnp.pad(w_in[:, o_wi:o_g], ((0, 0), (0, LANES - IDX_HEADS)))], axis=1).astype(BF16)
    h, small = _rmsnorm_proj(x, g_mix, w_small, name="mixer_norm_index_proj", proj_dtype=F32)
    w_in_t = w_in.T
    main = _matmul_ws(h, w_in_t, name="in_proj_main", first_row=0, n_blocks=o_ki // width,
                      out_dtype=BF16, tm=1024, tn=width)
    gates = _matmul_ws(h, w_in_t, name="in_proj_gates", first_row=o_g, n_blocks=2 * d // width,
                       out_dtype=BF16, tm=1024, tn=width)

    o_sb = _sb_attention(main, batch, seq, q_col=0, k_col=1, v_col=2)
    tq = 256
    bias = _near_bias(rel_bias, tq)
    o_ds = _dsa_attention(main, small, bias, batch, seq, qd_col=3, kd_col=4, vd_col=5, qi_col=6, tq=tq)

    x = _merge_out(o_sb, o_ds, w_proj_sb.astype(BF16), w_proj_dsa.astype(BF16), w_out.astype(BF16),
                   gates, 0, b_gate, x)

    _, kv = _rmsnorm_proj(mem, g_mem, w_ckv.astype(BF16), name="mem_norm_kv_proj", proj_dtype=BF16)
    x, h_ffn = _cross_attention(x, kv, g_cross, w_cq.astype(BF16), w_co.astype(BF16), g_ffn, batch, seq)

    act = _ffn_up_gate(h_ffn, w_up, conv_w, conv_b, seq)
    return _ffn_down(act, w_down.astype(BF16), x, g_final)


def kernel(x, mem, g_mix, w_in, b_gate, w_proj_sb, w_proj_dsa, w_out, rel_bias, g_cross, g_mem,
           w_cq, w_ckv, w_co, g_ffn, w_up, conv_w, conv_b, w_down, g_final):
    batch, seq, d = x.shape
    h = x.reshape(batch * seq, d)
    mem2 = mem.reshape(batch * mem.shape[1], d)
    depth = g_mix.shape[0]
    for l in range(depth):
        h = _layer(h, mem2, g_mix[l], w_in[l], b_gate[l], w_proj_sb[l], w_proj_dsa[l], w_out[l],
                   rel_bias, g_cross[l], g_mem[l], w_cq[l], w_ckv[l], w_co[l], g_ffn[l], w_up[l],
                   conv_w[l], conv_b[l], w_down[l], g_final if l == depth - 1 else None, batch, seq)
    return h.reshape(batch, seq, d)
```

```python
import functools

import jax
import jax.numpy as jnp
from jax import lax
from jax.experimental import pallas as pl
from jax.experimental.pallas import tpu as pltpu

F32, BF16, I32 = jnp.float32, jnp.bfloat16, jnp.int32

EPS = 1e-6
HEAD_DIM = 128
N_HEADS = 8
IDX_HEADS = 16
IDX_DIM = 64
CHUNK = 64
TOPK_MAX = 256
N_BUCKETS = 32
MAX_DISTANCE = 128
MEM_HEADS = 4
CONV_WIDTH = 3

LANES = 128
SUBLANES = 8
VMEM_LIMIT_BYTES = 56 * 1024 * 1024
NEG_BIG = -1e30
EXP2_UNDERFLOW = -151.0
LOG2E = 1.4426950408889634
KEY_BITS = 32
SIGN_BIT = -(2 ** 31)

_NT = (((1,), (1,)), ((), ()))


def _params(*sem):
    return pltpu.CompilerParams(dimension_semantics=sem, vmem_limit_bytes=VMEM_LIMIT_BYTES)


def _rms(x, g):
    inv = lax.rsqrt(jnp.mean(x * x, axis=-1, keepdims=True) + EPS)
    return x * inv * g


def _mm_ws_body(a_ref, wt_ref, o_ref, wb_ref):
    @pl.when(pl.program_id(1) == 0)
    def _():
        wb_ref[...] = wt_ref[...].astype(BF16)

    o_ref[...] = lax.dot_general(a_ref[...], wb_ref[...], _NT,
                                 preferred_element_type=F32).astype(o_ref.dtype)


def _row_window(first_rows, tn, k):
    assert all(r % SUBLANES == 0 for r in first_rows)

    def index_map(j, i):
        row = jnp.int32(first_rows[0])
        for step, first in enumerate(first_rows[1:], start=1):
            row = jnp.where(j >= step, first, row)
        return pl.multiple_of(row, SUBLANES), 0

    return pl.BlockSpec((pl.Element(tn), pl.Element(k)), index_map)


def _matmul_ws(a, w_t, *, name, first_rows, out_dtype, tm, tn):
    m, k = a.shape
    tm = min(tm, m)
    assert m % tm == 0 and max(first_rows) + tn <= w_t.shape[0] and w_t.shape[1] == k
    return pl.pallas_call(
        _mm_ws_body,
        grid=(len(first_rows), m // tm),
        in_specs=[pl.BlockSpec((tm, k), lambda j, i: (i, 0)), _row_window(first_rows, tn, k)],
        out_specs=pl.BlockSpec((tm, tn), lambda j, i: (i, j)),
        out_shape=jax.ShapeDtypeStruct((m, len(first_rows) * tn), out_dtype),
        scratch_shapes=[pltpu.VMEM((tn, k), BF16)],
        compiler_params=_params("parallel", "arbitrary"),
        name=name,
    )(a, w_t)


def _mm_ws_t_body(wt_ref, a_ref, o_ref, wb_ref):
    @pl.when(pl.program_id(1) == 0)
    def _():
        wb_ref[...] = wt_ref[...].astype(BF16)

    o_ref[...] = lax.dot_general(wb_ref[...], a_ref[...], _NT,
                                 preferred_element_type=F32).astype(o_ref.dtype)


def _matmul_ws_t(a, w_t, *, name, first_rows, tn, tq):
    m, k = a.shape
    assert m % tq == 0 and max(first_rows) + tn <= w_t.shape[0] and w_t.shape[1] == k
    return pl.pallas_call(
        _mm_ws_t_body,
        grid=(len(first_rows), m // tq),
        in_specs=[_row_window(first_rows, tn, k), pl.BlockSpec((tq, k), lambda j, i: (i, 0))],
        out_specs=pl.BlockSpec((None, None, tn, tq), lambda j, i: (j, i, 0, 0)),
        out_shape=jax.ShapeDtypeStruct((len(first_rows), m // tq, tn, tq), BF16),
        scratch_shapes=[pltpu.VMEM((tn, k), BF16)],
        compiler_params=_params("parallel", "arbitrary"),
        name=name,
    )(w_t, a)


def _norm_proj_body(x_ref, g_ref, w_ref, h_ref, p_ref):
    h = _rms(x_ref[...], g_ref[...]).astype(BF16)
    h_ref[...] = h
    p_ref[...] = jnp.dot(h, w_ref[...], preferred_element_type=F32).astype(p_ref.dtype)


def _rmsnorm_proj(x, g, w, *, name, proj_dtype, tm=512):
    m, d = x.shape
    n = w.shape[1]
    tm = min(tm, m)
    assert m % tm == 0
    return pl.pallas_call(
        _norm_proj_body,
        grid=(m // tm,),
        in_specs=[pl.BlockSpec((tm, d), lambda i: (i, 0)), pl.BlockSpec((1, d), lambda i: (0, 0)),
                  pl.BlockSpec((d, n), lambda i: (0, 0))],
        out_specs=[pl.BlockSpec((tm, d), lambda i: (i, 0)), pl.BlockSpec((tm, n), lambda i: (i, 0))],
        out_shape=[jax.ShapeDtypeStruct((m, d), BF16), jax.ShapeDtypeStruct((m, n), proj_dtype)],
        compiler_params=_params("parallel"),
        name=name,
    )(x, g.reshape(1, d).astype(F32), w)


def _sb_body(q_ref, k_ref, vt_ref, tri_ref, o_ref, qs_ref, z_ref, lb_ref, wb_ref, acc_ref, *, tq):
    i = pl.program_id(1)
    shape = (tq, tq)
    before = lax.broadcasted_iota(I32, shape, 0) < lax.broadcasted_iota(I32, shape, 1)
    heads = [slice(h * HEAD_DIM, (h + 1) * HEAD_DIM) for h in range(N_HEADS)]
    qs_ref[...] = (q_ref[...].astype(F32) * (HEAD_DIM ** -0.5 * LOG2E)).astype(BF16)
    acc_ref[...] = jnp.zeros(acc_ref.shape, F32)

    def tile(j, carry, diagonal):
        start = pl.multiple_of(j * tq, tq)
        for h, hs in enumerate(heads):
            z = lax.dot_general(k_ref[pl.ds(start, tq), hs], qs_ref[:, hs], _NT,
                                preferred_element_type=F32)
            neg_z = -z
            log_keep = jnp.minimum(neg_z, 0.0) - jnp.log2(1.0 + jnp.exp2(jnp.minimum(z, neg_z)))
            if diagonal:
                log_keep = jnp.where(before, log_keep, 0.0)
            z_ref[h] = z
            lb_ref[h] = log_keep.astype(BF16)
        new_carry = []
        for h in range(N_HEADS):
            c = jnp.dot(tri_ref[...], lb_ref[h], preferred_element_type=F32) + carry[h:h + 1, :]
            w = jnp.exp2(z_ref[h] + c)
            if diagonal:
                w = jnp.where(before, w, 0.0)
            wb_ref[h] = w.astype(BF16)
            new_carry.append(c[0:1, :])
        for h, hs in enumerate(heads):
            acc_ref[h] += jnp.dot(vt_ref[j, hs, :], wb_ref[h], preferred_element_type=F32)
        return jnp.concatenate(new_carry, axis=0)

    carry = tile(i, jnp.zeros((N_HEADS, tq), F32), True)

    def cond(state):
        j, live, _ = state
        return jnp.logical_and(j >= 0, live > EXP2_UNDERFLOW)

    def body(state):
        j, _, carry = state
        carry = tile(j, carry, False)
        return j - 1, jnp.max(carry), carry

    lax.while_loop(cond, body, (i - 1, jnp.max(carry), carry))
    for h, hs in enumerate(heads):
        o_ref[:, hs] = acc_ref[h].T.astype(o_ref.dtype)


def _sb_attention(qk, v_t, batch, seq, *, q_col, k_col, v_branch, tq):
    nq = seq // tq
    width = N_HEADS * HEAD_DIM
    tri = (jnp.arange(tq)[None, :] >= jnp.arange(tq)[:, None]).astype(BF16)
    resident = pl.Buffered(1)
    return pl.pallas_call(
        functools.partial(_sb_body, tq=tq),
        grid=(batch, nq),
        in_specs=[
            pl.BlockSpec((tq, width), lambda b, i: (b * nq + i, q_col)),
            pl.BlockSpec((seq, width), lambda b, i: (b, k_col), pipeline_mode=resident),
            pl.BlockSpec((None, None, nq, width, tq), lambda b, i: (v_branch, b, 0, 0, 0),
                         pipeline_mode=resident),
            pl.BlockSpec((tq, tq), lambda b, i: (0, 0), pipeline_mode=resident),
        ],
        out_specs=pl.BlockSpec((tq, width), lambda b, i: (b * nq + i, 0)),
        out_shape=jax.ShapeDtypeStruct((batch * seq, width), BF16),
        scratch_shapes=[
            pltpu.VMEM((tq, width), BF16),
            pltpu.VMEM((N_HEADS, tq, tq), F32),
            pltpu.VMEM((N_HEADS, tq, tq), BF16),
            pltpu.VMEM((N_HEADS, tq, tq), BF16),
            pltpu.VMEM((N_HEADS, HEAD_DIM, tq), F32),
        ],
        compiler_params=_params("parallel", "arbitrary"),
        name="sb_attention",
    )(qk, qk, v_t, tri)


def _bucket_thresholds():
    nb = N_BUCKETS // 2
    max_exact = nb // 2
    span = nb - max_exact
    out = []
    for k in range(1, span):
        n = max_exact
        while n ** span * max_exact ** k < MAX_DISTANCE ** k * max_exact ** span:
            n += 1
        out.append(n)
    return max_exact, out


def _bias_body(rb_ref, o_ref, *, tq):
    nb = N_BUCKETS // 2
    max_exact, steps = _bucket_thresholds()
    shape = (2 * tq, tq)
    rel = lax.broadcasted_iota(I32, shape, 0) - lax.broadcasted_iota(I32, shape, 1) - tq
    n = jnp.abs(rel)
    large = jnp.full(shape, max_exact, I32)
    for t in steps:
        large = large + (n >= t).astype(I32)
    bucket = jnp.where(rel > 0, nb, 0) + jnp.where(n < max_exact, n, large)
    for h in range(N_HEADS):
        val = jnp.zeros(shape, F32)
        for b in range(N_BUCKETS):
            val = jnp.where(bucket == b, rb_ref[b, h], val)
        o_ref[h] = (val - rb_ref[nb - 1, h]) * LOG2E


def _near_bias(rel_bias, tq):
    return pl.pallas_call(
        functools.partial(_bias_body, tq=tq),
        in_specs=[pl.BlockSpec(memory_space=pltpu.SMEM)],
        out_specs=pl.BlockSpec(memory_space=pltpu.VMEM),
        out_shape=jax.ShapeDtypeStruct((N_HEADS, 2 * tq, tq), F32),
        compiler_params=pltpu.CompilerParams(vmem_limit_bytes=VMEM_LIMIT_BYTES),
        name="dsa_near_bias",
    )(rel_bias.astype(F32))


def _order_key(x):
    bits = lax.bitcast_convert_type(x, I32)
    return bits ^ ((bits >> 31) | SIGN_BIT)


def _order_key_to_float(key):
    return lax.bitcast_convert_type(key ^ ((~key >> 31) | SIGN_BIT), F32)


def _bit_transpose32(words):
    a = list(words)
    j, m = 16, 0x0000FFFF
    while j:
        mask = jnp.int32(m - (1 << 32) if m >= 1 << 31 else m)
        k = 0
        while k < 32:
            t = (lax.shift_right_logical(a[k], jnp.int32(j)) ^ a[k + j]) & mask
            a[k] = a[k] ^ lax.shift_left(t, jnp.int32(j))
            a[k + j] = a[k + j] ^ t
            k = (k + j + 1) & ~j
        j >>= 1
        m = (m ^ (m << j)) & 0xFFFFFFFF
    return a


def _dsa_body(qd_ref, qi_ref, wq_ref, kd_ref, vt_ref, ki_ref, bias_ref, tri_ref, o_ref,
              sc_ref, plane_ref, qs_ref, m_ref, l_ref, acc_ref, lg_ref, bmax_ref, p_ref, *, tq, top):
    i = pl.program_id(1)
    shape = (tq, tq)
    key_row = lax.broadcasted_iota(I32, shape, 0)
    qry_col = lax.broadcasted_iota(I32, shape, 1)
    visible = key_row // CHUNK <= qry_col // CHUNK

    w_t = (wq_ref[...] * (IDX_DIM ** -0.5 * IDX_HEADS ** -0.5)).T

    heads_per_vreg = LANES // IDX_DIM

    def score_tile(j):
        start = pl.multiple_of(j * tq, tq)
        ki = [ki_ref[pl.ds(start, tq), c * LANES:(c + 1) * LANES].astype(BF16)
              for c in range(heads_per_vreg)]
        s = jnp.zeros(shape, F32)
        for h in range(IDX_HEADS):
            g, c = divmod(h, heads_per_vreg)
            d = lax.dot_general(ki[c], qi_ref[:, g * LANES:(g + 1) * LANES], _NT,
                                preferred_element_type=F32)
            s = s + w_t[h:h + 1, :] * jnp.maximum(d, 0.0)
        sc_ref[j] = s

    def key_planes(j):
        ukey = _order_key(sc_ref[j])
        planes = _bit_transpose32([ukey[g * SUBLANES:(g + 1) * SUBLANES, :] for g in range(KEY_BITS)])
        for b in range(KEY_BITS):
            plane_ref[b, pl.ds(pl.multiple_of(j * SUBLANES, SUBLANES), SUBLANES), :] = planes[b]

    @pl.when(i == 0)
    def _():
        plane_ref[...] = jnp.zeros(plane_ref.shape, I32)

    def score_step(j, carry):
        key_planes(j)
        score_tile(j + 1)
        return carry

    score_tile(0)
    lax.fori_loop(0, i, score_step, 0)
    key_planes(i)
    sc_ref[i] = jnp.where(visible, sc_ref[i], -jnp.inf)

    n_rows = plane_ref.shape[1]
    block_of_row = lax.broadcasted_iota(I32, (n_rows, tq), 0) // SUBLANES
    qry_of_col = lax.broadcasted_iota(I32, (n_rows, tq), 1)
    n_bits = (qry_of_col // CHUNK + 1) * (CHUNK // SUBLANES)
    diag_bits = jnp.where(n_bits >= KEY_BITS, -1, lax.shift_left(jnp.int32(1), n_bits) - 1)
    cand0 = jnp.where(block_of_row < i, -1, jnp.where(block_of_row == i, diag_bits, 0))

    def popcount_rows(words):
        return jnp.sum(lax.population_count(words), axis=0, keepdims=True)

    def bit_step(t, state):
        cand, n_above, thr_bits = state
        b = KEY_BITS - 1 - t
        ones = cand & plane_ref[b]
        n_ones = popcount_rows(ones)
        take = n_above + n_ones >= top
        cand = jnp.where(take, ones, cand ^ ones)
        n_above = jnp.where(take, n_above, n_above + n_ones)
        thr_bits = thr_bits | jnp.where(take, lax.shift_left(jnp.int32(1), b), 0)
        return cand, n_above, thr_bits

    zero = jnp.zeros((1, tq), I32)
    cand, n_above, thr_bits = lax.fori_loop(0, KEY_BITS, bit_step, (cand0, zero, zero))
    qry = lax.broadcasted_iota(I32, (1, tq), 1)
    n_visible = i * tq + (qry // CHUNK + 1) * CHUNK
    wanted = n_visible > top
    thr = jnp.where(wanted, _order_key_to_float(thr_bits), jnp.finfo(F32).min)
    tied = jnp.logical_and(wanted, n_above + popcount_rows(cand) > top)
    c_hi = n_above

    ones = jnp.ones((2 * SUBLANES, tq), BF16)

    def plain_mask(j, carry):
        sc_ref[j] = jnp.where(sc_ref[j] >= thr, 0.0, NEG_BIG)
        return carry

    def tie_mask(j, seen):
        s = sc_ref[j]
        equal = s == thr
        rank = jnp.dot(tri_ref[...], equal.astype(BF16), preferred_element_type=F32) + seen
        quota = jnp.where(tied, (top - c_hi).astype(F32), jnp.inf)
        keep_equal = jnp.where(rank < quota, 0.0, NEG_BIG)
        sc_ref[j] = jnp.where(s > thr, 0.0, jnp.where(equal, keep_equal, NEG_BIG))
        return seen + jnp.sum(equal.astype(F32), axis=0, keepdims=True)

    def with_ties():
        lax.fori_loop(0, i + 1, tie_mask, jnp.zeros((1, tq), F32))
        return jnp.int32(0)

    def without_ties():
        return lax.fori_loop(0, i + 1, plain_mask, jnp.int32(0))

    lax.cond(jnp.max(tied.astype(I32)) > 0, with_ties, without_ties)

    qs_ref[...] = (qd_ref[...].astype(F32) * (HEAD_DIM ** -0.5 * LOG2E)).astype(BF16)
    m_ref[...] = jnp.full(m_ref.shape, NEG_BIG, F32)
    l_ref[...] = jnp.zeros(l_ref.shape, F32)
    acc_ref[...] = jnp.zeros(acc_ref.shape, F32)

    heads = [slice(h * HEAD_DIM, (h + 1) * HEAD_DIM) for h in range(N_HEADS)]

    far, prev, diag = None, 0, 1

    def logits(u, near, slot):
        j = i - u
        start = pl.multiple_of(j * tq, tq)
        mask = sc_ref[j]
        block_max = []
        for h, hs in enumerate(heads):
            lg = lax.dot_general(kd_ref[pl.ds(start, tq), hs], qs_ref[:, hs], _NT,
                                 preferred_element_type=F32)
            if near is not None:
                lg = lg + bias_ref[h, near * tq:(near + 1) * tq, :]
            lg = lg + mask
            lg_ref[slot, h] = lg
            block_max.append(jnp.max(lg, axis=0, keepdims=True))
        bmax_ref[slot] = jnp.concatenate(block_max, axis=0)

    def values(u, slot):
        j = i - u
        m_old = m_ref[...]
        m_new = jnp.maximum(m_old, bmax_ref[slot])
        alpha = jnp.exp2(m_old - m_new)
        m_ref[...] = m_new
        for h in range(N_HEADS):
            p_ref[h] = jnp.exp2(lg_ref[slot, h] - m_new[h:h + 1, :]).astype(BF16)
        denom = []
        for h, hs in enumerate(heads):
            v_ext = jnp.concatenate([vt_ref[j, hs, :], ones], axis=0)
            pv = jnp.dot(v_ext, p_ref[h], preferred_element_type=F32)
            acc_ref[h] = alpha[h:h + 1, :] * acc_ref[h] + pv[:HEAD_DIM]
            denom.append(pv[HEAD_DIM:HEAD_DIM + 1])
        l_ref[...] = alpha * l_ref[...] + jnp.concatenate(denom, axis=0)

    def even_step(u, near_a, near_b):
        logits(u - 1, near_a, 1)
        values(u, 0)
        logits(u - 2, near_b, 0)
        values(u - 1, 1)

    @pl.when(i == 0)
    def _():
        logits(0, diag, 0)
        values(0, 0)

    @pl.when(i == 1)
    def _():
        logits(1, prev, 1)

    @pl.when(jnp.logical_and(i >= 2, i % 2 == 1))
    def _():
        logits(i, far, 1)
        logits(i - 1, far, 0)
        values(i, 1)

    @pl.when(jnp.logical_and(i >= 2, i % 2 == 0))
    def _():
        logits(i, far, 0)

    def far_pair(k, carry):
        even_step(2 * (i // 2 - k), far, far)
        return carry

    lax.fori_loop(0, i // 2 - 1, far_pair, 0)

    @pl.when(i >= 2)
    def _():
        even_step(2, prev, diag)

    @pl.when(i == 1)
    def _():
        logits(0, diag, 0)
        values(1, 1)

    @pl.when(i >= 1)
    def _():
        values(0, 0)

    for h in range(N_HEADS):
        o = acc_ref[h] / l_ref[h:h + 1, :]
        o_ref[:, h * HEAD_DIM:(h + 1) * HEAD_DIM] = o.T.astype(o_ref.dtype)


def _dsa_attention(main, v_t, small, bias, batch, seq, *, qd_col, kd_col, qi_col, v_branch, tq):
    nq = seq // tq
    width = N_HEADS * HEAD_DIM
    assert IDX_HEADS * IDX_DIM == width and tq == KEY_BITS * SUBLANES and tq % CHUNK == 0
    key_copies = LANES // IDX_DIM
    top = min(TOPK_MAX, seq // 4)
    tri = (jnp.arange(tq)[None, :] < jnp.arange(tq)[:, None]).astype(BF16)
    resident = pl.Buffered(1)
    return pl.pallas_call(
        functools.partial(_dsa_body, tq=tq, top=top),
        grid=(batch, nq),
        in_specs=[
            pl.BlockSpec((tq, width), lambda b, i: (b * nq + i, qd_col)),
            pl.BlockSpec((tq, IDX_HEADS * IDX_DIM), lambda b, i: (b * nq + i, qi_col)),
            pl.BlockSpec((tq, LANES), lambda b, i: (b * nq + i, key_copies)),
            pl.BlockSpec((seq, width), lambda b, i: (b, kd_col), pipeline_mode=resident),
            pl.BlockSpec((None, None, nq, width, tq), lambda b, i: (v_branch, b, 0, 0, 0),
                         pipeline_mode=resident),
            pl.BlockSpec((seq, key_copies * LANES), lambda b, i: (b, 0), pipeline_mode=resident),
            pl.BlockSpec((N_HEADS, 2 * tq, tq), lambda b, i: (0, 0, 0), pipeline_mode=resident),
            pl.BlockSpec((tq, tq), lambda b, i: (0, 0), pipeline_mode=resident),
        ],
        out_specs=pl.BlockSpec((tq, width), lambda b, i: (b * nq + i, 0)),
        out_shape=jax.ShapeDtypeStruct((batch * seq, width), BF16),
        scratch_shapes=[
            pltpu.VMEM((nq, tq, tq), F32),
            pltpu.VMEM((KEY_BITS, nq * SUBLANES, tq), I32),
            pltpu.VMEM((tq, width), BF16),
            pltpu.VMEM((N_HEADS, tq), F32),
            pltpu.VMEM((N_HEADS, tq), F32),
            pltpu.VMEM((N_HEADS, HEAD_DIM, tq), F32),
            pltpu.VMEM((2, N_HEADS, tq, tq), F32),
            pltpu.VMEM((2, N_HEADS, tq), F32),
            pltpu.VMEM((N_HEADS, tq, tq), BF16),
        ],
        compiler_params=_params("parallel", "arbitrary"),
        name="dsa_attention",
    )(main, main, small, main, v_t, small, bias, tri)


def _merge_body(osb_ref, ods_ref, wsb_ref, wds_ref, gsb_ref, gds_ref, bsb_ref, bds_ref, wo_ref,
                x_ref, o_ref):
    p_sb = jnp.dot(osb_ref[...], wsb_ref[...], preferred_element_type=F32)
    p_ds = jnp.dot(ods_ref[...], wds_ref[...], preferred_element_type=F32)
    g_sb = jax.nn.sigmoid(gsb_ref[...].astype(F32) + bsb_ref[...])
    g_ds = jax.nn.sigmoid(gds_ref[...].astype(F32) + bds_ref[...])
    merged = (g_sb * p_sb + g_ds * p_ds).astype(BF16)
    o_ref[...] = x_ref[...] + jnp.dot(merged, wo_ref[...], preferred_element_type=F32)


def _merge_out(o_sb, o_ds, w_sb, w_ds, w_out, proj, gate_offset, b_gate, x, *, tm=256):
    m, k = o_sb.shape
    d = w_sb.shape[1]
    tm = min(tm, m)
    assert gate_offset % d == 0 and m % tm == 0
    g = gate_offset // d
    b_gate = b_gate.reshape(1, 2 * d).astype(F32)
    resident = pl.Buffered(1)
    return pl.pallas_call(
        _merge_body,
        grid=(m // tm,),
        in_specs=[
            pl.BlockSpec((tm, k), lambda i: (i, 0)),
            pl.BlockSpec((tm, k), lambda i: (i, 0)),
            pl.BlockSpec((k, d), lambda i: (0, 0), pipeline_mode=resident),
            pl.BlockSpec((k, d), lambda i: (0, 0), pipeline_mode=resident),
            pl.BlockSpec((tm, d), lambda i: (i, g)),
            pl.BlockSpec((tm, d), lambda i: (i, g + 1)),
            pl.BlockSpec((1, d), lambda i: (0, 0), pipeline_mode=resident),
            pl.BlockSpec((1, d), lambda i: (0, 1), pipeline_mode=resident),
            pl.BlockSpec((d, d), lambda i: (0, 0), pipeline_mode=resident),
            pl.BlockSpec((tm, d), lambda i: (i, 0)),
        ],
        out_specs=pl.BlockSpec((tm, d), lambda i: (i, 0)),
        out_shape=jax.ShapeDtypeStruct((m, d), F32),
        compiler_params=_params("parallel"),
        name="merge_out_proj",
    )(o_sb, o_ds, w_sb, w_ds, proj, proj, b_gate, b_gate, w_out, x)


def _cross_body(x_ref, g_ref, wq_ref, km_ref, vm_ref, wo_ref, gn_ref, o_ref, hn_ref):
    x = x_ref[...]
    h = _rms(x, g_ref[...]).astype(BF16)
    q = jnp.dot(h, wq_ref[...], preferred_element_type=F32) * HEAD_DIM ** -0.5
    q = q.astype(BF16)
    outs = []
    for hh in range(MEM_HEADS):
        hs = slice(hh * HEAD_DIM, (hh + 1) * HEAD_DIM)
        lg = lax.dot_general(q[:, hs], km_ref[:, hs], _NT, preferred_element_type=F32)
        p = jnp.exp(lg - jnp.max(lg, axis=1, keepdims=True))
        o = jnp.dot(p.astype(BF16), vm_ref[:, hs], preferred_element_type=F32)
        outs.append((o / jnp.sum(p, axis=1, keepdims=True)).astype(BF16))
    o = jnp.concatenate(outs, axis=1)
    y = x + jnp.dot(o, wo_ref[...], preferred_element_type=F32)
    o_ref[...] = y
    hn_ref[...] = _rms(y, gn_ref[...]).astype(hn_ref.dtype)


def _cross_attention(x, kv, g_cross, w_cq, w_co, g_next, batch, seq, *, tm=512):
    m, d = x.shape
    n_mem = kv.shape[0] // batch
    width = MEM_HEADS * HEAD_DIM
    tm = min(tm, seq)
    nt = seq // tm
    row_tile = pl.BlockSpec((tm, d), lambda b, i: (b * nt + i, 0))
    gain = pl.BlockSpec((1, d), lambda b, i: (0, 0))
    return pl.pallas_call(
        _cross_body,
        grid=(batch, nt),
        in_specs=[
            row_tile,
            gain,
            pl.BlockSpec((d, width), lambda b, i: (0, 0)),
            pl.BlockSpec((n_mem, width), lambda b, i: (b, 0)),
            pl.BlockSpec((n_mem, width), lambda b, i: (b, 1)),
            pl.BlockSpec((width, d), lambda b, i: (0, 0)),
            gain,
        ],
        out_specs=[row_tile, row_tile],
        out_shape=[jax.ShapeDtypeStruct((m, d), F32), jax.ShapeDtypeStruct((m, d), BF16)],
        compiler_params=_params("parallel", "parallel"),
        name="cross_attention",
    )(x, g_cross.reshape(1, d).astype(F32), w_cq, kv, kv, w_co, g_next.reshape(1, d).astype(F32))


def _delayed(u, tail, shift):
    rolled = pltpu.roll(u, shift, axis=0)
    row = lax.broadcasted_iota(I32, tail.shape, 0)
    head = jnp.where(row < shift, pltpu.roll(tail, shift, axis=0), rolled[:SUBLANES])
    return jnp.concatenate([head, rolled[SUBLANES:]], axis=0)


def _ffn_up_body(h_ref, wa_ref, wv_ref, cwa_ref, cwv_ref, cba_ref, cbv_ref, o_ref,
                 wab_ref, wvb_ref, halo_ref, *, tiles_per_seq):
    i = pl.program_id(1)

    @pl.when(i == 0)
    def _():
        wab_ref[...] = wa_ref[...].astype(BF16)
        wvb_ref[...] = wv_ref[...].astype(BF16)

    h = h_ref[...]
    tm = h.shape[0]
    sequence_start = i % tiles_per_seq == 0

    def conv(wb_ref, cw_ref, cb_ref, slot):
        u = jnp.dot(h, wb_ref[...], preferred_element_type=F32)
        tail = jnp.where(sequence_start, 0.0, halo_ref[slot])
        halo_ref[slot] = u[tm - SUBLANES:, :]
        c = cb_ref[...] + cw_ref[CONV_WIDTH - 1:CONV_WIDTH, :] * u
        for tap in range(CONV_WIDTH - 1):
            c = c + cw_ref[tap:tap + 1, :] * _delayed(u, tail, CONV_WIDTH - 1 - tap)
        return c

    a = conv(wab_ref, cwa_ref, cba_ref, 0)
    val = conv(wvb_ref, cwv_ref, cbv_ref, 1)
    o_ref[...] = (jax.nn.gelu(a) * val).astype(o_ref.dtype)


def _ffn_up_gate(h, w_up, conv_w, conv_b, seq, *, tm=1024, tn=512):
    m, d = h.shape
    two_ff = w_up.shape[1]
    d_ff = two_ff // 2
    tm, tn = min(tm, seq), min(tn, d_ff)
    assert seq % tm == 0 and d_ff % tn == 0 and tm >= SUBLANES >= CONV_WIDTH - 1
    nf = d_ff // tn
    conv_w = conv_w.astype(F32)
    conv_b = conv_b.reshape(1, two_ff).astype(F32)
    return pl.pallas_call(
        functools.partial(_ffn_up_body, tiles_per_seq=seq // tm),
        grid=(nf, m // tm),
        in_specs=[
            pl.BlockSpec((tm, d), lambda j, i: (i, 0)),
            pl.BlockSpec((d, tn), lambda j, i: (0, j)),
            pl.BlockSpec((d, tn), lambda j, i: (0, nf + j)),
            pl.BlockSpec((CONV_WIDTH, tn), lambda j, i: (0, j)),
            pl.BlockSpec((CONV_WIDTH, tn), lambda j, i: (0, nf + j)),
            pl.BlockSpec((1, tn), lambda j, i: (0, j)),
            pl.BlockSpec((1, tn), lambda j, i: (0, nf + j)),
        ],
        out_specs=pl.BlockSpec((tm, tn), lambda j, i: (i, j)),
        out_shape=jax.ShapeDtypeStruct((m, d_ff), BF16),
        scratch_shapes=[pltpu.VMEM((d, tn), BF16), pltpu.VMEM((d, tn), BF16),
                        pltpu.VMEM((2, SUBLANES, tn), F32)],
        compiler_params=_params("parallel", "arbitrary"),
        name="ffn_up_conv_gate",
    )(h, w_up, w_up, conv_w, conv_w, conv_b, conv_b)


def _ffn_down_body(a_ref, w_ref, x_ref, g_ref, o_ref, *, final_norm):
    k = pl.program_id(1)

    @pl.when(k == 0)
    def _():
        o_ref[...] = x_ref[...]

    o_ref[...] += jnp.dot(a_ref[...], w_ref[...], preferred_element_type=F32)

    if final_norm:
        @pl.when(k == pl.num_programs(1) - 1)
        def _():
            o_ref[...] = _rms(o_ref[...], g_ref[...])


def _ffn_down(a, w, x, g_final, *, tm=1024, tk=1024):
    m, kdim = a.shape
    d = w.shape[1]
    tm, tk = min(tm, m), min(tk, kdim)
    assert m % tm == 0 and kdim % tk == 0
    final_norm = g_final is not None
    g = (g_final if final_norm else jnp.ones((d,), F32)).reshape(1, d).astype(F32)
    return pl.pallas_call(
        functools.partial(_ffn_down_body, final_norm=final_norm),
        grid=(m // tm, kdim // tk),
        in_specs=[
            pl.BlockSpec((tm, tk), lambda i, k: (i, k)),
            pl.BlockSpec((tk, d), lambda i, k: (k, 0)),
            pl.BlockSpec((tm, d), lambda i, k: (i, 0)),
            pl.BlockSpec((1, d), lambda i, k: (0, 0)),
        ],
        out_specs=pl.BlockSpec((tm, d), lambda i, k: (i, 0)),
        out_shape=jax.ShapeDtypeStruct((m, d), F32),
        compiler_params=_params("parallel", "arbitrary"),
        name="ffn_down",
    )(a, w, x, g)


def _layer(x, mem, g_mix, w_in, b_gate, w_proj_sb, w_proj_dsa, w_out, rel_bias,
           g_cross, g_mem, w_cq, w_ckv, w_co, g_ffn, w_up, conv_w, conv_b, w_down, g_final, batch, seq):
    d = x.shape[1]
    width = N_HEADS * HEAD_DIM
    idx_w = IDX_HEADS * IDX_DIM
    o_qi = 6 * width
    o_ki = o_qi + idx_w
    o_wi = o_ki + IDX_DIM
    o_g = o_wi + IDX_HEADS

    zeros = jnp.zeros((d, LANES - IDX_DIM), F32)
    w_small = jnp.concatenate([
        w_in[:, o_ki:o_wi], zeros, zeros, w_in[:, o_ki:o_wi],
        jnp.pad(w_in[:, o_wi:o_g], ((0, 0), (0, LANES - IDX_HEADS)))], axis=1).astype(BF16)
    h, small = _rmsnorm_proj(x, g_mix, w_small, name="mixer_norm_index_proj", proj_dtype=F32)
    w_in_t = w_in.T
    tq = 256
    nq = seq // tq
    q_sb, k_sb, v_sb, q_ds, k_ds, v_ds, q_ix = (g * width for g in range(o_ki // width))
    main = _matmul_ws(h, w_in_t, name="in_proj_main", first_rows=(q_sb, k_sb, q_ds, k_ds, q_ix),
                      out_dtype=BF16, tm=1024, tn=width)
    v_t = _matmul_ws_t(h, w_in_t, name="in_proj_values", first_rows=(v_sb, v_ds), tn=width, tq=tq)
    v_t = v_t.reshape(2, batch, nq, width, tq)
    gates = _matmul_ws(h, w_in_t, name="in_proj_gates",
                       first_rows=tuple(o_g + g * width for g in range(2 * d // width)),
                       out_dtype=BF16, tm=1024, tn=width)

    o_sb = _sb_attention(main, v_t, batch, seq, q_col=0, k_col=1, v_branch=0, tq=tq)
    bias = _near_bias(rel_bias, tq)
    o_ds = _dsa_attention(main, v_t, small, bias, batch, seq, qd_col=2, kd_col=3, qi_col=4,
                          v_branch=1, tq=tq)

    x = _merge_out(o_sb, o_ds, w_proj_sb.astype(BF16), w_proj_dsa.astype(BF16), w_out.astype(BF16),
                   gates, 0, b_gate, x)

    _, kv = _rmsnorm_proj(mem, g_mem, w_ckv.astype(BF16), name="mem_norm_kv_proj", proj_dtype=BF16)
    x, h_ffn = _cross_attention(x, kv, g_cross, w_cq.astype(BF16), w_co.astype(BF16), g_ffn, batch, seq)

    act = _ffn_up_gate(h_ffn, w_up, conv_w, conv_b, seq)
    return _ffn_down(act, w_down.astype(BF16), x, g_final)


def kernel(x, mem, g_mix, w_in, b_gate, w_proj_sb, w_proj_dsa, w_out, rel_bias, g_cross, g_mem,
           w_cq, w_ckv, w_co, g_ffn, w_up, conv_w, conv_b, w_down, g_final):
    batch, seq, d = x.shape
    h = x.reshape(batch * seq, d)
    mem2 = mem.reshape(batch * mem.shape[1], d)
    depth = g_mix.shape[0]
    for l in range(depth):
        h = _layer(h, mem2, g_mix[l], w_in[l], b_gate[l], w_proj_sb[l], w_proj_dsa[l], w_out[l],
                   rel_bias, g_cross[l], g_mem[l], w_cq[l], w_ckv[l], w_co[l], g_ffn[l], w_up[l],
                   conv_w[l], conv_b[l], w_down[l], g_final if l == depth - 1 else None, batch, seq)
    return h.reshape(batch, seq, d)
```

```python
import functools

import jax
import jax.numpy as jnp
from jax import lax
from jax.experimental import pallas as pl
from jax.experimental.pallas import tpu as pltpu

F32, BF16, I32 = jnp.float32, jnp.bfloat16, jnp.int32

EPS = 1e-6
HEAD_DIM = 128
N_HEADS = 8
IDX_HEADS = 16
IDX_DIM = 64
CHUNK = 64
TOPK_MAX = 256
N_BUCKETS = 32
MAX_DISTANCE = 128
MEM_HEADS = 4
CONV_WIDTH = 3

LANES = 128
SUBLANES = 8
VMEM_LIMIT_BYTES = 56 * 1024 * 1024
NEG_BIG = -1e30
EXP2_UNDERFLOW = -151.0
LOG2E = 1.4426950408889634
KEY_BITS = 32
SIGN_BIT = -(2 ** 31)

_NT = (((1,), (1,)), ((), ()))


def _params(*sem):
    return pltpu.CompilerParams(dimension_semantics=sem, vmem_limit_bytes=VMEM_LIMIT_BYTES)


def _rms(x, g):
    inv = lax.rsqrt(jnp.mean(x * x, axis=-1, keepdims=True) + EPS)
    return x * inv * g


def _mm_ws_body(a_ref, wt_ref, o_ref, wb_ref):
    @pl.when(pl.program_id(1) == 0)
    def _():
        wb_ref[...] = wt_ref[...].astype(BF16)

    o_ref[...] = lax.dot_general(a_ref[...], wb_ref[...], _NT,
                                 preferred_element_type=F32).astype(o_ref.dtype)


def _row_window(first_rows, tn, k):
    assert all(r % SUBLANES == 0 for r in first_rows)

    def index_map(j, i):
        row = jnp.int32(first_rows[0])
        for step, first in enumerate(first_rows[1:], start=1):
            row = jnp.where(j >= step, first, row)
        return pl.multiple_of(row, SUBLANES), 0

    return pl.BlockSpec((pl.Element(tn), pl.Element(k)), index_map)


def _matmul_ws(a, w_t, *, name, first_rows, out_dtype, tm, tn):
    m, k = a.shape
    tm = min(tm, m)
    assert m % tm == 0 and max(first_rows) + tn <= w_t.shape[0] and w_t.shape[1] == k
    return pl.pallas_call(
        _mm_ws_body,
        grid=(len(first_rows), m // tm),
        in_specs=[pl.BlockSpec((tm, k), lambda j, i: (i, 0)), _row_window(first_rows, tn, k)],
        out_specs=pl.BlockSpec((tm, tn), lambda j, i: (i, j)),
        out_shape=jax.ShapeDtypeStruct((m, len(first_rows) * tn), out_dtype),
        scratch_shapes=[pltpu.VMEM((tn, k), BF16)],
        compiler_params=_params("parallel", "arbitrary"),
        name=name,
    )(a, w_t)


def _mm_ws_t_body(wt_ref, a_ref, o_ref, wb_ref):
    @pl.when(pl.program_id(1) == 0)
    def _():
        wb_ref[...] = wt_ref[...].astype(BF16)

    res = lax.dot_general(wb_ref[...], a_ref[...], _NT, preferred_element_type=F32)
    n_blocks, _, tq = o_ref.shape
    for blk in range(n_blocks):
        o_ref[blk] = res[:, blk * tq:(blk + 1) * tq].astype(o_ref.dtype)


def _matmul_ws_t(a, w_t, *, name, first_rows, tn, tq, tm):
    m, k = a.shape
    tm = min(tm, m)
    assert m % tm == 0 and tm % tq == 0 and max(first_rows) + tn <= w_t.shape[0] and w_t.shape[1] == k
    per_step = tm // tq
    return pl.pallas_call(
        _mm_ws_t_body,
        grid=(len(first_rows), m // tm),
        in_specs=[_row_window(first_rows, tn, k), pl.BlockSpec((tm, k), lambda j, i: (i, 0))],
        out_specs=pl.BlockSpec((None, per_step, tn, tq), lambda j, i: (j, i, 0, 0)),
        out_shape=jax.ShapeDtypeStruct((len(first_rows), m // tq, tn, tq), BF16),
        scratch_shapes=[pltpu.VMEM((tn, k), BF16)],
        compiler_params=_params("parallel", "arbitrary"),
        name=name,
    )(w_t, a)


def _norm_proj_body(x_ref, g_ref, w_ref, h_ref, p_ref):
    h = _rms(x_ref[...], g_ref[...]).astype(BF16)
    h_ref[...] = h
    p_ref[...] = jnp.dot(h, w_ref[...], preferred_element_type=F32).astype(p_ref.dtype)


def _rmsnorm_proj(x, g, w, *, name, proj_dtype, tm=512):
    m, d = x.shape
    n = w.shape[1]
    tm = min(tm, m)
    assert m % tm == 0
    return pl.pallas_call(
        _norm_proj_body,
        grid=(m // tm,),
        in_specs=[pl.BlockSpec((tm, d), lambda i: (i, 0)), pl.BlockSpec((1, d), lambda i: (0, 0)),
                  pl.BlockSpec((d, n), lambda i: (0, 0))],
        out_specs=[pl.BlockSpec((tm, d), lambda i: (i, 0)), pl.BlockSpec((tm, n), lambda i: (i, 0))],
        out_shape=[jax.ShapeDtypeStruct((m, d), BF16), jax.ShapeDtypeStruct((m, n), proj_dtype)],
        compiler_params=_params("parallel"),
        name=name,
    )(x, g.reshape(1, d).astype(F32), w)


def _sb_body(q_ref, k_ref, vt_ref, tri_ref, o_ref, qs_ref, z_ref, lb_ref, wb_ref, acc_ref, *, tq):
    i = pl.program_id(1)
    shape = (tq, tq)
    before = lax.broadcasted_iota(I32, shape, 0) < lax.broadcasted_iota(I32, shape, 1)
    heads = [slice(h * HEAD_DIM, (h + 1) * HEAD_DIM) for h in range(N_HEADS)]
    qs_ref[...] = (q_ref[...].astype(F32) * (HEAD_DIM ** -0.5 * LOG2E)).astype(BF16)
    acc_ref[...] = jnp.zeros(acc_ref.shape, F32)

    def tile(j, carry, diagonal):
        start = pl.multiple_of(j * tq, tq)
        for h, hs in enumerate(heads):
            z = lax.dot_general(k_ref[pl.ds(start, tq), hs], qs_ref[:, hs], _NT,
                                preferred_element_type=F32)
            neg_z = -z
            log_keep = jnp.minimum(neg_z, 0.0) - jnp.log2(1.0 + jnp.exp2(jnp.minimum(z, neg_z)))
            if diagonal:
                log_keep = jnp.where(before, log_keep, 0.0)
            z_ref[h] = z
            lb_ref[h] = log_keep.astype(BF16)
        new_carry = []
        for h in range(N_HEADS):
            c = jnp.dot(tri_ref[...], lb_ref[h], preferred_element_type=F32) + carry[h:h + 1, :]
            w = jnp.exp2(z_ref[h] + c)
            if diagonal:
                w = jnp.where(before, w, 0.0)
            wb_ref[h] = w.astype(BF16)
            new_carry.append(c[0:1, :])
        for h, hs in enumerate(heads):
            acc_ref[h] += jnp.dot(vt_ref[j, hs, :], wb_ref[h], preferred_element_type=F32)
        return jnp.concatenate(new_carry, axis=0)

    carry = tile(i, jnp.zeros((N_HEADS, tq), F32), True)

    def cond(state):
        j, live, _ = state
        return jnp.logical_and(j >= 0, live > EXP2_UNDERFLOW)

    def body(state):
        j, _, carry = state
        carry = tile(j, carry, False)
        return j - 1, jnp.max(carry), carry

    lax.while_loop(cond, body, (i - 1, jnp.max(carry), carry))
    for h, hs in enumerate(heads):
        o_ref[:, hs] = acc_ref[h].T.astype(o_ref.dtype)


def _sb_attention(qk, v_t, batch, seq, *, q_col, k_col, v_branch, tq):
    nq = seq // tq
    width = N_HEADS * HEAD_DIM
    tri = (jnp.arange(tq)[None, :] >= jnp.arange(tq)[:, None]).astype(BF16)
    resident = pl.Buffered(1)
    return pl.pallas_call(
        functools.partial(_sb_body, tq=tq),
        grid=(batch, nq),
        in_specs=[
            pl.BlockSpec((tq, width), lambda b, i: (b * nq + i, q_col)),
            pl.BlockSpec((seq, width), lambda b, i: (b, k_col), pipeline_mode=resident),
            pl.BlockSpec((None, None, nq, width, tq), lambda b, i: (v_branch, b, 0, 0, 0),
                         pipeline_mode=resident),
            pl.BlockSpec((tq, tq), lambda b, i: (0, 0), pipeline_mode=resident),
        ],
        out_specs=pl.BlockSpec((tq, width), lambda b, i: (b * nq + i, 0)),
        out_shape=jax.ShapeDtypeStruct((batch * seq, width), BF16),
        scratch_shapes=[
            pltpu.VMEM((tq, width), BF16),
            pltpu.VMEM((N_HEADS, tq, tq), F32),
            pltpu.VMEM((N_HEADS, tq, tq), BF16),
            pltpu.VMEM((N_HEADS, tq, tq), BF16),
            pltpu.VMEM((N_HEADS, HEAD_DIM, tq), F32),
        ],
        compiler_params=_params("parallel", "arbitrary"),
        name="sb_attention",
    )(qk, qk, v_t, tri)


def _bucket_thresholds():
    nb = N_BUCKETS // 2
    max_exact = nb // 2
    span = nb - max_exact
    out = []
    for k in range(1, span):
        n = max_exact
        while n ** span * max_exact ** k < MAX_DISTANCE ** k * max_exact ** span:
            n += 1
        out.append(n)
    return max_exact, out


def _bias_body(rb_ref, o_ref, *, tq):
    nb = N_BUCKETS // 2
    max_exact, steps = _bucket_thresholds()
    shape = (2 * tq, tq)
    rel = lax.broadcasted_iota(I32, shape, 0) - lax.broadcasted_iota(I32, shape, 1) - tq
    n = jnp.abs(rel)
    large = jnp.full(shape, max_exact, I32)
    for t in steps:
        large = large + (n >= t).astype(I32)
    bucket = jnp.where(rel > 0, nb, 0) + jnp.where(n < max_exact, n, large)
    for h in range(N_HEADS):
        val = jnp.zeros(shape, F32)
        for b in range(N_BUCKETS):
            val = jnp.where(bucket == b, rb_ref[b, h], val)
        o_ref[h] = (val - rb_ref[nb - 1, h]) * LOG2E


def _near_bias(rel_bias, tq):
    return pl.pallas_call(
        functools.partial(_bias_body, tq=tq),
        in_specs=[pl.BlockSpec(memory_space=pltpu.SMEM)],
        out_specs=pl.BlockSpec(memory_space=pltpu.VMEM),
        out_shape=jax.ShapeDtypeStruct((N_HEADS, 2 * tq, tq), F32),
        compiler_params=pltpu.CompilerParams(vmem_limit_bytes=VMEM_LIMIT_BYTES),
        name="dsa_near_bias",
    )(rel_bias.astype(F32))


def _order_key(x):
    bits = lax.bitcast_convert_type(x, I32)
    return bits ^ ((bits >> 31) | SIGN_BIT)


def _order_key_to_float(key):
    return lax.bitcast_convert_type(key ^ ((~key >> 31) | SIGN_BIT), F32)


def _bit_transpose32(words):
    a = list(words)
    j, m = 16, 0x0000FFFF
    while j:
        mask = jnp.int32(m - (1 << 32) if m >= 1 << 31 else m)
        k = 0
        while k < 32:
            t = (lax.shift_right_logical(a[k], jnp.int32(j)) ^ a[k + j]) & mask
            a[k] = a[k] ^ lax.shift_left(t, jnp.int32(j))
            a[k + j] = a[k + j] ^ t
            k = (k + j + 1) & ~j
        j >>= 1
        m = (m ^ (m << j)) & 0xFFFFFFFF
    return a


def _dsa_body(qd_ref, qi_ref, wq_ref, kd_ref, vt_ref, ki_ref, bias_ref, tri_ref, o_ref,
              sc_ref, plane_ref, qs_ref, m_ref, l_ref, acc_ref, lg_ref, bmax_ref, p_ref, *, tq, top):
    i = pl.program_id(1)
    shape = (tq, tq)
    key_row = lax.broadcasted_iota(I32, shape, 0)
    qry_col = lax.broadcasted_iota(I32, shape, 1)
    visible = key_row // CHUNK <= qry_col // CHUNK

    w_t = (wq_ref[...] * (IDX_DIM ** -0.5 * IDX_HEADS ** -0.5)).T

    heads_per_vreg = LANES // IDX_DIM

    def score_tile(j):
        start = pl.multiple_of(j * tq, tq)
        ki = [ki_ref[pl.ds(start, tq), c * LANES:(c + 1) * LANES].astype(BF16)
              for c in range(heads_per_vreg)]
        s = jnp.zeros(shape, F32)
        for h in range(IDX_HEADS):
            g, c = divmod(h, heads_per_vreg)
            d = lax.dot_general(ki[c], qi_ref[:, g * LANES:(g + 1) * LANES], _NT,
                                preferred_element_type=F32)
            s = s + w_t[h:h + 1, :] * jnp.maximum(d, 0.0)
        sc_ref[j] = s

    def key_planes(j):
        ukey = _order_key(sc_ref[j])
        planes = _bit_transpose32([ukey[g * SUBLANES:(g + 1) * SUBLANES, :] for g in range(KEY_BITS)])
        for b in range(KEY_BITS):
            plane_ref[b, pl.ds(pl.multiple_of(j * SUBLANES, SUBLANES), SUBLANES), :] = planes[b]

    @pl.when(i == 0)
    def _():
        plane_ref[...] = jnp.zeros(plane_ref.shape, I32)

    def score_step(j, carry):
        key_planes(j)
        score_tile(j + 1)
        return carry

    score_tile(0)
    lax.fori_loop(0, i, score_step, 0)
    key_planes(i)
    sc_ref[i] = jnp.where(visible, sc_ref[i], -jnp.inf)

    n_rows = plane_ref.shape[1]
    block_of_row = lax.broadcasted_iota(I32, (n_rows, tq), 0) // SUBLANES
    qry_of_col = lax.broadcasted_iota(I32, (n_rows, tq), 1)
    n_bits = (qry_of_col // CHUNK + 1) * (CHUNK // SUBLANES)
    diag_bits = jnp.where(n_bits >= KEY_BITS, -1, lax.shift_left(jnp.int32(1), n_bits) - 1)
    cand0 = jnp.where(block_of_row < i, -1, jnp.where(block_of_row == i, diag_bits, 0))

    def popcount_rows(words):
        return jnp.sum(lax.population_count(words), axis=0, keepdims=True)

    def bit_step(t, state):
        cand, n_above, thr_bits = state
        b = KEY_BITS - 1 - t
        ones = cand & plane_ref[b]
        n_ones = popcount_rows(ones)
        take = n_above + n_ones >= top
        cand = jnp.where(take, ones, cand ^ ones)
        n_above = jnp.where(take, n_above, n_above + n_ones)
        thr_bits = thr_bits | jnp.where(take, lax.shift_left(jnp.int32(1), b), 0)
        return cand, n_above, thr_bits

    zero = jnp.zeros((1, tq), I32)
    cand, n_above, thr_bits = lax.fori_loop(0, KEY_BITS, bit_step, (cand0, zero, zero))
    qry = lax.broadcasted_iota(I32, (1, tq), 1)
    n_visible = i * tq + (qry // CHUNK + 1) * CHUNK
    wanted = n_visible > top
    thr = jnp.where(wanted, _order_key_to_float(thr_bits), jnp.finfo(F32).min)
    tied = jnp.logical_and(wanted, n_above + popcount_rows(cand) > top)
    c_hi = n_above

    ones = jnp.ones((2 * SUBLANES, tq), BF16)

    def plain_mask(j, carry):
        sc_ref[j] = jnp.where(sc_ref[j] >= thr, 0.0, NEG_BIG)
        return carry

    def tie_mask(j, seen):
        s = sc_ref[j]
        equal = s == thr
        rank = jnp.dot(tri_ref[...], equal.astype(BF16), preferred_element_type=F32) + seen
        quota = jnp.where(tied, (top - c_hi).astype(F32), jnp.inf)
        keep_equal = jnp.where(rank < quota, 0.0, NEG_BIG)
        sc_ref[j] = jnp.where(s > thr, 0.0, jnp.where(equal, keep_equal, NEG_BIG))
        return seen + jnp.sum(equal.astype(F32), axis=0, keepdims=True)

    def with_ties():
        lax.fori_loop(0, i + 1, tie_mask, jnp.zeros((1, tq), F32))
        return jnp.int32(0)

    def without_ties():
        return lax.fori_loop(0, i + 1, plain_mask, jnp.int32(0))

    lax.cond(jnp.max(tied.astype(I32)) > 0, with_ties, without_ties)

    qs_ref[...] = (qd_ref[...].astype(F32) * (HEAD_DIM ** -0.5 * LOG2E)).astype(BF16)
    m_ref[...] = jnp.full(m_ref.shape, NEG_BIG, F32)
    l_ref[...] = jnp.zeros(l_ref.shape, F32)
    acc_ref[...] = jnp.zeros(acc_ref.shape, F32)

    heads = [slice(h * HEAD_DIM, (h + 1) * HEAD_DIM) for h in range(N_HEADS)]

    far, prev, diag = None, 0, 1

    def logits(u, near, slot):
        j = i - u
        start = pl.multiple_of(j * tq, tq)
        mask = sc_ref[j]
        block_max = []
        for h, hs in enumerate(heads):
            lg = lax.dot_general(kd_ref[pl.ds(start, tq), hs], qs_ref[:, hs], _NT,
                                 preferred_element_type=F32)
            if near is not None:
                lg = lg + bias_ref[h, near * tq:(near + 1) * tq, :]
            lg = lg + mask
            lg_ref[slot, h] = lg
            block_max.append(jnp.max(lg, axis=0, keepdims=True))
        bmax_ref[slot] = jnp.concatenate(block_max, axis=0)

    def values(u, slot):
        j = i - u
        m_old = m_ref[...]
        m_new = jnp.maximum(m_old, bmax_ref[slot])
        alpha = jnp.exp2(m_old - m_new)
        m_ref[...] = m_new
        for h in range(N_HEADS):
            p_ref[h] = jnp.exp2(lg_ref[slot, h] - m_new[h:h + 1, :]).astype(BF16)
        denom = []
        for h, hs in enumerate(heads):
            v_ext = jnp.concatenate([vt_ref[j, hs, :], ones], axis=0)
            pv = jnp.dot(v_ext, p_ref[h], preferred_element_type=F32)
            acc_ref[h] = alpha[h:h + 1, :] * acc_ref[h] + pv[:HEAD_DIM]
            denom.append(pv[HEAD_DIM:HEAD_DIM + 1])
        l_ref[...] = alpha * l_ref[...] + jnp.concatenate(denom, axis=0)

    def even_step(u, near_a, near_b):
        logits(u - 1, near_a, 1)
        values(u, 0)
        logits(u - 2, near_b, 0)
        values(u - 1, 1)

    @pl.when(i == 0)
    def _():
        logits(0, diag, 0)
        values(0, 0)

    @pl.when(i == 1)
    def _():
        logits(1, prev, 1)

    @pl.when(jnp.logical_and(i >= 2, i % 2 == 1))
    def _():
        logits(i, far, 1)
        logits(i - 1, far, 0)
        values(i, 1)

    @pl.when(jnp.logical_and(i >= 2, i % 2 == 0))
    def _():
        logits(i, far, 0)

    def far_pair(k, carry):
        even_step(2 * (i // 2 - k), far, far)
        return carry

    lax.fori_loop(0, i // 2 - 1, far_pair, 0)

    @pl.when(i >= 2)
    def _():
        even_step(2, prev, diag)

    @pl.when(i == 1)
    def _():
        logits(0, diag, 0)
        values(1, 1)

    @pl.when(i >= 1)
    def _():
        values(0, 0)

    for h in range(N_HEADS):
        o = acc_ref[h] / l_ref[h:h + 1, :]
        o_ref[:, h * HEAD_DIM:(h + 1) * HEAD_DIM] = o.T.astype(o_ref.dtype)


def _dsa_attention(main, v_t, small, bias, batch, seq, *, qd_col, kd_col, qi_col, v_branch, tq):
    nq = seq // tq
    width = N_HEADS * HEAD_DIM
    assert IDX_HEADS * IDX_DIM == width and tq == KEY_BITS * SUBLANES and tq % CHUNK == 0
    key_copies = LANES // IDX_DIM
    top = min(TOPK_MAX, seq // 4)
    tri = (jnp.arange(tq)[None, :] < jnp.arange(tq)[:, None]).astype(BF16)
    resident = pl.Buffered(1)
    return pl.pallas_call(
        functools.partial(_dsa_body, tq=tq, top=top),
        grid=(batch, nq),
        in_specs=[
            pl.BlockSpec((tq, width), lambda b, i: (b * nq + i, qd_col)),
            pl.BlockSpec((tq, IDX_HEADS * IDX_DIM), lambda b, i: (b * nq + i, qi_col)),
            pl.BlockSpec((tq, LANES), lambda b, i: (b * nq + i, key_copies)),
            pl.BlockSpec((seq, width), lambda b, i: (b, kd_col), pipeline_mode=resident),
            pl.BlockSpec((None, None, nq, width, tq), lambda b, i: (v_branch, b, 0, 0, 0),
                         pipeline_mode=resident),
            pl.BlockSpec((seq, key_copies * LANES), lambda b, i: (b, 0), pipeline_mode=resident),
            pl.BlockSpec((N_HEADS, 2 * tq, tq), lambda b, i: (0, 0, 0), pipeline_mode=resident),
            pl.BlockSpec((tq, tq), lambda b, i: (0, 0), pipeline_mode=resident),
        ],
        out_specs=pl.BlockSpec((tq, width), lambda b, i: (b * nq + i, 0)),
        out_shape=jax.ShapeDtypeStruct((batch * seq, width), BF16),
        scratch_shapes=[
            pltpu.VMEM((nq, tq, tq), F32),
            pltpu.VMEM((KEY_BITS, nq * SUBLANES, tq), I32),
            pltpu.VMEM((tq, width), BF16),
            pltpu.VMEM((N_HEADS, tq), F32),
            pltpu.VMEM((N_HEADS, tq), F32),
            pltpu.VMEM((N_HEADS, HEAD_DIM, tq), F32),
            pltpu.VMEM((2, N_HEADS, tq, tq), F32),
            pltpu.VMEM((2, N_HEADS, tq), F32),
            pltpu.VMEM((N_HEADS, tq, tq), BF16),
        ],
        compiler_params=_params("parallel", "arbitrary"),
        name="dsa_attention",
    )(main, main, small, main, v_t, small, bias, tri)


def _merge_body(osb_ref, ods_ref, wsb_ref, wds_ref, gsb_ref, gds_ref, bsb_ref, bds_ref, wo_ref,
                x_ref, o_ref):
    p_sb = jnp.dot(osb_ref[...], wsb_ref[...], preferred_element_type=F32)
    p_ds = jnp.dot(ods_ref[...], wds_ref[...], preferred_element_type=F32)
    g_sb = jax.nn.sigmoid(gsb_ref[...].astype(F32) + bsb_ref[...])
    g_ds = jax.nn.sigmoid(gds_ref[...].astype(F32) + bds_ref[...])
    merged = (g_sb * p_sb + g_ds * p_ds).astype(BF16)
    o_ref[...] = x_ref[...] + jnp.dot(merged, wo_ref[...], preferred_element_type=F32)


def _merge_out(o_sb, o_ds, w_sb, w_ds, w_out, proj, gate_offset, b_gate, x, *, tm=256):
    m, k = o_sb.shape
    d = w_sb.shape[1]
    tm = min(tm, m)
    assert gate_offset % d == 0 and m % tm == 0
    g = gate_offset // d
    b_gate = b_gate.reshape(1, 2 * d).astype(F32)
    resident = pl.Buffered(1)
    return pl.pallas_call(
        _merge_body,
        grid=(m // tm,),
        in_specs=[
            pl.BlockSpec((tm, k), lambda i: (i, 0)),
            pl.BlockSpec((tm, k), lambda i: (i, 0)),
            pl.BlockSpec((k, d), lambda i: (0, 0), pipeline_mode=resident),
            pl.BlockSpec((k, d), lambda i: (0, 0), pipeline_mode=resident),
            pl.BlockSpec((tm, d), lambda i: (i, g)),
            pl.BlockSpec((tm, d), lambda i: (i, g + 1)),
            pl.BlockSpec((1, d), lambda i: (0, 0), pipeline_mode=resident),
            pl.BlockSpec((1, d), lambda i: (0, 1), pipeline_mode=resident),
            pl.BlockSpec((d, d), lambda i: (0, 0), pipeline_mode=resident),
            pl.BlockSpec((tm, d), lambda i: (i, 0)),
        ],
        out_specs=pl.BlockSpec((tm, d), lambda i: (i, 0)),
        out_shape=jax.ShapeDtypeStruct((m, d), F32),
        compiler_params=_params("parallel"),
        name="merge_out_proj",
    )(o_sb, o_ds, w_sb, w_ds, proj, proj, b_gate, b_gate, w_out, x)


def _cross_body(x_ref, g_ref, wq_ref, km_ref, vm_ref, wo_ref, gn_ref, o_ref, hn_ref):
    x = x_ref[...]
    h = _rms(x, g_ref[...]).astype(BF16)
    q = jnp.dot(h, wq_ref[...], preferred_element_type=F32) * HEAD_DIM ** -0.5
    q = q.astype(BF16)
    outs = []
    for hh in range(MEM_HEADS):
        hs = slice(hh * HEAD_DIM, (hh + 1) * HEAD_DIM)
        lg = lax.dot_general(q[:, hs], km_ref[:, hs], _NT, preferred_element_type=F32)
        p = jnp.exp(lg - jnp.max(lg, axis=1, keepdims=True))
        o = jnp.dot(p.astype(BF16), vm_ref[:, hs], preferred_element_type=F32)
        outs.append((o / jnp.sum(p, axis=1, keepdims=True)).astype(BF16))
    o = jnp.concatenate(outs, axis=1)
    y = x + jnp.dot(o, wo_ref[...], preferred_element_type=F32)
    o_ref[...] = y
    hn_ref[...] = _rms(y, gn_ref[...]).astype(hn_ref.dtype)


def _cross_attention(x, kv, g_cross, w_cq, w_co, g_next, batch, seq, *, tm=512):
    m, d = x.shape
    n_mem = kv.shape[0] // batch
    width = MEM_HEADS * HEAD_DIM
    tm = min(tm, seq)
    nt = seq // tm
    row_tile = pl.BlockSpec((tm, d), lambda b, i: (b * nt + i, 0))
    gain = pl.BlockSpec((1, d), lambda b, i: (0, 0))
    return pl.pallas_call(
        _cross_body,
        grid=(batch, nt),
        in_specs=[
            row_tile,
            gain,
            pl.BlockSpec((d, width), lambda b, i: (0, 0)),
            pl.BlockSpec((n_mem, width), lambda b, i: (b, 0)),
            pl.BlockSpec((n_mem, width), lambda b, i: (b, 1)),
            pl.BlockSpec((width, d), lambda b, i: (0, 0)),
            gain,
        ],
        out_specs=[row_tile, row_tile],
        out_shape=[jax.ShapeDtypeStruct((m, d), F32), jax.ShapeDtypeStruct((m, d), BF16)],
        compiler_params=_params("parallel", "parallel"),
        name="cross_attention",
    )(x, g_cross.reshape(1, d).astype(F32), w_cq, kv, kv, w_co, g_next.reshape(1, d).astype(F32))


def _delayed(u, tail, shift):
    rolled = pltpu.roll(u, shift, axis=0)
    row = lax.broadcasted_iota(I32, tail.shape, 0)
    head = jnp.where(row < shift, pltpu.roll(tail, shift, axis=0), rolled[:SUBLANES])
    return jnp.concatenate([head, rolled[SUBLANES:]], axis=0)


def _ffn_up_body(h_ref, wa_ref, wv_ref, cwa_ref, cwv_ref, cba_ref, cbv_ref, o_ref,
                 wab_ref, wvb_ref, halo_ref, *, tiles_per_seq):
    i = pl.program_id(1)

    @pl.when(i == 0)
    def _():
        wab_ref[...] = wa_ref[...].astype(BF16)
        wvb_ref[...] = wv_ref[...].astype(BF16)

    h = h_ref[...]
    tm = h.shape[0]
    sequence_start = i % tiles_per_seq == 0

    def conv(wb_ref, cw_ref, cb_ref, slot):
        u = jnp.dot(h, wb_ref[...], preferred_element_type=F32)
        tail = jnp.where(sequence_start, 0.0, halo_ref[slot])
        halo_ref[slot] = u[tm - SUBLANES:, :]
        c = cb_ref[...] + cw_ref[CONV_WIDTH - 1:CONV_WIDTH, :] * u
        for tap in range(CONV_WIDTH - 1):
            c = c + cw_ref[tap:tap + 1, :] * _delayed(u, tail, CONV_WIDTH - 1 - tap)
        return c

    a = conv(wab_ref, cwa_ref, cba_ref, 0)
    val = conv(wvb_ref, cwv_ref, cbv_ref, 1)
    o_ref[...] = (jax.nn.gelu(a) * val).astype(o_ref.dtype)


def _ffn_up_gate(h, w_up, conv_w, conv_b, seq, *, tm=1024, tn=512):
    m, d = h.shape
    two_ff = w_up.shape[1]
    d_ff = two_ff // 2
    tm, tn = min(tm, seq), min(tn, d_ff)
    assert seq % tm == 0 and d_ff % tn == 0 and tm >= SUBLANES >= CONV_WIDTH - 1
    nf = d_ff // tn
    conv_w = conv_w.astype(F32)
    conv_b = conv_b.reshape(1, two_ff).astype(F32)
    return pl.pallas_call(
        functools.partial(_ffn_up_body, tiles_per_seq=seq // tm),
        grid=(nf, m // tm),
        in_specs=[
            pl.BlockSpec((tm, d), lambda j, i: (i, 0)),
            pl.BlockSpec((d, tn), lambda j, i: (0, j)),
            pl.BlockSpec((d, tn), lambda j, i: (0, nf + j)),
            pl.BlockSpec((CONV_WIDTH, tn), lambda j, i: (0, j)),
            pl.BlockSpec((CONV_WIDTH, tn), lambda j, i: (0, nf + j)),
            pl.BlockSpec((1, tn), lambda j, i: (0, j)),
            pl.BlockSpec((1, tn), lambda j, i: (0, nf + j)),
        ],
        out_specs=pl.BlockSpec((tm, tn), lambda j, i: (i, j)),
        out_shape=jax.ShapeDtypeStruct((m, d_ff), BF16),
        scratch_shapes=[pltpu.VMEM((d, tn), BF16), pltpu.VMEM((d, tn), BF16),
                        pltpu.VMEM((2, SUBLANES, tn), F32)],
        compiler_params=_params("parallel", "arbitrary"),
        name="ffn_up_conv_gate",
    )(h, w_up, w_up, conv_w, conv_w, conv_b, conv_b)


def _ffn_down_body(a_ref, w_ref, x_ref, g_ref, o_ref, *, final_norm):
    k = pl.program_id(1)

    @pl.when(k == 0)
    def _():
        o_ref[...] = x_ref[...]

    o_ref[...] += jnp.dot(a_ref[...], w_ref[...], preferred_element_type=F32)

    if final_norm:
        @pl.when(k == pl.num_programs(1) - 1)
        def _():
            o_ref[...] = _rms(o_ref[...], g_ref[...])


def _ffn_down(a, w, x, g_final, *, tm=1024, tk=1024):
    m, kdim = a.shape
    d = w.shape[1]
    tm, tk = min(tm, m), min(tk, kdim)
    assert m % tm == 0 and kdim % tk == 0
    final_norm = g_final is not None
    g = (g_final if final_norm else jnp.ones((d,), F32)).reshape(1, d).astype(F32)
    return pl.pallas_call(
        functools.partial(_ffn_down_body, final_norm=final_norm),
        grid=(m // tm, kdim // tk),
        in_specs=[
            pl.BlockSpec((tm, tk), lambda i, k: (i, k)),
            pl.BlockSpec((tk, d), lambda i, k: (k, 0)),
            pl.BlockSpec((tm, d), lambda i, k: (i, 0)),
            pl.BlockSpec((1, d), lambda i, k: (0, 0)),
        ],
        out_specs=pl.BlockSpec((tm, d), lambda i, k: (i, 0)),
        out_shape=jax.ShapeDtypeStruct((m, d), F32),
        compiler_params=_params("parallel", "arbitrary"),
        name="ffn_down",
    )(a, w, x, g)


def _layer(x, mem, g_mix, w_in, b_gate, w_proj_sb, w_proj_dsa, w_out, rel_bias,
           g_cross, g_mem, w_cq, w_ckv, w_co, g_ffn, w_up, conv_w, conv_b, w_down, g_final, batch, seq):
    d = x.shape[1]
    width = N_HEADS * HEAD_DIM
    idx_w = IDX_HEADS * IDX_DIM
    o_qi = 6 * width
    o_ki = o_qi + idx_w
    o_wi = o_ki + IDX_DIM
    o_g = o_wi + IDX_HEADS

    zeros = jnp.zeros((d, LANES - IDX_DIM), F32)
    w_small = jnp.concatenate([
        w_in[:, o_ki:o_wi], zeros, zeros, w_in[:, o_ki:o_wi],
        jnp.pad(w_in[:, o_wi:o_g], ((0, 0), (0, LANES - IDX_HEADS)))], axis=1).astype(BF16)
    h, small = _rmsnorm_proj(x, g_mix, w_small, name="mixer_norm_index_proj", proj_dtype=F32)
    w_in_t = w_in.T
    tq = 256
    nq = seq // tq
    q_sb, k_sb, v_sb, q_ds, k_ds, v_ds, q_ix = (g * width for g in range(o_ki // width))
    gate_rows = tuple(o_g + g * width for g in range(2 * d // width))
    n_gate = len(gate_rows)
    main = _matmul_ws(h, w_in_t, name="in_proj_main",
                      first_rows=gate_rows + (q_sb, k_sb, q_ds, k_ds, q_ix),
                      out_dtype=BF16, tm=1024, tn=width)
    v_t = _matmul_ws_t(h, w_in_t, name="in_proj_values", first_rows=(v_sb, v_ds), tn=width, tq=tq,
                       tm=1024)
    v_t = v_t.reshape(2, batch, nq, width, tq)

    o_sb = _sb_attention(main, v_t, batch, seq, q_col=n_gate, k_col=n_gate + 1, v_branch=0, tq=tq)
    bias = _near_bias(rel_bias, tq)
    o_ds = _dsa_attention(main, v_t, small, bias, batch, seq, qd_col=n_gate + 2, kd_col=n_gate + 3,
                          qi_col=n_gate + 4, v_branch=1, tq=tq)

    x = _merge_out(o_sb, o_ds, w_proj_sb.astype(BF16), w_proj_dsa.astype(BF16), w_out.astype(BF16),
                   main, 0, b_gate, x)

    _, kv = _rmsnorm_proj(mem, g_mem, w_ckv.astype(BF16), name="mem_norm_kv_proj", proj_dtype=BF16)
    x, h_ffn = _cross_attention(x, kv, g_cross, w_cq.astype(BF16), w_co.astype(BF16), g_ffn, batch, seq)

    act = _ffn_up_gate(h_ffn, w_up, conv_w, conv_b, seq)
    return _ffn_down(act, w_down.astype(BF16), x, g_final)


def kernel(x, mem, g_mix, w_in, b_gate, w_proj_sb, w_proj_dsa, w_out, rel_bias, g_cross, g_mem,
           w_cq, w_ckv, w_co, g_ffn, w_up, conv_w, conv_b, w_down, g_final):
    batch, seq, d = x.shape
    h = x.reshape(batch * seq, d)
    mem2 = mem.reshape(batch * mem.shape[1], d)
    depth = g_mix.shape[0]
    for l in range(depth):
        h = _layer(h, mem2, g_mix[l], w_in[l], b_gate[l], w_proj_sb[l], w_proj_dsa[l], w_out[l],
                   rel_bias, g_cross[l], g_mem[l], w_cq[l], w_ckv[l], w_co[l], g_ffn[l], w_up[l],
                   conv_w[l], conv_b[l], w_down[l], g_final if l == depth - 1 else None, batch, seq)
    return h.reshape(batch, seq, d)
```

```python
import functools

import jax
import jax.numpy as jnp
from jax import lax
from jax.experimental import pallas as pl
from jax.experimental.pallas import tpu as pltpu

F32, BF16, I32 = jnp.float32, jnp.bfloat16, jnp.int32

EPS = 1e-6
HEAD_DIM = 128
N_HEADS = 8
IDX_HEADS = 16
IDX_DIM = 64
CHUNK = 64
TOPK_MAX = 256
N_BUCKETS = 32
MAX_DISTANCE = 128
MEM_HEADS = 4
CONV_WIDTH = 3

LANES = 128
SUBLANES = 8
VMEM_LIMIT_BYTES = 56 * 1024 * 1024
NEG_BIG = -1e30
EXP2_UNDERFLOW = -151.0
LOG2E = 1.4426950408889634
KEY_BITS = 32
SIGN_BIT = -(2 ** 31)

ROW_TILE = 1024
NORM_ROWS = 512
MERGE_ROWS = 256
FFN_COL_TILE = 512
FFN_K_TILE = 1024
ATT_BLOCK = KEY_BITS * SUBLANES

_NT = (((1,), (1,)), ((), ()))


def _params(*sem):
    return pltpu.CompilerParams(dimension_semantics=sem, vmem_limit_bytes=VMEM_LIMIT_BYTES)


def _rms(x, g):
    inv = lax.rsqrt(jnp.mean(x * x, axis=-1, keepdims=True) + EPS)
    return x * inv * g


def _mm_ws_body(a_ref, wt_ref, o_ref, wb_ref):
    @pl.when(pl.program_id(1) == 0)
    def _():
        wb_ref[...] = wt_ref[...].astype(BF16)

    o_ref[...] = lax.dot_general(a_ref[...], wb_ref[...], _NT,
                                 preferred_element_type=F32).astype(o_ref.dtype)


def _row_window(first_rows, tn, k):
    assert all(r % SUBLANES == 0 for r in first_rows)

    def index_map(j, i):
        row = jnp.int32(first_rows[0])
        for step, first in enumerate(first_rows[1:], start=1):
            row = jnp.where(j >= step, first, row)
        return pl.multiple_of(row, SUBLANES), 0

    return pl.BlockSpec((pl.Element(tn), pl.Element(k)), index_map)


def _matmul_ws(a, w_t, *, name, first_rows, out_dtype, tm, tn):
    m, k = a.shape
    tm = min(tm, m)
    assert m % tm == 0 and max(first_rows) + tn <= w_t.shape[0] and w_t.shape[1] == k
    return pl.pallas_call(
        _mm_ws_body,
        grid=(len(first_rows), m // tm),
        in_specs=[pl.BlockSpec((tm, k), lambda j, i: (i, 0)), _row_window(first_rows, tn, k)],
        out_specs=pl.BlockSpec((tm, tn), lambda j, i: (i, j)),
        out_shape=jax.ShapeDtypeStruct((m, len(first_rows) * tn), out_dtype),
        scratch_shapes=[pltpu.VMEM((tn, k), BF16)],
        compiler_params=_params("parallel", "arbitrary"),
        name=name,
    )(a, w_t)


def _mm_ws_t_body(wt_ref, a_ref, o_ref, wb_ref):
    @pl.when(pl.program_id(1) == 0)
    def _():
        wb_ref[...] = wt_ref[...].astype(BF16)

    res = lax.dot_general(wb_ref[...], a_ref[...], _NT, preferred_element_type=F32)
    n_blocks, _, tq = o_ref.shape
    for blk in range(n_blocks):
        o_ref[blk] = res[:, blk * tq:(blk + 1) * tq].astype(o_ref.dtype)


def _matmul_ws_t(a, w_t, *, name, first_rows, tn, tq, tm):
    m, k = a.shape
    tm = min(tm, m)
    assert m % tm == 0 and tm % tq == 0 and max(first_rows) + tn <= w_t.shape[0] and w_t.shape[1] == k
    per_step = tm // tq
    return pl.pallas_call(
        _mm_ws_t_body,
        grid=(len(first_rows), m // tm),
        in_specs=[_row_window(first_rows, tn, k), pl.BlockSpec((tm, k), lambda j, i: (i, 0))],
        out_specs=pl.BlockSpec((None, per_step, tn, tq), lambda j, i: (j, i, 0, 0)),
        out_shape=jax.ShapeDtypeStruct((len(first_rows), m // tq, tn, tq), BF16),
        scratch_shapes=[pltpu.VMEM((tn, k), BF16)],
        compiler_params=_params("parallel", "arbitrary"),
        name=name,
    )(w_t, a)


def _norm_proj_body(x_ref, g_ref, w_ref, h_ref, p_ref):
    h = _rms(x_ref[...], g_ref[...]).astype(BF16)
    h_ref[...] = h
    p_ref[...] = jnp.dot(h, w_ref[...], preferred_element_type=F32).astype(p_ref.dtype)


def _rmsnorm_proj(x, g, w, *, name, proj_dtype, tm=NORM_ROWS):
    m, d = x.shape
    n = w.shape[1]
    tm = min(tm, m)
    assert m % tm == 0
    return pl.pallas_call(
        _norm_proj_body,
        grid=(m // tm,),
        in_specs=[pl.BlockSpec((tm, d), lambda i: (i, 0)), pl.BlockSpec((1, d), lambda i: (0, 0)),
                  pl.BlockSpec((d, n), lambda i: (0, 0))],
        out_specs=[pl.BlockSpec((tm, d), lambda i: (i, 0)), pl.BlockSpec((tm, n), lambda i: (i, 0))],
        out_shape=[jax.ShapeDtypeStruct((m, d), BF16), jax.ShapeDtypeStruct((m, n), proj_dtype)],
        compiler_params=_params("parallel"),
        name=name,
    )(x, g.reshape(1, d).astype(F32), w)


def _sb_body(q_ref, k_ref, vt_ref, tri_ref, o_ref, qs_ref, z_ref, lb_ref, wb_ref, acc_ref, *, tq):
    i = pl.program_id(1)
    shape = (tq, tq)
    before = lax.broadcasted_iota(I32, shape, 0) < lax.broadcasted_iota(I32, shape, 1)
    heads = [slice(h * HEAD_DIM, (h + 1) * HEAD_DIM) for h in range(N_HEADS)]
    qs_ref[...] = (q_ref[...].astype(F32) * (HEAD_DIM ** -0.5 * LOG2E)).astype(BF16)
    acc_ref[...] = jnp.zeros(acc_ref.shape, F32)

    def tile(j, carry, diagonal):
        start = pl.multiple_of(j * tq, tq)
        for h, hs in enumerate(heads):
            z = lax.dot_general(k_ref[pl.ds(start, tq), hs], qs_ref[:, hs], _NT,
                                preferred_element_type=F32)
            neg_z = -z
            log_keep = jnp.minimum(neg_z, 0.0) - jnp.log2(1.0 + jnp.exp2(jnp.minimum(z, neg_z)))
            if diagonal:
                log_keep = jnp.where(before, log_keep, 0.0)
            z_ref[h] = z
            lb_ref[h] = log_keep.astype(BF16)
        new_carry = []
        for h in range(N_HEADS):
            c = jnp.dot(tri_ref[...], lb_ref[h], preferred_element_type=F32) + carry[h:h + 1, :]
            w = jnp.exp2(z_ref[h] + c)
            if diagonal:
                w = jnp.where(before, w, 0.0)
            wb_ref[h] = w.astype(BF16)
            new_carry.append(c[0:1, :])
        for h, hs in enumerate(heads):
            acc_ref[h] += jnp.dot(vt_ref[j, hs, :], wb_ref[h], preferred_element_type=F32)
        return jnp.concatenate(new_carry, axis=0)

    carry = tile(i, jnp.zeros((N_HEADS, tq), F32), True)

    def cond(state):
        j, live, _ = state
        return jnp.logical_and(j >= 0, live > EXP2_UNDERFLOW)

    def body(state):
        j, _, carry = state
        carry = tile(j, carry, False)
        return j - 1, jnp.max(carry), carry

    lax.while_loop(cond, body, (i - 1, jnp.max(carry), carry))
    for h, hs in enumerate(heads):
        o_ref[:, hs] = acc_ref[h].T.astype(o_ref.dtype)


def _sb_attention(qk, v_t, batch, seq, *, q_col, k_col, v_branch, tq):
    nq = seq // tq
    width = N_HEADS * HEAD_DIM
    tri = (jnp.arange(tq)[None, :] >= jnp.arange(tq)[:, None]).astype(BF16)
    resident = pl.Buffered(1)
    return pl.pallas_call(
        functools.partial(_sb_body, tq=tq),
        grid=(batch, nq),
        in_specs=[
            pl.BlockSpec((tq, width), lambda b, i: (b * nq + i, q_col)),
            pl.BlockSpec((seq, width), lambda b, i: (b, k_col), pipeline_mode=resident),
            pl.BlockSpec((None, None, nq, width, tq), lambda b, i: (v_branch, b, 0, 0, 0),
                         pipeline_mode=resident),
            pl.BlockSpec((tq, tq), lambda b, i: (0, 0), pipeline_mode=resident),
        ],
        out_specs=pl.BlockSpec((tq, width), lambda b, i: (b * nq + i, 0)),
        out_shape=jax.ShapeDtypeStruct((batch * seq, width), BF16),
        scratch_shapes=[
            pltpu.VMEM((tq, width), BF16),
            pltpu.VMEM((N_HEADS, tq, tq), F32),
            pltpu.VMEM((N_HEADS, tq, tq), BF16),
            pltpu.VMEM((N_HEADS, tq, tq), BF16),
            pltpu.VMEM((N_HEADS, HEAD_DIM, tq), F32),
        ],
        compiler_params=_params("parallel", "arbitrary"),
        name="sb_attention",
    )(qk, qk, v_t, tri)


def _bucket_thresholds():
    nb = N_BUCKETS // 2
    max_exact = nb // 2
    span = nb - max_exact
    out = []
    for k in range(1, span):
        n = max_exact
        while n ** span * max_exact ** k < MAX_DISTANCE ** k * max_exact ** span:
            n += 1
        out.append(n)
    return max_exact, out


def _bias_body(rb_ref, o_ref, *, tq):
    nb = N_BUCKETS // 2
    max_exact, steps = _bucket_thresholds()
    shape = (2 * tq, tq)
    rel = lax.broadcasted_iota(I32, shape, 0) - lax.broadcasted_iota(I32, shape, 1) - tq
    n = jnp.abs(rel)
    large = jnp.full(shape, max_exact, I32)
    for t in steps:
        large = large + (n >= t).astype(I32)
    bucket = jnp.where(rel > 0, nb, 0) + jnp.where(n < max_exact, n, large)
    for h in range(N_HEADS):
        val = jnp.zeros(shape, F32)
        for b in range(N_BUCKETS):
            val = jnp.where(bucket == b, rb_ref[b, h], val)
        o_ref[h] = (val - rb_ref[nb - 1, h]) * LOG2E


def _near_bias(rel_bias, tq):
    return pl.pallas_call(
        functools.partial(_bias_body, tq=tq),
        in_specs=[pl.BlockSpec(memory_space=pltpu.SMEM)],
        out_specs=pl.BlockSpec(memory_space=pltpu.VMEM),
        out_shape=jax.ShapeDtypeStruct((N_HEADS, 2 * tq, tq), F32),
        compiler_params=pltpu.CompilerParams(vmem_limit_bytes=VMEM_LIMIT_BYTES),
        name="dsa_near_bias",
    )(rel_bias.astype(F32))


def _order_key(x):
    bits = lax.bitcast_convert_type(x, I32)
    return bits ^ ((bits >> 31) | SIGN_BIT)


def _order_key_to_float(key):
    return lax.bitcast_convert_type(key ^ ((~key >> 31) | SIGN_BIT), F32)


def _bit_transpose32(words):
    a = list(words)
    j, m = 16, 0x0000FFFF
    while j:
        mask = jnp.int32(m - (1 << 32) if m >= 1 << 31 else m)
        k = 0
        while k < 32:
            t = (lax.shift_right_logical(a[k], jnp.int32(j)) ^ a[k + j]) & mask
            a[k] = a[k] ^ lax.shift_left(t, jnp.int32(j))
            a[k + j] = a[k + j] ^ t
            k = (k + j + 1) & ~j
        j >>= 1
        m = (m ^ (m << j)) & 0xFFFFFFFF
    return a


def _dsa_body(qd_ref, qi_ref, wq_ref, kd_ref, vt_ref, ki_ref, bias_ref, tri_ref, o_ref,
              sc_ref, plane_ref, qs_ref, m_ref, l_ref, acc_ref, lg_ref, bmax_ref, p_ref, *, tq, top):
    i = pl.program_id(1)
    shape = (tq, tq)
    key_row = lax.broadcasted_iota(I32, shape, 0)
    qry_col = lax.broadcasted_iota(I32, shape, 1)
    visible = key_row // CHUNK <= qry_col // CHUNK

    w_t = (wq_ref[...] * (IDX_DIM ** -0.5 * IDX_HEADS ** -0.5)).T

    heads_per_vreg = LANES // IDX_DIM

    def score_tile(j):
        start = pl.multiple_of(j * tq, tq)
        ki = [ki_ref[pl.ds(start, tq), c * LANES:(c + 1) * LANES].astype(BF16)
              for c in range(heads_per_vreg)]
        s = jnp.zeros(shape, F32)
        for h in range(IDX_HEADS):
            g, c = divmod(h, heads_per_vreg)
            d = lax.dot_general(ki[c], qi_ref[:, g * LANES:(g + 1) * LANES], _NT,
                                preferred_element_type=F32)
            s = s + w_t[h:h + 1, :] * jnp.maximum(d, 0.0)
        sc_ref[j] = s

    def key_planes(j):
        ukey = _order_key(sc_ref[j])
        planes = _bit_transpose32([ukey[g * SUBLANES:(g + 1) * SUBLANES, :] for g in range(KEY_BITS)])
        for b in range(KEY_BITS):
            plane_ref[b, pl.ds(pl.multiple_of(j * SUBLANES, SUBLANES), SUBLANES), :] = planes[b]

    @pl.when(i == 0)
    def _():
        plane_ref[...] = jnp.zeros(plane_ref.shape, I32)

    def score_step(j, carry):
        key_planes(j)
        score_tile(j + 1)
        return carry

    score_tile(0)
    lax.fori_loop(0, i, score_step, 0)
    key_planes(i)
    sc_ref[i] = jnp.where(visible, sc_ref[i], -jnp.inf)

    n_rows = plane_ref.shape[1]
    block_of_row = lax.broadcasted_iota(I32, (n_rows, tq), 0) // SUBLANES
    qry_of_col = lax.broadcasted_iota(I32, (n_rows, tq), 1)
    n_bits = (qry_of_col // CHUNK + 1) * (CHUNK // SUBLANES)
    diag_bits = jnp.where(n_bits >= KEY_BITS, -1, lax.shift_left(jnp.int32(1), n_bits) - 1)
    cand0 = jnp.where(block_of_row < i, -1, jnp.where(block_of_row == i, diag_bits, 0))

    def popcount_rows(words):
        return jnp.sum(lax.population_count(words), axis=0, keepdims=True)

    def bit_step(t, state):
        cand, n_above, thr_bits = state
        b = KEY_BITS - 1 - t
        ones = cand & plane_ref[b]
        n_ones = popcount_rows(ones)
        take = n_above + n_ones >= top
        cand = jnp.where(take, ones, cand ^ ones)
        n_above = jnp.where(take, n_above, n_above + n_ones)
        thr_bits = thr_bits | jnp.where(take, lax.shift_left(jnp.int32(1), b), 0)
        return cand, n_above, thr_bits

    zero = jnp.zeros((1, tq), I32)
    cand, n_above, thr_bits = lax.fori_loop(0, KEY_BITS, bit_step, (cand0, zero, zero))
    qry = lax.broadcasted_iota(I32, (1, tq), 1)
    n_visible = i * tq + (qry // CHUNK + 1) * CHUNK
    wanted = n_visible > top
    thr = jnp.where(wanted, _order_key_to_float(thr_bits), jnp.finfo(F32).min)
    tied = jnp.logical_and(wanted, n_above + popcount_rows(cand) > top)
    c_hi = n_above

    ones = jnp.ones((2 * SUBLANES, tq), BF16)

    def plain_mask(j, carry):
        sc_ref[j] = jnp.where(sc_ref[j] >= thr, 0.0, NEG_BIG)
        return carry

    def tie_mask(j, seen):
        s = sc_ref[j]
        equal = s == thr
        rank = jnp.dot(tri_ref[...], equal.astype(BF16), preferred_element_type=F32) + seen
        quota = jnp.where(tied, (top - c_hi).astype(F32), jnp.inf)
        keep_equal = jnp.where(rank < quota, 0.0, NEG_BIG)
        sc_ref[j] = jnp.where(s > thr, 0.0, jnp.where(equal, keep_equal, NEG_BIG))
        return seen + jnp.sum(equal.astype(F32), axis=0, keepdims=True)

    def with_ties():
        lax.fori_loop(0, i + 1, tie_mask, jnp.zeros((1, tq), F32))
        return jnp.int32(0)

    def without_ties():
        return lax.fori_loop(0, i + 1, plain_mask, jnp.int32(0))

    lax.cond(jnp.max(tied.astype(I32)) > 0, with_ties, without_ties)

    qs_ref[...] = (qd_ref[...].astype(F32) * (HEAD_DIM ** -0.5 * LOG2E)).astype(BF16)
    m_ref[...] = jnp.full(m_ref.shape, NEG_BIG, F32)
    l_ref[...] = jnp.zeros(l_ref.shape, F32)
    acc_ref[...] = jnp.zeros(acc_ref.shape, F32)

    heads = [slice(h * HEAD_DIM, (h + 1) * HEAD_DIM) for h in range(N_HEADS)]

    far, prev, diag = None, 0, 1

    def logits(u, near, slot):
        j = i - u
        start = pl.multiple_of(j * tq, tq)
        mask = sc_ref[j]
        block_max = []
        for h, hs in enumerate(heads):
            lg = lax.dot_general(kd_ref[pl.ds(start, tq), hs], qs_ref[:, hs], _NT,
                                 preferred_element_type=F32)
            if near is not None:
                lg = lg + bias_ref[h, near * tq:(near + 1) * tq, :]
            lg = lg + mask
            lg_ref[slot, h] = lg
            block_max.append(jnp.max(lg, axis=0, keepdims=True))
        bmax_ref[slot] = jnp.concatenate(block_max, axis=0)

    def values(u, slot):
        j = i - u
        m_old = m_ref[...]
        m_new = jnp.maximum(m_old, bmax_ref[slot])
        alpha = jnp.exp2(m_old - m_new)
        m_ref[...] = m_new
        for h in range(N_HEADS):
            p_ref[h] = jnp.exp2(lg_ref[slot, h] - m_new[h:h + 1, :]).astype(BF16)
        denom = []
        for h, hs in enumerate(heads):
            v_ext = jnp.concatenate([vt_ref[j, hs, :], ones], axis=0)
            pv = jnp.dot(v_ext, p_ref[h], preferred_element_type=F32)
            acc_ref[h] = alpha[h:h + 1, :] * acc_ref[h] + pv[:HEAD_DIM]
            denom.append(pv[HEAD_DIM:HEAD_DIM + 1])
        l_ref[...] = alpha * l_ref[...] + jnp.concatenate(denom, axis=0)

    def even_step(u, near_a, near_b):
        logits(u - 1, near_a, 1)
        values(u, 0)
        logits(u - 2, near_b, 0)
        values(u - 1, 1)

    @pl.when(i == 0)
    def _():
        logits(0, diag, 0)
        values(0, 0)

    @pl.when(i == 1)
    def _():
        logits(1, prev, 1)

    @pl.when(jnp.logical_and(i >= 2, i % 2 == 1))
    def _():
        logits(i, far, 1)
        logits(i - 1, far, 0)
        values(i, 1)

    @pl.when(jnp.logical_and(i >= 2, i % 2 == 0))
    def _():
        logits(i, far, 0)

    def far_pair(k, carry):
        even_step(2 * (i // 2 - k), far, far)
        return carry

    lax.fori_loop(0, i // 2 - 1, far_pair, 0)

    @pl.when(i >= 2)
    def _():
        even_step(2, prev, diag)

    @pl.when(i == 1)
    def _():
        logits(0, diag, 0)
        values(1, 1)

    @pl.when(i >= 1)
    def _():
        values(0, 0)

    for h in range(N_HEADS):
        o = acc_ref[h] / l_ref[h:h + 1, :]
        o_ref[:, h * HEAD_DIM:(h + 1) * HEAD_DIM] = o.T.astype(o_ref.dtype)


def _dsa_attention(main, v_t, small, bias, batch, seq, *, qd_col, kd_col, qi_col, v_branch, tq):
    nq = seq // tq
    width = N_HEADS * HEAD_DIM
    assert IDX_HEADS * IDX_DIM == width and tq == KEY_BITS * SUBLANES and tq % CHUNK == 0
    key_copies = LANES // IDX_DIM
    top = min(TOPK_MAX, seq // 4)
    tri = (jnp.arange(tq)[None, :] < jnp.arange(tq)[:, None]).astype(BF16)
    resident = pl.Buffered(1)
    return pl.pallas_call(
        functools.partial(_dsa_body, tq=tq, top=top),
        grid=(batch, nq),
        in_specs=[
            pl.BlockSpec((tq, width), lambda b, i: (b * nq + i, qd_col)),
            pl.BlockSpec((tq, IDX_HEADS * IDX_DIM), lambda b, i: (b * nq + i, qi_col)),
            pl.BlockSpec((tq, LANES), lambda b, i: (b * nq + i, key_copies)),
            pl.BlockSpec((seq, width), lambda b, i: (b, kd_col), pipeline_mode=resident),
            pl.BlockSpec((None, None, nq, width, tq), lambda b, i: (v_branch, b, 0, 0, 0),
                         pipeline_mode=resident),
            pl.BlockSpec((seq, key_copies * LANES), lambda b, i: (b, 0), pipeline_mode=resident),
            pl.BlockSpec((N_HEADS, 2 * tq, tq), lambda b, i: (0, 0, 0), pipeline_mode=resident),
            pl.BlockSpec((tq, tq), lambda b, i: (0, 0), pipeline_mode=resident),
        ],
        out_specs=pl.BlockSpec((tq, width), lambda b, i: (b * nq + i, 0)),
        out_shape=jax.ShapeDtypeStruct((batch * seq, width), BF16),
        scratch_shapes=[
            pltpu.VMEM((nq, tq, tq), F32),
            pltpu.VMEM((KEY_BITS, nq * SUBLANES, tq), I32),
            pltpu.VMEM((tq, width), BF16),
            pltpu.VMEM((N_HEADS, tq), F32),
            pltpu.VMEM((N_HEADS, tq), F32),
            pltpu.VMEM((N_HEADS, HEAD_DIM, tq), F32),
            pltpu.VMEM((2, N_HEADS, tq, tq), F32),
            pltpu.VMEM((2, N_HEADS, tq), F32),
            pltpu.VMEM((N_HEADS, tq, tq), BF16),
        ],
        compiler_params=_params("parallel", "arbitrary"),
        name="dsa_attention",
    )(main, main, small, main, v_t, small, bias, tri)


def _merge_cross_body(osb_ref, ods_ref, wsb_ref, wds_ref, gsb_ref, gds_ref, bsb_ref, bds_ref, wo_ref,
                      x_ref, gc_ref, wq_ref, km_ref, vm_ref, wco_ref, gn_ref, o_ref, hn_ref):
    p_sb = jnp.dot(osb_ref[...], wsb_ref[...], preferred_element_type=F32)
    p_ds = jnp.dot(ods_ref[...], wds_ref[...], preferred_element_type=F32)
    g_sb = jax.nn.sigmoid(gsb_ref[...].astype(F32) + bsb_ref[...])
    g_ds = jax.nn.sigmoid(gds_ref[...].astype(F32) + bds_ref[...])
    merged = (g_sb * p_sb + g_ds * p_ds).astype(BF16)
    x1 = x_ref[...] + jnp.dot(merged, wo_ref[...], preferred_element_type=F32)

    h = _rms(x1, gc_ref[...]).astype(BF16)
    q = jnp.dot(h, wq_ref[...], preferred_element_type=F32) * HEAD_DIM ** -0.5
    q = q.astype(BF16)
    outs = []
    for hh in range(MEM_HEADS):
        hs = slice(hh * HEAD_DIM, (hh + 1) * HEAD_DIM)
        lg = lax.dot_general(q[:, hs], km_ref[:, hs], _NT, preferred_element_type=F32)
        p = jnp.exp(lg - jnp.max(lg, axis=1, keepdims=True))
        o = jnp.dot(p.astype(BF16), vm_ref[:, hs], preferred_element_type=F32)
        outs.append((o / jnp.sum(p, axis=1, keepdims=True)).astype(BF16))
    o = jnp.concatenate(outs, axis=1)
    x2 = x1 + jnp.dot(o, wco_ref[...], preferred_element_type=F32)
    o_ref[...] = x2
    hn_ref[...] = _rms(x2, gn_ref[...]).astype(hn_ref.dtype)


def _merge_cross(o_sb, o_ds, w_sb, w_ds, w_out, proj, gate_offset, b_gate, x,
                 kv, g_cross, w_cq, w_co, g_next, batch, seq, *, tm):
    m, k = o_sb.shape
    d = w_sb.shape[1]
    n_mem = kv.shape[0] // batch
    width = MEM_HEADS * HEAD_DIM
    tm = min(tm, seq)
    assert gate_offset % d == 0 and seq % tm == 0
    g = gate_offset // d
    nt = seq // tm
    b_gate = b_gate.reshape(1, 2 * d).astype(F32)
    resident = pl.Buffered(1)

    def rows(width_, col=0):
        return pl.BlockSpec((tm, width_), lambda b, i: (b * nt + i, col))

    def whole(shape, *block):
        return pl.BlockSpec(shape, lambda b, i: block or (0,) * len(shape), pipeline_mode=resident)

    return pl.pallas_call(
        _merge_cross_body,
        grid=(batch, nt),
        in_specs=[
            rows(k), rows(k), whole((k, d)), whole((k, d)),
            rows(d, g), rows(d, g + 1), whole((1, d)), whole((1, d), 0, 1),
            whole((d, d)), rows(d),
            whole((1, d)), whole((d, width)),
            pl.BlockSpec((n_mem, width), lambda b, i: (b, 0)),
            pl.BlockSpec((n_mem, width), lambda b, i: (b, 1)),
            whole((width, d)), whole((1, d)),
        ],
        out_specs=[rows(d), rows(d)],
        out_shape=[jax.ShapeDtypeStruct((m, d), F32), jax.ShapeDtypeStruct((m, d), BF16)],
        compiler_params=_params("parallel", "parallel"),
        name="merge_out_cross",
    )(o_sb, o_ds, w_sb, w_ds, proj, proj, b_gate, b_gate, w_out, x,
      g_cross.reshape(1, d).astype(F32), w_cq, kv, kv, w_co, g_next.reshape(1, d).astype(F32))


def _delayed(u, tail, shift):
    rolled = pltpu.roll(u, shift, axis=0)
    row = lax.broadcasted_iota(I32, tail.shape, 0)
    head = jnp.where(row < shift, pltpu.roll(tail, shift, axis=0), rolled[:SUBLANES])
    return jnp.concatenate([head, rolled[SUBLANES:]], axis=0)


def _ffn_up_body(h_ref, wa_ref, wv_ref, cwa_ref, cwv_ref, cba_ref, cbv_ref, o_ref,
                 wab_ref, wvb_ref, halo_ref, *, tiles_per_seq):
    i = pl.program_id(1)

    @pl.when(i == 0)
    def _():
        wab_ref[...] = wa_ref[...].astype(BF16)
        wvb_ref[...] = wv_ref[...].astype(BF16)

    h = h_ref[...]
    tm = h.shape[0]
    sequence_start = i % tiles_per_seq == 0

    def conv(wb_ref, cw_ref, cb_ref, slot):
        u = jnp.dot(h, wb_ref[...], preferred_element_type=F32)
        tail = jnp.where(sequence_start, 0.0, halo_ref[slot])
        halo_ref[slot] = u[tm - SUBLANES:, :]
        c = cb_ref[...] + cw_ref[CONV_WIDTH - 1:CONV_WIDTH, :] * u
        for tap in range(CONV_WIDTH - 1):
            c = c + cw_ref[tap:tap + 1, :] * _delayed(u, tail, CONV_WIDTH - 1 - tap)
        return c

    a = conv(wab_ref, cwa_ref, cba_ref, 0)
    val = conv(wvb_ref, cwv_ref, cbv_ref, 1)
    o_ref[...] = (jax.nn.gelu(a) * val).astype(o_ref.dtype)


def _ffn_up_gate(h, w_up, conv_w, conv_b, seq, *, tm=ROW_TILE, tn=FFN_COL_TILE):
    m, d = h.shape
    two_ff = w_up.shape[1]
    d_ff = two_ff // 2
    tm, tn = min(tm, seq), min(tn, d_ff)
    assert seq % tm == 0 and d_ff % tn == 0 and tm >= SUBLANES >= CONV_WIDTH - 1
    nf = d_ff // tn
    conv_w = conv_w.astype(F32)
    conv_b = conv_b.reshape(1, two_ff).astype(F32)
    return pl.pallas_call(
        functools.partial(_ffn_up_body, tiles_per_seq=seq // tm),
        grid=(nf, m // tm),
        in_specs=[
            pl.BlockSpec((tm, d), lambda j, i: (i, 0)),
            pl.BlockSpec((d, tn), lambda j, i: (0, j)),
            pl.BlockSpec((d, tn), lambda j, i: (0, nf + j)),
            pl.BlockSpec((CONV_WIDTH, tn), lambda j, i: (0, j)),
            pl.BlockSpec((CONV_WIDTH, tn), lambda j, i: (0, nf + j)),
            pl.BlockSpec((1, tn), lambda j, i: (0, j)),
            pl.BlockSpec((1, tn), lambda j, i: (0, nf + j)),
        ],
        out_specs=pl.BlockSpec((tm, tn), lambda j, i: (i, j)),
        out_shape=jax.ShapeDtypeStruct((m, d_ff), BF16),
        scratch_shapes=[pltpu.VMEM((d, tn), BF16), pltpu.VMEM((d, tn), BF16),
                        pltpu.VMEM((2, SUBLANES, tn), F32)],
        compiler_params=_params("parallel", "arbitrary"),
        name="ffn_up_conv_gate",
    )(h, w_up, w_up, conv_w, conv_w, conv_b, conv_b)


def _ffn_down_body(a_ref, w_ref, x_ref, g_ref, o_ref, *, final_norm):
    k = pl.program_id(1)

    @pl.when(k == 0)
    def _():
        o_ref[...] = x_ref[...]

    o_ref[...] += jnp.dot(a_ref[...], w_ref[...], preferred_element_type=F32)

    if final_norm:
        @pl.when(k == pl.num_programs(1) - 1)
        def _():
            o_ref[...] = _rms(o_ref[...], g_ref[...])


def _ffn_down(a, w, x, g_final, *, tm=ROW_TILE, tk=FFN_K_TILE):
    m, kdim = a.shape
    d = w.shape[1]
    tm, tk = min(tm, m), min(tk, kdim)
    assert m % tm == 0 and kdim % tk == 0
    final_norm = g_final is not None
    g = (g_final if final_norm else jnp.ones((d,), F32)).reshape(1, d).astype(F32)
    return pl.pallas_call(
        functools.partial(_ffn_down_body, final_norm=final_norm),
        grid=(m // tm, kdim // tk),
        in_specs=[
            pl.BlockSpec((tm, tk), lambda i, k: (i, k)),
            pl.BlockSpec((tk, d), lambda i, k: (k, 0)),
            pl.BlockSpec((tm, d), lambda i, k: (i, 0)),
            pl.BlockSpec((1, d), lambda i, k: (0, 0)),
        ],
        out_specs=pl.BlockSpec((tm, d), lambda i, k: (i, 0)),
        out_shape=jax.ShapeDtypeStruct((m, d), F32),
        compiler_params=_params("parallel", "arbitrary"),
        name="ffn_down",
    )(a, w, x, g)


def _layer(x, mem, g_mix, w_in, b_gate, w_proj_sb, w_proj_dsa, w_out, rel_bias,
           g_cross, g_mem, w_cq, w_ckv, w_co, g_ffn, w_up, conv_w, conv_b, w_down, g_final, batch, seq):
    d = x.shape[1]
    width = N_HEADS * HEAD_DIM
    idx_w = IDX_HEADS * IDX_DIM
    o_qi = 6 * width
    o_ki = o_qi + idx_w
    o_wi = o_ki + IDX_DIM
    o_g = o_wi + IDX_HEADS

    zeros = jnp.zeros((d, LANES - IDX_DIM), F32)
    w_small = jnp.concatenate([
        w_in[:, o_ki:o_wi], zeros, zeros, w_in[:, o_ki:o_wi],
        jnp.pad(w_in[:, o_wi:o_g], ((0, 0), (0, LANES - IDX_HEADS)))], axis=1).astype(BF16)
    h, small = _rmsnorm_proj(x, g_mix, w_small, name="mixer_norm_index_proj", proj_dtype=F32)
    w_in_t = w_in.T
    tq = ATT_BLOCK
    nq = seq // tq
    q_sb, k_sb, v_sb, q_ds, k_ds, v_ds, q_ix = (g * width for g in range(o_ki // width))
    gate_rows = tuple(o_g + g * width for g in range(2 * d // width))
    n_gate = len(gate_rows)
    main = _matmul_ws(h, w_in_t, name="in_proj_main",
                      first_rows=gate_rows + (q_sb, k_sb, q_ds, k_ds, q_ix),
                      out_dtype=BF16, tm=ROW_TILE, tn=width)
    v_t = _matmul_ws_t(h, w_in_t, name="in_proj_values", first_rows=(v_sb, v_ds), tn=width, tq=tq,
                       tm=ROW_TILE)
    v_t = v_t.reshape(2, batch, nq, width, tq)

    o_sb = _sb_attention(main, v_t, batch, seq, q_col=n_gate, k_col=n_gate + 1, v_branch=0, tq=tq)
    bias = _near_bias(rel_bias, tq)
    o_ds = _dsa_attention(main, v_t, small, bias, batch, seq, qd_col=n_gate + 2, kd_col=n_gate + 3,
                          qi_col=n_gate + 4, v_branch=1, tq=tq)

    _, kv = _rmsnorm_proj(mem, g_mem, w_ckv.astype(BF16), name="mem_norm_kv_proj", proj_dtype=BF16)
    x, h_ffn = _merge_cross(o_sb, o_ds, w_proj_sb.astype(BF16), w_proj_dsa.astype(BF16),
                            w_out.astype(BF16), main, 0, b_gate, x, kv, g_cross, w_cq.astype(BF16),
                            w_co.astype(BF16), g_ffn, batch, seq, tm=MERGE_ROWS)

    act = _ffn_up_gate(h_ffn, w_up, conv_w, conv_b, seq)
    return _ffn_down(act, w_down.astype(BF16), x, g_final)


def kernel(x, mem, g_mix, w_in, b_gate, w_proj_sb, w_proj_dsa, w_out, rel_bias, g_cross, g_mem,
           w_cq, w_ckv, w_co, g_ffn, w_up, conv_w, conv_b, w_down, g_final):
    batch, seq, d = x.shape
    h = x.reshape(batch * seq, d)
    mem2 = mem.reshape(batch * mem.shape[1], d)
    depth = g_mix.shape[0]
    for l in range(depth):
        h = _layer(h, mem2, g_mix[l], w_in[l], b_gate[l], w_proj_sb[l], w_proj_dsa[l], w_out[l],
                   rel_bias, g_cross[l], g_mem[l], w_cq[l], w_ckv[l], w_co[l], g_ffn[l], w_up[l],
                   conv_w[l], conv_b[l], w_down[l], g_final if l == depth - 1 else None, batch, seq)
    return h.reshape(batch, seq, d)
```

```python
import functools

import jax
import jax.numpy as jnp
from jax import lax
from jax.experimental import pallas as pl
from jax.experimental.pallas import tpu as pltpu

F32, BF16, I32 = jnp.float32, jnp.bfloat16, jnp.int32

EPS = 1e-6
HEAD_DIM = 128
N_HEADS = 8
IDX_HEADS = 16
IDX_DIM = 64
CHUNK = 64
TOPK_MAX = 256
N_BUCKETS = 32
MAX_DISTANCE = 128
MEM_HEADS = 4
CONV_WIDTH = 3

LANES = 128
SUBLANES = 8
VMEM_LIMIT_BYTES = 56 * 1024 * 1024
NEG_BIG = -1e30
EXP2_UNDERFLOW = -151.0
LOG2E = 1.4426950408889634
KEY_BITS = 32
SIGN_BIT = -(2 ** 31)

ROW_TILE = 1024
NORM_ROWS = 512
MERGE_ROWS = 256
FFN_COL_TILE = 512
FFN_K_TILE = 1024
ATT_BLOCK = KEY_BITS * SUBLANES

_NT = (((1,), (1,)), ((), ()))


def _params(*sem):
    return pltpu.CompilerParams(dimension_semantics=sem, vmem_limit_bytes=VMEM_LIMIT_BYTES)


def _rms(x, g):
    inv = lax.rsqrt(jnp.mean(x * x, axis=-1, keepdims=True) + EPS)
    return x * inv * g


def _mm_ws_body(a_ref, wt_ref, o_ref, wb_ref):
    @pl.when(pl.program_id(1) == 0)
    def _():
        wb_ref[...] = wt_ref[...].astype(BF16)

    o_ref[...] = lax.dot_general(a_ref[...], wb_ref[...], _NT,
                                 preferred_element_type=F32).astype(o_ref.dtype)


def _row_window(first_rows, tn, k):
    assert all(r % SUBLANES == 0 for r in first_rows)

    def index_map(j, i):
        row = jnp.int32(first_rows[0])
        for step, first in enumerate(first_rows[1:], start=1):
            row = jnp.where(j >= step, first, row)
        return pl.multiple_of(row, SUBLANES), 0

    return pl.BlockSpec((pl.Element(tn), pl.Element(k)), index_map)


def _matmul_ws(a, w_t, *, name, first_rows, out_dtype, tm, tn):
    m, k = a.shape
    tm = min(tm, m)
    assert m % tm == 0 and max(first_rows) + tn <= w_t.shape[0] and w_t.shape[1] == k
    return pl.pallas_call(
        _mm_ws_body,
        grid=(len(first_rows), m // tm),
        in_specs=[pl.BlockSpec((tm, k), lambda j, i: (i, 0)), _row_window(first_rows, tn, k)],
        out_specs=pl.BlockSpec((tm, tn), lambda j, i: (i, j)),
        out_shape=jax.ShapeDtypeStruct((m, len(first_rows) * tn), out_dtype),
        scratch_shapes=[pltpu.VMEM((tn, k), BF16)],
        compiler_params=_params("parallel", "arbitrary"),
        name=name,
    )(a, w_t)


def _mm_ws_t_body(wt_ref, a_ref, o_ref, wb_ref):
    @pl.when(pl.program_id(1) == 0)
    def _():
        wb_ref[...] = wt_ref[...].astype(BF16)

    res = lax.dot_general(wb_ref[...], a_ref[...], _NT, preferred_element_type=F32)
    n_blocks, _, tq = o_ref.shape
    for blk in range(n_blocks):
        o_ref[blk] = res[:, blk * tq:(blk + 1) * tq].astype(o_ref.dtype)


def _matmul_ws_t(a, w_t, *, name, first_rows, tn, tq, tm):
    m, k = a.shape
    tm = min(tm, m)
    assert m % tm == 0 and tm % tq == 0 and max(first_rows) + tn <= w_t.shape[0] and w_t.shape[1] == k
    per_step = tm // tq
    return pl.pallas_call(
        _mm_ws_t_body,
        grid=(len(first_rows), m // tm),
        in_specs=[_row_window(first_rows, tn, k), pl.BlockSpec((tm, k), lambda j, i: (i, 0))],
        out_specs=pl.BlockSpec((None, per_step, tn, tq), lambda j, i: (j, i, 0, 0)),
        out_shape=jax.ShapeDtypeStruct((len(first_rows), m // tq, tn, tq), BF16),
        scratch_shapes=[pltpu.VMEM((tn, k), BF16)],
        compiler_params=_params("parallel", "arbitrary"),
        name=name,
    )(w_t, a)


def _norm_proj_body(x_ref, g_ref, w_ref, h_ref, p_ref):
    h = _rms(x_ref[...], g_ref[...]).astype(BF16)
    h_ref[...] = h
    p_ref[...] = jnp.dot(h, w_ref[...], preferred_element_type=F32).astype(p_ref.dtype)


def _rmsnorm_proj(x, g, w, *, name, proj_dtype, tm=NORM_ROWS):
    m, d = x.shape
    n = w.shape[1]
    tm = min(tm, m)
    assert m % tm == 0
    return pl.pallas_call(
        _norm_proj_body,
        grid=(m // tm,),
        in_specs=[pl.BlockSpec((tm, d), lambda i: (i, 0)), pl.BlockSpec((1, d), lambda i: (0, 0)),
                  pl.BlockSpec((d, n), lambda i: (0, 0))],
        out_specs=[pl.BlockSpec((tm, d), lambda i: (i, 0)), pl.BlockSpec((tm, n), lambda i: (i, 0))],
        out_shape=[jax.ShapeDtypeStruct((m, d), BF16), jax.ShapeDtypeStruct((m, n), proj_dtype)],
        compiler_params=_params("parallel"),
        name=name,
    )(x, g.reshape(1, d).astype(F32), w)


def _sb_body(q_ref, k_ref, vt_ref, tri_ref, o_ref, qs_ref, z_ref, lb_ref, wb_ref, acc_ref, *, tq):
    i = pl.program_id(1)
    shape = (tq, tq)
    before = lax.broadcasted_iota(I32, shape, 0) < lax.broadcasted_iota(I32, shape, 1)
    heads = [slice(h * HEAD_DIM, (h + 1) * HEAD_DIM) for h in range(N_HEADS)]
    qs_ref[...] = (q_ref[...].astype(F32) * (HEAD_DIM ** -0.5 * LOG2E)).astype(BF16)
    acc_ref[...] = jnp.zeros(acc_ref.shape, F32)

    def tile(j, carry, diagonal):
        start = pl.multiple_of(j * tq, tq)
        for h, hs in enumerate(heads):
            z = lax.dot_general(k_ref[pl.ds(start, tq), hs], qs_ref[:, hs], _NT,
                                preferred_element_type=F32)
            neg_z = -z
            log_keep = jnp.minimum(neg_z, 0.0) - jnp.log2(1.0 + jnp.exp2(jnp.minimum(z, neg_z)))
            if diagonal:
                log_keep = jnp.where(before, log_keep, 0.0)
            z_ref[h] = z
            lb_ref[h] = log_keep.astype(BF16)
        new_carry = []
        for h in range(N_HEADS):
            c = jnp.dot(tri_ref[...], lb_ref[h], preferred_element_type=F32) + carry[h:h + 1, :]
            w = jnp.exp2(z_ref[h] + c)
            if diagonal:
                w = jnp.where(before, w, 0.0)
            wb_ref[h] = w.astype(BF16)
            new_carry.append(c[0:1, :])
        for h, hs in enumerate(heads):
            acc_ref[h] += jnp.dot(vt_ref[j, hs, :], wb_ref[h], preferred_element_type=F32)
        return jnp.concatenate(new_carry, axis=0)

    carry = tile(i, jnp.zeros((N_HEADS, tq), F32), True)

    def cond(state):
        j, live, _ = state
        return jnp.logical_and(j >= 0, live > EXP2_UNDERFLOW)

    def body(state):
        j, _, carry = state
        carry = tile(j, carry, False)
        return j - 1, jnp.max(carry), carry

    lax.while_loop(cond, body, (i - 1, jnp.max(carry), carry))
    for h, hs in enumerate(heads):
        o_ref[:, hs] = acc_ref[h].T.astype(o_ref.dtype)


def _sb_attention(qk, v_t, batch, seq, *, q_col, k_col, v_branch, tq):
    nq = seq // tq
    width = N_HEADS * HEAD_DIM
    tri = (jnp.arange(tq)[None, :] >= jnp.arange(tq)[:, None]).astype(BF16)
    resident = pl.Buffered(1)
    return pl.pallas_call(
        functools.partial(_sb_body, tq=tq),
        grid=(batch, nq),
        in_specs=[
            pl.BlockSpec((tq, width), lambda b, i: (b * nq + i, q_col)),
            pl.BlockSpec((seq, width), lambda b, i: (b, k_col), pipeline_mode=resident),
            pl.BlockSpec((None, None, nq, width, tq), lambda b, i: (v_branch, b, 0, 0, 0),
                         pipeline_mode=resident),
            pl.BlockSpec((tq, tq), lambda b, i: (0, 0), pipeline_mode=resident),
        ],
        out_specs=pl.BlockSpec((tq, width), lambda b, i: (b * nq + i, 0)),
        out_shape=jax.ShapeDtypeStruct((batch * seq, width), BF16),
        scratch_shapes=[
            pltpu.VMEM((tq, width), BF16),
            pltpu.VMEM((N_HEADS, tq, tq), F32),
            pltpu.VMEM((N_HEADS, tq, tq), BF16),
            pltpu.VMEM((N_HEADS, tq, tq), BF16),
            pltpu.VMEM((N_HEADS, HEAD_DIM, tq), F32),
        ],
        compiler_params=_params("parallel", "arbitrary"),
        name="sb_attention",
    )(qk, qk, v_t, tri)


def _bucket_thresholds():
    nb = N_BUCKETS // 2
    max_exact = nb // 2
    span = nb - max_exact
    out = []
    for k in range(1, span):
        n = max_exact
        while n ** span * max_exact ** k < MAX_DISTANCE ** k * max_exact ** span:
            n += 1
        out.append(n)
    return max_exact, out


def _bias_body(rb_ref, o_ref, *, tq):
    nb = N_BUCKETS // 2
    max_exact, steps = _bucket_thresholds()
    shape = (2 * tq, tq)
    rel = lax.broadcasted_iota(I32, shape, 0) - lax.broadcasted_iota(I32, shape, 1) - tq
    n = jnp.abs(rel)
    large = jnp.full(shape, max_exact, I32)
    for t in steps:
        large = large + (n >= t).astype(I32)
    bucket = jnp.where(rel > 0, nb, 0) + jnp.where(n < max_exact, n, large)
    for h in range(N_HEADS):
        val = jnp.zeros(shape, F32)
        for b in range(N_BUCKETS):
            val = jnp.where(bucket == b, rb_ref[b, h], val)
        o_ref[h] = (val - rb_ref[nb - 1, h]) * LOG2E


def _near_bias(rel_bias, tq):
    return pl.pallas_call(
        functools.partial(_bias_body, tq=tq),
        in_specs=[pl.BlockSpec(memory_space=pltpu.SMEM)],
        out_specs=pl.BlockSpec(memory_space=pltpu.VMEM),
        out_shape=jax.ShapeDtypeStruct((N_HEADS, 2 * tq, tq), F32),
        compiler_params=pltpu.CompilerParams(vmem_limit_bytes=VMEM_LIMIT_BYTES),
        name="dsa_near_bias",
    )(rel_bias.astype(F32))


def _order_key(x):
    bits = lax.bitcast_convert_type(x, I32)
    return bits ^ ((bits >> 31) | SIGN_BIT)


def _order_key_to_float(key):
    return lax.bitcast_convert_type(key ^ ((~key >> 31) | SIGN_BIT), F32)


def _bit_transpose32(words):
    a = list(words)
    j, m = 16, 0x0000FFFF
    while j:
        mask = jnp.int32(m - (1 << 32) if m >= 1 << 31 else m)
        k = 0
        while k < 32:
            t = (lax.shift_right_logical(a[k], jnp.int32(j)) ^ a[k + j]) & mask
            a[k] = a[k] ^ lax.shift_left(t, jnp.int32(j))
            a[k + j] = a[k + j] ^ t
            k = (k + j + 1) & ~j
        j >>= 1
        m = (m ^ (m << j)) & 0xFFFFFFFF
    return a


def _dsa_body(qd_ref, qi_ref, wq_ref, kd_ref, vt_ref, ki_ref, bias_ref, tri_ref, o_ref,
              sc_ref, plane_ref, qs_ref, m_ref, l_ref, acc_ref, lg_ref, bmax_ref, p_ref, *, tq, top):
    i = pl.program_id(1)
    shape = (tq, tq)
    key_row = lax.broadcasted_iota(I32, shape, 0)
    qry_col = lax.broadcasted_iota(I32, shape, 1)
    visible = key_row // CHUNK <= qry_col // CHUNK

    w_t = (wq_ref[...] * (IDX_DIM ** -0.5 * IDX_HEADS ** -0.5)).T

    heads_per_vreg = LANES // IDX_DIM

    def score_tile(j):
        start = pl.multiple_of(j * tq, tq)
        ki = [ki_ref[pl.ds(start, tq), c * LANES:(c + 1) * LANES].astype(BF16)
              for c in range(heads_per_vreg)]
        s = jnp.zeros(shape, F32)
        for h in range(IDX_HEADS):
            g, c = divmod(h, heads_per_vreg)
            d = lax.dot_general(ki[c], qi_ref[:, g * LANES:(g + 1) * LANES], _NT,
                                preferred_element_type=F32)
            s = s + w_t[h:h + 1, :] * jnp.maximum(d, 0.0)
        sc_ref[j] = s

    def key_planes(j):
        rows = pl.ds(pl.multiple_of(j * SUBLANES, SUBLANES), SUBLANES)
        for c0 in range(0, tq, LANES):
            lanes = slice(c0, c0 + LANES)
            ukey = _order_key(sc_ref[j, :, lanes])
            planes = _bit_transpose32(
                [ukey[g * SUBLANES:(g + 1) * SUBLANES, :] for g in range(KEY_BITS)])
            for b in range(KEY_BITS):
                plane_ref[b, rows, lanes] = planes[b]

    @pl.when(i == 0)
    def _():
        plane_ref[...] = jnp.zeros(plane_ref.shape, I32)

    def score_step(j, carry):
        key_planes(j)
        score_tile(j + 1)
        return carry

    score_tile(0)
    lax.fori_loop(0, i, score_step, 0)
    key_planes(i)
    sc_ref[i] = jnp.where(visible, sc_ref[i], -jnp.inf)

    n_rows = plane_ref.shape[1]
    block_of_row = lax.broadcasted_iota(I32, (n_rows, tq), 0) // SUBLANES
    qry_of_col = lax.broadcasted_iota(I32, (n_rows, tq), 1)
    n_bits = (qry_of_col // CHUNK + 1) * (CHUNK // SUBLANES)
    diag_bits = jnp.where(n_bits >= KEY_BITS, -1, lax.shift_left(jnp.int32(1), n_bits) - 1)
    cand0 = jnp.where(block_of_row < i, -1, jnp.where(block_of_row == i, diag_bits, 0))

    def popcount_rows(words):
        return jnp.sum(lax.population_count(words), axis=0, keepdims=True)

    def bit_step(t, state):
        cand, n_above, thr_bits = state
        b = KEY_BITS - 1 - t
        ones = cand & plane_ref[b]
        n_ones = popcount_rows(ones)
        take = n_above + n_ones >= top
        cand = jnp.where(take, ones, cand ^ ones)
        n_above = jnp.where(take, n_above, n_above + n_ones)
        thr_bits = thr_bits | jnp.where(take, lax.shift_left(jnp.int32(1), b), 0)
        return cand, n_above, thr_bits

    zero = jnp.zeros((1, tq), I32)
    cand, n_above, thr_bits = lax.fori_loop(0, KEY_BITS, bit_step, (cand0, zero, zero))
    qry = lax.broadcasted_iota(I32, (1, tq), 1)
    n_visible = i * tq + (qry // CHUNK + 1) * CHUNK
    wanted = n_visible > top
    thr = jnp.where(wanted, _order_key_to_float(thr_bits), jnp.finfo(F32).min)
    tied = jnp.logical_and(wanted, n_above + popcount_rows(cand) > top)
    c_hi = n_above

    ones = jnp.ones((2 * SUBLANES, tq), BF16)

    def plain_mask(j, carry):
        sc_ref[j] = jnp.where(sc_ref[j] >= thr, 0.0, NEG_BIG)
        return carry

    def tie_mask(j, seen):
        s = sc_ref[j]
        equal = s == thr
        rank = jnp.dot(tri_ref[...], equal.astype(BF16), preferred_element_type=F32) + seen
        quota = jnp.where(tied, (top - c_hi).astype(F32), jnp.inf)
        keep_equal = jnp.where(rank < quota, 0.0, NEG_BIG)
        sc_ref[j] = jnp.where(s > thr, 0.0, jnp.where(equal, keep_equal, NEG_BIG))
        return seen + jnp.sum(equal.astype(F32), axis=0, keepdims=True)

    def with_ties():
        lax.fori_loop(0, i + 1, tie_mask, jnp.zeros((1, tq), F32))
        return jnp.int32(0)

    def without_ties():
        return lax.fori_loop(0, i + 1, plain_mask, jnp.int32(0))

    lax.cond(jnp.max(tied.astype(I32)) > 0, with_ties, without_ties)

    qs_ref[...] = (qd_ref[...].astype(F32) * (HEAD_DIM ** -0.5 * LOG2E)).astype(BF16)
    m_ref[...] = jnp.full(m_ref.shape, NEG_BIG, F32)
    l_ref[...] = jnp.zeros(l_ref.shape, F32)
    acc_ref[...] = jnp.zeros(acc_ref.shape, F32)

    heads = [slice(h * HEAD_DIM, (h + 1) * HEAD_DIM) for h in range(N_HEADS)]

    far, prev, diag = None, 0, 1

    def logits(u, near, slot):
        j = i - u
        start = pl.multiple_of(j * tq, tq)
        mask = sc_ref[j]
        block_max = []
        for h, hs in enumerate(heads):
            lg = lax.dot_general(kd_ref[pl.ds(start, tq), hs], qs_ref[:, hs], _NT,
                                 preferred_element_type=F32)
            if near is not None:
                lg = lg + bias_ref[h, near * tq:(near + 1) * tq, :]
            lg = lg + mask
            lg_ref[slot, h] = lg
            block_max.append(jnp.max(lg, axis=0, keepdims=True))
        bmax_ref[slot] = jnp.concatenate(block_max, axis=0)

    def values(u, slot):
        j = i - u
        m_old = m_ref[...]
        m_new = jnp.maximum(m_old, bmax_ref[slot])
        alpha = jnp.exp2(m_old - m_new)
        m_ref[...] = m_new
        for h in range(N_HEADS):
            p_ref[h] = jnp.exp2(lg_ref[slot, h] - m_new[h:h + 1, :]).astype(BF16)
        denom = []
        for h, hs in enumerate(heads):
            v_ext = jnp.concatenate([vt_ref[j, hs, :], ones], axis=0)
            pv = jnp.dot(v_ext, p_ref[h], preferred_element_type=F32)
            acc_ref[h] = alpha[h:h + 1, :] * acc_ref[h] + pv[:HEAD_DIM]
            denom.append(pv[HEAD_DIM:HEAD_DIM + 1])
        l_ref[...] = alpha * l_ref[...] + jnp.concatenate(denom, axis=0)

    def even_step(u, near_a, near_b):
        logits(u - 1, near_a, 1)
        values(u, 0)
        logits(u - 2, near_b, 0)
        values(u - 1, 1)

    @pl.when(i == 0)
    def _():
        logits(0, diag, 0)
        values(0, 0)

    @pl.when(i == 1)
    def _():
        logits(1, prev, 1)

    @pl.when(jnp.logical_and(i >= 2, i % 2 == 1))
    def _():
        logits(i, far, 1)
        logits(i - 1, far, 0)
        values(i, 1)

    @pl.when(jnp.logical_and(i >= 2, i % 2 == 0))
    def _():
        logits(i, far, 0)

    def far_pair(k, carry):
        even_step(2 * (i // 2 - k), far, far)
        return carry

    lax.fori_loop(0, i // 2 - 1, far_pair, 0)

    @pl.when(i >= 2)
    def _():
        even_step(2, prev, diag)

    @pl.when(i == 1)
    def _():
        logits(0, diag, 0)
        values(1, 1)

    @pl.when(i >= 1)
    def _():
        values(0, 0)

    for h in range(N_HEADS):
        o = acc_ref[h] / l_ref[h:h + 1, :]
        o_ref[:, h * HEAD_DIM:(h + 1) * HEAD_DIM] = o.T.astype(o_ref.dtype)


def _dsa_attention(main, v_t, small, bias, batch, seq, *, qd_col, kd_col, qi_col, v_branch, tq):
    nq = seq // tq
    width = N_HEADS * HEAD_DIM
    assert IDX_HEADS * IDX_DIM == width and tq == KEY_BITS * SUBLANES and tq % CHUNK == 0
    key_copies = LANES // IDX_DIM
    top = min(TOPK_MAX, seq // 4)
    tri = (jnp.arange(tq)[None, :] < jnp.arange(tq)[:, None]).astype(BF16)
    resident = pl.Buffered(1)
    return pl.pallas_call(
        functools.partial(_dsa_body, tq=tq, top=top),
        grid=(batch, nq),
        in_specs=[
            pl.BlockSpec((tq, width), lambda b, i: (b * nq + i, qd_col)),
            pl.BlockSpec((tq, IDX_HEADS * IDX_DIM), lambda b, i: (b * nq + i, qi_col)),
            pl.BlockSpec((tq, LANES), lambda b, i: (b * nq + i, key_copies)),
            pl.BlockSpec((seq, width), lambda b, i: (b, kd_col), pipeline_mode=resident),
            pl.BlockSpec((None, None, nq, width, tq), lambda b, i: (v_branch, b, 0, 0, 0),
                         pipeline_mode=resident),
            pl.BlockSpec((seq, key_copies * LANES), lambda b, i: (b, 0), pipeline_mode=resident),
            pl.BlockSpec((N_HEADS, 2 * tq, tq), lambda b, i: (0, 0, 0), pipeline_mode=resident),
            pl.BlockSpec((tq, tq), lambda b, i: (0, 0), pipeline_mode=resident),
        ],
        out_specs=pl.BlockSpec((tq, width), lambda b, i: (b * nq + i, 0)),
        out_shape=jax.ShapeDtypeStruct((batch * seq, width), BF16),
        scratch_shapes=[
            pltpu.VMEM((nq, tq, tq), F32),
            pltpu.VMEM((KEY_BITS, nq * SUBLANES, tq), I32),
            pltpu.VMEM((tq, width), BF16),
            pltpu.VMEM((N_HEADS, tq), F32),
            pltpu.VMEM((N_HEADS, tq), F32),
            pltpu.VMEM((N_HEADS, HEAD_DIM, tq), F32),
            pltpu.VMEM((2, N_HEADS, tq, tq), F32),
            pltpu.VMEM((2, N_HEADS, tq), F32),
            pltpu.VMEM((N_HEADS, tq, tq), BF16),
        ],
        compiler_params=_params("parallel", "arbitrary"),
        name="dsa_attention",
    )(main, main, small, main, v_t, small, bias, tri)


def _merge_cross_body(osb_ref, ods_ref, wsb_ref, wds_ref, gsb_ref, gds_ref, bsb_ref, bds_ref, wo_ref,
                      x_ref, gc_ref, wq_ref, km_ref, vm_ref, wco_ref, gn_ref, o_ref, hn_ref):
    p_sb = jnp.dot(osb_ref[...], wsb_ref[...], preferred_element_type=F32)
    p_ds = jnp.dot(ods_ref[...], wds_ref[...], preferred_element_type=F32)
    g_sb = jax.nn.sigmoid(gsb_ref[...].astype(F32) + bsb_ref[...])
    g_ds = jax.nn.sigmoid(gds_ref[...].astype(F32) + bds_ref[...])
    merged = (g_sb * p_sb + g_ds * p_ds).astype(BF16)
    x1 = x_ref[...] + jnp.dot(merged, wo_ref[...], preferred_element_type=F32)

    h = _rms(x1, gc_ref[...]).astype(BF16)
    q = jnp.dot(h, wq_ref[...], preferred_element_type=F32) * HEAD_DIM ** -0.5
    q = q.astype(BF16)
    outs = []
    for hh in range(MEM_HEADS):
        hs = slice(hh * HEAD_DIM, (hh + 1) * HEAD_DIM)
        lg = lax.dot_general(q[:, hs], km_ref[:, hs], _NT, preferred_element_type=F32)
        p = jnp.exp(lg - jnp.max(lg, axis=1, keepdims=True))
        o = jnp.dot(p.astype(BF16), vm_ref[:, hs], preferred_element_type=F32)
        outs.append((o / jnp.sum(p, axis=1, keepdims=True)).astype(BF16))
    o = jnp.concatenate(outs, axis=1)
    x2 = x1 + jnp.dot(o, wco_ref[...], preferred_element_type=F32)
    o_ref[...] = x2
    hn_ref[...] = _rms(x2, gn_ref[...]).astype(hn_ref.dtype)


def _merge_cross(o_sb, o_ds, w_sb, w_ds, w_out, proj, gate_offset, b_gate, x,
                 kv, g_cross, w_cq, w_co, g_next, batch, seq, *, tm):
    m, k = o_sb.shape
    d = w_sb.shape[1]
    n_mem = kv.shape[0] // batch
    width = MEM_HEADS * HEAD_DIM
    tm = min(tm, seq)
    assert gate_offset % d == 0 and seq % tm == 0
    g = gate_offset // d
    nt = seq // tm
    b_gate = b_gate.reshape(1, 2 * d).astype(F32)
    resident = pl.Buffered(1)

    def rows(width_, col=0):
        return pl.BlockSpec((tm, width_), lambda b, i: (b * nt + i, col))

    def whole(shape, *block):
        return pl.BlockSpec(shape, lambda b, i: block or (0,) * len(shape), pipeline_mode=resident)

    return pl.pallas_call(
        _merge_cross_body,
        grid=(batch, nt),
        in_specs=[
            rows(k), rows(k), whole((k, d)), whole((k, d)),
            rows(d, g), rows(d, g + 1), whole((1, d)), whole((1, d), 0, 1),
            whole((d, d)), rows(d),
            whole((1, d)), whole((d, width)),
            pl.BlockSpec((n_mem, width), lambda b, i: (b, 0)),
            pl.BlockSpec((n_mem, width), lambda b, i: (b, 1)),
            whole((width, d)), whole((1, d)),
        ],
        out_specs=[rows(d), rows(d)],
        out_shape=[jax.ShapeDtypeStruct((m, d), F32), jax.ShapeDtypeStruct((m, d), BF16)],
        compiler_params=_params("parallel", "parallel"),
        name="merge_out_cross",
    )(o_sb, o_ds, w_sb, w_ds, proj, proj, b_gate, b_gate, w_out, x,
      g_cross.reshape(1, d).astype(F32), w_cq, kv, kv, w_co, g_next.reshape(1, d).astype(F32))


def _delayed(u, tail, shift):
    rolled = pltpu.roll(u, shift, axis=0)
    row = lax.broadcasted_iota(I32, tail.shape, 0)
    head = jnp.where(row < shift, pltpu.roll(tail, shift, axis=0), rolled[:SUBLANES])
    return jnp.concatenate([head, rolled[SUBLANES:]], axis=0)


def _ffn_up_body(h_ref, wa_ref, wv_ref, cwa_ref, cwv_ref, cba_ref, cbv_ref, o_ref,
                 wab_ref, wvb_ref, halo_ref, *, tiles_per_seq):
    i = pl.program_id(1)

    @pl.when(i == 0)
    def _():
        wab_ref[...] = wa_ref[...].astype(BF16)
        wvb_ref[...] = wv_ref[...].astype(BF16)

    h = h_ref[...]
    tm = h.shape[0]
    sequence_start = i % tiles_per_seq == 0

    def conv(wb_ref, cw_ref, cb_ref, slot):
        u = jnp.dot(h, wb_ref[...], preferred_element_type=F32)
        tail = jnp.where(sequence_start, 0.0, halo_ref[slot])
        halo_ref[slot] = u[tm - SUBLANES:, :]
        c = cb_ref[...] + cw_ref[CONV_WIDTH - 1:CONV_WIDTH, :] * u
        for tap in range(CONV_WIDTH - 1):
            c = c + cw_ref[tap:tap + 1, :] * _delayed(u, tail, CONV_WIDTH - 1 - tap)
        return c

    a = conv(wab_ref, cwa_ref, cba_ref, 0)
    val = conv(wvb_ref, cwv_ref, cbv_ref, 1)
    o_ref[...] = (jax.nn.gelu(a) * val).astype(o_ref.dtype)


def _ffn_up_gate(h, w_up, conv_w, conv_b, seq, *, tm=ROW_TILE, tn=FFN_COL_TILE):
    m, d = h.shape
    two_ff = w_up.shape[1]
    d_ff = two_ff // 2
    tm, tn = min(tm, seq), min(tn, d_ff)
    assert seq % tm == 0 and d_ff % tn == 0 and tm >= SUBLANES >= CONV_WIDTH - 1
    nf = d_ff // tn
    conv_w = conv_w.astype(F32)
    conv_b = conv_b.reshape(1, two_ff).astype(F32)
    return pl.pallas_call(
        functools.partial(_ffn_up_body, tiles_per_seq=seq // tm),
        grid=(nf, m // tm),
        in_specs=[
            pl.BlockSpec((tm, d), lambda j, i: (i, 0)),
            pl.BlockSpec((d, tn), lambda j, i: (0, j)),
            pl.BlockSpec((d, tn), lambda j, i: (0, nf + j)),
            pl.BlockSpec((CONV_WIDTH, tn), lambda j, i: (0, j)),
            pl.BlockSpec((CONV_WIDTH, tn), lambda j, i: (0, nf + j)),
            pl.BlockSpec((1, tn), lambda j, i: (0, j)),
            pl.BlockSpec((1, tn), lambda j, i: (0, nf + j)),
        ],
        out_specs=pl.BlockSpec((tm, tn), lambda j, i: (i, j)),
        out_shape=jax.ShapeDtypeStruct((m, d_ff), BF16),
        scratch_shapes=[pltpu.VMEM((d, tn), BF16), pltpu.VMEM((d, tn), BF16),
                        pltpu.VMEM((2, SUBLANES, tn), F32)],
        compiler_params=_params("parallel", "arbitrary"),
        name="ffn_up_conv_gate",
    )(h, w_up, w_up, conv_w, conv_w, conv_b, conv_b)


def _ffn_down_body(a_ref, w_ref, x_ref, g_ref, o_ref, *, final_norm):
    k = pl.program_id(1)

    @pl.when(k == 0)
    def _():
        o_ref[...] = x_ref[...]

    o_ref[...] += jnp.dot(a_ref[...], w_ref[...], preferred_element_type=F32)

    if final_norm:
        @pl.when(k == pl.num_programs(1) - 1)
        def _():
            o_ref[...] = _rms(o_ref[...], g_ref[...])


def _ffn_down(a, w, x, g_final, *, tm=ROW_TILE, tk=FFN_K_TILE):
    m, kdim = a.shape
    d = w.shape[1]
    tm, tk = min(tm, m), min(tk, kdim)
    assert m % tm == 0 and kdim % tk == 0
    final_norm = g_final is not None
    g = (g_final if final_norm else jnp.ones((d,), F32)).reshape(1, d).astype(F32)
    return pl.pallas_call(
        functools.partial(_ffn_down_body, final_norm=final_norm),
        grid=(m // tm, kdim // tk),
        in_specs=[
            pl.BlockSpec((tm, tk), lambda i, k: (i, k)),
            pl.BlockSpec((tk, d), lambda i, k: (k, 0)),
            pl.BlockSpec((tm, d), lambda i, k: (i, 0)),
            pl.BlockSpec((1, d), lambda i, k: (0, 0)),
        ],
        out_specs=pl.BlockSpec((tm, d), lambda i, k: (i, 0)),
        out_shape=jax.ShapeDtypeStruct((m, d), F32),
        compiler_params=_params("parallel", "arbitrary"),
        name="ffn_down",
    )(a, w, x, g)


def _layer(x, mem, g_mix, w_in, b_gate, w_proj_sb, w_proj_dsa, w_out, rel_bias,
           g_cross, g_mem, w_cq, w_ckv, w_co, g_ffn, w_up, conv_w, conv_b, w_down, g_final, batch, seq):
    d = x.shape[1]
    width = N_HEADS * HEAD_DIM
    idx_w = IDX_HEADS * IDX_DIM
    o_qi = 6 * width
    o_ki = o_qi + idx_w
    o_wi = o_ki + IDX_DIM
    o_g = o_wi + IDX_HEADS

    zeros = jnp.zeros((d, LANES - IDX_DIM), F32)
    w_small = jnp.concatenate([
        w_in[:, o_ki:o_wi], zeros, zeros, w_in[:, o_ki:o_wi],
        jnp.pad(w_in[:, o_wi:o_g], ((0, 0), (0, LANES - IDX_HEADS)))], axis=1).astype(BF16)
    h, small = _rmsnorm_proj(x, g_mix, w_small, name="mixer_norm_index_proj", proj_dtype=F32)
    w_in_t = w_in.T
    tq = ATT_BLOCK
    nq = seq // tq
    q_sb, k_sb, v_sb, q_ds, k_ds, v_ds, q_ix = (g * width for g in range(o_ki // width))
    gate_rows = tuple(o_g + g * width for g in range(2 * d // width))
    n_gate = len(gate_rows)
    main = _matmul_ws(h, w_in_t, name="in_proj_main",
                      first_rows=gate_rows + (q_sb, k_sb, q_ds, k_ds, q_ix),
                      out_dtype=BF16, tm=ROW_TILE, tn=width)
    v_t = _matmul_ws_t(h, w_in_t, name="in_proj_values", first_rows=(v_sb, v_ds), tn=width, tq=tq,
                       tm=ROW_TILE)
    v_t = v_t.reshape(2, batch, nq, width, tq)

    o_sb = _sb_attention(main, v_t, batch, seq, q_col=n_gate, k_col=n_gate + 1, v_branch=0, tq=tq)
    bias = _near_bias(rel_bias, tq)
    o_ds = _dsa_attention(main, v_t, small, bias, batch, seq, qd_col=n_gate + 2, kd_col=n_gate + 3,
                          qi_col=n_gate + 4, v_branch=1, tq=tq)

    _, kv = _rmsnorm_proj(mem, g_mem, w_ckv.astype(BF16), name="mem_norm_kv_proj", proj_dtype=BF16)
    x, h_ffn = _merge_cross(o_sb, o_ds, w_proj_sb.astype(BF16), w_proj_dsa.astype(BF16),
                            w_out.astype(BF16), main, 0, b_gate, x, kv, g_cross, w_cq.astype(BF16),
                            w_co.astype(BF16), g_ffn, batch, seq, tm=MERGE_ROWS)

    act = _ffn_up_gate(h_ffn, w_up, conv_w, conv_b, seq)
    return _ffn_down(act, w_down.astype(BF16), x, g_final)


def kernel(x, mem, g_mix, w_in, b_gate, w_proj_sb, w_proj_dsa, w_out, rel_bias, g_cross, g_mem,
           w_cq, w_ckv, w_co, g_ffn, w_up, conv_w, conv_b, w_down, g_final):
    batch, seq, d = x.shape
    h = x.reshape(batch * seq, d)
    mem2 = mem.reshape(batch * mem.shape[1], d)
    depth = g_mix.shape[0]
    for l in range(depth):
        h = _layer(h, mem2, g_mix[l], w_in[l], b_gate[l], w_proj_sb[l], w_proj_dsa[l], w_out[l],
                   rel_bias, g_cross[l], g_mem[l], w_cq[l], w_ckv[l], w_co[l], g_ffn[l], w_up[l],
                   conv_w[l], conv_b[l], w_down[l], g_final if l == depth - 1 else None, batch, seq)
    return h.reshape(batch, seq, d)
```

```python
import functools

import jax
import jax.numpy as jnp
from jax import lax
from jax.experimental import pallas as pl
from jax.experimental.pallas import tpu as pltpu

F32, BF16, I32 = jnp.float32, jnp.bfloat16, jnp.int32

EPS = 1e-6
HEAD_DIM = 128
N_HEADS = 8
IDX_HEADS = 16
IDX_DIM = 64
CHUNK = 64
TOPK_MAX = 256
N_BUCKETS = 32
MAX_DISTANCE = 128
MEM_HEADS = 4
CONV_WIDTH = 3

LANES = 128
SUBLANES = 8
VMEM_LIMIT_BYTES = 56 * 1024 * 1024
NEG_BIG = -1e30
EXP2_UNDERFLOW = -151.0
LOG2E = 1.4426950408889634
KEY_BITS = 32
SIGN_BIT = -(2 ** 31)

PROJ_ROWS = 2048
ROW_TILE = 1024
NORM_ROWS = 512
MERGE_ROWS = 256
FFN_COL_TILE = 512
FFN_K_TILE = 1024
ATT_BLOCK = KEY_BITS * SUBLANES

_NT = (((1,), (1,)), ((), ()))


def _params(*sem):
    return pltpu.CompilerParams(dimension_semantics=sem, vmem_limit_bytes=VMEM_LIMIT_BYTES)


def _rms(x, g):
    inv = lax.rsqrt(jnp.mean(x * x, axis=-1, keepdims=True) + EPS)
    return x * inv * g


def _mm_ws_body(a_ref, wt_ref, o_ref, wb_ref):
    @pl.when(pl.program_id(1) == 0)
    def _():
        wb_ref[...] = wt_ref[...].astype(BF16)

    o_ref[...] = lax.dot_general(a_ref[...], wb_ref[...], _NT,
                                 preferred_element_type=F32).astype(o_ref.dtype)


def _row_window(first_rows, tn, k):
    assert all(r % SUBLANES == 0 for r in first_rows)

    def index_map(j, i):
        row = jnp.int32(first_rows[0])
        for step, first in enumerate(first_rows[1:], start=1):
            row = jnp.where(j >= step, first, row)
        return pl.multiple_of(row, SUBLANES), 0

    return pl.BlockSpec((pl.Element(tn), pl.Element(k)), index_map)


def _matmul_ws(a, w_t, *, name, first_rows, out_dtype, tm, tn):
    m, k = a.shape
    tm = min(tm, m)
    assert m % tm == 0 and max(first_rows) + tn <= w_t.shape[0] and w_t.shape[1] == k
    return pl.pallas_call(
        _mm_ws_body,
        grid=(len(first_rows), m // tm),
        in_specs=[pl.BlockSpec((tm, k), lambda j, i: (i, 0)), _row_window(first_rows, tn, k)],
        out_specs=pl.BlockSpec((tm, tn), lambda j, i: (i, j)),
        out_shape=jax.ShapeDtypeStruct((m, len(first_rows) * tn), out_dtype),
        scratch_shapes=[pltpu.VMEM((tn, k), BF16)],
        compiler_params=_params("parallel", "arbitrary"),
        name=name,
    )(a, w_t)


def _mm_ws_t_body(wt_ref, a_ref, o_ref, wb_ref):
    @pl.when(pl.program_id(1) == 0)
    def _():
        wb_ref[...] = wt_ref[...].astype(BF16)

    res = lax.dot_general(wb_ref[...], a_ref[...], _NT, preferred_element_type=F32)
    n_blocks, _, tq = o_ref.shape
    for blk in range(n_blocks):
        o_ref[blk] = res[:, blk * tq:(blk + 1) * tq].astype(o_ref.dtype)


def _matmul_ws_t(a, w_t, *, name, first_rows, tn, tq, tm):
    m, k = a.shape
    tm = min(tm, m)
    assert m % tm == 0 and tm % tq == 0 and max(first_rows) + tn <= w_t.shape[0] and w_t.shape[1] == k
    per_step = tm // tq
    return pl.pallas_call(
        _mm_ws_t_body,
        grid=(len(first_rows), m // tm),
        in_specs=[_row_window(first_rows, tn, k), pl.BlockSpec((tm, k), lambda j, i: (i, 0))],
        out_specs=pl.BlockSpec((None, per_step, tn, tq), lambda j, i: (j, i, 0, 0)),
        out_shape=jax.ShapeDtypeStruct((len(first_rows), m // tq, tn, tq), BF16),
        scratch_shapes=[pltpu.VMEM((tn, k), BF16)],
        compiler_params=_params("parallel", "arbitrary"),
        name=name,
    )(w_t, a)


def _norm_proj_body(x_ref, g_ref, w_ref, *out_refs):
    h = _rms(x_ref[...], g_ref[...]).astype(BF16)
    p_ref = out_refs[-1]
    p_ref[...] = jnp.dot(h, w_ref[...], preferred_element_type=F32).astype(p_ref.dtype)
    if len(out_refs) == 2:
        out_refs[0][...] = h


def _rmsnorm_proj(x, g, w, *, name, proj_dtype, keep_rows, tm=NORM_ROWS):
    m, d = x.shape
    n = w.shape[1]
    tm = min(tm, m)
    assert m % tm == 0
    out_specs = [pl.BlockSpec((tm, n), lambda i: (i, 0))]
    out_shape = [jax.ShapeDtypeStruct((m, n), proj_dtype)]
    if keep_rows:
        out_specs.insert(0, pl.BlockSpec((tm, d), lambda i: (i, 0)))
        out_shape.insert(0, jax.ShapeDtypeStruct((m, d), BF16))
    out = pl.pallas_call(
        _norm_proj_body,
        grid=(m // tm,),
        in_specs=[pl.BlockSpec((tm, d), lambda i: (i, 0)), pl.BlockSpec((1, d), lambda i: (0, 0)),
                  pl.BlockSpec((d, n), lambda i: (0, 0))],
        out_specs=out_specs,
        out_shape=out_shape,
        compiler_params=_params("parallel"),
        name=name,
    )(x, g.reshape(1, d).astype(F32), w)
    return out if keep_rows else out[0]


def _sb_body(q_ref, k_ref, vt_ref, tri_ref, o_ref, qs_ref, z_ref, lb_ref, wb_ref, acc_ref, *, tq):
    i = pl.program_id(1)
    shape = (tq, tq)
    before = lax.broadcasted_iota(I32, shape, 0) < lax.broadcasted_iota(I32, shape, 1)
    heads = [slice(h * HEAD_DIM, (h + 1) * HEAD_DIM) for h in range(N_HEADS)]
    qs_ref[...] = (q_ref[...].astype(F32) * (HEAD_DIM ** -0.5 * LOG2E)).astype(BF16)
    acc_ref[...] = jnp.zeros(acc_ref.shape, F32)

    def tile(j, carry, diagonal):
        start = pl.multiple_of(j * tq, tq)
        for h, hs in enumerate(heads):
            z = lax.dot_general(k_ref[pl.ds(start, tq), hs], qs_ref[:, hs], _NT,
                                preferred_element_type=F32)
            neg_z = -z
            log_keep = jnp.minimum(neg_z, 0.0) - jnp.log2(1.0 + jnp.exp2(jnp.minimum(z, neg_z)))
            if diagonal:
                log_keep = jnp.where(before, log_keep, 0.0)
            z_ref[h] = z
            lb_ref[h] = log_keep.astype(BF16)
        new_carry = []
        for h in range(N_HEADS):
            c = jnp.dot(tri_ref[...], lb_ref[h], preferred_element_type=F32) + carry[h:h + 1, :]
            w = jnp.exp2(z_ref[h] + c)
            if diagonal:
                w = jnp.where(before, w, 0.0)
            wb_ref[h] = w.astype(BF16)
            new_carry.append(c[0:1, :])
        for h, hs in enumerate(heads):
            acc_ref[h] += jnp.dot(vt_ref[j, hs, :], wb_ref[h], preferred_element_type=F32)
        return jnp.concatenate(new_carry, axis=0)

    carry = tile(i, jnp.zeros((N_HEADS, tq), F32), True)

    def cond(state):
        j, live, _ = state
        return jnp.logical_and(j >= 0, live > EXP2_UNDERFLOW)

    def body(state):
        j, _, carry = state
        carry = tile(j, carry, False)
        return j - 1, jnp.max(carry), carry

    lax.while_loop(cond, body, (i - 1, jnp.max(carry), carry))
    for h, hs in enumerate(heads):
        o_ref[:, hs] = acc_ref[h].T.astype(o_ref.dtype)


def _sb_attention(qk, v_t, batch, seq, *, q_col, k_col, v_branch, tq):
    nq = seq // tq
    width = N_HEADS * HEAD_DIM
    tri = (jnp.arange(tq)[None, :] >= jnp.arange(tq)[:, None]).astype(BF16)
    resident = pl.Buffered(1)
    return pl.pallas_call(
        functools.partial(_sb_body, tq=tq),
        grid=(batch, nq),
        in_specs=[
            pl.BlockSpec((tq, width), lambda b, i: (b * nq + i, q_col)),
            pl.BlockSpec((seq, width), lambda b, i: (b, k_col), pipeline_mode=resident),
            pl.BlockSpec((None, None, nq, width, tq), lambda b, i: (v_branch, b, 0, 0, 0),
                         pipeline_mode=resident),
            pl.BlockSpec((tq, tq), lambda b, i: (0, 0), pipeline_mode=resident),
        ],
        out_specs=pl.BlockSpec((tq, width), lambda b, i: (b * nq + i, 0)),
        out_shape=jax.ShapeDtypeStruct((batch * seq, width), BF16),
        scratch_shapes=[
            pltpu.VMEM((tq, width), BF16),
            pltpu.VMEM((N_HEADS, tq, tq), F32),
            pltpu.VMEM((N_HEADS, tq, tq), BF16),
            pltpu.VMEM((N_HEADS, tq, tq), BF16),
            pltpu.VMEM((N_HEADS, HEAD_DIM, tq), F32),
        ],
        compiler_params=_params("parallel", "arbitrary"),
        name="sb_attention",
    )(qk, qk, v_t, tri)


def _bucket_thresholds():
    nb = N_BUCKETS // 2
    max_exact = nb // 2
    span = nb - max_exact
    out = []
    for k in range(1, span):
        n = max_exact
        while n ** span * max_exact ** k < MAX_DISTANCE ** k * max_exact ** span:
            n += 1
        out.append(n)
    return max_exact, out


def _bias_body(rb_ref, o_ref, *, tq):
    nb = N_BUCKETS // 2
    max_exact, steps = _bucket_thresholds()
    shape = (2 * tq, tq)
    rel = lax.broadcasted_iota(I32, shape, 0) - lax.broadcasted_iota(I32, shape, 1) - tq
    n = jnp.abs(rel)
    large = jnp.full(shape, max_exact, I32)
    for t in steps:
        large = large + (n >= t).astype(I32)
    bucket = jnp.where(rel > 0, nb, 0) + jnp.where(n < max_exact, n, large)
    for h in range(N_HEADS):
        val = jnp.zeros(shape, F32)
        for b in range(N_BUCKETS):
            val = jnp.where(bucket == b, rb_ref[b, h], val)
        o_ref[h] = (val - rb_ref[nb - 1, h]) * LOG2E


def _near_bias(rel_bias, tq):
    return pl.pallas_call(
        functools.partial(_bias_body, tq=tq),
        in_specs=[pl.BlockSpec(memory_space=pltpu.SMEM)],
        out_specs=pl.BlockSpec(memory_space=pltpu.VMEM),
        out_shape=jax.ShapeDtypeStruct((N_HEADS, 2 * tq, tq), F32),
        compiler_params=pltpu.CompilerParams(vmem_limit_bytes=VMEM_LIMIT_BYTES),
        name="dsa_near_bias",
    )(rel_bias.astype(F32))


def _order_key(x):
    bits = lax.bitcast_convert_type(x, I32)
    return bits ^ ((bits >> 31) | SIGN_BIT)


def _order_key_to_float(key):
    return lax.bitcast_convert_type(key ^ ((~key >> 31) | SIGN_BIT), F32)


def _bit_transpose32(words):
    a = list(words)
    j, m = 16, 0x0000FFFF
    while j:
        mask = jnp.int32(m - (1 << 32) if m >= 1 << 31 else m)
        k = 0
        while k < 32:
            t = (lax.shift_right_logical(a[k], jnp.int32(j)) ^ a[k + j]) & mask
            a[k] = a[k] ^ lax.shift_left(t, jnp.int32(j))
            a[k + j] = a[k + j] ^ t
            k = (k + j + 1) & ~j
        j >>= 1
        m = (m ^ (m << j)) & 0xFFFFFFFF
    return a


def _dsa_body(qd_ref, qi_ref, wq_ref, kd_ref, vt_ref, ki_ref, bias_ref, tri_ref, o_ref,
              sc_ref, plane_ref, qs_ref, m_ref, l_ref, acc_ref, lg_ref, bmax_ref, p_ref, *, tq, top):
    i = pl.program_id(1)
    shape = (tq, tq)
    key_row = lax.broadcasted_iota(I32, shape, 0)
    qry_col = lax.broadcasted_iota(I32, shape, 1)
    visible = key_row // CHUNK <= qry_col // CHUNK

    w_t = (wq_ref[...] * (IDX_DIM ** -0.5 * IDX_HEADS ** -0.5)).T

    heads_per_vreg = LANES // IDX_DIM

    def score_tile(j):
        start = pl.multiple_of(j * tq, tq)
        ki = [ki_ref[pl.ds(start, tq), c * LANES:(c + 1) * LANES].astype(BF16)
              for c in range(heads_per_vreg)]
        s = jnp.zeros(shape, F32)
        for h in range(IDX_HEADS):
            g, c = divmod(h, heads_per_vreg)
            d = lax.dot_general(ki[c], qi_ref[:, g * LANES:(g + 1) * LANES], _NT,
                                preferred_element_type=F32)
            s = s + w_t[h:h + 1, :] * jnp.maximum(d, 0.0)
        sc_ref[j] = s

    def key_planes(j):
        ukey = _order_key(sc_ref[j])
        planes = _bit_transpose32([ukey[g * SUBLANES:(g + 1) * SUBLANES, :] for g in range(KEY_BITS)])
        for b in range(KEY_BITS):
            plane_ref[b, pl.ds(pl.multiple_of(j * SUBLANES, SUBLANES), SUBLANES), :] = planes[b]

    @pl.when(i == 0)
    def _():
        plane_ref[...] = jnp.zeros(plane_ref.shape, I32)

    def score_step(j, carry):
        key_planes(j)
        score_tile(j + 1)
        return carry

    score_tile(0)
    lax.fori_loop(0, i, score_step, 0)
    key_planes(i)
    sc_ref[i] = jnp.where(visible, sc_ref[i], -jnp.inf)

    n_rows = plane_ref.shape[1]
    block_of_row = lax.broadcasted_iota(I32, (n_rows, tq), 0) // SUBLANES
    qry_of_col = lax.broadcasted_iota(I32, (n_rows, tq), 1)
    n_bits = (qry_of_col // CHUNK + 1) * (CHUNK // SUBLANES)
    diag_bits = jnp.where(n_bits >= KEY_BITS, -1, lax.shift_left(jnp.int32(1), n_bits) - 1)
    cand0 = jnp.where(block_of_row < i, -1, jnp.where(block_of_row == i, diag_bits, 0))

    def popcount_rows(words):
        return jnp.sum(lax.population_count(words), axis=0, keepdims=True)

    def bit_step(t, state):
        cand, n_above, thr_bits = state
        b = KEY_BITS - 1 - t
        ones = cand & plane_ref[b]
        n_ones = popcount_rows(ones)
        take = n_above + n_ones >= top
        cand = jnp.where(take, ones, cand ^ ones)
        n_above = jnp.where(take, n_above, n_above + n_ones)
        thr_bits = thr_bits | jnp.where(take, lax.shift_left(jnp.int32(1), b), 0)
        return cand, n_above, thr_bits

    zero = jnp.zeros((1, tq), I32)
    cand, n_above, thr_bits = lax.fori_loop(0, KEY_BITS, bit_step, (cand0, zero, zero))
    qry = lax.broadcasted_iota(I32, (1, tq), 1)
    n_visible = i * tq + (qry // CHUNK + 1) * CHUNK
    wanted = n_visible > top
    thr = jnp.where(wanted, _order_key_to_float(thr_bits), jnp.finfo(F32).min)
    tied = jnp.logical_and(wanted, n_above + popcount_rows(cand) > top)
    c_hi = n_above

    ones = jnp.ones((2 * SUBLANES, tq), BF16)

    def plain_mask(j, carry):
        sc_ref[j] = jnp.where(sc_ref[j] >= thr, 0.0, NEG_BIG)
        return carry

    def tie_mask(j, seen):
        s = sc_ref[j]
        equal = s == thr
        rank = jnp.dot(tri_ref[...], equal.astype(BF16), preferred_element_type=F32) + seen
        quota = jnp.where(tied, (top - c_hi).astype(F32), jnp.inf)
        keep_equal = jnp.where(rank < quota, 0.0, NEG_BIG)
        sc_ref[j] = jnp.where(s > thr, 0.0, jnp.where(equal, keep_equal, NEG_BIG))
        return seen + jnp.sum(equal.astype(F32), axis=0, keepdims=True)

    def with_ties():
        lax.fori_loop(0, i + 1, tie_mask, jnp.zeros((1, tq), F32))
        return jnp.int32(0)

    def without_ties():
        return lax.fori_loop(0, i + 1, plain_mask, jnp.int32(0))

    lax.cond(jnp.max(tied.astype(I32)) > 0, with_ties, without_ties)

    qs_ref[...] = (qd_ref[...].astype(F32) * (HEAD_DIM ** -0.5 * LOG2E)).astype(BF16)
    m_ref[...] = jnp.full(m_ref.shape, NEG_BIG, F32)
    l_ref[...] = jnp.zeros(l_ref.shape, F32)
    acc_ref[...] = jnp.zeros(acc_ref.shape, F32)

    heads = [slice(h * HEAD_DIM, (h + 1) * HEAD_DIM) for h in range(N_HEADS)]

    far, prev, diag = None, 0, 1

    def logits(u, near, slot):
        j = i - u
        start = pl.multiple_of(j * tq, tq)
        mask = sc_ref[j]
        block_max = []
        for h, hs in enumerate(heads):
            lg = lax.dot_general(kd_ref[pl.ds(start, tq), hs], qs_ref[:, hs], _NT,
                                 preferred_element_type=F32)
            if near is not None:
                lg = lg + bias_ref[h, near * tq:(near + 1) * tq, :]
            lg = lg + mask
            lg_ref[slot, h] = lg
            block_max.append(jnp.max(lg, axis=0, keepdims=True))
        bmax_ref[slot] = jnp.concatenate(block_max, axis=0)

    def values(u, slot):
        j = i - u
        m_old = m_ref[...]
        m_new = jnp.maximum(m_old, bmax_ref[slot])
        alpha = jnp.exp2(m_old - m_new)
        m_ref[...] = m_new
        for h in range(N_HEADS):
            p_ref[h] = jnp.exp2(lg_ref[slot, h] - m_new[h:h + 1, :]).astype(BF16)
        denom = []
        for h, hs in enumerate(heads):
            v_ext = jnp.concatenate([vt_ref[j, hs, :], ones], axis=0)
            pv = jnp.dot(v_ext, p_ref[h], preferred_element_type=F32)
            acc_ref[h] = alpha[h:h + 1, :] * acc_ref[h] + pv[:HEAD_DIM]
            denom.append(pv[HEAD_DIM:HEAD_DIM + 1])
        l_ref[...] = alpha * l_ref[...] + jnp.concatenate(denom, axis=0)

    def even_step(u, near_a, near_b):
        logits(u - 1, near_a, 1)
        values(u, 0)
        logits(u - 2, near_b, 0)
        values(u - 1, 1)

    @pl.when(i == 0)
    def _():
        logits(0, diag, 0)
        values(0, 0)

    @pl.when(i == 1)
    def _():
        logits(1, prev, 1)

    @pl.when(jnp.logical_and(i >= 2, i % 2 == 1))
    def _():
        logits(i, far, 1)
        logits(i - 1, far, 0)
        values(i, 1)

    @pl.when(jnp.logical_and(i >= 2, i % 2 == 0))
    def _():
        logits(i, far, 0)

    def far_pair(k, carry):
        even_step(2 * (i // 2 - k), far, far)
        return carry

    lax.fori_loop(0, i // 2 - 1, far_pair, 0)

    @pl.when(i >= 2)
    def _():
        even_step(2, prev, diag)

    @pl.when(i == 1)
    def _():
        logits(0, diag, 0)
        values(1, 1)

    @pl.when(i >= 1)
    def _():
        values(0, 0)

    for h in range(N_HEADS):
        o = acc_ref[h] / l_ref[h:h + 1, :]
        o_ref[:, h * HEAD_DIM:(h + 1) * HEAD_DIM] = o.T.astype(o_ref.dtype)


def _dsa_attention(main, v_t, small, bias, batch, seq, *, qd_col, kd_col, qi_col, v_branch, tq):
    nq = seq // tq
    width = N_HEADS * HEAD_DIM
    assert IDX_HEADS * IDX_DIM == width and tq == KEY_BITS * SUBLANES and tq % CHUNK == 0
    key_copies = LANES // IDX_DIM
    top = min(TOPK_MAX, seq // 4)
    tri = (jnp.arange(tq)[None, :] < jnp.arange(tq)[:, None]).astype(BF16)
    resident = pl.Buffered(1)
    return pl.pallas_call(
        functools.partial(_dsa_body, tq=tq, top=top),
        grid=(batch, nq),
        in_specs=[
            pl.BlockSpec((tq, width), lambda b, i: (b * nq + i, qd_col)),
            pl.BlockSpec((tq, IDX_HEADS * IDX_DIM), lambda b, i: (b * nq + i, qi_col)),
            pl.BlockSpec((tq, LANES), lambda b, i: (b * nq + i, key_copies)),
            pl.BlockSpec((seq, width), lambda b, i: (b, kd_col), pipeline_mode=resident),
            pl.BlockSpec((None, None, nq, width, tq), lambda b, i: (v_branch, b, 0, 0, 0),
                         pipeline_mode=resident),
            pl.BlockSpec((seq, key_copies * LANES), lambda b, i: (b, 0), pipeline_mode=resident),
            pl.BlockSpec((N_HEADS, 2 * tq, tq), lambda b, i: (0, 0, 0), pipeline_mode=resident),
            pl.BlockSpec((tq, tq), lambda b, i: (0, 0), pipeline_mode=resident),
        ],
        out_specs=pl.BlockSpec((tq, width), lambda b, i: (b * nq + i, 0)),
        out_shape=jax.ShapeDtypeStruct((batch * seq, width), BF16),
        scratch_shapes=[
            pltpu.VMEM((nq, tq, tq), F32),
            pltpu.VMEM((KEY_BITS, nq * SUBLANES, tq), I32),
            pltpu.VMEM((tq, width), BF16),
            pltpu.VMEM((N_HEADS, tq), F32),
            pltpu.VMEM((N_HEADS, tq), F32),
            pltpu.VMEM((N_HEADS, HEAD_DIM, tq), F32),
            pltpu.VMEM((2, N_HEADS, tq, tq), F32),
            pltpu.VMEM((2, N_HEADS, tq), F32),
            pltpu.VMEM((N_HEADS, tq, tq), BF16),
        ],
        compiler_params=_params("parallel", "arbitrary"),
        name="dsa_attention",
    )(main, main, small, main, v_t, small, bias, tri)


def _merge_cross_body(osb_ref, ods_ref, wsb_ref, wds_ref, gsb_ref, gds_ref, bsb_ref, bds_ref, wo_ref,
                      x_ref, gc_ref, wq_ref, km_ref, vm_ref, wco_ref, gn_ref, o_ref, hn_ref):
    p_sb = jnp.dot(osb_ref[...], wsb_ref[...], preferred_element_type=F32)
    p_ds = jnp.dot(ods_ref[...], wds_ref[...], preferred_element_type=F32)
    g_sb = jax.nn.sigmoid(gsb_ref[...].astype(F32) + bsb_ref[...])
    g_ds = jax.nn.sigmoid(gds_ref[...].astype(F32) + bds_ref[...])
    merged = (g_sb * p_sb + g_ds * p_ds).astype(BF16)
    x1 = x_ref[...] + jnp.dot(merged, wo_ref[...], preferred_element_type=F32)

    h = _rms(x1, gc_ref[...]).astype(BF16)
    q = jnp.dot(h, wq_ref[...], preferred_element_type=F32) * HEAD_DIM ** -0.5
    q = q.astype(BF16)
    outs = []
    for hh in range(MEM_HEADS):
        hs = slice(hh * HEAD_DIM, (hh + 1) * HEAD_DIM)
        lg = lax.dot_general(q[:, hs], km_ref[:, hs], _NT, preferred_element_type=F32)
        p = jnp.exp(lg - jnp.max(lg, axis=1, keepdims=True))
        o = jnp.dot(p.astype(BF16), vm_ref[:, hs], preferred_element_type=F32)
        outs.append((o / jnp.sum(p, axis=1, keepdims=True)).astype(BF16))
    o = jnp.concatenate(outs, axis=1)
    x2 = x1 + jnp.dot(o, wco_ref[...], preferred_element_type=F32)
    o_ref[...] = x2
    hn_ref[...] = _rms(x2, gn_ref[...]).astype(hn_ref.dtype)


def _merge_cross(o_sb, o_ds, w_sb, w_ds, w_out, proj, gate_offset, b_gate, x,
                 kv, g_cross, w_cq, w_co, g_next, batch, seq, *, tm):
    m, k = o_sb.shape
    d = w_sb.shape[1]
    n_mem = kv.shape[0] // batch
    width = MEM_HEADS * HEAD_DIM
    tm = min(tm, seq)
    assert gate_offset % d == 0 and seq % tm == 0
    g = gate_offset // d
    nt = seq // tm
    b_gate = b_gate.reshape(1, 2 * d).astype(F32)
    resident = pl.Buffered(1)

    def rows(width_, col=0):
        return pl.BlockSpec((tm, width_), lambda b, i: (b * nt + i, col))

    def whole(shape, *block):
        return pl.BlockSpec(shape, lambda b, i: block or (0,) * len(shape), pipeline_mode=resident)

    return pl.pallas_call(
        _merge_cross_body,
        grid=(batch, nt),
        in_specs=[
            rows(k), rows(k), whole((k, d)), whole((k, d)),
            rows(d, g), rows(d, g + 1), whole((1, d)), whole((1, d), 0, 1),
            whole((d, d)), rows(d),
            whole((1, d)), whole((d, width)),
            pl.BlockSpec((n_mem, width), lambda b, i: (b, 0)),
            pl.BlockSpec((n_mem, width), lambda b, i: (b, 1)),
            whole((width, d)), whole((1, d)),
        ],
        out_specs=[rows(d), rows(d)],
        out_shape=[jax.ShapeDtypeStruct((m, d), F32), jax.ShapeDtypeStruct((m, d), BF16)],
        compiler_params=_params("parallel", "parallel"),
        name="merge_out_cross",
    )(o_sb, o_ds, w_sb, w_ds, proj, proj, b_gate, b_gate, w_out, x,
      g_cross.reshape(1, d).astype(F32), w_cq, kv, kv, w_co, g_next.reshape(1, d).astype(F32))


def _delayed(u, tail, shift):
    rolled = pltpu.roll(u, shift, axis=0)
    row = lax.broadcasted_iota(I32, tail.shape, 0)
    head = jnp.where(row < shift, pltpu.roll(tail, shift, axis=0), rolled[:SUBLANES])
    return jnp.concatenate([head, rolled[SUBLANES:]], axis=0)


def _ffn_up_body(h_ref, wa_ref, wv_ref, cwa_ref, cwv_ref, cba_ref, cbv_ref, o_ref,
                 wab_ref, wvb_ref, halo_ref, *, tiles_per_seq):
    i = pl.program_id(1)

    @pl.when(i == 0)
    def _():
        wab_ref[...] = wa_ref[...].astype(BF16)
        wvb_ref[...] = wv_ref[...].astype(BF16)

    h = h_ref[...]
    tm = h.shape[0]
    sequence_start = i % tiles_per_seq == 0

    def conv(wb_ref, cw_ref, cb_ref, slot):
        u = jnp.dot(h, wb_ref[...], preferred_element_type=F32)
        tail = jnp.where(sequence_start, 0.0, halo_ref[slot])
        halo_ref[slot] = u[tm - SUBLANES:, :]
        c = cb_ref[...] + cw_ref[CONV_WIDTH - 1:CONV_WIDTH, :] * u
        for tap in range(CONV_WIDTH - 1):
            c = c + cw_ref[tap:tap + 1, :] * _delayed(u, tail, CONV_WIDTH - 1 - tap)
        return c

    a = conv(wab_ref, cwa_ref, cba_ref, 0)
    val = conv(wvb_ref, cwv_ref, cbv_ref, 1)
    o_ref[...] = (jax.nn.gelu(a) * val).astype(o_ref.dtype)


def _ffn_up_gate(h, w_up, conv_w, conv_b, seq, *, tm=ROW_TILE, tn=FFN_COL_TILE):
    m, d = h.shape
    two_ff = w_up.shape[1]
    d_ff = two_ff // 2
    tm, tn = min(tm, seq), min(tn, d_ff)
    assert seq % tm == 0 and d_ff % tn == 0 and tm >= SUBLANES >= CONV_WIDTH - 1
    nf = d_ff // tn
    conv_w = conv_w.astype(F32)
    conv_b = conv_b.reshape(1, two_ff).astype(F32)
    return pl.pallas_call(
        functools.partial(_ffn_up_body, tiles_per_seq=seq // tm),
        grid=(nf, m // tm),
        in_specs=[
            pl.BlockSpec((tm, d), lambda j, i: (i, 0)),
            pl.BlockSpec((d, tn), lambda j, i: (0, j)),
            pl.BlockSpec((d, tn), lambda j, i: (0, nf + j)),
            pl.BlockSpec((CONV_WIDTH, tn), lambda j, i: (0, j)),
            pl.BlockSpec((CONV_WIDTH, tn), lambda j, i: (0, nf + j)),
            pl.BlockSpec((1, tn), lambda j, i: (0, j)),
            pl.BlockSpec((1, tn), lambda j, i: (0, nf + j)),
        ],
        out_specs=pl.BlockSpec((tm, tn), lambda j, i: (i, j)),
        out_shape=jax.ShapeDtypeStruct((m, d_ff), BF16),
        scratch_shapes=[pltpu.VMEM((d, tn), BF16), pltpu.VMEM((d, tn), BF16),
                        pltpu.VMEM((2, SUBLANES, tn), F32)],
        compiler_params=_params("parallel", "arbitrary"),
        name="ffn_up_conv_gate",
    )(h, w_up, w_up, conv_w, conv_w, conv_b, conv_b)


def _ffn_down_body(a_ref, w_ref, x_ref, g_ref, o_ref, *, final_norm):
    k = pl.program_id(1)

    @pl.when(k == 0)
    def _():
        o_ref[...] = x_ref[...]

    o_ref[...] += jnp.dot(a_ref[...], w_ref[...], preferred_element_type=F32)

    if final_norm:
        @pl.when(k == pl.num_programs(1) - 1)
        def _():
            o_ref[...] = _rms(o_ref[...], g_ref[...])


def _ffn_down(a, w, x, g_final, *, tm=ROW_TILE, tk=FFN_K_TILE):
    m, kdim = a.shape
    d = w.shape[1]
    tm, tk = min(tm, m), min(tk, kdim)
    assert m % tm == 0 and kdim % tk == 0
    final_norm = g_final is not None
    g = (g_final if final_norm else jnp.ones((d,), F32)).reshape(1, d).astype(F32)
    return pl.pallas_call(
        functools.partial(_ffn_down_body, final_norm=final_norm),
        grid=(m // tm, kdim // tk),
        in_specs=[
            pl.BlockSpec((tm, tk), lambda i, k: (i, k)),
            pl.BlockSpec((tk, d), lambda i, k: (k, 0)),
            pl.BlockSpec((tm, d), lambda i, k: (i, 0)),
            pl.BlockSpec((1, d), lambda i, k: (0, 0)),
        ],
        out_specs=pl.BlockSpec((tm, d), lambda i, k: (i, 0)),
        out_shape=jax.ShapeDtypeStruct((m, d), F32),
        compiler_params=_params("parallel", "arbitrary"),
        name="ffn_down",
    )(a, w, x, g)


def _layer(x, mem, g_mix, w_in, b_gate, w_proj_sb, w_proj_dsa, w_out, rel_bias,
           g_cross, g_mem, w_cq, w_ckv, w_co, g_ffn, w_up, conv_w, conv_b, w_down, g_final, batch, seq):
    d = x.shape[1]
    width = N_HEADS * HEAD_DIM
    idx_w = IDX_HEADS * IDX_DIM
    o_qi = 6 * width
    o_ki = o_qi + idx_w
    o_wi = o_ki + IDX_DIM
    o_g = o_wi + IDX_HEADS

    zeros = jnp.zeros((d, LANES - IDX_DIM), F32)
    w_small = jnp.concatenate([
        w_in[:, o_ki:o_wi], zeros, zeros, w_in[:, o_ki:o_wi],
        jnp.pad(w_in[:, o_wi:o_g], ((0, 0), (0, LANES - IDX_HEADS)))], axis=1).astype(BF16)
    h, small = _rmsnorm_proj(x, g_mix, w_small, name="mixer_norm_index_proj", proj_dtype=F32,
                             keep_rows=True)
    w_in_t = w_in.T
    tq = ATT_BLOCK
    nq = seq // tq
    q_sb, k_sb, v_sb, q_ds, k_ds, v_ds, q_ix = (g * width for g in range(o_ki // width))
    gate_rows = tuple(o_g + g * width for g in range(2 * d // width))
    n_gate = len(gate_rows)
    main = _matmul_ws(h, w_in_t, name="in_proj_main",
                      first_rows=gate_rows + (q_sb, k_sb, q_ds, k_ds, q_ix),
                      out_dtype=BF16, tm=PROJ_ROWS, tn=width)
    v_t = _matmul_ws_t(h, w_in_t, name="in_proj_values", first_rows=(v_sb, v_ds), tn=width, tq=tq,
                       tm=PROJ_ROWS)
    v_t = v_t.reshape(2, batch, nq, width, tq)

    o_sb = _sb_attention(main, v_t, batch, seq, q_col=n_gate, k_col=n_gate + 1, v_branch=0, tq=tq)
    bias = _near_bias(rel_bias, tq)
    o_ds = _dsa_attention(main, v_t, small, bias, batch, seq, qd_col=n_gate + 2, kd_col=n_gate + 3,
                          qi_col=n_gate + 4, v_branch=1, tq=tq)

    kv = _rmsnorm_proj(mem, g_mem, w_ckv.astype(BF16), name="mem_norm_kv_proj", proj_dtype=BF16,
                       keep_rows=False)
    x, h_ffn = _merge_cross(o_sb, o_ds, w_proj_sb.astype(BF16), w_proj_dsa.astype(BF16),
                            w_out.astype(BF16), main, 0, b_gate, x, kv, g_cross, w_cq.astype(BF16),
                            w_co.astype(BF16), g_ffn, batch, seq, tm=MERGE_ROWS)

    act = _ffn_up_gate(h_ffn, w_up, conv_w, conv_b, seq)
    return _ffn_down(act, w_down.astype(BF16), x, g_final)


def kernel(x, mem, g_mix, w_in, b_gate, w_proj_sb, w_proj_dsa, w_out, rel_bias, g_cross, g_mem,
           w_cq, w_ckv, w_co, g_ffn, w_up, conv_w, conv_b, w_down, g_final):
    batch, seq, d = x.shape
    h = x.reshape(batch * seq, d)
    mem2 = mem.reshape(batch * mem.shape[1], d)
    depth = g_mix.shape[0]
    for l in range(depth):
        h = _layer(h, mem2, g_mix[l], w_in[l], b_gate[l], w_proj_sb[l], w_proj_dsa[l], w_out[l],
                   rel_bias, g_cross[l], g_mem[l], w_cq[l], w_ckv[l], w_co[l], g_ffn[l], w_up[l],
                   conv_w[l], conv_b[l], w_down[l], g_final if l == depth - 1 else None, batch, seq)
    return h.reshape(batch, seq, d)
```

```python
import functools

import jax
import jax.numpy as jnp
from jax import lax
from jax.experimental import pallas as pl
from jax.experimental.pallas import tpu as pltpu

F32, BF16, I32 = jnp.float32, jnp.bfloat16, jnp.int32

EPS = 1e-6
HEAD_DIM = 128
N_HEADS = 8
IDX_HEADS = 16
IDX_DIM = 64
CHUNK = 64
TOPK_MAX = 256
N_BUCKETS = 32
MAX_DISTANCE = 128
MEM_HEADS = 4
CONV_WIDTH = 3

LANES = 128
SUBLANES = 8
VMEM_LIMIT_BYTES = 56 * 1024 * 1024
NEG_BIG = -1e30
EXP2_UNDERFLOW = -151.0
LOG2E = 1.4426950408889634
KEY_BITS = 32
SIGN_BIT = -(2 ** 31)

PROJ_ROWS = 2048
ROW_TILE = 1024
NORM_ROWS = 512
MERGE_ROWS = 256
FFN_COL_TILE = 512
FFN_K_TILE = 1024
ATT_BLOCK = KEY_BITS * SUBLANES

_NT = (((1,), (1,)), ((), ()))


def _params(*sem):
    return pltpu.CompilerParams(dimension_semantics=sem, vmem_limit_bytes=VMEM_LIMIT_BYTES)


def _rms(x, g):
    inv = lax.rsqrt(jnp.mean(x * x, axis=-1, keepdims=True) + EPS)
    return x * inv * g


def _mm_ws_body(a_ref, wt_ref, o_ref, wb_ref):
    @pl.when(pl.program_id(1) == 0)
    def _():
        wb_ref[...] = wt_ref[...].astype(BF16)

    o_ref[...] = lax.dot_general(a_ref[...], wb_ref[...], _NT,
                                 preferred_element_type=F32).astype(o_ref.dtype)


def _row_window(first_rows, tn, k):
    assert all(r % SUBLANES == 0 for r in first_rows)

    def index_map(j, i):
        row = jnp.int32(first_rows[0])
        for step, first in enumerate(first_rows[1:], start=1):
            row = jnp.where(j >= step, first, row)
        return pl.multiple_of(row, SUBLANES), 0

    return pl.BlockSpec((pl.Element(tn), pl.Element(k)), index_map)


def _matmul_ws(a, w_t, *, name, first_rows, out_dtype, tm, tn):
    m, k = a.shape
    tm = min(tm, m)
    assert m % tm == 0 and max(first_rows) + tn <= w_t.shape[0] and w_t.shape[1] == k
    return pl.pallas_call(
        _mm_ws_body,
        grid=(len(first_rows), m // tm),
        in_specs=[pl.BlockSpec((tm, k), lambda j, i: (i, 0)), _row_window(first_rows, tn, k)],
        out_specs=pl.BlockSpec((tm, tn), lambda j, i: (i, j)),
        out_shape=jax.ShapeDtypeStruct((m, len(first_rows) * tn), out_dtype),
        scratch_shapes=[pltpu.VMEM((tn, k), BF16)],
        compiler_params=_params("parallel", "arbitrary"),
        name=name,
    )(a, w_t)


def _mm_ws_t_body(wt_ref, a_ref, o_ref, wb_ref):
    @pl.when(pl.program_id(1) == 0)
    def _():
        wb_ref[...] = wt_ref[...].astype(BF16)

    res = lax.dot_general(wb_ref[...], a_ref[...], _NT, preferred_element_type=F32)
    n_blocks, _, tq = o_ref.shape
    for blk in range(n_blocks):
        o_ref[blk] = res[:, blk * tq:(blk + 1) * tq].astype(o_ref.dtype)


def _matmul_ws_t(a, w_t, *, name, first_rows, tn, tq, tm):
    m, k = a.shape
    tm = min(tm, m)
    assert m % tm == 0 and tm % tq == 0 and max(first_rows) + tn <= w_t.shape[0] and w_t.shape[1] == k
    per_step = tm // tq
    return pl.pallas_call(
        _mm_ws_t_body,
        grid=(len(first_rows), m // tm),
        in_specs=[_row_window(first_rows, tn, k), pl.BlockSpec((tm, k), lambda j, i: (i, 0))],
        out_specs=pl.BlockSpec((None, per_step, tn, tq), lambda j, i: (j, i, 0, 0)),
        out_shape=jax.ShapeDtypeStruct((len(first_rows), m // tq, tn, tq), BF16),
        scratch_shapes=[pltpu.VMEM((tn, k), BF16)],
        compiler_params=_params("parallel", "arbitrary"),
        name=name,
    )(w_t, a)


def _norm_proj_body(x_ref, g_ref, w_ref, *out_refs):
    h = _rms(x_ref[...], g_ref[...]).astype(BF16)
    p_ref = out_refs[-1]
    p_ref[...] = jnp.dot(h, w_ref[...], preferred_element_type=F32).astype(p_ref.dtype)
    if len(out_refs) == 2:
        out_refs[0][...] = h


def _rmsnorm_proj(x, g, w, *, name, proj_dtype, keep_rows, tm=NORM_ROWS):
    m, d = x.shape
    n = w.shape[1]
    tm = min(tm, m)
    assert m % tm == 0
    out_specs = [pl.BlockSpec((tm, n), lambda i: (i, 0))]
    out_shape = [jax.ShapeDtypeStruct((m, n), proj_dtype)]
    if keep_rows:
        out_specs.insert(0, pl.BlockSpec((tm, d), lambda i: (i, 0)))
        out_shape.insert(0, jax.ShapeDtypeStruct((m, d), BF16))
    out = pl.pallas_call(
        _norm_proj_body,
        grid=(m // tm,),
        in_specs=[pl.BlockSpec((tm, d), lambda i: (i, 0)), pl.BlockSpec((1, d), lambda i: (0, 0)),
                  pl.BlockSpec((d, n), lambda i: (0, 0))],
        out_specs=out_specs,
        out_shape=out_shape,
        compiler_params=_params("parallel"),
        name=name,
    )(x, g.reshape(1, d).astype(F32), w)
    return out if keep_rows else out[0]


def _sb_body(q_ref, k_ref, vt_ref, tri_ref, o_ref, qs_ref, z_ref, lb_ref, wb_ref, acc_ref, *, tq):
    i = pl.program_id(1)
    shape = (tq, tq)
    before = lax.broadcasted_iota(I32, shape, 0) < lax.broadcasted_iota(I32, shape, 1)
    heads = [slice(h * HEAD_DIM, (h + 1) * HEAD_DIM) for h in range(N_HEADS)]
    qs_ref[...] = (q_ref[...].astype(F32) * (HEAD_DIM ** -0.5 * LOG2E)).astype(BF16)
    acc_ref[...] = jnp.zeros(acc_ref.shape, F32)

    def tile(j, carry, diagonal):
        start = pl.multiple_of(j * tq, tq)
        for h, hs in enumerate(heads):
            z = lax.dot_general(k_ref[pl.ds(start, tq), hs], qs_ref[:, hs], _NT,
                                preferred_element_type=F32)
            neg_z = -z
            log_keep = jnp.minimum(neg_z, 0.0) - jnp.log2(1.0 + jnp.exp2(jnp.minimum(z, neg_z)))
            if diagonal:
                log_keep = jnp.where(before, log_keep, 0.0)
            z_ref[h] = z
            lb_ref[h] = log_keep.astype(BF16)
        new_carry = []
        for h in range(N_HEADS):
            c = jnp.dot(tri_ref[...], lb_ref[h], preferred_element_type=F32) + carry[h:h + 1, :]
            w = jnp.exp2(z_ref[h] + c)
            if diagonal:
                w = jnp.where(before, w, 0.0)
            wb_ref[h] = w.astype(BF16)
            new_carry.append(c[0:1, :])
        for h, hs in enumerate(heads):
            acc_ref[h] += jnp.dot(vt_ref[j, hs, :], wb_ref[h], preferred_element_type=F32)
        return jnp.concatenate(new_carry, axis=0)

    carry = tile(i, jnp.zeros((N_HEADS, tq), F32), True)

    def cond(state):
        j, live, _ = state
        return jnp.logical_and(j >= 0, live > EXP2_UNDERFLOW)

    def body(state):
        j, _, carry = state
        carry = tile(j, carry, False)
        return j - 1, jnp.max(carry), carry

    lax.while_loop(cond, body, (i - 1, jnp.max(carry), carry))
    for h, hs in enumerate(heads):
        o_ref[:, hs] = acc_ref[h].T.astype(o_ref.dtype)


def _sb_attention(qk, v_t, batch, seq, *, q_col, k_col, v_branch, tq):
    nq = seq // tq
    width = N_HEADS * HEAD_DIM
    tri = (jnp.arange(tq)[None, :] >= jnp.arange(tq)[:, None]).astype(BF16)
    resident = pl.Buffered(1)
    return pl.pallas_call(
        functools.partial(_sb_body, tq=tq),
        grid=(batch, nq),
        in_specs=[
            pl.BlockSpec((tq, width), lambda b, i: (b * nq + i, q_col)),
            pl.BlockSpec((seq, width), lambda b, i: (b, k_col), pipeline_mode=resident),
            pl.BlockSpec((None, None, nq, width, tq), lambda b, i: (v_branch, b, 0, 0, 0),
                         pipeline_mode=resident),
            pl.BlockSpec((tq, tq), lambda b, i: (0, 0), pipeline_mode=resident),
        ],
        out_specs=pl.BlockSpec((tq, width), lambda b, i: (b * nq + i, 0)),
        out_shape=jax.ShapeDtypeStruct((batch * seq, width), BF16),
        scratch_shapes=[
            pltpu.VMEM((tq, width), BF16),
            pltpu.VMEM((N_HEADS, tq, tq), F32),
            pltpu.VMEM((N_HEADS, tq, tq), BF16),
            pltpu.VMEM((N_HEADS, tq, tq), BF16),
            pltpu.VMEM((N_HEADS, HEAD_DIM, tq), F32),
        ],
        compiler_params=_params("parallel", "arbitrary"),
        name="sb_attention",
    )(qk, qk, v_t, tri)


def _bucket_thresholds():
    nb = N_BUCKETS // 2
    max_exact = nb // 2
    span = nb - max_exact
    out = []
    for k in range(1, span):
        n = max_exact
        while n ** span * max_exact ** k < MAX_DISTANCE ** k * max_exact ** span:
            n += 1
        out.append(n)
    return max_exact, out


def _bias_body(rb_ref, o_ref, *, tq):
    nb = N_BUCKETS // 2
    max_exact, steps = _bucket_thresholds()
    shape = (2 * tq, tq)
    rel = lax.broadcasted_iota(I32, shape, 0) - lax.broadcasted_iota(I32, shape, 1) - tq
    n = jnp.abs(rel)
    large = jnp.full(shape, max_exact, I32)
    for t in steps:
        large = large + (n >= t).astype(I32)
    bucket = jnp.where(rel > 0, nb, 0) + jnp.where(n < max_exact, n, large)
    for h in range(N_HEADS):
        val = jnp.zeros(shape, F32)
        for b in range(N_BUCKETS):
            val = jnp.where(bucket == b, rb_ref[b, h], val)
        o_ref[h] = (val - rb_ref[nb - 1, h]) * LOG2E


def _near_bias(rel_bias, tq):
    return pl.pallas_call(
        functools.partial(_bias_body, tq=tq),
        in_specs=[pl.BlockSpec(memory_space=pltpu.SMEM)],
        out_specs=pl.BlockSpec(memory_space=pltpu.VMEM),
        out_shape=jax.ShapeDtypeStruct((N_HEADS, 2 * tq, tq), F32),
        compiler_params=pltpu.CompilerParams(vmem_limit_bytes=VMEM_LIMIT_BYTES),
        name="dsa_near_bias",
    )(rel_bias.astype(F32))


def _order_key(x):
    bits = lax.bitcast_convert_type(x, I32)
    return bits ^ ((bits >> 31) | SIGN_BIT)


def _order_key_to_float(key):
    return lax.bitcast_convert_type(key ^ ((~key >> 31) | SIGN_BIT), F32)


def _bit_transpose32(words):
    a = list(words)
    j, m = 16, 0x0000FFFF
    while j:
        mask = jnp.int32(m - (1 << 32) if m >= 1 << 31 else m)
        k = 0
        while k < 32:
            t = (lax.shift_right_logical(a[k], jnp.int32(j)) ^ a[k + j]) & mask
            a[k] = a[k] ^ lax.shift_left(t, jnp.int32(j))
            a[k + j] = a[k + j] ^ t
            k = (k + j + 1) & ~j
        j >>= 1
        m = (m ^ (m << j)) & 0xFFFFFFFF
    return a


def _dsa_body(qd_ref, qi_ref, wq_ref, kd_ref, vt_ref, ki_ref, bias_ref, tri_ref, o_ref,
              sc_ref, plane_ref, qs_ref, m_ref, l_ref, acc_ref, lg_ref, bmax_ref, *, tq, top):
    i = pl.program_id(1)
    shape = (tq, tq)
    key_row = lax.broadcasted_iota(I32, shape, 0)
    qry_col = lax.broadcasted_iota(I32, shape, 1)
    visible = key_row // CHUNK <= qry_col // CHUNK

    w_t = (wq_ref[...] * (IDX_DIM ** -0.5 * IDX_HEADS ** -0.5)).T

    heads_per_vreg = LANES // IDX_DIM

    def score_tile(j):
        start = pl.multiple_of(j * tq, tq)
        ki = [ki_ref[pl.ds(start, tq), c * LANES:(c + 1) * LANES].astype(BF16)
              for c in range(heads_per_vreg)]
        s = jnp.zeros(shape, F32)
        for h in range(IDX_HEADS):
            g, c = divmod(h, heads_per_vreg)
            d = lax.dot_general(ki[c], qi_ref[:, g * LANES:(g + 1) * LANES], _NT,
                                preferred_element_type=F32)
            s = s + w_t[h:h + 1, :] * jnp.maximum(d, 0.0)
        sc_ref[j] = s

    def key_planes(j):
        ukey = _order_key(sc_ref[j])
        planes = _bit_transpose32([ukey[g * SUBLANES:(g + 1) * SUBLANES, :] for g in range(KEY_BITS)])
        for b in range(KEY_BITS):
            plane_ref[b, pl.ds(pl.multiple_of(j * SUBLANES, SUBLANES), SUBLANES), :] = planes[b]

    @pl.when(i == 0)
    def _():
        plane_ref[...] = jnp.zeros(plane_ref.shape, I32)

    def score_step(j, carry):
        key_planes(j)
        score_tile(j + 1)
        return carry

    score_tile(0)
    lax.fori_loop(0, i, score_step, 0)
    key_planes(i)
    sc_ref[i] = jnp.where(visible, sc_ref[i], -jnp.inf)

    n_rows = plane_ref.shape[1]
    block_of_row = lax.broadcasted_iota(I32, (n_rows, tq), 0) // SUBLANES
    qry_of_col = lax.broadcasted_iota(I32, (n_rows, tq), 1)
    n_bits = (qry_of_col // CHUNK + 1) * (CHUNK // SUBLANES)
    diag_bits = jnp.where(n_bits >= KEY_BITS, -1, lax.shift_left(jnp.int32(1), n_bits) - 1)
    cand0 = jnp.where(block_of_row < i, -1, jnp.where(block_of_row == i, diag_bits, 0))

    def popcount_rows(words):
        return jnp.sum(lax.population_count(words), axis=0, keepdims=True)

    def bit_step(t, state):
        cand, n_above, thr_bits = state
        b = KEY_BITS - 1 - t
        ones = cand & plane_ref[b]
        n_ones = popcount_rows(ones)
        take = n_above + n_ones >= top
        cand = jnp.where(take, ones, cand ^ ones)
        n_above = jnp.where(take, n_above, n_above + n_ones)
        thr_bits = thr_bits | jnp.where(take, lax.shift_left(jnp.int32(1), b), 0)
        return cand, n_above, thr_bits

    zero = jnp.zeros((1, tq), I32)
    cand, n_above, thr_bits = lax.fori_loop(0, KEY_BITS, bit_step, (cand0, zero, zero))
    qry = lax.broadcasted_iota(I32, (1, tq), 1)
    n_visible = i * tq + (qry // CHUNK + 1) * CHUNK
    wanted = n_visible > top
    thr = jnp.where(wanted, _order_key_to_float(thr_bits), jnp.finfo(F32).min)
    tied = jnp.logical_and(wanted, n_above + popcount_rows(cand) > top)
    c_hi = n_above

    ones = jnp.ones((2 * SUBLANES, tq), BF16)

    def plain_mask(j, carry):
        sc_ref[j] = jnp.where(sc_ref[j] >= thr, 0.0, NEG_BIG)
        return carry

    def tie_mask(j, seen):
        s = sc_ref[j]
        equal = s == thr
        rank = jnp.dot(tri_ref[...], equal.astype(BF16), preferred_element_type=F32) + seen
        quota = jnp.where(tied, (top - c_hi).astype(F32), jnp.inf)
        keep_equal = jnp.where(rank < quota, 0.0, NEG_BIG)
        sc_ref[j] = jnp.where(s > thr, 0.0, jnp.where(equal, keep_equal, NEG_BIG))
        return seen + jnp.sum(equal.astype(F32), axis=0, keepdims=True)

    def with_ties():
        lax.fori_loop(0, i + 1, tie_mask, jnp.zeros((1, tq), F32))
        return jnp.int32(0)

    def without_ties():
        return lax.fori_loop(0, i + 1, plain_mask, jnp.int32(0))

    lax.cond(jnp.max(tied.astype(I32)) > 0, with_ties, without_ties)

    qs_ref[...] = (qd_ref[...].astype(F32) * (HEAD_DIM ** -0.5 * LOG2E)).astype(BF16)
    m_ref[...] = jnp.full(m_ref.shape, NEG_BIG, F32)
    l_ref[...] = jnp.zeros(l_ref.shape, F32)
    acc_ref[...] = jnp.zeros(acc_ref.shape, F32)

    heads = [slice(h * HEAD_DIM, (h + 1) * HEAD_DIM) for h in range(N_HEADS)]

    far, prev, diag = None, 0, 1

    def logits(u, near, slot):
        j = i - u
        start = pl.multiple_of(j * tq, tq)
        mask = sc_ref[j]
        block_max = []
        for h, hs in enumerate(heads):
            lg = lax.dot_general(kd_ref[pl.ds(start, tq), hs], qs_ref[:, hs], _NT,
                                 preferred_element_type=F32)
            if near is not None:
                lg = lg + bias_ref[h, near * tq:(near + 1) * tq, :]
            lg = lg + mask
            lg_ref[slot, h] = lg
            block_max.append(jnp.max(lg, axis=0, keepdims=True))
        bmax_ref[slot] = jnp.concatenate(block_max, axis=0)

    def values(u, slot):
        j = i - u
        m_old = m_ref[...]
        m_new = jnp.maximum(m_old, bmax_ref[slot])
        alpha = jnp.exp2(m_old - m_new)
        m_ref[...] = m_new
        denom = []
        for h, hs in enumerate(heads):
            p = jnp.exp2(lg_ref[slot, h] - m_new[h:h + 1, :]).astype(BF16)
            v_ext = jnp.concatenate([vt_ref[j, hs, :], ones], axis=0)
            pv = jnp.dot(v_ext, p, preferred_element_type=F32)
            acc_ref[h] = alpha[h:h + 1, :] * acc_ref[h] + pv[:HEAD_DIM]
            denom.append(pv[HEAD_DIM:HEAD_DIM + 1])
        l_ref[...] = alpha * l_ref[...] + jnp.concatenate(denom, axis=0)

    def even_step(u, near_a, near_b):
        logits(u - 1, near_a, 1)
        values(u, 0)
        logits(u - 2, near_b, 0)
        values(u - 1, 1)

    @pl.when(i == 0)
    def _():
        logits(0, diag, 0)
        values(0, 0)

    @pl.when(i == 1)
    def _():
        logits(1, prev, 1)

    @pl.when(jnp.logical_and(i >= 2, i % 2 == 1))
    def _():
        logits(i, far, 1)
        logits(i - 1, far, 0)
        values(i, 1)

    @pl.when(jnp.logical_and(i >= 2, i % 2 == 0))
    def _():
        logits(i, far, 0)

    def far_pair(k, carry):
        even_step(2 * (i // 2 - k), far, far)
        return carry

    lax.fori_loop(0, i // 2 - 1, far_pair, 0)

    @pl.when(i >= 2)
    def _():
        even_step(2, prev, diag)

    @pl.when(i == 1)
    def _():
        logits(0, diag, 0)
        values(1, 1)

    @pl.when(i >= 1)
    def _():
        values(0, 0)

    for h in range(N_HEADS):
        o = acc_ref[h] / l_ref[h:h + 1, :]
        o_ref[:, h * HEAD_DIM:(h + 1) * HEAD_DIM] = o.T.astype(o_ref.dtype)


def _dsa_attention(main, v_t, small, bias, batch, seq, *, qd_col, kd_col, qi_col, v_branch, tq):
    nq = seq // tq
    width = N_HEADS * HEAD_DIM
    assert IDX_HEADS * IDX_DIM == width and tq == KEY_BITS * SUBLANES and tq % CHUNK == 0
    key_copies = LANES // IDX_DIM
    top = min(TOPK_MAX, seq // 4)
    tri = (jnp.arange(tq)[None, :] < jnp.arange(tq)[:, None]).astype(BF16)
    resident = pl.Buffered(1)
    return pl.pallas_call(
        functools.partial(_dsa_body, tq=tq, top=top),
        grid=(batch, nq),
        in_specs=[
            pl.BlockSpec((tq, width), lambda b, i: (b * nq + i, qd_col)),
            pl.BlockSpec((tq, IDX_HEADS * IDX_DIM), lambda b, i: (b * nq + i, qi_col)),
            pl.BlockSpec((tq, LANES), lambda b, i: (b * nq + i, key_copies)),
            pl.BlockSpec((seq, width), lambda b, i: (b, kd_col), pipeline_mode=resident),
            pl.BlockSpec((None, None, nq, width, tq), lambda b, i: (v_branch, b, 0, 0, 0),
                         pipeline_mode=resident),
            pl.BlockSpec((seq, key_copies * LANES), lambda b, i: (b, 0), pipeline_mode=resident),
            pl.BlockSpec((N_HEADS, 2 * tq, tq), lambda b, i: (0, 0, 0), pipeline_mode=resident),
            pl.BlockSpec((tq, tq), lambda b, i: (0, 0), pipeline_mode=resident),
        ],
        out_specs=pl.BlockSpec((tq, width), lambda b, i: (b * nq + i, 0)),
        out_shape=jax.ShapeDtypeStruct((batch * seq, width), BF16),
        scratch_shapes=[
            pltpu.VMEM((nq, tq, tq), F32),
            pltpu.VMEM((KEY_BITS, nq * SUBLANES, tq), I32),
            pltpu.VMEM((tq, width), BF16),
            pltpu.VMEM((N_HEADS, tq), F32),
            pltpu.VMEM((N_HEADS, tq), F32),
            pltpu.VMEM((N_HEADS, HEAD_DIM, tq), F32),
            pltpu.VMEM((2, N_HEADS, tq, tq), F32),
            pltpu.VMEM((2, N_HEADS, tq), F32),
        ],
        compiler_params=_params("parallel", "arbitrary"),
        name="dsa_attention",
    )(main, main, small, main, v_t, small, bias, tri)


def _merge_cross_body(osb_ref, ods_ref, wsb_ref, wds_ref, gsb_ref, gds_ref, bsb_ref, bds_ref, wo_ref,
                      x_ref, gc_ref, wq_ref, km_ref, vm_ref, wco_ref, gn_ref, o_ref, hn_ref):
    p_sb = jnp.dot(osb_ref[...], wsb_ref[...], preferred_element_type=F32)
    p_ds = jnp.dot(ods_ref[...], wds_ref[...], preferred_element_type=F32)
    g_sb = jax.nn.sigmoid(gsb_ref[...].astype(F32) + bsb_ref[...])
    g_ds = jax.nn.sigmoid(gds_ref[...].astype(F32) + bds_ref[...])
    merged = (g_sb * p_sb + g_ds * p_ds).astype(BF16)
    x1 = x_ref[...] + jnp.dot(merged, wo_ref[...], preferred_element_type=F32)

    h = _rms(x1, gc_ref[...]).astype(BF16)
    q = jnp.dot(h, wq_ref[...], preferred_element_type=F32) * HEAD_DIM ** -0.5
    q = q.astype(BF16)
    outs = []
    for hh in range(MEM_HEADS):
        hs = slice(hh * HEAD_DIM, (hh + 1) * HEAD_DIM)
        lg = lax.dot_general(q[:, hs], km_ref[:, hs], _NT, preferred_element_type=F32)
        p = jnp.exp(lg - jnp.max(lg, axis=1, keepdims=True))
        o = jnp.dot(p.astype(BF16), vm_ref[:, hs], preferred_element_type=F32)
        outs.append((o / jnp.sum(p, axis=1, keepdims=True)).astype(BF16))
    o = jnp.concatenate(outs, axis=1)
    x2 = x1 + jnp.dot(o, wco_ref[...], preferred_element_type=F32)
    o_ref[...] = x2
    hn_ref[...] = _rms(x2, gn_ref[...]).astype(hn_ref.dtype)


def _merge_cross(o_sb, o_ds, w_sb, w_ds, w_out, proj, gate_offset, b_gate, x,
                 kv, g_cross, w_cq, w_co, g_next, batch, seq, *, tm):
    m, k = o_sb.shape
    d = w_sb.shape[1]
    n_mem = kv.shape[0] // batch
    width = MEM_HEADS * HEAD_DIM
    tm = min(tm, seq)
    assert gate_offset % d == 0 and seq % tm == 0
    g = gate_offset // d
    nt = seq // tm
    b_gate = b_gate.reshape(1, 2 * d).astype(F32)
    resident = pl.Buffered(1)

    def rows(width_, col=0):
        return pl.BlockSpec((tm, width_), lambda b, i: (b * nt + i, col))

    def whole(shape, *block):
        return pl.BlockSpec(shape, lambda b, i: block or (0,) * len(shape), pipeline_mode=resident)

    return pl.pallas_call(
        _merge_cross_body,
        grid=(batch, nt),
        in_specs=[
            rows(k), rows(k), whole((k, d)), whole((k, d)),
            rows(d, g), rows(d, g + 1), whole((1, d)), whole((1, d), 0, 1),
            whole((d, d)), rows(d),
            whole((1, d)), whole((d, width)),
            pl.BlockSpec((n_mem, width), lambda b, i: (b, 0)),
            pl.BlockSpec((n_mem, width), lambda b, i: (b, 1)),
            whole((width, d)), whole((1, d)),
        ],
        out_specs=[rows(d), rows(d)],
        out_shape=[jax.ShapeDtypeStruct((m, d), F32), jax.ShapeDtypeStruct((m, d), BF16)],
        compiler_params=_params("parallel", "parallel"),
        name="merge_out_cross",
    )(o_sb, o_ds, w_sb, w_ds, proj, proj, b_gate, b_gate, w_out, x,
      g_cross.reshape(1, d).astype(F32), w_cq, kv, kv, w_co, g_next.reshape(1, d).astype(F32))


def _delayed(u, tail, shift):
    rolled = pltpu.roll(u, shift, axis=0)
    row = lax.broadcasted_iota(I32, tail.shape, 0)
    head = jnp.where(row < shift, pltpu.roll(tail, shift, axis=0), rolled[:SUBLANES])
    return jnp.concatenate([head, rolled[SUBLANES:]], axis=0)


def _ffn_up_body(h_ref, wa_ref, wv_ref, cwa_ref, cwv_ref, cba_ref, cbv_ref, o_ref,
                 wab_ref, wvb_ref, halo_ref, *, tiles_per_seq):
    i = pl.program_id(1)

    @pl.when(i == 0)
    def _():
        wab_ref[...] = wa_ref[...].astype(BF16)
        wvb_ref[...] = wv_ref[...].astype(BF16)

    h = h_ref[...]
    tm = h.shape[0]
    sequence_start = i % tiles_per_seq == 0

    def conv(wb_ref, cw_ref, cb_ref, slot):
        u = jnp.dot(h, wb_ref[...], preferred_element_type=F32)
        tail = jnp.where(sequence_start, 0.0, halo_ref[slot])
        halo_ref[slot] = u[tm - SUBLANES:, :]
        c = cb_ref[...] + cw_ref[CONV_WIDTH - 1:CONV_WIDTH, :] * u
        for tap in range(CONV_WIDTH - 1):
            c = c + cw_ref[tap:tap + 1, :] * _delayed(u, tail, CONV_WIDTH - 1 - tap)
        return c

    a = conv(wab_ref, cwa_ref, cba_ref, 0)
    val = conv(wvb_ref, cwv_ref, cbv_ref, 1)
    o_ref[...] = (jax.nn.gelu(a) * val).astype(o_ref.dtype)


def _ffn_up_gate(h, w_up, conv_w, conv_b, seq, *, tm=ROW_TILE, tn=FFN_COL_TILE):
    m, d = h.shape
    two_ff = w_up.shape[1]
    d_ff = two_ff // 2
    tm, tn = min(tm, seq), min(tn, d_ff)
    assert seq % tm == 0 and d_ff % tn == 0 and tm >= SUBLANES >= CONV_WIDTH - 1
    nf = d_ff // tn
    conv_w = conv_w.astype(F32)
    conv_b = conv_b.reshape(1, two_ff).astype(F32)
    return pl.pallas_call(
        functools.partial(_ffn_up_body, tiles_per_seq=seq // tm),
        grid=(nf, m // tm),
        in_specs=[
            pl.BlockSpec((tm, d), lambda j, i: (i, 0)),
            pl.BlockSpec((d, tn), lambda j, i: (0, j)),
            pl.BlockSpec((d, tn), lambda j, i: (0, nf + j)),
            pl.BlockSpec((CONV_WIDTH, tn), lambda j, i: (0, j)),
            pl.BlockSpec((CONV_WIDTH, tn), lambda j, i: (0, nf + j)),
            pl.BlockSpec((1, tn), lambda j, i: (0, j)),
            pl.BlockSpec((1, tn), lambda j, i: (0, nf + j)),
        ],
        out_specs=pl.BlockSpec((tm, tn), lambda j, i: (i, j)),
        out_shape=jax.ShapeDtypeStruct((m, d_ff), BF16),
        scratch_shapes=[pltpu.VMEM((d, tn), BF16), pltpu.VMEM((d, tn), BF16),
                        pltpu.VMEM((2, SUBLANES, tn), F32)],
        compiler_params=_params("parallel", "arbitrary"),
        name="ffn_up_conv_gate",
    )(h, w_up, w_up, conv_w, conv_w, conv_b, conv_b)


def _ffn_down_body(a_ref, w_ref, x_ref, g_ref, o_ref, *, final_norm):
    k = pl.program_id(1)

    @pl.when(k == 0)
    def _():
        o_ref[...] = x_ref[...]

    o_ref[...] += jnp.dot(a_ref[...], w_ref[...], preferred_element_type=F32)

    if final_norm:
        @pl.when(k == pl.num_programs(1) - 1)
        def _():
            o_ref[...] = _rms(o_ref[...], g_ref[...])


def _ffn_down(a, w, x, g_final, *, tm=ROW_TILE, tk=FFN_K_TILE):
    m, kdim = a.shape
    d = w.shape[1]
    tm, tk = min(tm, m), min(tk, kdim)
    assert m % tm == 0 and kdim % tk == 0
    final_norm = g_final is not None
    g = (g_final if final_norm else jnp.ones((d,), F32)).reshape(1, d).astype(F32)
    return pl.pallas_call(
        functools.partial(_ffn_down_body, final_norm=final_norm),
        grid=(m // tm, kdim // tk),
        in_specs=[
            pl.BlockSpec((tm, tk), lambda i, k: (i, k)),
            pl.BlockSpec((tk, d), lambda i, k: (k, 0)),
            pl.BlockSpec((tm, d), lambda i, k: (i, 0)),
            pl.BlockSpec((1, d), lambda i, k: (0, 0)),
        ],
        out_specs=pl.BlockSpec((tm, d), lambda i, k: (i, 0)),
        out_shape=jax.ShapeDtypeStruct((m, d), F32),
        compiler_params=_params("parallel", "arbitrary"),
        name="ffn_down",
    )(a, w, x, g)


def _layer(x, mem, g_mix, w_in, b_gate, w_proj_sb, w_proj_dsa, w_out, rel_bias,
           g_cross, g_mem, w_cq, w_ckv, w_co, g_ffn, w_up, conv_w, conv_b, w_down, g_final, batch, seq):
    d = x.shape[1]
    width = N_HEADS * HEAD_DIM
    idx_w = IDX_HEADS * IDX_DIM
    o_qi = 6 * width
    o_ki = o_qi + idx_w
    o_wi = o_ki + IDX_DIM
    o_g = o_wi + IDX_HEADS

    zeros = jnp.zeros((d, LANES - IDX_DIM), F32)
    w_small = jnp.concatenate([
        w_in[:, o_ki:o_wi], zeros, zeros, w_in[:, o_ki:o_wi],
        jnp.pad(w_in[:, o_wi:o_g], ((0, 0), (0, LANES - IDX_HEADS)))], axis=1).astype(BF16)
    h, small = _rmsnorm_proj(x, g_mix, w_small, name="mixer_norm_index_proj", proj_dtype=F32,
                             keep_rows=True)
    w_in_t = w_in.T
    tq = ATT_BLOCK
    nq = seq // tq
    q_sb, k_sb, v_sb, q_ds, k_ds, v_ds, q_ix = (g * width for g in range(o_ki // width))
    gate_rows = tuple(o_g + g * width for g in range(2 * d // width))
    n_gate = len(gate_rows)
    main = _matmul_ws(h, w_in_t, name="in_proj_main",
                      first_rows=gate_rows + (q_sb, k_sb, q_ds, k_ds, q_ix),
                      out_dtype=BF16, tm=PROJ_ROWS, tn=width)
    v_t = _matmul_ws_t(h, w_in_t, name="in_proj_values", first_rows=(v_sb, v_ds), tn=width, tq=tq,
                       tm=PROJ_ROWS)
    v_t = v_t.reshape(2, batch, nq, width, tq)

    o_sb = _sb_attention(main, v_t, batch, seq, q_col=n_gate, k_col=n_gate + 1, v_branch=0, tq=tq)
    bias = _near_bias(rel_bias, tq)
    o_ds = _dsa_attention(main, v_t, small, bias, batch, seq, qd_col=n_gate + 2, kd_col=n_gate + 3,
                          qi_col=n_gate + 4, v_branch=1, tq=tq)

    kv = _rmsnorm_proj(mem, g_mem, w_ckv.astype(BF16), name="mem_norm_kv_proj", proj_dtype=BF16,
                       keep_rows=False)
    x, h_ffn = _merge_cross(o_sb, o_ds, w_proj_sb.astype(BF16), w_proj_dsa.astype(BF16),
                            w_out.astype(BF16), main, 0, b_gate, x, kv, g_cross, w_cq.astype(BF16),
                            w_co.astype(BF16), g_ffn, batch, seq, tm=MERGE_ROWS)

    act = _ffn_up_gate(h_ffn, w_up, conv_w, conv_b, seq)
    return _ffn_down(act, w_down.astype(BF16), x, g_final)


def kernel(x, mem, g_mix, w_in, b_gate, w_proj_sb, w_proj_dsa, w_out, rel_bias, g_cross, g_mem,
           w_cq, w_ckv, w_co, g_ffn, w_up, conv_w, conv_b, w_down, g_final):
    batch, seq, d = x.shape
    h = x.reshape(batch * seq, d)
    mem2 = mem.reshape(batch * mem.shape[1], d)
    depth = g_mix.shape[0]
    for l in range(depth):
        h = _layer(h, mem2, g_mix[l], w_in[l], b_gate[l], w_proj_sb[l], w_proj_dsa[l], w_out[l],
                   rel_bias, g_cross[l], g_mem[l], w_cq[l], w_ckv[l], w_co[l], g_ffn[l], w_up[l],
                   conv_w[l], conv_b[l], w_down[l], g_final if l == depth - 1 else None, batch, seq)
    return h.reshape(batch, seq, d)
```

```python
import functools

import jax
import jax.numpy as jnp
from jax import lax
from jax.experimental import pallas as pl
from jax.experimental.pallas import tpu as pltpu

F32, BF16, I32 = jnp.float32, jnp.bfloat16, jnp.int32

EPS = 1e-6
HEAD_DIM = 128
N_HEADS = 8
IDX_HEADS = 16
IDX_DIM = 64
CHUNK = 64
TOPK_MAX = 256
N_BUCKETS = 32
MAX_DISTANCE = 128
MEM_HEADS = 4
CONV_WIDTH = 3

LANES = 128
SUBLANES = 8
VMEM_LIMIT_BYTES = 56 * 1024 * 1024
NEG_BIG = -1e30
EXP2_UNDERFLOW = -151.0
LOG2E = 1.4426950408889634
KEY_BITS = 32
SIGN_BIT = -(2 ** 31)

PROJ_ROWS = 2048
ROW_TILE = 1024
NORM_ROWS = 512
MERGE_ROWS = 256
FFN_COL_TILE = 512
FFN_K_TILE = 1024
ATT_BLOCK = KEY_BITS * SUBLANES

_NT = (((1,), (1,)), ((), ()))


def _params(*sem):
    return pltpu.CompilerParams(dimension_semantics=sem, vmem_limit_bytes=VMEM_LIMIT_BYTES)


def _rms(x, g):
    inv = lax.rsqrt(jnp.mean(x * x, axis=-1, keepdims=True) + EPS)
    return x * inv * g


def _rounding_riders(weights, n_steps, step_index):
    in_specs, out_specs, out_shape = [], [], []
    for w in weights:
        rows, cols = w.shape
        assert rows % n_steps == 0 and (rows // n_steps) % (2 * SUBLANES) == 0
        for specs in (in_specs, out_specs):
            specs.append(pl.BlockSpec((rows // n_steps, cols), lambda *ids: (step_index(*ids), 0)))
        out_shape.append(jax.ShapeDtypeStruct(w.shape, BF16))
    return in_specs, out_specs, out_shape


def _round_chunks(src_refs, dst_refs):
    for src, dst in zip(src_refs, dst_refs):
        dst[...] = src[...].astype(BF16)


def _mm_ws_body(a_ref, wt_ref, o_ref, wb_ref):
    @pl.when(pl.program_id(1) == 0)
    def _():
        wb_ref[...] = wt_ref[...].astype(BF16)

    o_ref[...] = lax.dot_general(a_ref[...], wb_ref[...], _NT,
                                 preferred_element_type=F32).astype(o_ref.dtype)


def _row_window(first_rows, tn, k):
    assert all(r % SUBLANES == 0 for r in first_rows)

    def index_map(j, i):
        row = jnp.int32(first_rows[0])
        for step, first in enumerate(first_rows[1:], start=1):
            row = jnp.where(j >= step, first, row)
        return pl.multiple_of(row, SUBLANES), 0

    return pl.BlockSpec((pl.Element(tn), pl.Element(k)), index_map)


def _matmul_ws(a, w_t, *, name, first_rows, out_dtype, tm, tn):
    m, k = a.shape
    tm = min(tm, m)
    assert m % tm == 0 and max(first_rows) + tn <= w_t.shape[0] and w_t.shape[1] == k
    return pl.pallas_call(
        _mm_ws_body,
        grid=(len(first_rows), m // tm),
        in_specs=[pl.BlockSpec((tm, k), lambda j, i: (i, 0)), _row_window(first_rows, tn, k)],
        out_specs=pl.BlockSpec((tm, tn), lambda j, i: (i, j)),
        out_shape=jax.ShapeDtypeStruct((m, len(first_rows) * tn), out_dtype),
        scratch_shapes=[pltpu.VMEM((tn, k), BF16)],
        compiler_params=_params("parallel", "arbitrary"),
        name=name,
    )(a, w_t)


def _mm_ws_t_body(wt_ref, a_ref, o_ref, wb_ref):
    @pl.when(pl.program_id(1) == 0)
    def _():
        wb_ref[...] = wt_ref[...].astype(BF16)

    res = lax.dot_general(wb_ref[...], a_ref[...], _NT, preferred_element_type=F32)
    n_blocks, _, tq = o_ref.shape
    for blk in range(n_blocks):
        o_ref[blk] = res[:, blk * tq:(blk + 1) * tq].astype(o_ref.dtype)


def _matmul_ws_t(a, w_t, *, name, first_rows, tn, tq, tm):
    m, k = a.shape
    tm = min(tm, m)
    assert m % tm == 0 and tm % tq == 0 and max(first_rows) + tn <= w_t.shape[0] and w_t.shape[1] == k
    per_step = tm // tq
    return pl.pallas_call(
        _mm_ws_t_body,
        grid=(len(first_rows), m // tm),
        in_specs=[_row_window(first_rows, tn, k), pl.BlockSpec((tm, k), lambda j, i: (i, 0))],
        out_specs=pl.BlockSpec((None, per_step, tn, tq), lambda j, i: (j, i, 0, 0)),
        out_shape=jax.ShapeDtypeStruct((len(first_rows), m // tq, tn, tq), BF16),
        scratch_shapes=[pltpu.VMEM((tn, k), BF16)],
        compiler_params=_params("parallel", "arbitrary"),
        name=name,
    )(w_t, a)


def _norm_proj_body(x_ref, g_ref, w_ref, *out_refs):
    h = _rms(x_ref[...], g_ref[...]).astype(BF16)
    p_ref = out_refs[-1]
    p_ref[...] = jnp.dot(h, w_ref[...].astype(BF16), preferred_element_type=F32).astype(p_ref.dtype)
    if len(out_refs) == 2:
        out_refs[0][...] = h


def _rmsnorm_proj(x, g, w, *, name, proj_dtype, keep_rows, tm=NORM_ROWS):
    m, d = x.shape
    n = w.shape[1]
    tm = min(tm, m)
    assert m % tm == 0
    out_specs = [pl.BlockSpec((tm, n), lambda i: (i, 0))]
    out_shape = [jax.ShapeDtypeStruct((m, n), proj_dtype)]
    if keep_rows:
        out_specs.insert(0, pl.BlockSpec((tm, d), lambda i: (i, 0)))
        out_shape.insert(0, jax.ShapeDtypeStruct((m, d), BF16))
    out = pl.pallas_call(
        _norm_proj_body,
        grid=(m // tm,),
        in_specs=[pl.BlockSpec((tm, d), lambda i: (i, 0)), pl.BlockSpec((1, d), lambda i: (0, 0)),
                  pl.BlockSpec((d, n), lambda i: (0, 0))],
        out_specs=out_specs,
        out_shape=out_shape,
        compiler_params=_params("parallel"),
        name=name,
    )(x, g.reshape(1, d).astype(F32), w)
    return out if keep_rows else out[0]


def _sb_body(q_ref, k_ref, vt_ref, tri_ref, *refs, tq, n_riders):
    o_ref, qs_ref, z_ref, lb_ref, wb_ref, acc_ref = refs[n_riders], *refs[2 * n_riders + 1:]
    _round_chunks(refs[:n_riders], refs[n_riders + 1:2 * n_riders + 1])
    i = pl.program_id(1)
    shape = (tq, tq)
    before = lax.broadcasted_iota(I32, shape, 0) < lax.broadcasted_iota(I32, shape, 1)
    heads = [slice(h * HEAD_DIM, (h + 1) * HEAD_DIM) for h in range(N_HEADS)]
    qs_ref[...] = (q_ref[...].astype(F32) * (HEAD_DIM ** -0.5 * LOG2E)).astype(BF16)
    acc_ref[...] = jnp.zeros(acc_ref.shape, F32)

    def tile(j, carry, diagonal):
        start = pl.multiple_of(j * tq, tq)
        for h, hs in enumerate(heads):
            z = lax.dot_general(k_ref[pl.ds(start, tq), hs], qs_ref[:, hs], _NT,
                                preferred_element_type=F32)
            neg_z = -z
            log_keep = jnp.minimum(neg_z, 0.0) - jnp.log2(1.0 + jnp.exp2(jnp.minimum(z, neg_z)))
            if diagonal:
                log_keep = jnp.where(before, log_keep, 0.0)
            z_ref[h] = z
            lb_ref[h] = log_keep.astype(BF16)
        new_carry = []
        for h in range(N_HEADS):
            c = jnp.dot(tri_ref[...], lb_ref[h], preferred_element_type=F32) + carry[h:h + 1, :]
            w = jnp.exp2(z_ref[h] + c)
            if diagonal:
                w = jnp.where(before, w, 0.0)
            wb_ref[h] = w.astype(BF16)
            new_carry.append(c[0:1, :])
        for h, hs in enumerate(heads):
            acc_ref[h] += jnp.dot(vt_ref[j, hs, :], wb_ref[h], preferred_element_type=F32)
        return jnp.concatenate(new_carry, axis=0)

    carry = tile(i, jnp.zeros((N_HEADS, tq), F32), True)

    def cond(state):
        j, live, _ = state
        return jnp.logical_and(j >= 0, live > EXP2_UNDERFLOW)

    def body(state):
        j, _, carry = state
        carry = tile(j, carry, False)
        return j - 1, jnp.max(carry), carry

    lax.while_loop(cond, body, (i - 1, jnp.max(carry), carry))
    for h, hs in enumerate(heads):
        o_ref[:, hs] = acc_ref[h].T.astype(o_ref.dtype)


def _sb_attention(qk, v_t, batch, seq, *, q_col, k_col, v_branch, tq, round_weights=()):
    nq = seq // tq
    width = N_HEADS * HEAD_DIM
    tri = (jnp.arange(tq)[None, :] >= jnp.arange(tq)[:, None]).astype(BF16)
    resident = pl.Buffered(1)
    rider_in, rider_out, rider_shape = _rounding_riders(round_weights, batch * nq,
                                                        lambda b, i: b * nq + i)
    return pl.pallas_call(
        functools.partial(_sb_body, tq=tq, n_riders=len(round_weights)),
        grid=(batch, nq),
        in_specs=[
            pl.BlockSpec((tq, width), lambda b, i: (b * nq + i, q_col)),
            pl.BlockSpec((seq, width), lambda b, i: (b, k_col), pipeline_mode=resident),
            pl.BlockSpec((None, None, nq, width, tq), lambda b, i: (v_branch, b, 0, 0, 0),
                         pipeline_mode=resident),
            pl.BlockSpec((tq, tq), lambda b, i: (0, 0), pipeline_mode=resident),
        ] + rider_in,
        out_specs=[pl.BlockSpec((tq, width), lambda b, i: (b * nq + i, 0))] + rider_out,
        out_shape=[jax.ShapeDtypeStruct((batch * seq, width), BF16)] + rider_shape,
        scratch_shapes=[
            pltpu.VMEM((tq, width), BF16),
            pltpu.VMEM((N_HEADS, tq, tq), F32),
            pltpu.VMEM((N_HEADS, tq, tq), BF16),
            pltpu.VMEM((N_HEADS, tq, tq), BF16),
            pltpu.VMEM((N_HEADS, HEAD_DIM, tq), F32),
        ],
        compiler_params=_params("parallel", "arbitrary"),
        name="sb_attention",
    )(qk, qk, v_t, tri, *round_weights)


def _bucket_thresholds():
    nb = N_BUCKETS // 2
    max_exact = nb // 2
    span = nb - max_exact
    out = []
    for k in range(1, span):
        n = max_exact
        while n ** span * max_exact ** k < MAX_DISTANCE ** k * max_exact ** span:
            n += 1
        out.append(n)
    return max_exact, out


def _bias_body(rb_ref, o_ref, *, tq):
    nb = N_BUCKETS // 2
    max_exact, steps = _bucket_thresholds()
    shape = (2 * tq, tq)
    rel = lax.broadcasted_iota(I32, shape, 0) - lax.broadcasted_iota(I32, shape, 1) - tq
    n = jnp.abs(rel)
    large = jnp.full(shape, max_exact, I32)
    for t in steps:
        large = large + (n >= t).astype(I32)
    bucket = jnp.where(rel > 0, nb, 0) + jnp.where(n < max_exact, n, large)
    for h in range(N_HEADS):
        val = jnp.zeros(shape, F32)
        for b in range(N_BUCKETS):
            val = jnp.where(bucket == b, rb_ref[b, h], val)
        o_ref[h] = (val - rb_ref[nb - 1, h]) * LOG2E


def _near_bias(rel_bias, tq):
    return pl.pallas_call(
        functools.partial(_bias_body, tq=tq),
        in_specs=[pl.BlockSpec(memory_space=pltpu.SMEM)],
        out_specs=pl.BlockSpec(memory_space=pltpu.VMEM),
        out_shape=jax.ShapeDtypeStruct((N_HEADS, 2 * tq, tq), F32),
        compiler_params=pltpu.CompilerParams(vmem_limit_bytes=VMEM_LIMIT_BYTES),
        name="dsa_near_bias",
    )(rel_bias.astype(F32))


def _order_key(x):
    bits = lax.bitcast_convert_type(x, I32)
    return bits ^ ((bits >> 31) | SIGN_BIT)


def _order_key_to_float(key):
    return lax.bitcast_convert_type(key ^ ((~key >> 31) | SIGN_BIT), F32)


def _bit_transpose32(words):
    a = list(words)
    j, m = 16, 0x0000FFFF
    while j:
        mask = jnp.int32(m - (1 << 32) if m >= 1 << 31 else m)
        k = 0
        while k < 32:
            t = (lax.shift_right_logical(a[k], jnp.int32(j)) ^ a[k + j]) & mask
            a[k] = a[k] ^ lax.shift_left(t, jnp.int32(j))
            a[k + j] = a[k + j] ^ t
            k = (k + j + 1) & ~j
        j >>= 1
        m = (m ^ (m << j)) & 0xFFFFFFFF
    return a


def _dsa_body(qd_ref, qi_ref, wq_ref, kd_ref, vt_ref, ki_ref, bias_ref, tri_ref, o_ref,
              sc_ref, plane_ref, qs_ref, m_ref, l_ref, acc_ref, lg_ref, bmax_ref, *, tq, top):
    i = pl.program_id(1)
    shape = (tq, tq)
    key_row = lax.broadcasted_iota(I32, shape, 0)
    qry_col = lax.broadcasted_iota(I32, shape, 1)
    visible = key_row // CHUNK <= qry_col // CHUNK

    w_t = (wq_ref[...] * (IDX_DIM ** -0.5 * IDX_HEADS ** -0.5)).T

    heads_per_vreg = LANES // IDX_DIM

    def score_tile(j):
        start = pl.multiple_of(j * tq, tq)
        ki = [ki_ref[pl.ds(start, tq), c * LANES:(c + 1) * LANES].astype(BF16)
              for c in range(heads_per_vreg)]
        s = jnp.zeros(shape, F32)
        for h in range(IDX_HEADS):
            g, c = divmod(h, heads_per_vreg)
            d = lax.dot_general(ki[c], qi_ref[:, g * LANES:(g + 1) * LANES], _NT,
                                preferred_element_type=F32)
            s = s + w_t[h:h + 1, :] * jnp.maximum(d, 0.0)
        sc_ref[j] = s

    def key_planes(j):
        ukey = _order_key(sc_ref[j])
        planes = _bit_transpose32([ukey[g * SUBLANES:(g + 1) * SUBLANES, :] for g in range(KEY_BITS)])
        for b in range(KEY_BITS):
            plane_ref[b, pl.ds(pl.multiple_of(j * SUBLANES, SUBLANES), SUBLANES), :] = planes[b]

    @pl.when(i == 0)
    def _():
        plane_ref[...] = jnp.zeros(plane_ref.shape, I32)

    def score_step(j, carry):
        key_planes(j)
        score_tile(j + 1)
        return carry

    score_tile(0)
    lax.fori_loop(0, i, score_step, 0)
    key_planes(i)
    sc_ref[i] = jnp.where(visible, sc_ref[i], -jnp.inf)

    n_rows = plane_ref.shape[1]
    block_of_row = lax.broadcasted_iota(I32, (n_rows, tq), 0) // SUBLANES
    qry_of_col = lax.broadcasted_iota(I32, (n_rows, tq), 1)
    n_bits = (qry_of_col // CHUNK + 1) * (CHUNK // SUBLANES)
    diag_bits = jnp.where(n_bits >= KEY_BITS, -1, lax.shift_left(jnp.int32(1), n_bits) - 1)
    cand0 = jnp.where(block_of_row < i, -1, jnp.where(block_of_row == i, diag_bits, 0))

    def popcount_rows(words):
        return jnp.sum(lax.population_count(words), axis=0, keepdims=True)

    def bit_step(t, state):
        cand, n_above, thr_bits = state
        b = KEY_BITS - 1 - t
        ones = cand & plane_ref[b]
        n_ones = popcount_rows(ones)
        take = n_above + n_ones >= top
        cand = jnp.where(take, ones, cand ^ ones)
        n_above = jnp.where(take, n_above, n_above + n_ones)
        thr_bits = thr_bits | jnp.where(take, lax.shift_left(jnp.int32(1), b), 0)
        return cand, n_above, thr_bits

    zero = jnp.zeros((1, tq), I32)
    cand, n_above, thr_bits = lax.fori_loop(0, KEY_BITS, bit_step, (cand0, zero, zero))
    qry = lax.broadcasted_iota(I32, (1, tq), 1)
    n_visible = i * tq + (qry // CHUNK + 1) * CHUNK
    wanted = n_visible > top
    thr = jnp.where(wanted, _order_key_to_float(thr_bits), jnp.finfo(F32).min)
    tied = jnp.logical_and(wanted, n_above + popcount_rows(cand) > top)
    c_hi = n_above

    ones = jnp.ones((2 * SUBLANES, tq), BF16)

    def plain_mask(j, carry):
        sc_ref[j] = jnp.where(sc_ref[j] >= thr, 0.0, NEG_BIG)
        return carry

    def tie_mask(j, seen):
        s = sc_ref[j]
        equal = s == thr
        rank = jnp.dot(tri_ref[...], equal.astype(BF16), preferred_element_type=F32) + seen
        quota = jnp.where(tied, (top - c_hi).astype(F32), jnp.inf)
        keep_equal = jnp.where(rank < quota, 0.0, NEG_BIG)
        sc_ref[j] = jnp.where(s > thr, 0.0, jnp.where(equal, keep_equal, NEG_BIG))
        return seen + jnp.sum(equal.astype(F32), axis=0, keepdims=True)

    def with_ties():
        lax.fori_loop(0, i + 1, tie_mask, jnp.zeros((1, tq), F32))
        return jnp.int32(0)

    def without_ties():
        return lax.fori_loop(0, i + 1, plain_mask, jnp.int32(0))

    lax.cond(jnp.max(tied.astype(I32)) > 0, with_ties, without_ties)

    qs_ref[...] = (qd_ref[...].astype(F32) * (HEAD_DIM ** -0.5 * LOG2E)).astype(BF16)
    m_ref[...] = jnp.full(m_ref.shape, NEG_BIG, F32)
    l_ref[...] = jnp.zeros(l_ref.shape, F32)
    acc_ref[...] = jnp.zeros(acc_ref.shape, F32)

    heads = [slice(h * HEAD_DIM, (h + 1) * HEAD_DIM) for h in range(N_HEADS)]

    far, prev, diag = None, 0, 1

    def logits(u, near, slot):
        j = i - u
        start = pl.multiple_of(j * tq, tq)
        mask = sc_ref[j]
        block_max = []
        for h, hs in enumerate(heads):
            lg = lax.dot_general(kd_ref[pl.ds(start, tq), hs], qs_ref[:, hs], _NT,
                                 preferred_element_type=F32)
            if near is not None:
                lg = lg + bias_ref[h, near * tq:(near + 1) * tq, :]
            lg = lg + mask
            lg_ref[slot, h] = lg
            block_max.append(jnp.max(lg, axis=0, keepdims=True))
        bmax_ref[slot] = jnp.concatenate(block_max, axis=0)

    def values(u, slot):
        j = i - u
        m_old = m_ref[...]
        m_new = jnp.maximum(m_old, bmax_ref[slot])
        alpha = jnp.exp2(m_old - m_new)
        m_ref[...] = m_new
        denom = []
        for h, hs in enumerate(heads):
            p = jnp.exp2(lg_ref[slot, h] - m_new[h:h + 1, :]).astype(BF16)
            v_ext = jnp.concatenate([vt_ref[j, hs, :], ones], axis=0)
            pv = jnp.dot(v_ext, p, preferred_element_type=F32)
            acc_ref[h] = alpha[h:h + 1, :] * acc_ref[h] + pv[:HEAD_DIM]
            denom.append(pv[HEAD_DIM:HEAD_DIM + 1])
        l_ref[...] = alpha * l_ref[...] + jnp.concatenate(denom, axis=0)

    def even_step(u, near_a, near_b):
        logits(u - 1, near_a, 1)
        values(u, 0)
        logits(u - 2, near_b, 0)
        values(u - 1, 1)

    @pl.when(i == 0)
    def _():
        logits(0, diag, 0)
        values(0, 0)

    @pl.when(i == 1)
    def _():
        logits(1, prev, 1)

    @pl.when(jnp.logical_and(i >= 2, i % 2 == 1))
    def _():
        logits(i, far, 1)
        logits(i - 1, far, 0)
        values(i, 1)

    @pl.when(jnp.logical_and(i >= 2, i % 2 == 0))
    def _():
        logits(i, far, 0)

    def far_pair(k, carry):
        even_step(2 * (i // 2 - k), far, far)
        return carry

    lax.fori_loop(0, i // 2 - 1, far_pair, 0)

    @pl.when(i >= 2)
    def _():
        even_step(2, prev, diag)

    @pl.when(i == 1)
    def _():
        logits(0, diag, 0)
        values(1, 1)

    @pl.when(i >= 1)
    def _():
        values(0, 0)

    for h in range(N_HEADS):
        o = acc_ref[h] / l_ref[h:h + 1, :]
        o_ref[:, h * HEAD_DIM:(h + 1) * HEAD_DIM] = o.T.astype(o_ref.dtype)


def _dsa_attention(main, v_t, small, bias, batch, seq, *, qd_col, kd_col, qi_col, v_branch, tq):
    nq = seq // tq
    width = N_HEADS * HEAD_DIM
    assert IDX_HEADS * IDX_DIM == width and tq == KEY_BITS * SUBLANES and tq % CHUNK == 0
    key_copies = LANES // IDX_DIM
    top = min(TOPK_MAX, seq // 4)
    tri = (jnp.arange(tq)[None, :] < jnp.arange(tq)[:, None]).astype(BF16)
    resident = pl.Buffered(1)
    return pl.pallas_call(
        functools.partial(_dsa_body, tq=tq, top=top),
        grid=(batch, nq),
        in_specs=[
            pl.BlockSpec((tq, width), lambda b, i: (b * nq + i, qd_col)),
            pl.BlockSpec((tq, IDX_HEADS * IDX_DIM), lambda b, i: (b * nq + i, qi_col)),
            pl.BlockSpec((tq, LANES), lambda b, i: (b * nq + i, key_copies)),
            pl.BlockSpec((seq, width), lambda b, i: (b, kd_col), pipeline_mode=resident),
            pl.BlockSpec((None, None, nq, width, tq), lambda b, i: (v_branch, b, 0, 0, 0),
                         pipeline_mode=resident),
            pl.BlockSpec((seq, key_copies * LANES), lambda b, i: (b, 0), pipeline_mode=resident),
            pl.BlockSpec((N_HEADS, 2 * tq, tq), lambda b, i: (0, 0, 0), pipeline_mode=resident),
            pl.BlockSpec((tq, tq), lambda b, i: (0, 0), pipeline_mode=resident),
        ],
        out_specs=pl.BlockSpec((tq, width), lambda b, i: (b * nq + i, 0)),
        out_shape=jax.ShapeDtypeStruct((batch * seq, width), BF16),
        scratch_shapes=[
            pltpu.VMEM((nq, tq, tq), F32),
            pltpu.VMEM((KEY_BITS, nq * SUBLANES, tq), I32),
            pltpu.VMEM((tq, width), BF16),
            pltpu.VMEM((N_HEADS, tq), F32),
            pltpu.VMEM((N_HEADS, tq), F32),
            pltpu.VMEM((N_HEADS, HEAD_DIM, tq), F32),
            pltpu.VMEM((2, N_HEADS, tq, tq), F32),
            pltpu.VMEM((2, N_HEADS, tq), F32),
        ],
        compiler_params=_params("parallel", "arbitrary"),
        name="dsa_attention",
    )(main, main, small, main, v_t, small, bias, tri)


def _merge_cross_body(osb_ref, ods_ref, wsb_ref, wds_ref, gsb_ref, gds_ref, bsb_ref, bds_ref, wo_ref,
                      x_ref, gc_ref, wq_ref, km_ref, vm_ref, wco_ref, gn_ref, o_ref, hn_ref):
    p_sb = jnp.dot(osb_ref[...], wsb_ref[...], preferred_element_type=F32)
    p_ds = jnp.dot(ods_ref[...], wds_ref[...], preferred_element_type=F32)
    g_sb = jax.nn.sigmoid(gsb_ref[...].astype(F32) + bsb_ref[...])
    g_ds = jax.nn.sigmoid(gds_ref[...].astype(F32) + bds_ref[...])
    merged = (g_sb * p_sb + g_ds * p_ds).astype(BF16)
    x1 = x_ref[...] + jnp.dot(merged, wo_ref[...], preferred_element_type=F32)

    h = _rms(x1, gc_ref[...]).astype(BF16)
    q = jnp.dot(h, wq_ref[...], preferred_element_type=F32) * HEAD_DIM ** -0.5
    q = q.astype(BF16)
    outs = []
    for hh in range(MEM_HEADS):
        hs = slice(hh * HEAD_DIM, (hh + 1) * HEAD_DIM)
        lg = lax.dot_general(q[:, hs], km_ref[:, hs], _NT, preferred_element_type=F32)
        p = jnp.exp(lg - jnp.max(lg, axis=1, keepdims=True))
        o = jnp.dot(p.astype(BF16), vm_ref[:, hs], preferred_element_type=F32)
        outs.append((o / jnp.sum(p, axis=1, keepdims=True)).astype(BF16))
    o = jnp.concatenate(outs, axis=1)
    x2 = x1 + jnp.dot(o, wco_ref[...], preferred_element_type=F32)
    o_ref[...] = x2
    hn_ref[...] = _rms(x2, gn_ref[...]).astype(hn_ref.dtype)


def _merge_cross(o_sb, o_ds, w_sb, w_ds, w_out, proj, gate_offset, b_gate, x,
                 kv, g_cross, w_cq, w_co, g_next, batch, seq, *, tm):
    m, k = o_sb.shape
    d = w_sb.shape[1]
    n_mem = kv.shape[0] // batch
    width = MEM_HEADS * HEAD_DIM
    tm = min(tm, seq)
    assert gate_offset % d == 0 and seq % tm == 0
    g = gate_offset // d
    nt = seq // tm
    b_gate = b_gate.reshape(1, 2 * d).astype(F32)
    resident = pl.Buffered(1)

    def rows(width_, col=0):
        return pl.BlockSpec((tm, width_), lambda b, i: (b * nt + i, col))

    def whole(shape, *block):
        return pl.BlockSpec(shape, lambda b, i: block or (0,) * len(shape), pipeline_mode=resident)

    return pl.pallas_call(
        _merge_cross_body,
        grid=(batch, nt),
        in_specs=[
            rows(k), rows(k), whole((k, d)), whole((k, d)),
            rows(d, g), rows(d, g + 1), whole((1, d)), whole((1, d), 0, 1),
            whole((d, d)), rows(d),
            whole((1, d)), whole((d, width)),
            pl.BlockSpec((n_mem, width), lambda b, i: (b, 0)),
            pl.BlockSpec((n_mem, width), lambda b, i: (b, 1)),
            whole((width, d)), whole((1, d)),
        ],
        out_specs=[rows(d), rows(d)],
        out_shape=[jax.ShapeDtypeStruct((m, d), F32), jax.ShapeDtypeStruct((m, d), BF16)],
        compiler_params=_params("parallel", "parallel"),
        name="merge_out_cross",
    )(o_sb, o_ds, w_sb, w_ds, proj, proj, b_gate, b_gate, w_out, x,
      g_cross.reshape(1, d).astype(F32), w_cq, kv, kv, w_co, g_next.reshape(1, d).astype(F32))


def _delayed(u, tail, shift):
    rolled = pltpu.roll(u, shift, axis=0)
    row = lax.broadcasted_iota(I32, tail.shape, 0)
    head = jnp.where(row < shift, pltpu.roll(tail, shift, axis=0), rolled[:SUBLANES])
    return jnp.concatenate([head, rolled[SUBLANES:]], axis=0)


def _ffn_up_body(h_ref, wa_ref, wv_ref, cwa_ref, cwv_ref, cba_ref, cbv_ref, wd_ref, o_ref, wdb_ref,
                 wab_ref, wvb_ref, halo_ref, *, tiles_per_seq):
    i = pl.program_id(1)
    _round_chunks([wd_ref], [wdb_ref])

    @pl.when(i == 0)
    def _():
        wab_ref[...] = wa_ref[...].astype(BF16)
        wvb_ref[...] = wv_ref[...].astype(BF16)

    h = h_ref[...]
    tm = h.shape[0]
    sequence_start = i % tiles_per_seq == 0

    def conv(wb_ref, cw_ref, cb_ref, slot):
        u = jnp.dot(h, wb_ref[...], preferred_element_type=F32)
        tail = jnp.where(sequence_start, 0.0, halo_ref[slot])
        halo_ref[slot] = u[tm - SUBLANES:, :]
        c = cb_ref[...] + cw_ref[CONV_WIDTH - 1:CONV_WIDTH, :] * u
        for tap in range(CONV_WIDTH - 1):
            c = c + cw_ref[tap:tap + 1, :] * _delayed(u, tail, CONV_WIDTH - 1 - tap)
        return c

    a = conv(wab_ref, cwa_ref, cba_ref, 0)
    val = conv(wvb_ref, cwv_ref, cbv_ref, 1)
    o_ref[...] = (jax.nn.gelu(a) * val).astype(o_ref.dtype)


def _ffn_up_gate(h, w_up, conv_w, conv_b, w_down, seq, *, tm=ROW_TILE, tn=FFN_COL_TILE):
    m, d = h.shape
    two_ff = w_up.shape[1]
    d_ff = two_ff // 2
    tm, tn = min(tm, seq), min(tn, d_ff)
    assert seq % tm == 0 and d_ff % tn == 0 and tm >= SUBLANES >= CONV_WIDTH - 1
    nf, nt = d_ff // tn, m // tm
    conv_w = conv_w.astype(F32)
    conv_b = conv_b.reshape(1, two_ff).astype(F32)
    rider_in, rider_out, rider_shape = _rounding_riders([w_down], nf * nt, lambda j, i: j * nt + i)
    return pl.pallas_call(
        functools.partial(_ffn_up_body, tiles_per_seq=seq // tm),
        grid=(nf, nt),
        in_specs=[
            pl.BlockSpec((tm, d), lambda j, i: (i, 0)),
            pl.BlockSpec((d, tn), lambda j, i: (0, j)),
            pl.BlockSpec((d, tn), lambda j, i: (0, nf + j)),
            pl.BlockSpec((CONV_WIDTH, tn), lambda j, i: (0, j)),
            pl.BlockSpec((CONV_WIDTH, tn), lambda j, i: (0, nf + j)),
            pl.BlockSpec((1, tn), lambda j, i: (0, j)),
            pl.BlockSpec((1, tn), lambda j, i: (0, nf + j)),
        ] + rider_in,
        out_specs=[pl.BlockSpec((tm, tn), lambda j, i: (i, j))] + rider_out,
        out_shape=[jax.ShapeDtypeStruct((m, d_ff), BF16)] + rider_shape,
        scratch_shapes=[pltpu.VMEM((d, tn), BF16), pltpu.VMEM((d, tn), BF16),
                        pltpu.VMEM((2, SUBLANES, tn), F32)],
        compiler_params=_params("parallel", "arbitrary"),
        name="ffn_up_conv_gate",
    )(h, w_up, w_up, conv_w, conv_w, conv_b, conv_b, w_down)


def _ffn_down_body(a_ref, w_ref, x_ref, g_ref, o_ref, *, final_norm):
    k = pl.program_id(1)

    @pl.when(k == 0)
    def _():
        o_ref[...] = x_ref[...]

    o_ref[...] += jnp.dot(a_ref[...], w_ref[...], preferred_element_type=F32)

    if final_norm:
        @pl.when(k == pl.num_programs(1) - 1)
        def _():
            o_ref[...] = _rms(o_ref[...], g_ref[...])


def _ffn_down(a, w, x, g_final, *, tm=ROW_TILE, tk=FFN_K_TILE):
    m, kdim = a.shape
    d = w.shape[1]
    tm, tk = min(tm, m), min(tk, kdim)
    assert m % tm == 0 and kdim % tk == 0
    final_norm = g_final is not None
    g = (g_final if final_norm else jnp.ones((d,), F32)).reshape(1, d).astype(F32)
    return pl.pallas_call(
        functools.partial(_ffn_down_body, final_norm=final_norm),
        grid=(m // tm, kdim // tk),
        in_specs=[
            pl.BlockSpec((tm, tk), lambda i, k: (i, k)),
            pl.BlockSpec((tk, d), lambda i, k: (k, 0)),
            pl.BlockSpec((tm, d), lambda i, k: (i, 0)),
            pl.BlockSpec((1, d), lambda i, k: (0, 0)),
        ],
        out_specs=pl.BlockSpec((tm, d), lambda i, k: (i, 0)),
        out_shape=jax.ShapeDtypeStruct((m, d), F32),
        compiler_params=_params("parallel", "arbitrary"),
        name="ffn_down",
    )(a, w, x, g)


def _layer(x, mem, g_mix, w_in, b_gate, w_proj_sb, w_proj_dsa, w_out, rel_bias,
           g_cross, g_mem, w_cq, w_ckv, w_co, g_ffn, w_up, conv_w, conv_b, w_down, g_final, batch, seq):
    d = x.shape[1]
    width = N_HEADS * HEAD_DIM
    idx_w = IDX_HEADS * IDX_DIM
    o_qi = 6 * width
    o_ki = o_qi + idx_w
    o_wi = o_ki + IDX_DIM
    o_g = o_wi + IDX_HEADS

    zeros = jnp.zeros((d, LANES - IDX_DIM), F32)
    w_small = jnp.concatenate([
        w_in[:, o_ki:o_wi], zeros, zeros, w_in[:, o_ki:o_wi],
        jnp.pad(w_in[:, o_wi:o_g], ((0, 0), (0, LANES - IDX_HEADS)))], axis=1).astype(BF16)
    h, small = _rmsnorm_proj(x, g_mix, w_small, name="mixer_norm_index_proj", proj_dtype=F32,
                             keep_rows=True)
    w_in_t = w_in.T
    tq = ATT_BLOCK
    nq = seq // tq
    q_sb, k_sb, v_sb, q_ds, k_ds, v_ds, q_ix = (g * width for g in range(o_ki // width))
    gate_rows = tuple(o_g + g * width for g in range(2 * d // width))
    n_gate = len(gate_rows)
    main = _matmul_ws(h, w_in_t, name="in_proj_main",
                      first_rows=gate_rows + (q_sb, k_sb, q_ds, k_ds, q_ix),
                      out_dtype=BF16, tm=PROJ_ROWS, tn=width)
    v_t = _matmul_ws_t(h, w_in_t, name="in_proj_values", first_rows=(v_sb, v_ds), tn=width, tq=tq,
                       tm=PROJ_ROWS)
    v_t = v_t.reshape(2, batch, nq, width, tq)

    o_sb, w_proj_sb, w_proj_dsa, w_out, w_cq, w_co = _sb_attention(
        main, v_t, batch, seq, q_col=n_gate, k_col=n_gate + 1, v_branch=0, tq=tq,
        round_weights=(w_proj_sb, w_proj_dsa, w_out, w_cq, w_co))
    bias = _near_bias(rel_bias, tq)
    o_ds = _dsa_attention(main, v_t, small, bias, batch, seq, qd_col=n_gate + 2, kd_col=n_gate + 3,
                          qi_col=n_gate + 4, v_branch=1, tq=tq)

    kv = _rmsnorm_proj(mem, g_mem, w_ckv, name="mem_norm_kv_proj", proj_dtype=BF16, keep_rows=False)
    x, h_ffn = _merge_cross(o_sb, o_ds, w_proj_sb, w_proj_dsa, w_out, main, 0, b_gate, x, kv, g_cross,
                            w_cq, w_co, g_ffn, batch, seq, tm=MERGE_ROWS)

    act, w_down = _ffn_up_gate(h_ffn, w_up, conv_w, conv_b, w_down, seq)
    return _ffn_down(act, w_down, x, g_final)


def kernel(x, mem, g_mix, w_in, b_gate, w_proj_sb, w_proj_dsa, w_out, rel_bias, g_cross, g_mem,
           w_cq, w_ckv, w_co, g_ffn, w_up, conv_w, conv_b, w_down, g_final):
    batch, seq, d = x.shape
    h = x.reshape(batch * seq, d)
    mem2 = mem.reshape(batch * mem.shape[1], d)
    depth = g_mix.shape[0]
    for l in range(depth):
        h = _layer(h, mem2, g_mix[l], w_in[l], b_gate[l], w_proj_sb[l], w_proj_dsa[l], w_out[l],
                   rel_bias, g_cross[l], g_mem[l], w_cq[l], w_ckv[l], w_co[l], g_ffn[l], w_up[l],
                   conv_w[l], conv_b[l], w_down[l], g_final if l == depth - 1 else None, batch, seq)
    return h.reshape(batch, seq, d)
```

```python
import functools

import jax
import jax.numpy as jnp
from jax import lax
from jax.experimental import pallas as pl
from jax.experimental.pallas import tpu as pltpu

F32, BF16, I32 = jnp.float32, jnp.bfloat16, jnp.int32

EPS = 1e-6
HEAD_DIM = 128
N_HEADS = 8
IDX_HEADS = 16
IDX_DIM = 64
CHUNK = 64
TOPK_MAX = 256
N_BUCKETS = 32
MAX_DISTANCE = 128
MEM_HEADS = 4
CONV_WIDTH = 3

LANES = 128
SUBLANES = 8
VMEM_LIMIT_BYTES = 62 * 1024 * 1024
NEG_BIG = -1e30
EXP2_UNDERFLOW = -151.0
LOG2E = 1.4426950408889634
KEY_BITS = 32
SIGN_BIT = -(2 ** 31)

PROJ_ROWS = 2048
ROW_TILE = 1024
NORM_ROWS = 1024
MERGE_ROWS = 512
FFN_COL_TILE = 512
FFN_K_TILE = 2048
ATT_BLOCK = KEY_BITS * SUBLANES

_NT = (((1,), (1,)), ((), ()))


def _params(*sem):
    return pltpu.CompilerParams(dimension_semantics=sem, vmem_limit_bytes=VMEM_LIMIT_BYTES)


def _rms(x, g):
    inv = lax.rsqrt(jnp.mean(x * x, axis=-1, keepdims=True) + EPS)
    return x * inv * g


def _rounding_riders(weights, n_steps, step_index):
    in_specs, out_specs, out_shape = [], [], []
    for w in weights:
        rows, cols = w.shape
        assert rows % n_steps == 0 and (rows // n_steps) % (2 * SUBLANES) == 0
        for specs in (in_specs, out_specs):
            specs.append(pl.BlockSpec((rows // n_steps, cols), lambda *ids: (step_index(*ids), 0)))
        out_shape.append(jax.ShapeDtypeStruct(w.shape, BF16))
    return in_specs, out_specs, out_shape


def _round_chunks(src_refs, dst_refs):
    for src, dst in zip(src_refs, dst_refs):
        dst[...] = src[...].astype(BF16)


def _mm_ws_body(a_ref, wt_ref, o_ref, wb_ref):
    @pl.when(pl.program_id(1) == 0)
    def _():
        wb_ref[...] = wt_ref[...].astype(BF16)

    o_ref[...] = lax.dot_general(a_ref[...], wb_ref[...], _NT,
                                 preferred_element_type=F32).astype(o_ref.dtype)


def _row_window(first_rows, tn, k):
    assert all(r % SUBLANES == 0 for r in first_rows)

    def index_map(j, i):
        row = jnp.int32(first_rows[0])
        for step, first in enumerate(first_rows[1:], start=1):
            row = jnp.where(j >= step, first, row)
        return pl.multiple_of(row, SUBLANES), 0

    return pl.BlockSpec((pl.Element(tn), pl.Element(k)), index_map)


def _matmul_ws(a, w_t, *, name, first_rows, out_dtype, tm, tn):
    m, k = a.shape
    tm = min(tm, m)
    assert m % tm == 0 and max(first_rows) + tn <= w_t.shape[0] and w_t.shape[1] == k
    return pl.pallas_call(
        _mm_ws_body,
        grid=(len(first_rows), m // tm),
        in_specs=[pl.BlockSpec((tm, k), lambda j, i: (i, 0)), _row_window(first_rows, tn, k)],
        out_specs=pl.BlockSpec((tm, tn), lambda j, i: (i, j)),
        out_shape=jax.ShapeDtypeStruct((m, len(first_rows) * tn), out_dtype),
        scratch_shapes=[pltpu.VMEM((tn, k), BF16)],
        compiler_params=_params("parallel", "arbitrary"),
        name=name,
    )(a, w_t)


def _mm_ws_t_body(wt_ref, a_ref, o_ref, wb_ref):
    @pl.when(pl.program_id(1) == 0)
    def _():
        wb_ref[...] = wt_ref[...].astype(BF16)

    res = lax.dot_general(wb_ref[...], a_ref[...], _NT, preferred_element_type=F32)
    n_blocks, _, tq = o_ref.shape
    for blk in range(n_blocks):
        o_ref[blk] = res[:, blk * tq:(blk + 1) * tq].astype(o_ref.dtype)


def _matmul_ws_t(a, w_t, *, name, first_rows, tn, tq, tm):
    m, k = a.shape
    tm = min(tm, m)
    assert m % tm == 0 and tm % tq == 0 and max(first_rows) + tn <= w_t.shape[0] and w_t.shape[1] == k
    per_step = tm // tq
    return pl.pallas_call(
        _mm_ws_t_body,
        grid=(len(first_rows), m // tm),
        in_specs=[_row_window(first_rows, tn, k), pl.BlockSpec((tm, k), lambda j, i: (i, 0))],
        out_specs=pl.BlockSpec((None, per_step, tn, tq), lambda j, i: (j, i, 0, 0)),
        out_shape=jax.ShapeDtypeStruct((len(first_rows), m // tq, tn, tq), BF16),
        scratch_shapes=[pltpu.VMEM((tn, k), BF16)],
        compiler_params=_params("parallel", "arbitrary"),
        name=name,
    )(w_t, a)


def _norm_proj_body(x_ref, g_ref, w_ref, *out_refs):
    h = _rms(x_ref[...], g_ref[...]).astype(BF16)
    p_ref = out_refs[-1]
    p_ref[...] = jnp.dot(h, w_ref[...].astype(BF16), preferred_element_type=F32).astype(p_ref.dtype)
    if len(out_refs) == 2:
        out_refs[0][...] = h


def _rmsnorm_proj(x, g, w, *, name, proj_dtype, keep_rows, tm=NORM_ROWS):
    m, d = x.shape
    n = w.shape[1]
    tm = min(tm, m)
    assert m % tm == 0
    out_specs = [pl.BlockSpec((tm, n), lambda i: (i, 0))]
    out_shape = [jax.ShapeDtypeStruct((m, n), proj_dtype)]
    if keep_rows:
        out_specs.insert(0, pl.BlockSpec((tm, d), lambda i: (i, 0)))
        out_shape.insert(0, jax.ShapeDtypeStruct((m, d), BF16))
    out = pl.pallas_call(
        _norm_proj_body,
        grid=(m // tm,),
        in_specs=[pl.BlockSpec((tm, d), lambda i: (i, 0)), pl.BlockSpec((1, d), lambda i: (0, 0)),
                  pl.BlockSpec((d, n), lambda i: (0, 0))],
        out_specs=out_specs,
        out_shape=out_shape,
        compiler_params=_params("parallel"),
        name=name,
    )(x, g.reshape(1, d).astype(F32), w)
    return out if keep_rows else out[0]


def _sb_body(q_ref, k_ref, vt_ref, tri_ref, *refs, tq, n_riders):
    o_ref, qs_ref, z_ref, lb_ref, wb_ref, acc_ref = refs[n_riders], *refs[2 * n_riders + 1:]
    _round_chunks(refs[:n_riders], refs[n_riders + 1:2 * n_riders + 1])
    i = pl.program_id(1)
    shape = (tq, tq)
    before = lax.broadcasted_iota(I32, shape, 0) < lax.broadcasted_iota(I32, shape, 1)
    heads = [slice(h * HEAD_DIM, (h + 1) * HEAD_DIM) for h in range(N_HEADS)]
    qs_ref[...] = (q_ref[...].astype(F32) * (HEAD_DIM ** -0.5 * LOG2E)).astype(BF16)
    acc_ref[...] = jnp.zeros(acc_ref.shape, F32)

    def tile(j, carry, diagonal):
        start = pl.multiple_of(j * tq, tq)
        for h, hs in enumerate(heads):
            z = lax.dot_general(k_ref[pl.ds(start, tq), hs], qs_ref[:, hs], _NT,
                                preferred_element_type=F32)
            neg_z = -z
            log_keep = jnp.minimum(neg_z, 0.0) - jnp.log2(1.0 + jnp.exp2(jnp.minimum(z, neg_z)))
            if diagonal:
                log_keep = jnp.where(before, log_keep, 0.0)
            z_ref[h] = z
            lb_ref[h] = log_keep.astype(BF16)
        new_carry = []
        for h in range(N_HEADS):
            c = jnp.dot(tri_ref[...], lb_ref[h], preferred_element_type=F32) + carry[h:h + 1, :]
            w = jnp.exp2(z_ref[h] + c)
            if diagonal:
                w = jnp.where(before, w, 0.0)
            wb_ref[h] = w.astype(BF16)
            new_carry.append(c[0:1, :])
        for h, hs in enumerate(heads):
            acc_ref[h] += jnp.dot(vt_ref[j, hs, :], wb_ref[h], preferred_element_type=F32)
        return jnp.concatenate(new_carry, axis=0)

    carry = tile(i, jnp.zeros((N_HEADS, tq), F32), True)

    def cond(state):
        j, live, _ = state
        return jnp.logical_and(j >= 0, live > EXP2_UNDERFLOW)

    def body(state):
        j, _, carry = state
        carry = tile(j, carry, False)
        return j - 1, jnp.max(carry), carry

    lax.while_loop(cond, body, (i - 1, jnp.max(carry), carry))
    for h, hs in enumerate(heads):
        o_ref[:, hs] = acc_ref[h].T.astype(o_ref.dtype)


def _sb_attention(qk, v_t, batch, seq, *, q_col, k_col, v_branch, tq, round_weights=()):
    nq = seq // tq
    width = N_HEADS * HEAD_DIM
    tri = (jnp.arange(tq)[None, :] >= jnp.arange(tq)[:, None]).astype(BF16)
    resident = pl.Buffered(1)
    rider_in, rider_out, rider_shape = _rounding_riders(round_weights, batch * nq,
                                                        lambda b, i: b * nq + i)
    return pl.pallas_call(
        functools.partial(_sb_body, tq=tq, n_riders=len(round_weights)),
        grid=(batch, nq),
        in_specs=[
            pl.BlockSpec((tq, width), lambda b, i: (b * nq + i, q_col)),
            pl.BlockSpec((seq, width), lambda b, i: (b, k_col), pipeline_mode=resident),
            pl.BlockSpec((None, None, nq, width, tq), lambda b, i: (v_branch, b, 0, 0, 0),
                         pipeline_mode=resident),
            pl.BlockSpec((tq, tq), lambda b, i: (0, 0), pipeline_mode=resident),
        ] + rider_in,
        out_specs=[pl.BlockSpec((tq, width), lambda b, i: (b * nq + i, 0))] + rider_out,
        out_shape=[jax.ShapeDtypeStruct((batch * seq, width), BF16)] + rider_shape,
        scratch_shapes=[
            pltpu.VMEM((tq, width), BF16),
            pltpu.VMEM((N_HEADS, tq, tq), F32),
            pltpu.VMEM((N_HEADS, tq, tq), BF16),
            pltpu.VMEM((N_HEADS, tq, tq), BF16),
            pltpu.VMEM((N_HEADS, HEAD_DIM, tq), F32),
        ],
        compiler_params=_params("parallel", "arbitrary"),
        name="sb_attention",
    )(qk, qk, v_t, tri, *round_weights)


def _bucket_thresholds():
    nb = N_BUCKETS // 2
    max_exact = nb // 2
    span = nb - max_exact
    out = []
    for k in range(1, span):
        n = max_exact
        while n ** span * max_exact ** k < MAX_DISTANCE ** k * max_exact ** span:
            n += 1
        out.append(n)
    return max_exact, out


def _bias_body(rb_ref, o_ref, *, tq):
    nb = N_BUCKETS // 2
    max_exact, steps = _bucket_thresholds()
    shape = (2 * tq, tq)
    rel = lax.broadcasted_iota(I32, shape, 0) - lax.broadcasted_iota(I32, shape, 1) - tq
    n = jnp.abs(rel)
    large = jnp.full(shape, max_exact, I32)
    for t in steps:
        large = large + (n >= t).astype(I32)
    bucket = jnp.where(rel > 0, nb, 0) + jnp.where(n < max_exact, n, large)
    for h in range(N_HEADS):
        val = jnp.zeros(shape, F32)
        for b in range(N_BUCKETS):
            val = jnp.where(bucket == b, rb_ref[b, h], val)
        o_ref[h] = (val - rb_ref[nb - 1, h]) * LOG2E


def _near_bias(rel_bias, tq):
    return pl.pallas_call(
        functools.partial(_bias_body, tq=tq),
        in_specs=[pl.BlockSpec(memory_space=pltpu.SMEM)],
        out_specs=pl.BlockSpec(memory_space=pltpu.VMEM),
        out_shape=jax.ShapeDtypeStruct((N_HEADS, 2 * tq, tq), F32),
        compiler_params=pltpu.CompilerParams(vmem_limit_bytes=VMEM_LIMIT_BYTES),
        name="dsa_near_bias",
    )(rel_bias.astype(F32))


def _order_key(x):
    bits = lax.bitcast_convert_type(x, I32)
    return bits ^ ((bits >> 31) | SIGN_BIT)


def _order_key_to_float(key):
    return lax.bitcast_convert_type(key ^ ((~key >> 31) | SIGN_BIT), F32)


def _bit_transpose32(words):
    a = list(words)
    j, m = 16, 0x0000FFFF
    while j:
        mask = jnp.int32(m - (1 << 32) if m >= 1 << 31 else m)
        k = 0
        while k < 32:
            t = (lax.shift_right_logical(a[k], jnp.int32(j)) ^ a[k + j]) & mask
            a[k] = a[k] ^ lax.shift_left(t, jnp.int32(j))
            a[k + j] = a[k + j] ^ t
            k = (k + j + 1) & ~j
        j >>= 1
        m = (m ^ (m << j)) & 0xFFFFFFFF
    return a


def _dsa_body(qd_ref, qi_ref, wq_ref, kd_ref, vt_ref, ki_ref, bias_ref, tri_ref, o_ref,
              sc_ref, plane_ref, qs_ref, m_ref, l_ref, acc_ref, lg_ref, bmax_ref, *, tq, top):
    i = pl.program_id(1)
    shape = (tq, tq)
    key_row = lax.broadcasted_iota(I32, shape, 0)
    qry_col = lax.broadcasted_iota(I32, shape, 1)
    visible = key_row // CHUNK <= qry_col // CHUNK

    w_t = (wq_ref[...] * (IDX_DIM ** -0.5 * IDX_HEADS ** -0.5)).T

    heads_per_vreg = LANES // IDX_DIM

    def score_tile(j):
        start = pl.multiple_of(j * tq, tq)
        ki = [ki_ref[pl.ds(start, tq), c * LANES:(c + 1) * LANES].astype(BF16)
              for c in range(heads_per_vreg)]
        s = jnp.zeros(shape, F32)
        for h in range(IDX_HEADS):
            g, c = divmod(h, heads_per_vreg)
            d = lax.dot_general(ki[c], qi_ref[:, g * LANES:(g + 1) * LANES], _NT,
                                preferred_element_type=F32)
            s = s + w_t[h:h + 1, :] * jnp.maximum(d, 0.0)
        sc_ref[j] = s

    def key_planes(j):
        ukey = _order_key(sc_ref[j])
        planes = _bit_transpose32([ukey[g * SUBLANES:(g + 1) * SUBLANES, :] for g in range(KEY_BITS)])
        for b in range(KEY_BITS):
            plane_ref[b, pl.ds(pl.multiple_of(j * SUBLANES, SUBLANES), SUBLANES), :] = planes[b]

    @pl.when(i == 0)
    def _():
        plane_ref[...] = jnp.zeros(plane_ref.shape, I32)

    def score_step(j, carry):
        key_planes(j)
        score_tile(j + 1)
        return carry

    score_tile(0)
    lax.fori_loop(0, i, score_step, 0)
    key_planes(i)
    sc_ref[i] = jnp.where(visible, sc_ref[i], -jnp.inf)

    n_rows = plane_ref.shape[1]
    block_of_row = lax.broadcasted_iota(I32, (n_rows, tq), 0) // SUBLANES
    qry_of_col = lax.broadcasted_iota(I32, (n_rows, tq), 1)
    n_bits = (qry_of_col // CHUNK + 1) * (CHUNK // SUBLANES)
    diag_bits = jnp.where(n_bits >= KEY_BITS, -1, lax.shift_left(jnp.int32(1), n_bits) - 1)
    cand0 = jnp.where(block_of_row < i, -1, jnp.where(block_of_row == i, diag_bits, 0))

    def popcount_rows(words):
        return jnp.sum(lax.population_count(words), axis=0, keepdims=True)

    def bit_step(t, state):
        cand, n_above, thr_bits = state
        b = KEY_BITS - 1 - t
        ones = cand & plane_ref[b]
        n_ones = popcount_rows(ones)
        take = n_above + n_ones >= top
        cand = jnp.where(take, ones, cand ^ ones)
        n_above = jnp.where(take, n_above, n_above + n_ones)
        thr_bits = thr_bits | jnp.where(take, lax.shift_left(jnp.int32(1), b), 0)
        return cand, n_above, thr_bits

    zero = jnp.zeros((1, tq), I32)
    cand, n_above, thr_bits = lax.fori_loop(0, KEY_BITS, bit_step, (cand0, zero, zero))
    qry = lax.broadcasted_iota(I32, (1, tq), 1)
    n_visible = i * tq + (qry // CHUNK + 1) * CHUNK
    wanted = n_visible > top
    thr = jnp.where(wanted, _order_key_to_float(thr_bits), jnp.finfo(F32).min)
    tied = jnp.logical_and(wanted, n_above + popcount_rows(cand) > top)
    c_hi = n_above

    ones = jnp.ones((2 * SUBLANES, tq), BF16)

    def plain_mask(j, carry):
        sc_ref[j] = jnp.where(sc_ref[j] >= thr, 0.0, NEG_BIG)
        return carry

    def tie_mask(j, seen):
        s = sc_ref[j]
        equal = s == thr
        rank = jnp.dot(tri_ref[...], equal.astype(BF16), preferred_element_type=F32) + seen
        quota = jnp.where(tied, (top - c_hi).astype(F32), jnp.inf)
        keep_equal = jnp.where(rank < quota, 0.0, NEG_BIG)
        sc_ref[j] = jnp.where(s > thr, 0.0, jnp.where(equal, keep_equal, NEG_BIG))
        return seen + jnp.sum(equal.astype(F32), axis=0, keepdims=True)

    def with_ties():
        lax.fori_loop(0, i + 1, tie_mask, jnp.zeros((1, tq), F32))
        return jnp.int32(0)

    def without_ties():
        return lax.fori_loop(0, i + 1, plain_mask, jnp.int32(0))

    lax.cond(jnp.max(tied.astype(I32)) > 0, with_ties, without_ties)

    qs_ref[...] = (qd_ref[...].astype(F32) * (HEAD_DIM ** -0.5 * LOG2E)).astype(BF16)
    m_ref[...] = jnp.full(m_ref.shape, NEG_BIG, F32)
    l_ref[...] = jnp.zeros(l_ref.shape, F32)
    acc_ref[...] = jnp.zeros(acc_ref.shape, F32)

    heads = [slice(h * HEAD_DIM, (h + 1) * HEAD_DIM) for h in range(N_HEADS)]

    far, prev, diag = None, 0, 1

    def logits(u, near, slot):
        j = i - u
        start = pl.multiple_of(j * tq, tq)
        mask = sc_ref[j]
        block_max = []
        for h, hs in enumerate(heads):
            lg = lax.dot_general(kd_ref[pl.ds(start, tq), hs], qs_ref[:, hs], _NT,
                                 preferred_element_type=F32)
            if near is not None:
                lg = lg + bias_ref[h, near * tq:(near + 1) * tq, :]
            lg = lg + mask
            lg_ref[slot, h] = lg
            block_max.append(jnp.max(lg, axis=0, keepdims=True))
        bmax_ref[slot] = jnp.concatenate(block_max, axis=0)

    def values(u, slot):
        j = i - u
        m_old = m_ref[...]
        m_new = jnp.maximum(m_old, bmax_ref[slot])
        alpha = jnp.exp2(m_old - m_new)
        m_ref[...] = m_new
        denom = []
        for h, hs in enumerate(heads):
            p = jnp.exp2(lg_ref[slot, h] - m_new[h:h + 1, :]).astype(BF16)
            v_ext = jnp.concatenate([vt_ref[j, hs, :], ones], axis=0)
            pv = jnp.dot(v_ext, p, preferred_element_type=F32)
            acc_ref[h] = alpha[h:h + 1, :] * acc_ref[h] + pv[:HEAD_DIM]
            denom.append(pv[HEAD_DIM:HEAD_DIM + 1])
        l_ref[...] = alpha * l_ref[...] + jnp.concatenate(denom, axis=0)

    def even_step(u, near_a, near_b):
        logits(u - 1, near_a, 1)
        values(u, 0)
        logits(u - 2, near_b, 0)
        values(u - 1, 1)

    @pl.when(i == 0)
    def _():
        logits(0, diag, 0)
        values(0, 0)

    @pl.when(i == 1)
    def _():
        logits(1, prev, 1)

    @pl.when(jnp.logical_and(i >= 2, i % 2 == 1))
    def _():
        logits(i, far, 1)
        logits(i - 1, far, 0)
        values(i, 1)

    @pl.when(jnp.logical_and(i >= 2, i % 2 == 0))
    def _():
        logits(i, far, 0)

    def far_pair(k, carry):
        even_step(2 * (i // 2 - k), far, far)
        return carry

    lax.fori_loop(0, i // 2 - 1, far_pair, 0)

    @pl.when(i >= 2)
    def _():
        even_step(2, prev, diag)

    @pl.when(i == 1)
    def _():
        logits(0, diag, 0)
        values(1, 1)

    @pl.when(i >= 1)
    def _():
        values(0, 0)

    for h in range(N_HEADS):
        o = acc_ref[h] / l_ref[h:h + 1, :]
        o_ref[:, h * HEAD_DIM:(h + 1) * HEAD_DIM] = o.T.astype(o_ref.dtype)


def _dsa_attention(main, v_t, small, bias, batch, seq, *, qd_col, kd_col, qi_col, v_branch, tq):
    nq = seq // tq
    width = N_HEADS * HEAD_DIM
    assert IDX_HEADS * IDX_DIM == width and tq == KEY_BITS * SUBLANES and tq % CHUNK == 0
    key_copies = LANES // IDX_DIM
    top = min(TOPK_MAX, seq // 4)
    tri = (jnp.arange(tq)[None, :] < jnp.arange(tq)[:, None]).astype(BF16)
    resident = pl.Buffered(1)
    return pl.pallas_call(
        functools.partial(_dsa_body, tq=tq, top=top),
        grid=(batch, nq),
        in_specs=[
            pl.BlockSpec((tq, width), lambda b, i: (b * nq + i, qd_col)),
            pl.BlockSpec((tq, IDX_HEADS * IDX_DIM), lambda b, i: (b * nq + i, qi_col)),
            pl.BlockSpec((tq, LANES), lambda b, i: (b * nq + i, key_copies)),
            pl.BlockSpec((seq, width), lambda b, i: (b, kd_col), pipeline_mode=resident),
            pl.BlockSpec((None, None, nq, width, tq), lambda b, i: (v_branch, b, 0, 0, 0),
                         pipeline_mode=resident),
            pl.BlockSpec((seq, key_copies * LANES), lambda b, i: (b, 0), pipeline_mode=resident),
            pl.BlockSpec((N_HEADS, 2 * tq, tq), lambda b, i: (0, 0, 0), pipeline_mode=resident),
            pl.BlockSpec((tq, tq), lambda b, i: (0, 0), pipeline_mode=resident),
        ],
        out_specs=pl.BlockSpec((tq, width), lambda b, i: (b * nq + i, 0)),
        out_shape=jax.ShapeDtypeStruct((batch * seq, width), BF16),
        scratch_shapes=[
            pltpu.VMEM((nq, tq, tq), F32),
            pltpu.VMEM((KEY_BITS, nq * SUBLANES, tq), I32),
            pltpu.VMEM((tq, width), BF16),
            pltpu.VMEM((N_HEADS, tq), F32),
            pltpu.VMEM((N_HEADS, tq), F32),
            pltpu.VMEM((N_HEADS, HEAD_DIM, tq), F32),
            pltpu.VMEM((2, N_HEADS, tq, tq), F32),
            pltpu.VMEM((2, N_HEADS, tq), F32),
        ],
        compiler_params=_params("parallel", "arbitrary"),
        name="dsa_attention",
    )(main, main, small, main, v_t, small, bias, tri)


def _merge_cross_body(osb_ref, ods_ref, wsb_ref, wds_ref, gsb_ref, gds_ref, bsb_ref, bds_ref, wo_ref,
                      x_ref, gc_ref, wq_ref, km_ref, vm_ref, wco_ref, gn_ref, o_ref, hn_ref):
    p_sb = jnp.dot(osb_ref[...], wsb_ref[...], preferred_element_type=F32)
    p_ds = jnp.dot(ods_ref[...], wds_ref[...], preferred_element_type=F32)
    g_sb = jax.nn.sigmoid(gsb_ref[...].astype(F32) + bsb_ref[...])
    g_ds = jax.nn.sigmoid(gds_ref[...].astype(F32) + bds_ref[...])
    merged = (g_sb * p_sb + g_ds * p_ds).astype(BF16)
    x1 = x_ref[...] + jnp.dot(merged, wo_ref[...], preferred_element_type=F32)

    h = _rms(x1, gc_ref[...]).astype(BF16)
    q = jnp.dot(h, wq_ref[...], preferred_element_type=F32) * HEAD_DIM ** -0.5
    q = q.astype(BF16)
    outs = []
    for hh in range(MEM_HEADS):
        hs = slice(hh * HEAD_DIM, (hh + 1) * HEAD_DIM)
        lg = lax.dot_general(q[:, hs], km_ref[:, hs], _NT, preferred_element_type=F32)
        p = jnp.exp(lg - jnp.max(lg, axis=1, keepdims=True))
        o = jnp.dot(p.astype(BF16), vm_ref[:, hs], preferred_element_type=F32)
        outs.append((o / jnp.sum(p, axis=1, keepdims=True)).astype(BF16))
    o = jnp.concatenate(outs, axis=1)
    x2 = x1 + jnp.dot(o, wco_ref[...], preferred_element_type=F32)
    o_ref[...] = x2
    hn_ref[...] = _rms(x2, gn_ref[...]).astype(hn_ref.dtype)


def _merge_cross(o_sb, o_ds, w_sb, w_ds, w_out, proj, gate_offset, b_gate, x,
                 kv, g_cross, w_cq, w_co, g_next, batch, seq, *, tm):
    m, k = o_sb.shape
    d = w_sb.shape[1]
    n_mem = kv.shape[0] // batch
    width = MEM_HEADS * HEAD_DIM
    tm = min(tm, seq)
    assert gate_offset % d == 0 and seq % tm == 0
    g = gate_offset // d
    nt = seq // tm
    b_gate = b_gate.reshape(1, 2 * d).astype(F32)
    resident = pl.Buffered(1)

    def rows(width_, col=0):
        return pl.BlockSpec((tm, width_), lambda b, i: (b * nt + i, col))

    def whole(shape, *block):
        return pl.BlockSpec(shape, lambda b, i: block or (0,) * len(shape), pipeline_mode=resident)

    return pl.pallas_call(
        _merge_cross_body,
        grid=(batch, nt),
        in_specs=[
            rows(k), rows(k), whole((k, d)), whole((k, d)),
            rows(d, g), rows(d, g + 1), whole((1, d)), whole((1, d), 0, 1),
            whole((d, d)), rows(d),
            whole((1, d)), whole((d, width)),
            pl.BlockSpec((n_mem, width), lambda b, i: (b, 0)),
            pl.BlockSpec((n_mem, width), lambda b, i: (b, 1)),
            whole((width, d)), whole((1, d)),
        ],
        out_specs=[rows(d), rows(d)],
        out_shape=[jax.ShapeDtypeStruct((m, d), F32), jax.ShapeDtypeStruct((m, d), BF16)],
        compiler_params=_params("parallel", "parallel"),
        name="merge_out_cross",
    )(o_sb, o_ds, w_sb, w_ds, proj, proj, b_gate, b_gate, w_out, x,
      g_cross.reshape(1, d).astype(F32), w_cq, kv, kv, w_co, g_next.reshape(1, d).astype(F32))


def _delayed(u, tail, shift):
    rolled = pltpu.roll(u, shift, axis=0)
    row = lax.broadcasted_iota(I32, tail.shape, 0)
    head = jnp.where(row < shift, pltpu.roll(tail, shift, axis=0), rolled[:SUBLANES])
    return jnp.concatenate([head, rolled[SUBLANES:]], axis=0)


def _ffn_up_body(h_ref, wa_ref, wv_ref, cwa_ref, cwv_ref, cba_ref, cbv_ref, wd_ref, o_ref, wdb_ref,
                 wab_ref, wvb_ref, halo_ref, *, tiles_per_seq):
    i = pl.program_id(1)
    _round_chunks([wd_ref], [wdb_ref])

    @pl.when(i == 0)
    def _():
        wab_ref[...] = wa_ref[...].astype(BF16)
        wvb_ref[...] = wv_ref[...].astype(BF16)

    h = h_ref[...]
    tm = h.shape[0]
    sequence_start = i % tiles_per_seq == 0

    def conv(wb_ref, cw_ref, cb_ref, slot):
        u = jnp.dot(h, wb_ref[...], preferred_element_type=F32)
        tail = jnp.where(sequence_start, 0.0, halo_ref[slot])
        halo_ref[slot] = u[tm - SUBLANES:, :]
        c = cb_ref[...] + cw_ref[CONV_WIDTH - 1:CONV_WIDTH, :] * u
        for tap in range(CONV_WIDTH - 1):
            c = c + cw_ref[tap:tap + 1, :] * _delayed(u, tail, CONV_WIDTH - 1 - tap)
        return c

    a = conv(wab_ref, cwa_ref, cba_ref, 0)
    val = conv(wvb_ref, cwv_ref, cbv_ref, 1)
    o_ref[...] = (jax.nn.gelu(a) * val).astype(o_ref.dtype)


def _ffn_up_gate(h, w_up, conv_w, conv_b, w_down, seq, *, tm=ROW_TILE, tn=FFN_COL_TILE):
    m, d = h.shape
    two_ff = w_up.shape[1]
    d_ff = two_ff // 2
    tm, tn = min(tm, seq), min(tn, d_ff)
    assert seq % tm == 0 and d_ff % tn == 0 and tm >= SUBLANES >= CONV_WIDTH - 1
    nf, nt = d_ff // tn, m // tm
    conv_w = conv_w.astype(F32)
    conv_b = conv_b.reshape(1, two_ff).astype(F32)
    rider_in, rider_out, rider_shape = _rounding_riders([w_down], nf * nt, lambda j, i: j * nt + i)
    return pl.pallas_call(
        functools.partial(_ffn_up_body, tiles_per_seq=seq // tm),
        grid=(nf, nt),
        in_specs=[
            pl.BlockSpec((tm, d), lambda j, i: (i, 0)),
            pl.BlockSpec((d, tn), lambda j, i: (0, j)),
            pl.BlockSpec((d, tn), lambda j, i: (0, nf + j)),
            pl.BlockSpec((CONV_WIDTH, tn), lambda j, i: (0, j)),
            pl.BlockSpec((CONV_WIDTH, tn), lambda j, i: (0, nf + j)),
            pl.BlockSpec((1, tn), lambda j, i: (0, j)),
            pl.BlockSpec((1, tn), lambda j, i: (0, nf + j)),
        ] + rider_in,
        out_specs=[pl.BlockSpec((tm, tn), lambda j, i: (i, j))] + rider_out,
        out_shape=[jax.ShapeDtypeStruct((m, d_ff), BF16)] + rider_shape,
        scratch_shapes=[pltpu.VMEM((d, tn), BF16), pltpu.VMEM((d, tn), BF16),
                        pltpu.VMEM((2, SUBLANES, tn), F32)],
        compiler_params=_params("parallel", "arbitrary"),
        name="ffn_up_conv_gate",
    )(h, w_up, w_up, conv_w, conv_w, conv_b, conv_b, w_down)


def _ffn_down_body(a_ref, w_ref, x_ref, g_ref, o_ref, *, final_norm):
    k = pl.program_id(1)

    @pl.when(k == 0)
    def _():
        o_ref[...] = x_ref[...]

    o_ref[...] += jnp.dot(a_ref[...], w_ref[...], preferred_element_type=F32)

    if final_norm:
        @pl.when(k == pl.num_programs(1) - 1)
        def _():
            o_ref[...] = _rms(o_ref[...], g_ref[...])


def _ffn_down(a, w, x, g_final, *, tm=ROW_TILE, tk=FFN_K_TILE):
    m, kdim = a.shape
    d = w.shape[1]
    tm, tk = min(tm, m), min(tk, kdim)
    assert m % tm == 0 and kdim % tk == 0
    final_norm = g_final is not None
    g = (g_final if final_norm else jnp.ones((d,), F32)).reshape(1, d).astype(F32)
    return pl.pallas_call(
        functools.partial(_ffn_down_body, final_norm=final_norm),
        grid=(m // tm, kdim // tk),
        in_specs=[
            pl.BlockSpec((tm, tk), lambda i, k: (i, k)),
            pl.BlockSpec((tk, d), lambda i, k: (k, 0)),
            pl.BlockSpec((tm, d), lambda i, k: (i, 0)),
            pl.BlockSpec((1, d), lambda i, k: (0, 0)),
        ],
        out_specs=pl.BlockSpec((tm, d), lambda i, k: (i, 0)),
        out_shape=jax.ShapeDtypeStruct((m, d), F32),
        compiler_params=_params("parallel", "arbitrary"),
        name="ffn_down",
    )(a, w, x, g)


def _layer(x, mem, g_mix, w_in, b_gate, w_proj_sb, w_proj_dsa, w_out, rel_bias,
           g_cross, g_mem, w_cq, w_ckv, w_co, g_ffn, w_up, conv_w, conv_b, w_down, g_final, batch, seq):
    d = x.shape[1]
    width = N_HEADS * HEAD_DIM
    idx_w = IDX_HEADS * IDX_DIM
    o_qi = 6 * width
    o_ki = o_qi + idx_w
    o_wi = o_ki + IDX_DIM
    o_g = o_wi + IDX_HEADS

    zeros = jnp.zeros((d, LANES - IDX_DIM), F32)
    w_small = jnp.concatenate([
        w_in[:, o_ki:o_wi], zeros, zeros, w_in[:, o_ki:o_wi],
        jnp.pad(w_in[:, o_wi:o_g], ((0, 0), (0, LANES - IDX_HEADS)))], axis=1).astype(BF16)
    h, small = _rmsnorm_proj(x, g_mix, w_small, name="mixer_norm_index_proj", proj_dtype=F32,
                             keep_rows=True)
    w_in_t = w_in.T
    tq = ATT_BLOCK
    nq = seq // tq
    q_sb, k_sb, v_sb, q_ds, k_ds, v_ds, q_ix = (g * width for g in range(o_ki // width))
    gate_rows = tuple(o_g + g * width for g in range(2 * d // width))
    n_gate = len(gate_rows)
    main = _matmul_ws(h, w_in_t, name="in_proj_main",
                      first_rows=gate_rows + (q_sb, k_sb, q_ds, k_ds, q_ix),
                      out_dtype=BF16, tm=PROJ_ROWS, tn=width)
    v_t = _matmul_ws_t(h, w_in_t, name="in_proj_values", first_rows=(v_sb, v_ds), tn=width, tq=tq,
                       tm=PROJ_ROWS)
    v_t = v_t.reshape(2, batch, nq, width, tq)

    o_sb, w_proj_sb, w_proj_dsa, w_out, w_cq, w_co = _sb_attention(
        main, v_t, batch, seq, q_col=n_gate, k_col=n_gate + 1, v_branch=0, tq=tq,
        round_weights=(w_proj_sb, w_proj_dsa, w_out, w_cq, w_co))
    bias = _near_bias(rel_bias, tq)
    o_ds = _dsa_attention(main, v_t, small, bias, batch, seq, qd_col=n_gate + 2, kd_col=n_gate + 3,
                          qi_col=n_gate + 4, v_branch=1, tq=tq)

    kv = _rmsnorm_proj(mem, g_mem, w_ckv, name="mem_norm_kv_proj", proj_dtype=BF16, keep_rows=False)
    x, h_ffn = _merge_cross(o_sb, o_ds, w_proj_sb, w_proj_dsa, w_out, main, 0, b_gate, x, kv, g_cross,
                            w_cq, w_co, g_ffn, batch, seq, tm=MERGE_ROWS)

    act, w_down = _ffn_up_gate(h_ffn, w_up, conv_w, conv_b, w_down, seq)
    return _ffn_down(act, w_down, x, g_final)


def kernel(x, mem, g_mix, w_in, b_gate, w_proj_sb, w_proj_dsa, w_out, rel_bias, g_cross, g_mem,
           w_cq, w_ckv, w_co, g_ffn, w_up, conv_w, conv_b, w_down, g_final):
    batch, seq, d = x.shape
    h = x.reshape(batch * seq, d)
    mem2 = mem.reshape(batch * mem.shape[1], d)
    depth = g_mix.shape[0]
    for l in range(depth):
        h = _layer(h, mem2, g_mix[l], w_in[l], b_gate[l], w_proj_sb[l], w_proj_dsa[l], w_out[l],
                   rel_bias, g_cross[l], g_mem[l], w_cq[l], w_ckv[l], w_co[l], g_ffn[l], w_up[l],
                   conv_w[l], conv_b[l], w_down[l], g_final if l == depth - 1 else None, batch, seq)
    return h.reshape(batch, seq, d)
```

```python
import functools

import jax
import jax.numpy as jnp
from jax import lax
from jax.experimental import pallas as pl
from jax.experimental.pallas import tpu as pltpu

F32, BF16, I32 = jnp.float32, jnp.bfloat16, jnp.int32

EPS = 1e-6
HEAD_DIM = 128
N_HEADS = 8
IDX_HEADS = 16
IDX_DIM = 64
CHUNK = 64
TOPK_MAX = 256
N_BUCKETS = 32
MAX_DISTANCE = 128
MEM_HEADS = 4
CONV_WIDTH = 3

LANES = 128
SUBLANES = 8
VMEM_LIMIT_BYTES = 62 * 1024 * 1024
NEG_BIG = -1e30
EXP2_UNDERFLOW = -151.0
LOG2E = 1.4426950408889634
KEY_BITS = 32
SIGN_BIT = -(2 ** 31)

PROJ_ROWS = 2048
ROW_TILE = 1024
NORM_ROWS = 1024
MERGE_ROWS = 512
FFN_COL_TILE = 512
FFN_K_TILE = 2048
ATT_BLOCK = KEY_BITS * SUBLANES
RADIX_GROUPS = 4

_NT = (((1,), (1,)), ((), ()))


def _params(*sem):
    return pltpu.CompilerParams(dimension_semantics=sem, vmem_limit_bytes=VMEM_LIMIT_BYTES)


def _rms(x, g):
    inv = lax.rsqrt(jnp.mean(x * x, axis=-1, keepdims=True) + EPS)
    return x * inv * g


def _rounding_riders(weights, n_steps, step_index):
    in_specs, out_specs, out_shape = [], [], []
    for w in weights:
        rows, cols = w.shape
        assert rows % n_steps == 0 and (rows // n_steps) % (2 * SUBLANES) == 0
        for specs in (in_specs, out_specs):
            specs.append(pl.BlockSpec((rows // n_steps, cols), lambda *ids: (step_index(*ids), 0)))
        out_shape.append(jax.ShapeDtypeStruct(w.shape, BF16))
    return in_specs, out_specs, out_shape


def _round_chunks(src_refs, dst_refs):
    for src, dst in zip(src_refs, dst_refs):
        dst[...] = src[...].astype(BF16)


def _mm_ws_body(a_ref, wt_ref, o_ref, wb_ref):
    @pl.when(pl.program_id(1) == 0)
    def _():
        wb_ref[...] = wt_ref[...].astype(BF16)

    o_ref[...] = lax.dot_general(a_ref[...], wb_ref[...], _NT,
                                 preferred_element_type=F32).astype(o_ref.dtype)


def _row_window(first_rows, tn, k):
    assert all(r % SUBLANES == 0 for r in first_rows)

    def index_map(j, i):
        row = jnp.int32(first_rows[0])
        for step, first in enumerate(first_rows[1:], start=1):
            row = jnp.where(j >= step, first, row)
        return pl.multiple_of(row, SUBLANES), 0

    return pl.BlockSpec((pl.Element(tn), pl.Element(k)), index_map)


def _matmul_ws(a, w_t, *, name, first_rows, out_dtype, tm, tn):
    m, k = a.shape
    tm = min(tm, m)
    assert m % tm == 0 and max(first_rows) + tn <= w_t.shape[0] and w_t.shape[1] == k
    return pl.pallas_call(
        _mm_ws_body,
        grid=(len(first_rows), m // tm),
        in_specs=[pl.BlockSpec((tm, k), lambda j, i: (i, 0)), _row_window(first_rows, tn, k)],
        out_specs=pl.BlockSpec((tm, tn), lambda j, i: (i, j)),
        out_shape=jax.ShapeDtypeStruct((m, len(first_rows) * tn), out_dtype),
        scratch_shapes=[pltpu.VMEM((tn, k), BF16)],
        compiler_params=_params("parallel", "arbitrary"),
        name=name,
    )(a, w_t)


def _mm_ws_t_body(wt_ref, a_ref, o_ref, wb_ref):
    @pl.when(pl.program_id(1) == 0)
    def _():
        wb_ref[...] = wt_ref[...].astype(BF16)

    res = lax.dot_general(wb_ref[...], a_ref[...], _NT, preferred_element_type=F32)
    n_blocks, _, tq = o_ref.shape
    for blk in range(n_blocks):
        o_ref[blk] = res[:, blk * tq:(blk + 1) * tq].astype(o_ref.dtype)


def _matmul_ws_t(a, w_t, *, name, first_rows, tn, tq, tm):
    m, k = a.shape
    tm = min(tm, m)
    assert m % tm == 0 and tm % tq == 0 and max(first_rows) + tn <= w_t.shape[0] and w_t.shape[1] == k
    per_step = tm // tq
    return pl.pallas_call(
        _mm_ws_t_body,
        grid=(len(first_rows), m // tm),
        in_specs=[_row_window(first_rows, tn, k), pl.BlockSpec((tm, k), lambda j, i: (i, 0))],
        out_specs=pl.BlockSpec((None, per_step, tn, tq), lambda j, i: (j, i, 0, 0)),
        out_shape=jax.ShapeDtypeStruct((len(first_rows), m // tq, tn, tq), BF16),
        scratch_shapes=[pltpu.VMEM((tn, k), BF16)],
        compiler_params=_params("parallel", "arbitrary"),
        name=name,
    )(w_t, a)


def _norm_proj_body(x_ref, g_ref, w_ref, *out_refs):
    h = _rms(x_ref[...], g_ref[...]).astype(BF16)
    p_ref = out_refs[-1]
    p_ref[...] = jnp.dot(h, w_ref[...].astype(BF16), preferred_element_type=F32).astype(p_ref.dtype)
    if len(out_refs) == 2:
        out_refs[0][...] = h


def _rmsnorm_proj(x, g, w, *, name, proj_dtype, keep_rows, tm=NORM_ROWS):
    m, d = x.shape
    n = w.shape[1]
    tm = min(tm, m)
    assert m % tm == 0
    out_specs = [pl.BlockSpec((tm, n), lambda i: (i, 0))]
    out_shape = [jax.ShapeDtypeStruct((m, n), proj_dtype)]
    if keep_rows:
        out_specs.insert(0, pl.BlockSpec((tm, d), lambda i: (i, 0)))
        out_shape.insert(0, jax.ShapeDtypeStruct((m, d), BF16))
    out = pl.pallas_call(
        _norm_proj_body,
        grid=(m // tm,),
        in_specs=[pl.BlockSpec((tm, d), lambda i: (i, 0)), pl.BlockSpec((1, d), lambda i: (0, 0)),
                  pl.BlockSpec((d, n), lambda i: (0, 0))],
        out_specs=out_specs,
        out_shape=out_shape,
        compiler_params=_params("parallel"),
        name=name,
    )(x, g.reshape(1, d).astype(F32), w)
    return out if keep_rows else out[0]


def _sb_body(q_ref, k_ref, vt_ref, tri_ref, *refs, tq, n_riders):
    o_ref, qs_ref, z_ref, lb_ref, wb_ref, acc_ref = refs[n_riders], *refs[2 * n_riders + 1:]
    _round_chunks(refs[:n_riders], refs[n_riders + 1:2 * n_riders + 1])
    i = pl.program_id(1)
    shape = (tq, tq)
    before = lax.broadcasted_iota(I32, shape, 0) < lax.broadcasted_iota(I32, shape, 1)
    heads = [slice(h * HEAD_DIM, (h + 1) * HEAD_DIM) for h in range(N_HEADS)]
    qs_ref[...] = (q_ref[...].astype(F32) * (HEAD_DIM ** -0.5 * LOG2E)).astype(BF16)
    acc_ref[...] = jnp.zeros(acc_ref.shape, F32)

    def tile(j, carry, diagonal):
        start = pl.multiple_of(j * tq, tq)
        for h, hs in enumerate(heads):
            z = lax.dot_general(k_ref[pl.ds(start, tq), hs], qs_ref[:, hs], _NT,
                                preferred_element_type=F32)
            neg_z = -z
            log_keep = jnp.minimum(neg_z, 0.0) - jnp.log2(1.0 + jnp.exp2(jnp.minimum(z, neg_z)))
            if diagonal:
                log_keep = jnp.where(before, log_keep, 0.0)
            z_ref[h] = z
            lb_ref[h] = log_keep.astype(BF16)
        new_carry = []
        for h in range(N_HEADS):
            c = jnp.dot(tri_ref[...], lb_ref[h], preferred_element_type=F32) + carry[h:h + 1, :]
            w = jnp.exp2(z_ref[h] + c)
            if diagonal:
                w = jnp.where(before, w, 0.0)
            wb_ref[h] = w.astype(BF16)
            new_carry.append(c[0:1, :])
        for h, hs in enumerate(heads):
            acc_ref[h] += jnp.dot(vt_ref[j, hs, :], wb_ref[h], preferred_element_type=F32)
        return jnp.concatenate(new_carry, axis=0)

    carry = tile(i, jnp.zeros((N_HEADS, tq), F32), True)

    def cond(state):
        j, live, _ = state
        return jnp.logical_and(j >= 0, live > EXP2_UNDERFLOW)

    def body(state):
        j, _, carry = state
        carry = tile(j, carry, False)
        return j - 1, jnp.max(carry), carry

    lax.while_loop(cond, body, (i - 1, jnp.max(carry), carry))
    for h, hs in enumerate(heads):
        o_ref[:, hs] = acc_ref[h].T.astype(o_ref.dtype)


def _sb_attention(qk, v_t, batch, seq, *, q_col, k_col, v_branch, tq, round_weights=()):
    nq = seq // tq
    width = N_HEADS * HEAD_DIM
    tri = (jnp.arange(tq)[None, :] >= jnp.arange(tq)[:, None]).astype(BF16)
    resident = pl.Buffered(1)
    rider_in, rider_out, rider_shape = _rounding_riders(round_weights, batch * nq,
                                                        lambda b, i: b * nq + i)
    return pl.pallas_call(
        functools.partial(_sb_body, tq=tq, n_riders=len(round_weights)),
        grid=(batch, nq),
        in_specs=[
            pl.BlockSpec((tq, width), lambda b, i: (b * nq + i, q_col)),
            pl.BlockSpec((seq, width), lambda b, i: (b, k_col), pipeline_mode=resident),
            pl.BlockSpec((None, None, nq, width, tq), lambda b, i: (v_branch, b, 0, 0, 0),
                         pipeline_mode=resident),
            pl.BlockSpec((tq, tq), lambda b, i: (0, 0), pipeline_mode=resident),
        ] + rider_in,
        out_specs=[pl.BlockSpec((tq, width), lambda b, i: (b * nq + i, 0))] + rider_out,
        out_shape=[jax.ShapeDtypeStruct((batch * seq, width), BF16)] + rider_shape,
        scratch_shapes=[
            pltpu.VMEM((tq, width), BF16),
            pltpu.VMEM((N_HEADS, tq, tq), F32),
            pltpu.VMEM((N_HEADS, tq, tq), BF16),
            pltpu.VMEM((N_HEADS, tq, tq), BF16),
            pltpu.VMEM((N_HEADS, HEAD_DIM, tq), F32),
        ],
        compiler_params=_params("parallel", "arbitrary"),
        name="sb_attention",
    )(qk, qk, v_t, tri, *round_weights)


def _bucket_thresholds():
    nb = N_BUCKETS // 2
    max_exact = nb // 2
    span = nb - max_exact
    out = []
    for k in range(1, span):
        n = max_exact
        while n ** span * max_exact ** k < MAX_DISTANCE ** k * max_exact ** span:
            n += 1
        out.append(n)
    return max_exact, out


def _bias_body(rb_ref, o_ref, *, tq):
    nb = N_BUCKETS // 2
    max_exact, steps = _bucket_thresholds()
    shape = (2 * tq, tq)
    rel = lax.broadcasted_iota(I32, shape, 0) - lax.broadcasted_iota(I32, shape, 1) - tq
    n = jnp.abs(rel)
    large = jnp.full(shape, max_exact, I32)
    for t in steps:
        large = large + (n >= t).astype(I32)
    bucket = jnp.where(rel > 0, nb, 0) + jnp.where(n < max_exact, n, large)
    for h in range(N_HEADS):
        val = jnp.zeros(shape, F32)
        for b in range(N_BUCKETS):
            val = jnp.where(bucket == b, rb_ref[b, h], val)
        o_ref[h] = (val - rb_ref[nb - 1, h]) * LOG2E


def _near_bias(rel_bias, tq):
    return pl.pallas_call(
        functools.partial(_bias_body, tq=tq),
        in_specs=[pl.BlockSpec(memory_space=pltpu.SMEM)],
        out_specs=pl.BlockSpec(memory_space=pltpu.VMEM),
        out_shape=jax.ShapeDtypeStruct((N_HEADS, 2 * tq, tq), F32),
        compiler_params=pltpu.CompilerParams(vmem_limit_bytes=VMEM_LIMIT_BYTES),
        name="dsa_near_bias",
    )(rel_bias.astype(F32))


def _order_key(x):
    bits = lax.bitcast_convert_type(x, I32)
    return bits ^ ((bits >> 31) | SIGN_BIT)


def _order_key_to_float(key):
    return lax.bitcast_convert_type(key ^ ((~key >> 31) | SIGN_BIT), F32)


def _bit_transpose32(words):
    a = list(words)
    j, m = 16, 0x0000FFFF
    while j:
        mask = jnp.int32(m - (1 << 32) if m >= 1 << 31 else m)
        k = 0
        while k < 32:
            t = (lax.shift_right_logical(a[k], jnp.int32(j)) ^ a[k + j]) & mask
            a[k] = a[k] ^ lax.shift_left(t, jnp.int32(j))
            a[k + j] = a[k + j] ^ t
            k = (k + j + 1) & ~j
        j >>= 1
        m = (m ^ (m << j)) & 0xFFFFFFFF
    return a


def _dsa_body(qd_ref, qi_ref, wq_ref, kd_ref, vt_ref, ki_ref, bias_ref, tri_ref, o_ref,
              sc_ref, plane_ref, qs_ref, m_ref, l_ref, acc_ref, lg_ref, bmax_ref, *, tq, top):
    i = pl.program_id(1)
    shape = (tq, tq)
    key_row = lax.broadcasted_iota(I32, shape, 0)
    qry_col = lax.broadcasted_iota(I32, shape, 1)
    visible = key_row // CHUNK <= qry_col // CHUNK

    w_t = (wq_ref[...] * (IDX_DIM ** -0.5 * IDX_HEADS ** -0.5)).T

    heads_per_vreg = LANES // IDX_DIM

    def score_tile(j):
        start = pl.multiple_of(j * tq, tq)
        ki = [ki_ref[pl.ds(start, tq), c * LANES:(c + 1) * LANES].astype(BF16)
              for c in range(heads_per_vreg)]
        s = jnp.zeros(shape, F32)
        for h in range(IDX_HEADS):
            g, c = divmod(h, heads_per_vreg)
            d = lax.dot_general(ki[c], qi_ref[:, g * LANES:(g + 1) * LANES], _NT,
                                preferred_element_type=F32)
            s = s + w_t[h:h + 1, :] * jnp.maximum(d, 0.0)
        sc_ref[j] = s

    def key_planes(j):
        ukey = _order_key(sc_ref[j])
        planes = _bit_transpose32([ukey[g * SUBLANES:(g + 1) * SUBLANES, :] for g in range(KEY_BITS)])
        for b in range(KEY_BITS):
            plane_ref[b, pl.ds(pl.multiple_of(j * SUBLANES, SUBLANES), SUBLANES), :] = planes[b]

    @pl.when(i == 0)
    def _():
        plane_ref[...] = jnp.zeros(plane_ref.shape, I32)

    def score_step(j, carry):
        key_planes(j)
        score_tile(j + 1)
        return carry

    score_tile(0)
    lax.fori_loop(0, i, score_step, 0)
    key_planes(i)
    sc_ref[i] = jnp.where(visible, sc_ref[i], -jnp.inf)

    def popcount_rows(words):
        return jnp.sum(lax.population_count(words), axis=0, keepdims=True)

    def radix_select(n_rows):
        block_of_row = lax.broadcasted_iota(I32, (n_rows, tq), 0) // SUBLANES
        qry_of_col = lax.broadcasted_iota(I32, (n_rows, tq), 1)
        n_bits = (qry_of_col // CHUNK + 1) * (CHUNK // SUBLANES)
        diag_bits = jnp.where(n_bits >= KEY_BITS, -1, lax.shift_left(jnp.int32(1), n_bits) - 1)
        cand0 = jnp.where(block_of_row < i, -1, jnp.where(block_of_row == i, diag_bits, 0))

        def bit_step(t, state):
            cand, n_above, thr_bits = state
            b = KEY_BITS - 1 - t
            ones = cand & plane_ref[b, :n_rows]
            n_ones = popcount_rows(ones)
            take = n_above + n_ones >= top
            cand = jnp.where(take, ones, cand ^ ones)
            n_above = jnp.where(take, n_above, n_above + n_ones)
            thr_bits = thr_bits | jnp.where(take, lax.shift_left(jnp.int32(1), b), 0)
            return cand, n_above, thr_bits

        zero = jnp.zeros((1, tq), I32)
        cand, n_above, thr_bits = lax.fori_loop(0, KEY_BITS, bit_step, (cand0, zero, zero))
        return n_above, thr_bits, popcount_rows(cand)

    def select_from(group):
        if group == RADIX_GROUPS - 1:
            return radix_select(plane_ref.shape[1])
        rows = (group + 1) * (plane_ref.shape[1] // RADIX_GROUPS)
        return lax.cond(i * SUBLANES < rows, functools.partial(radix_select, rows),
                        functools.partial(select_from, group + 1))

    n_above, thr_bits, n_equal = select_from(0)
    qry = lax.broadcasted_iota(I32, (1, tq), 1)
    n_visible = i * tq + (qry // CHUNK + 1) * CHUNK
    wanted = n_visible > top
    thr = jnp.where(wanted, _order_key_to_float(thr_bits), jnp.finfo(F32).min)
    tied = jnp.logical_and(wanted, n_above + n_equal > top)
    c_hi = n_above

    ones = jnp.ones((2 * SUBLANES, tq), BF16)

    def plain_mask(j, carry):
        sc_ref[j] = jnp.where(sc_ref[j] >= thr, 0.0, NEG_BIG)
        return carry

    def tie_mask(j, seen):
        s = sc_ref[j]
        equal = s == thr
        rank = jnp.dot(tri_ref[...], equal.astype(BF16), preferred_element_type=F32) + seen
        quota = jnp.where(tied, (top - c_hi).astype(F32), jnp.inf)
        keep_equal = jnp.where(rank < quota, 0.0, NEG_BIG)
        sc_ref[j] = jnp.where(s > thr, 0.0, jnp.where(equal, keep_equal, NEG_BIG))
        return seen + jnp.sum(equal.astype(F32), axis=0, keepdims=True)

    def with_ties():
        lax.fori_loop(0, i + 1, tie_mask, jnp.zeros((1, tq), F32))
        return jnp.int32(0)

    def without_ties():
        return lax.fori_loop(0, i + 1, plain_mask, jnp.int32(0))

    lax.cond(jnp.max(tied.astype(I32)) > 0, with_ties, without_ties)

    qs_ref[...] = (qd_ref[...].astype(F32) * (HEAD_DIM ** -0.5 * LOG2E)).astype(BF16)
    m_ref[...] = jnp.full(m_ref.shape, NEG_BIG, F32)
    l_ref[...] = jnp.zeros(l_ref.shape, F32)
    acc_ref[...] = jnp.zeros(acc_ref.shape, F32)

    heads = [slice(h * HEAD_DIM, (h + 1) * HEAD_DIM) for h in range(N_HEADS)]

    far, prev, diag = None, 0, 1

    def logits(u, near, slot):
        j = i - u
        start = pl.multiple_of(j * tq, tq)
        mask = sc_ref[j]
        block_max = []
        for h, hs in enumerate(heads):
            lg = lax.dot_general(kd_ref[pl.ds(start, tq), hs], qs_ref[:, hs], _NT,
                                 preferred_element_type=F32)
            if near is not None:
                lg = lg + bias_ref[h, near * tq:(near + 1) * tq, :]
            lg = lg + mask
            lg_ref[slot, h] = lg
            block_max.append(jnp.max(lg, axis=0, keepdims=True))
        bmax_ref[slot] = jnp.concatenate(block_max, axis=0)

    def values(u, slot):
        j = i - u
        m_old = m_ref[...]
        m_new = jnp.maximum(m_old, bmax_ref[slot])
        alpha = jnp.exp2(m_old - m_new)
        m_ref[...] = m_new
        denom = []
        for h, hs in enumerate(heads):
            p = jnp.exp2(lg_ref[slot, h] - m_new[h:h + 1, :]).astype(BF16)
            v_ext = jnp.concatenate([vt_ref[j, hs, :], ones], axis=0)
            pv = jnp.dot(v_ext, p, preferred_element_type=F32)
            acc_ref[h] = alpha[h:h + 1, :] * acc_ref[h] + pv[:HEAD_DIM]
            denom.append(pv[HEAD_DIM:HEAD_DIM + 1])
        l_ref[...] = alpha * l_ref[...] + jnp.concatenate(denom, axis=0)

    def even_step(u, near_a, near_b):
        logits(u - 1, near_a, 1)
        values(u, 0)
        logits(u - 2, near_b, 0)
        values(u - 1, 1)

    @pl.when(i == 0)
    def _():
        logits(0, diag, 0)
        values(0, 0)

    @pl.when(i == 1)
    def _():
        logits(1, prev, 1)

    @pl.when(jnp.logical_and(i >= 2, i % 2 == 1))
    def _():
        logits(i, far, 1)
        logits(i - 1, far, 0)
        values(i, 1)

    @pl.when(jnp.logical_and(i >= 2, i % 2 == 0))
    def _():
        logits(i, far, 0)

    def far_pair(k, carry):
        even_step(2 * (i // 2 - k), far, far)
        return carry

    lax.fori_loop(0, i // 2 - 1, far_pair, 0)

    @pl.when(i >= 2)
    def _():
        even_step(2, prev, diag)

    @pl.when(i == 1)
    def _():
        logits(0, diag, 0)
        values(1, 1)

    @pl.when(i >= 1)
    def _():
        values(0, 0)

    for h in range(N_HEADS):
        o = acc_ref[h] / l_ref[h:h + 1, :]
        o_ref[:, h * HEAD_DIM:(h + 1) * HEAD_DIM] = o.T.astype(o_ref.dtype)


def _dsa_attention(main, v_t, small, bias, batch, seq, *, qd_col, kd_col, qi_col, v_branch, tq):
    nq = seq // tq
    width = N_HEADS * HEAD_DIM
    assert IDX_HEADS * IDX_DIM == width and tq == KEY_BITS * SUBLANES and tq % CHUNK == 0
    assert nq % RADIX_GROUPS == 0
    key_copies = LANES // IDX_DIM
    top = min(TOPK_MAX, seq // 4)
    tri = (jnp.arange(tq)[None, :] < jnp.arange(tq)[:, None]).astype(BF16)
    resident = pl.Buffered(1)
    return pl.pallas_call(
        functools.partial(_dsa_body, tq=tq, top=top),
        grid=(batch, nq),
        in_specs=[
            pl.BlockSpec((tq, width), lambda b, i: (b * nq + i, qd_col)),
            pl.BlockSpec((tq, IDX_HEADS * IDX_DIM), lambda b, i: (b * nq + i, qi_col)),
            pl.BlockSpec((tq, LANES), lambda b, i: (b * nq + i, key_copies)),
            pl.BlockSpec((seq, width), lambda b, i: (b, kd_col), pipeline_mode=resident),
            pl.BlockSpec((None, None, nq, width, tq), lambda b, i: (v_branch, b, 0, 0, 0),
                         pipeline_mode=resident),
            pl.BlockSpec((seq, key_copies * LANES), lambda b, i: (b, 0), pipeline_mode=resident),
            pl.BlockSpec((N_HEADS, 2 * tq, tq), lambda b, i: (0, 0, 0), pipeline_mode=resident),
            pl.BlockSpec((tq, tq), lambda b, i: (0, 0), pipeline_mode=resident),
        ],
        out_specs=pl.BlockSpec((tq, width), lambda b, i: (b * nq + i, 0)),
        out_shape=jax.ShapeDtypeStruct((batch * seq, width), BF16),
        scratch_shapes=[
            pltpu.VMEM((nq, tq, tq), F32),
            pltpu.VMEM((KEY_BITS, nq * SUBLANES, tq), I32),
            pltpu.VMEM((tq, width), BF16),
            pltpu.VMEM((N_HEADS, tq), F32),
            pltpu.VMEM((N_HEADS, tq), F32),
            pltpu.VMEM((N_HEADS, HEAD_DIM, tq), F32),
            pltpu.VMEM((2, N_HEADS, tq, tq), F32),
            pltpu.VMEM((2, N_HEADS, tq), F32),
        ],
        compiler_params=_params("parallel", "arbitrary"),
        name="dsa_attention",
    )(main, main, small, main, v_t, small, bias, tri)


def _merge_cross_body(osb_ref, ods_ref, wsb_ref, wds_ref, gsb_ref, gds_ref, bsb_ref, bds_ref, wo_ref,
                      x_ref, gc_ref, wq_ref, km_ref, vm_ref, wco_ref, gn_ref, o_ref, hn_ref):
    p_sb = jnp.dot(osb_ref[...], wsb_ref[...], preferred_element_type=F32)
    p_ds = jnp.dot(ods_ref[...], wds_ref[...], preferred_element_type=F32)
    g_sb = jax.nn.sigmoid(gsb_ref[...].astype(F32) + bsb_ref[...])
    g_ds = jax.nn.sigmoid(gds_ref[...].astype(F32) + bds_ref[...])
    merged = (g_sb * p_sb + g_ds * p_ds).astype(BF16)
    x1 = x_ref[...] + jnp.dot(merged, wo_ref[...], preferred_element_type=F32)

    h = _rms(x1, gc_ref[...]).astype(BF16)
    q = jnp.dot(h, wq_ref[...], preferred_element_type=F32) * HEAD_DIM ** -0.5
    q = q.astype(BF16)
    outs = []
    for hh in range(MEM_HEADS):
        hs = slice(hh * HEAD_DIM, (hh + 1) * HEAD_DIM)
        lg = lax.dot_general(q[:, hs], km_ref[:, hs], _NT, preferred_element_type=F32)
        p = jnp.exp(lg - jnp.max(lg, axis=1, keepdims=True))
        o = jnp.dot(p.astype(BF16), vm_ref[:, hs], preferred_element_type=F32)
        outs.append((o / jnp.sum(p, axis=1, keepdims=True)).astype(BF16))
    o = jnp.concatenate(outs, axis=1)
    x2 = x1 + jnp.dot(o, wco_ref[...], preferred_element_type=F32)
    o_ref[...] = x2
    hn_ref[...] = _rms(x2, gn_ref[...]).astype(hn_ref.dtype)


def _merge_cross(o_sb, o_ds, w_sb, w_ds, w_out, proj, gate_offset, b_gate, x,
                 kv, g_cross, w_cq, w_co, g_next, batch, seq, *, tm):
    m, k = o_sb.shape
    d = w_sb.shape[1]
    n_mem = kv.shape[0] // batch
    width = MEM_HEADS * HEAD_DIM
    tm = min(tm, seq)
    assert gate_offset % d == 0 and seq % tm == 0
    g = gate_offset // d
    nt = seq // tm
    b_gate = b_gate.reshape(1, 2 * d).astype(F32)
    resident = pl.Buffered(1)

    def rows(width_, col=0):
        return pl.BlockSpec((tm, width_), lambda b, i: (b * nt + i, col))

    def whole(shape, *block):
        return pl.BlockSpec(shape, lambda b, i: block or (0,) * len(shape), pipeline_mode=resident)

    return pl.pallas_call(
        _merge_cross_body,
        grid=(batch, nt),
        in_specs=[
            rows(k), rows(k), whole((k, d)), whole((k, d)),
            rows(d, g), rows(d, g + 1), whole((1, d)), whole((1, d), 0, 1),
            whole((d, d)), rows(d),
            whole((1, d)), whole((d, width)),
            pl.BlockSpec((n_mem, width), lambda b, i: (b, 0)),
            pl.BlockSpec((n_mem, width), lambda b, i: (b, 1)),
            whole((width, d)), whole((1, d)),
        ],
        out_specs=[rows(d), rows(d)],
        out_shape=[jax.ShapeDtypeStruct((m, d), F32), jax.ShapeDtypeStruct((m, d), BF16)],
        compiler_params=_params("parallel", "parallel"),
        name="merge_out_cross",
    )(o_sb, o_ds, w_sb, w_ds, proj, proj, b_gate, b_gate, w_out, x,
      g_cross.reshape(1, d).astype(F32), w_cq, kv, kv, w_co, g_next.reshape(1, d).astype(F32))


def _delayed(u, tail, shift):
    rolled = pltpu.roll(u, shift, axis=0)
    row = lax.broadcasted_iota(I32, tail.shape, 0)
    head = jnp.where(row < shift, pltpu.roll(tail, shift, axis=0), rolled[:SUBLANES])
    return jnp.concatenate([head, rolled[SUBLANES:]], axis=0)


def _ffn_up_body(h_ref, wa_ref, wv_ref, cwa_ref, cwv_ref, cba_ref, cbv_ref, wd_ref, o_ref, wdb_ref,
                 wab_ref, wvb_ref, halo_ref, *, tiles_per_seq):
    i = pl.program_id(1)
    _round_chunks([wd_ref], [wdb_ref])

    @pl.when(i == 0)
    def _():
        wab_ref[...] = wa_ref[...].astype(BF16)
        wvb_ref[...] = wv_ref[...].astype(BF16)

    h = h_ref[...]
    tm = h.shape[0]
    sequence_start = i % tiles_per_seq == 0

    def conv(wb_ref, cw_ref, cb_ref, slot):
        u = jnp.dot(h, wb_ref[...], preferred_element_type=F32)
        tail = jnp.where(sequence_start, 0.0, halo_ref[slot])
        halo_ref[slot] = u[tm - SUBLANES:, :]
        c = cb_ref[...] + cw_ref[CONV_WIDTH - 1:CONV_WIDTH, :] * u
        for tap in range(CONV_WIDTH - 1):
            c = c + cw_ref[tap:tap + 1, :] * _delayed(u, tail, CONV_WIDTH - 1 - tap)
        return c

    a = conv(wab_ref, cwa_ref, cba_ref, 0)
    val = conv(wvb_ref, cwv_ref, cbv_ref, 1)
    o_ref[...] = (jax.nn.gelu(a) * val).astype(o_ref.dtype)


def _ffn_up_gate(h, w_up, conv_w, conv_b, w_down, seq, *, tm=ROW_TILE, tn=FFN_COL_TILE):
    m, d = h.shape
    two_ff = w_up.shape[1]
    d_ff = two_ff // 2
    tm, tn = min(tm, seq), min(tn, d_ff)
    assert seq % tm == 0 and d_ff % tn == 0 and tm >= SUBLANES >= CONV_WIDTH - 1
    nf, nt = d_ff // tn, m // tm
    conv_w = conv_w.astype(F32)
    conv_b = conv_b.reshape(1, two_ff).astype(F32)
    rider_in, rider_out, rider_shape = _rounding_riders([w_down], nf * nt, lambda j, i: j * nt + i)
    return pl.pallas_call(
        functools.partial(_ffn_up_body, tiles_per_seq=seq // tm),
        grid=(nf, nt),
        in_specs=[
            pl.BlockSpec((tm, d), lambda j, i: (i, 0)),
            pl.BlockSpec((d, tn), lambda j, i: (0, j)),
            pl.BlockSpec((d, tn), lambda j, i: (0, nf + j)),
            pl.BlockSpec((CONV_WIDTH, tn), lambda j, i: (0, j)),
            pl.BlockSpec((CONV_WIDTH, tn), lambda j, i: (0, nf + j)),
            pl.BlockSpec((1, tn), lambda j, i: (0, j)),
            pl.BlockSpec((1, tn), lambda j, i: (0, nf + j)),
        ] + rider_in,
        out_specs=[pl.BlockSpec((tm, tn), lambda j, i: (i, j))] + rider_out,
        out_shape=[jax.ShapeDtypeStruct((m, d_ff), BF16)] + rider_shape,
        scratch_shapes=[pltpu.VMEM((d, tn), BF16), pltpu.VMEM((d, tn), BF16),
                        pltpu.VMEM((2, SUBLANES, tn), F32)],
        compiler_params=_params("parallel", "arbitrary"),
        name="ffn_up_conv_gate",
    )(h, w_up, w_up, conv_w, conv_w, conv_b, conv_b, w_down)


def _ffn_down_body(a_ref, w_ref, x_ref, g_ref, o_ref, *, final_norm):
    k = pl.program_id(1)

    @pl.when(k == 0)
    def _():
        o_ref[...] = x_ref[...]

    o_ref[...] += jnp.dot(a_ref[...], w_ref[...], preferred_element_type=F32)

    if final_norm:
        @pl.when(k == pl.num_programs(1) - 1)
        def _():
            o_ref[...] = _rms(o_ref[...], g_ref[...])


def _ffn_down(a, w, x, g_final, *, tm=ROW_TILE, tk=FFN_K_TILE):
    m, kdim = a.shape
    d = w.shape[1]
    tm, tk = min(tm, m), min(tk, kdim)
    assert m % tm == 0 and kdim % tk == 0
    final_norm = g_final is not None
    g = (g_final if final_norm else jnp.ones((d,), F32)).reshape(1, d).astype(F32)
    return pl.pallas_call(
        functools.partial(_ffn_down_body, final_norm=final_norm),
        grid=(m // tm, kdim // tk),
        in_specs=[
            pl.BlockSpec((tm, tk), lambda i, k: (i, k)),
            pl.BlockSpec((tk, d), lambda i, k: (k, 0)),
            pl.BlockSpec((tm, d), lambda i, k: (i, 0)),
            pl.BlockSpec((1, d), lambda i, k: (0, 0)),
        ],
        out_specs=pl.BlockSpec((tm, d), lambda i, k: (i, 0)),
        out_shape=jax.ShapeDtypeStruct((m, d), F32),
        compiler_params=_params("parallel", "arbitrary"),
        name="ffn_down",
    )(a, w, x, g)


def _layer(x, mem, g_mix, w_in, b_gate, w_proj_sb, w_proj_dsa, w_out, rel_bias,
           g_cross, g_mem, w_cq, w_ckv, w_co, g_ffn, w_up, conv_w, conv_b, w_down, g_final, batch, seq):
    d = x.shape[1]
    width = N_HEADS * HEAD_DIM
    idx_w = IDX_HEADS * IDX_DIM
    o_qi = 6 * width
    o_ki = o_qi + idx_w
    o_wi = o_ki + IDX_DIM
    o_g = o_wi + IDX_HEADS

    zeros = jnp.zeros((d, LANES - IDX_DIM), F32)
    w_small = jnp.concatenate([
        w_in[:, o_ki:o_wi], zeros, zeros, w_in[:, o_ki:o_wi],
        jnp.pad(w_in[:, o_wi:o_g], ((0, 0), (0, LANES - IDX_HEADS)))], axis=1).astype(BF16)
    h, small = _rmsnorm_proj(x, g_mix, w_small, name="mixer_norm_index_proj", proj_dtype=F32,
                             keep_rows=True)
    w_in_t = w_in.T
    tq = ATT_BLOCK
    nq = seq // tq
    q_sb, k_sb, v_sb, q_ds, k_ds, v_ds, q_ix = (g * width for g in range(o_ki // width))
    gate_rows = tuple(o_g + g * width for g in range(2 * d // width))
    n_gate = len(gate_rows)
    main = _matmul_ws(h, w_in_t, name="in_proj_main",
                      first_rows=gate_rows + (q_sb, k_sb, q_ds, k_ds, q_ix),
                      out_dtype=BF16, tm=PROJ_ROWS, tn=width)
    v_t = _matmul_ws_t(h, w_in_t, name="in_proj_values", first_rows=(v_sb, v_ds), tn=width, tq=tq,
                       tm=PROJ_ROWS)
    v_t = v_t.reshape(2, batch, nq, width, tq)

    o_sb, w_proj_sb, w_proj_dsa, w_out, w_cq, w_co = _sb_attention(
        main, v_t, batch, seq, q_col=n_gate, k_col=n_gate + 1, v_branch=0, tq=tq,
        round_weights=(w_proj_sb, w_proj_dsa, w_out, w_cq, w_co))
    bias = _near_bias(rel_bias, tq)
    o_ds = _dsa_attention(main, v_t, small, bias, batch, seq, qd_col=n_gate + 2, kd_col=n_gate + 3,
                          qi_col=n_gate + 4, v_branch=1, tq=tq)

    kv = _rmsnorm_proj(mem, g_mem, w_ckv, name="mem_norm_kv_proj", proj_dtype=BF16, keep_rows=False)
    x, h_ffn = _merge_cross(o_sb, o_ds, w_proj_sb, w_proj_dsa, w_out, main, 0, b_gate, x, kv, g_cross,
                            w_cq, w_co, g_ffn, batch, seq, tm=MERGE_ROWS)

    act, w_down = _ffn_up_gate(h_ffn, w_up, conv_w, conv_b, w_down, seq)
    return _ffn_down(act, w_down, x, g_final)


def kernel(x, mem, g_mix, w_in, b_gate, w_proj_sb, w_proj_dsa, w_out, rel_bias, g_cross, g_mem,
           w_cq, w_ckv, w_co, g_ffn, w_up, conv_w, conv_b, w_down, g_final):
    batch, seq, d = x.shape
    h = x.reshape(batch * seq, d)
    mem2 = mem.reshape(batch * mem.shape[1], d)
    depth = g_mix.shape[0]
    for l in range(depth):
        h = _layer(h, mem2, g_mix[l], w_in[l], b_gate[l], w_proj_sb[l], w_proj_dsa[l], w_out[l],
                   rel_bias, g_cross[l], g_mem[l], w_cq[l], w_ckv[l], w_co[l], g_ffn[l], w_up[l],
                   conv_w[l], conv_b[l], w_down[l], g_final if l == depth - 1 else None, batch, seq)
    return h.reshape(batch, seq, d)
```

```python
import functools

import jax
import jax.numpy as jnp
from jax import lax
from jax.experimental import pallas as pl
from jax.experimental.pallas import tpu as pltpu

F32, BF16, I32 = jnp.float32, jnp.bfloat16, jnp.int32

EPS = 1e-6
HEAD_DIM = 128
N_HEADS = 8
IDX_HEADS = 16
IDX_DIM = 64
CHUNK = 64
TOPK_MAX = 256
N_BUCKETS = 32
MAX_DISTANCE = 128
MEM_HEADS = 4
CONV_WIDTH = 3

LANES = 128
SUBLANES = 8
VMEM_LIMIT_BYTES = 62 * 1024 * 1024
NEG_BIG = -1e30
EXP2_UNDERFLOW = -151.0
LOG2E = 1.4426950408889634
KEY_BITS = 32
SIGN_BIT = -(2 ** 31)

PROJ_ROWS = 2048
ROW_TILE = 1024
NORM_ROWS = 1024
MERGE_ROWS = 512
FFN_COL_TILE = 512
FFN_K_TILE = 2048
ATT_BLOCK = KEY_BITS * SUBLANES
RADIX_GROUPS = 4

_NT = (((1,), (1,)), ((), ()))


def _params(*sem):
    return pltpu.CompilerParams(dimension_semantics=sem, vmem_limit_bytes=VMEM_LIMIT_BYTES)


def _rms(x, g):
    inv = lax.rsqrt(jnp.mean(x * x, axis=-1, keepdims=True) + EPS)
    return x * inv * g


def _rounding_riders(weights, n_steps, step_index):
    in_specs, out_specs, out_shape = [], [], []
    for w in weights:
        rows, cols = w.shape
        assert rows % n_steps == 0 and (rows // n_steps) % (2 * SUBLANES) == 0
        for specs in (in_specs, out_specs):
            specs.append(pl.BlockSpec((rows // n_steps, cols), lambda *ids: (step_index(*ids), 0)))
        out_shape.append(jax.ShapeDtypeStruct(w.shape, BF16))
    return in_specs, out_specs, out_shape


def _round_chunks(src_refs, dst_refs):
    for src, dst in zip(src_refs, dst_refs):
        dst[...] = src[...].astype(BF16)


def _mm_ws_body(a_ref, wt_ref, o_ref, wb_ref):
    @pl.when(pl.program_id(1) == 0)
    def _():
        wb_ref[...] = wt_ref[...].astype(BF16)

    o_ref[...] = lax.dot_general(a_ref[...], wb_ref[...], _NT,
                                 preferred_element_type=F32).astype(o_ref.dtype)


def _row_window(first_rows, tn, k):
    assert all(r % SUBLANES == 0 for r in first_rows)

    def index_map(j, i):
        row = jnp.int32(first_rows[0])
        for step, first in enumerate(first_rows[1:], start=1):
            row = jnp.where(j >= step, first, row)
        return pl.multiple_of(row, SUBLANES), 0

    return pl.BlockSpec((pl.Element(tn), pl.Element(k)), index_map)


def _matmul_ws(a, w_t, *, name, first_rows, out_dtype, tm, tn):
    m, k = a.shape
    tm = min(tm, m)
    assert m % tm == 0 and max(first_rows) + tn <= w_t.shape[0] and w_t.shape[1] == k
    return pl.pallas_call(
        _mm_ws_body,
        grid=(len(first_rows), m // tm),
        in_specs=[pl.BlockSpec((tm, k), lambda j, i: (i, 0)), _row_window(first_rows, tn, k)],
        out_specs=pl.BlockSpec((tm, tn), lambda j, i: (i, j)),
        out_shape=jax.ShapeDtypeStruct((m, len(first_rows) * tn), out_dtype),
        scratch_shapes=[pltpu.VMEM((tn, k), BF16)],
        compiler_params=_params("parallel", "arbitrary"),
        name=name,
    )(a, w_t)


def _mm_ws_t_body(wt_ref, a_ref, o_ref, wb_ref):
    @pl.when(pl.program_id(1) == 0)
    def _():
        wb_ref[...] = wt_ref[...].astype(BF16)

    res = lax.dot_general(wb_ref[...], a_ref[...], _NT, preferred_element_type=F32)
    n_blocks, _, tq = o_ref.shape
    for blk in range(n_blocks):
        o_ref[blk] = res[:, blk * tq:(blk + 1) * tq].astype(o_ref.dtype)


def _matmul_ws_t(a, w_t, *, name, first_rows, tn, tq, tm):
    m, k = a.shape
    tm = min(tm, m)
    assert m % tm == 0 and tm % tq == 0 and max(first_rows) + tn <= w_t.shape[0] and w_t.shape[1] == k
    per_step = tm // tq
    return pl.pallas_call(
        _mm_ws_t_body,
        grid=(len(first_rows), m // tm),
        in_specs=[_row_window(first_rows, tn, k), pl.BlockSpec((tm, k), lambda j, i: (i, 0))],
        out_specs=pl.BlockSpec((None, per_step, tn, tq), lambda j, i: (j, i, 0, 0)),
        out_shape=jax.ShapeDtypeStruct((len(first_rows), m // tq, tn, tq), BF16),
        scratch_shapes=[pltpu.VMEM((tn, k), BF16)],
        compiler_params=_params("parallel", "arbitrary"),
        name=name,
    )(w_t, a)


def _norm_proj_body(x_ref, g_ref, w_ref, *out_refs):
    h = _rms(x_ref[...], g_ref[...]).astype(BF16)
    p_ref = out_refs[-1]
    p_ref[...] = jnp.dot(h, w_ref[...].astype(BF16), preferred_element_type=F32).astype(p_ref.dtype)
    if len(out_refs) == 2:
        out_refs[0][...] = h


def _rmsnorm_proj(x, g, w, *, name, proj_dtype, keep_rows, tm=NORM_ROWS):
    m, d = x.shape
    n = w.shape[1]
    tm = min(tm, m)
    assert m % tm == 0
    out_specs = [pl.BlockSpec((tm, n), lambda i: (i, 0))]
    out_shape = [jax.ShapeDtypeStruct((m, n), proj_dtype)]
    if keep_rows:
        out_specs.insert(0, pl.BlockSpec((tm, d), lambda i: (i, 0)))
        out_shape.insert(0, jax.ShapeDtypeStruct((m, d), BF16))
    out = pl.pallas_call(
        _norm_proj_body,
        grid=(m // tm,),
        in_specs=[pl.BlockSpec((tm, d), lambda i: (i, 0)), pl.BlockSpec((1, d), lambda i: (0, 0)),
                  pl.BlockSpec((d, n), lambda i: (0, 0))],
        out_specs=out_specs,
        out_shape=out_shape,
        compiler_params=_params("parallel"),
        name=name,
    )(x, g.reshape(1, d).astype(F32), w)
    return out if keep_rows else out[0]


def _sb_body(q_ref, k_ref, vt_ref, tri_ref, *refs, tq, n_riders):
    o_ref, qs_ref, z_ref, lb_ref, wb_ref, acc_ref = refs[n_riders], *refs[2 * n_riders + 1:]
    _round_chunks(refs[:n_riders], refs[n_riders + 1:2 * n_riders + 1])
    i = pl.program_id(1)
    shape = (tq, tq)
    before = lax.broadcasted_iota(I32, shape, 0) < lax.broadcasted_iota(I32, shape, 1)
    heads = [slice(h * HEAD_DIM, (h + 1) * HEAD_DIM) for h in range(N_HEADS)]
    qs_ref[...] = (q_ref[...].astype(F32) * (HEAD_DIM ** -0.5 * LOG2E)).astype(BF16)
    acc_ref[...] = jnp.zeros(acc_ref.shape, F32)

    def tile(j, carry, diagonal):
        start = pl.multiple_of(j * tq, tq)
        for h, hs in enumerate(heads):
            z = lax.dot_general(k_ref[pl.ds(start, tq), hs], qs_ref[:, hs], _NT,
                                preferred_element_type=F32)
            neg_z = -z
            log_keep = jnp.minimum(neg_z, 0.0) - jnp.log2(1.0 + jnp.exp2(jnp.minimum(z, neg_z)))
            if diagonal:
                log_keep = jnp.where(before, log_keep, 0.0)
            z_ref[h] = z
            lb_ref[h] = log_keep.astype(BF16)
        new_carry = []
        for h in range(N_HEADS):
            c = jnp.dot(tri_ref[...], lb_ref[h], preferred_element_type=F32) + carry[h:h + 1, :]
            w = jnp.exp2(z_ref[h] + c)
            if diagonal:
                w = jnp.where(before, w, 0.0)
            wb_ref[h] = w.astype(BF16)
            new_carry.append(c[0:1, :])
        for h, hs in enumerate(heads):
            acc_ref[h] += jnp.dot(vt_ref[j, hs, :], wb_ref[h], preferred_element_type=F32)
        return jnp.concatenate(new_carry, axis=0)

    carry = tile(i, jnp.zeros((N_HEADS, tq), F32), True)

    def cond(state):
        j, live, _ = state
        return jnp.logical_and(j >= 0, live > EXP2_UNDERFLOW)

    def body(state):
        j, _, carry = state
        carry = tile(j, carry, False)
        return j - 1, jnp.max(carry), carry

    lax.while_loop(cond, body, (i - 1, jnp.max(carry), carry))
    for h, hs in enumerate(heads):
        o_ref[:, hs] = acc_ref[h].T.astype(o_ref.dtype)


def _sb_attention(qk, v_t, batch, seq, *, q_col, k_col, v_branch, tq, round_weights=()):
    nq = seq // tq
    width = N_HEADS * HEAD_DIM
    tri = (jnp.arange(tq)[None, :] >= jnp.arange(tq)[:, None]).astype(BF16)
    resident = pl.Buffered(1)
    rider_in, rider_out, rider_shape = _rounding_riders(round_weights, batch * nq,
                                                        lambda b, i: b * nq + i)
    return pl.pallas_call(
        functools.partial(_sb_body, tq=tq, n_riders=len(round_weights)),
        grid=(batch, nq),
        in_specs=[
            pl.BlockSpec((tq, width), lambda b, i: (b * nq + i, q_col)),
            pl.BlockSpec((seq, width), lambda b, i: (b, k_col), pipeline_mode=resident),
            pl.BlockSpec((None, None, nq, width, tq), lambda b, i: (v_branch, b, 0, 0, 0),
                         pipeline_mode=resident),
            pl.BlockSpec((tq, tq), lambda b, i: (0, 0), pipeline_mode=resident),
        ] + rider_in,
        out_specs=[pl.BlockSpec((tq, width), lambda b, i: (b * nq + i, 0))] + rider_out,
        out_shape=[jax.ShapeDtypeStruct((batch * seq, width), BF16)] + rider_shape,
        scratch_shapes=[
            pltpu.VMEM((tq, width), BF16),
            pltpu.VMEM((N_HEADS, tq, tq), F32),
            pltpu.VMEM((N_HEADS, tq, tq), BF16),
            pltpu.VMEM((N_HEADS, tq, tq), BF16),
            pltpu.VMEM((N_HEADS, HEAD_DIM, tq), F32),
        ],
        compiler_params=_params("parallel", "arbitrary"),
        name="sb_attention",
    )(qk, qk, v_t, tri, *round_weights)


def _bucket_thresholds():
    nb = N_BUCKETS // 2
    max_exact = nb // 2
    span = nb - max_exact
    out = []
    for k in range(1, span):
        n = max_exact
        while n ** span * max_exact ** k < MAX_DISTANCE ** k * max_exact ** span:
            n += 1
        out.append(n)
    return max_exact, out


def _bias_body(rb_ref, o_ref, *, tq):
    nb = N_BUCKETS // 2
    max_exact, steps = _bucket_thresholds()
    shape = (2 * tq, tq)
    rel = lax.broadcasted_iota(I32, shape, 0) - lax.broadcasted_iota(I32, shape, 1) - tq
    n = jnp.abs(rel)
    large = jnp.full(shape, max_exact, I32)
    for t in steps:
        large = large + (n >= t).astype(I32)
    bucket = jnp.where(rel > 0, nb, 0) + jnp.where(n < max_exact, n, large)
    for h in range(N_HEADS):
        val = jnp.zeros(shape, F32)
        for b in range(N_BUCKETS):
            val = jnp.where(bucket == b, rb_ref[b, h], val)
        o_ref[h] = (val - rb_ref[nb - 1, h]) * LOG2E


def _near_bias(rel_bias, tq):
    return pl.pallas_call(
        functools.partial(_bias_body, tq=tq),
        in_specs=[pl.BlockSpec(memory_space=pltpu.SMEM)],
        out_specs=pl.BlockSpec(memory_space=pltpu.VMEM),
        out_shape=jax.ShapeDtypeStruct((N_HEADS, 2 * tq, tq), F32),
        compiler_params=pltpu.CompilerParams(vmem_limit_bytes=VMEM_LIMIT_BYTES),
        name="dsa_near_bias",
    )(rel_bias.astype(F32))


def _order_key(x):
    bits = lax.bitcast_convert_type(x, I32)
    return bits ^ ((bits >> 31) | SIGN_BIT)


def _order_key_to_float(key):
    return lax.bitcast_convert_type(key ^ ((~key >> 31) | SIGN_BIT), F32)


def _bit_transpose32(words):
    a = list(words)
    j, m = 16, 0x0000FFFF
    while j:
        mask = jnp.int32(m - (1 << 32) if m >= 1 << 31 else m)
        k = 0
        while k < 32:
            t = (lax.shift_right_logical(a[k], jnp.int32(j)) ^ a[k + j]) & mask
            a[k] = a[k] ^ lax.shift_left(t, jnp.int32(j))
            a[k + j] = a[k + j] ^ t
            k = (k + j + 1) & ~j
        j >>= 1
        m = (m ^ (m << j)) & 0xFFFFFFFF
    return a


def _dsa_body(qd_ref, qi_ref, wq_ref, kd_ref, vt_ref, ki_ref, bias_ref, tri_ref, o_ref,
              sc_ref, plane_ref, qs_ref, m_ref, l_ref, acc_ref, lg_ref, bmax_ref, *, tq, top):
    i = pl.program_id(1)
    shape = (tq, tq)
    key_row = lax.broadcasted_iota(I32, shape, 0)
    qry_col = lax.broadcasted_iota(I32, shape, 1)
    visible = key_row // CHUNK <= qry_col // CHUNK

    w_t = (wq_ref[...] * (IDX_DIM ** -0.5 * IDX_HEADS ** -0.5)).T

    heads_per_vreg = LANES // IDX_DIM

    def score_tile(j):
        start = pl.multiple_of(j * tq, tq)
        ki = [ki_ref[pl.ds(start, tq), c * LANES:(c + 1) * LANES].astype(BF16)
              for c in range(heads_per_vreg)]
        s = jnp.zeros(shape, F32)
        for h in range(IDX_HEADS):
            g, c = divmod(h, heads_per_vreg)
            d = lax.dot_general(ki[c], qi_ref[:, g * LANES:(g + 1) * LANES], _NT,
                                preferred_element_type=F32)
            s = s + w_t[h:h + 1, :] * jnp.maximum(d, 0.0)
        sc_ref[j] = s

    def key_planes(j):
        ukey = _order_key(sc_ref[j])
        planes = _bit_transpose32([ukey[g * SUBLANES:(g + 1) * SUBLANES, :] for g in range(KEY_BITS)])
        for b in range(KEY_BITS):
            plane_ref[b, pl.ds(pl.multiple_of(j * SUBLANES, SUBLANES), SUBLANES), :] = planes[b]

    @pl.when(i == 0)
    def _():
        plane_ref[...] = jnp.zeros(plane_ref.shape, I32)

    def score_pair(p, carry):
        for j in (2 * p, 2 * p + 1):
            key_planes(j)
            score_tile(j + 1)
        return carry

    score_tile(0)
    lax.fori_loop(0, i // 2, score_pair, 0)

    @pl.when(i % 2 == 1)
    def _():
        key_planes(i - 1)
        score_tile(i)

    key_planes(i)
    sc_ref[i] = jnp.where(visible, sc_ref[i], -jnp.inf)

    def popcount_rows(words):
        return jnp.sum(lax.population_count(words), axis=0, keepdims=True)

    def radix_select(n_rows):
        block_of_row = lax.broadcasted_iota(I32, (n_rows, tq), 0) // SUBLANES
        qry_of_col = lax.broadcasted_iota(I32, (n_rows, tq), 1)
        n_bits = (qry_of_col // CHUNK + 1) * (CHUNK // SUBLANES)
        diag_bits = jnp.where(n_bits >= KEY_BITS, -1, lax.shift_left(jnp.int32(1), n_bits) - 1)
        cand0 = jnp.where(block_of_row < i, -1, jnp.where(block_of_row == i, diag_bits, 0))

        def bit_step(t, state):
            cand, n_above, thr_bits = state
            b = KEY_BITS - 1 - t
            ones = cand & plane_ref[b, :n_rows]
            n_ones = popcount_rows(ones)
            take = n_above + n_ones >= top
            cand = jnp.where(take, ones, cand ^ ones)
            n_above = jnp.where(take, n_above, n_above + n_ones)
            thr_bits = thr_bits | jnp.where(take, lax.shift_left(jnp.int32(1), b), 0)
            return cand, n_above, thr_bits

        zero = jnp.zeros((1, tq), I32)
        cand, n_above, thr_bits = lax.fori_loop(0, KEY_BITS, bit_step, (cand0, zero, zero))
        return n_above, thr_bits, popcount_rows(cand)

    def select_from(group):
        if group == RADIX_GROUPS - 1:
            return radix_select(plane_ref.shape[1])
        rows = (group + 1) * (plane_ref.shape[1] // RADIX_GROUPS)
        return lax.cond(i * SUBLANES < rows, functools.partial(radix_select, rows),
                        functools.partial(select_from, group + 1))

    n_above, thr_bits, n_equal = select_from(0)
    qry = lax.broadcasted_iota(I32, (1, tq), 1)
    n_visible = i * tq + (qry // CHUNK + 1) * CHUNK
    wanted = n_visible > top
    thr = jnp.where(wanted, _order_key_to_float(thr_bits), jnp.finfo(F32).min)
    tied = jnp.logical_and(wanted, n_above + n_equal > top)
    c_hi = n_above

    ones = jnp.ones((2 * SUBLANES, tq), BF16)

    def plain_mask(j, carry):
        sc_ref[j] = jnp.where(sc_ref[j] >= thr, 0.0, NEG_BIG)
        return carry

    def tie_mask(j, seen):
        s = sc_ref[j]
        equal = s == thr
        rank = jnp.dot(tri_ref[...], equal.astype(BF16), preferred_element_type=F32) + seen
        quota = jnp.where(tied, (top - c_hi).astype(F32), jnp.inf)
        keep_equal = jnp.where(rank < quota, 0.0, NEG_BIG)
        sc_ref[j] = jnp.where(s > thr, 0.0, jnp.where(equal, keep_equal, NEG_BIG))
        return seen + jnp.sum(equal.astype(F32), axis=0, keepdims=True)

    def with_ties():
        lax.fori_loop(0, i + 1, tie_mask, jnp.zeros((1, tq), F32))
        return jnp.int32(0)

    def without_ties():
        return lax.fori_loop(0, i + 1, plain_mask, jnp.int32(0))

    lax.cond(jnp.max(tied.astype(I32)) > 0, with_ties, without_ties)

    qs_ref[...] = (qd_ref[...].astype(F32) * (HEAD_DIM ** -0.5 * LOG2E)).astype(BF16)
    m_ref[...] = jnp.full(m_ref.shape, NEG_BIG, F32)
    l_ref[...] = jnp.zeros(l_ref.shape, F32)
    acc_ref[...] = jnp.zeros(acc_ref.shape, F32)

    heads = [slice(h * HEAD_DIM, (h + 1) * HEAD_DIM) for h in range(N_HEADS)]

    far, prev, diag = None, 0, 1

    def logits(u, near, slot):
        j = i - u
        start = pl.multiple_of(j * tq, tq)
        mask = sc_ref[j]
        block_max = []
        for h, hs in enumerate(heads):
            lg = lax.dot_general(kd_ref[pl.ds(start, tq), hs], qs_ref[:, hs], _NT,
                                 preferred_element_type=F32)
            if near is not None:
                lg = lg + bias_ref[h, near * tq:(near + 1) * tq, :]
            lg = lg + mask
            lg_ref[slot, h] = lg
            block_max.append(jnp.max(lg, axis=0, keepdims=True))
        bmax_ref[slot] = jnp.concatenate(block_max, axis=0)

    def values(u, slot):
        j = i - u
        m_old = m_ref[...]
        m_new = jnp.maximum(m_old, bmax_ref[slot])
        alpha = jnp.exp2(m_old - m_new)
        m_ref[...] = m_new
        denom = []
        for h, hs in enumerate(heads):
            p = jnp.exp2(lg_ref[slot, h] - m_new[h:h + 1, :]).astype(BF16)
            v_ext = jnp.concatenate([vt_ref[j, hs, :], ones], axis=0)
            pv = jnp.dot(v_ext, p, preferred_element_type=F32)
            acc_ref[h] = alpha[h:h + 1, :] * acc_ref[h] + pv[:HEAD_DIM]
            denom.append(pv[HEAD_DIM:HEAD_DIM + 1])
        l_ref[...] = alpha * l_ref[...] + jnp.concatenate(denom, axis=0)

    def even_step(u, near_a, near_b):
        logits(u - 1, near_a, 1)
        values(u, 0)
        logits(u - 2, near_b, 0)
        values(u - 1, 1)

    @pl.when(i == 0)
    def _():
        logits(0, diag, 0)
        values(0, 0)

    @pl.when(i == 1)
    def _():
        logits(1, prev, 1)

    @pl.when(jnp.logical_and(i >= 2, i % 2 == 1))
    def _():
        logits(i, far, 1)
        logits(i - 1, far, 0)
        values(i, 1)

    @pl.when(jnp.logical_and(i >= 2, i % 2 == 0))
    def _():
        logits(i, far, 0)

    def far_pair(k, carry):
        even_step(2 * (i // 2 - k), far, far)
        return carry

    lax.fori_loop(0, i // 2 - 1, far_pair, 0)

    @pl.when(i >= 2)
    def _():
        even_step(2, prev, diag)

    @pl.when(i == 1)
    def _():
        logits(0, diag, 0)
        values(1, 1)

    @pl.when(i >= 1)
    def _():
        values(0, 0)

    for h in range(N_HEADS):
        o = acc_ref[h] / l_ref[h:h + 1, :]
        o_ref[:, h * HEAD_DIM:(h + 1) * HEAD_DIM] = o.T.astype(o_ref.dtype)


def _dsa_attention(main, v_t, small, bias, batch, seq, *, qd_col, kd_col, qi_col, v_branch, tq):
    nq = seq // tq
    width = N_HEADS * HEAD_DIM
    assert IDX_HEADS * IDX_DIM == width and tq == KEY_BITS * SUBLANES and tq % CHUNK == 0
    assert nq % RADIX_GROUPS == 0
    key_copies = LANES // IDX_DIM
    top = min(TOPK_MAX, seq // 4)
    tri = (jnp.arange(tq)[None, :] < jnp.arange(tq)[:, None]).astype(BF16)
    resident = pl.Buffered(1)
    return pl.pallas_call(
        functools.partial(_dsa_body, tq=tq, top=top),
        grid=(batch, nq),
        in_specs=[
            pl.BlockSpec((tq, width), lambda b, i: (b * nq + i, qd_col)),
            pl.BlockSpec((tq, IDX_HEADS * IDX_DIM), lambda b, i: (b * nq + i, qi_col)),
            pl.BlockSpec((tq, LANES), lambda b, i: (b * nq + i, key_copies)),
            pl.BlockSpec((seq, width), lambda b, i: (b, kd_col), pipeline_mode=resident),
            pl.BlockSpec((None, None, nq, width, tq), lambda b, i: (v_branch, b, 0, 0, 0),
                         pipeline_mode=resident),
            pl.BlockSpec((seq, key_copies * LANES), lambda b, i: (b, 0), pipeline_mode=resident),
            pl.BlockSpec((N_HEADS, 2 * tq, tq), lambda b, i: (0, 0, 0), pipeline_mode=resident),
            pl.BlockSpec((tq, tq), lambda b, i: (0, 0), pipeline_mode=resident),
        ],
        out_specs=pl.BlockSpec((tq, width), lambda b, i: (b * nq + i, 0)),
        out_shape=jax.ShapeDtypeStruct((batch * seq, width), BF16),
        scratch_shapes=[
            pltpu.VMEM((nq, tq, tq), F32),
            pltpu.VMEM((KEY_BITS, nq * SUBLANES, tq), I32),
            pltpu.VMEM((tq, width), BF16),
            pltpu.VMEM((N_HEADS, tq), F32),
            pltpu.VMEM((N_HEADS, tq), F32),
            pltpu.VMEM((N_HEADS, HEAD_DIM, tq), F32),
            pltpu.VMEM((2, N_HEADS, tq, tq), F32),
            pltpu.VMEM((2, N_HEADS, tq), F32),
        ],
        compiler_params=_params("parallel", "arbitrary"),
        name="dsa_attention",
    )(main, main, small, main, v_t, small, bias, tri)


def _merge_cross_body(osb_ref, ods_ref, wsb_ref, wds_ref, gsb_ref, gds_ref, bsb_ref, bds_ref, wo_ref,
                      x_ref, gc_ref, wq_ref, km_ref, vm_ref, wco_ref, gn_ref, o_ref, hn_ref):
    p_sb = jnp.dot(osb_ref[...], wsb_ref[...], preferred_element_type=F32)
    p_ds = jnp.dot(ods_ref[...], wds_ref[...], preferred_element_type=F32)
    g_sb = jax.nn.sigmoid(gsb_ref[...].astype(F32) + bsb_ref[...])
    g_ds = jax.nn.sigmoid(gds_ref[...].astype(F32) + bds_ref[...])
    merged = (g_sb * p_sb + g_ds * p_ds).astype(BF16)
    x1 = x_ref[...] + jnp.dot(merged, wo_ref[...], preferred_element_type=F32)

    h = _rms(x1, gc_ref[...]).astype(BF16)
    q = jnp.dot(h, wq_ref[...], preferred_element_type=F32) * HEAD_DIM ** -0.5
    q = q.astype(BF16)
    outs = []
    for hh in range(MEM_HEADS):
        hs = slice(hh * HEAD_DIM, (hh + 1) * HEAD_DIM)
        lg = lax.dot_general(q[:, hs], km_ref[:, hs], _NT, preferred_element_type=F32)
        p = jnp.exp(lg - jnp.max(lg, axis=1, keepdims=True))
        o = jnp.dot(p.astype(BF16), vm_ref[:, hs], preferred_element_type=F32)
        outs.append((o / jnp.sum(p, axis=1, keepdims=True)).astype(BF16))
    o = jnp.concatenate(outs, axis=1)
    x2 = x1 + jnp.dot(o, wco_ref[...], preferred_element_type=F32)
    o_ref[...] = x2
    hn_ref[...] = _rms(x2, gn_ref[...]).astype(hn_ref.dtype)


def _merge_cross(o_sb, o_ds, w_sb, w_ds, w_out, proj, gate_offset, b_gate, x,
                 kv, g_cross, w_cq, w_co, g_next, batch, seq, *, tm):
    m, k = o_sb.shape
    d = w_sb.shape[1]
    n_mem = kv.shape[0] // batch
    width = MEM_HEADS * HEAD_DIM
    tm = min(tm, seq)
    assert gate_offset % d == 0 and seq % tm == 0
    g = gate_offset // d
    nt = seq // tm
    b_gate = b_gate.reshape(1, 2 * d).astype(F32)
    resident = pl.Buffered(1)

    def rows(width_, col=0):
        return pl.BlockSpec((tm, width_), lambda b, i: (b * nt + i, col))

    def whole(shape, *block):
        return pl.BlockSpec(shape, lambda b, i: block or (0,) * len(shape), pipeline_mode=resident)

    return pl.pallas_call(
        _merge_cross_body,
        grid=(batch, nt),
        in_specs=[
            rows(k), rows(k), whole((k, d)), whole((k, d)),
            rows(d, g), rows(d, g + 1), whole((1, d)), whole((1, d), 0, 1),
            whole((d, d)), rows(d),
            whole((1, d)), whole((d, width)),
            pl.BlockSpec((n_mem, width), lambda b, i: (b, 0)),
            pl.BlockSpec((n_mem, width), lambda b, i: (b, 1)),
            whole((width, d)), whole((1, d)),
        ],
        out_specs=[rows(d), rows(d)],
        out_shape=[jax.ShapeDtypeStruct((m, d), F32), jax.ShapeDtypeStruct((m, d), BF16)],
        compiler_params=_params("parallel", "parallel"),
        name="merge_out_cross",
    )(o_sb, o_ds, w_sb, w_ds, proj, proj, b_gate, b_gate, w_out, x,
      g_cross.reshape(1, d).astype(F32), w_cq, kv, kv, w_co, g_next.reshape(1, d).astype(F32))


def _delayed(u, tail, shift):
    rolled = pltpu.roll(u, shift, axis=0)
    row = lax.broadcasted_iota(I32, tail.shape, 0)
    head = jnp.where(row < shift, pltpu.roll(tail, shift, axis=0), rolled[:SUBLANES])
    return jnp.concatenate([head, rolled[SUBLANES:]], axis=0)


def _ffn_up_body(h_ref, wa_ref, wv_ref, cwa_ref, cwv_ref, cba_ref, cbv_ref, wd_ref, o_ref, wdb_ref,
                 wab_ref, wvb_ref, halo_ref, *, tiles_per_seq):
    i = pl.program_id(1)
    _round_chunks([wd_ref], [wdb_ref])

    @pl.when(i == 0)
    def _():
        wab_ref[...] = wa_ref[...].astype(BF16)
        wvb_ref[...] = wv_ref[...].astype(BF16)

    h = h_ref[...]
    tm = h.shape[0]
    sequence_start = i % tiles_per_seq == 0

    def conv(wb_ref, cw_ref, cb_ref, slot):
        u = jnp.dot(h, wb_ref[...], preferred_element_type=F32)
        tail = jnp.where(sequence_start, 0.0, halo_ref[slot])
        halo_ref[slot] = u[tm - SUBLANES:, :]
        c = cb_ref[...] + cw_ref[CONV_WIDTH - 1:CONV_WIDTH, :] * u
        for tap in range(CONV_WIDTH - 1):
            c = c + cw_ref[tap:tap + 1, :] * _delayed(u, tail, CONV_WIDTH - 1 - tap)
        return c

    a = conv(wab_ref, cwa_ref, cba_ref, 0)
    val = conv(wvb_ref, cwv_ref, cbv_ref, 1)
    o_ref[...] = (jax.nn.gelu(a) * val).astype(o_ref.dtype)


def _ffn_up_gate(h, w_up, conv_w, conv_b, w_down, seq, *, tm=ROW_TILE, tn=FFN_COL_TILE):
    m, d = h.shape
    two_ff = w_up.shape[1]
    d_ff = two_ff // 2
    tm, tn = min(tm, seq), min(tn, d_ff)
    assert seq % tm == 0 and d_ff % tn == 0 and tm >= SUBLANES >= CONV_WIDTH - 1
    nf, nt = d_ff // tn, m // tm
    conv_w = conv_w.astype(F32)
    conv_b = conv_b.reshape(1, two_ff).astype(F32)
    rider_in, rider_out, rider_shape = _rounding_riders([w_down], nf * nt, lambda j, i: j * nt + i)
    return pl.pallas_call(
        functools.partial(_ffn_up_body, tiles_per_seq=seq // tm),
        grid=(nf, nt),
        in_specs=[
            pl.BlockSpec((tm, d), lambda j, i: (i, 0)),
            pl.BlockSpec((d, tn), lambda j, i: (0, j)),
            pl.BlockSpec((d, tn), lambda j, i: (0, nf + j)),
            pl.BlockSpec((CONV_WIDTH, tn), lambda j, i: (0, j)),
            pl.BlockSpec((CONV_WIDTH, tn), lambda j, i: (0, nf + j)),
            pl.BlockSpec((1, tn), lambda j, i: (0, j)),
            pl.BlockSpec((1, tn), lambda j, i: (0, nf + j)),
        ] + rider_in,
        out_specs=[pl.BlockSpec((tm, tn), lambda j, i: (i, j))] + rider_out,
        out_shape=[jax.ShapeDtypeStruct((m, d_ff), BF16)] + rider_shape,
        scratch_shapes=[pltpu.VMEM((d, tn), BF16), pltpu.VMEM((d, tn), BF16),
                        pltpu.VMEM((2, SUBLANES, tn), F32)],
        compiler_params=_params("parallel", "arbitrary"),
        name="ffn_up_conv_gate",
    )(h, w_up, w_up, conv_w, conv_w, conv_b, conv_b, w_down)


def _ffn_down_body(a_ref, w_ref, x_ref, g_ref, o_ref, *, final_norm):
    k = pl.program_id(1)

    @pl.when(k == 0)
    def _():
        o_ref[...] = x_ref[...]

    o_ref[...] += jnp.dot(a_ref[...], w_ref[...], preferred_element_type=F32)

    if final_norm:
        @pl.when(k == pl.num_programs(1) - 1)
        def _():
            o_ref[...] = _rms(o_ref[...], g_ref[...])


def _ffn_down(a, w, x, g_final, *, tm=ROW_TILE, tk=FFN_K_TILE):
    m, kdim = a.shape
    d = w.shape[1]
    tm, tk = min(tm, m), min(tk, kdim)
    assert m % tm == 0 and kdim % tk == 0
    final_norm = g_final is not None
    g = (g_final if final_norm else jnp.ones((d,), F32)).reshape(1, d).astype(F32)
    return pl.pallas_call(
        functools.partial(_ffn_down_body, final_norm=final_norm),
        grid=(m // tm, kdim // tk),
        in_specs=[
            pl.BlockSpec((tm, tk), lambda i, k: (i, k)),
            pl.BlockSpec((tk, d), lambda i, k: (k, 0)),
            pl.BlockSpec((tm, d), lambda i, k: (i, 0)),
            pl.BlockSpec((1, d), lambda i, k: (0, 0)),
        ],
        out_specs=pl.BlockSpec((tm, d), lambda i, k: (i, 0)),
        out_shape=jax.ShapeDtypeStruct((m, d), F32),
        compiler_params=_params("parallel", "arbitrary"),
        name="ffn_down",
    )(a, w, x, g)


def _layer(x, mem, g_mix, w_in, b_gate, w_proj_sb, w_proj_dsa, w_out, rel_bias,
           g_cross, g_mem, w_cq, w_ckv, w_co, g_ffn, w_up, conv_w, conv_b, w_down, g_final, batch, seq):
    d = x.shape[1]
    width = N_HEADS * HEAD_DIM
    idx_w = IDX_HEADS * IDX_DIM
    o_qi = 6 * width
    o_ki = o_qi + idx_w
    o_wi = o_ki + IDX_DIM
    o_g = o_wi + IDX_HEADS

    zeros = jnp.zeros((d, LANES - IDX_DIM), F32)
    w_small = jnp.concatenate([
        w_in[:, o_ki:o_wi], zeros, zeros, w_in[:, o_ki:o_wi],
        jnp.pad(w_in[:, o_wi:o_g], ((0, 0), (0, LANES - IDX_HEADS)))], axis=1).astype(BF16)
    h, small = _rmsnorm_proj(x, g_mix, w_small, name="mixer_norm_index_proj", proj_dtype=F32,
                             keep_rows=True)
    w_in_t = w_in.T
    tq = ATT_BLOCK
    nq = seq // tq
    q_sb, k_sb, v_sb, q_ds, k_ds, v_ds, q_ix = (g * width for g in range(o_ki // width))
    gate_rows = tuple(o_g + g * width for g in range(2 * d // width))
    n_gate = len(gate_rows)
    main = _matmul_ws(h, w_in_t, name="in_proj_main",
                      first_rows=gate_rows + (q_sb, k_sb, q_ds, k_ds, q_ix),
                      out_dtype=BF16, tm=PROJ_ROWS, tn=width)
    v_t = _matmul_ws_t(h, w_in_t, name="in_proj_values", first_rows=(v_sb, v_ds), tn=width, tq=tq,
                       tm=PROJ_ROWS)
    v_t = v_t.reshape(2, batch, nq, width, tq)

    o_sb, w_proj_sb, w_proj_dsa, w_out, w_cq, w_co = _sb_attention(
        main, v_t, batch, seq, q_col=n_gate, k_col=n_gate + 1, v_branch=0, tq=tq,
        round_weights=(w_proj_sb, w_proj_dsa, w_out, w_cq, w_co))
    bias = _near_bias(rel_bias, tq)
    o_ds = _dsa_attention(main, v_t, small, bias, batch, seq, qd_col=n_gate + 2, kd_col=n_gate + 3,
                          qi_col=n_gate + 4, v_branch=1, tq=tq)

    kv = _rmsnorm_proj(mem, g_mem, w_ckv, name="mem_norm_kv_proj", proj_dtype=BF16, keep_rows=False)
    x, h_ffn = _merge_cross(o_sb, o_ds, w_proj_sb, w_proj_dsa, w_out, main, 0, b_gate, x, kv, g_cross,
                            w_cq, w_co, g_ffn, batch, seq, tm=MERGE_ROWS)

    act, w_down = _ffn_up_gate(h_ffn, w_up, conv_w, conv_b, w_down, seq)
    return _ffn_down(act, w_down, x, g_final)


def kernel(x, mem, g_mix, w_in, b_gate, w_proj_sb, w_proj_dsa, w_out, rel_bias, g_cross, g_mem,
           w_cq, w_ckv, w_co, g_ffn, w_up, conv_w, conv_b, w_down, g_final):
    batch, seq, d = x.shape
    h = x.reshape(batch * seq, d)
    mem2 = mem.reshape(batch * mem.shape[1], d)
    depth = g_mix.shape[0]
    for l in range(depth):
        h = _layer(h, mem2, g_mix[l], w_in[l], b_gate[l], w_proj_sb[l], w_proj_dsa[l], w_out[l],
                   rel_bias, g_cross[l], g_mem[l], w_cq[l], w_ckv[l], w_co[l], g_ffn[l], w_up[l],
                   conv_w[l], conv_b[l], w_down[l], g_final if l == depth - 1 else None, batch, seq)
    return h.reshape(batch, seq, d)
```

```python
import functools

import jax
import jax.numpy as jnp
from jax import lax
from jax.experimental import pallas as pl
from jax.experimental.pallas import tpu as pltpu

F32, BF16, I32 = jnp.float32, jnp.bfloat16, jnp.int32

EPS = 1e-6
HEAD_DIM = 128
N_HEADS = 8
IDX_HEADS = 16
IDX_DIM = 64
CHUNK = 64
TOPK_MAX = 256
N_BUCKETS = 32
MAX_DISTANCE = 128
MEM_HEADS = 4
CONV_WIDTH = 3

LANES = 128
SUBLANES = 8
VMEM_LIMIT_BYTES = 62 * 1024 * 1024
NEG_BIG = -1e30
EXP2_UNDERFLOW = -151.0
LOG2E = 1.4426950408889634
KEY_BITS = 32
SIGN_BIT = -(2 ** 31)

PROJ_ROWS = 2048
ROW_TILE = 1024
NORM_ROWS = 1024
MERGE_ROWS = 512
FFN_COL_TILE = 512
FFN_K_TILE = 2048
ATT_BLOCK = KEY_BITS * SUBLANES
RADIX_GROUPS = 8

_NT = (((1,), (1,)), ((), ()))


def _params(*sem):
    return pltpu.CompilerParams(dimension_semantics=sem, vmem_limit_bytes=VMEM_LIMIT_BYTES)


def _rms(x, g):
    inv = lax.rsqrt(jnp.mean(x * x, axis=-1, keepdims=True) + EPS)
    return x * inv * g


def _rounding_riders(weights, n_steps, step_index):
    in_specs, out_specs, out_shape = [], [], []
    for w in weights:
        rows, cols = w.shape
        assert rows % n_steps == 0 and (rows // n_steps) % (2 * SUBLANES) == 0
        for specs in (in_specs, out_specs):
            specs.append(pl.BlockSpec((rows // n_steps, cols), lambda *ids: (step_index(*ids), 0)))
        out_shape.append(jax.ShapeDtypeStruct(w.shape, BF16))
    return in_specs, out_specs, out_shape


def _round_chunks(src_refs, dst_refs):
    for src, dst in zip(src_refs, dst_refs):
        dst[...] = src[...].astype(BF16)


def _mm_ws_body(a_ref, wt_ref, o_ref, wb_ref):
    @pl.when(pl.program_id(1) == 0)
    def _():
        wb_ref[...] = wt_ref[...].astype(BF16)

    o_ref[...] = lax.dot_general(a_ref[...], wb_ref[...], _NT,
                                 preferred_element_type=F32).astype(o_ref.dtype)


def _row_window(first_rows, tn, k):
    assert all(r % SUBLANES == 0 for r in first_rows)

    def index_map(j, i):
        row = jnp.int32(first_rows[0])
        for step, first in enumerate(first_rows[1:], start=1):
            row = jnp.where(j >= step, first, row)
        return pl.multiple_of(row, SUBLANES), 0

    return pl.BlockSpec((pl.Element(tn), pl.Element(k)), index_map)


def _matmul_ws(a, w_t, *, name, first_rows, out_dtype, tm, tn):
    m, k = a.shape
    tm = min(tm, m)
    assert m % tm == 0 and max(first_rows) + tn <= w_t.shape[0] and w_t.shape[1] == k
    return pl.pallas_call(
        _mm_ws_body,
        grid=(len(first_rows), m // tm),
        in_specs=[pl.BlockSpec((tm, k), lambda j, i: (i, 0)), _row_window(first_rows, tn, k)],
        out_specs=pl.BlockSpec((tm, tn), lambda j, i: (i, j)),
        out_shape=jax.ShapeDtypeStruct((m, len(first_rows) * tn), out_dtype),
        scratch_shapes=[pltpu.VMEM((tn, k), BF16)],
        compiler_params=_params("parallel", "arbitrary"),
        name=name,
    )(a, w_t)


def _mm_ws_t_body(wt_ref, a_ref, o_ref, wb_ref):
    @pl.when(pl.program_id(1) == 0)
    def _():
        wb_ref[...] = wt_ref[...].astype(BF16)

    res = lax.dot_general(wb_ref[...], a_ref[...], _NT, preferred_element_type=F32)
    n_blocks, _, tq = o_ref.shape
    for blk in range(n_blocks):
        o_ref[blk] = res[:, blk * tq:(blk + 1) * tq].astype(o_ref.dtype)


def _matmul_ws_t(a, w_t, *, name, first_rows, tn, tq, tm):
    m, k = a.shape
    tm = min(tm, m)
    assert m % tm == 0 and tm % tq == 0 and max(first_rows) + tn <= w_t.shape[0] and w_t.shape[1] == k
    per_step = tm // tq
    return pl.pallas_call(
        _mm_ws_t_body,
        grid=(len(first_rows), m // tm),
        in_specs=[_row_window(first_rows, tn, k), pl.BlockSpec((tm, k), lambda j, i: (i, 0))],
        out_specs=pl.BlockSpec((None, per_step, tn, tq), lambda j, i: (j, i, 0, 0)),
        out_shape=jax.ShapeDtypeStruct((len(first_rows), m // tq, tn, tq), BF16),
        scratch_shapes=[pltpu.VMEM((tn, k), BF16)],
        compiler_params=_params("parallel", "arbitrary"),
        name=name,
    )(w_t, a)


def _norm_proj_body(x_ref, g_ref, w_ref, *out_refs):
    h = _rms(x_ref[...], g_ref[...]).astype(BF16)
    p_ref = out_refs[-1]
    p_ref[...] = jnp.dot(h, w_ref[...].astype(BF16), preferred_element_type=F32).astype(p_ref.dtype)
    if len(out_refs) == 2:
        out_refs[0][...] = h


def _rmsnorm_proj(x, g, w, *, name, proj_dtype, keep_rows, tm=NORM_ROWS):
    m, d = x.shape
    n = w.shape[1]
    tm = min(tm, m)
    assert m % tm == 0
    out_specs = [pl.BlockSpec((tm, n), lambda i: (i, 0))]
    out_shape = [jax.ShapeDtypeStruct((m, n), proj_dtype)]
    if keep_rows:
        out_specs.insert(0, pl.BlockSpec((tm, d), lambda i: (i, 0)))
        out_shape.insert(0, jax.ShapeDtypeStruct((m, d), BF16))
    out = pl.pallas_call(
        _norm_proj_body,
        grid=(m // tm,),
        in_specs=[pl.BlockSpec((tm, d), lambda i: (i, 0)), pl.BlockSpec((1, d), lambda i: (0, 0)),
                  pl.BlockSpec((d, n), lambda i: (0, 0))],
        out_specs=out_specs,
        out_shape=out_shape,
        compiler_params=_params("parallel"),
        name=name,
    )(x, g.reshape(1, d).astype(F32), w)
    return out if keep_rows else out[0]


def _sb_body(q_ref, k_ref, vt_ref, tri_ref, *refs, tq, n_riders):
    o_ref, qs_ref, z_ref, lb_ref, wb_ref, acc_ref = refs[n_riders], *refs[2 * n_riders + 1:]
    _round_chunks(refs[:n_riders], refs[n_riders + 1:2 * n_riders + 1])
    i = pl.program_id(1)
    shape = (tq, tq)
    before = lax.broadcasted_iota(I32, shape, 0) < lax.broadcasted_iota(I32, shape, 1)
    heads = [slice(h * HEAD_DIM, (h + 1) * HEAD_DIM) for h in range(N_HEADS)]
    qs_ref[...] = (q_ref[...].astype(F32) * (HEAD_DIM ** -0.5 * LOG2E)).astype(BF16)
    acc_ref[...] = jnp.zeros(acc_ref.shape, F32)

    def tile(j, carry, diagonal):
        start = pl.multiple_of(j * tq, tq)
        for h, hs in enumerate(heads):
            z = lax.dot_general(k_ref[pl.ds(start, tq), hs], qs_ref[:, hs], _NT,
                                preferred_element_type=F32)
            neg_abs = lax.bitcast_convert_type(lax.bitcast_convert_type(z, I32) | SIGN_BIT, F32)
            log_lose = jnp.maximum(z, 0.0) + jnp.log2(1.0 + jnp.exp2(neg_abs))
            if diagonal:
                log_lose = jnp.where(before, log_lose, 0.0)
            z_ref[h] = z
            lb_ref[h] = log_lose.astype(BF16)
        new_carry = []
        for h in range(N_HEADS):
            c = carry[h:h + 1, :] - jnp.dot(tri_ref[...], lb_ref[h], preferred_element_type=F32)
            w = jnp.exp2(z_ref[h] + c)
            if diagonal:
                w = jnp.where(before, w, 0.0)
            wb_ref[h] = w.astype(BF16)
            new_carry.append(c[0:1, :])
        for h, hs in enumerate(heads):
            acc_ref[h] += jnp.dot(vt_ref[j, hs, :], wb_ref[h], preferred_element_type=F32)
        return jnp.concatenate(new_carry, axis=0)

    carry = tile(i, jnp.zeros((N_HEADS, tq), F32), True)

    def cond(state):
        j, live, _ = state
        return jnp.logical_and(j >= 0, live > EXP2_UNDERFLOW)

    def body(state):
        j, _, carry = state
        carry = tile(j, carry, False)
        return j - 1, jnp.max(carry), carry

    lax.while_loop(cond, body, (i - 1, jnp.max(carry), carry))
    for h, hs in enumerate(heads):
        o_ref[:, hs] = acc_ref[h].T.astype(o_ref.dtype)


def _sb_attention(qk, v_t, batch, seq, *, q_col, k_col, v_branch, tq, round_weights=()):
    nq = seq // tq
    width = N_HEADS * HEAD_DIM
    tri = (jnp.arange(tq)[None, :] >= jnp.arange(tq)[:, None]).astype(BF16)
    resident = pl.Buffered(1)
    rider_in, rider_out, rider_shape = _rounding_riders(round_weights, batch * nq,
                                                        lambda b, i: b * nq + i)
    return pl.pallas_call(
        functools.partial(_sb_body, tq=tq, n_riders=len(round_weights)),
        grid=(batch, nq),
        in_specs=[
            pl.BlockSpec((tq, width), lambda b, i: (b * nq + i, q_col)),
            pl.BlockSpec((seq, width), lambda b, i: (b, k_col), pipeline_mode=resident),
            pl.BlockSpec((None, None, nq, width, tq), lambda b, i: (v_branch, b, 0, 0, 0),
                         pipeline_mode=resident),
            pl.BlockSpec((tq, tq), lambda b, i: (0, 0), pipeline_mode=resident),
        ] + rider_in,
        out_specs=[pl.BlockSpec((tq, width), lambda b, i: (b * nq + i, 0))] + rider_out,
        out_shape=[jax.ShapeDtypeStruct((batch * seq, width), BF16)] + rider_shape,
        scratch_shapes=[
            pltpu.VMEM((tq, width), BF16),
            pltpu.VMEM((N_HEADS, tq, tq), F32),
            pltpu.VMEM((N_HEADS, tq, tq), BF16),
            pltpu.VMEM((N_HEADS, tq, tq), BF16),
            pltpu.VMEM((N_HEADS, HEAD_DIM, tq), F32),
        ],
        compiler_params=_params("parallel", "arbitrary"),
        name="sb_attention",
    )(qk, qk, v_t, tri, *round_weights)


def _bucket_thresholds():
    nb = N_BUCKETS // 2
    max_exact = nb // 2
    span = nb - max_exact
    out = []
    for k in range(1, span):
        n = max_exact
        while n ** span * max_exact ** k < MAX_DISTANCE ** k * max_exact ** span:
            n += 1
        out.append(n)
    return max_exact, out


def _bias_body(rb_ref, o_ref, *, tq):
    nb = N_BUCKETS // 2
    max_exact, steps = _bucket_thresholds()
    shape = (2 * tq, tq)
    rel = lax.broadcasted_iota(I32, shape, 0) - lax.broadcasted_iota(I32, shape, 1) - tq
    n = jnp.abs(rel)
    large = jnp.full(shape, max_exact, I32)
    for t in steps:
        large = large + (n >= t).astype(I32)
    bucket = jnp.where(rel > 0, nb, 0) + jnp.where(n < max_exact, n, large)
    for h in range(N_HEADS):
        val = jnp.zeros(shape, F32)
        for b in range(N_BUCKETS):
            val = jnp.where(bucket == b, rb_ref[b, h], val)
        o_ref[h] = (val - rb_ref[nb - 1, h]) * LOG2E


def _near_bias(rel_bias, tq):
    return pl.pallas_call(
        functools.partial(_bias_body, tq=tq),
        in_specs=[pl.BlockSpec(memory_space=pltpu.SMEM)],
        out_specs=pl.BlockSpec(memory_space=pltpu.VMEM),
        out_shape=jax.ShapeDtypeStruct((N_HEADS, 2 * tq, tq), F32),
        compiler_params=pltpu.CompilerParams(vmem_limit_bytes=VMEM_LIMIT_BYTES),
        name="dsa_near_bias",
    )(rel_bias.astype(F32))


def _order_key(x):
    bits = lax.bitcast_convert_type(x, I32)
    return bits ^ ((bits >> 31) | SIGN_BIT)


def _order_key_to_float(key):
    return lax.bitcast_convert_type(key ^ ((~key >> 31) | SIGN_BIT), F32)


def _bit_transpose32(words):
    a = list(words)
    j, m = 16, 0x0000FFFF
    while j:
        mask = jnp.int32(m - (1 << 32) if m >= 1 << 31 else m)
        k = 0
        while k < 32:
            t = (lax.shift_right_logical(a[k], jnp.int32(j)) ^ a[k + j]) & mask
            a[k] = a[k] ^ lax.shift_left(t, jnp.int32(j))
            a[k + j] = a[k + j] ^ t
            k = (k + j + 1) & ~j
        j >>= 1
        m = (m ^ (m << j)) & 0xFFFFFFFF
    return a


def _dsa_body(qd_ref, qi_ref, wq_ref, kd_ref, vt_ref, ki_ref, bias_ref, tri_ref, o_ref,
              sc_ref, plane_ref, qs_ref, m_ref, l_ref, acc_ref, lg_ref, bmax_ref, *, tq, top):
    i = pl.program_id(1)
    shape = (tq, tq)
    key_row = lax.broadcasted_iota(I32, shape, 0)
    qry_col = lax.broadcasted_iota(I32, shape, 1)
    visible = key_row // CHUNK <= qry_col // CHUNK

    w_t = (wq_ref[...] * (IDX_DIM ** -0.5 * IDX_HEADS ** -0.5)).T

    heads_per_vreg = LANES // IDX_DIM

    def score_tile(j):
        start = pl.multiple_of(j * tq, tq)
        ki = [ki_ref[pl.ds(start, tq), c * LANES:(c + 1) * LANES].astype(BF16)
              for c in range(heads_per_vreg)]
        s = jnp.zeros(shape, F32)
        for h in range(IDX_HEADS):
            g, c = divmod(h, heads_per_vreg)
            d = lax.dot_general(ki[c], qi_ref[:, g * LANES:(g + 1) * LANES], _NT,
                                preferred_element_type=F32)
            s = s + w_t[h:h + 1, :] * jnp.maximum(d, 0.0)
        sc_ref[j] = s

    def key_planes(j):
        ukey = _order_key(sc_ref[j])
        planes = _bit_transpose32([ukey[g * SUBLANES:(g + 1) * SUBLANES, :] for g in range(KEY_BITS)])
        for b in range(KEY_BITS):
            plane_ref[b, pl.ds(pl.multiple_of(j * SUBLANES, SUBLANES), SUBLANES), :] = planes[b]

    @pl.when(i == 0)
    def _():
        plane_ref[...] = jnp.zeros(plane_ref.shape, I32)

    def score_pair(p, carry):
        for j in (2 * p, 2 * p + 1):
            key_planes(j)
            score_tile(j + 1)
        return carry

    score_tile(0)
    lax.fori_loop(0, i // 2, score_pair, 0)

    @pl.when(i % 2 == 1)
    def _():
        key_planes(i - 1)
        score_tile(i)

    key_planes(i)
    sc_ref[i] = jnp.where(visible, sc_ref[i], -jnp.inf)

    def popcount_rows(words):
        return jnp.sum(lax.population_count(words), axis=0, keepdims=True)

    def radix_select(n_rows):
        block_of_row = lax.broadcasted_iota(I32, (n_rows, tq), 0) // SUBLANES
        qry_of_col = lax.broadcasted_iota(I32, (n_rows, tq), 1)
        n_bits = (qry_of_col // CHUNK + 1) * (CHUNK // SUBLANES)
        diag_bits = jnp.where(n_bits >= KEY_BITS, -1, lax.shift_left(jnp.int32(1), n_bits) - 1)
        cand0 = jnp.where(block_of_row < i, -1, jnp.where(block_of_row == i, diag_bits, 0))

        def bit_step(t, state):
            cand, n_above, thr_bits = state
            b = KEY_BITS - 1 - t
            ones = cand & plane_ref[b, :n_rows]
            n_ones = popcount_rows(ones)
            take = n_above + n_ones >= top
            cand = jnp.where(take, ones, cand ^ ones)
            n_above = jnp.where(take, n_above, n_above + n_ones)
            thr_bits = thr_bits | jnp.where(take, lax.shift_left(jnp.int32(1), b), 0)
            return cand, n_above, thr_bits

        zero = jnp.zeros((1, tq), I32)
        cand, n_above, thr_bits = lax.fori_loop(0, KEY_BITS, bit_step, (cand0, zero, zero))
        return n_above, thr_bits, popcount_rows(cand)

    n_groups = min(RADIX_GROUPS, plane_ref.shape[1] // SUBLANES)

    def select_from(group):
        if group == n_groups - 1:
            return radix_select(plane_ref.shape[1])
        rows = (group + 1) * (plane_ref.shape[1] // n_groups)
        return lax.cond(i * SUBLANES < rows, functools.partial(radix_select, rows),
                        functools.partial(select_from, group + 1))

    n_above, thr_bits, n_equal = select_from(0)
    qry = lax.broadcasted_iota(I32, (1, tq), 1)
    n_visible = i * tq + (qry // CHUNK + 1) * CHUNK
    wanted = n_visible > top
    thr = jnp.where(wanted, _order_key_to_float(thr_bits), jnp.finfo(F32).min)
    tied = jnp.logical_and(wanted, n_above + n_equal > top)
    c_hi = n_above

    ones = jnp.ones((2 * SUBLANES, tq), BF16)

    def plain_mask(j, carry):
        sc_ref[j] = jnp.where(sc_ref[j] >= thr, 0.0, NEG_BIG)
        return carry

    def tie_mask(j, seen):
        s = sc_ref[j]
        equal = s == thr
        rank = jnp.dot(tri_ref[...], equal.astype(BF16), preferred_element_type=F32) + seen
        quota = jnp.where(tied, (top - c_hi).astype(F32), jnp.inf)
        keep_equal = jnp.where(rank < quota, 0.0, NEG_BIG)
        sc_ref[j] = jnp.where(s > thr, 0.0, jnp.where(equal, keep_equal, NEG_BIG))
        return seen + jnp.sum(equal.astype(F32), axis=0, keepdims=True)

    def with_ties():
        lax.fori_loop(0, i + 1, tie_mask, jnp.zeros((1, tq), F32))
        return jnp.int32(0)

    def without_ties():
        return lax.fori_loop(0, i + 1, plain_mask, jnp.int32(0))

    lax.cond(jnp.max(tied.astype(I32)) > 0, with_ties, without_ties)

    qs_ref[...] = (qd_ref[...].astype(F32) * (HEAD_DIM ** -0.5 * LOG2E)).astype(BF16)
    m_ref[...] = jnp.full(m_ref.shape, NEG_BIG, F32)
    l_ref[...] = jnp.zeros(l_ref.shape, F32)
    acc_ref[...] = jnp.zeros(acc_ref.shape, F32)

    heads = [slice(h * HEAD_DIM, (h + 1) * HEAD_DIM) for h in range(N_HEADS)]

    far, prev, diag = None, 0, 1

    def logits(u, near, slot):
        j = i - u
        start = pl.multiple_of(j * tq, tq)
        mask = sc_ref[j]
        block_max = []
        for h, hs in enumerate(heads):
            lg = lax.dot_general(kd_ref[pl.ds(start, tq), hs], qs_ref[:, hs], _NT,
                                 preferred_element_type=F32)
            if near is not None:
                lg = lg + bias_ref[h, near * tq:(near + 1) * tq, :]
            lg = lg + mask
            lg_ref[slot, h] = lg
            block_max.append(jnp.max(lg, axis=0, keepdims=True))
        bmax_ref[slot] = jnp.concatenate(block_max, axis=0)

    def values(u, slot):
        j = i - u
        m_old = m_ref[...]
        m_new = jnp.maximum(m_old, bmax_ref[slot])
        alpha = jnp.exp2(m_old - m_new)
        m_ref[...] = m_new
        denom = []
        for h, hs in enumerate(heads):
            p = jnp.exp2(lg_ref[slot, h] - m_new[h:h + 1, :]).astype(BF16)
            v_ext = jnp.concatenate([vt_ref[j, hs, :], ones], axis=0)
            pv = jnp.dot(v_ext, p, preferred_element_type=F32)
            acc_ref[h] = alpha[h:h + 1, :] * acc_ref[h] + pv[:HEAD_DIM]
            denom.append(pv[HEAD_DIM:HEAD_DIM + 1])
        l_ref[...] = alpha * l_ref[...] + jnp.concatenate(denom, axis=0)

    def even_step(u, near_a, near_b):
        logits(u - 1, near_a, 1)
        values(u, 0)
        logits(u - 2, near_b, 0)
        values(u - 1, 1)

    @pl.when(i == 0)
    def _():
        logits(0, diag, 0)
        values(0, 0)

    @pl.when(i == 1)
    def _():
        logits(1, prev, 1)

    @pl.when(jnp.logical_and(i >= 2, i % 2 == 1))
    def _():
        logits(i, far, 1)
        logits(i - 1, far, 0)
        values(i, 1)

    @pl.when(jnp.logical_and(i >= 2, i % 2 == 0))
    def _():
        logits(i, far, 0)

    n_far_pairs = i // 2 - 1

    def far_quad(k, carry):
        u = 2 * (i // 2 - 2 * k)
        even_step(u, far, far)
        even_step(u - 2, far, far)
        return carry

    lax.fori_loop(0, n_far_pairs // 2, far_quad, 0)

    @pl.when(jnp.logical_and(n_far_pairs > 0, n_far_pairs % 2 == 1))
    def _():
        even_step(4, far, far)

    @pl.when(i >= 2)
    def _():
        even_step(2, prev, diag)

    @pl.when(i == 1)
    def _():
        logits(0, diag, 0)
        values(1, 1)

    @pl.when(i >= 1)
    def _():
        values(0, 0)

    for h in range(N_HEADS):
        o = acc_ref[h] / l_ref[h:h + 1, :]
        o_ref[:, h * HEAD_DIM:(h + 1) * HEAD_DIM] = o.T.astype(o_ref.dtype)


def _dsa_attention(main, v_t, small, bias, batch, seq, *, qd_col, kd_col, qi_col, v_branch, tq):
    nq = seq // tq
    width = N_HEADS * HEAD_DIM
    assert IDX_HEADS * IDX_DIM == width and tq == KEY_BITS * SUBLANES and tq % CHUNK == 0
    assert nq % min(RADIX_GROUPS, nq) == 0
    key_copies = LANES // IDX_DIM
    top = min(TOPK_MAX, seq // 4)
    tri = (jnp.arange(tq)[None, :] < jnp.arange(tq)[:, None]).astype(BF16)
    resident = pl.Buffered(1)
    return pl.pallas_call(
        functools.partial(_dsa_body, tq=tq, top=top),
        grid=(batch, nq),
        in_specs=[
            pl.BlockSpec((tq, width), lambda b, i: (b * nq + i, qd_col)),
            pl.BlockSpec((tq, IDX_HEADS * IDX_DIM), lambda b, i: (b * nq + i, qi_col)),
            pl.BlockSpec((tq, LANES), lambda b, i: (b * nq + i, key_copies)),
            pl.BlockSpec((seq, width), lambda b, i: (b, kd_col), pipeline_mode=resident),
            pl.BlockSpec((None, None, nq, width, tq), lambda b, i: (v_branch, b, 0, 0, 0),
                         pipeline_mode=resident),
            pl.BlockSpec((seq, key_copies * LANES), lambda b, i: (b, 0), pipeline_mode=resident),
            pl.BlockSpec((N_HEADS, 2 * tq, tq), lambda b, i: (0, 0, 0), pipeline_mode=resident),
            pl.BlockSpec((tq, tq), lambda b, i: (0, 0), pipeline_mode=resident),
        ],
        out_specs=pl.BlockSpec((tq, width), lambda b, i: (b * nq + i, 0)),
        out_shape=jax.ShapeDtypeStruct((batch * seq, width), BF16),
        scratch_shapes=[
            pltpu.VMEM((nq, tq, tq), F32),
            pltpu.VMEM((KEY_BITS, nq * SUBLANES, tq), I32),
            pltpu.VMEM((tq, width), BF16),
            pltpu.VMEM((N_HEADS, tq), F32),
            pltpu.VMEM((N_HEADS, tq), F32),
            pltpu.VMEM((N_HEADS, HEAD_DIM, tq), F32),
            pltpu.VMEM((2, N_HEADS, tq, tq), F32),
            pltpu.VMEM((2, N_HEADS, tq), F32),
        ],
        compiler_params=_params("parallel", "arbitrary"),
        name="dsa_attention",
    )(main, main, small, main, v_t, small, bias, tri)


def _merge_cross_body(osb_ref, ods_ref, wsb_ref, wds_ref, gsb_ref, gds_ref, bsb_ref, bds_ref, wo_ref,
                      x_ref, gc_ref, wq_ref, km_ref, vm_ref, wco_ref, gn_ref, o_ref, hn_ref):
    p_sb = jnp.dot(osb_ref[...], wsb_ref[...], preferred_element_type=F32)
    p_ds = jnp.dot(ods_ref[...], wds_ref[...], preferred_element_type=F32)
    g_sb = jax.nn.sigmoid(gsb_ref[...].astype(F32) + bsb_ref[...])
    g_ds = jax.nn.sigmoid(gds_ref[...].astype(F32) + bds_ref[...])
    merged = (g_sb * p_sb + g_ds * p_ds).astype(BF16)
    x1 = x_ref[...] + jnp.dot(merged, wo_ref[...], preferred_element_type=F32)

    h = _rms(x1, gc_ref[...]).astype(BF16)
    q = jnp.dot(h, wq_ref[...], preferred_element_type=F32) * HEAD_DIM ** -0.5
    q = q.astype(BF16)
    outs = []
    for hh in range(MEM_HEADS):
        hs = slice(hh * HEAD_DIM, (hh + 1) * HEAD_DIM)
        lg = lax.dot_general(q[:, hs], km_ref[:, hs], _NT, preferred_element_type=F32)
        p = jnp.exp(lg - jnp.max(lg, axis=1, keepdims=True))
        o = jnp.dot(p.astype(BF16), vm_ref[:, hs], preferred_element_type=F32)
        outs.append((o / jnp.sum(p, axis=1, keepdims=True)).astype(BF16))
    o = jnp.concatenate(outs, axis=1)
    x2 = x1 + jnp.dot(o, wco_ref[...], preferred_element_type=F32)
    o_ref[...] = x2
    hn_ref[...] = _rms(x2, gn_ref[...]).astype(hn_ref.dtype)


def _merge_cross(o_sb, o_ds, w_sb, w_ds, w_out, proj, gate_offset, b_gate, x,
                 kv, g_cross, w_cq, w_co, g_next, batch, seq, *, tm):
    m, k = o_sb.shape
    d = w_sb.shape[1]
    n_mem = kv.shape[0] // batch
    width = MEM_HEADS * HEAD_DIM
    tm = min(tm, seq)
    assert gate_offset % d == 0 and seq % tm == 0
    g = gate_offset // d
    nt = seq // tm
    b_gate = b_gate.reshape(1, 2 * d).astype(F32)
    resident = pl.Buffered(1)

    def rows(width_, col=0):
        return pl.BlockSpec((tm, width_), lambda b, i: (b * nt + i, col))

    def whole(shape, *block):
        return pl.BlockSpec(shape, lambda b, i: block or (0,) * len(shape), pipeline_mode=resident)

    return pl.pallas_call(
        _merge_cross_body,
        grid=(batch, nt),
        in_specs=[
            rows(k), rows(k), whole((k, d)), whole((k, d)),
            rows(d, g), rows(d, g + 1), whole((1, d)), whole((1, d), 0, 1),
            whole((d, d)), rows(d),
            whole((1, d)), whole((d, width)),
            pl.BlockSpec((n_mem, width), lambda b, i: (b, 0)),
            pl.BlockSpec((n_mem, width), lambda b, i: (b, 1)),
            whole((width, d)), whole((1, d)),
        ],
        out_specs=[rows(d), rows(d)],
        out_shape=[jax.ShapeDtypeStruct((m, d), F32), jax.ShapeDtypeStruct((m, d), BF16)],
        compiler_params=_params("parallel", "parallel"),
        name="merge_out_cross",
    )(o_sb, o_ds, w_sb, w_ds, proj, proj, b_gate, b_gate, w_out, x,
      g_cross.reshape(1, d).astype(F32), w_cq, kv, kv, w_co, g_next.reshape(1, d).astype(F32))


def _delayed(u, tail, shift):
    rolled = pltpu.roll(u, shift, axis=0)
    row = lax.broadcasted_iota(I32, tail.shape, 0)
    head = jnp.where(row < shift, pltpu.roll(tail, shift, axis=0), rolled[:SUBLANES])
    return jnp.concatenate([head, rolled[SUBLANES:]], axis=0)


def _ffn_up_body(h_ref, wa_ref, wv_ref, cwa_ref, cwv_ref, cba_ref, cbv_ref, wd_ref, o_ref, wdb_ref,
                 wab_ref, wvb_ref, halo_ref, *, tiles_per_seq):
    i = pl.program_id(1)
    _round_chunks([wd_ref], [wdb_ref])

    @pl.when(i == 0)
    def _():
        wab_ref[...] = wa_ref[...].astype(BF16)
        wvb_ref[...] = wv_ref[...].astype(BF16)

    h = h_ref[...]
    tm = h.shape[0]
    sequence_start = i % tiles_per_seq == 0

    def conv(wb_ref, cw_ref, cb_ref, slot):
        u = jnp.dot(h, wb_ref[...], preferred_element_type=F32)
        tail = jnp.where(sequence_start, 0.0, halo_ref[slot])
        halo_ref[slot] = u[tm - SUBLANES:, :]
        c = cb_ref[...] + cw_ref[CONV_WIDTH - 1:CONV_WIDTH, :] * u
        for tap in range(CONV_WIDTH - 1):
            c = c + cw_ref[tap:tap + 1, :] * _delayed(u, tail, CONV_WIDTH - 1 - tap)
        return c

    a = conv(wab_ref, cwa_ref, cba_ref, 0)
    val = conv(wvb_ref, cwv_ref, cbv_ref, 1)
    o_ref[...] = (jax.nn.gelu(a) * val).astype(o_ref.dtype)


def _ffn_up_gate(h, w_up, conv_w, conv_b, w_down, seq, *, tm=ROW_TILE, tn=FFN_COL_TILE):
    m, d = h.shape
    two_ff = w_up.shape[1]
    d_ff = two_ff // 2
    tm, tn = min(tm, seq), min(tn, d_ff)
    assert seq % tm == 0 and d_ff % tn == 0 and tm >= SUBLANES >= CONV_WIDTH - 1
    nf, nt = d_ff // tn, m // tm
    conv_w = conv_w.astype(F32)
    conv_b = conv_b.reshape(1, two_ff).astype(F32)
    rider_in, rider_out, rider_shape = _rounding_riders([w_down], nf * nt, lambda j, i: j * nt + i)
    return pl.pallas_call(
        functools.partial(_ffn_up_body, tiles_per_seq=seq // tm),
        grid=(nf, nt),
        in_specs=[
            pl.BlockSpec((tm, d), lambda j, i: (i, 0)),
            pl.BlockSpec((d, tn), lambda j, i: (0, j)),
            pl.BlockSpec((d, tn), lambda j, i: (0, nf + j)),
            pl.BlockSpec((CONV_WIDTH, tn), lambda j, i: (0, j)),
            pl.BlockSpec((CONV_WIDTH, tn), lambda j, i: (0, nf + j)),
            pl.BlockSpec((1, tn), lambda j, i: (0, j)),
            pl.BlockSpec((1, tn), lambda j, i: (0, nf + j)),
        ] + rider_in,
        out_specs=[pl.BlockSpec((tm, tn), lambda j, i: (i, j))] + rider_out,
        out_shape=[jax.ShapeDtypeStruct((m, d_ff), BF16)] + rider_shape,
        scratch_shapes=[pltpu.VMEM((d, tn), BF16), pltpu.VMEM((d, tn), BF16),
                        pltpu.VMEM((2, SUBLANES, tn), F32)],
        compiler_params=_params("parallel", "arbitrary"),
        name="ffn_up_conv_gate",
    )(h, w_up, w_up, conv_w, conv_w, conv_b, conv_b, w_down)


def _ffn_down_body(a_ref, w_ref, x_ref, g_ref, o_ref, *, final_norm):
    k = pl.program_id(1)

    @pl.when(k == 0)
    def _():
        o_ref[...] = x_ref[...]

    o_ref[...] += jnp.dot(a_ref[...], w_ref[...], preferred_element_type=F32)

    if final_norm:
        @pl.when(k == pl.num_programs(1) - 1)
        def _():
            o_ref[...] = _rms(o_ref[...], g_ref[...])


def _ffn_down(a, w, x, g_final, *, tm=ROW_TILE, tk=FFN_K_TILE):
    m, kdim = a.shape
    d = w.shape[1]
    tm, tk = min(tm, m), min(tk, kdim)
    assert m % tm == 0 and kdim % tk == 0
    final_norm = g_final is not None
    g = (g_final if final_norm else jnp.ones((d,), F32)).reshape(1, d).astype(F32)
    return pl.pallas_call(
        functools.partial(_ffn_down_body, final_norm=final_norm),
        grid=(m // tm, kdim // tk),
        in_specs=[
            pl.BlockSpec((tm, tk), lambda i, k: (i, k)),
            pl.BlockSpec((tk, d), lambda i, k: (k, 0)),
            pl.BlockSpec((tm, d), lambda i, k: (i, 0)),
            pl.BlockSpec((1, d), lambda i, k: (0, 0)),
        ],
        out_specs=pl.BlockSpec((tm, d), lambda i, k: (i, 0)),
        out_shape=jax.ShapeDtypeStruct((m, d), F32),
        compiler_params=_params("parallel", "arbitrary"),
        name="ffn_down",
    )(a, w, x, g)


def _layer(x, mem, g_mix, w_in, b_gate, w_proj_sb, w_proj_dsa, w_out, rel_bias,
           g_cross, g_mem, w_cq, w_ckv, w_co, g_ffn, w_up, conv_w, conv_b, w_down, g_final, batch, seq):
    d = x.shape[1]
    width = N_HEADS * HEAD_DIM
    idx_w = IDX_HEADS * IDX_DIM
    o_qi = 6 * width
    o_ki = o_qi + idx_w
    o_wi = o_ki + IDX_DIM
    o_g = o_wi + IDX_HEADS

    zeros = jnp.zeros((d, LANES - IDX_DIM), F32)
    w_small = jnp.concatenate([
        w_in[:, o_ki:o_wi], zeros, zeros, w_in[:, o_ki:o_wi],
        jnp.pad(w_in[:, o_wi:o_g], ((0, 0), (0, LANES - IDX_HEADS)))], axis=1).astype(BF16)
    h, small = _rmsnorm_proj(x, g_mix, w_small, name="mixer_norm_index_proj", proj_dtype=F32,
                             keep_rows=True)
    w_in_t = w_in.T
    tq = ATT_BLOCK
    nq = seq // tq
    q_sb, k_sb, v_sb, q_ds, k_ds, v_ds, q_ix = (g * width for g in range(o_ki // width))
    gate_rows = tuple(o_g + g * width for g in range(2 * d // width))
    n_gate = len(gate_rows)
    main = _matmul_ws(h, w_in_t, name="in_proj_main",
                      first_rows=gate_rows + (q_sb, k_sb, q_ds, k_ds, q_ix),
                      out_dtype=BF16, tm=PROJ_ROWS, tn=width)
    v_t = _matmul_ws_t(h, w_in_t, name="in_proj_values", first_rows=(v_sb, v_ds), tn=width, tq=tq,
                       tm=PROJ_ROWS)
    v_t = v_t.reshape(2, batch, nq, width, tq)

    o_sb, w_proj_sb, w_proj_dsa, w_out, w_cq, w_co = _sb_attention(
        main, v_t, batch, seq, q_col=n_gate, k_col=n_gate + 1, v_branch=0, tq=tq,
        round_weights=(w_proj_sb, w_proj_dsa, w_out, w_cq, w_co))
    bias = _near_bias(rel_bias, tq)
    o_ds = _dsa_attention(main, v_t, small, bias, batch, seq, qd_col=n_gate + 2, kd_col=n_gate + 3,
                          qi_col=n_gate + 4, v_branch=1, tq=tq)

    kv = _rmsnorm_proj(mem, g_mem, w_ckv, name="mem_norm_kv_proj", proj_dtype=BF16, keep_rows=False)
    x, h_ffn = _merge_cross(o_sb, o_ds, w_proj_sb, w_proj_dsa, w_out, main, 0, b_gate, x, kv, g_cross,
                            w_cq, w_co, g_ffn, batch, seq, tm=MERGE_ROWS)

    act, w_down = _ffn_up_gate(h_ffn, w_up, conv_w, conv_b, w_down, seq)
    return _ffn_down(act, w_down, x, g_final)


def kernel(x, mem, g_mix, w_in, b_gate, w_proj_sb, w_proj_dsa, w_out, rel_bias, g_cross, g_mem,
           w_cq, w_ckv, w_co, g_ffn, w_up, conv_w, conv_b, w_down, g_final):
    batch, seq, d = x.shape
    h = x.reshape(batch * seq, d)
    mem2 = mem.reshape(batch * mem.shape[1], d)
    depth = g_mix.shape[0]
    for l in range(depth):
        h = _layer(h, mem2, g_mix[l], w_in[l], b_gate[l], w_proj_sb[l], w_proj_dsa[l], w_out[l],
                   rel_bias, g_cross[l], g_mem[l], w_cq[l], w_ckv[l], w_co[l], g_ffn[l], w_up[l],
                   conv_w[l], conv_b[l], w_down[l], g_final if l == depth - 1 else None, batch, seq)
    return h.reshape(batch, seq, d)
```

```python
import functools

import jax
import jax.numpy as jnp
from jax import lax
from jax.experimental import pallas as pl
from jax.experimental.pallas import tpu as pltpu

F32, BF16, I32 = jnp.float32, jnp.bfloat16, jnp.int32

EPS = 1e-6
HEAD_DIM = 128
N_HEADS = 8
IDX_HEADS = 16
IDX_DIM = 64
CHUNK = 64
TOPK_MAX = 256
N_BUCKETS = 32
MAX_DISTANCE = 128
MEM_HEADS = 4
CONV_WIDTH = 3

LANES = 128
SUBLANES = 8
VMEM_LIMIT_BYTES = 62 * 1024 * 1024
NEG_BIG = -1e30
EXP2_UNDERFLOW = -151.0
LOG2E = 1.4426950408889634
KEY_BITS = 32
SIGN_BIT = -(2 ** 31)

PROJ_ROWS = 2048
ROW_TILE = 1024
NORM_ROWS = 1024
MERGE_ROWS = 512
FFN_COL_TILE = 512
FFN_K_TILE = 2048
ATT_BLOCK = KEY_BITS * SUBLANES
RADIX_GROUPS = 8

_NT = (((1,), (1,)), ((), ()))


def _params(*sem):
    return pltpu.CompilerParams(dimension_semantics=sem, vmem_limit_bytes=VMEM_LIMIT_BYTES)


def _rms(x, g):
    inv = lax.rsqrt(jnp.mean(x * x, axis=-1, keepdims=True) + EPS)
    return x * inv * g


def _rounding_riders(weights, n_steps, step_index):
    in_specs, out_specs, out_shape = [], [], []
    for w in weights:
        rows, cols = w.shape
        assert rows % n_steps == 0 and (rows // n_steps) % (2 * SUBLANES) == 0
        for specs in (in_specs, out_specs):
            specs.append(pl.BlockSpec((rows // n_steps, cols), lambda *ids: (step_index(*ids), 0)))
        out_shape.append(jax.ShapeDtypeStruct(w.shape, BF16))
    return in_specs, out_specs, out_shape


def _round_chunks(src_refs, dst_refs):
    for src, dst in zip(src_refs, dst_refs):
        dst[...] = src[...].astype(BF16)


def _mm_ws_body(a_ref, wt_ref, o_ref, wb_ref):
    @pl.when(pl.program_id(1) == 0)
    def _():
        wb_ref[...] = wt_ref[...].astype(BF16)

    o_ref[...] = lax.dot_general(a_ref[...], wb_ref[...], _NT,
                                 preferred_element_type=F32).astype(o_ref.dtype)


def _row_window(first_rows, tn, k):
    assert all(r % SUBLANES == 0 for r in first_rows)

    def index_map(j, i):
        row = jnp.int32(first_rows[0])
        for step, first in enumerate(first_rows[1:], start=1):
            row = jnp.where(j >= step, first, row)
        return pl.multiple_of(row, SUBLANES), 0

    return pl.BlockSpec((pl.Element(tn), pl.Element(k)), index_map)


def _matmul_ws(a, w_t, *, name, first_rows, out_dtype, tm, tn):
    m, k = a.shape
    tm = min(tm, m)
    assert m % tm == 0 and max(first_rows) + tn <= w_t.shape[0] and w_t.shape[1] == k
    return pl.pallas_call(
        _mm_ws_body,
        grid=(len(first_rows), m // tm),
        in_specs=[pl.BlockSpec((tm, k), lambda j, i: (i, 0)), _row_window(first_rows, tn, k)],
        out_specs=pl.BlockSpec((tm, tn), lambda j, i: (i, j)),
        out_shape=jax.ShapeDtypeStruct((m, len(first_rows) * tn), out_dtype),
        scratch_shapes=[pltpu.VMEM((tn, k), BF16)],
        compiler_params=_params("parallel", "arbitrary"),
        name=name,
    )(a, w_t)


def _mm_ws_t_body(wt_ref, a_ref, o_ref, wb_ref):
    @pl.when(pl.program_id(1) == 0)
    def _():
        wb_ref[...] = wt_ref[...].astype(BF16)

    res = lax.dot_general(wb_ref[...], a_ref[...], _NT, preferred_element_type=F32)
    n_blocks, _, tq = o_ref.shape
    for blk in range(n_blocks):
        o_ref[blk] = res[:, blk * tq:(blk + 1) * tq].astype(o_ref.dtype)


def _matmul_ws_t(a, w_t, *, name, first_rows, tn, tq, tm):
    m, k = a.shape
    tm = min(tm, m)
    assert m % tm == 0 and tm % tq == 0 and max(first_rows) + tn <= w_t.shape[0] and w_t.shape[1] == k
    per_step = tm // tq
    return pl.pallas_call(
        _mm_ws_t_body,
        grid=(len(first_rows), m // tm),
        in_specs=[_row_window(first_rows, tn, k), pl.BlockSpec((tm, k), lambda j, i: (i, 0))],
        out_specs=pl.BlockSpec((None, per_step, tn, tq), lambda j, i: (j, i, 0, 0)),
        out_shape=jax.ShapeDtypeStruct((len(first_rows), m // tq, tn, tq), BF16),
        scratch_shapes=[pltpu.VMEM((tn, k), BF16)],
        compiler_params=_params("parallel", "arbitrary"),
        name=name,
    )(w_t, a)


def _norm_proj_body(x_ref, g_ref, w_ref, *out_refs):
    h = _rms(x_ref[...], g_ref[...]).astype(BF16)
    p_ref = out_refs[-1]
    p_ref[...] = jnp.dot(h, w_ref[...].astype(BF16), preferred_element_type=F32).astype(p_ref.dtype)
    if len(out_refs) == 2:
        out_refs[0][...] = h


def _rmsnorm_proj(x, g, w, *, name, proj_dtype, keep_rows, tm=NORM_ROWS):
    m, d = x.shape
    n = w.shape[1]
    tm = min(tm, m)
    assert m % tm == 0
    out_specs = [pl.BlockSpec((tm, n), lambda i: (i, 0))]
    out_shape = [jax.ShapeDtypeStruct((m, n), proj_dtype)]
    if keep_rows:
        out_specs.insert(0, pl.BlockSpec((tm, d), lambda i: (i, 0)))
        out_shape.insert(0, jax.ShapeDtypeStruct((m, d), BF16))
    out = pl.pallas_call(
        _norm_proj_body,
        grid=(m // tm,),
        in_specs=[pl.BlockSpec((tm, d), lambda i: (i, 0)), pl.BlockSpec((1, d), lambda i: (0, 0)),
                  pl.BlockSpec((d, n), lambda i: (0, 0))],
        out_specs=out_specs,
        out_shape=out_shape,
        compiler_params=_params("parallel"),
        name=name,
    )(x, g.reshape(1, d).astype(F32), w)
    return out if keep_rows else out[0]


def _sb_body(q_ref, k_ref, vt_ref, tri_ref, *refs, tq, n_riders):
    o_ref, qs_ref, z_ref, lb_ref, wb_ref, acc_ref = refs[n_riders], *refs[2 * n_riders + 1:]
    _round_chunks(refs[:n_riders], refs[n_riders + 1:2 * n_riders + 1])
    i = pl.program_id(1)
    shape = (tq, tq)
    before = lax.broadcasted_iota(I32, shape, 0) < lax.broadcasted_iota(I32, shape, 1)
    heads = [slice(h * HEAD_DIM, (h + 1) * HEAD_DIM) for h in range(N_HEADS)]
    qs_ref[...] = (q_ref[...].astype(F32) * (HEAD_DIM ** -0.5 * LOG2E)).astype(BF16)
    acc_ref[...] = jnp.zeros(acc_ref.shape, F32)

    def tile(j, carry, diagonal):
        start = pl.multiple_of(j * tq, tq)
        for h, hs in enumerate(heads):
            z = lax.dot_general(k_ref[pl.ds(start, tq), hs], qs_ref[:, hs], _NT,
                                preferred_element_type=F32)
            neg_abs = lax.bitcast_convert_type(lax.bitcast_convert_type(z, I32) | SIGN_BIT, F32)
            log_lose = jnp.maximum(z, 0.0) + jnp.log2(1.0 + jnp.exp2(neg_abs))
            if diagonal:
                log_lose = jnp.where(before, log_lose, 0.0)
            z_ref[h] = z
            lb_ref[h] = log_lose.astype(BF16)
        new_carry = []
        for h in range(N_HEADS):
            c = carry[h:h + 1, :] - jnp.dot(tri_ref[...], lb_ref[h], preferred_element_type=F32)
            w = jnp.exp2(z_ref[h] + c)
            if diagonal:
                w = jnp.where(before, w, 0.0)
            wb_ref[h] = w.astype(BF16)
            new_carry.append(c[0:1, :])
        for h, hs in enumerate(heads):
            acc_ref[h] += jnp.dot(vt_ref[j, hs, :], wb_ref[h], preferred_element_type=F32)
        return jnp.concatenate(new_carry, axis=0)

    carry = tile(i, jnp.zeros((N_HEADS, tq), F32), True)

    def cond(state):
        j, live, _ = state
        return jnp.logical_and(j >= 0, live > EXP2_UNDERFLOW)

    def body(state):
        j, _, carry = state
        carry = tile(j, carry, False)
        return j - 1, jnp.max(carry), carry

    lax.while_loop(cond, body, (i - 1, jnp.max(carry), carry))
    for h, hs in enumerate(heads):
        o_ref[:, hs] = acc_ref[h].T.astype(o_ref.dtype)


def _sb_attention(qk, v_t, batch, seq, *, q_col, k_col, v_branch, tq, round_weights=()):
    nq = seq // tq
    width = N_HEADS * HEAD_DIM
    tri = (jnp.arange(tq)[None, :] >= jnp.arange(tq)[:, None]).astype(BF16)
    resident = pl.Buffered(1)
    rider_in, rider_out, rider_shape = _rounding_riders(round_weights, batch * nq,
                                                        lambda b, i: b * nq + i)
    return pl.pallas_call(
        functools.partial(_sb_body, tq=tq, n_riders=len(round_weights)),
        grid=(batch, nq),
        in_specs=[
            pl.BlockSpec((tq, width), lambda b, i: (b * nq + i, q_col)),
            pl.BlockSpec((seq, width), lambda b, i: (b, k_col), pipeline_mode=resident),
            pl.BlockSpec((None, None, nq, width, tq), lambda b, i: (v_branch, b, 0, 0, 0),
                         pipeline_mode=resident),
            pl.BlockSpec((tq, tq), lambda b, i: (0, 0), pipeline_mode=resident),
        ] + rider_in,
        out_specs=[pl.BlockSpec((tq, width), lambda b, i: (b * nq + i, 0))] + rider_out,
        out_shape=[jax.ShapeDtypeStruct((batch * seq, width), BF16)] + rider_shape,
        scratch_shapes=[
            pltpu.VMEM((tq, width), BF16),
            pltpu.VMEM((N_HEADS, tq, tq), F32),
            pltpu.VMEM((N_HEADS, tq, tq), BF16),
            pltpu.VMEM((N_HEADS, tq, tq), BF16),
            pltpu.VMEM((N_HEADS, HEAD_DIM, tq), F32),
        ],
        compiler_params=_params("parallel", "arbitrary"),
        name="sb_attention",
    )(qk, qk, v_t, tri, *round_weights)


def _bucket_thresholds():
    nb = N_BUCKETS // 2
    max_exact = nb // 2
    span = nb - max_exact
    out = []
    for k in range(1, span):
        n = max_exact
        while n ** span * max_exact ** k < MAX_DISTANCE ** k * max_exact ** span:
            n += 1
        out.append(n)
    return max_exact, out


def _bias_body(rb_ref, o_ref, *, tq):
    nb = N_BUCKETS // 2
    max_exact, steps = _bucket_thresholds()
    shape = (2 * tq, tq)
    rel = lax.broadcasted_iota(I32, shape, 0) - lax.broadcasted_iota(I32, shape, 1) - tq
    n = jnp.abs(rel)
    large = jnp.full(shape, max_exact, I32)
    for t in steps:
        large = large + (n >= t).astype(I32)
    bucket = jnp.where(rel > 0, nb, 0) + jnp.where(n < max_exact, n, large)
    for h in range(N_HEADS):
        val = jnp.zeros(shape, F32)
        for b in range(N_BUCKETS):
            val = jnp.where(bucket == b, rb_ref[b, h], val)
        o_ref[h] = (val - rb_ref[nb - 1, h]) * LOG2E


def _near_bias(rel_bias, tq):
    return pl.pallas_call(
        functools.partial(_bias_body, tq=tq),
        in_specs=[pl.BlockSpec(memory_space=pltpu.SMEM)],
        out_specs=pl.BlockSpec(memory_space=pltpu.VMEM),
        out_shape=jax.ShapeDtypeStruct((N_HEADS, 2 * tq, tq), F32),
        compiler_params=pltpu.CompilerParams(vmem_limit_bytes=VMEM_LIMIT_BYTES),
        name="dsa_near_bias",
    )(rel_bias.astype(F32))


def _order_key(x):
    bits = lax.bitcast_convert_type(x, I32)
    return bits ^ ((bits >> 31) | SIGN_BIT)


def _order_key_to_float(key):
    return lax.bitcast_convert_type(key ^ ((~key >> 31) | SIGN_BIT), F32)


def _bit_transpose32(words):
    a = list(words)
    j, m = 16, 0x0000FFFF
    while j:
        mask = jnp.int32(m - (1 << 32) if m >= 1 << 31 else m)
        k = 0
        while k < 32:
            t = (lax.shift_right_logical(a[k], jnp.int32(j)) ^ a[k + j]) & mask
            a[k] = a[k] ^ lax.shift_left(t, jnp.int32(j))
            a[k + j] = a[k + j] ^ t
            k = (k + j + 1) & ~j
        j >>= 1
        m = (m ^ (m << j)) & 0xFFFFFFFF
    return a


def _dsa_body(qd_ref, qi_ref, wq_ref, kd_ref, vt_ref, ki_ref, bias_ref, tri_ref, o_ref,
              sc_ref, plane_ref, qs_ref, m_ref, l_ref, acc_ref, lg_ref, bmax_ref, *, tq, top):
    i = pl.program_id(1)
    shape = (tq, tq)
    key_row = lax.broadcasted_iota(I32, shape, 0)
    qry_col = lax.broadcasted_iota(I32, shape, 1)
    visible = key_row // CHUNK <= qry_col // CHUNK

    w_t = (wq_ref[...] * (IDX_DIM ** -0.5 * IDX_HEADS ** -0.5)).T

    heads_per_vreg = LANES // IDX_DIM

    def score_tile(j):
        start = pl.multiple_of(j * tq, tq)
        ki = [ki_ref[pl.ds(start, tq), c * LANES:(c + 1) * LANES].astype(BF16)
              for c in range(heads_per_vreg)]
        s = jnp.zeros(shape, F32)
        for h in range(IDX_HEADS):
            g, c = divmod(h, heads_per_vreg)
            d = lax.dot_general(ki[c], qi_ref[:, g * LANES:(g + 1) * LANES], _NT,
                                preferred_element_type=F32)
            s = s + w_t[h:h + 1, :] * jnp.maximum(d, 0.0)
        sc_ref[j] = s

    def key_planes(j):
        ukey = _order_key(sc_ref[j])
        planes = _bit_transpose32([ukey[g * SUBLANES:(g + 1) * SUBLANES, :] for g in range(KEY_BITS)])
        for b in range(KEY_BITS):
            plane_ref[b, pl.ds(pl.multiple_of(j * SUBLANES, SUBLANES), SUBLANES), :] = planes[b]

    @pl.when(i == 0)
    def _():
        plane_ref[...] = jnp.zeros(plane_ref.shape, I32)

    def score_steps(first, count):
        for j in range(count):
            key_planes(first + j)
            score_tile(first + j + 1)

    def score_quad(p, carry):
        score_steps(4 * p, 4)
        return carry

    score_tile(0)
    lax.fori_loop(0, i // 4, score_quad, 0)

    @pl.when(i % 4 >= 2)
    def _():
        score_steps(4 * (i // 4), 2)

    @pl.when(i % 2 == 1)
    def _():
        score_steps(i - 1, 1)

    key_planes(i)
    sc_ref[i] = jnp.where(visible, sc_ref[i], -jnp.inf)

    def popcount_rows(words):
        return jnp.sum(lax.population_count(words), axis=0, keepdims=True)

    def radix_select(n_rows):
        block_of_row = lax.broadcasted_iota(I32, (n_rows, tq), 0) // SUBLANES
        qry_of_col = lax.broadcasted_iota(I32, (n_rows, tq), 1)
        n_bits = (qry_of_col // CHUNK + 1) * (CHUNK // SUBLANES)
        diag_bits = jnp.where(n_bits >= KEY_BITS, -1, lax.shift_left(jnp.int32(1), n_bits) - 1)
        cand0 = jnp.where(block_of_row < i, -1, jnp.where(block_of_row == i, diag_bits, 0))

        def bit_step(t, state):
            cand, n_above, thr_bits = state
            b = KEY_BITS - 1 - t
            ones = cand & plane_ref[b, :n_rows]
            n_ones = popcount_rows(ones)
            take = n_above + n_ones >= top
            cand = jnp.where(take, ones, cand ^ ones)
            n_above = jnp.where(take, n_above, n_above + n_ones)
            thr_bits = thr_bits | jnp.where(take, lax.shift_left(jnp.int32(1), b), 0)
            return cand, n_above, thr_bits

        zero = jnp.zeros((1, tq), I32)
        cand, n_above, thr_bits = lax.fori_loop(0, KEY_BITS, bit_step, (cand0, zero, zero), unroll=4)
        return n_above, thr_bits, popcount_rows(cand)

    n_groups = min(RADIX_GROUPS, plane_ref.shape[1] // SUBLANES)

    def select_from(group):
        if group == n_groups - 1:
            return radix_select(plane_ref.shape[1])
        rows = (group + 1) * (plane_ref.shape[1] // n_groups)
        return lax.cond(i * SUBLANES < rows, functools.partial(radix_select, rows),
                        functools.partial(select_from, group + 1))

    n_above, thr_bits, n_equal = select_from(0)
    qry = lax.broadcasted_iota(I32, (1, tq), 1)
    n_visible = i * tq + (qry // CHUNK + 1) * CHUNK
    wanted = n_visible > top
    thr = jnp.where(wanted, _order_key_to_float(thr_bits), jnp.finfo(F32).min)
    tied = jnp.logical_and(wanted, n_above + n_equal > top)
    c_hi = n_above

    ones = jnp.ones((2 * SUBLANES, tq), BF16)

    def plain_mask(j, carry):
        sc_ref[j] = jnp.where(sc_ref[j] >= thr, 0.0, NEG_BIG)
        return carry

    def tie_mask(j, seen):
        s = sc_ref[j]
        equal = s == thr
        rank = jnp.dot(tri_ref[...], equal.astype(BF16), preferred_element_type=F32) + seen
        quota = jnp.where(tied, (top - c_hi).astype(F32), jnp.inf)
        keep_equal = jnp.where(rank < quota, 0.0, NEG_BIG)
        sc_ref[j] = jnp.where(s > thr, 0.0, jnp.where(equal, keep_equal, NEG_BIG))
        return seen + jnp.sum(equal.astype(F32), axis=0, keepdims=True)

    def with_ties():
        lax.fori_loop(0, i + 1, tie_mask, jnp.zeros((1, tq), F32))
        return jnp.int32(0)

    def without_ties():
        return lax.fori_loop(0, i + 1, plain_mask, jnp.int32(0))

    lax.cond(jnp.max(tied.astype(I32)) > 0, with_ties, without_ties)

    qs_ref[...] = (qd_ref[...].astype(F32) * (HEAD_DIM ** -0.5 * LOG2E)).astype(BF16)
    m_ref[...] = jnp.full(m_ref.shape, NEG_BIG, F32)
    l_ref[...] = jnp.zeros(l_ref.shape, F32)
    acc_ref[...] = jnp.zeros(acc_ref.shape, F32)

    heads = [slice(h * HEAD_DIM, (h + 1) * HEAD_DIM) for h in range(N_HEADS)]

    far, prev, diag = None, 0, 1

    def logits(u, near, slot):
        j = i - u
        start = pl.multiple_of(j * tq, tq)
        mask = sc_ref[j]
        block_max = []
        for h, hs in enumerate(heads):
            lg = lax.dot_general(kd_ref[pl.ds(start, tq), hs], qs_ref[:, hs], _NT,
                                 preferred_element_type=F32)
            if near is not None:
                lg = lg + bias_ref[h, near * tq:(near + 1) * tq, :]
            lg = lg + mask
            lg_ref[slot, h] = lg
            block_max.append(jnp.max(lg, axis=0, keepdims=True))
        bmax_ref[slot] = jnp.concatenate(block_max, axis=0)

    def values(u, slot):
        j = i - u
        m_old = m_ref[...]
        m_new = jnp.maximum(m_old, bmax_ref[slot])
        alpha = jnp.exp2(m_old - m_new)
        m_ref[...] = m_new
        denom = []
        for h, hs in enumerate(heads):
            p = jnp.exp2(lg_ref[slot, h] - m_new[h:h + 1, :]).astype(BF16)
            v_ext = jnp.concatenate([vt_ref[j, hs, :], ones], axis=0)
            pv = jnp.dot(v_ext, p, preferred_element_type=F32)
            acc_ref[h] = alpha[h:h + 1, :] * acc_ref[h] + pv[:HEAD_DIM]
            denom.append(pv[HEAD_DIM:HEAD_DIM + 1])
        l_ref[...] = alpha * l_ref[...] + jnp.concatenate(denom, axis=0)

    def even_step(u, near_a, near_b):
        logits(u - 1, near_a, 1)
        values(u, 0)
        logits(u - 2, near_b, 0)
        values(u - 1, 1)

    @pl.when(i == 0)
    def _():
        logits(0, diag, 0)
        values(0, 0)

    @pl.when(i == 1)
    def _():
        logits(1, prev, 1)

    @pl.when(jnp.logical_and(i >= 2, i % 2 == 1))
    def _():
        logits(i, far, 1)
        logits(i - 1, far, 0)
        values(i, 1)

    @pl.when(jnp.logical_and(i >= 2, i % 2 == 0))
    def _():
        logits(i, far, 0)

    n_far_pairs = i // 2 - 1

    def far_quad(k, carry):
        u = 2 * (i // 2 - 2 * k)
        even_step(u, far, far)
        even_step(u - 2, far, far)
        return carry

    lax.fori_loop(0, n_far_pairs // 2, far_quad, 0)

    @pl.when(jnp.logical_and(n_far_pairs > 0, n_far_pairs % 2 == 1))
    def _():
        even_step(4, far, far)

    @pl.when(i >= 2)
    def _():
        even_step(2, prev, diag)

    @pl.when(i == 1)
    def _():
        logits(0, diag, 0)
        values(1, 1)

    @pl.when(i >= 1)
    def _():
        values(0, 0)

    for h in range(N_HEADS):
        o = acc_ref[h] / l_ref[h:h + 1, :]
        o_ref[:, h * HEAD_DIM:(h + 1) * HEAD_DIM] = o.T.astype(o_ref.dtype)


def _dsa_attention(main, v_t, small, bias, batch, seq, *, qd_col, kd_col, qi_col, v_branch, tq):
    nq = seq // tq
    width = N_HEADS * HEAD_DIM
    assert IDX_HEADS * IDX_DIM == width and tq == KEY_BITS * SUBLANES and tq % CHUNK == 0
    assert nq % min(RADIX_GROUPS, nq) == 0
    key_copies = LANES // IDX_DIM
    top = min(TOPK_MAX, seq // 4)
    tri = (jnp.arange(tq)[None, :] < jnp.arange(tq)[:, None]).astype(BF16)
    resident = pl.Buffered(1)
    return pl.pallas_call(
        functools.partial(_dsa_body, tq=tq, top=top),
        grid=(batch, nq),
        in_specs=[
            pl.BlockSpec((tq, width), lambda b, i: (b * nq + i, qd_col)),
            pl.BlockSpec((tq, IDX_HEADS * IDX_DIM), lambda b, i: (b * nq + i, qi_col)),
            pl.BlockSpec((tq, LANES), lambda b, i: (b * nq + i, key_copies)),
            pl.BlockSpec((seq, width), lambda b, i: (b, kd_col), pipeline_mode=resident),
            pl.BlockSpec((None, None, nq, width, tq), lambda b, i: (v_branch, b, 0, 0, 0),
                         pipeline_mode=resident),
            pl.BlockSpec((seq, key_copies * LANES), lambda b, i: (b, 0), pipeline_mode=resident),
            pl.BlockSpec((N_HEADS, 2 * tq, tq), lambda b, i: (0, 0, 0), pipeline_mode=resident),
            pl.BlockSpec((tq, tq), lambda b, i: (0, 0), pipeline_mode=resident),
        ],
        out_specs=pl.BlockSpec((tq, width), lambda b, i: (b * nq + i, 0)),
        out_shape=jax.ShapeDtypeStruct((batch * seq, width), BF16),
        scratch_shapes=[
            pltpu.VMEM((nq, tq, tq), F32),
            pltpu.VMEM((KEY_BITS, nq * SUBLANES, tq), I32),
            pltpu.VMEM((tq, width), BF16),
            pltpu.VMEM((N_HEADS, tq), F32),
            pltpu.VMEM((N_HEADS, tq), F32),
            pltpu.VMEM((N_HEADS, HEAD_DIM, tq), F32),
            pltpu.VMEM((2, N_HEADS, tq, tq), F32),
            pltpu.VMEM((2, N_HEADS, tq), F32),
        ],
        compiler_params=_params("parallel", "arbitrary"),
        name="dsa_attention",
    )(main, main, small, main, v_t, small, bias, tri)


def _merge_cross_body(osb_ref, ods_ref, wsb_ref, wds_ref, gsb_ref, gds_ref, bsb_ref, bds_ref, wo_ref,
                      x_ref, gc_ref, wq_ref, km_ref, vm_ref, wco_ref, gn_ref, o_ref, hn_ref):
    p_sb = jnp.dot(osb_ref[...], wsb_ref[...], preferred_element_type=F32)
    p_ds = jnp.dot(ods_ref[...], wds_ref[...], preferred_element_type=F32)
    g_sb = jax.nn.sigmoid(gsb_ref[...].astype(F32) + bsb_ref[...])
    g_ds = jax.nn.sigmoid(gds_ref[...].astype(F32) + bds_ref[...])
    merged = (g_sb * p_sb + g_ds * p_ds).astype(BF16)
    x1 = x_ref[...] + jnp.dot(merged, wo_ref[...], preferred_element_type=F32)

    h = _rms(x1, gc_ref[...]).astype(BF16)
    q = jnp.dot(h, wq_ref[...], preferred_element_type=F32) * HEAD_DIM ** -0.5
    q = q.astype(BF16)
    outs = []
    for hh in range(MEM_HEADS):
        hs = slice(hh * HEAD_DIM, (hh + 1) * HEAD_DIM)
        lg = lax.dot_general(q[:, hs], km_ref[:, hs], _NT, preferred_element_type=F32)
        p = jnp.exp(lg - jnp.max(lg, axis=1, keepdims=True))
        o = jnp.dot(p.astype(BF16), vm_ref[:, hs], preferred_element_type=F32)
        outs.append((o / jnp.sum(p, axis=1, keepdims=True)).astype(BF16))
    o = jnp.concatenate(outs, axis=1)
    x2 = x1 + jnp.dot(o, wco_ref[...], preferred_element_type=F32)
    o_ref[...] = x2
    hn_ref[...] = _rms(x2, gn_ref[...]).astype(hn_ref.dtype)


def _merge_cross(o_sb, o_ds, w_sb, w_ds, w_out, proj, gate_offset, b_gate, x,
                 kv, g_cross, w_cq, w_co, g_next, batch, seq, *, tm):
    m, k = o_sb.shape
    d = w_sb.shape[1]
    n_mem = kv.shape[0] // batch
    width = MEM_HEADS * HEAD_DIM
    tm = min(tm, seq)
    assert gate_offset % d == 0 and seq % tm == 0
    g = gate_offset // d
    nt = seq // tm
    b_gate = b_gate.reshape(1, 2 * d).astype(F32)
    resident = pl.Buffered(1)

    def rows(width_, col=0):
        return pl.BlockSpec((tm, width_), lambda b, i: (b * nt + i, col))

    def whole(shape, *block):
        return pl.BlockSpec(shape, lambda b, i: block or (0,) * len(shape), pipeline_mode=resident)

    return pl.pallas_call(
        _merge_cross_body,
        grid=(batch, nt),
        in_specs=[
            rows(k), rows(k), whole((k, d)), whole((k, d)),
            rows(d, g), rows(d, g + 1), whole((1, d)), whole((1, d), 0, 1),
            whole((d, d)), rows(d),
            whole((1, d)), whole((d, width)),
            pl.BlockSpec((n_mem, width), lambda b, i: (b, 0)),
            pl.BlockSpec((n_mem, width), lambda b, i: (b, 1)),
            whole((width, d)), whole((1, d)),
        ],
        out_specs=[rows(d), rows(d)],
        out_shape=[jax.ShapeDtypeStruct((m, d), F32), jax.ShapeDtypeStruct((m, d), BF16)],
        compiler_params=_params("parallel", "parallel"),
        name="merge_out_cross",
    )(o_sb, o_ds, w_sb, w_ds, proj, proj, b_gate, b_gate, w_out, x,
      g_cross.reshape(1, d).astype(F32), w_cq, kv, kv, w_co, g_next.reshape(1, d).astype(F32))


def _delayed(u, tail, shift):
    rolled = pltpu.roll(u, shift, axis=0)
    row = lax.broadcasted_iota(I32, tail.shape, 0)
    head = jnp.where(row < shift, pltpu.roll(tail, shift, axis=0), rolled[:SUBLANES])
    return jnp.concatenate([head, rolled[SUBLANES:]], axis=0)


def _ffn_up_body(h_ref, wa_ref, wv_ref, cwa_ref, cwv_ref, cba_ref, cbv_ref, wd_ref, o_ref, wdb_ref,
                 wab_ref, wvb_ref, halo_ref, *, tiles_per_seq):
    i = pl.program_id(1)
    _round_chunks([wd_ref], [wdb_ref])

    @pl.when(i == 0)
    def _():
        wab_ref[...] = wa_ref[...].astype(BF16)
        wvb_ref[...] = wv_ref[...].astype(BF16)

    h = h_ref[...]
    tm = h.shape[0]
    sequence_start = i % tiles_per_seq == 0

    def conv(wb_ref, cw_ref, cb_ref, slot):
        u = jnp.dot(h, wb_ref[...], preferred_element_type=F32)
        tail = jnp.where(sequence_start, 0.0, halo_ref[slot])
        halo_ref[slot] = u[tm - SUBLANES:, :]
        c = cb_ref[...] + cw_ref[CONV_WIDTH - 1:CONV_WIDTH, :] * u
        for tap in range(CONV_WIDTH - 1):
            c = c + cw_ref[tap:tap + 1, :] * _delayed(u, tail, CONV_WIDTH - 1 - tap)
        return c

    a = conv(wab_ref, cwa_ref, cba_ref, 0)
    val = conv(wvb_ref, cwv_ref, cbv_ref, 1)
    o_ref[...] = (jax.nn.gelu(a) * val).astype(o_ref.dtype)


def _ffn_up_gate(h, w_up, conv_w, conv_b, w_down, seq, *, tm=ROW_TILE, tn=FFN_COL_TILE):
    m, d = h.shape
    two_ff = w_up.shape[1]
    d_ff = two_ff // 2
    tm, tn = min(tm, seq), min(tn, d_ff)
    assert seq % tm == 0 and d_ff % tn == 0 and tm >= SUBLANES >= CONV_WIDTH - 1
    nf, nt = d_ff // tn, m // tm
    conv_w = conv_w.astype(F32)
    conv_b = conv_b.reshape(1, two_ff).astype(F32)
    rider_in, rider_out, rider_shape = _rounding_riders([w_down], nf * nt, lambda j, i: j * nt + i)
    return pl.pallas_call(
        functools.partial(_ffn_up_body, tiles_per_seq=seq // tm),
        grid=(nf, nt),
        in_specs=[
            pl.BlockSpec((tm, d), lambda j, i: (i, 0)),
            pl.BlockSpec((d, tn), lambda j, i: (0, j)),
            pl.BlockSpec((d, tn), lambda j, i: (0, nf + j)),
            pl.BlockSpec((CONV_WIDTH, tn), lambda j, i: (0, j)),
            pl.BlockSpec((CONV_WIDTH, tn), lambda j, i: (0, nf + j)),
            pl.BlockSpec((1, tn), lambda j, i: (0, j)),
            pl.BlockSpec((1, tn), lambda j, i: (0, nf + j)),
        ] + rider_in,
        out_specs=[pl.BlockSpec((tm, tn), lambda j, i: (i, j))] + rider_out,
        out_shape=[jax.ShapeDtypeStruct((m, d_ff), BF16)] + rider_shape,
        scratch_shapes=[pltpu.VMEM((d, tn), BF16), pltpu.VMEM((d, tn), BF16),
                        pltpu.VMEM((2, SUBLANES, tn), F32)],
        compiler_params=_params("parallel", "arbitrary"),
        name="ffn_up_conv_gate",
    )(h, w_up, w_up, conv_w, conv_w, conv_b, conv_b, w_down)


def _ffn_down_body(a_ref, w_ref, x_ref, g_ref, o_ref, *, final_norm):
    k = pl.program_id(1)

    @pl.when(k == 0)
    def _():
        o_ref[...] = x_ref[...]

    o_ref[...] += jnp.dot(a_ref[...], w_ref[...], preferred_element_type=F32)

    if final_norm:
        @pl.when(k == pl.num_programs(1) - 1)
        def _():
            o_ref[...] = _rms(o_ref[...], g_ref[...])


def _ffn_down(a, w, x, g_final, *, tm=ROW_TILE, tk=FFN_K_TILE):
    m, kdim = a.shape
    d = w.shape[1]
    tm, tk = min(tm, m), min(tk, kdim)
    assert m % tm == 0 and kdim % tk == 0
    final_norm = g_final is not None
    g = (g_final if final_norm else jnp.ones((d,), F32)).reshape(1, d).astype(F32)
    return pl.pallas_call(
        functools.partial(_ffn_down_body, final_norm=final_norm),
        grid=(m // tm, kdim // tk),
        in_specs=[
            pl.BlockSpec((tm, tk), lambda i, k: (i, k)),
            pl.BlockSpec((tk, d), lambda i, k: (k, 0)),
            pl.BlockSpec((tm, d), lambda i, k: (i, 0)),
            pl.BlockSpec((1, d), lambda i, k: (0, 0)),
        ],
        out_specs=pl.BlockSpec((tm, d), lambda i, k: (i, 0)),
        out_shape=jax.ShapeDtypeStruct((m, d), F32),
        compiler_params=_params("parallel", "arbitrary"),
        name="ffn_down",
    )(a, w, x, g)


def _layer(x, mem, g_mix, w_in, b_gate, w_proj_sb, w_proj_dsa, w_out, rel_bias,
           g_cross, g_mem, w_cq, w_ckv, w_co, g_ffn, w_up, conv_w, conv_b, w_down, g_final, batch, seq):
    d = x.shape[1]
    width = N_HEADS * HEAD_DIM
    idx_w = IDX_HEADS * IDX_DIM
    o_qi = 6 * width
    o_ki = o_qi + idx_w
    o_wi = o_ki + IDX_DIM
    o_g = o_wi + IDX_HEADS

    zeros = jnp.zeros((d, LANES - IDX_DIM), F32)
    w_small = jnp.concatenate([
        w_in[:, o_ki:o_wi], zeros, zeros, w_in[:, o_ki:o_wi],
        jnp.pad(w_in[:, o_wi:o_g], ((0, 0), (0, LANES - IDX_HEADS)))], axis=1).astype(BF16)
    h, small = _rmsnorm_proj(x, g_mix, w_small, name="mixer_norm_index_proj", proj_dtype=F32,
                             keep_rows=True)
    w_in_t = w_in.T
    tq = ATT_BLOCK
    nq = seq // tq
    q_sb, k_sb, v_sb, q_ds, k_ds, v_ds, q_ix = (g * width for g in range(o_ki // width))
    gate_rows = tuple(o_g + g * width for g in range(2 * d // width))
    n_gate = len(gate_rows)
    main = _matmul_ws(h, w_in_t, name="in_proj_main",
                      first_rows=gate_rows + (q_sb, k_sb, q_ds, k_ds, q_ix),
                      out_dtype=BF16, tm=PROJ_ROWS, tn=width)
    v_t = _matmul_ws_t(h, w_in_t, name="in_proj_values", first_rows=(v_sb, v_ds), tn=width, tq=tq,
                       tm=PROJ_ROWS)
    v_t = v_t.reshape(2, batch, nq, width, tq)

    o_sb, w_proj_sb, w_proj_dsa, w_out, w_cq, w_co = _sb_attention(
        main, v_t, batch, seq, q_col=n_gate, k_col=n_gate + 1, v_branch=0, tq=tq,
        round_weights=(w_proj_sb, w_proj_dsa, w_out, w_cq, w_co))
    bias = _near_bias(rel_bias, tq)
    o_ds = _dsa_attention(main, v_t, small, bias, batch, seq, qd_col=n_gate + 2, kd_col=n_gate + 3,
                          qi_col=n_gate + 4, v_branch=1, tq=tq)

    kv = _rmsnorm_proj(mem, g_mem, w_ckv, name="mem_norm_kv_proj", proj_dtype=BF16, keep_rows=False)
    x, h_ffn = _merge_cross(o_sb, o_ds, w_proj_sb, w_proj_dsa, w_out, main, 0, b_gate, x, kv, g_cross,
                            w_cq, w_co, g_ffn, batch, seq, tm=MERGE_ROWS)

    act, w_down = _ffn_up_gate(h_ffn, w_up, conv_w, conv_b, w_down, seq)
    return _ffn_down(act, w_down, x, g_final)


def kernel(x, mem, g_mix, w_in, b_gate, w_proj_sb, w_proj_dsa, w_out, rel_bias, g_cross, g_mem,
           w_cq, w_ckv, w_co, g_ffn, w_up, conv_w, conv_b, w_down, g_final):
    batch, seq, d = x.shape
    h = x.reshape(batch * seq, d)
    mem2 = mem.reshape(batch * mem.shape[1], d)
    depth = g_mix.shape[0]
    for l in range(depth):
        h = _layer(h, mem2, g_mix[l], w_in[l], b_gate[l], w_proj_sb[l], w_proj_dsa[l], w_out[l],
                   rel_bias, g_cross[l], g_mem[l], w_cq[l], w_ckv[l], w_co[l], g_ffn[l], w_up[l],
                   conv_w[l], conv_b[l], w_down[l], g_final if l == depth - 1 else None, batch, seq)
    return h.reshape(batch, seq, d)
```

```python
import functools

import jax
import jax.numpy as jnp
from jax import lax
from jax.experimental import pallas as pl
from jax.experimental.pallas import tpu as pltpu

F32, BF16, I32 = jnp.float32, jnp.bfloat16, jnp.int32

EPS = 1e-6
HEAD_DIM = 128
N_HEADS = 8
IDX_HEADS = 16
IDX_DIM = 64
CHUNK = 64
TOPK_MAX = 256
N_BUCKETS = 32
MAX_DISTANCE = 128
MEM_HEADS = 4
CONV_WIDTH = 3

LANES = 128
SUBLANES = 8
VMEM_LIMIT_BYTES = 62 * 1024 * 1024
NEG_BIG = -1e30
EXP2_UNDERFLOW = -151.0
LOG2E = 1.4426950408889634
KEY_BITS = 32
SIGN_BIT = -(2 ** 31)

PROJ_ROWS = 2048
ROW_TILE = 1024
NORM_ROWS = 1024
MERGE_ROWS = 512
FFN_COL_TILE = 512
FFN_K_TILE = 2048
ATT_BLOCK = KEY_BITS * SUBLANES
RADIX_GROUPS = 8

_NT = (((1,), (1,)), ((), ()))


def _params(*sem):
    return pltpu.CompilerParams(dimension_semantics=sem, vmem_limit_bytes=VMEM_LIMIT_BYTES)


def _rms(x, g):
    inv = lax.rsqrt(jnp.mean(x * x, axis=-1, keepdims=True) + EPS)
    return x * inv * g


def _rounding_riders(weights, n_steps, step_index):
    in_specs, out_specs, out_shape = [], [], []
    for w in weights:
        rows, cols = w.shape
        assert rows % n_steps == 0 and (rows // n_steps) % (2 * SUBLANES) == 0
        for specs in (in_specs, out_specs):
            specs.append(pl.BlockSpec((rows // n_steps, cols), lambda *ids: (step_index(*ids), 0)))
        out_shape.append(jax.ShapeDtypeStruct(w.shape, BF16))
    return in_specs, out_specs, out_shape


def _round_chunks(src_refs, dst_refs):
    for src, dst in zip(src_refs, dst_refs):
        dst[...] = src[...].astype(BF16)


def _mm_ws_body(a_ref, wt_ref, o_ref, wb_ref):
    @pl.when(pl.program_id(1) == 0)
    def _():
        wb_ref[...] = wt_ref[...].astype(BF16)

    o_ref[...] = lax.dot_general(a_ref[...], wb_ref[...], _NT,
                                 preferred_element_type=F32).astype(o_ref.dtype)


def _row_window(first_rows, tn, k):
    assert all(r % SUBLANES == 0 for r in first_rows)

    def index_map(j, i):
        row = jnp.int32(first_rows[0])
        for step, first in enumerate(first_rows[1:], start=1):
            row = jnp.where(j >= step, first, row)
        return pl.multiple_of(row, SUBLANES), 0

    return pl.BlockSpec((pl.Element(tn), pl.Element(k)), index_map)


def _matmul_ws(a, w_t, *, name, first_rows, out_dtype, tm, tn):
    m, k = a.shape
    tm = min(tm, m)
    assert m % tm == 0 and max(first_rows) + tn <= w_t.shape[0] and w_t.shape[1] == k
    return pl.pallas_call(
        _mm_ws_body,
        grid=(len(first_rows), m // tm),
        in_specs=[pl.BlockSpec((tm, k), lambda j, i: (i, 0)), _row_window(first_rows, tn, k)],
        out_specs=pl.BlockSpec((tm, tn), lambda j, i: (i, j)),
        out_shape=jax.ShapeDtypeStruct((m, len(first_rows) * tn), out_dtype),
        scratch_shapes=[pltpu.VMEM((tn, k), BF16)],
        compiler_params=_params("parallel", "arbitrary"),
        name=name,
    )(a, w_t)


def _mm_ws_t_body(wt_ref, a_ref, o_ref, wb_ref):
    @pl.when(pl.program_id(1) == 0)
    def _():
        wb_ref[...] = wt_ref[...].astype(BF16)

    res = lax.dot_general(wb_ref[...], a_ref[...], _NT, preferred_element_type=F32)
    n_blocks, _, tq = o_ref.shape
    for blk in range(n_blocks):
        o_ref[blk] = res[:, blk * tq:(blk + 1) * tq].astype(o_ref.dtype)


def _matmul_ws_t(a, w_t, *, name, first_rows, tn, tq, tm):
    m, k = a.shape
    tm = min(tm, m)
    assert m % tm == 0 and tm % tq == 0 and max(first_rows) + tn <= w_t.shape[0] and w_t.shape[1] == k
    per_step = tm // tq
    return pl.pallas_call(
        _mm_ws_t_body,
        grid=(len(first_rows), m // tm),
        in_specs=[_row_window(first_rows, tn, k), pl.BlockSpec((tm, k), lambda j, i: (i, 0))],
        out_specs=pl.BlockSpec((None, per_step, tn, tq), lambda j, i: (j, i, 0, 0)),
        out_shape=jax.ShapeDtypeStruct((len(first_rows), m // tq, tn, tq), BF16),
        scratch_shapes=[pltpu.VMEM((tn, k), BF16)],
        compiler_params=_params("parallel", "arbitrary"),
        name=name,
    )(w_t, a)


def _norm_proj_body(x_ref, g_ref, w_ref, *out_refs):
    h = _rms(x_ref[...], g_ref[...]).astype(BF16)
    p_ref = out_refs[-1]
    p_ref[...] = jnp.dot(h, w_ref[...].astype(BF16), preferred_element_type=F32).astype(p_ref.dtype)
    if len(out_refs) == 2:
        out_refs[0][...] = h


def _rmsnorm_proj(x, g, w, *, name, proj_dtype, keep_rows, tm=NORM_ROWS):
    m, d = x.shape
    n = w.shape[1]
    tm = min(tm, m)
    assert m % tm == 0
    out_specs = [pl.BlockSpec((tm, n), lambda i: (i, 0))]
    out_shape = [jax.ShapeDtypeStruct((m, n), proj_dtype)]
    if keep_rows:
        out_specs.insert(0, pl.BlockSpec((tm, d), lambda i: (i, 0)))
        out_shape.insert(0, jax.ShapeDtypeStruct((m, d), BF16))
    out = pl.pallas_call(
        _norm_proj_body,
        grid=(m // tm,),
        in_specs=[pl.BlockSpec((tm, d), lambda i: (i, 0)), pl.BlockSpec((1, d), lambda i: (0, 0)),
                  pl.BlockSpec((d, n), lambda i: (0, 0))],
        out_specs=out_specs,
        out_shape=out_shape,
        compiler_params=_params("parallel"),
        name=name,
    )(x, g.reshape(1, d).astype(F32), w)
    return out if keep_rows else out[0]


def _sb_body(q_ref, k_ref, vt_ref, tri_ref, *refs, tq, n_riders):
    o_ref, qs_ref, z_ref, lb_ref, wb_ref, acc_ref = refs[n_riders], *refs[2 * n_riders + 1:]
    _round_chunks(refs[:n_riders], refs[n_riders + 1:2 * n_riders + 1])
    i = pl.program_id(1)
    shape = (tq, tq)
    before = lax.broadcasted_iota(I32, shape, 0) < lax.broadcasted_iota(I32, shape, 1)
    heads = [slice(h * HEAD_DIM, (h + 1) * HEAD_DIM) for h in range(N_HEADS)]
    qs_ref[...] = (q_ref[...].astype(F32) * (HEAD_DIM ** -0.5 * LOG2E)).astype(BF16)
    acc_ref[...] = jnp.zeros(acc_ref.shape, F32)

    def logits(j, diagonal, slot):
        start = pl.multiple_of(j * tq, tq)
        for h, hs in enumerate(heads):
            z = lax.dot_general(k_ref[pl.ds(start, tq), hs], qs_ref[:, hs], _NT,
                                preferred_element_type=F32)
            neg_abs = lax.bitcast_convert_type(lax.bitcast_convert_type(z, I32) | SIGN_BIT, F32)
            log_lose = jnp.maximum(z, 0.0) + jnp.log2(1.0 + jnp.exp2(neg_abs))
            if diagonal:
                log_lose = jnp.where(before, log_lose, 0.0)
            z_ref[slot, h] = z
            lb_ref[slot, h] = log_lose.astype(BF16)

    def weights(carry, diagonal, slot):
        new_carry = []
        for h in range(N_HEADS):
            c = carry[h:h + 1, :] - jnp.dot(tri_ref[...], lb_ref[slot, h], preferred_element_type=F32)
            w = jnp.exp2(z_ref[slot, h] + c)
            if diagonal:
                w = jnp.where(before, w, 0.0)
            wb_ref[slot, h] = w.astype(BF16)
            new_carry.append(c[0:1, :])
        return jnp.concatenate(new_carry, axis=0)

    def values(j, slot):
        for h, hs in enumerate(heads):
            acc_ref[h] += jnp.dot(vt_ref[j, hs, :], wb_ref[slot, h], preferred_element_type=F32)

    def tile(j, carry, diagonal):
        logits(j, diagonal, 0)
        carry = weights(carry, diagonal, 0)
        values(j, 0)
        return carry

    no_carry = jnp.zeros((N_HEADS, tq), F32)

    def diagonal_and_previous():
        logits(i, True, 0)
        logits(i - 1, False, 1)
        carry = weights(weights(no_carry, True, 0), False, 1)
        values(i, 0)
        values(i - 1, 1)
        return carry

    carry = lax.cond(i >= 1, diagonal_and_previous, lambda: tile(i, no_carry, True))

    def cond(state):
        j, live, _ = state
        return jnp.logical_and(j >= 0, live > EXP2_UNDERFLOW)

    def body(state):
        j, _, carry = state
        carry = tile(j, carry, False)
        return j - 1, jnp.max(carry), carry

    lax.while_loop(cond, body, (i - 2, jnp.max(carry), carry))
    for h, hs in enumerate(heads):
        o_ref[:, hs] = acc_ref[h].T.astype(o_ref.dtype)


def _sb_attention(qk, v_t, batch, seq, *, q_col, k_col, v_branch, tq, round_weights=()):
    nq = seq // tq
    width = N_HEADS * HEAD_DIM
    tri = (jnp.arange(tq)[None, :] >= jnp.arange(tq)[:, None]).astype(BF16)
    resident = pl.Buffered(1)
    rider_in, rider_out, rider_shape = _rounding_riders(round_weights, batch * nq,
                                                        lambda b, i: b * nq + i)
    return pl.pallas_call(
        functools.partial(_sb_body, tq=tq, n_riders=len(round_weights)),
        grid=(batch, nq),
        in_specs=[
            pl.BlockSpec((tq, width), lambda b, i: (b * nq + i, q_col)),
            pl.BlockSpec((seq, width), lambda b, i: (b, k_col), pipeline_mode=resident),
            pl.BlockSpec((None, None, nq, width, tq), lambda b, i: (v_branch, b, 0, 0, 0),
                         pipeline_mode=resident),
            pl.BlockSpec((tq, tq), lambda b, i: (0, 0), pipeline_mode=resident),
        ] + rider_in,
        out_specs=[pl.BlockSpec((tq, width), lambda b, i: (b * nq + i, 0))] + rider_out,
        out_shape=[jax.ShapeDtypeStruct((batch * seq, width), BF16)] + rider_shape,
        scratch_shapes=[
            pltpu.VMEM((tq, width), BF16),
            pltpu.VMEM((2, N_HEADS, tq, tq), F32),
            pltpu.VMEM((2, N_HEADS, tq, tq), BF16),
            pltpu.VMEM((2, N_HEADS, tq, tq), BF16),
            pltpu.VMEM((N_HEADS, HEAD_DIM, tq), F32),
        ],
        compiler_params=_params("parallel", "arbitrary"),
        name="sb_attention",
    )(qk, qk, v_t, tri, *round_weights)


def _bucket_thresholds():
    nb = N_BUCKETS // 2
    max_exact = nb // 2
    span = nb - max_exact
    out = []
    for k in range(1, span):
        n = max_exact
        while n ** span * max_exact ** k < MAX_DISTANCE ** k * max_exact ** span:
            n += 1
        out.append(n)
    return max_exact, out


def _bias_body(rb_ref, o_ref, *, tq):
    nb = N_BUCKETS // 2
    max_exact, steps = _bucket_thresholds()
    shape = (2 * tq, tq)
    rel = lax.broadcasted_iota(I32, shape, 0) - lax.broadcasted_iota(I32, shape, 1) - tq
    n = jnp.abs(rel)
    large = jnp.full(shape, max_exact, I32)
    for t in steps:
        large = large + (n >= t).astype(I32)
    bucket = jnp.where(rel > 0, nb, 0) + jnp.where(n < max_exact, n, large)
    for h in range(N_HEADS):
        val = jnp.zeros(shape, F32)
        for b in range(N_BUCKETS):
            val = jnp.where(bucket == b, rb_ref[b, h], val)
        o_ref[h] = (val - rb_ref[nb - 1, h]) * LOG2E


def _near_bias(rel_bias, tq):
    return pl.pallas_call(
        functools.partial(_bias_body, tq=tq),
        in_specs=[pl.BlockSpec(memory_space=pltpu.SMEM)],
        out_specs=pl.BlockSpec(memory_space=pltpu.VMEM),
        out_shape=jax.ShapeDtypeStruct((N_HEADS, 2 * tq, tq), F32),
        compiler_params=pltpu.CompilerParams(vmem_limit_bytes=VMEM_LIMIT_BYTES),
        name="dsa_near_bias",
    )(rel_bias.astype(F32))


def _order_key(x):
    bits = lax.bitcast_convert_type(x, I32)
    return bits ^ ((bits >> 31) | SIGN_BIT)


def _order_key_to_float(key):
    return lax.bitcast_convert_type(key ^ ((~key >> 31) | SIGN_BIT), F32)


def _bit_transpose32(words):
    a = list(words)
    j, m = 16, 0x0000FFFF
    while j:
        mask = jnp.int32(m - (1 << 32) if m >= 1 << 31 else m)
        k = 0
        while k < 32:
            t = (lax.shift_right_logical(a[k], jnp.int32(j)) ^ a[k + j]) & mask
            a[k] = a[k] ^ lax.shift_left(t, jnp.int32(j))
            a[k + j] = a[k + j] ^ t
            k = (k + j + 1) & ~j
        j >>= 1
        m = (m ^ (m << j)) & 0xFFFFFFFF
    return a


def _dsa_body(qd_ref, qi_ref, wq_ref, kd_ref, vt_ref, ki_ref, bias_ref, tri_ref, o_ref,
              sc_ref, plane_ref, qs_ref, m_ref, l_ref, acc_ref, lg_ref, bmax_ref, *, tq, top):
    i = pl.program_id(1)
    shape = (tq, tq)
    key_row = lax.broadcasted_iota(I32, shape, 0)
    qry_col = lax.broadcasted_iota(I32, shape, 1)
    visible = key_row // CHUNK <= qry_col // CHUNK

    w_t = (wq_ref[...] * (IDX_DIM ** -0.5 * IDX_HEADS ** -0.5)).T

    heads_per_vreg = LANES // IDX_DIM

    def score_tile(j):
        start = pl.multiple_of(j * tq, tq)
        ki = [ki_ref[pl.ds(start, tq), c * LANES:(c + 1) * LANES].astype(BF16)
              for c in range(heads_per_vreg)]
        s = jnp.zeros(shape, F32)
        for h in range(IDX_HEADS):
            g, c = divmod(h, heads_per_vreg)
            d = lax.dot_general(ki[c], qi_ref[:, g * LANES:(g + 1) * LANES], _NT,
                                preferred_element_type=F32)
            s = s + w_t[h:h + 1, :] * jnp.maximum(d, 0.0)
        sc_ref[j] = s

    def key_planes(j):
        ukey = _order_key(sc_ref[j])
        planes = _bit_transpose32([ukey[g * SUBLANES:(g + 1) * SUBLANES, :] for g in range(KEY_BITS)])
        for b in range(KEY_BITS):
            plane_ref[b, pl.ds(pl.multiple_of(j * SUBLANES, SUBLANES), SUBLANES), :] = planes[b]

    @pl.when(i == 0)
    def _():
        plane_ref[...] = jnp.zeros(plane_ref.shape, I32)

    def score_steps(first, count):
        for j in range(count):
            key_planes(first + j)
            score_tile(first + j + 1)

    def score_quad(p, carry):
        score_steps(4 * p, 4)
        return carry

    score_tile(0)
    lax.fori_loop(0, i // 4, score_quad, 0)

    @pl.when(i % 4 >= 2)
    def _():
        score_steps(4 * (i // 4), 2)

    @pl.when(i % 2 == 1)
    def _():
        score_steps(i - 1, 1)

    key_planes(i)
    sc_ref[i] = jnp.where(visible, sc_ref[i], -jnp.inf)

    def popcount_rows(words):
        return jnp.sum(lax.population_count(words), axis=0, keepdims=True)

    def radix_select(n_rows):
        block_of_row = lax.broadcasted_iota(I32, (n_rows, tq), 0) // SUBLANES
        qry_of_col = lax.broadcasted_iota(I32, (n_rows, tq), 1)
        n_bits = (qry_of_col // CHUNK + 1) * (CHUNK // SUBLANES)
        diag_bits = jnp.where(n_bits >= KEY_BITS, -1, lax.shift_left(jnp.int32(1), n_bits) - 1)
        cand0 = jnp.where(block_of_row < i, -1, jnp.where(block_of_row == i, diag_bits, 0))

        def bit_step(t, state):
            cand, n_above, thr_bits = state
            b = KEY_BITS - 1 - t
            ones = cand & plane_ref[b, :n_rows]
            n_ones = popcount_rows(ones)
            take = n_above + n_ones >= top
            cand = jnp.where(take, ones, cand ^ ones)
            n_above = jnp.where(take, n_above, n_above + n_ones)
            thr_bits = thr_bits | jnp.where(take, lax.shift_left(jnp.int32(1), b), 0)
            return cand, n_above, thr_bits

        zero = jnp.zeros((1, tq), I32)
        cand, n_above, thr_bits = lax.fori_loop(0, KEY_BITS, bit_step, (cand0, zero, zero), unroll=4)
        return n_above, thr_bits, popcount_rows(cand)

    n_groups = min(RADIX_GROUPS, plane_ref.shape[1] // SUBLANES)

    def select_from(group):
        if group == n_groups - 1:
            return radix_select(plane_ref.shape[1])
        rows = (group + 1) * (plane_ref.shape[1] // n_groups)
        return lax.cond(i * SUBLANES < rows, functools.partial(radix_select, rows),
                        functools.partial(select_from, group + 1))

    n_above, thr_bits, n_equal = select_from(0)
    qry = lax.broadcasted_iota(I32, (1, tq), 1)
    n_visible = i * tq + (qry // CHUNK + 1) * CHUNK
    wanted = n_visible > top
    thr = jnp.where(wanted, _order_key_to_float(thr_bits), jnp.finfo(F32).min)
    tied = jnp.logical_and(wanted, n_above + n_equal > top)
    c_hi = n_above

    ones = jnp.ones((2 * SUBLANES, tq), BF16)

    def plain_mask(j, carry):
        sc_ref[j] = jnp.where(sc_ref[j] >= thr, 0.0, NEG_BIG)
        return carry

    def tie_mask(j, seen):
        s = sc_ref[j]
        equal = s == thr
        rank = jnp.dot(tri_ref[...], equal.astype(BF16), preferred_element_type=F32) + seen
        quota = jnp.where(tied, (top - c_hi).astype(F32), jnp.inf)
        keep_equal = jnp.where(rank < quota, 0.0, NEG_BIG)
        sc_ref[j] = jnp.where(s > thr, 0.0, jnp.where(equal, keep_equal, NEG_BIG))
        return seen + jnp.sum(equal.astype(F32), axis=0, keepdims=True)

    def with_ties():
        lax.fori_loop(0, i + 1, tie_mask, jnp.zeros((1, tq), F32))
        return jnp.int32(0)

    def without_ties():
        return lax.fori_loop(0, i + 1, plain_mask, jnp.int32(0))

    lax.cond(jnp.max(tied.astype(I32)) > 0, with_ties, without_ties)

    qs_ref[...] = (qd_ref[...].astype(F32) * (HEAD_DIM ** -0.5 * LOG2E)).astype(BF16)
    m_ref[...] = jnp.full(m_ref.shape, NEG_BIG, F32)
    l_ref[...] = jnp.zeros(l_ref.shape, F32)
    acc_ref[...] = jnp.zeros(acc_ref.shape, F32)

    heads = [slice(h * HEAD_DIM, (h + 1) * HEAD_DIM) for h in range(N_HEADS)]

    far, prev, diag = None, 0, 1

    def logits(u, near, slot):
        j = i - u
        start = pl.multiple_of(j * tq, tq)
        mask = sc_ref[j]
        block_max = []
        for h, hs in enumerate(heads):
            lg = lax.dot_general(kd_ref[pl.ds(start, tq), hs], qs_ref[:, hs], _NT,
                                 preferred_element_type=F32)
            if near is not None:
                lg = lg + bias_ref[h, near * tq:(near + 1) * tq, :]
            lg = lg + mask
            lg_ref[slot, h] = lg
            block_max.append(jnp.max(lg, axis=0, keepdims=True))
        bmax_ref[slot] = jnp.concatenate(block_max, axis=0)

    def values(u, slot):
        j = i - u
        m_old = m_ref[...]
        m_new = jnp.maximum(m_old, bmax_ref[slot])
        alpha = jnp.exp2(m_old - m_new)
        m_ref[...] = m_new
        denom = []
        for h, hs in enumerate(heads):
            p = jnp.exp2(lg_ref[slot, h] - m_new[h:h + 1, :]).astype(BF16)
            v_ext = jnp.concatenate([vt_ref[j, hs, :], ones], axis=0)
            pv = jnp.dot(v_ext, p, preferred_element_type=F32)
            acc_ref[h] = alpha[h:h + 1, :] * acc_ref[h] + pv[:HEAD_DIM]
            denom.append(pv[HEAD_DIM:HEAD_DIM + 1])
        l_ref[...] = alpha * l_ref[...] + jnp.concatenate(denom, axis=0)

    def even_step(u, near_a, near_b):
        logits(u - 1, near_a, 1)
        values(u, 0)
        logits(u - 2, near_b, 0)
        values(u - 1, 1)

    @pl.when(i == 0)
    def _():
        logits(0, diag, 0)
        values(0, 0)

    @pl.when(i == 1)
    def _():
        logits(1, prev, 1)

    @pl.when(jnp.logical_and(i >= 2, i % 2 == 1))
    def _():
        logits(i, far, 1)
        logits(i - 1, far, 0)
        values(i, 1)

    @pl.when(jnp.logical_and(i >= 2, i % 2 == 0))
    def _():
        logits(i, far, 0)

    n_far_pairs = i // 2 - 1

    def far_quad(k, carry):
        u = 2 * (i // 2 - 2 * k)
        even_step(u, far, far)
        even_step(u - 2, far, far)
        return carry

    lax.fori_loop(0, n_far_pairs // 2, far_quad, 0)

    @pl.when(jnp.logical_and(n_far_pairs > 0, n_far_pairs % 2 == 1))
    def _():
        even_step(4, far, far)

    @pl.when(i >= 2)
    def _():
        even_step(2, prev, diag)

    @pl.when(i == 1)
    def _():
        logits(0, diag, 0)
        values(1, 1)

    @pl.when(i >= 1)
    def _():
        values(0, 0)

    for h in range(N_HEADS):
        o = acc_ref[h] / l_ref[h:h + 1, :]
        o_ref[:, h * HEAD_DIM:(h + 1) * HEAD_DIM] = o.T.astype(o_ref.dtype)


def _dsa_attention(main, v_t, small, bias, batch, seq, *, qd_col, kd_col, qi_col, v_branch, tq):
    nq = seq // tq
    width = N_HEADS * HEAD_DIM
    assert IDX_HEADS * IDX_DIM == width and tq == KEY_BITS * SUBLANES and tq % CHUNK == 0
    assert nq % min(RADIX_GROUPS, nq) == 0
    key_copies = LANES // IDX_DIM
    top = min(TOPK_MAX, seq // 4)
    tri = (jnp.arange(tq)[None, :] < jnp.arange(tq)[:, None]).astype(BF16)
    resident = pl.Buffered(1)
    return pl.pallas_call(
        functools.partial(_dsa_body, tq=tq, top=top),
        grid=(batch, nq),
        in_specs=[
            pl.BlockSpec((tq, width), lambda b, i: (b * nq + i, qd_col)),
            pl.BlockSpec((tq, IDX_HEADS * IDX_DIM), lambda b, i: (b * nq + i, qi_col)),
            pl.BlockSpec((tq, LANES), lambda b, i: (b * nq + i, key_copies)),
            pl.BlockSpec((seq, width), lambda b, i: (b, kd_col), pipeline_mode=resident),
            pl.BlockSpec((None, None, nq, width, tq), lambda b, i: (v_branch, b, 0, 0, 0),
                         pipeline_mode=resident),
            pl.BlockSpec((seq, key_copies * LANES), lambda b, i: (b, 0), pipeline_mode=resident),
            pl.BlockSpec((N_HEADS, 2 * tq, tq), lambda b, i: (0, 0, 0), pipeline_mode=resident),
            pl.BlockSpec((tq, tq), lambda b, i: (0, 0), pipeline_mode=resident),
        ],
        out_specs=pl.BlockSpec((tq, width), lambda b, i: (b * nq + i, 0)),
        out_shape=jax.ShapeDtypeStruct((batch * seq, width), BF16),
        scratch_shapes=[
            pltpu.VMEM((nq, tq, tq), F32),
            pltpu.VMEM((KEY_BITS, nq * SUBLANES, tq), I32),
            pltpu.VMEM((tq, width), BF16),
            pltpu.VMEM((N_HEADS, tq), F32),
            pltpu.VMEM((N_HEADS, tq), F32),
            pltpu.VMEM((N_HEADS, HEAD_DIM, tq), F32),
            pltpu.VMEM((2, N_HEADS, tq, tq), F32),
            pltpu.VMEM((2, N_HEADS, tq), F32),
        ],
        compiler_params=_params("parallel", "arbitrary"),
        name="dsa_attention",
    )(main, main, small, main, v_t, small, bias, tri)


def _merge_cross_body(osb_ref, ods_ref, wsb_ref, wds_ref, gsb_ref, gds_ref, bsb_ref, bds_ref, wo_ref,
                      x_ref, gc_ref, wq_ref, km_ref, vm_ref, wco_ref, gn_ref, o_ref, hn_ref):
    p_sb = jnp.dot(osb_ref[...], wsb_ref[...], preferred_element_type=F32)
    p_ds = jnp.dot(ods_ref[...], wds_ref[...], preferred_element_type=F32)
    g_sb = jax.nn.sigmoid(gsb_ref[...].astype(F32) + bsb_ref[...])
    g_ds = jax.nn.sigmoid(gds_ref[...].astype(F32) + bds_ref[...])
    merged = (g_sb * p_sb + g_ds * p_ds).astype(BF16)
    x1 = x_ref[...] + jnp.dot(merged, wo_ref[...], preferred_element_type=F32)

    h = _rms(x1, gc_ref[...]).astype(BF16)
    q = jnp.dot(h, wq_ref[...], preferred_element_type=F32) * HEAD_DIM ** -0.5
    q = q.astype(BF16)
    outs = []
    for hh in range(MEM_HEADS):
        hs = slice(hh * HEAD_DIM, (hh + 1) * HEAD_DIM)
        lg = lax.dot_general(q[:, hs], km_ref[:, hs], _NT, preferred_element_type=F32)
        p = jnp.exp(lg - jnp.max(lg, axis=1, keepdims=True))
        o = jnp.dot(p.astype(BF16), vm_ref[:, hs], preferred_element_type=F32)
        outs.append((o / jnp.sum(p, axis=1, keepdims=True)).astype(BF16))
    o = jnp.concatenate(outs, axis=1)
    x2 = x1 + jnp.dot(o, wco_ref[...], preferred_element_type=F32)
    o_ref[...] = x2
    hn_ref[...] = _rms(x2, gn_ref[...]).astype(hn_ref.dtype)


def _merge_cross(o_sb, o_ds, w_sb, w_ds, w_out, proj, gate_offset, b_gate, x,
                 kv, g_cross, w_cq, w_co, g_next, batch, seq, *, tm):
    m, k = o_sb.shape
    d = w_sb.shape[1]
    n_mem = kv.shape[0] // batch
    width = MEM_HEADS * HEAD_DIM
    tm = min(tm, seq)
    assert gate_offset % d == 0 and seq % tm == 0
    g = gate_offset // d
    nt = seq // tm
    b_gate = b_gate.reshape(1, 2 * d).astype(F32)
    resident = pl.Buffered(1)

    def rows(width_, col=0):
        return pl.BlockSpec((tm, width_), lambda b, i: (b * nt + i, col))

    def whole(shape, *block):
        return pl.BlockSpec(shape, lambda b, i: block or (0,) * len(shape), pipeline_mode=resident)

    return pl.pallas_call(
        _merge_cross_body,
        grid=(batch, nt),
        in_specs=[
            rows(k), rows(k), whole((k, d)), whole((k, d)),
            rows(d, g), rows(d, g + 1), whole((1, d)), whole((1, d), 0, 1),
            whole((d, d)), rows(d),
            whole((1, d)), whole((d, width)),
            pl.BlockSpec((n_mem, width), lambda b, i: (b, 0)),
            pl.BlockSpec((n_mem, width), lambda b, i: (b, 1)),
            whole((width, d)), whole((1, d)),
        ],
        out_specs=[rows(d), rows(d)],
        out_shape=[jax.ShapeDtypeStruct((m, d), F32), jax.ShapeDtypeStruct((m, d), BF16)],
        compiler_params=_params("parallel", "parallel"),
        name="merge_out_cross",
    )(o_sb, o_ds, w_sb, w_ds, proj, proj, b_gate, b_gate, w_out, x,
      g_cross.reshape(1, d).astype(F32), w_cq, kv, kv, w_co, g_next.reshape(1, d).astype(F32))


def _delayed(u, tail, shift):
    rolled = pltpu.roll(u, shift, axis=0)
    row = lax.broadcasted_iota(I32, tail.shape, 0)
    head = jnp.where(row < shift, pltpu.roll(tail, shift, axis=0), rolled[:SUBLANES])
    return jnp.concatenate([head, rolled[SUBLANES:]], axis=0)


def _ffn_up_body(h_ref, wa_ref, wv_ref, cwa_ref, cwv_ref, cba_ref, cbv_ref, wd_ref, o_ref, wdb_ref,
                 wab_ref, wvb_ref, halo_ref, *, tiles_per_seq):
    i = pl.program_id(1)
    _round_chunks([wd_ref], [wdb_ref])

    @pl.when(i == 0)
    def _():
        wab_ref[...] = wa_ref[...].astype(BF16)
        wvb_ref[...] = wv_ref[...].astype(BF16)

    h = h_ref[...]
    tm = h.shape[0]
    sequence_start = i % tiles_per_seq == 0

    def conv(wb_ref, cw_ref, cb_ref, slot):
        u = jnp.dot(h, wb_ref[...], preferred_element_type=F32)
        tail = jnp.where(sequence_start, 0.0, halo_ref[slot])
        halo_ref[slot] = u[tm - SUBLANES:, :]
        c = cb_ref[...] + cw_ref[CONV_WIDTH - 1:CONV_WIDTH, :] * u
        for tap in range(CONV_WIDTH - 1):
            c = c + cw_ref[tap:tap + 1, :] * _delayed(u, tail, CONV_WIDTH - 1 - tap)
        return c

    a = conv(wab_ref, cwa_ref, cba_ref, 0)
    val = conv(wvb_ref, cwv_ref, cbv_ref, 1)
    o_ref[...] = (jax.nn.gelu(a) * val).astype(o_ref.dtype)


def _ffn_up_gate(h, w_up, conv_w, conv_b, w_down, seq, *, tm=ROW_TILE, tn=FFN_COL_TILE):
    m, d = h.shape
    two_ff = w_up.shape[1]
    d_ff = two_ff // 2
    tm, tn = min(tm, seq), min(tn, d_ff)
    assert seq % tm == 0 and d_ff % tn == 0 and tm >= SUBLANES >= CONV_WIDTH - 1
    nf, nt = d_ff // tn, m // tm
    conv_w = conv_w.astype(F32)
    conv_b = conv_b.reshape(1, two_ff).astype(F32)
    rider_in, rider_out, rider_shape = _rounding_riders([w_down], nf * nt, lambda j, i: j * nt + i)
    return pl.pallas_call(
        functools.partial(_ffn_up_body, tiles_per_seq=seq // tm),
        grid=(nf, nt),
        in_specs=[
            pl.BlockSpec((tm, d), lambda j, i: (i, 0)),
            pl.BlockSpec((d, tn), lambda j, i: (0, j)),
            pl.BlockSpec((d, tn), lambda j, i: (0, nf + j)),
            pl.BlockSpec((CONV_WIDTH, tn), lambda j, i: (0, j)),
            pl.BlockSpec((CONV_WIDTH, tn), lambda j, i: (0, nf + j)),
            pl.BlockSpec((1, tn), lambda j, i: (0, j)),
            pl.BlockSpec((1, tn), lambda j, i: (0, nf + j)),
        ] + rider_in,
        out_specs=[pl.BlockSpec((tm, tn), lambda j, i: (i, j))] + rider_out,
        out_shape=[jax.ShapeDtypeStruct((m, d_ff), BF16)] + rider_shape,
        scratch_shapes=[pltpu.VMEM((d, tn), BF16), pltpu.VMEM((d, tn), BF16),
                        pltpu.VMEM((2, SUBLANES, tn), F32)],
        compiler_params=_params("parallel", "arbitrary"),
        name="ffn_up_conv_gate",
    )(h, w_up, w_up, conv_w, conv_w, conv_b, conv_b, w_down)


def _ffn_down_body(a_ref, w_ref, x_ref, g_ref, o_ref, *, final_norm):
    k = pl.program_id(1)

    @pl.when(k == 0)
    def _():
        o_ref[...] = x_ref[...]

    o_ref[...] += jnp.dot(a_ref[...], w_ref[...], preferred_element_type=F32)

    if final_norm:
        @pl.when(k == pl.num_programs(1) - 1)
        def _():
            o_ref[...] = _rms(o_ref[...], g_ref[...])


def _ffn_down(a, w, x, g_final, *, tm=ROW_TILE, tk=FFN_K_TILE):
    m, kdim = a.shape
    d = w.shape[1]
    tm, tk = min(tm, m), min(tk, kdim)
    assert m % tm == 0 and kdim % tk == 0
    final_norm = g_final is not None
    g = (g_final if final_norm else jnp.ones((d,), F32)).reshape(1, d).astype(F32)
    return pl.pallas_call(
        functools.partial(_ffn_down_body, final_norm=final_norm),
        grid=(m // tm, kdim // tk),
        in_specs=[
            pl.BlockSpec((tm, tk), lambda i, k: (i, k)),
            pl.BlockSpec((tk, d), lambda i, k: (k, 0)),
            pl.BlockSpec((tm, d), lambda i, k: (i, 0)),
            pl.BlockSpec((1, d), lambda i, k: (0, 0)),
        ],
        out_specs=pl.BlockSpec((tm, d), lambda i, k: (i, 0)),
        out_shape=jax.ShapeDtypeStruct((m, d), F32),
        compiler_params=_params("parallel", "arbitrary"),
        name="ffn_down",
    )(a, w, x, g)


def _layer(x, mem, g_mix, w_in, b_gate, w_proj_sb, w_proj_dsa, w_out, rel_bias,
           g_cross, g_mem, w_cq, w_ckv, w_co, g_ffn, w_up, conv_w, conv_b, w_down, g_final, batch, seq):
    d = x.shape[1]
    width = N_HEADS * HEAD_DIM
    idx_w = IDX_HEADS * IDX_DIM
    o_qi = 6 * width
    o_ki = o_qi + idx_w
    o_wi = o_ki + IDX_DIM
    o_g = o_wi + IDX_HEADS

    zeros = jnp.zeros((d, LANES - IDX_DIM), F32)
    w_small = jnp.concatenate([
        w_in[:, o_ki:o_wi], zeros, zeros, w_in[:, o_ki:o_wi],
        jnp.pad(w_in[:, o_wi:o_g], ((0, 0), (0, LANES - IDX_HEADS)))], axis=1).astype(BF16)
    h, small = _rmsnorm_proj(x, g_mix, w_small, name="mixer_norm_index_proj", proj_dtype=F32,
                             keep_rows=True)
    w_in_t = w_in.T
    tq = ATT_BLOCK
    nq = seq // tq
    q_sb, k_sb, v_sb, q_ds, k_ds, v_ds, q_ix = (g * width for g in range(o_ki // width))
    gate_rows = tuple(o_g + g * width for g in range(2 * d // width))
    n_gate = len(gate_rows)
    main = _matmul_ws(h, w_in_t, name="in_proj_main",
                      first_rows=gate_rows + (q_sb, k_sb, q_ds, k_ds, q_ix),
                      out_dtype=BF16, tm=PROJ_ROWS, tn=width)
    v_t = _matmul_ws_t(h, w_in_t, name="in_proj_values", first_rows=(v_sb, v_ds), tn=width, tq=tq,
                       tm=PROJ_ROWS)
    v_t = v_t.reshape(2, batch, nq, width, tq)

    o_sb, w_proj_sb, w_proj_dsa, w_out, w_cq, w_co = _sb_attention(
        main, v_t, batch, seq, q_col=n_gate, k_col=n_gate + 1, v_branch=0, tq=tq,
        round_weights=(w_proj_sb, w_proj_dsa, w_out, w_cq, w_co))
    bias = _near_bias(rel_bias, tq)
    o_ds = _dsa_attention(main, v_t, small, bias, batch, seq, qd_col=n_gate + 2, kd_col=n_gate + 3,
                          qi_col=n_gate + 4, v_branch=1, tq=tq)

    kv = _rmsnorm_proj(mem, g_mem, w_ckv, name="mem_norm_kv_proj", proj_dtype=BF16, keep_rows=False)
    x, h_ffn = _merge_cross(o_sb, o_ds, w_proj_sb, w_proj_dsa, w_out, main, 0, b_gate, x, kv, g_cross,
                            w_cq, w_co, g_ffn, batch, seq, tm=MERGE_ROWS)

    act, w_down = _ffn_up_gate(h_ffn, w_up, conv_w, conv_b, w_down, seq)
    return _ffn_down(act, w_down, x, g_final)


def kernel(x, mem, g_mix, w_in, b_gate, w_proj_sb, w_proj_dsa, w_out, rel_bias, g_cross, g_mem,
           w_cq, w_ckv, w_co, g_ffn, w_up, conv_w, conv_b, w_down, g_final):
    batch, seq, d = x.shape
    h = x.reshape(batch * seq, d)
    mem2 = mem.reshape(batch * mem.shape[1], d)
    depth = g_mix.shape[0]
    for l in range(depth):
        h = _layer(h, mem2, g_mix[l], w_in[l], b_gate[l], w_proj_sb[l], w_proj_dsa[l], w_out[l],
                   rel_bias, g_cross[l], g_mem[l], w_cq[l], w_ckv[l], w_co[l], g_ffn[l], w_up[l],
                   conv_w[l], conv_b[l], w_down[l], g_final if l == depth - 1 else None, batch, seq)
    return h.reshape(batch, seq, d)
```

```python
import functools

import jax
import jax.numpy as jnp
from jax import lax
from jax.experimental import pallas as pl
from jax.experimental.pallas import tpu as pltpu

F32, BF16, I32 = jnp.float32, jnp.bfloat16, jnp.int32

EPS = 1e-6
HEAD_DIM = 128
N_HEADS = 8
IDX_HEADS = 16
IDX_DIM = 64
CHUNK = 64
TOPK_MAX = 256
N_BUCKETS = 32
MAX_DISTANCE = 128
MEM_HEADS = 4
CONV_WIDTH = 3

LANES = 128
SUBLANES = 8
VMEM_LIMIT_BYTES = 62 * 1024 * 1024
NEG_BIG = -1e30
EXP2_UNDERFLOW = -151.0
LOG2E = 1.4426950408889634
KEY_BITS = 32
SIGN_BIT = -(2 ** 31)

PROJ_ROWS = 2048
ROW_TILE = 1024
FFN_DOWN_ROWS = 256
NORM_ROWS = 1024
MERGE_ROWS = 512
FFN_COL_TILE = 512
FFN_K_TILE = 1024
ATT_BLOCK = KEY_BITS * SUBLANES
RADIX_GROUPS = 8

_NT = (((1,), (1,)), ((), ()))


def _params(*sem):
    return pltpu.CompilerParams(dimension_semantics=sem, vmem_limit_bytes=VMEM_LIMIT_BYTES)


def _rms(x, g):
    inv = lax.rsqrt(jnp.mean(x * x, axis=-1, keepdims=True) + EPS)
    return x * inv * g


def _rounding_riders(weights, n_steps, step_index):
    in_specs, out_specs, out_shape = [], [], []
    for w in weights:
        rows, cols = w.shape
        assert rows % n_steps == 0 and (rows // n_steps) % (2 * SUBLANES) == 0
        for specs in (in_specs, out_specs):
            specs.append(pl.BlockSpec((rows // n_steps, cols), lambda *ids: (step_index(*ids), 0)))
        out_shape.append(jax.ShapeDtypeStruct(w.shape, BF16))
    return in_specs, out_specs, out_shape


def _round_chunks(src_refs, dst_refs):
    for src, dst in zip(src_refs, dst_refs):
        dst[...] = src[...].astype(BF16)


def _mm_ws_body(a_ref, wt_ref, o_ref, wb_ref):
    @pl.when(pl.program_id(1) == 0)
    def _():
        wb_ref[...] = wt_ref[...].astype(BF16)

    o_ref[...] = lax.dot_general(a_ref[...], wb_ref[...], _NT,
                                 preferred_element_type=F32).astype(o_ref.dtype)


def _row_window(first_rows, tn, k):
    assert all(r % SUBLANES == 0 for r in first_rows)

    def index_map(j, i):
        row = jnp.int32(first_rows[0])
        for step, first in enumerate(first_rows[1:], start=1):
            row = jnp.where(j >= step, first, row)
        return pl.multiple_of(row, SUBLANES), 0

    return pl.BlockSpec((pl.Element(tn), pl.Element(k)), index_map)


def _matmul_ws(a, w_t, *, name, first_rows, out_dtype, tm, tn):
    m, k = a.shape
    tm = min(tm, m)
    assert m % tm == 0 and max(first_rows) + tn <= w_t.shape[0] and w_t.shape[1] == k
    return pl.pallas_call(
        _mm_ws_body,
        grid=(len(first_rows), m // tm),
        in_specs=[pl.BlockSpec((tm, k), lambda j, i: (i, 0)), _row_window(first_rows, tn, k)],
        out_specs=pl.BlockSpec((tm, tn), lambda j, i: (i, j)),
        out_shape=jax.ShapeDtypeStruct((m, len(first_rows) * tn), out_dtype),
        scratch_shapes=[pltpu.VMEM((tn, k), BF16)],
        compiler_params=_params("parallel", "arbitrary"),
        name=name,
    )(a, w_t)


def _mm_ws_t_body(wt_ref, a_ref, o_ref, wb_ref):
    @pl.when(pl.program_id(1) == 0)
    def _():
        wb_ref[...] = wt_ref[...].astype(BF16)

    res = lax.dot_general(wb_ref[...], a_ref[...], _NT, preferred_element_type=F32)
    n_blocks, _, tq = o_ref.shape
    for blk in range(n_blocks):
        o_ref[blk] = res[:, blk * tq:(blk + 1) * tq].astype(o_ref.dtype)


def _matmul_ws_t(a, w_t, *, name, first_rows, tn, tq, tm):
    m, k = a.shape
    tm = min(tm, m)
    assert m % tm == 0 and tm % tq == 0 and max(first_rows) + tn <= w_t.shape[0] and w_t.shape[1] == k
    per_step = tm // tq
    return pl.pallas_call(
        _mm_ws_t_body,
        grid=(len(first_rows), m // tm),
        in_specs=[_row_window(first_rows, tn, k), pl.BlockSpec((tm, k), lambda j, i: (i, 0))],
        out_specs=pl.BlockSpec((None, per_step, tn, tq), lambda j, i: (j, i, 0, 0)),
        out_shape=jax.ShapeDtypeStruct((len(first_rows), m // tq, tn, tq), BF16),
        scratch_shapes=[pltpu.VMEM((tn, k), BF16)],
        compiler_params=_params("parallel", "arbitrary"),
        name=name,
    )(w_t, a)


def _norm_proj_body(x_ref, g_ref, w_ref, *out_refs):
    h = _rms(x_ref[...], g_ref[...]).astype(BF16)
    p_ref = out_refs[-1]
    p_ref[...] = jnp.dot(h, w_ref[...].astype(BF16), preferred_element_type=F32).astype(p_ref.dtype)
    if len(out_refs) == 2:
        out_refs[0][...] = h


def _rmsnorm_proj(x, g, w, *, name, proj_dtype, keep_rows, tm=NORM_ROWS):
    m, d = x.shape
    n = w.shape[1]
    tm = min(tm, m)
    assert m % tm == 0
    out_specs = [pl.BlockSpec((tm, n), lambda i: (i, 0))]
    out_shape = [jax.ShapeDtypeStruct((m, n), proj_dtype)]
    if keep_rows:
        out_specs.insert(0, pl.BlockSpec((tm, d), lambda i: (i, 0)))
        out_shape.insert(0, jax.ShapeDtypeStruct((m, d), BF16))
    out = pl.pallas_call(
        _norm_proj_body,
        grid=(m // tm,),
        in_specs=[pl.BlockSpec((tm, d), lambda i: (i, 0)), pl.BlockSpec((1, d), lambda i: (0, 0)),
                  pl.BlockSpec((d, n), lambda i: (0, 0))],
        out_specs=out_specs,
        out_shape=out_shape,
        compiler_params=_params("parallel"),
        name=name,
    )(x, g.reshape(1, d).astype(F32), w)
    return out if keep_rows else out[0]


def _sb_body(q_ref, k_ref, vt_ref, tri_ref, *refs, tq, n_riders):
    o_ref, qs_ref, z_ref, lb_ref, wb_ref, acc_ref = refs[n_riders], *refs[2 * n_riders + 1:]
    _round_chunks(refs[:n_riders], refs[n_riders + 1:2 * n_riders + 1])
    i = pl.program_id(1)
    shape = (tq, tq)
    before = lax.broadcasted_iota(I32, shape, 0) < lax.broadcasted_iota(I32, shape, 1)
    heads = [slice(h * HEAD_DIM, (h + 1) * HEAD_DIM) for h in range(N_HEADS)]
    qs_ref[...] = (q_ref[...].astype(F32) * (HEAD_DIM ** -0.5 * LOG2E)).astype(BF16)
    acc_ref[...] = jnp.zeros(acc_ref.shape, F32)

    def logits(j, diagonal, slot):
        start = pl.multiple_of(j * tq, tq)
        for h, hs in enumerate(heads):
            z = lax.dot_general(k_ref[pl.ds(start, tq), hs], qs_ref[:, hs], _NT,
                                preferred_element_type=F32)
            neg_abs = lax.bitcast_convert_type(lax.bitcast_convert_type(z, I32) | SIGN_BIT, F32)
            log_lose = jnp.maximum(z, 0.0) + jnp.log2(1.0 + jnp.exp2(neg_abs))
            if diagonal:
                log_lose = jnp.where(before, log_lose, 0.0)
            z_ref[slot, h] = z
            lb_ref[slot, h] = log_lose.astype(BF16)

    def weights(carry, diagonal, slot):
        new_carry = []
        for h in range(N_HEADS):
            c = carry[h:h + 1, :] - jnp.dot(tri_ref[...], lb_ref[slot, h], preferred_element_type=F32)
            w = jnp.exp2(z_ref[slot, h] + c)
            if diagonal:
                w = jnp.where(before, w, 0.0)
            wb_ref[slot, h] = w.astype(BF16)
            new_carry.append(c[0:1, :])
        return jnp.concatenate(new_carry, axis=0)

    def values(j, slot):
        for h, hs in enumerate(heads):
            acc_ref[h] += jnp.dot(vt_ref[j, hs, :], wb_ref[slot, h], preferred_element_type=F32)

    def tile(j, carry, diagonal):
        logits(j, diagonal, 0)
        carry = weights(carry, diagonal, 0)
        values(j, 0)
        return carry

    no_carry = jnp.zeros((N_HEADS, tq), F32)

    def diagonal_and_previous():
        logits(i, True, 0)
        logits(i - 1, False, 1)
        carry = weights(weights(no_carry, True, 0), False, 1)
        values(i, 0)
        values(i - 1, 1)
        return carry

    carry = lax.cond(i >= 1, diagonal_and_previous, lambda: tile(i, no_carry, True))

    def cond(state):
        j, live, _ = state
        return jnp.logical_and(j >= 0, live > EXP2_UNDERFLOW)

    def body(state):
        j, _, carry = state
        carry = tile(j, carry, False)
        return j - 1, jnp.max(carry), carry

    lax.while_loop(cond, body, (i - 2, jnp.max(carry), carry))
    for h, hs in enumerate(heads):
        o_ref[:, hs] = acc_ref[h].T.astype(o_ref.dtype)


def _sb_attention(qk, v_t, batch, seq, *, q_col, k_col, v_branch, tq, round_weights=()):
    nq = seq // tq
    width = N_HEADS * HEAD_DIM
    tri = (jnp.arange(tq)[None, :] >= jnp.arange(tq)[:, None]).astype(BF16)
    resident = pl.Buffered(1)
    rider_in, rider_out, rider_shape = _rounding_riders(round_weights, batch * nq,
                                                        lambda b, i: b * nq + i)
    return pl.pallas_call(
        functools.partial(_sb_body, tq=tq, n_riders=len(round_weights)),
        grid=(batch, nq),
        in_specs=[
            pl.BlockSpec((tq, width), lambda b, i: (b * nq + i, q_col)),
            pl.BlockSpec((seq, width), lambda b, i: (b, k_col), pipeline_mode=resident),
            pl.BlockSpec((None, None, nq, width, tq), lambda b, i: (v_branch, b, 0, 0, 0),
                         pipeline_mode=resident),
            pl.BlockSpec((tq, tq), lambda b, i: (0, 0), pipeline_mode=resident),
        ] + rider_in,
        out_specs=[pl.BlockSpec((tq, width), lambda b, i: (b * nq + i, 0))] + rider_out,
        out_shape=[jax.ShapeDtypeStruct((batch * seq, width), BF16)] + rider_shape,
        scratch_shapes=[
            pltpu.VMEM((tq, width), BF16),
            pltpu.VMEM((2, N_HEADS, tq, tq), F32),
            pltpu.VMEM((2, N_HEADS, tq, tq), BF16),
            pltpu.VMEM((2, N_HEADS, tq, tq), BF16),
            pltpu.VMEM((N_HEADS, HEAD_DIM, tq), F32),
        ],
        compiler_params=_params("parallel", "arbitrary"),
        name="sb_attention",
    )(qk, qk, v_t, tri, *round_weights)


def _bucket_thresholds():
    nb = N_BUCKETS // 2
    max_exact = nb // 2
    span = nb - max_exact
    out = []
    for k in range(1, span):
        n = max_exact
        while n ** span * max_exact ** k < MAX_DISTANCE ** k * max_exact ** span:
            n += 1
        out.append(n)
    return max_exact, out


def _bias_body(rb_ref, o_ref, *, tq):
    nb = N_BUCKETS // 2
    max_exact, steps = _bucket_thresholds()
    shape = (2 * tq, tq)
    rel = lax.broadcasted_iota(I32, shape, 0) - lax.broadcasted_iota(I32, shape, 1) - tq
    n = jnp.abs(rel)
    large = jnp.full(shape, max_exact, I32)
    for t in steps:
        large = large + (n >= t).astype(I32)
    bucket = jnp.where(rel > 0, nb, 0) + jnp.where(n < max_exact, n, large)
    for h in range(N_HEADS):
        val = jnp.zeros(shape, F32)
        for b in range(N_BUCKETS):
            val = jnp.where(bucket == b, rb_ref[b, h], val)
        o_ref[h] = (val - rb_ref[nb - 1, h]) * LOG2E


def _near_bias(rel_bias, tq):
    return pl.pallas_call(
        functools.partial(_bias_body, tq=tq),
        in_specs=[pl.BlockSpec(memory_space=pltpu.SMEM)],
        out_specs=pl.BlockSpec(memory_space=pltpu.VMEM),
        out_shape=jax.ShapeDtypeStruct((N_HEADS, 2 * tq, tq), F32),
        compiler_params=pltpu.CompilerParams(vmem_limit_bytes=VMEM_LIMIT_BYTES),
        name="dsa_near_bias",
    )(rel_bias.astype(F32))


def _order_key(x):
    bits = lax.bitcast_convert_type(x, I32)
    return bits ^ ((bits >> 31) | SIGN_BIT)


def _order_key_to_float(key):
    return lax.bitcast_convert_type(key ^ ((~key >> 31) | SIGN_BIT), F32)


def _bit_transpose32(words):
    a = list(words)
    j, m = 16, 0x0000FFFF
    while j:
        mask = jnp.int32(m - (1 << 32) if m >= 1 << 31 else m)
        k = 0
        while k < 32:
            t = (lax.shift_right_logical(a[k], jnp.int32(j)) ^ a[k + j]) & mask
            a[k] = a[k] ^ lax.shift_left(t, jnp.int32(j))
            a[k + j] = a[k + j] ^ t
            k = (k + j + 1) & ~j
        j >>= 1
        m = (m ^ (m << j)) & 0xFFFFFFFF
    return a


def _dsa_body(qd_ref, qi_ref, wq_ref, kd_ref, vt_ref, ki_ref, bias_ref, tri_ref, o_ref,
              sc_ref, plane_ref, qs_ref, m_ref, l_ref, acc_ref, lg_ref, bmax_ref, *, tq, top):
    i = pl.program_id(1)
    shape = (tq, tq)
    key_row = lax.broadcasted_iota(I32, shape, 0)
    qry_col = lax.broadcasted_iota(I32, shape, 1)
    visible = key_row // CHUNK <= qry_col // CHUNK

    w_t = (wq_ref[...] * (IDX_DIM ** -0.5 * IDX_HEADS ** -0.5)).T

    heads_per_vreg = LANES // IDX_DIM

    def score_tile(j):
        start = pl.multiple_of(j * tq, tq)
        ki = [ki_ref[pl.ds(start, tq), c * LANES:(c + 1) * LANES].astype(BF16)
              for c in range(heads_per_vreg)]
        s = jnp.zeros(shape, F32)
        for h in range(IDX_HEADS):
            g, c = divmod(h, heads_per_vreg)
            d = lax.dot_general(ki[c], qi_ref[:, g * LANES:(g + 1) * LANES], _NT,
                                preferred_element_type=F32)
            s = s + w_t[h:h + 1, :] * jnp.maximum(d, 0.0)
        sc_ref[j] = s

    def key_planes(j):
        ukey = _order_key(sc_ref[j])
        planes = _bit_transpose32([ukey[g * SUBLANES:(g + 1) * SUBLANES, :] for g in range(KEY_BITS)])
        for b in range(KEY_BITS):
            plane_ref[b, pl.ds(pl.multiple_of(j * SUBLANES, SUBLANES), SUBLANES), :] = planes[b]

    @pl.when(i == 0)
    def _():
        plane_ref[...] = jnp.zeros(plane_ref.shape, I32)

    def score_steps(first, count):
        for j in range(count):
            key_planes(first + j)
            score_tile(first + j + 1)

    def score_quad(p, carry):
        score_steps(4 * p, 4)
        return carry

    score_tile(0)
    lax.fori_loop(0, i // 4, score_quad, 0)

    @pl.when(i % 4 >= 2)
    def _():
        score_steps(4 * (i // 4), 2)

    @pl.when(i % 2 == 1)
    def _():
        score_steps(i - 1, 1)

    key_planes(i)
    sc_ref[i] = jnp.where(visible, sc_ref[i], -jnp.inf)

    def popcount_rows(words):
        return jnp.sum(lax.population_count(words), axis=0, keepdims=True)

    def radix_select(n_rows):
        block_of_row = lax.broadcasted_iota(I32, (n_rows, tq), 0) // SUBLANES
        qry_of_col = lax.broadcasted_iota(I32, (n_rows, tq), 1)
        n_bits = (qry_of_col // CHUNK + 1) * (CHUNK // SUBLANES)
        diag_bits = jnp.where(n_bits >= KEY_BITS, -1, lax.shift_left(jnp.int32(1), n_bits) - 1)
        cand0 = jnp.where(block_of_row < i, -1, jnp.where(block_of_row == i, diag_bits, 0))

        def bit_step(t, state):
            cand, n_above, thr_bits = state
            b = KEY_BITS - 1 - t
            ones = cand & plane_ref[b, :n_rows]
            n_ones = popcount_rows(ones)
            take = n_above + n_ones >= top
            cand = jnp.where(take, ones, cand ^ ones)
            n_above = jnp.where(take, n_above, n_above + n_ones)
            thr_bits = thr_bits | jnp.where(take, lax.shift_left(jnp.int32(1), b), 0)
            return cand, n_above, thr_bits

        zero = jnp.zeros((1, tq), I32)
        cand, n_above, thr_bits = lax.fori_loop(0, KEY_BITS, bit_step, (cand0, zero, zero), unroll=4)
        return n_above, thr_bits, popcount_rows(cand)

    n_groups = min(RADIX_GROUPS, plane_ref.shape[1] // SUBLANES)

    def select_from(group):
        if group == n_groups - 1:
            return radix_select(plane_ref.shape[1])
        rows = (group + 1) * (plane_ref.shape[1] // n_groups)
        return lax.cond(i * SUBLANES < rows, functools.partial(radix_select, rows),
                        functools.partial(select_from, group + 1))

    n_above, thr_bits, n_equal = select_from(0)
    qry = lax.broadcasted_iota(I32, (1, tq), 1)
    n_visible = i * tq + (qry // CHUNK + 1) * CHUNK
    wanted = n_visible > top
    thr = jnp.where(wanted, _order_key_to_float(thr_bits), jnp.finfo(F32).min)
    tied = jnp.logical_and(wanted, n_above + n_equal > top)
    c_hi = n_above

    ones = jnp.ones((2 * SUBLANES, tq), BF16)

    def plain_mask(j, carry):
        sc_ref[j] = jnp.where(sc_ref[j] >= thr, 0.0, NEG_BIG)
        return carry

    def tie_mask(j, seen):
        s = sc_ref[j]
        equal = s == thr
        rank = jnp.dot(tri_ref[...], equal.astype(BF16), preferred_element_type=F32) + seen
        quota = jnp.where(tied, (top - c_hi).astype(F32), jnp.inf)
        keep_equal = jnp.where(rank < quota, 0.0, NEG_BIG)
        sc_ref[j] = jnp.where(s > thr, 0.0, jnp.where(equal, keep_equal, NEG_BIG))
        return seen + jnp.sum(equal.astype(F32), axis=0, keepdims=True)

    def with_ties():
        lax.fori_loop(0, i + 1, tie_mask, jnp.zeros((1, tq), F32))
        return jnp.int32(0)

    def without_ties():
        return lax.fori_loop(0, i + 1, plain_mask, jnp.int32(0))

    lax.cond(jnp.max(tied.astype(I32)) > 0, with_ties, without_ties)

    qs_ref[...] = (qd_ref[...].astype(F32) * (HEAD_DIM ** -0.5 * LOG2E)).astype(BF16)
    m_ref[...] = jnp.full(m_ref.shape, NEG_BIG, F32)
    l_ref[...] = jnp.zeros(l_ref.shape, F32)
    acc_ref[...] = jnp.zeros(acc_ref.shape, F32)

    heads = [slice(h * HEAD_DIM, (h + 1) * HEAD_DIM) for h in range(N_HEADS)]

    far, prev, diag = None, 0, 1

    def logits(u, near, slot):
        j = i - u
        start = pl.multiple_of(j * tq, tq)
        mask = sc_ref[j]
        block_max = []
        for h, hs in enumerate(heads):
            lg = lax.dot_general(kd_ref[pl.ds(start, tq), hs], qs_ref[:, hs], _NT,
                                 preferred_element_type=F32)
            if near is not None:
                lg = lg + bias_ref[h, near * tq:(near + 1) * tq, :]
            lg = lg + mask
            lg_ref[slot, h] = lg
            block_max.append(jnp.max(lg, axis=0, keepdims=True))
        bmax_ref[slot] = jnp.concatenate(block_max, axis=0)

    def values(u, slot):
        j = i - u
        m_old = m_ref[...]
        m_new = jnp.maximum(m_old, bmax_ref[slot])
        alpha = jnp.exp2(m_old - m_new)
        m_ref[...] = m_new
        denom = []
        for h, hs in enumerate(heads):
            p = jnp.exp2(lg_ref[slot, h] - m_new[h:h + 1, :]).astype(BF16)
            v_ext = jnp.concatenate([vt_ref[j, hs, :], ones], axis=0)
            pv = jnp.dot(v_ext, p, preferred_element_type=F32)
            acc_ref[h] = alpha[h:h + 1, :] * acc_ref[h] + pv[:HEAD_DIM]
            denom.append(pv[HEAD_DIM:HEAD_DIM + 1])
        l_ref[...] = alpha * l_ref[...] + jnp.concatenate(denom, axis=0)

    def even_step(u, near_a, near_b):
        logits(u - 1, near_a, 1)
        values(u, 0)
        logits(u - 2, near_b, 0)
        values(u - 1, 1)

    @pl.when(i == 0)
    def _():
        logits(0, diag, 0)
        values(0, 0)

    @pl.when(i == 1)
    def _():
        logits(1, prev, 1)

    @pl.when(jnp.logical_and(i >= 2, i % 2 == 1))
    def _():
        logits(i, far, 1)
        logits(i - 1, far, 0)
        values(i, 1)

    @pl.when(jnp.logical_and(i >= 2, i % 2 == 0))
    def _():
        logits(i, far, 0)

    n_far_pairs = i // 2 - 1

    def far_quad(k, carry):
        u = 2 * (i // 2 - 2 * k)
        even_step(u, far, far)
        even_step(u - 2, far, far)
        return carry

    lax.fori_loop(0, n_far_pairs // 2, far_quad, 0)

    @pl.when(jnp.logical_and(n_far_pairs > 0, n_far_pairs % 2 == 1))
    def _():
        even_step(4, far, far)

    @pl.when(i >= 2)
    def _():
        even_step(2, prev, diag)

    @pl.when(i == 1)
    def _():
        logits(0, diag, 0)
        values(1, 1)

    @pl.when(i >= 1)
    def _():
        values(0, 0)

    for h in range(N_HEADS):
        o = acc_ref[h] / l_ref[h:h + 1, :]
        o_ref[:, h * HEAD_DIM:(h + 1) * HEAD_DIM] = o.T.astype(o_ref.dtype)


def _dsa_attention(main, v_t, small, bias, batch, seq, *, qd_col, kd_col, qi_col, v_branch, tq):
    nq = seq // tq
    width = N_HEADS * HEAD_DIM
    assert IDX_HEADS * IDX_DIM == width and tq == KEY_BITS * SUBLANES and tq % CHUNK == 0
    assert nq % min(RADIX_GROUPS, nq) == 0
    key_copies = LANES // IDX_DIM
    top = min(TOPK_MAX, seq // 4)
    tri = (jnp.arange(tq)[None, :] < jnp.arange(tq)[:, None]).astype(BF16)
    resident = pl.Buffered(1)
    return pl.pallas_call(
        functools.partial(_dsa_body, tq=tq, top=top),
        grid=(batch, nq),
        in_specs=[
            pl.BlockSpec((tq, width), lambda b, i: (b * nq + i, qd_col)),
            pl.BlockSpec((tq, IDX_HEADS * IDX_DIM), lambda b, i: (b * nq + i, qi_col)),
            pl.BlockSpec((tq, LANES), lambda b, i: (b * nq + i, key_copies)),
            pl.BlockSpec((seq, width), lambda b, i: (b, kd_col), pipeline_mode=resident),
            pl.BlockSpec((None, None, nq, width, tq), lambda b, i: (v_branch, b, 0, 0, 0),
                         pipeline_mode=resident),
            pl.BlockSpec((seq, key_copies * LANES), lambda b, i: (b, 0), pipeline_mode=resident),
            pl.BlockSpec((N_HEADS, 2 * tq, tq), lambda b, i: (0, 0, 0), pipeline_mode=resident),
            pl.BlockSpec((tq, tq), lambda b, i: (0, 0), pipeline_mode=resident),
        ],
        out_specs=pl.BlockSpec((tq, width), lambda b, i: (b * nq + i, 0)),
        out_shape=jax.ShapeDtypeStruct((batch * seq, width), BF16),
        scratch_shapes=[
            pltpu.VMEM((nq, tq, tq), F32),
            pltpu.VMEM((KEY_BITS, nq * SUBLANES, tq), I32),
            pltpu.VMEM((tq, width), BF16),
            pltpu.VMEM((N_HEADS, tq), F32),
            pltpu.VMEM((N_HEADS, tq), F32),
            pltpu.VMEM((N_HEADS, HEAD_DIM, tq), F32),
            pltpu.VMEM((2, N_HEADS, tq, tq), F32),
            pltpu.VMEM((2, N_HEADS, tq), F32),
        ],
        compiler_params=_params("parallel", "arbitrary"),
        name="dsa_attention",
    )(main, main, small, main, v_t, small, bias, tri)


def _merge_cross_body(osb_ref, ods_ref, wsb_ref, wds_ref, gsb_ref, gds_ref, bsb_ref, bds_ref, wo_ref,
                      x_ref, gc_ref, wq_ref, km_ref, vm_ref, wco_ref, gn_ref, o_ref, hn_ref):
    p_sb = jnp.dot(osb_ref[...], wsb_ref[...], preferred_element_type=F32)
    p_ds = jnp.dot(ods_ref[...], wds_ref[...], preferred_element_type=F32)
    g_sb = jax.nn.sigmoid(gsb_ref[...].astype(F32) + bsb_ref[...])
    g_ds = jax.nn.sigmoid(gds_ref[...].astype(F32) + bds_ref[...])
    merged = (g_sb * p_sb + g_ds * p_ds).astype(BF16)
    x1 = x_ref[...] + jnp.dot(merged, wo_ref[...], preferred_element_type=F32)

    h = _rms(x1, gc_ref[...]).astype(BF16)
    q = jnp.dot(h, wq_ref[...], preferred_element_type=F32) * HEAD_DIM ** -0.5
    q = q.astype(BF16)
    outs = []
    for hh in range(MEM_HEADS):
        hs = slice(hh * HEAD_DIM, (hh + 1) * HEAD_DIM)
        lg = lax.dot_general(q[:, hs], km_ref[:, hs], _NT, preferred_element_type=F32)
        p = jnp.exp(lg - jnp.max(lg, axis=1, keepdims=True))
        o = jnp.dot(p.astype(BF16), vm_ref[:, hs], preferred_element_type=F32)
        outs.append((o / jnp.sum(p, axis=1, keepdims=True)).astype(BF16))
    o = jnp.concatenate(outs, axis=1)
    x2 = x1 + jnp.dot(o, wco_ref[...], preferred_element_type=F32)
    o_ref[...] = x2
    hn_ref[...] = _rms(x2, gn_ref[...]).astype(hn_ref.dtype)


def _merge_cross(o_sb, o_ds, w_sb, w_ds, w_out, proj, gate_offset, b_gate, x,
                 kv, g_cross, w_cq, w_co, g_next, batch, seq, *, tm):
    m, k = o_sb.shape
    d = w_sb.shape[1]
    n_mem = kv.shape[0] // batch
    width = MEM_HEADS * HEAD_DIM
    tm = min(tm, seq)
    assert gate_offset % d == 0 and seq % tm == 0
    g = gate_offset // d
    nt = seq // tm
    b_gate = b_gate.reshape(1, 2 * d).astype(F32)
    resident = pl.Buffered(1)

    def rows(width_, col=0):
        return pl.BlockSpec((tm, width_), lambda b, i: (b * nt + i, col))

    def whole(shape, *block):
        return pl.BlockSpec(shape, lambda b, i: block or (0,) * len(shape), pipeline_mode=resident)

    return pl.pallas_call(
        _merge_cross_body,
        grid=(batch, nt),
        in_specs=[
            rows(k), rows(k), whole((k, d)), whole((k, d)),
            rows(d, g), rows(d, g + 1), whole((1, d)), whole((1, d), 0, 1),
            whole((d, d)), rows(d),
            whole((1, d)), whole((d, width)),
            pl.BlockSpec((n_mem, width), lambda b, i: (b, 0)),
            pl.BlockSpec((n_mem, width), lambda b, i: (b, 1)),
            whole((width, d)), whole((1, d)),
        ],
        out_specs=[rows(d), rows(d)],
        out_shape=[jax.ShapeDtypeStruct((m, d), F32), jax.ShapeDtypeStruct((m, d), BF16)],
        compiler_params=_params("parallel", "parallel"),
        name="merge_out_cross",
    )(o_sb, o_ds, w_sb, w_ds, proj, proj, b_gate, b_gate, w_out, x,
      g_cross.reshape(1, d).astype(F32), w_cq, kv, kv, w_co, g_next.reshape(1, d).astype(F32))


def _delayed(u, tail, shift):
    rolled = pltpu.roll(u, shift, axis=0)
    row = lax.broadcasted_iota(I32, tail.shape, 0)
    head = jnp.where(row < shift, pltpu.roll(tail, shift, axis=0), rolled[:SUBLANES])
    return jnp.concatenate([head, rolled[SUBLANES:]], axis=0)


def _ffn_up_body(h_ref, wa_ref, wv_ref, cwa_ref, cwv_ref, cba_ref, cbv_ref, wd_ref, o_ref, wdb_ref,
                 wab_ref, wvb_ref, halo_ref, *, tiles_per_seq):
    i = pl.program_id(1)
    _round_chunks([wd_ref], [wdb_ref])

    @pl.when(i == 0)
    def _():
        wab_ref[...] = wa_ref[...].astype(BF16)
        wvb_ref[...] = wv_ref[...].astype(BF16)

    h = h_ref[...]
    tm = h.shape[0]
    sequence_start = i % tiles_per_seq == 0

    def conv(wb_ref, cw_ref, cb_ref, slot):
        u = jnp.dot(h, wb_ref[...], preferred_element_type=F32)
        tail = jnp.where(sequence_start, 0.0, halo_ref[slot])
        halo_ref[slot] = u[tm - SUBLANES:, :]
        c = cb_ref[...] + cw_ref[CONV_WIDTH - 1:CONV_WIDTH, :] * u
        for tap in range(CONV_WIDTH - 1):
            c = c + cw_ref[tap:tap + 1, :] * _delayed(u, tail, CONV_WIDTH - 1 - tap)
        return c

    a = conv(wab_ref, cwa_ref, cba_ref, 0)
    val = conv(wvb_ref, cwv_ref, cbv_ref, 1)
    o_ref[...] = (jax.nn.gelu(a) * val).astype(o_ref.dtype)


def _ffn_up_gate(h, w_up, conv_w, conv_b, w_down, seq, *, tm=ROW_TILE, tn=FFN_COL_TILE):
    m, d = h.shape
    two_ff = w_up.shape[1]
    d_ff = two_ff // 2
    tm, tn = min(tm, seq), min(tn, d_ff)
    assert seq % tm == 0 and d_ff % tn == 0 and tm >= SUBLANES >= CONV_WIDTH - 1
    nf, nt = d_ff // tn, m // tm
    conv_w = conv_w.astype(F32)
    conv_b = conv_b.reshape(1, two_ff).astype(F32)
    rider_in, rider_out, rider_shape = _rounding_riders([w_down], nf * nt, lambda j, i: j * nt + i)
    return pl.pallas_call(
        functools.partial(_ffn_up_body, tiles_per_seq=seq // tm),
        grid=(nf, nt),
        in_specs=[
            pl.BlockSpec((tm, d), lambda j, i: (i, 0)),
            pl.BlockSpec((d, tn), lambda j, i: (0, j)),
            pl.BlockSpec((d, tn), lambda j, i: (0, nf + j)),
            pl.BlockSpec((CONV_WIDTH, tn), lambda j, i: (0, j)),
            pl.BlockSpec((CONV_WIDTH, tn), lambda j, i: (0, nf + j)),
            pl.BlockSpec((1, tn), lambda j, i: (0, j)),
            pl.BlockSpec((1, tn), lambda j, i: (0, nf + j)),
        ] + rider_in,
        out_specs=[pl.BlockSpec((tm, tn), lambda j, i: (i, j))] + rider_out,
        out_shape=[jax.ShapeDtypeStruct((m, d_ff), BF16)] + rider_shape,
        scratch_shapes=[pltpu.VMEM((d, tn), BF16), pltpu.VMEM((d, tn), BF16),
                        pltpu.VMEM((2, SUBLANES, tn), F32)],
        compiler_params=_params("parallel", "arbitrary"),
        name="ffn_up_conv_gate",
    )(h, w_up, w_up, conv_w, conv_w, conv_b, conv_b, w_down)


def _ffn_down_body(a_ref, w_hbm, x_ref, g_ref, o_ref, w_ref, sem, *, n_chunks, final_norm):
    tk = w_ref.shape[0] // n_chunks

    def chunk_copy(c):
        rows = pl.ds(c * tk, tk)
        return pltpu.make_async_copy(w_hbm.at[rows], w_ref.at[rows], sem.at[c])

    def finish(y):
        o_ref[...] = _rms(y, g_ref[...]) if final_norm else y

    @pl.when(pl.program_id(0) == 0)
    def _():
        for c in range(n_chunks):
            chunk_copy(c).start()
        y = x_ref[...]
        for c in range(n_chunks):
            chunk_copy(c).wait()
            y = y + jnp.dot(a_ref[:, c * tk:(c + 1) * tk], w_ref[c * tk:(c + 1) * tk, :],
                            preferred_element_type=F32)
        finish(y)

    @pl.when(pl.program_id(0) > 0)
    def _():
        finish(x_ref[...] + jnp.dot(a_ref[...], w_ref[...], preferred_element_type=F32))


def _ffn_down(a, w, x, g_final, *, tm=FFN_DOWN_ROWS, tk=FFN_K_TILE):
    m, kdim = a.shape
    d = w.shape[1]
    tm, tk = min(tm, m), min(tk, kdim)
    assert m % tm == 0 and kdim % tk == 0
    final_norm = g_final is not None
    g = (g_final if final_norm else jnp.ones((d,), F32)).reshape(1, d).astype(F32)
    return pl.pallas_call(
        functools.partial(_ffn_down_body, n_chunks=kdim // tk, final_norm=final_norm),
        grid=(m // tm,),
        in_specs=[
            pl.BlockSpec((tm, kdim), lambda i: (i, 0)),
            pl.BlockSpec(memory_space=pl.ANY),
            pl.BlockSpec((tm, d), lambda i: (i, 0)),
            pl.BlockSpec((1, d), lambda i: (0, 0)),
        ],
        out_specs=pl.BlockSpec((tm, d), lambda i: (i, 0)),
        out_shape=jax.ShapeDtypeStruct((m, d), F32),
        scratch_shapes=[pltpu.VMEM((kdim, d), BF16), pltpu.SemaphoreType.DMA((kdim // tk,))],
        compiler_params=_params("arbitrary"),
        name="ffn_down",
    )(a, w, x, g)


def _layer(x, mem, g_mix, w_in, b_gate, w_proj_sb, w_proj_dsa, w_out, rel_bias,
           g_cross, g_mem, w_cq, w_ckv, w_co, g_ffn, w_up, conv_w, conv_b, w_down, g_final, batch, seq):
    d = x.shape[1]
    width = N_HEADS * HEAD_DIM
    idx_w = IDX_HEADS * IDX_DIM
    o_qi = 6 * width
    o_ki = o_qi + idx_w
    o_wi = o_ki + IDX_DIM
    o_g = o_wi + IDX_HEADS

    zeros = jnp.zeros((d, LANES - IDX_DIM), F32)
    w_small = jnp.concatenate([
        w_in[:, o_ki:o_wi], zeros, zeros, w_in[:, o_ki:o_wi],
        jnp.pad(w_in[:, o_wi:o_g], ((0, 0), (0, LANES - IDX_HEADS)))], axis=1).astype(BF16)
    h, small = _rmsnorm_proj(x, g_mix, w_small, name="mixer_norm_index_proj", proj_dtype=F32,
                             keep_rows=True)
    w_in_t = w_in.T
    tq = ATT_BLOCK
    nq = seq // tq
    q_sb, k_sb, v_sb, q_ds, k_ds, v_ds, q_ix = (g * width for g in range(o_ki // width))
    gate_rows = tuple(o_g + g * width for g in range(2 * d // width))
    n_gate = len(gate_rows)
    main = _matmul_ws(h, w_in_t, name="in_proj_main",
                      first_rows=gate_rows + (q_sb, k_sb, q_ds, k_ds, q_ix),
                      out_dtype=BF16, tm=PROJ_ROWS, tn=width)
    v_t = _matmul_ws_t(h, w_in_t, name="in_proj_values", first_rows=(v_sb, v_ds), tn=width, tq=tq,
                       tm=PROJ_ROWS)
    v_t = v_t.reshape(2, batch, nq, width, tq)

    o_sb, w_proj_sb, w_proj_dsa, w_out, w_cq, w_co = _sb_attention(
        main, v_t, batch, seq, q_col=n_gate, k_col=n_gate + 1, v_branch=0, tq=tq,
        round_weights=(w_proj_sb, w_proj_dsa, w_out, w_cq, w_co))
    bias = _near_bias(rel_bias, tq)
    o_ds = _dsa_attention(main, v_t, small, bias, batch, seq, qd_col=n_gate + 2, kd_col=n_gate + 3,
                          qi_col=n_gate + 4, v_branch=1, tq=tq)

    kv = _rmsnorm_proj(mem, g_mem, w_ckv, name="mem_norm_kv_proj", proj_dtype=BF16, keep_rows=False)
    x, h_ffn = _merge_cross(o_sb, o_ds, w_proj_sb, w_proj_dsa, w_out, main, 0, b_gate, x, kv, g_cross,
                            w_cq, w_co, g_ffn, batch, seq, tm=MERGE_ROWS)

    act, w_down = _ffn_up_gate(h_ffn, w_up, conv_w, conv_b, w_down, seq)
    return _ffn_down(act, w_down, x, g_final)


def kernel(x, mem, g_mix, w_in, b_gate, w_proj_sb, w_proj_dsa, w_out, rel_bias, g_cross, g_mem,
           w_cq, w_ckv, w_co, g_ffn, w_up, conv_w, conv_b, w_down, g_final):
    batch, seq, d = x.shape
    h = x.reshape(batch * seq, d)
    mem2 = mem.reshape(batch * mem.shape[1], d)
    depth = g_mix.shape[0]
    for l in range(depth):
        h = _layer(h, mem2, g_mix[l], w_in[l], b_gate[l], w_proj_sb[l], w_proj_dsa[l], w_out[l],
                   rel_bias, g_cross[l], g_mem[l], w_cq[l], w_ckv[l], w_co[l], g_ffn[l], w_up[l],
                   conv_w[l], conv_b[l], w_down[l], g_final if l == depth - 1 else None, batch, seq)
    return h.reshape(batch, seq, d)
```

```python
import functools

import jax
import jax.numpy as jnp
from jax import lax
from jax.experimental import pallas as pl
from jax.experimental.pallas import tpu as pltpu

F32, BF16, I32 = jnp.float32, jnp.bfloat16, jnp.int32

EPS = 1e-6
HEAD_DIM = 128
N_HEADS = 8
IDX_HEADS = 16
IDX_DIM = 64
CHUNK = 64
TOPK_MAX = 256
N_BUCKETS = 32
MAX_DISTANCE = 128
MEM_HEADS = 4
CONV_WIDTH = 3

LANES = 128
SUBLANES = 8
VMEM_LIMIT_BYTES = 62 * 1024 * 1024
NEG_BIG = -1e30
EXP2_UNDERFLOW = -151.0
LOG2E = 1.4426950408889634
KEY_BITS = 32
SIGN_BIT = -(2 ** 31)

PROJ_ROWS = 2048
ROW_TILE = 1024
FFN_DOWN_ROWS = 512
NORM_ROWS = 1024
MERGE_ROWS = 512
FFN_COL_TILE = 512
FFN_K_TILE = 1024
ATT_BLOCK = KEY_BITS * SUBLANES
RADIX_GROUPS = 8

_NT = (((1,), (1,)), ((), ()))


def _params(*sem):
    return pltpu.CompilerParams(dimension_semantics=sem, vmem_limit_bytes=VMEM_LIMIT_BYTES)


def _rms(x, g):
    inv = lax.rsqrt(jnp.mean(x * x, axis=-1, keepdims=True) + EPS)
    return x * inv * g


def _rounding_riders(weights, n_steps, step_index):
    in_specs, out_specs, out_shape = [], [], []
    for w in weights:
        rows, cols = w.shape
        assert rows % n_steps == 0 and (rows // n_steps) % (2 * SUBLANES) == 0
        for specs in (in_specs, out_specs):
            specs.append(pl.BlockSpec((rows // n_steps, cols), lambda *ids: (step_index(*ids), 0)))
        out_shape.append(jax.ShapeDtypeStruct(w.shape, BF16))
    return in_specs, out_specs, out_shape


def _round_chunks(src_refs, dst_refs):
    for src, dst in zip(src_refs, dst_refs):
        dst[...] = src[...].astype(BF16)


def _mm_ws_body(a_ref, wt_ref, o_ref, wb_ref):
    @pl.when(pl.program_id(1) == 0)
    def _():
        wb_ref[...] = wt_ref[...].astype(BF16)

    o_ref[...] = lax.dot_general(a_ref[...], wb_ref[...], _NT,
                                 preferred_element_type=F32).astype(o_ref.dtype)


def _row_window(first_rows, tn, k):
    assert all(r % SUBLANES == 0 for r in first_rows)

    def index_map(j, i):
        row = jnp.int32(first_rows[0])
        for step, first in enumerate(first_rows[1:], start=1):
            row = jnp.where(j >= step, first, row)
        return pl.multiple_of(row, SUBLANES), 0

    return pl.BlockSpec((pl.Element(tn), pl.Element(k)), index_map)


def _matmul_ws(a, w_t, *, name, first_rows, out_dtype, tm, tn):
    m, k = a.shape
    tm = min(tm, m)
    assert m % tm == 0 and max(first_rows) + tn <= w_t.shape[0] and w_t.shape[1] == k
    return pl.pallas_call(
        _mm_ws_body,
        grid=(len(first_rows), m // tm),
        in_specs=[pl.BlockSpec((tm, k), lambda j, i: (i, 0)), _row_window(first_rows, tn, k)],
        out_specs=pl.BlockSpec((tm, tn), lambda j, i: (i, j)),
        out_shape=jax.ShapeDtypeStruct((m, len(first_rows) * tn), out_dtype),
        scratch_shapes=[pltpu.VMEM((tn, k), BF16)],
        compiler_params=_params("parallel", "arbitrary"),
        name=name,
    )(a, w_t)


def _mm_ws_t_body(wt_ref, a_ref, o_ref, wb_ref):
    @pl.when(pl.program_id(1) == 0)
    def _():
        wb_ref[...] = wt_ref[...].astype(BF16)

    res = lax.dot_general(wb_ref[...], a_ref[...], _NT, preferred_element_type=F32)
    n_blocks, _, tq = o_ref.shape
    for blk in range(n_blocks):
        o_ref[blk] = res[:, blk * tq:(blk + 1) * tq].astype(o_ref.dtype)


def _matmul_ws_t(a, w_t, *, name, first_rows, tn, tq, tm):
    m, k = a.shape
    tm = min(tm, m)
    assert m % tm == 0 and tm % tq == 0 and max(first_rows) + tn <= w_t.shape[0] and w_t.shape[1] == k
    per_step = tm // tq
    return pl.pallas_call(
        _mm_ws_t_body,
        grid=(len(first_rows), m // tm),
        in_specs=[_row_window(first_rows, tn, k), pl.BlockSpec((tm, k), lambda j, i: (i, 0))],
        out_specs=pl.BlockSpec((None, per_step, tn, tq), lambda j, i: (j, i, 0, 0)),
        out_shape=jax.ShapeDtypeStruct((len(first_rows), m // tq, tn, tq), BF16),
        scratch_shapes=[pltpu.VMEM((tn, k), BF16)],
        compiler_params=_params("parallel", "arbitrary"),
        name=name,
    )(w_t, a)


def _norm_proj_body(x_ref, g_ref, w_ref, *out_refs):
    h = _rms(x_ref[...], g_ref[...]).astype(BF16)
    p_ref = out_refs[-1]
    p_ref[...] = jnp.dot(h, w_ref[...].astype(BF16), preferred_element_type=F32).astype(p_ref.dtype)
    if len(out_refs) == 2:
        out_refs[0][...] = h


def _rmsnorm_proj(x, g, w, *, name, proj_dtype, keep_rows, tm=NORM_ROWS):
    m, d = x.shape
    n = w.shape[1]
    tm = min(tm, m)
    assert m % tm == 0
    out_specs = [pl.BlockSpec((tm, n), lambda i: (i, 0))]
    out_shape = [jax.ShapeDtypeStruct((m, n), proj_dtype)]
    if keep_rows:
        out_specs.insert(0, pl.BlockSpec((tm, d), lambda i: (i, 0)))
        out_shape.insert(0, jax.ShapeDtypeStruct((m, d), BF16))
    out = pl.pallas_call(
        _norm_proj_body,
        grid=(m // tm,),
        in_specs=[pl.BlockSpec((tm, d), lambda i: (i, 0)), pl.BlockSpec((1, d), lambda i: (0, 0)),
                  pl.BlockSpec((d, n), lambda i: (0, 0))],
        out_specs=out_specs,
        out_shape=out_shape,
        compiler_params=_params("parallel"),
        name=name,
    )(x, g.reshape(1, d).astype(F32), w)
    return out if keep_rows else out[0]


def _sb_body(q_ref, k_ref, vt_ref, tri_ref, *refs, tq, n_riders):
    o_ref, qs_ref, z_ref, lb_ref, wb_ref, acc_ref = refs[n_riders], *refs[2 * n_riders + 1:]
    _round_chunks(refs[:n_riders], refs[n_riders + 1:2 * n_riders + 1])
    i = pl.program_id(1)
    shape = (tq, tq)
    before = lax.broadcasted_iota(I32, shape, 0) < lax.broadcasted_iota(I32, shape, 1)
    heads = [slice(h * HEAD_DIM, (h + 1) * HEAD_DIM) for h in range(N_HEADS)]
    qs_ref[...] = (q_ref[...].astype(F32) * (HEAD_DIM ** -0.5 * LOG2E)).astype(BF16)
    acc_ref[...] = jnp.zeros(acc_ref.shape, F32)

    def logits(j, diagonal, slot):
        start = pl.multiple_of(j * tq, tq)
        for h, hs in enumerate(heads):
            z = lax.dot_general(k_ref[pl.ds(start, tq), hs], qs_ref[:, hs], _NT,
                                preferred_element_type=F32)
            neg_abs = lax.bitcast_convert_type(lax.bitcast_convert_type(z, I32) | SIGN_BIT, F32)
            log_lose = jnp.maximum(z, 0.0) + jnp.log2(1.0 + jnp.exp2(neg_abs))
            if diagonal:
                log_lose = jnp.where(before, log_lose, 0.0)
            z_ref[slot, h] = z
            lb_ref[slot, h] = log_lose.astype(BF16)

    def weights(carry, diagonal, slot):
        new_carry = []
        for h in range(N_HEADS):
            c = carry[h:h + 1, :] - jnp.dot(tri_ref[...], lb_ref[slot, h], preferred_element_type=F32)
            w = jnp.exp2(z_ref[slot, h] + c)
            if diagonal:
                w = jnp.where(before, w, 0.0)
            wb_ref[slot, h] = w.astype(BF16)
            new_carry.append(c[0:1, :])
        return jnp.concatenate(new_carry, axis=0)

    def values(j, slot):
        for h, hs in enumerate(heads):
            acc_ref[h] += jnp.dot(vt_ref[j, hs, :], wb_ref[slot, h], preferred_element_type=F32)

    def tile(j, carry, diagonal):
        logits(j, diagonal, 0)
        carry = weights(carry, diagonal, 0)
        values(j, 0)
        return carry

    no_carry = jnp.zeros((N_HEADS, tq), F32)

    def diagonal_and_previous():
        logits(i, True, 0)
        logits(i - 1, False, 1)
        carry = weights(weights(no_carry, True, 0), False, 1)
        values(i, 0)
        values(i - 1, 1)
        return carry

    carry = lax.cond(i >= 1, diagonal_and_previous, lambda: tile(i, no_carry, True))

    def cond(state):
        j, live, _ = state
        return jnp.logical_and(j >= 0, live > EXP2_UNDERFLOW)

    def body(state):
        j, _, carry = state
        carry = tile(j, carry, False)
        return j - 1, jnp.max(carry), carry

    lax.while_loop(cond, body, (i - 2, jnp.max(carry), carry))
    for h, hs in enumerate(heads):
        o_ref[:, hs] = acc_ref[h].T.astype(o_ref.dtype)


def _sb_attention(qk, v_t, batch, seq, *, q_col, k_col, v_branch, tq, round_weights=()):
    nq = seq // tq
    width = N_HEADS * HEAD_DIM
    tri = (jnp.arange(tq)[None, :] >= jnp.arange(tq)[:, None]).astype(BF16)
    resident = pl.Buffered(1)
    rider_in, rider_out, rider_shape = _rounding_riders(round_weights, batch * nq,
                                                        lambda b, i: b * nq + i)
    return pl.pallas_call(
        functools.partial(_sb_body, tq=tq, n_riders=len(round_weights)),
        grid=(batch, nq),
        in_specs=[
            pl.BlockSpec((tq, width), lambda b, i: (b * nq + i, q_col)),
            pl.BlockSpec((seq, width), lambda b, i: (b, k_col), pipeline_mode=resident),
            pl.BlockSpec((None, None, nq, width, tq), lambda b, i: (v_branch, b, 0, 0, 0),
                         pipeline_mode=resident),
            pl.BlockSpec((tq, tq), lambda b, i: (0, 0), pipeline_mode=resident),
        ] + rider_in,
        out_specs=[pl.BlockSpec((tq, width), lambda b, i: (b * nq + i, 0))] + rider_out,
        out_shape=[jax.ShapeDtypeStruct((batch * seq, width), BF16)] + rider_shape,
        scratch_shapes=[
            pltpu.VMEM((tq, width), BF16),
            pltpu.VMEM((2, N_HEADS, tq, tq), F32),
            pltpu.VMEM((2, N_HEADS, tq, tq), BF16),
            pltpu.VMEM((2, N_HEADS, tq, tq), BF16),
            pltpu.VMEM((N_HEADS, HEAD_DIM, tq), F32),
        ],
        compiler_params=_params("parallel", "arbitrary"),
        name="sb_attention",
    )(qk, qk, v_t, tri, *round_weights)


def _bucket_thresholds():
    nb = N_BUCKETS // 2
    max_exact = nb // 2
    span = nb - max_exact
    out = []
    for k in range(1, span):
        n = max_exact
        while n ** span * max_exact ** k < MAX_DISTANCE ** k * max_exact ** span:
            n += 1
        out.append(n)
    return max_exact, out


def _bias_body(rb_ref, o_ref, *, tq):
    nb = N_BUCKETS // 2
    max_exact, steps = _bucket_thresholds()
    shape = (2 * tq, tq)
    rel = lax.broadcasted_iota(I32, shape, 0) - lax.broadcasted_iota(I32, shape, 1) - tq
    n = jnp.abs(rel)
    large = jnp.full(shape, max_exact, I32)
    for t in steps:
        large = large + (n >= t).astype(I32)
    bucket = jnp.where(rel > 0, nb, 0) + jnp.where(n < max_exact, n, large)
    for h in range(N_HEADS):
        val = jnp.zeros(shape, F32)
        for b in range(N_BUCKETS):
            val = jnp.where(bucket == b, rb_ref[b, h], val)
        o_ref[h] = (val - rb_ref[nb - 1, h]) * LOG2E


def _near_bias(rel_bias, tq):
    return pl.pallas_call(
        functools.partial(_bias_body, tq=tq),
        in_specs=[pl.BlockSpec(memory_space=pltpu.SMEM)],
        out_specs=pl.BlockSpec(memory_space=pltpu.VMEM),
        out_shape=jax.ShapeDtypeStruct((N_HEADS, 2 * tq, tq), F32),
        compiler_params=pltpu.CompilerParams(vmem_limit_bytes=VMEM_LIMIT_BYTES),
        name="dsa_near_bias",
    )(rel_bias.astype(F32))


def _order_key(x):
    bits = lax.bitcast_convert_type(x, I32)
    return bits ^ ((bits >> 31) | SIGN_BIT)


def _order_key_to_float(key):
    return lax.bitcast_convert_type(key ^ ((~key >> 31) | SIGN_BIT), F32)


def _bit_transpose32(words):
    a = list(words)
    j, m = 16, 0x0000FFFF
    while j:
        mask = jnp.int32(m - (1 << 32) if m >= 1 << 31 else m)
        k = 0
        while k < 32:
            t = (lax.shift_right_logical(a[k], jnp.int32(j)) ^ a[k + j]) & mask
            a[k] = a[k] ^ lax.shift_left(t, jnp.int32(j))
            a[k + j] = a[k + j] ^ t
            k = (k + j + 1) & ~j
        j >>= 1
        m = (m ^ (m << j)) & 0xFFFFFFFF
    return a


def _dsa_body(qd_ref, qi_ref, wq_ref, kd_ref, vt_ref, ki_ref, bias_ref, tri_ref, o_ref,
              sc_ref, plane_ref, qs_ref, m_ref, l_ref, acc_ref, lg_ref, bmax_ref, *, tq, top):
    i = pl.program_id(1)
    shape = (tq, tq)
    key_row = lax.broadcasted_iota(I32, shape, 0)
    qry_col = lax.broadcasted_iota(I32, shape, 1)
    visible = key_row // CHUNK <= qry_col // CHUNK

    w_t = (wq_ref[...] * (IDX_DIM ** -0.5 * IDX_HEADS ** -0.5)).T

    heads_per_vreg = LANES // IDX_DIM

    def score_tile(j):
        start = pl.multiple_of(j * tq, tq)
        ki = [ki_ref[pl.ds(start, tq), c * LANES:(c + 1) * LANES].astype(BF16)
              for c in range(heads_per_vreg)]
        s = jnp.zeros(shape, F32)
        for h in range(IDX_HEADS):
            g, c = divmod(h, heads_per_vreg)
            d = lax.dot_general(ki[c], qi_ref[:, g * LANES:(g + 1) * LANES], _NT,
                                preferred_element_type=F32)
            s = s + w_t[h:h + 1, :] * jnp.maximum(d, 0.0)
        sc_ref[j] = s

    def key_planes(j):
        ukey = _order_key(sc_ref[j])
        planes = _bit_transpose32([ukey[g * SUBLANES:(g + 1) * SUBLANES, :] for g in range(KEY_BITS)])
        for b in range(KEY_BITS):
            plane_ref[b, pl.ds(pl.multiple_of(j * SUBLANES, SUBLANES), SUBLANES), :] = planes[b]

    @pl.when(i == 0)
    def _():
        plane_ref[...] = jnp.zeros(plane_ref.shape, I32)

    def score_steps(first, count):
        for j in range(count):
            key_planes(first + j)
            score_tile(first + j + 1)

    def score_quad(p, carry):
        score_steps(4 * p, 4)
        return carry

    score_tile(0)
    lax.fori_loop(0, i // 4, score_quad, 0)

    @pl.when(i % 4 >= 2)
    def _():
        score_steps(4 * (i // 4), 2)

    @pl.when(i % 2 == 1)
    def _():
        score_steps(i - 1, 1)

    key_planes(i)
    sc_ref[i] = jnp.where(visible, sc_ref[i], -jnp.inf)

    def popcount_rows(words):
        return jnp.sum(lax.population_count(words), axis=0, keepdims=True)

    def radix_select(n_rows):
        block_of_row = lax.broadcasted_iota(I32, (n_rows, tq), 0) // SUBLANES
        qry_of_col = lax.broadcasted_iota(I32, (n_rows, tq), 1)
        n_bits = (qry_of_col // CHUNK + 1) * (CHUNK // SUBLANES)
        diag_bits = jnp.where(n_bits >= KEY_BITS, -1, lax.shift_left(jnp.int32(1), n_bits) - 1)
        cand0 = jnp.where(block_of_row < i, -1, jnp.where(block_of_row == i, diag_bits, 0))

        def bit_step(t, state):
            cand, n_above, thr_bits = state
            b = KEY_BITS - 1 - t
            ones = cand & plane_ref[b, :n_rows]
            n_ones = popcount_rows(ones)
            take = n_above + n_ones >= top
            cand = jnp.where(take, ones, cand ^ ones)
            n_above = jnp.where(take, n_above, n_above + n_ones)
            thr_bits = thr_bits | jnp.where(take, lax.shift_left(jnp.int32(1), b), 0)
            return cand, n_above, thr_bits

        zero = jnp.zeros((1, tq), I32)
        cand, n_above, thr_bits = lax.fori_loop(0, KEY_BITS, bit_step, (cand0, zero, zero), unroll=4)
        return n_above, thr_bits, popcount_rows(cand)

    n_groups = min(RADIX_GROUPS, plane_ref.shape[1] // SUBLANES)

    def select_from(group):
        if group == n_groups - 1:
            return radix_select(plane_ref.shape[1])
        rows = (group + 1) * (plane_ref.shape[1] // n_groups)
        return lax.cond(i * SUBLANES < rows, functools.partial(radix_select, rows),
                        functools.partial(select_from, group + 1))

    n_above, thr_bits, n_equal = select_from(0)
    qry = lax.broadcasted_iota(I32, (1, tq), 1)
    n_visible = i * tq + (qry // CHUNK + 1) * CHUNK
    wanted = n_visible > top
    thr = jnp.where(wanted, _order_key_to_float(thr_bits), jnp.finfo(F32).min)
    tied = jnp.logical_and(wanted, n_above + n_equal > top)
    c_hi = n_above

    ones = jnp.ones((2 * SUBLANES, tq), BF16)

    def plain_mask(j, carry):
        sc_ref[j] = jnp.where(sc_ref[j] >= thr, 0.0, NEG_BIG)
        return carry

    def tie_mask(j, seen):
        s = sc_ref[j]
        equal = s == thr
        rank = jnp.dot(tri_ref[...], equal.astype(BF16), preferred_element_type=F32) + seen
        quota = jnp.where(tied, (top - c_hi).astype(F32), jnp.inf)
        keep_equal = jnp.where(rank < quota, 0.0, NEG_BIG)
        sc_ref[j] = jnp.where(s > thr, 0.0, jnp.where(equal, keep_equal, NEG_BIG))
        return seen + jnp.sum(equal.astype(F32), axis=0, keepdims=True)

    def with_ties():
        lax.fori_loop(0, i + 1, tie_mask, jnp.zeros((1, tq), F32))
        return jnp.int32(0)

    def without_ties():
        return lax.fori_loop(0, i + 1, plain_mask, jnp.int32(0))

    lax.cond(jnp.max(tied.astype(I32)) > 0, with_ties, without_ties)

    qs_ref[...] = (qd_ref[...].astype(F32) * (HEAD_DIM ** -0.5 * LOG2E)).astype(BF16)
    m_ref[...] = jnp.full(m_ref.shape, NEG_BIG, F32)
    l_ref[...] = jnp.zeros(l_ref.shape, F32)
    acc_ref[...] = jnp.zeros(acc_ref.shape, F32)

    heads = [slice(h * HEAD_DIM, (h + 1) * HEAD_DIM) for h in range(N_HEADS)]

    far, prev, diag = None, 0, 1

    def logits(u, near, slot):
        j = i - u
        start = pl.multiple_of(j * tq, tq)
        mask = sc_ref[j]
        block_max = []
        for h, hs in enumerate(heads):
            lg = lax.dot_general(kd_ref[pl.ds(start, tq), hs], qs_ref[:, hs], _NT,
                                 preferred_element_type=F32)
            if near is not None:
                lg = lg + bias_ref[h, near * tq:(near + 1) * tq, :]
            lg = lg + mask
            lg_ref[slot, h] = lg
            block_max.append(jnp.max(lg, axis=0, keepdims=True))
        bmax_ref[slot] = jnp.concatenate(block_max, axis=0)

    def values(u, slot):
        j = i - u
        m_old = m_ref[...]
        m_new = jnp.maximum(m_old, bmax_ref[slot])
        alpha = jnp.exp2(m_old - m_new)
        m_ref[...] = m_new
        denom = []
        for h, hs in enumerate(heads):
            p = jnp.exp2(lg_ref[slot, h] - m_new[h:h + 1, :]).astype(BF16)
            v_ext = jnp.concatenate([vt_ref[j, hs, :], ones], axis=0)
            pv = jnp.dot(v_ext, p, preferred_element_type=F32)
            acc_ref[h] = alpha[h:h + 1, :] * acc_ref[h] + pv[:HEAD_DIM]
            denom.append(pv[HEAD_DIM:HEAD_DIM + 1])
        l_ref[...] = alpha * l_ref[...] + jnp.concatenate(denom, axis=0)

    def even_step(u, near_a, near_b):
        logits(u - 1, near_a, 1)
        values(u, 0)
        logits(u - 2, near_b, 0)
        values(u - 1, 1)

    @pl.when(i == 0)
    def _():
        logits(0, diag, 0)
        values(0, 0)

    @pl.when(i == 1)
    def _():
        logits(1, prev, 1)

    @pl.when(jnp.logical_and(i >= 2, i % 2 == 1))
    def _():
        logits(i, far, 1)
        logits(i - 1, far, 0)
        values(i, 1)

    @pl.when(jnp.logical_and(i >= 2, i % 2 == 0))
    def _():
        logits(i, far, 0)

    n_far_pairs = i // 2 - 1

    def far_quad(k, carry):
        u = 2 * (i // 2 - 2 * k)
        even_step(u, far, far)
        even_step(u - 2, far, far)
        return carry

    lax.fori_loop(0, n_far_pairs // 2, far_quad, 0)

    @pl.when(jnp.logical_and(n_far_pairs > 0, n_far_pairs % 2 == 1))
    def _():
        even_step(4, far, far)

    @pl.when(i >= 2)
    def _():
        even_step(2, prev, diag)

    @pl.when(i == 1)
    def _():
        logits(0, diag, 0)
        values(1, 1)

    @pl.when(i >= 1)
    def _():
        values(0, 0)

    for h in range(N_HEADS):
        o = acc_ref[h] / l_ref[h:h + 1, :]
        o_ref[:, h * HEAD_DIM:(h + 1) * HEAD_DIM] = o.T.astype(o_ref.dtype)


def _dsa_attention(main, v_t, small, bias, batch, seq, *, qd_col, kd_col, qi_col, v_branch, tq):
    nq = seq // tq
    width = N_HEADS * HEAD_DIM
    assert IDX_HEADS * IDX_DIM == width and tq == KEY_BITS * SUBLANES and tq % CHUNK == 0
    assert nq % min(RADIX_GROUPS, nq) == 0
    key_copies = LANES // IDX_DIM
    top = min(TOPK_MAX, seq // 4)
    tri = (jnp.arange(tq)[None, :] < jnp.arange(tq)[:, None]).astype(BF16)
    resident = pl.Buffered(1)
    return pl.pallas_call(
        functools.partial(_dsa_body, tq=tq, top=top),
        grid=(batch, nq),
        in_specs=[
            pl.BlockSpec((tq, width), lambda b, i: (b * nq + i, qd_col)),
            pl.BlockSpec((tq, IDX_HEADS * IDX_DIM), lambda b, i: (b * nq + i, qi_col)),
            pl.BlockSpec((tq, LANES), lambda b, i: (b * nq + i, key_copies)),
            pl.BlockSpec((seq, width), lambda b, i: (b, kd_col), pipeline_mode=resident),
            pl.BlockSpec((None, None, nq, width, tq), lambda b, i: (v_branch, b, 0, 0, 0),
                         pipeline_mode=resident),
            pl.BlockSpec((seq, key_copies * LANES), lambda b, i: (b, 0), pipeline_mode=resident),
            pl.BlockSpec((N_HEADS, 2 * tq, tq), lambda b, i: (0, 0, 0), pipeline_mode=resident),
            pl.BlockSpec((tq, tq), lambda b, i: (0, 0), pipeline_mode=resident),
        ],
        out_specs=pl.BlockSpec((tq, width), lambda b, i: (b * nq + i, 0)),
        out_shape=jax.ShapeDtypeStruct((batch * seq, width), BF16),
        scratch_shapes=[
            pltpu.VMEM((nq, tq, tq), F32),
            pltpu.VMEM((KEY_BITS, nq * SUBLANES, tq), I32),
            pltpu.VMEM((tq, width), BF16),
            pltpu.VMEM((N_HEADS, tq), F32),
            pltpu.VMEM((N_HEADS, tq), F32),
            pltpu.VMEM((N_HEADS, HEAD_DIM, tq), F32),
            pltpu.VMEM((2, N_HEADS, tq, tq), F32),
            pltpu.VMEM((2, N_HEADS, tq), F32),
        ],
        compiler_params=_params("parallel", "arbitrary"),
        name="dsa_attention",
    )(main, main, small, main, v_t, small, bias, tri)


def _merge_cross_body(osb_ref, ods_ref, wsb_ref, wds_ref, gsb_ref, gds_ref, bsb_ref, bds_ref, wo_ref,
                      x_ref, gc_ref, wq_ref, km_ref, vm_ref, wco_ref, gn_ref, o_ref, hn_ref):
    p_sb = jnp.dot(osb_ref[...], wsb_ref[...], preferred_element_type=F32)
    p_ds = jnp.dot(ods_ref[...], wds_ref[...], preferred_element_type=F32)
    g_sb = jax.nn.sigmoid(gsb_ref[...].astype(F32) + bsb_ref[...])
    g_ds = jax.nn.sigmoid(gds_ref[...].astype(F32) + bds_ref[...])
    merged = (g_sb * p_sb + g_ds * p_ds).astype(BF16)
    x1 = x_ref[...] + jnp.dot(merged, wo_ref[...], preferred_element_type=F32)

    h = _rms(x1, gc_ref[...]).astype(BF16)
    q = jnp.dot(h, wq_ref[...], preferred_element_type=F32) * HEAD_DIM ** -0.5
    q = q.astype(BF16)
    outs = []
    for hh in range(MEM_HEADS):
        hs = slice(hh * HEAD_DIM, (hh + 1) * HEAD_DIM)
        lg = lax.dot_general(q[:, hs], km_ref[:, hs], _NT, preferred_element_type=F32)
        p = jnp.exp(lg - jnp.max(lg, axis=1, keepdims=True))
        o = jnp.dot(p.astype(BF16), vm_ref[:, hs], preferred_element_type=F32)
        outs.append((o / jnp.sum(p, axis=1, keepdims=True)).astype(BF16))
    o = jnp.concatenate(outs, axis=1)
    x2 = x1 + jnp.dot(o, wco_ref[...], preferred_element_type=F32)
    o_ref[...] = x2
    hn_ref[...] = _rms(x2, gn_ref[...]).astype(hn_ref.dtype)


def _merge_cross(o_sb, o_ds, w_sb, w_ds, w_out, proj, gate_offset, b_gate, x,
                 kv, g_cross, w_cq, w_co, g_next, batch, seq, *, tm):
    m, k = o_sb.shape
    d = w_sb.shape[1]
    n_mem = kv.shape[0] // batch
    width = MEM_HEADS * HEAD_DIM
    tm = min(tm, seq)
    assert gate_offset % d == 0 and seq % tm == 0
    g = gate_offset // d
    nt = seq // tm
    b_gate = b_gate.reshape(1, 2 * d).astype(F32)
    resident = pl.Buffered(1)

    def rows(width_, col=0):
        return pl.BlockSpec((tm, width_), lambda b, i: (b * nt + i, col))

    def whole(shape, *block):
        return pl.BlockSpec(shape, lambda b, i: block or (0,) * len(shape), pipeline_mode=resident)

    return pl.pallas_call(
        _merge_cross_body,
        grid=(batch, nt),
        in_specs=[
            rows(k), rows(k), whole((k, d)), whole((k, d)),
            rows(d, g), rows(d, g + 1), whole((1, d)), whole((1, d), 0, 1),
            whole((d, d)), rows(d),
            whole((1, d)), whole((d, width)),
            pl.BlockSpec((n_mem, width), lambda b, i: (b, 0)),
            pl.BlockSpec((n_mem, width), lambda b, i: (b, 1)),
            whole((width, d)), whole((1, d)),
        ],
        out_specs=[rows(d), rows(d)],
        out_shape=[jax.ShapeDtypeStruct((m, d), F32), jax.ShapeDtypeStruct((m, d), BF16)],
        compiler_params=_params("parallel", "parallel"),
        name="merge_out_cross",
    )(o_sb, o_ds, w_sb, w_ds, proj, proj, b_gate, b_gate, w_out, x,
      g_cross.reshape(1, d).astype(F32), w_cq, kv, kv, w_co, g_next.reshape(1, d).astype(F32))


def _delayed(u, tail, shift):
    rolled = pltpu.roll(u, shift, axis=0)
    row = lax.broadcasted_iota(I32, tail.shape, 0)
    head = jnp.where(row < shift, pltpu.roll(tail, shift, axis=0), rolled[:SUBLANES])
    return jnp.concatenate([head, rolled[SUBLANES:]], axis=0)


def _ffn_up_body(h_ref, wa_ref, wv_ref, cwa_ref, cwv_ref, cba_ref, cbv_ref, wd_ref, o_ref, wdb_ref,
                 wab_ref, wvb_ref, halo_ref, *, tiles_per_seq):
    i = pl.program_id(1)
    _round_chunks([wd_ref], [wdb_ref])

    @pl.when(i == 0)
    def _():
        wab_ref[...] = wa_ref[...].astype(BF16)
        wvb_ref[...] = wv_ref[...].astype(BF16)

    h = h_ref[...]
    tm = h.shape[0]
    sequence_start = i % tiles_per_seq == 0

    def conv(wb_ref, cw_ref, cb_ref, slot):
        u = jnp.dot(h, wb_ref[...], preferred_element_type=F32)
        tail = jnp.where(sequence_start, 0.0, halo_ref[slot])
        halo_ref[slot] = u[tm - SUBLANES:, :]
        c = cb_ref[...] + cw_ref[CONV_WIDTH - 1:CONV_WIDTH, :] * u
        for tap in range(CONV_WIDTH - 1):
            c = c + cw_ref[tap:tap + 1, :] * _delayed(u, tail, CONV_WIDTH - 1 - tap)
        return c

    a = conv(wab_ref, cwa_ref, cba_ref, 0)
    val = conv(wvb_ref, cwv_ref, cbv_ref, 1)
    o_ref[...] = (jax.nn.gelu(a) * val).astype(o_ref.dtype)


def _ffn_up_gate(h, w_up, conv_w, conv_b, w_down, seq, *, tm=ROW_TILE, tn=FFN_COL_TILE):
    m, d = h.shape
    two_ff = w_up.shape[1]
    d_ff = two_ff // 2
    tm, tn = min(tm, seq), min(tn, d_ff)
    assert seq % tm == 0 and d_ff % tn == 0 and tm >= SUBLANES >= CONV_WIDTH - 1
    nf, nt = d_ff // tn, m // tm
    conv_w = conv_w.astype(F32)
    conv_b = conv_b.reshape(1, two_ff).astype(F32)
    rider_in, rider_out, rider_shape = _rounding_riders([w_down], nf * nt, lambda j, i: j * nt + i)
    return pl.pallas_call(
        functools.partial(_ffn_up_body, tiles_per_seq=seq // tm),
        grid=(nf, nt),
        in_specs=[
            pl.BlockSpec((tm, d), lambda j, i: (i, 0)),
            pl.BlockSpec((d, tn), lambda j, i: (0, j)),
            pl.BlockSpec((d, tn), lambda j, i: (0, nf + j)),
            pl.BlockSpec((CONV_WIDTH, tn), lambda j, i: (0, j)),
            pl.BlockSpec((CONV_WIDTH, tn), lambda j, i: (0, nf + j)),
            pl.BlockSpec((1, tn), lambda j, i: (0, j)),
            pl.BlockSpec((1, tn), lambda j, i: (0, nf + j)),
        ] + rider_in,
        out_specs=[pl.BlockSpec((tm, tn), lambda j, i: (i, j))] + rider_out,
        out_shape=[jax.ShapeDtypeStruct((m, d_ff), BF16)] + rider_shape,
        scratch_shapes=[pltpu.VMEM((d, tn), BF16), pltpu.VMEM((d, tn), BF16),
                        pltpu.VMEM((2, SUBLANES, tn), F32)],
        compiler_params=_params("parallel", "arbitrary"),
        name="ffn_up_conv_gate",
    )(h, w_up, w_up, conv_w, conv_w, conv_b, conv_b, w_down)


def _ffn_down_body(a_ref, w_hbm, x_hbm, g_ref, o_ref, w_ref, sem, *, n_chunks, final_norm):
    i = pl.program_id(0)
    tm = o_ref.shape[0]
    tk = w_ref.shape[0] // n_chunks

    def chunk_copy(c):
        rows = pl.ds(c * tk, tk)
        return pltpu.make_async_copy(w_hbm.at[rows], w_ref.at[rows], sem.at[c])

    residual_copy = pltpu.make_async_copy(x_hbm.at[pl.ds(pl.multiple_of(i * tm, tm), tm)], o_ref,
                                          sem.at[n_chunks])
    residual_copy.start()

    def finish(y):
        residual_copy.wait()
        y = o_ref[...] + y
        o_ref[...] = _rms(y, g_ref[...]) if final_norm else y

    @pl.when(i == 0)
    def _():
        for c in range(n_chunks):
            chunk_copy(c).start()
        y = jnp.zeros(o_ref.shape, F32)
        for c in range(n_chunks):
            chunk_copy(c).wait()
            y = y + jnp.dot(a_ref[:, c * tk:(c + 1) * tk], w_ref[c * tk:(c + 1) * tk, :],
                            preferred_element_type=F32)
        finish(y)

    @pl.when(i > 0)
    def _():
        finish(jnp.dot(a_ref[...], w_ref[...], preferred_element_type=F32))


def _ffn_down(a, w, x, g_final, *, tm=FFN_DOWN_ROWS, tk=FFN_K_TILE):
    m, kdim = a.shape
    d = w.shape[1]
    tm, tk = min(tm, m), min(tk, kdim)
    assert m % tm == 0 and kdim % tk == 0
    final_norm = g_final is not None
    g = (g_final if final_norm else jnp.ones((d,), F32)).reshape(1, d).astype(F32)
    return pl.pallas_call(
        functools.partial(_ffn_down_body, n_chunks=kdim // tk, final_norm=final_norm),
        grid=(m // tm,),
        in_specs=[
            pl.BlockSpec((tm, kdim), lambda i: (i, 0)),
            pl.BlockSpec(memory_space=pl.ANY),
            pl.BlockSpec(memory_space=pl.ANY),
            pl.BlockSpec((1, d), lambda i: (0, 0)),
        ],
        out_specs=pl.BlockSpec((tm, d), lambda i: (i, 0)),
        out_shape=jax.ShapeDtypeStruct((m, d), F32),
        scratch_shapes=[pltpu.VMEM((kdim, d), BF16), pltpu.SemaphoreType.DMA((kdim // tk + 1,))],
        compiler_params=_params("arbitrary"),
        name="ffn_down",
    )(a, w, x, g)


def _layer(x, mem, g_mix, w_in, b_gate, w_proj_sb, w_proj_dsa, w_out, rel_bias,
           g_cross, g_mem, w_cq, w_ckv, w_co, g_ffn, w_up, conv_w, conv_b, w_down, g_final, batch, seq):
    d = x.shape[1]
    width = N_HEADS * HEAD_DIM
    idx_w = IDX_HEADS * IDX_DIM
    o_qi = 6 * width
    o_ki = o_qi + idx_w
    o_wi = o_ki + IDX_DIM
    o_g = o_wi + IDX_HEADS

    zeros = jnp.zeros((d, LANES - IDX_DIM), F32)
    w_small = jnp.concatenate([
        w_in[:, o_ki:o_wi], zeros, zeros, w_in[:, o_ki:o_wi],
        jnp.pad(w_in[:, o_wi:o_g], ((0, 0), (0, LANES - IDX_HEADS)))], axis=1).astype(BF16)
    h, small = _rmsnorm_proj(x, g_mix, w_small, name="mixer_norm_index_proj", proj_dtype=F32,
                             keep_rows=True)
    w_in_t = w_in.T
    tq = ATT_BLOCK
    nq = seq // tq
    q_sb, k_sb, v_sb, q_ds, k_ds, v_ds, q_ix = (g * width for g in range(o_ki // width))
    gate_rows = tuple(o_g + g * width for g in range(2 * d // width))
    n_gate = len(gate_rows)
    main = _matmul_ws(h, w_in_t, name="in_proj_main",
                      first_rows=gate_rows + (q_sb, k_sb, q_ds, k_ds, q_ix),
                      out_dtype=BF16, tm=PROJ_ROWS, tn=width)
    v_t = _matmul_ws_t(h, w_in_t, name="in_proj_values", first_rows=(v_sb, v_ds), tn=width, tq=tq,
                       tm=PROJ_ROWS)
    v_t = v_t.reshape(2, batch, nq, width, tq)

    o_sb, w_proj_sb, w_proj_dsa, w_out, w_cq, w_co = _sb_attention(
        main, v_t, batch, seq, q_col=n_gate, k_col=n_gate + 1, v_branch=0, tq=tq,
        round_weights=(w_proj_sb, w_proj_dsa, w_out, w_cq, w_co))
    bias = _near_bias(rel_bias, tq)
    o_ds = _dsa_attention(main, v_t, small, bias, batch, seq, qd_col=n_gate + 2, kd_col=n_gate + 3,
                          qi_col=n_gate + 4, v_branch=1, tq=tq)

    kv = _rmsnorm_proj(mem, g_mem, w_ckv, name="mem_norm_kv_proj", proj_dtype=BF16, keep_rows=False)
    x, h_ffn = _merge_cross(o_sb, o_ds, w_proj_sb, w_proj_dsa, w_out, main, 0, b_gate, x, kv, g_cross,
                            w_cq, w_co, g_ffn, batch, seq, tm=MERGE_ROWS)

    act, w_down = _ffn_up_gate(h_ffn, w_up, conv_w, conv_b, w_down, seq)
    return _ffn_down(act, w_down, x, g_final)


def kernel(x, mem, g_mix, w_in, b_gate, w_proj_sb, w_proj_dsa, w_out, rel_bias, g_cross, g_mem,
           w_cq, w_ckv, w_co, g_ffn, w_up, conv_w, conv_b, w_down, g_final):
    batch, seq, d = x.shape
    h = x.reshape(batch * seq, d)
    mem2 = mem.reshape(batch * mem.shape[1], d)
    depth = g_mix.shape[0]
    for l in range(depth):
        h = _layer(h, mem2, g_mix[l], w_in[l], b_gate[l], w_proj_sb[l], w_proj_dsa[l], w_out[l],
                   rel_bias, g_cross[l], g_mem[l], w_cq[l], w_ckv[l], w_co[l], g_ffn[l], w_up[l],
                   conv_w[l], conv_b[l], w_down[l], g_final if l == depth - 1 else None, batch, seq)
    return h.reshape(batch, seq, d)
```

```python
import functools

import jax
import jax.numpy as jnp
from jax import lax
from jax.experimental import pallas as pl
from jax.experimental.pallas import tpu as pltpu

F32, BF16, I32 = jnp.float32, jnp.bfloat16, jnp.int32

EPS = 1e-6
HEAD_DIM = 128
N_HEADS = 8
IDX_HEADS = 16
IDX_DIM = 64
CHUNK = 64
TOPK_MAX = 256
N_BUCKETS = 32
MAX_DISTANCE = 128
MEM_HEADS = 4
CONV_WIDTH = 3

LANES = 128
SUBLANES = 8
VMEM_LIMIT_BYTES = 62 * 1024 * 1024
NEG_BIG = -1e30
EXP2_UNDERFLOW = -151.0
LOG2E = 1.4426950408889634
KEY_BITS = 32
SIGN_BIT = -(2 ** 31)

PROJ_ROWS = 2048
ROW_TILE = 1024
FFN_DOWN_ROWS = 256
NORM_ROWS = 1024
MERGE_ROWS = 512
FFN_COL_TILE = 512
FFN_K_TILE = 1024
ATT_BLOCK = KEY_BITS * SUBLANES
RADIX_GROUPS = 8

_NT = (((1,), (1,)), ((), ()))


def _params(*sem):
    return pltpu.CompilerParams(dimension_semantics=sem, vmem_limit_bytes=VMEM_LIMIT_BYTES)


def _rms(x, g):
    inv = lax.rsqrt(jnp.mean(x * x, axis=-1, keepdims=True) + EPS)
    return x * inv * g


def _rounding_riders(weights, n_steps, step_index):
    in_specs, out_specs, out_shape = [], [], []
    for w in weights:
        rows, cols = w.shape
        assert rows % n_steps == 0 and (rows // n_steps) % (2 * SUBLANES) == 0
        for specs in (in_specs, out_specs):
            specs.append(pl.BlockSpec((rows // n_steps, cols), lambda *ids: (step_index(*ids), 0)))
        out_shape.append(jax.ShapeDtypeStruct(w.shape, BF16))
    return in_specs, out_specs, out_shape


def _round_chunks(src_refs, dst_refs):
    for src, dst in zip(src_refs, dst_refs):
        dst[...] = src[...].astype(BF16)


def _mm_ws_body(a_ref, wt_ref, o_ref, wb_ref):
    @pl.when(pl.program_id(1) == 0)
    def _():
        wb_ref[...] = wt_ref[...].astype(BF16)

    o_ref[...] = lax.dot_general(a_ref[...], wb_ref[...], _NT,
                                 preferred_element_type=F32).astype(o_ref.dtype)


def _row_window(first_rows, tn, k):
    assert all(r % SUBLANES == 0 for r in first_rows)

    def index_map(j, i):
        row = jnp.int32(first_rows[0])
        for step, first in enumerate(first_rows[1:], start=1):
            row = jnp.where(j >= step, first, row)
        return pl.multiple_of(row, SUBLANES), 0

    return pl.BlockSpec((pl.Element(tn), pl.Element(k)), index_map)


def _matmul_ws(a, w_t, *, name, first_rows, out_dtype, tm, tn):
    m, k = a.shape
    tm = min(tm, m)
    assert m % tm == 0 and max(first_rows) + tn <= w_t.shape[0] and w_t.shape[1] == k
    return pl.pallas_call(
        _mm_ws_body,
        grid=(len(first_rows), m // tm),
        in_specs=[pl.BlockSpec((tm, k), lambda j, i: (i, 0)), _row_window(first_rows, tn, k)],
        out_specs=pl.BlockSpec((tm, tn), lambda j, i: (i, j)),
        out_shape=jax.ShapeDtypeStruct((m, len(first_rows) * tn), out_dtype),
        scratch_shapes=[pltpu.VMEM((tn, k), BF16)],
        compiler_params=_params("parallel", "arbitrary"),
        name=name,
    )(a, w_t)


def _mm_ws_t_body(wt_ref, a_ref, o_ref, wb_ref):
    @pl.when(pl.program_id(1) == 0)
    def _():
        wb_ref[...] = wt_ref[...].astype(BF16)

    res = lax.dot_general(wb_ref[...], a_ref[...], _NT, preferred_element_type=F32)
    n_blocks, _, tq = o_ref.shape
    for blk in range(n_blocks):
        o_ref[blk] = res[:, blk * tq:(blk + 1) * tq].astype(o_ref.dtype)


def _matmul_ws_t(a, w_t, *, name, first_rows, tn, tq, tm):
    m, k = a.shape
    tm = min(tm, m)
    assert m % tm == 0 and tm % tq == 0 and max(first_rows) + tn <= w_t.shape[0] and w_t.shape[1] == k
    per_step = tm // tq
    return pl.pallas_call(
        _mm_ws_t_body,
        grid=(len(first_rows), m // tm),
        in_specs=[_row_window(first_rows, tn, k), pl.BlockSpec((tm, k), lambda j, i: (i, 0))],
        out_specs=pl.BlockSpec((None, per_step, tn, tq), lambda j, i: (j, i, 0, 0)),
        out_shape=jax.ShapeDtypeStruct((len(first_rows), m // tq, tn, tq), BF16),
        scratch_shapes=[pltpu.VMEM((tn, k), BF16)],
        compiler_params=_params("parallel", "arbitrary"),
        name=name,
    )(w_t, a)


def _norm_proj_body(x_ref, g_ref, w_ref, *out_refs):
    h = _rms(x_ref[...], g_ref[...]).astype(BF16)
    p_ref = out_refs[-1]
    p_ref[...] = jnp.dot(h, w_ref[...].astype(BF16), preferred_element_type=F32).astype(p_ref.dtype)
    if len(out_refs) == 2:
        out_refs[0][...] = h


def _rmsnorm_proj(x, g, w, *, name, proj_dtype, keep_rows, tm=NORM_ROWS):
    m, d = x.shape
    n = w.shape[1]
    tm = min(tm, m)
    assert m % tm == 0
    out_specs = [pl.BlockSpec((tm, n), lambda i: (i, 0))]
    out_shape = [jax.ShapeDtypeStruct((m, n), proj_dtype)]
    if keep_rows:
        out_specs.insert(0, pl.BlockSpec((tm, d), lambda i: (i, 0)))
        out_shape.insert(0, jax.ShapeDtypeStruct((m, d), BF16))
    out = pl.pallas_call(
        _norm_proj_body,
        grid=(m // tm,),
        in_specs=[pl.BlockSpec((tm, d), lambda i: (i, 0)), pl.BlockSpec((1, d), lambda i: (0, 0)),
                  pl.BlockSpec((d, n), lambda i: (0, 0))],
        out_specs=out_specs,
        out_shape=out_shape,
        compiler_params=_params("parallel"),
        name=name,
    )(x, g.reshape(1, d).astype(F32), w)
    return out if keep_rows else out[0]


def _sb_body(q_ref, k_ref, vt_ref, tri_ref, *refs, tq, n_riders):
    o_ref, qs_ref, z_ref, lb_ref, wb_ref, acc_ref = refs[n_riders], *refs[2 * n_riders + 1:]
    _round_chunks(refs[:n_riders], refs[n_riders + 1:2 * n_riders + 1])
    i = pl.program_id(1)
    shape = (tq, tq)
    before = lax.broadcasted_iota(I32, shape, 0) < lax.broadcasted_iota(I32, shape, 1)
    heads = [slice(h * HEAD_DIM, (h + 1) * HEAD_DIM) for h in range(N_HEADS)]
    qs_ref[...] = (q_ref[...].astype(F32) * (HEAD_DIM ** -0.5 * LOG2E)).astype(BF16)
    acc_ref[...] = jnp.zeros(acc_ref.shape, F32)

    def logits(j, diagonal, slot):
        start = pl.multiple_of(j * tq, tq)
        for h, hs in enumerate(heads):
            z = lax.dot_general(k_ref[pl.ds(start, tq), hs], qs_ref[:, hs], _NT,
                                preferred_element_type=F32)
            neg_abs = lax.bitcast_convert_type(lax.bitcast_convert_type(z, I32) | SIGN_BIT, F32)
            log_lose = jnp.maximum(z, 0.0) + jnp.log2(1.0 + jnp.exp2(neg_abs))
            if diagonal:
                log_lose = jnp.where(before, log_lose, 0.0)
            z_ref[slot, h] = z
            lb_ref[slot, h] = log_lose.astype(BF16)

    def weights(carry, diagonal, slot):
        new_carry = []
        for h in range(N_HEADS):
            c = carry[h:h + 1, :] - jnp.dot(tri_ref[...], lb_ref[slot, h], preferred_element_type=F32)
            w = jnp.exp2(z_ref[slot, h] + c)
            if diagonal:
                w = jnp.where(before, w, 0.0)
            wb_ref[slot, h] = w.astype(BF16)
            new_carry.append(c[0:1, :])
        return jnp.concatenate(new_carry, axis=0)

    def values(j, slot):
        for h, hs in enumerate(heads):
            acc_ref[h] += jnp.dot(vt_ref[j, hs, :], wb_ref[slot, h], preferred_element_type=F32)

    def tile(j, carry, diagonal):
        logits(j, diagonal, 0)
        carry = weights(carry, diagonal, 0)
        values(j, 0)
        return carry

    no_carry = jnp.zeros((N_HEADS, tq), F32)

    def diagonal_and_previous():
        logits(i, True, 0)
        logits(i - 1, False, 1)
        carry = weights(weights(no_carry, True, 0), False, 1)
        values(i, 0)
        values(i - 1, 1)
        return carry

    carry = lax.cond(i >= 1, diagonal_and_previous, lambda: tile(i, no_carry, True))

    def cond(state):
        j, live, _ = state
        return jnp.logical_and(j >= 0, live > EXP2_UNDERFLOW)

    def body(state):
        j, _, carry = state
        carry = tile(j, carry, False)
        return j - 1, jnp.max(carry), carry

    lax.while_loop(cond, body, (i - 2, jnp.max(carry), carry))
    for h, hs in enumerate(heads):
        o_ref[:, hs] = acc_ref[h].T.astype(o_ref.dtype)


def _sb_attention(qk, v_t, batch, seq, *, q_col, k_col, v_branch, tq, round_weights=()):
    nq = seq // tq
    width = N_HEADS * HEAD_DIM
    tri = (jnp.arange(tq)[None, :] >= jnp.arange(tq)[:, None]).astype(BF16)
    resident = pl.Buffered(1)
    rider_in, rider_out, rider_shape = _rounding_riders(round_weights, batch * nq,
                                                        lambda b, i: b * nq + i)
    return pl.pallas_call(
        functools.partial(_sb_body, tq=tq, n_riders=len(round_weights)),
        grid=(batch, nq),
        in_specs=[
            pl.BlockSpec((tq, width), lambda b, i: (b * nq + i, q_col)),
            pl.BlockSpec((seq, width), lambda b, i: (b, k_col), pipeline_mode=resident),
            pl.BlockSpec((None, None, nq, width, tq), lambda b, i: (v_branch, b, 0, 0, 0),
                         pipeline_mode=resident),
            pl.BlockSpec((tq, tq), lambda b, i: (0, 0), pipeline_mode=resident),
        ] + rider_in,
        out_specs=[pl.BlockSpec((tq, width), lambda b, i: (b * nq + i, 0))] + rider_out,
        out_shape=[jax.ShapeDtypeStruct((batch * seq, width), BF16)] + rider_shape,
        scratch_shapes=[
            pltpu.VMEM((tq, width), BF16),
            pltpu.VMEM((2, N_HEADS, tq, tq), F32),
            pltpu.VMEM((2, N_HEADS, tq, tq), BF16),
            pltpu.VMEM((2, N_HEADS, tq, tq), BF16),
            pltpu.VMEM((N_HEADS, HEAD_DIM, tq), F32),
        ],
        compiler_params=_params("parallel", "arbitrary"),
        name="sb_attention",
    )(qk, qk, v_t, tri, *round_weights)


def _bucket_thresholds():
    nb = N_BUCKETS // 2
    max_exact = nb // 2
    span = nb - max_exact
    out = []
    for k in range(1, span):
        n = max_exact
        while n ** span * max_exact ** k < MAX_DISTANCE ** k * max_exact ** span:
            n += 1
        out.append(n)
    return max_exact, out


def _bias_body(rb_ref, o_ref, *, tq):
    nb = N_BUCKETS // 2
    max_exact, steps = _bucket_thresholds()
    shape = (2 * tq, tq)
    rel = lax.broadcasted_iota(I32, shape, 0) - lax.broadcasted_iota(I32, shape, 1) - tq
    n = jnp.abs(rel)
    large = jnp.full(shape, max_exact, I32)
    for t in steps:
        large = large + (n >= t).astype(I32)
    bucket = jnp.where(rel > 0, nb, 0) + jnp.where(n < max_exact, n, large)
    for h in range(N_HEADS):
        val = jnp.zeros(shape, F32)
        for b in range(N_BUCKETS):
            val = jnp.where(bucket == b, rb_ref[b, h], val)
        o_ref[h] = (val - rb_ref[nb - 1, h]) * LOG2E


def _near_bias(rel_bias, tq):
    return pl.pallas_call(
        functools.partial(_bias_body, tq=tq),
        in_specs=[pl.BlockSpec(memory_space=pltpu.SMEM)],
        out_specs=pl.BlockSpec(memory_space=pltpu.VMEM),
        out_shape=jax.ShapeDtypeStruct((N_HEADS, 2 * tq, tq), F32),
        compiler_params=pltpu.CompilerParams(vmem_limit_bytes=VMEM_LIMIT_BYTES),
        name="dsa_near_bias",
    )(rel_bias.astype(F32))


def _order_key(x):
    bits = lax.bitcast_convert_type(x, I32)
    return bits ^ ((bits >> 31) | SIGN_BIT)


def _order_key_to_float(key):
    return lax.bitcast_convert_type(key ^ ((~key >> 31) | SIGN_BIT), F32)


def _bit_transpose32(words):
    a = list(words)
    j, m = 16, 0x0000FFFF
    while j:
        mask = jnp.int32(m - (1 << 32) if m >= 1 << 31 else m)
        k = 0
        while k < 32:
            t = (lax.shift_right_logical(a[k], jnp.int32(j)) ^ a[k + j]) & mask
            a[k] = a[k] ^ lax.shift_left(t, jnp.int32(j))
            a[k + j] = a[k + j] ^ t
            k = (k + j + 1) & ~j
        j >>= 1
        m = (m ^ (m << j)) & 0xFFFFFFFF
    return a


def _dsa_body(qd_ref, qi_ref, wq_ref, kd_ref, vt_ref, ki_ref, bias_ref, tri_ref, o_ref,
              sc_ref, plane_ref, qs_ref, m_ref, l_ref, acc_ref, lg_ref, bmax_ref, *, tq, top):
    i = pl.program_id(1)
    shape = (tq, tq)
    key_row = lax.broadcasted_iota(I32, shape, 0)
    qry_col = lax.broadcasted_iota(I32, shape, 1)
    visible = key_row // CHUNK <= qry_col // CHUNK

    w_t = (wq_ref[...] * (IDX_DIM ** -0.5 * IDX_HEADS ** -0.5)).T

    heads_per_vreg = LANES // IDX_DIM

    def score_tile(j):
        start = pl.multiple_of(j * tq, tq)
        ki = [ki_ref[pl.ds(start, tq), c * LANES:(c + 1) * LANES].astype(BF16)
              for c in range(heads_per_vreg)]
        s = jnp.zeros(shape, F32)
        for h in range(IDX_HEADS):
            g, c = divmod(h, heads_per_vreg)
            d = lax.dot_general(ki[c], qi_ref[:, g * LANES:(g + 1) * LANES], _NT,
                                preferred_element_type=F32)
            s = s + w_t[h:h + 1, :] * jnp.maximum(d, 0.0)
        sc_ref[j] = s

    def key_planes(j):
        ukey = _order_key(sc_ref[j])
        planes = _bit_transpose32([ukey[g * SUBLANES:(g + 1) * SUBLANES, :] for g in range(KEY_BITS)])
        for b in range(KEY_BITS):
            plane_ref[b, pl.ds(pl.multiple_of(j * SUBLANES, SUBLANES), SUBLANES), :] = planes[b]

    @pl.when(i == 0)
    def _():
        plane_ref[...] = jnp.zeros(plane_ref.shape, I32)

    def score_steps(first, count):
        for j in range(count):
            key_planes(first + j)
            score_tile(first + j + 1)

    def score_quad(p, carry):
        score_steps(4 * p, 4)
        return carry

    score_tile(0)
    lax.fori_loop(0, i // 4, score_quad, 0)

    @pl.when(i % 4 >= 2)
    def _():
        score_steps(4 * (i // 4), 2)

    @pl.when(i % 2 == 1)
    def _():
        score_steps(i - 1, 1)

    key_planes(i)
    sc_ref[i] = jnp.where(visible, sc_ref[i], -jnp.inf)

    def popcount_rows(words):
        return jnp.sum(lax.population_count(words), axis=0, keepdims=True)

    def radix_select(n_rows):
        block_of_row = lax.broadcasted_iota(I32, (n_rows, tq), 0) // SUBLANES
        qry_of_col = lax.broadcasted_iota(I32, (n_rows, tq), 1)
        n_bits = (qry_of_col // CHUNK + 1) * (CHUNK // SUBLANES)
        diag_bits = jnp.where(n_bits >= KEY_BITS, -1, lax.shift_left(jnp.int32(1), n_bits) - 1)
        cand0 = jnp.where(block_of_row < i, -1, jnp.where(block_of_row == i, diag_bits, 0))

        def bit_step(t, state):
            cand, n_above, thr_bits = state
            b = KEY_BITS - 1 - t
            ones = cand & plane_ref[b, :n_rows]
            n_ones = popcount_rows(ones)
            take = n_above + n_ones >= top
            cand = jnp.where(take, ones, cand ^ ones)
            n_above = jnp.where(take, n_above, n_above + n_ones)
            thr_bits = thr_bits | jnp.where(take, lax.shift_left(jnp.int32(1), b), 0)
            return cand, n_above, thr_bits

        zero = jnp.zeros((1, tq), I32)
        cand, n_above, thr_bits = lax.fori_loop(0, KEY_BITS, bit_step, (cand0, zero, zero), unroll=4)
        return n_above, thr_bits, popcount_rows(cand)

    n_groups = min(RADIX_GROUPS, plane_ref.shape[1] // SUBLANES)

    def select_from(group):
        if group == n_groups - 1:
            return radix_select(plane_ref.shape[1])
        rows = (group + 1) * (plane_ref.shape[1] // n_groups)
        return lax.cond(i * SUBLANES < rows, functools.partial(radix_select, rows),
                        functools.partial(select_from, group + 1))

    n_above, thr_bits, n_equal = select_from(0)
    qry = lax.broadcasted_iota(I32, (1, tq), 1)
    n_visible = i * tq + (qry // CHUNK + 1) * CHUNK
    wanted = n_visible > top
    thr = jnp.where(wanted, _order_key_to_float(thr_bits), jnp.finfo(F32).min)
    tied = jnp.logical_and(wanted, n_above + n_equal > top)
    c_hi = n_above

    ones = jnp.ones((2 * SUBLANES, tq), BF16)

    def plain_mask(j, carry):
        sc_ref[j] = jnp.where(sc_ref[j] >= thr, 0.0, NEG_BIG)
        return carry

    def tie_mask(j, seen):
        s = sc_ref[j]
        equal = s == thr
        rank = jnp.dot(tri_ref[...], equal.astype(BF16), preferred_element_type=F32) + seen
        quota = jnp.where(tied, (top - c_hi).astype(F32), jnp.inf)
        keep_equal = jnp.where(rank < quota, 0.0, NEG_BIG)
        sc_ref[j] = jnp.where(s > thr, 0.0, jnp.where(equal, keep_equal, NEG_BIG))
        return seen + jnp.sum(equal.astype(F32), axis=0, keepdims=True)

    def with_ties():
        lax.fori_loop(0, i + 1, tie_mask, jnp.zeros((1, tq), F32))
        return jnp.int32(0)

    def without_ties():
        return lax.fori_loop(0, i + 1, plain_mask, jnp.int32(0))

    lax.cond(jnp.max(tied.astype(I32)) > 0, with_ties, without_ties)

    qs_ref[...] = (qd_ref[...].astype(F32) * (HEAD_DIM ** -0.5 * LOG2E)).astype(BF16)
    m_ref[...] = jnp.full(m_ref.shape, NEG_BIG, F32)
    l_ref[...] = jnp.zeros(l_ref.shape, F32)
    acc_ref[...] = jnp.zeros(acc_ref.shape, F32)

    heads = [slice(h * HEAD_DIM, (h + 1) * HEAD_DIM) for h in range(N_HEADS)]

    far, prev, diag = None, 0, 1

    def logits(u, near, slot):
        j = i - u
        start = pl.multiple_of(j * tq, tq)
        mask = sc_ref[j]
        block_max = []
        for h, hs in enumerate(heads):
            lg = lax.dot_general(kd_ref[pl.ds(start, tq), hs], qs_ref[:, hs], _NT,
                                 preferred_element_type=F32)
            if near is not None:
                lg = lg + bias_ref[h, near * tq:(near + 1) * tq, :]
            lg = lg + mask
            lg_ref[slot, h] = lg
            block_max.append(jnp.max(lg, axis=0, keepdims=True))
        bmax_ref[slot] = jnp.concatenate(block_max, axis=0)

    def values(u, slot):
        j = i - u
        m_old = m_ref[...]
        m_new = jnp.maximum(m_old, bmax_ref[slot])
        alpha = jnp.exp2(m_old - m_new)
        m_ref[...] = m_new
        denom = []
        for h, hs in enumerate(heads):
            p = jnp.exp2(lg_ref[slot, h] - m_new[h:h + 1, :]).astype(BF16)
            v_ext = jnp.concatenate([vt_ref[j, hs, :], ones], axis=0)
            pv = jnp.dot(v_ext, p, preferred_element_type=F32)
            acc_ref[h] = alpha[h:h + 1, :] * acc_ref[h] + pv[:HEAD_DIM]
            denom.append(pv[HEAD_DIM:HEAD_DIM + 1])
        l_ref[...] = alpha * l_ref[...] + jnp.concatenate(denom, axis=0)

    def even_step(u, near_a, near_b):
        logits(u - 1, near_a, 1)
        values(u, 0)
        logits(u - 2, near_b, 0)
        values(u - 1, 1)

    @pl.when(i == 0)
    def _():
        logits(0, diag, 0)
        values(0, 0)

    @pl.when(i == 1)
    def _():
        logits(1, prev, 1)

    @pl.when(jnp.logical_and(i >= 2, i % 2 == 1))
    def _():
        logits(i, far, 1)
        logits(i - 1, far, 0)
        values(i, 1)

    @pl.when(jnp.logical_and(i >= 2, i % 2 == 0))
    def _():
        logits(i, far, 0)

    n_far_pairs = i // 2 - 1

    def far_quad(k, carry):
        u = 2 * (i // 2 - 2 * k)
        even_step(u, far, far)
        even_step(u - 2, far, far)
        return carry

    lax.fori_loop(0, n_far_pairs // 2, far_quad, 0)

    @pl.when(jnp.logical_and(n_far_pairs > 0, n_far_pairs % 2 == 1))
    def _():
        even_step(4, far, far)

    @pl.when(i >= 2)
    def _():
        even_step(2, prev, diag)

    @pl.when(i == 1)
    def _():
        logits(0, diag, 0)
        values(1, 1)

    @pl.when(i >= 1)
    def _():
        values(0, 0)

    for h in range(N_HEADS):
        o = acc_ref[h] / l_ref[h:h + 1, :]
        o_ref[:, h * HEAD_DIM:(h + 1) * HEAD_DIM] = o.T.astype(o_ref.dtype)


def _dsa_attention(main, v_t, small, bias, batch, seq, *, qd_col, kd_col, qi_col, v_branch, tq):
    nq = seq // tq
    width = N_HEADS * HEAD_DIM
    assert IDX_HEADS * IDX_DIM == width and tq == KEY_BITS * SUBLANES and tq % CHUNK == 0
    assert nq % min(RADIX_GROUPS, nq) == 0
    key_copies = LANES // IDX_DIM
    top = min(TOPK_MAX, seq // 4)
    tri = (jnp.arange(tq)[None, :] < jnp.arange(tq)[:, None]).astype(BF16)
    resident = pl.Buffered(1)
    return pl.pallas_call(
        functools.partial(_dsa_body, tq=tq, top=top),
        grid=(batch, nq),
        in_specs=[
            pl.BlockSpec((tq, width), lambda b, i: (b * nq + i, qd_col)),
            pl.BlockSpec((tq, IDX_HEADS * IDX_DIM), lambda b, i: (b * nq + i, qi_col)),
            pl.BlockSpec((tq, LANES), lambda b, i: (b * nq + i, key_copies)),
            pl.BlockSpec((seq, width), lambda b, i: (b, kd_col), pipeline_mode=resident),
            pl.BlockSpec((None, None, nq, width, tq), lambda b, i: (v_branch, b, 0, 0, 0),
                         pipeline_mode=resident),
            pl.BlockSpec((seq, key_copies * LANES), lambda b, i: (b, 0), pipeline_mode=resident),
            pl.BlockSpec((N_HEADS, 2 * tq, tq), lambda b, i: (0, 0, 0), pipeline_mode=resident),
            pl.BlockSpec((tq, tq), lambda b, i: (0, 0), pipeline_mode=resident),
        ],
        out_specs=pl.BlockSpec((tq, width), lambda b, i: (b * nq + i, 0)),
        out_shape=jax.ShapeDtypeStruct((batch * seq, width), BF16),
        scratch_shapes=[
            pltpu.VMEM((nq, tq, tq), F32),
            pltpu.VMEM((KEY_BITS, nq * SUBLANES, tq), I32),
            pltpu.VMEM((tq, width), BF16),
            pltpu.VMEM((N_HEADS, tq), F32),
            pltpu.VMEM((N_HEADS, tq), F32),
            pltpu.VMEM((N_HEADS, HEAD_DIM, tq), F32),
            pltpu.VMEM((2, N_HEADS, tq, tq), F32),
            pltpu.VMEM((2, N_HEADS, tq), F32),
        ],
        compiler_params=_params("parallel", "arbitrary"),
        name="dsa_attention",
    )(main, main, small, main, v_t, small, bias, tri)


def _merge_cross_body(osb_ref, ods_ref, wsb_hbm, wds_hbm, gsb_ref, gds_ref, bsb_ref, bds_ref, wo_hbm,
                      x_ref, gc_ref, wq_hbm, km_ref, vm_ref, wco_hbm, gn_ref, o_ref, hn_ref,
                      wsb_ref, wds_ref, wo_ref, wq_ref, wco_ref, sem):
    first = jnp.logical_and(pl.program_id(0) == 0, pl.program_id(1) == 0)
    pairs = ((wsb_hbm, wsb_ref), (wds_hbm, wds_ref), (wo_hbm, wo_ref), (wq_hbm, wq_ref),
             (wco_hbm, wco_ref))
    copies = [pltpu.make_async_copy(src, dst, sem.at[n]) for n, (src, dst) in enumerate(pairs)]

    def arrive(*which):
        @pl.when(first)
        def _():
            for n in which:
                copies[n].wait()

    @pl.when(first)
    def _():
        for c in copies:
            c.start()

    arrive(0, 1)
    p_sb = jnp.dot(osb_ref[...], wsb_ref[...], preferred_element_type=F32)
    p_ds = jnp.dot(ods_ref[...], wds_ref[...], preferred_element_type=F32)
    g_sb = jax.nn.sigmoid(gsb_ref[...].astype(F32) + bsb_ref[...])
    g_ds = jax.nn.sigmoid(gds_ref[...].astype(F32) + bds_ref[...])
    merged = (g_sb * p_sb + g_ds * p_ds).astype(BF16)
    arrive(2, 3, 4)
    x1 = x_ref[...] + jnp.dot(merged, wo_ref[...], preferred_element_type=F32)

    h = _rms(x1, gc_ref[...]).astype(BF16)
    q = jnp.dot(h, wq_ref[...], preferred_element_type=F32) * HEAD_DIM ** -0.5
    q = q.astype(BF16)
    outs = []
    for hh in range(MEM_HEADS):
        hs = slice(hh * HEAD_DIM, (hh + 1) * HEAD_DIM)
        lg = lax.dot_general(q[:, hs], km_ref[:, hs], _NT, preferred_element_type=F32)
        p = jnp.exp(lg - jnp.max(lg, axis=1, keepdims=True))
        o = jnp.dot(p.astype(BF16), vm_ref[:, hs], preferred_element_type=F32)
        outs.append((o / jnp.sum(p, axis=1, keepdims=True)).astype(BF16))
    o = jnp.concatenate(outs, axis=1)
    x2 = x1 + jnp.dot(o, wco_ref[...], preferred_element_type=F32)
    o_ref[...] = x2
    hn_ref[...] = _rms(x2, gn_ref[...]).astype(hn_ref.dtype)


def _merge_cross(o_sb, o_ds, w_sb, w_ds, w_out, proj, gate_offset, b_gate, x,
                 kv, g_cross, w_cq, w_co, g_next, batch, seq, *, tm):
    m, k = o_sb.shape
    d = w_sb.shape[1]
    n_mem = kv.shape[0] // batch
    width = MEM_HEADS * HEAD_DIM
    tm = min(tm, seq)
    assert gate_offset % d == 0 and seq % tm == 0
    g = gate_offset // d
    nt = seq // tm
    b_gate = b_gate.reshape(1, 2 * d).astype(F32)
    resident = pl.Buffered(1)
    in_hbm = pl.BlockSpec(memory_space=pl.ANY)

    def rows(width_, col=0):
        return pl.BlockSpec((tm, width_), lambda b, i: (b * nt + i, col))

    def whole(shape, *block):
        return pl.BlockSpec(shape, lambda b, i: block or (0,) * len(shape), pipeline_mode=resident)

    return pl.pallas_call(
        _merge_cross_body,
        grid=(batch, nt),
        in_specs=[
            rows(k), rows(k), in_hbm, in_hbm,
            rows(d, g), rows(d, g + 1), whole((1, d)), whole((1, d), 0, 1),
            in_hbm, rows(d),
            whole((1, d)), in_hbm,
            pl.BlockSpec((n_mem, width), lambda b, i: (b, 0)),
            pl.BlockSpec((n_mem, width), lambda b, i: (b, 1)),
            in_hbm, whole((1, d)),
        ],
        out_specs=[rows(d), rows(d)],
        out_shape=[jax.ShapeDtypeStruct((m, d), F32), jax.ShapeDtypeStruct((m, d), BF16)],
        scratch_shapes=[pltpu.VMEM(w.shape, BF16) for w in (w_sb, w_ds, w_out, w_cq, w_co)]
        + [pltpu.SemaphoreType.DMA((5,))],
        compiler_params=_params("arbitrary", "arbitrary"),
        name="merge_out_cross",
    )(o_sb, o_ds, w_sb, w_ds, proj, proj, b_gate, b_gate, w_out, x,
      g_cross.reshape(1, d).astype(F32), w_cq, kv, kv, w_co, g_next.reshape(1, d).astype(F32))


def _delayed(u, tail, shift):
    rolled = pltpu.roll(u, shift, axis=0)
    row = lax.broadcasted_iota(I32, tail.shape, 0)
    head = jnp.where(row < shift, pltpu.roll(tail, shift, axis=0), rolled[:SUBLANES])
    return jnp.concatenate([head, rolled[SUBLANES:]], axis=0)


def _ffn_up_body(h_ref, wa_ref, wv_ref, cwa_ref, cwv_ref, cba_ref, cbv_ref, wd_ref, o_ref, wdb_ref,
                 wab_ref, wvb_ref, halo_ref, *, tiles_per_seq):
    i = pl.program_id(1)
    _round_chunks([wd_ref], [wdb_ref])

    @pl.when(i == 0)
    def _():
        wab_ref[...] = wa_ref[...].astype(BF16)
        wvb_ref[...] = wv_ref[...].astype(BF16)

    h = h_ref[...]
    tm = h.shape[0]
    sequence_start = i % tiles_per_seq == 0

    def conv(wb_ref, cw_ref, cb_ref, slot):
        u = jnp.dot(h, wb_ref[...], preferred_element_type=F32)
        tail = jnp.where(sequence_start, 0.0, halo_ref[slot])
        halo_ref[slot] = u[tm - SUBLANES:, :]
        c = cb_ref[...] + cw_ref[CONV_WIDTH - 1:CONV_WIDTH, :] * u
        for tap in range(CONV_WIDTH - 1):
            c = c + cw_ref[tap:tap + 1, :] * _delayed(u, tail, CONV_WIDTH - 1 - tap)
        return c

    a = conv(wab_ref, cwa_ref, cba_ref, 0)
    val = conv(wvb_ref, cwv_ref, cbv_ref, 1)
    o_ref[...] = (jax.nn.gelu(a) * val).astype(o_ref.dtype)


def _ffn_up_gate(h, w_up, conv_w, conv_b, w_down, seq, *, tm=ROW_TILE, tn=FFN_COL_TILE):
    m, d = h.shape
    two_ff = w_up.shape[1]
    d_ff = two_ff // 2
    tm, tn = min(tm, seq), min(tn, d_ff)
    assert seq % tm == 0 and d_ff % tn == 0 and tm >= SUBLANES >= CONV_WIDTH - 1
    nf, nt = d_ff // tn, m // tm
    conv_w = conv_w.astype(F32)
    conv_b = conv_b.reshape(1, two_ff).astype(F32)
    rider_in, rider_out, rider_shape = _rounding_riders([w_down], nf * nt, lambda j, i: j * nt + i)
    return pl.pallas_call(
        functools.partial(_ffn_up_body, tiles_per_seq=seq // tm),
        grid=(nf, nt),
        in_specs=[
            pl.BlockSpec((tm, d), lambda j, i: (i, 0)),
            pl.BlockSpec((d, tn), lambda j, i: (0, j)),
            pl.BlockSpec((d, tn), lambda j, i: (0, nf + j)),
            pl.BlockSpec((CONV_WIDTH, tn), lambda j, i: (0, j)),
            pl.BlockSpec((CONV_WIDTH, tn), lambda j, i: (0, nf + j)),
            pl.BlockSpec((1, tn), lambda j, i: (0, j)),
            pl.BlockSpec((1, tn), lambda j, i: (0, nf + j)),
        ] + rider_in,
        out_specs=[pl.BlockSpec((tm, tn), lambda j, i: (i, j))] + rider_out,
        out_shape=[jax.ShapeDtypeStruct((m, d_ff), BF16)] + rider_shape,
        scratch_shapes=[pltpu.VMEM((d, tn), BF16), pltpu.VMEM((d, tn), BF16),
                        pltpu.VMEM((2, SUBLANES, tn), F32)],
        compiler_params=_params("parallel", "arbitrary"),
        name="ffn_up_conv_gate",
    )(h, w_up, w_up, conv_w, conv_w, conv_b, conv_b, w_down)


def _ffn_down_body(a_ref, w_hbm, x_ref, g_ref, o_ref, w_ref, sem, *, n_chunks, final_norm):
    tk = w_ref.shape[0] // n_chunks

    def chunk_copy(c):
        rows = pl.ds(c * tk, tk)
        return pltpu.make_async_copy(w_hbm.at[rows], w_ref.at[rows], sem.at[c])

    def finish(y):
        o_ref[...] = _rms(y, g_ref[...]) if final_norm else y

    @pl.when(pl.program_id(0) == 0)
    def _():
        for c in range(n_chunks):
            chunk_copy(c).start()
        y = x_ref[...]
        for c in range(n_chunks):
            chunk_copy(c).wait()
            y = y + jnp.dot(a_ref[:, c * tk:(c + 1) * tk], w_ref[c * tk:(c + 1) * tk, :],
                            preferred_element_type=F32)
        finish(y)

    @pl.when(pl.program_id(0) > 0)
    def _():
        finish(x_ref[...] + jnp.dot(a_ref[...], w_ref[...], preferred_element_type=F32))


def _ffn_down(a, w, x, g_final, *, tm=FFN_DOWN_ROWS, tk=FFN_K_TILE):
    m, kdim = a.shape
    d = w.shape[1]
    tm, tk = min(tm, m), min(tk, kdim)
    assert m % tm == 0 and kdim % tk == 0
    final_norm = g_final is not None
    g = (g_final if final_norm else jnp.ones((d,), F32)).reshape(1, d).astype(F32)
    return pl.pallas_call(
        functools.partial(_ffn_down_body, n_chunks=kdim // tk, final_norm=final_norm),
        grid=(m // tm,),
        in_specs=[
            pl.BlockSpec((tm, kdim), lambda i: (i, 0)),
            pl.BlockSpec(memory_space=pl.ANY),
            pl.BlockSpec((tm, d), lambda i: (i, 0)),
            pl.BlockSpec((1, d), lambda i: (0, 0)),
        ],
        out_specs=pl.BlockSpec((tm, d), lambda i: (i, 0)),
        out_shape=jax.ShapeDtypeStruct((m, d), F32),
        scratch_shapes=[pltpu.VMEM((kdim, d), BF16), pltpu.SemaphoreType.DMA((kdim // tk,))],
        compiler_params=_params("arbitrary"),
        name="ffn_down",
    )(a, w, x, g)


def _layer(x, mem, g_mix, w_in, b_gate, w_proj_sb, w_proj_dsa, w_out, rel_bias,
           g_cross, g_mem, w_cq, w_ckv, w_co, g_ffn, w_up, conv_w, conv_b, w_down, g_final, batch, seq):
    d = x.shape[1]
    width = N_HEADS * HEAD_DIM
    idx_w = IDX_HEADS * IDX_DIM
    o_qi = 6 * width
    o_ki = o_qi + idx_w
    o_wi = o_ki + IDX_DIM
    o_g = o_wi + IDX_HEADS

    zeros = jnp.zeros((d, LANES - IDX_DIM), F32)
    w_small = jnp.concatenate([
        w_in[:, o_ki:o_wi], zeros, zeros, w_in[:, o_ki:o_wi],
        jnp.pad(w_in[:, o_wi:o_g], ((0, 0), (0, LANES - IDX_HEADS)))], axis=1).astype(BF16)
    h, small = _rmsnorm_proj(x, g_mix, w_small, name="mixer_norm_index_proj", proj_dtype=F32,
                             keep_rows=True)
    w_in_t = w_in.T
    tq = ATT_BLOCK
    nq = seq // tq
    q_sb, k_sb, v_sb, q_ds, k_ds, v_ds, q_ix = (g * width for g in range(o_ki // width))
    gate_rows = tuple(o_g + g * width for g in range(2 * d // width))
    n_gate = len(gate_rows)
    main = _matmul_ws(h, w_in_t, name="in_proj_main",
                      first_rows=gate_rows + (q_sb, k_sb, q_ds, k_ds, q_ix),
                      out_dtype=BF16, tm=PROJ_ROWS, tn=width)
    v_t = _matmul_ws_t(h, w_in_t, name="in_proj_values", first_rows=(v_sb, v_ds), tn=width, tq=tq,
                       tm=PROJ_ROWS)
    v_t = v_t.reshape(2, batch, nq, width, tq)

    o_sb, w_proj_sb, w_proj_dsa, w_out, w_cq, w_co = _sb_attention(
        main, v_t, batch, seq, q_col=n_gate, k_col=n_gate + 1, v_branch=0, tq=tq,
        round_weights=(w_proj_sb, w_proj_dsa, w_out, w_cq, w_co))
    bias = _near_bias(rel_bias, tq)
    o_ds = _dsa_attention(main, v_t, small, bias, batch, seq, qd_col=n_gate + 2, kd_col=n_gate + 3,
                          qi_col=n_gate + 4, v_branch=1, tq=tq)

    kv = _rmsnorm_proj(mem, g_mem, w_ckv, name="mem_norm_kv_proj", proj_dtype=BF16, keep_rows=False)
    x, h_ffn = _merge_cross(o_sb, o_ds, w_proj_sb, w_proj_dsa, w_out, main, 0, b_gate, x, kv, g_cross,
                            w_cq, w_co, g_ffn, batch, seq, tm=MERGE_ROWS)

    act, w_down = _ffn_up_gate(h_ffn, w_up, conv_w, conv_b, w_down, seq)
    return _ffn_down(act, w_down, x, g_final)


def kernel(x, mem, g_mix, w_in, b_gate, w_proj_sb, w_proj_dsa, w_out, rel_bias, g_cross, g_mem,
           w_cq, w_ckv, w_co, g_ffn, w_up, conv_w, conv_b, w_down, g_final):
    batch, seq, d = x.shape
    h = x.reshape(batch * seq, d)
    mem2 = mem.reshape(batch * mem.shape[1], d)
    depth = g_mix.shape[0]
    for l in range(depth):
        h = _layer(h, mem2, g_mix[l], w_in[l], b_gate[l], w_proj_sb[l], w_proj_dsa[l], w_out[l],
                   rel_bias, g_cross[l], g_mem[l], w_cq[l], w_ckv[l], w_co[l], g_ffn[l], w_up[l],
                   conv_w[l], conv_b[l], w_down[l], g_final if l == depth - 1 else None, batch, seq)
    return h.reshape(batch, seq, d)
```
